```python
import jax
import jax.numpy as jnp
from jax import lax
import numpy as np

D_MODEL = 1024
BATCH = 32
SEQ = 256
DEPTH = 2
DEC_BATCH = 2
DEC_SEQ = 1024
PAST_LEN = 512

GRID_W = 64
NORM_EPS = 1e-6

MLA_HEADS = 4
MLA_NOPE = 128
MLA_ROPE = 64
MLA_V = 128
Q_LORA = 384
KV_LORA = 256
ROPE_BASE = 10000.0
MLA_SCALE = (MLA_NOPE + MLA_ROPE) ** -0.5
Q_BLOCK = 128

GDN_HEADS = 4
GDN_DK = 64
GDN_DV = 64
GDN_CONV = 3
GDN_CHUNK = 64

RWKV_HEADS = 4
RWKV_N = 64
RWKV_W_LORA = 64
RWKV_A_LORA = 64
RWKV_G_LORA = 128
RWKV_GN_EPS = 64e-5

N_EXPERTS = 64
TOP_K = 8
N_GROUPS = 8
TOPK_GROUPS = 4
D_EXPERT = 256
D_SHARED = 256
ROUTE_SCALE = 2.5
MOE_BLOCK = 128

MLA_W = MLA_HEADS * MLA_V
GDN_QK = GDN_HEADS * GDN_DK
GDN_W = GDN_HEADS * GDN_DV
RWKV_W = RWKV_HEADS * RWKV_N
MIX_W = MLA_W + GDN_W + RWKV_W
GDN_CONV_CH = 2 * GDN_QK + GDN_W
P_MLA = Q_LORA + KV_LORA + MLA_ROPE
P_GDN = GDN_CONV_CH + GDN_W + 4 * GDN_HEADS
P_RWKV = 3 * RWKV_W + 2 * RWKV_W_LORA + 2 * RWKV_A_LORA + RWKV_G_LORA
P_IN = P_MLA + P_GDN + P_RWKV

kernel_name = 'hybrid_mla_gdn_rwkv7_moe_diffusion_step'


def rmsnorm(x, g):
    xf = x.astype(jnp.float32)
    y = xf * lax.rsqrt(jnp.mean(xf * xf, axis=-1, keepdims=True) + NORM_EPS)
    return (y * g.astype(jnp.float32)).astype(x.dtype)


def l2norm(x):
    xf = x.astype(jnp.float32)
    return xf * lax.rsqrt(jnp.sum(xf * xf, axis=-1, keepdims=True) + 1e-6)


def adaln(cvec, w, b):
    m = jnp.einsum('bd,de->be', jax.nn.silu(cvec), w) + b
    return jnp.split(m[:, None, :], 6, axis=-1)


def swiglu(x, w_gu, w_down):
    gt, up = jnp.split(x @ w_gu, 2, axis=-1)
    return (jax.nn.silu(gt) * up) @ w_down


def rope_2d_tables(n):
    rows = n // GRID_W
    row = jnp.repeat(jnp.arange(rows, dtype=jnp.float32), GRID_W)
    col = jnp.tile(jnp.arange(GRID_W, dtype=jnp.float32), rows)
    axis_dim = MLA_ROPE // 2
    inv = jnp.power(ROPE_BASE, -jnp.arange(0, axis_dim, 2, dtype=jnp.float32) / axis_dim)
    ang_r = row[:, None] * inv
    ang_c = col[:, None] * inv
    return (jnp.cos(ang_r), jnp.sin(ang_r), jnp.cos(ang_c), jnp.sin(ang_c))


def _rotate(x, cos, sin):
    h = x.shape[-1] // 2
    x1, x2 = x[..., :h], x[..., h:]
    return jnp.concatenate([x1 * cos - x2 * sin, x1 * sin + x2 * cos], axis=-1)


def rope_2d(x, tables):
    cr, sr, cc, sc = (t.astype(x.dtype) for t in tables)
    h = x.shape[-1] // 2
    return jnp.concatenate([_rotate(x[..., :h], cr, sr), _rotate(x[..., h:], cc, sc)], axis=-1)


def mla_expand(ckv, w_ukv):
    B, T, _ = ckv.shape
    kv = jnp.einsum('btc,cf->btf', ckv, w_ukv).reshape(B, T, MLA_HEADS, MLA_NOPE + MLA_V)
    return kv[..., :MLA_NOPE], kv[..., MLA_NOPE:]


def mla_attend(q_nope, q_pe, k_nope, k_pe, v):
    B, Tq, H, _ = q_nope.shape
    nb = Tq // Q_BLOCK

    def blocks(t):
        return jnp.moveaxis(t.reshape((B, nb, Q_BLOCK) + t.shape[2:]), 1, 0)

    def one_block(qs):
        qn, qp = qs
        s = jnp.einsum('bqhd,bkhd->bhqk', qn, k_nope) + jnp.einsum('bqhr,bkr->bhqk', qp, k_pe)
        p = jax.nn.softmax(s.astype(jnp.float32) * MLA_SCALE, axis=-1).astype(v.dtype)
        return jnp.einsum('bhqk,bkhd->bqhd', p, v)

    o = lax.map(one_block, (blocks(q_nope), blocks(q_pe)))
    return jnp.moveaxis(o, 0, 1).reshape(B, Tq, H * MLA_V)


def short_conv(z, w):
    T = z.shape[1]
    pad = GDN_CONV // 2
    zp = jnp.pad(z, ((0, 0), (pad, pad), (0, 0)))
    out = zp[:, 0:T] * w[0]
    for j in range(1, GDN_CONV):
        out = out + zp[:, j:j + T] * w[j]
    return out


def token_shift(z, mu_prev, mu_next):
    zp = jnp.pad(z, ((0, 0), (1, 0), (0, 0)))[:, :-1]
    zn = jnp.pad(z, ((0, 0), (0, 1), (0, 0)))[:, 1:]
    return z + mu_prev * (zp - z) + mu_next * (zn - z)


def chunk_gated_delta(q, k, v, g, beta, s0):
    B, T, H, K = q.shape
    V = v.shape[-1]
    C = GDN_CHUNK
    N = T // C
    f32 = jnp.float32

    def blocks(x):
        x = x.astype(f32).reshape((B, N, C, H) + x.shape[3:])
        return jnp.moveaxis(x, (1, 3), (0, 2))

    qc, kc, vc, gc, bc = blocks(q), blocks(k), blocks(v), blocks(g), blocks(beta)
    gcum = jnp.cumsum(gc, axis=-1)
    diff = gcum[..., :, None] - gcum[..., None, :]
    causal = jnp.tril(jnp.ones((C, C), bool))
    decay = jnp.where(causal, jnp.exp(jnp.where(causal, diff, 0.0)), 0.0)
    strict = jnp.tril(jnp.ones((C, C), f32), -1)
    kb = kc * bc[..., None]
    a_mat = jnp.einsum('nbhik,nbhjk->nbhij', kb, kc) * decay * strict
    rhs = jnp.concatenate([vc * bc[..., None], kb * jnp.exp(gcum)[..., None]], axis=-1)
    sol = lax.linalg.triangular_solve(a_mat + jnp.eye(C, dtype=f32), rhs, left_side=True,
                                      lower=True, unit_diagonal=True)
    u, w = sol[..., :V], sol[..., V:]
    qk = jnp.einsum('nbhik,nbhjk->nbhij', qc, kc) * decay
    g_last = gcum[..., -1]

    def step(S, xs):
        q_n, k_n, u_n, w_n, qk_n, gc_n, gl_n = xs
        v_new = u_n - jnp.einsum('bhck,bhkv->bhcv', w_n, S)
        o = (jnp.einsum('bhck,bhkv->bhcv', q_n * jnp.exp(gc_n)[..., None], S)
             + jnp.einsum('bhij,bhjv->bhiv', qk_n, v_new))
        k_dec = k_n * jnp.exp(gl_n[..., None] - gc_n)[..., None]
        S = S * jnp.exp(gl_n)[..., None, None] + jnp.einsum('bhck,bhcv->bhkv', k_dec, v_new)
        return S, o

    S, o = lax.scan(step, s0.astype(f32), (qc, kc, u, w, qk, gcum, g_last))
    o = jnp.moveaxis(o, (0, 2), (1, 3)).reshape(B, T, H, V)
    return o, S


def gdn_mixer(zg, lp, s0):
    B, T, _ = zg.shape
    f32 = jnp.float32
    qkv = jax.nn.silu(short_conv(zg[..., :GDN_CONV_CH], lp['gdn_conv']))
    q = l2norm(qkv[..., :GDN_QK].reshape(B, T, GDN_HEADS, GDN_DK)) * (GDN_DK ** -0.5)
    k = l2norm(qkv[..., GDN_QK:2 * GDN_QK].reshape(B, T, GDN_HEADS, GDN_DK))
    v = qkv[..., 2 * GDN_QK:].reshape(B, T, GDN_HEADS, GDN_DV).astype(f32)
    gate = zg[..., GDN_CONV_CH:GDN_CONV_CH + GDN_W].reshape(B, T, GDN_HEADS, GDN_DV)
    off = GDN_CONV_CH + GDN_W
    a = zg[..., off:off + 2 * GDN_HEADS].reshape(B, T, 2, GDN_HEADS).astype(f32)
    b = zg[..., off + 2 * GDN_HEADS:].reshape(B, T, 2, GDN_HEADS).astype(f32)
    g = -jnp.exp(lp['gdn_a_log'].astype(f32)) * jax.nn.softplus(a + lp['gdn_dt_bias'].astype(f32))
    beta = jax.nn.sigmoid(b)
    fl = lambda t: jnp.flip(t, axis=1)
    o_f, s_f = chunk_gated_delta(q, k, v, g[:, :, 0], beta[:, :, 0], s0[:, 0])
    o_b, s_b = chunk_gated_delta(fl(q), fl(k), fl(v), fl(g[:, :, 1]), fl(beta[:, :, 1]), s0[:, 1])
    o = (o_f + fl(o_b)).astype(zg.dtype)
    o = rmsnorm(o, lp['gdn_norm']) * jax.nn.silu(gate)
    return o.reshape(B, T, GDN_W), jnp.stack([s_f, s_b], axis=1)


def rwkv_direction(r, w, k, v, kk, a, s0):
    def step(S, xs):
        r_t, w_t, k_t, v_t, kk_t, a_t = xs
        sa = jnp.einsum('bhvk,bhk->bhv', S, -kk_t)
        S = (S * w_t[:, :, None, :] + sa[..., None] * (kk_t * a_t)[:, :, None, :]
             + v_t[..., None] * k_t[:, :, None, :])
        return S, jnp.einsum('bhvk,bhk->bhv', S, r_t)

    tm = lambda t: jnp.moveaxis(t, 1, 0)
    S, o = lax.scan(step, s0.astype(jnp.float32), (tm(r), tm(w), tm(k), tm(v), tm(kk), tm(a)))
    return jnp.moveaxis(o, 0, 1), S


def rwkv_mixer(zr, lp, s0):
    B, T, _ = zr.shape
    f32 = jnp.float32
    zr = token_shift(zr, lp['rwkv_mu_prev'], lp['rwkv_mu_next'])
    heads = lambda t: t.reshape(B, T, RWKV_HEADS, RWKV_N)
    fl = lambda t: jnp.flip(t, axis=1)
    r = heads(zr[..., :RWKV_W]).astype(f32)
    k = zr[..., RWKV_W:2 * RWKV_W].astype(f32)
    v = heads(zr[..., 2 * RWKV_W:3 * RWKV_W]).astype(f32)
    o1 = 3 * RWKV_W
    o2 = o1 + 2 * RWKV_W_LORA
    o3 = o2 + 2 * RWKV_A_LORA
    wd = zr[..., o1:o2].reshape(B, T, 2, RWKV_W_LORA)
    ad = zr[..., o2:o3].reshape(B, T, 2, RWKV_A_LORA)
    gd = zr[..., o3:]
    w_pre = lp['rwkv_w0'] + jnp.einsum('btdr,drc->btdc', jnp.tanh(wd), lp['rwkv_w2'])
    w_log = -jax.nn.softplus(-w_pre.astype(f32)) - 0.5
    decay = jnp.exp(-jnp.exp(w_log))
    a = jax.nn.sigmoid((lp['rwkv_a0'] + jnp.einsum('btdr,drc->btdc', ad, lp['rwkv_a2'])).astype(f32))
    gate = jnp.einsum('btr,rc->btc', jax.nn.sigmoid(gd), lp['rwkv_g2']).astype(f32)
    kk = l2norm(heads(k * lp['rwkv_k_k'].astype(f32)))
    k_a = lp['rwkv_k_a'].astype(f32)
    r_k = lp['rwkv_r_k'].astype(f32)

    def direction(d):
        a_flat = a[:, :, d]
        k_d = heads(k * (1.0 + (a_flat - 1.0) * k_a))
        a_d = heads(a_flat)
        w_d = heads(decay[:, :, d])
        bonus = jnp.sum(r * k_d * r_k, axis=-1, keepdims=True) * v
        seqs = (r, w_d, k_d, v, kk, a_d)
        if d == 1:
            seqs = tuple(fl(t) for t in seqs)
        o_d, s_d = rwkv_direction(*seqs, s0[:, d])
        if d == 1:
            o_d = fl(o_d)
        return o_d, bonus, s_d

    o_f, bonus_f, s_f = direction(0)
    o_b, bonus_b, s_b = direction(1)
    o = o_f + o_b
    mu = jnp.mean(o, axis=-1, keepdims=True)
    var = jnp.mean((o - mu) ** 2, axis=-1, keepdims=True)
    o = ((o - mu) * lax.rsqrt(var + RWKV_GN_EPS) * lp['rwkv_gn_w'].astype(f32).reshape(RWKV_HEADS, RWKV_N)
         + lp['rwkv_gn_b'].astype(f32).reshape(RWKV_HEADS, RWKV_N))
    out = (o + bonus_f + bonus_b) * heads(gate)
    return out.reshape(B, T, RWKV_W).astype(zr.dtype), jnp.stack([s_f, s_b], axis=1)


def token_mixers(h, lp, cache):
    B, T, _ = h.shape
    z = jnp.einsum('btd,dp->btp', h, lp['w_in'])
    zm, zg, zr = z[..., :P_MLA], z[..., P_MLA:P_MLA + P_GDN], z[..., P_MLA + P_GDN:]
    cq = rmsnorm(zm[..., :Q_LORA], lp['mla_q_norm'])
    q = jnp.einsum('btc,cf->btf', cq, lp['mla_w_uq']).reshape(B, T, MLA_HEADS, MLA_NOPE + MLA_ROPE)
    q_nope, q_pe = q[..., :MLA_NOPE], q[..., MLA_NOPE:]
    ckv = rmsnorm(zm[..., Q_LORA:Q_LORA + KV_LORA], lp['mla_kv_norm'])
    k_pe = zm[..., Q_LORA + KV_LORA:]
    k_nope, v = mla_expand(ckv, lp['mla_w_ukv'])
    if cache is None:
        s0_gdn = jnp.zeros((B, 2, GDN_HEADS, GDN_DK, GDN_DV), jnp.float32)
        s0_rwkv = jnp.zeros((B, 2, RWKV_HEADS, RWKV_N, RWKV_N), jnp.float32)
        k_pe_all = k_pe
    else:
        ckv_ctx, kpe_ctx, s0_gdn, s0_rwkv = cache
        tabs = rope_2d_tables(T)
        q_pe = rope_2d(q_pe, tuple(t[:, None, :] for t in tabs))
        kn_ctx, v_ctx = mla_expand(ckv_ctx, lp['mla_w_ukv'])
        k_nope = jnp.concatenate([kn_ctx, k_nope], axis=1)
        v = jnp.concatenate([v_ctx, v], axis=1)
        k_pe_all = jnp.concatenate([kpe_ctx, rope_2d(k_pe, tabs)], axis=1)
    o_mla = mla_attend(q_nope, q_pe, k_nope, k_pe_all, v)
    o_gdn, s_gdn = gdn_mixer(zg, lp, s0_gdn)
    o_rwkv, s_rwkv = rwkv_mixer(zr, lp, s0_rwkv)
    mixed = jnp.concatenate([o_mla, o_gdn, o_rwkv], axis=-1)
    out = jnp.einsum('btm,md->btd', mixed, lp['w_out'])
    return out, (ckv, k_pe, s_gdn, s_rwkv)


def routed_experts(x, eidx, ew, w_gu, w_down):
    n, D = x.shape
    m = n * TOP_K
    flat_e = eidx.reshape(m)
    order = jnp.argsort(flat_e)
    e_sorted = flat_e[order]
    tok = (order // TOP_K).astype(jnp.int32)
    counts = jnp.bincount(flat_e, length=N_EXPERTS)
    padded = (counts + MOE_BLOCK - 1) // MOE_BLOCK * MOE_BLOCK
    pad_end = jnp.cumsum(padded)
    pad_start = pad_end - padded
    start = jnp.cumsum(counts) - counts
    dest = pad_start[e_sorted] + jnp.arange(m) - start[e_sorted]
    n_blocks = m // MOE_BLOCK + N_EXPERTS
    rows = n_blocks * MOE_BLOCK
    row_tok = jnp.zeros((rows,), jnp.int32).at[dest].set(tok)
    row_w = jnp.zeros((rows,), x.dtype).at[dest].set(ew.reshape(m)[order].astype(x.dtype))
    block_e = jnp.minimum(jnp.searchsorted(pad_end, jnp.arange(n_blocks) * MOE_BLOCK, side='right'),
                          N_EXPERTS - 1)

    def expert_block(args):
        tok_b, e = args
        return swiglu(x[tok_b], w_gu[e], w_down[e])

    y = lax.map(expert_block, (row_tok.reshape(n_blocks, MOE_BLOCK), block_e)).reshape(rows, D)
    return jnp.zeros_like(x).at[row_tok].add(y * row_w[:, None])


def moe_ffn(h, lp):
    B, T, D = h.shape
    x = h.reshape(B * T, D)
    n = x.shape[0]
    scores = jax.nn.sigmoid(jnp.einsum('nd,de->ne', x, lp['moe_router']).astype(jnp.float32))
    sel = scores + lp['moe_bias'].astype(jnp.float32)
    grp = jnp.sum(lax.top_k(sel.reshape(n, N_GROUPS, N_EXPERTS // N_GROUPS), 2)[0], axis=-1)
    _, gidx = lax.top_k(grp, TOPK_GROUPS)
    gmask = jnp.sum(jax.nn.one_hot(gidx, N_GROUPS, dtype=jnp.float32), axis=1) > 0
    emask = jnp.repeat(gmask, N_EXPERTS // N_GROUPS, axis=1)
    _, eidx = lax.top_k(jnp.where(emask, sel, -jnp.inf), TOP_K)
    ew = jnp.take_along_axis(scores, eidx, axis=1)
    ew = ew / jnp.sum(ew, axis=-1, keepdims=True) * ROUTE_SCALE
    routed = routed_experts(x, eidx, ew, lp['moe_w_gu'], lp['moe_w_down'])
    shared = swiglu(x, lp['shared_w_gu'], lp['shared_w_down'])
    return (routed + shared).reshape(B, T, D)


def trunk_layer(x, mods, lp, cache):
    sh1, sc1, g1, sh2, sc2, g2 = mods
    h = rmsnorm(x, lp['norm1']) * (1.0 + sc1) + sh1
    mix, ctx_tensors = token_mixers(h, lp, cache)
    x = x + g1 * mix
    h = rmsnorm(x, lp['norm2']) * (1.0 + sc2) + sh2
    x = x + g2 * moe_ffn(h, lp)
    return x, ctx_tensors


def setup_inputs(seed: int = 0) -> dict:
    key = jax.random.key(seed)
    ks = jax.random.split(key, 48)
    ctr = [0]

    def nxt():
        k = ks[ctr[0]]
        ctr[0] += 1
        return k

    def nrm(shape, scale):
        return jax.random.normal(nxt(), shape, jnp.float32) * scale

    def unif(shape, lo, hi):
        return jax.random.uniform(nxt(), shape, jnp.float32, lo, hi)

    def gain(shape):
        return 1.0 + nrm(shape, 0.02)

    D = D_MODEL
    return {
        'x_prompt': nrm((BATCH, SEQ, D), 1.0),
        'x_sample': nrm((DEC_BATCH, DEC_SEQ, D), 1.0),
        'cache_mla_ckv': nrm((DEC_BATCH, DEPTH, PAST_LEN, KV_LORA), 1.0),
        'cache_mla_kpe': nrm((DEC_BATCH, DEPTH, PAST_LEN, MLA_ROPE), 1.0),
        'state_gdn': nrm((DEC_BATCH, DEPTH, 2, GDN_HEADS, GDN_DK, GDN_DV), 0.3),
        'state_rwkv': nrm((DEC_BATCH, DEPTH, 2, RWKV_HEADS, RWKV_N, RWKV_N), 0.3),
        'c': nrm((DEC_BATCH, D), 1.0),
        'c_ctx': nrm((D,), 1.0),
        'ada_w': nrm((DEPTH, D, 6 * D), 0.5 * D ** -0.5),
        'ada_b': nrm((DEPTH, 6 * D), 0.02),
        'norm1': gain((DEPTH, D)),
        'w_in': nrm((DEPTH, D, P_IN), D ** -0.5),
        'mla_q_norm': gain((DEPTH, Q_LORA)),
        'mla_w_uq': nrm((DEPTH, Q_LORA, MLA_HEADS * (MLA_NOPE + MLA_ROPE)), Q_LORA ** -0.5),
        'mla_kv_norm': gain((DEPTH, KV_LORA)),
        'mla_w_ukv': nrm((DEPTH, KV_LORA, MLA_HEADS * (MLA_NOPE + MLA_V)), KV_LORA ** -0.5),
        'gdn_conv': nrm((DEPTH, GDN_CONV, GDN_CONV_CH), GDN_CONV ** -0.5),
        'gdn_a_log': jnp.log(unif((DEPTH, 2, GDN_HEADS), 1.0, 16.0)),
        'gdn_dt_bias': unif((DEPTH, 2, GDN_HEADS), -4.5, -2.5),
        'gdn_norm': gain((DEPTH, GDN_DV)),
        'rwkv_mu_prev': unif((DEPTH, P_RWKV), 0.0, 0.5),
        'rwkv_mu_next': unif((DEPTH, P_RWKV), 0.0, 0.5),
        'rwkv_w0': unif((DEPTH, 2, RWKV_W), -6.0, 1.0),
        'rwkv_w2': nrm((DEPTH, 2, RWKV_W_LORA, RWKV_W), RWKV_W_LORA ** -0.5),
        'rwkv_a0': nrm((DEPTH, 2, RWKV_W), 0.5),
        'rwkv_a2': nrm((DEPTH, 2, RWKV_A_LORA, RWKV_W), RWKV_A_LORA ** -0.5),
        'rwkv_g2': nrm((DEPTH, RWKV_G_LORA, RWKV_W), RWKV_G_LORA ** -0.5),
        'rwkv_k_k': 0.85 + nrm((DEPTH, RWKV_W), 0.02),
        'rwkv_k_a': gain((DEPTH, RWKV_W)),
        'rwkv_r_k': nrm((DEPTH, RWKV_HEADS, RWKV_N), 0.1),
        'rwkv_gn_w': gain((DEPTH, RWKV_W)),
        'rwkv_gn_b': nrm((DEPTH, RWKV_W), 0.02),
        'w_out': nrm((DEPTH, MIX_W, D), MIX_W ** -0.5),
        'norm2': gain((DEPTH, D)),
        'moe_router': nrm((DEPTH, D, N_EXPERTS), D ** -0.5),
        'moe_bias': nrm((DEPTH, N_EXPERTS), 0.01),
        'moe_w_gu': nrm((DEPTH, N_EXPERTS, D, 2 * D_EXPERT), D ** -0.5),
        'moe_w_down': nrm((DEPTH, N_EXPERTS, D_EXPERT, D), D_EXPERT ** -0.5),
        'shared_w_gu': nrm((DEPTH, D, 2 * D_SHARED), D ** -0.5),
        'shared_w_down': nrm((DEPTH, D_SHARED, D), D_SHARED ** -0.5),
        'norm_f': gain((D,)),
    }


def reference(x_prompt, x_sample, cache_mla_ckv, cache_mla_kpe, state_gdn, state_rwkv, c, c_ctx,
              ada_w, ada_b, norm1, w_in, mla_q_norm, mla_w_uq, mla_kv_norm, mla_w_ukv,
              gdn_conv, gdn_a_log, gdn_dt_bias, gdn_norm,
              rwkv_mu_prev, rwkv_mu_next, rwkv_w0, rwkv_w2, rwkv_a0, rwkv_a2, rwkv_g2,
              rwkv_k_k, rwkv_k_a, rwkv_r_k, rwkv_gn_w, rwkv_gn_b,
              w_out, norm2, moe_router, moe_bias, moe_w_gu, moe_w_down, shared_w_gu, shared_w_down,
              norm_f):
    def layer_params(l):
        return {
            'norm1': norm1[l], 'w_in': w_in[l],
            'mla_q_norm': mla_q_norm[l], 'mla_w_uq': mla_w_uq[l],
            'mla_kv_norm': mla_kv_norm[l], 'mla_w_ukv': mla_w_ukv[l],
            'gdn_conv': gdn_conv[l], 'gdn_a_log': gdn_a_log[l], 'gdn_dt_bias': gdn_dt_bias[l],
            'gdn_norm': gdn_norm[l],
            'rwkv_mu_prev': rwkv_mu_prev[l], 'rwkv_mu_next': rwkv_mu_next[l],
            'rwkv_w0': rwkv_w0[l], 'rwkv_w2': rwkv_w2[l], 'rwkv_a0': rwkv_a0[l], 'rwkv_a2': rwkv_a2[l],
            'rwkv_g2': rwkv_g2[l], 'rwkv_k_k': rwkv_k_k[l], 'rwkv_k_a': rwkv_k_a[l],
            'rwkv_r_k': rwkv_r_k[l], 'rwkv_gn_w': rwkv_gn_w[l], 'rwkv_gn_b': rwkv_gn_b[l],
            'w_out': w_out[l], 'norm2': norm2[l],
            'moe_router': moe_router[l], 'moe_bias': moe_bias[l],
            'moe_w_gu': moe_w_gu[l], 'moe_w_down': moe_w_down[l],
            'shared_w_gu': shared_w_gu[l], 'shared_w_down': shared_w_down[l],
        }

    xp = x_prompt
    ckv_list, kpe_list, sg_list, sr_list = [], [], [], []
    for l in range(DEPTH):
        lp = layer_params(l)
        mods = adaln(c_ctx[None, :], ada_w[l], ada_b[l])
        xp, (ckv_l, kpe_l, sg_l, sr_l) = trunk_layer(xp, mods, lp, None)
        ckv_list.append(ckv_l)
        kpe_list.append(kpe_l)
        sg_list.append(sg_l)
        sr_list.append(sr_l)
    y_prompt = rmsnorm(xp, norm_f)

    xs = x_sample
    for l in range(DEPTH):
        lp = layer_params(l)
        mods = adaln(c, ada_w[l], ada_b[l])
        cache = (cache_mla_ckv[:, l], cache_mla_kpe[:, l], state_gdn[:, l], state_rwkv[:, l])
        xs, _ = trunk_layer(xs, mods, lp, cache)
    y_sample = rmsnorm(xs, norm_f)

    new_cache_mla_ckv = jnp.stack(ckv_list, axis=1)
    new_cache_mla_kpe = jnp.stack(kpe_list, axis=1)
    new_state_gdn = jnp.stack(sg_list, axis=1)
    new_state_rwkv = jnp.stack(sr_list, axis=1)
    return (y_prompt, y_sample, new_cache_mla_ckv, new_cache_mla_kpe, new_state_gdn, new_state_rwkv)
```

```python
import functools

import numpy as np
import jax
import jax.numpy as jnp
from jax import lax
from jax.experimental import pallas as pl
from jax.experimental.pallas import tpu as pltpu

F32 = jnp.float32
BF16 = jnp.bfloat16

D_MODEL = 1024
BATCH = 32
SEQ = 256
DEPTH = 2
DEC_BATCH = 2
DEC_SEQ = 1024
PAST_LEN = 512
GRID_W = 64
NORM_EPS = 1e-6

MLA_HEADS = 4
MLA_NOPE = 128
MLA_ROPE = 64
MLA_V = 128
Q_LORA = 384
KV_LORA = 256
ROPE_BASE = 10000.0
MLA_SCALE = (MLA_NOPE + MLA_ROPE) ** -0.5

HEADS = 4
HEAD_DIM = 64
MIX_W = HEADS * HEAD_DIM
GDN_CONV_CH = 3 * MIX_W
CHUNK = 64
RWKV_GN_EPS = 64e-5

N_EXPERTS = 64
TOP_K = 8
N_GROUPS = 8
GROUP_SIZE = N_EXPERTS // N_GROUPS
TOPK_GROUPS = 4
D_EXPERT = 256
ROUTE_SCALE = 2.5

P_MLA = Q_LORA + KV_LORA + MLA_ROPE
P_GDN = GDN_CONV_CH + MIX_W + 4 * HEADS
P_RWKV = 3 * MIX_W + 128 + 128 + 128

LANES = 128
ZM_W = Q_LORA + KV_LORA + 2 * LANES
ZG_W = GDN_CONV_CH + MIX_W
ZR_W = P_RWKV
QH_W = 2 * LANES
VMEM_LIMIT = 48 * 1024 * 1024

_ROPE_SWAP = np.concatenate([np.arange(16, 32), np.arange(0, 16), np.arange(48, 64), np.arange(32, 48)])


def _sigmoid(x):
    return 1.0 / (1.0 + jnp.exp(-x))


def _silu(x):
    return x * _sigmoid(x)


def _softplus(x):
    return jnp.maximum(x, 0.0) + jnp.log(1.0 + jnp.exp(-jnp.abs(x)))


def _rms(x, g, eps=NORM_EPS):
    return x * lax.rsqrt(jnp.mean(x * x, axis=-1, keepdims=True) + eps) * g


def _mm(a, b):
    return jnp.dot(a.astype(BF16), b.astype(BF16), preferred_element_type=F32)


def _mm_nt(a, b):
    return lax.dot_general(a.astype(BF16), b.astype(BF16), (((1,), (1,)), ((), ())),
                           preferred_element_type=F32)


def _mm_tn(a, b):
    return lax.dot_general(a.astype(BF16), b.astype(BF16), (((0,), (0,)), ((), ())),
                           preferred_element_type=F32)


def _split3(x):
    p1 = x.astype(BF16)
    r1 = x - p1.astype(F32)
    p2 = r1.astype(BF16)
    r2 = r1 - p2.astype(F32)
    return p1, p2, r2.astype(BF16)


def _mm_sel_l(sel, x):
    p1, p2, p3 = _split3(x)
    return _mm(sel, p1) + _mm(sel, p2) + _mm(sel, p3)


def _mm_sel_r(x, sel):
    p1, p2, p3 = _split3(x)
    return _mm(p1, sel) + _mm(p2, sel) + _mm(p3, sel)


def _iota(shape, dim):
    return lax.broadcasted_iota(jnp.int32, shape, dim)


def _bd(x, maskbd):
    xb = x.astype(BF16)
    return jnp.concatenate([xb] * HEADS, axis=0) * maskbd


def _chunk_masks(rev):
    row = _iota((CHUNK, MIX_W), 0)
    col = jnp.bitwise_and(_iota((CHUNK, MIX_W), 1), HEAD_DIM - 1)
    r2 = _iota((CHUNK, CHUNK), 0)
    c2 = _iota((CHUNK, CHUNK), 1)
    if rev:
        inc, strict, tri = row <= col, row < col, r2 <= c2
    else:
        inc, strict, tri = row >= col, row > col, r2 >= c2
    eye = jnp.where(row == col, 1.0, 0.0).astype(F32)
    return inc, strict, jnp.where(tri, 1.0, 0.0).astype(BF16), eye


def _neumann_inverse(a, eye, maskbd):
    b = -a
    m = eye + b
    p = _mm(b, _bd(b, maskbd))
    for _ in range(4):
        both = _mm(jnp.concatenate([m, p], axis=0), _bd(p, maskbd))
        m = m + both[:CHUNK]
        p = both[CHUNK:]
    return m + _mm(m, _bd(p, maskbd))


def _adaln_kernel(c_ref, w_ref, b_ref, o_ref):
    cv = c_ref[...]
    o_ref[0] = _mm(_silu(cv), w_ref[0]) + b_ref[0]


def _adaln(cvec8, ada_w, ada_b):
    tn = 768
    n_out = 6 * D_MODEL
    return pl.pallas_call(
        _adaln_kernel,
        grid=(DEPTH, n_out // tn),
        in_specs=[
            pl.BlockSpec((8, D_MODEL), lambda l, j: (0, 0)),
            pl.BlockSpec((1, D_MODEL, tn), lambda l, j: (l, 0, j)),
            pl.BlockSpec((1, 1, tn), lambda l, j: (l, 0, j)),
        ],
        out_specs=pl.BlockSpec((1, 8, tn), lambda l, j: (l, 0, j)),
        out_shape=jax.ShapeDtypeStruct((DEPTH, 8, n_out), F32),
        compiler_params=pltpu.CompilerParams(dimension_semantics=("arbitrary", "arbitrary"),
                                             vmem_limit_bytes=VMEM_LIMIT),
        name="adaln",
    )(cvec8, ada_w, ada_b.reshape(DEPTH, 1, n_out))


def _inproj_kernel(x_ref, mod_ref, n1_ref, w_ref, zm_ref, zg_ref, zab_ref, zr_ref):
    m = mod_ref[0]
    sh = m[:, 0:D_MODEL]
    sc = m[:, D_MODEL:2 * D_MODEL]
    h = _rms(x_ref[...], n1_ref[...]) * (1.0 + sc) + sh
    z = _mm(h, w_ref[...])
    o1 = ZM_W
    o2 = o1 + ZG_W
    o3 = o2 + LANES
    zm_ref[...] = z[:, :o1]
    zg_ref[...] = z[:, o1:o2]
    zab_ref[...] = z[:, o2:o3]
    zr_ref[...] = z[:, o3:]


def _inproj(x2d, mods, n1, w, tm, seq_len, mod_base):
    n = x2d.shape[0]
    wtot = w.shape[1]
    tiles_per_seq = seq_len // tm if mod_base else 1

    def mod_idx(i):
        return (mod_base + i // tiles_per_seq if mod_base else 0, 0, 0)

    return pl.pallas_call(
        _inproj_kernel,
        grid=(n // tm,),
        in_specs=[
            pl.BlockSpec((tm, D_MODEL), lambda i: (i, 0)),
            pl.BlockSpec((1, 1, 6 * D_MODEL), mod_idx),
            pl.BlockSpec((1, D_MODEL), lambda i: (0, 0)),
            pl.BlockSpec((D_MODEL, wtot), lambda i: (0, 0)),
        ],
        out_specs=[
            pl.BlockSpec((tm, ZM_W), lambda i: (i, 0)),
            pl.BlockSpec((tm, ZG_W), lambda i: (i, 0)),
            pl.BlockSpec((tm, LANES), lambda i: (i, 0)),
            pl.BlockSpec((tm, ZR_W), lambda i: (i, 0)),
        ],
        out_shape=[
            jax.ShapeDtypeStruct((n, ZM_W), F32),
            jax.ShapeDtypeStruct((n, ZG_W), F32),
            jax.ShapeDtypeStruct((n, LANES), F32),
            jax.ShapeDtypeStruct((n, ZR_W), F32),
        ],
        compiler_params=pltpu.CompilerParams(dimension_semantics=("arbitrary",),
                                             vmem_limit_bytes=VMEM_LIMIT),
        name="inproj",
    )(x2d, mods, n1, w)


def _mla_kernel(*refs, seq_len, tq, past, cached):
    if cached:
        (zm_ref, cckv_ref, ckpe_ref, rc_ref, rs_ref, qn_ref, wuq_ref, wuqs_ref, kvn_ref, wukv_ref,
         o_ref, ckv_ref, k_s, v_s) = refs
    else:
        (zm_ref, qn_ref, wuq_ref, kvn_ref, wukv_ref, o_ref, ckv_ref, k_s, v_s) = refs
    qi = pl.program_id(1)
    o_kpe = Q_LORA + KV_LORA

    @pl.when(qi == 0)
    def _():
        zm = zm_ref[...]
        ckv = _rms(zm[:, Q_LORA:o_kpe], kvn_ref[...])
        ckv_ref[...] = ckv
        kpe = zm[:, o_kpe:o_kpe + LANES]
        if cached:
            kpe = kpe * rc_ref[...] + zm[:, o_kpe + LANES:o_kpe + 2 * LANES] * rs_ref[...]
            kvc = _mm(cckv_ref[0], wukv_ref[...])
            kpc = ckpe_ref[0].astype(BF16)
        kv = _mm(ckv, wukv_ref[...])
        kpe = kpe.astype(BF16)
        for h in range(MLA_HEADS):
            c0 = h * QH_W
            if cached:
                k_s[0:past, c0:c0 + LANES] = kvc[:, c0:c0 + LANES].astype(BF16)
                k_s[0:past, c0 + LANES:c0 + QH_W] = kpc
                v_s[0:past, h * MLA_V:(h + 1) * MLA_V] = kvc[:, c0 + LANES:c0 + QH_W].astype(BF16)
            k_s[past:past + seq_len, c0:c0 + LANES] = kv[:, c0:c0 + LANES].astype(BF16)
            k_s[past:past + seq_len, c0 + LANES:c0 + QH_W] = kpe
            v_s[past:past + seq_len, h * MLA_V:(h + 1) * MLA_V] = kv[:, c0 + LANES:c0 + QH_W].astype(BF16)

    r0 = pl.multiple_of(qi * tq, tq)
    zq = zm_ref[pl.ds(r0, tq), :]
    cq = _rms(zq[:, :Q_LORA], qn_ref[...])
    q = _mm(cq, wuq_ref[...])
    if cached:
        qs = _mm(cq, wuqs_ref[...])
        ones = jnp.ones((tq, LANES), F32)
        zeros = jnp.zeros((tq, LANES), F32)
        qc = jnp.concatenate([ones, rc_ref[pl.ds(r0, tq), :]], axis=1)
        qsn = jnp.concatenate([zeros, rs_ref[pl.ds(r0, tq), :]], axis=1)
    for h in range(MLA_HEADS):
        c0 = h * QH_W
        qh = q[:, c0:c0 + QH_W]
        if cached:
            qh = qh * qc + qs[:, c0:c0 + QH_W] * qsn
        s = _mm_nt(qh, k_s[:, c0:c0 + QH_W]) * MLA_SCALE
        e = jnp.exp(s - jnp.max(s, axis=-1, keepdims=True))
        den = jnp.sum(e, axis=-1, keepdims=True)
        o_ref[:, h * MLA_V:(h + 1) * MLA_V] = _mm(e, v_s[:, h * MLA_V:(h + 1) * MLA_V]) / den


def _mla(zm, lw, seq_len, tq, cache=None):
    n = zm.shape[0]
    nb = n // seq_len
    cached = cache is not None
    past = PAST_LEN if cached else 0
    tk = past + seq_len
    seq_spec = lambda w: pl.BlockSpec((seq_len, w), lambda b, q: (b, 0))
    full = lambda a: pl.BlockSpec(a.shape, lambda b, q: (0,) * a.ndim)
    if cached:
        cckv, ckpe, rc, rs = cache
        args = [zm, cckv, ckpe, rc, rs, lw["q_norm"], lw["w_uq"], lw["w_uq_sw"], lw["kv_norm"], lw["w_ukv"]]
        in_specs = [seq_spec(ZM_W),
                    pl.BlockSpec((1, past, KV_LORA), lambda b, q: (b, 0, 0)),
                    pl.BlockSpec((1, past, LANES), lambda b, q: (b, 0, 0)),
                    full(rc), full(rs)] + [full(a) for a in args[5:]]
    else:
        args = [zm, lw["q_norm"], lw["w_uq"], lw["kv_norm"], lw["w_ukv"]]
        in_specs = [seq_spec(ZM_W)] + [full(a) for a in args[1:]]
    return pl.pallas_call(
        functools.partial(_mla_kernel, seq_len=seq_len, tq=tq, past=past, cached=cached),
        grid=(nb, seq_len // tq),
        in_specs=in_specs,
        out_specs=[
            pl.BlockSpec((tq, MLA_HEADS * MLA_V), lambda b, q: (b * (seq_len // tq) + q, 0)),
            pl.BlockSpec((seq_len, KV_LORA), lambda b, q: (b, 0)),
        ],
        out_shape=[
            jax.ShapeDtypeStruct((n, MLA_HEADS * MLA_V), F32),
            jax.ShapeDtypeStruct((n, KV_LORA), F32),
        ],
        scratch_shapes=[
            pltpu.VMEM((tk, MLA_HEADS * QH_W), BF16),
            pltpu.VMEM((tk, MLA_HEADS * MLA_V), BF16),
        ],
        compiler_params=pltpu.CompilerParams(dimension_semantics=("arbitrary", "arbitrary"),
                                             vmem_limit_bytes=VMEM_LIMIT),
        name="mla_lat" if cached else "mla_ctx",
    )(*args)


def _gdn_chunk(q, k, v, g, beta, s_bd, maskbd, masks, rev):
    inc, strict, tri, eye = masks
    gc = _mm_sel_l(tri, g)
    gc_row = jnp.sum(eye * gc, axis=0, keepdims=True)
    decay = jnp.where(inc, jnp.exp(jnp.where(inc, gc - gc_row, 0.0)), 0.0)
    kb = k * beta
    aq = _mm_nt(jnp.concatenate([kb, q], axis=0), _bd(k, maskbd))
    a_mat = jnp.where(strict, aq[:CHUNK] * decay, 0.0)
    qk = aq[CHUNK:] * decay
    t_inv = _neumann_inverse(a_mat, eye, maskbd)
    egc = jnp.exp(gc)
    u = _mm(t_inv, _bd(v * beta, maskbd))
    w = _mm(t_inv, _bd(kb * egc, maskbd))
    wq = _mm(jnp.concatenate([w, q * egc], axis=0), s_bd)
    v_new = u - wq[:CHUNK]
    o = wq[CHUNK:] + _mm(qk, _bd(v_new, maskbd))
    g_last = gc[0:1] if rev else gc[CHUNK - 1:CHUNK]
    k_dec = k * jnp.exp(g_last - gc)
    s_new = s_bd * jnp.exp(g_last) + _mm_tn(k_dec, v_new) * maskbd.astype(F32)
    return o, s_new


def _gdn_kernel(*refs, seq_len, cached):
    if cached:
        (zg_ref, zab_ref, s0_ref, conv_ref, alog_ref, dtb_ref, gn_ref, eexp_ref, ones_ref, maskbd_ref,
         o_ref, sout_ref, q_s, k_s, v_s, ge_s, oacc_s, st_s) = refs
    else:
        (zg_ref, zab_ref, conv_ref, alog_ref, dtb_ref, gn_ref, eexp_ref, ones_ref, maskbd_ref,
         o_ref, sout_ref, q_s, k_s, v_s, ge_s, oacc_s, st_s) = refs
    t = seq_len
    z = zg_ref[:, :GDN_CONV_CH]
    rowi = _iota((t, 1), 0)
    zp = jnp.where(rowi == 0, 0.0, pltpu.roll(z, 1, 0))
    zn = jnp.where(rowi == t - 1, 0.0, pltpu.roll(z, t - 1, 0))
    cw = conv_ref[...]
    qkv = _silu(zp * cw[0:1] + z * cw[1:2] + zn * cw[2:3])
    ones_bd = ones_ref[...]
    q = qkv[:, :MIX_W]
    k = qkv[:, MIX_W:2 * MIX_W]
    q_s[...] = q * lax.rsqrt(_mm_sel_r(q * q, ones_bd) + 1e-6) * (HEAD_DIM ** -0.5)
    k_s[...] = k * lax.rsqrt(_mm_sel_r(k * k, ones_bd) + 1e-6)
    v_s[...] = qkv[:, 2 * MIX_W:]
    ab = zab_ref[...]
    lane = _iota((t, LANES), 1)
    gb = jnp.where(lane < 2 * HEADS, -jnp.exp(alog_ref[...]) * _softplus(ab + dtb_ref[...]), _sigmoid(ab))
    ge_s[...] = _mm_sel_r(gb, eexp_ref[...])
    oacc_s[...] = jnp.zeros((t, MIX_W), F32)
    if cached:
        st_s[...] = s0_ref[0]
    else:
        st_s[...] = jnp.zeros((2, MIX_W, MIX_W), F32)
    maskbd = maskbd_ref[...]
    masks = (_chunk_masks(False), _chunk_masks(True))
    n_chunks = t // CHUNK

    def body(i, carry):
        for d in range(2):
            c = i if d == 0 else n_chunks - 1 - i
            rows = pl.ds(pl.multiple_of(c * CHUNK, CHUNK), CHUNK)
            o, s_new = _gdn_chunk(q_s[rows, :], k_s[rows, :], v_s[rows, :],
                                  ge_s[rows, d * MIX_W:(d + 1) * MIX_W],
                                  ge_s[rows, (2 + d) * MIX_W:(3 + d) * MIX_W],
                                  st_s[d], maskbd, masks[d], rev=(d == 1))
            oacc_s[rows, :] = oacc_s[rows, :] + o
            st_s[d] = s_new
        return carry

    lax.fori_loop(0, n_chunks, body, 0)
    o = oacc_s[...]
    ms = _mm_sel_r(o * o, ones_bd) * (1.0 / HEAD_DIM)
    gate = zg_ref[:, GDN_CONV_CH:]
    o_ref[...] = o * lax.rsqrt(ms + NORM_EPS) * gn_ref[...] * _silu(gate)
    for d in range(2):
        for h in range(HEADS):
            sl = slice(h * HEAD_DIM, (h + 1) * HEAD_DIM)
            sout_ref[0, d, h] = st_s[d, sl, sl]


def _gdn(zg, zab, lw, consts, seq_len, s0_bd=None):
    n = zg.shape[0]
    nb = n // seq_len
    cached = s0_bd is not None
    full = lambda a: pl.BlockSpec(a.shape, lambda b: (0,) * a.ndim)
    args = [zg, zab]
    in_specs = [pl.BlockSpec((seq_len, ZG_W), lambda b: (b, 0)),
                pl.BlockSpec((seq_len, LANES), lambda b: (b, 0))]
    if cached:
        args.append(s0_bd)
        in_specs.append(pl.BlockSpec((1, 2, MIX_W, MIX_W), lambda b: (b, 0, 0, 0)))
    tail = [lw["gdn_conv"], lw["gdn_alog"], lw["gdn_dtb"], lw["gdn_norm"], consts["eexp"], consts["ones_bd"],
            consts["maskbd"]]
    args += tail
    in_specs += [full(a) for a in tail]
    return pl.pallas_call(
        functools.partial(_gdn_kernel, seq_len=seq_len, cached=cached),
        grid=(nb,),
        in_specs=in_specs,
        out_specs=[
            pl.BlockSpec((seq_len, MIX_W), lambda b: (b, 0)),
            pl.BlockSpec((1, 2, HEADS, HEAD_DIM, HEAD_DIM), lambda b: (b, 0, 0, 0, 0)),
        ],
        out_shape=[
            jax.ShapeDtypeStruct((n, MIX_W), F32),
            jax.ShapeDtypeStruct((nb, 2, HEADS, HEAD_DIM, HEAD_DIM), F32),
        ],
        scratch_shapes=[
            pltpu.VMEM((seq_len, MIX_W), F32),
            pltpu.VMEM((seq_len, MIX_W), F32),
            pltpu.VMEM((seq_len, MIX_W), F32),
            pltpu.VMEM((seq_len, 4 * MIX_W), F32),
            pltpu.VMEM((seq_len, MIX_W), F32),
            pltpu.VMEM((2, MIX_W, MIX_W), F32),
        ],
        compiler_params=pltpu.CompilerParams(dimension_semantics=("arbitrary",),
                                             vmem_limit_bytes=VMEM_LIMIT),
        name="gdn_lat" if cached else "gdn_ctx",
    )(*args)


def _rwkv_chunk(r, kd, v, kk, b, lw, s_bd, maskbd, masks, rev):
    inc, strict, tri, eye = masks
    cum = _mm_sel_l(tri, lw)
    e = jnp.exp(cum)
    einv = jnp.exp(-cum)
    kr = jnp.concatenate([kk * jnp.exp(cum - lw), r * e], axis=0)
    lb_all = _mm_nt(kr, _bd(b * einv, maskbd))
    lk_all = _mm_nt(kr, _bd(kd * einv, maskbd))
    lb = jnp.where(strict, lb_all[:CHUNK], 0.0)
    rb = jnp.where(inc, lb_all[CHUNK:], 0.0)
    lk = jnp.where(strict, lk_all[:CHUNK], 0.0)
    rk = jnp.where(inc, lk_all[CHUNK:], 0.0)
    t_inv = _neumann_inverse(lb, eye, maskbd)
    ks_rs = _mm_nt(kr, s_bd)
    lv = _mm(jnp.concatenate([lk, rk], axis=0), _bd(v, maskbd))
    p = _mm(t_inv, _bd(ks_rs[:CHUNK] + lv[:CHUNK], maskbd))
    o = ks_rs[CHUNK:] + lv[CHUNK:] - _mm(rb, _bd(p, maskbd))
    c_last = cum[0:1] if rev else cum[CHUNK - 1:CHUNK]
    tail = jnp.exp(c_last - cum)
    upd = _mm_tn(jnp.concatenate([v, -p], axis=0), jnp.concatenate([kd * tail, b * tail], axis=0))
    s_new = s_bd * jnp.exp(c_last) + upd * maskbd.astype(F32)
    return o, s_new


def _rwkv_kernel(*refs, seq_len, cached):
    if cached:
        (zr_ref, s0_ref, mup_ref, mun_ref, w0_ref, w2_ref, a0_ref, a2_ref, g2_ref, kk_ref, ka_ref, rk_ref,
         gnw_ref, gnb_ref, ones_ref, maskbd_ref, o_ref, sout_ref,
         r_s, v_s, kk_s, dir_s, bg_s, oacc_s, st_s) = refs
    else:
        (zr_ref, mup_ref, mun_ref, w0_ref, w2_ref, a0_ref, a2_ref, g2_ref, kk_ref, ka_ref, rk_ref,
         gnw_ref, gnb_ref, ones_ref, maskbd_ref, o_ref, sout_ref,
         r_s, v_s, kk_s, dir_s, bg_s, oacc_s, st_s) = refs
    t = seq_len
    z = zr_ref[...]
    rowi = _iota((t, 1), 0)
    zp = jnp.where(rowi == 0, 0.0, pltpu.roll(z, 1, 0))
    zn = jnp.where(rowi == t - 1, 0.0, pltpu.roll(z, t - 1, 0))
    z = z + mup_ref[...] * (zp - z) + mun_ref[...] * (zn - z)
    w = MIX_W
    r = z[:, :w]
    k = z[:, w:2 * w]
    v = z[:, 2 * w:3 * w]
    wd = jnp.tanh(z[:, 3 * w:3 * w + LANES])
    ad = z[:, 3 * w + LANES:3 * w + 2 * LANES]
    gd = _sigmoid(z[:, 3 * w + 2 * LANES:])
    ones_bd = ones_ref[...]
    kk = k * kk_ref[...]
    kk = kk * lax.rsqrt(_mm_sel_r(kk * kk, ones_bd) + 1e-6)
    r_s[...] = r
    v_s[...] = v
    kk_s[...] = kk
    bonus = jnp.zeros((t, w), F32)
    for d in range(2):
        w_log = -_softplus(-(w0_ref[d:d + 1] + _mm(wd, w2_ref[d]))) - 0.5
        a = _sigmoid(a0_ref[d:d + 1] + _mm(ad, a2_ref[d]))
        kd = k * (1.0 + (a - 1.0) * ka_ref[...])
        dir_s[:, (3 * d) * w:(3 * d + 1) * w] = -jnp.exp(w_log)
        dir_s[:, (3 * d + 1) * w:(3 * d + 2) * w] = kd
        dir_s[:, (3 * d + 2) * w:(3 * d + 3) * w] = kk * a
        bonus = bonus + _mm_sel_r(r * kd * rk_ref[...], ones_bd) * v
    bg_s[:, :w] = bonus
    bg_s[:, w:] = _mm(gd, g2_ref[...])
    oacc_s[...] = jnp.zeros((t, w), F32)
    if cached:
        st_s[...] = s0_ref[0]
    else:
        st_s[...] = jnp.zeros((2, w, w), F32)
    maskbd = maskbd_ref[...]
    masks = (_chunk_masks(False), _chunk_masks(True))
    n_chunks = t // CHUNK

    def body(i, carry):
        for d in range(2):
            c = i if d == 0 else n_chunks - 1 - i
            rows = pl.ds(pl.multiple_of(c * CHUNK, CHUNK), CHUNK)
            o, s_new = _rwkv_chunk(r_s[rows, :], dir_s[rows, (3 * d + 1) * w:(3 * d + 2) * w], v_s[rows, :],
                                   kk_s[rows, :], dir_s[rows, (3 * d + 2) * w:(3 * d + 3) * w],
                                   dir_s[rows, (3 * d) * w:(3 * d + 1) * w],
                                   st_s[d], maskbd, masks[d], rev=(d == 1))
            oacc_s[rows, :] = oacc_s[rows, :] + o
            st_s[d] = s_new
        return carry

    lax.fori_loop(0, n_chunks, body, 0)
    o = oacc_s[...]
    inv_n = 1.0 / HEAD_DIM
    mu = _mm_sel_r(o, ones_bd) * inv_n
    oc = o - mu
    var = _mm_sel_r(oc * oc, ones_bd) * inv_n
    y = oc * lax.rsqrt(var + RWKV_GN_EPS) * gnw_ref[...] + gnb_ref[...]
    o_ref[...] = (y + bg_s[:, :w]) * bg_s[:, w:]
    for d in range(2):
        for h in range(HEADS):
            sl = slice(h * HEAD_DIM, (h + 1) * HEAD_DIM)
            sout_ref[0, d, h] = st_s[d, sl, sl]


def _rwkv(zr, lw, consts, seq_len, s0_bd=None):
    n = zr.shape[0]
    nb = n // seq_len
    cached = s0_bd is not None
    full = lambda a: pl.BlockSpec(a.shape, lambda b: (0,) * a.ndim)
    args = [zr]
    in_specs = [pl.BlockSpec((seq_len, ZR_W), lambda b: (b, 0))]
    if cached:
        args.append(s0_bd)
        in_specs.append(pl.BlockSpec((1, 2, MIX_W, MIX_W), lambda b: (b, 0, 0, 0)))
    tail = [lw["rwkv_mu_prev"], lw["rwkv_mu_next"], lw["rwkv_w0"], lw["rwkv_w2"], lw["rwkv_a0"], lw["rwkv_a2"],
            lw["rwkv_g2"], lw["rwkv_k_k"], lw["rwkv_k_a"], lw["rwkv_r_k"], lw["rwkv_gn_w"], lw["rwkv_gn_b"],
            consts["ones_bd"], consts["maskbd"]]
    args += tail
    in_specs += [full(a) for a in tail]
    return pl.pallas_call(
        functools.partial(_rwkv_kernel, seq_len=seq_len, cached=cached),
        grid=(nb,),
        in_specs=in_specs,
        out_specs=[
            pl.BlockSpec((seq_len, MIX_W), lambda b: (b, 0)),
            pl.BlockSpec((1, 2, HEADS, HEAD_DIM, HEAD_DIM), lambda b: (b, 0, 0, 0, 0)),
        ],
        out_shape=[
            jax.ShapeDtypeStruct((n, MIX_W), F32),
            jax.ShapeDtypeStruct((nb, 2, HEADS, HEAD_DIM, HEAD_DIM), F32),
        ],
        scratch_shapes=[
            pltpu.VMEM((seq_len, MIX_W), F32),
            pltpu.VMEM((seq_len, MIX_W), F32),
            pltpu.VMEM((seq_len, MIX_W), F32),
            pltpu.VMEM((seq_len, 6 * MIX_W), F32),
            pltpu.VMEM((seq_len, 2 * MIX_W), F32),
            pltpu.VMEM((seq_len, MIX_W), F32),
            pltpu.VMEM((2, MIX_W, MIX_W), F32),
        ],
        compiler_params=pltpu.CompilerParams(dimension_semantics=("arbitrary",),
                                             vmem_limit_bytes=VMEM_LIMIT),
        name="rwkv_lat" if cached else "rwkv_ctx",
    )(*args)


def _split2(x):
    hi = x.astype(BF16)
    return hi, (x - hi.astype(F32)).astype(BF16)


def _route(logits_t, bias):
    tm = logits_t.shape[1]
    neg = -jnp.inf
    sc = _sigmoid(logits_t)
    sel = (sc + bias).reshape(N_GROUPS, GROUP_SIZE, tm)
    si = _iota(sel.shape, 1).astype(F32)
    m1 = jnp.max(sel, axis=1, keepdims=True)
    f1 = jnp.min(jnp.where(sel == m1, si, float(GROUP_SIZE)), axis=1, keepdims=True)
    m2 = jnp.max(jnp.where(si == f1, neg, sel), axis=1, keepdims=True)
    grp = m1 + m2
    gi = _iota(grp.shape, 0).astype(F32)
    gsel = jnp.zeros(grp.shape, F32)
    for _ in range(TOPK_GROUPS):
        mx = jnp.max(grp, axis=0, keepdims=True)
        fi = jnp.min(jnp.where(grp == mx, gi, float(N_GROUPS)), axis=0, keepdims=True)
        hit = gi == fi
        gsel = jnp.where(hit, 1.0, gsel)
        grp = jnp.where(hit, neg, grp)
    cur = jnp.where(gsel > 0.0, sel, neg)
    ei = (_iota(cur.shape, 0) * GROUP_SIZE + _iota(cur.shape, 1)).astype(F32)
    chosen = jnp.zeros(cur.shape, F32)
    for _ in range(TOP_K):
        mx = jnp.max(jnp.max(cur, axis=0, keepdims=True), axis=1, keepdims=True)
        fi = jnp.min(jnp.min(jnp.where(cur == mx, ei, float(N_EXPERTS)), axis=0, keepdims=True),
                     axis=1, keepdims=True)
        hit = ei == fi
        chosen = jnp.where(hit, 1.0, chosen)
        cur = jnp.where(hit, neg, cur)
    wsel = chosen.reshape(N_EXPERTS, tm) * sc
    return wsel / jnp.sum(wsel, axis=0, keepdims=True) * ROUTE_SCALE


def _post_kernel(x_ref, om_ref, og_ref, or_ref, mod_ref, wo_ref, n2_ref, rt_ref, rb_ref,
                 x1_ref, h2_ref, cw_ref):
    m = mod_ref[0]
    g1 = m[:, 2 * D_MODEL:3 * D_MODEL]
    sh2 = m[:, 3 * D_MODEL:4 * D_MODEL]
    sc2 = m[:, 4 * D_MODEL:5 * D_MODEL]
    w_mla = MLA_HEADS * MLA_V
    mix = (_mm(om_ref[...], wo_ref[0:w_mla, :]) + _mm(og_ref[...], wo_ref[w_mla:w_mla + MIX_W, :])
           + _mm(or_ref[...], wo_ref[w_mla + MIX_W:, :]))
    x1 = x_ref[...] + g1 * mix
    x1_ref[...] = x1
    h2 = _rms(x1, n2_ref[...]) * (1.0 + sc2) + sh2
    h2_ref[...] = h2.astype(BF16)
    r_hi, r_lo = _split2(rt_ref[...])
    h_hi, h_lo = _split2(h2)
    logits_t = _mm_nt(r_hi, h_hi) + _mm_nt(r_hi, h_lo) + _mm_nt(r_lo, h_hi)
    cw_t = _route(logits_t, rb_ref[...])
    cw_ref[...] = jnp.concatenate([cw_t, jnp.zeros_like(cw_t)], axis=0).T


def _post(x2d, om, og, orw, mods, lw, tm, seq_len, mod_base):
    n = x2d.shape[0]
    tiles_per_seq = seq_len // tm if mod_base else 1

    def mod_idx(i):
        return (mod_base + i // tiles_per_seq if mod_base else 0, 0, 0)

    row = lambda w: pl.BlockSpec((tm, w), lambda i: (i, 0))
    full = lambda a: pl.BlockSpec(a.shape, lambda i: (0,) * a.ndim)
    tail = [lw["w_out"], lw["norm2"], lw["router_t"], lw["router_b"]]
    return pl.pallas_call(
        _post_kernel,
        grid=(n // tm,),
        in_specs=[row(D_MODEL), row(MLA_HEADS * MLA_V), row(MIX_W), row(MIX_W),
                  pl.BlockSpec((1, 1, 6 * D_MODEL), mod_idx)] + [full(a) for a in tail],
        out_specs=[row(D_MODEL), row(D_MODEL), row(LANES)],
        out_shape=[
            jax.ShapeDtypeStruct((n, D_MODEL), F32),
            jax.ShapeDtypeStruct((n, D_MODEL), BF16),
            jax.ShapeDtypeStruct((n, LANES), F32),
        ],
        compiler_params=pltpu.CompilerParams(dimension_semantics=("arbitrary",),
                                             vmem_limit_bytes=VMEM_LIMIT),
        name="post",
    )(x2d, om, og, orw, mods, *tail)


def _swiglu_act(gu):
    return _silu(gu[:, :D_EXPERT]) * gu[:, D_EXPERT:]


def _moe_kernel(h_ref, x1_ref, cw_ref, mod_ref, wgu_ref, wdn_ref, sgu_ref, sdn_ref, nf_ref, o_ref, acc_ref,
                *, final):
    e = pl.program_id(1)
    h = h_ref[...]

    @pl.when(e == 0)
    def _():
        acc_ref[...] = _mm(_swiglu_act(_mm(h, sgu_ref[...])), sdn_ref[...])

    act = _swiglu_act(_mm(h, wgu_ref[0]))
    sel = jnp.where(_iota((LANES, D_EXPERT), 0) == e, 1.0, 0.0).astype(BF16)
    scale = _mm_sel_r(cw_ref[...], sel)
    acc_ref[...] += _mm(act * scale, wdn_ref[0])

    @pl.when(e == N_EXPERTS - 1)
    def _():
        g2 = mod_ref[0][:, 5 * D_MODEL:]
        x2 = x1_ref[...] + g2 * acc_ref[...]
        if final:
            x2 = _rms(x2, nf_ref[...])
        o_ref[...] = x2


def _moe(h2, x1, cw, mods, lw, norm_f, tm, seq_len, mod_base, final):
    n = h2.shape[0]
    tiles_per_seq = seq_len // tm if mod_base else 1

    def mod_idx(i, e):
        return (mod_base + i // tiles_per_seq if mod_base else 0, 0, 0)

    row = lambda w: pl.BlockSpec((tm, w), lambda i, e: (i, 0))
    full = lambda a: pl.BlockSpec(a.shape, lambda i, e: (0,) * a.ndim)
    return pl.pallas_call(
        functools.partial(_moe_kernel, final=final),
        grid=(n // tm, N_EXPERTS),
        in_specs=[row(D_MODEL), row(D_MODEL), row(LANES),
                  pl.BlockSpec((1, 1, 6 * D_MODEL), mod_idx),
                  pl.BlockSpec((1, D_MODEL, 2 * D_EXPERT), lambda i, e: (e, 0, 0)),
                  pl.BlockSpec((1, D_EXPERT, D_MODEL), lambda i, e: (e, 0, 0)),
                  full(lw["shared_w_gu"]), full(lw["shared_w_down"]), full(norm_f)],
        out_specs=row(D_MODEL),
        out_shape=jax.ShapeDtypeStruct((n, D_MODEL), F32),
        scratch_shapes=[pltpu.VMEM((tm, D_MODEL), F32)],
        compiler_params=pltpu.CompilerParams(dimension_semantics=("arbitrary", "arbitrary"),
                                             vmem_limit_bytes=VMEM_LIMIT),
        name="moe_final" if final else "moe",
    )(h2, x1, cw, mods, lw["moe_w_gu"], lw["moe_w_down"], lw["shared_w_gu"], lw["shared_w_down"], norm_f)


def _constants():
    idx = np.arange(MIX_W)
    same_head = (idx[:, None] // HEAD_DIM) == (idx[None, :] // HEAD_DIM)
    maskbd = jnp.asarray(same_head, BF16)
    eexp = np.zeros((LANES, 4 * MIX_W), np.float32)
    for blk in range(4):
        kind, d = divmod(blk, 2)
        for h in range(HEADS):
            src = kind * 2 * HEADS + d * HEADS + h
            eexp[src, blk * MIX_W + h * HEAD_DIM: blk * MIX_W + (h + 1) * HEAD_DIM] = 1.0
    return {"maskbd": maskbd, "ones_bd": maskbd, "eexp": jnp.asarray(eexp, BF16)}


def _rope_tables(n):
    rows = n // GRID_W
    row = jnp.repeat(jnp.arange(rows, dtype=F32), GRID_W)
    col = jnp.tile(jnp.arange(GRID_W, dtype=F32), rows)
    axis_dim = MLA_ROPE // 2
    inv = jnp.power(ROPE_BASE, -jnp.arange(0, axis_dim, 2, dtype=F32) / axis_dim)
    ang_r = row[:, None] * inv
    ang_c = col[:, None] * inv
    cr, sr, cc, sc = jnp.cos(ang_r), jnp.sin(ang_r), jnp.cos(ang_c), jnp.sin(ang_c)
    zeros = jnp.zeros((n, LANES - MLA_ROPE), F32)
    cos_t = jnp.concatenate([cr, cr, cc, cc, zeros], axis=1)
    sin_t = jnp.concatenate([-sr, sr, -sc, sc, zeros], axis=1)
    return cos_t, sin_t


def _pad_lanes(v, width):
    v = v.reshape(1, -1)
    return jnp.pad(v, ((0, 0), (0, width - v.shape[1])))


def _layer_weights(p, l):
    w_in = p["w_in"][l]
    kpe0 = Q_LORA + KV_LORA
    zc = jnp.zeros((D_MODEL, LANES - MLA_ROPE), F32)
    w_in_p = jnp.concatenate([
        w_in[:, :P_MLA], zc, w_in[:, kpe0 + _ROPE_SWAP], zc,
        w_in[:, P_MLA:P_MLA + ZG_W],
        w_in[:, P_MLA + ZG_W:P_MLA + P_GDN], jnp.zeros((D_MODEL, LANES - 4 * HEADS), F32),
        w_in[:, P_MLA + P_GDN:]], axis=1).astype(BF16)
    w_uq = p["mla_w_uq"][l].reshape(Q_LORA, MLA_HEADS, MLA_NOPE + MLA_ROPE)
    zq = jnp.zeros((Q_LORA, MLA_HEADS, QH_W - MLA_NOPE - MLA_ROPE), F32)
    w_uq_a = jnp.concatenate([w_uq, zq], axis=2).reshape(Q_LORA, MLA_HEADS * QH_W).astype(BF16)
    w_uq_sw = jnp.concatenate([jnp.zeros((Q_LORA, MLA_HEADS, MLA_NOPE), F32),
                               w_uq[:, :, MLA_NOPE + _ROPE_SWAP], zq], axis=2)
    w_uq_sw = w_uq_sw.reshape(Q_LORA, MLA_HEADS * QH_W).astype(BF16)
    half = jnp.zeros((64, MIX_W), F32)
    w2 = p["rwkv_w2"][l]
    a2 = p["rwkv_a2"][l]
    tile4 = lambda v: jnp.tile(v.reshape(1, HEAD_DIM), (1, HEADS))
    return {
        "norm1": p["norm1"][l].reshape(1, D_MODEL),
        "w_in": w_in_p,
        "q_norm": p["mla_q_norm"][l].reshape(1, Q_LORA),
        "w_uq": w_uq_a, "w_uq_sw": w_uq_sw,
        "kv_norm": p["mla_kv_norm"][l].reshape(1, KV_LORA),
        "w_ukv": p["mla_w_ukv"][l].astype(BF16),
        "gdn_conv": p["gdn_conv"][l],
        "gdn_alog": _pad_lanes(p["gdn_a_log"][l], LANES),
        "gdn_dtb": _pad_lanes(p["gdn_dt_bias"][l], LANES),
        "gdn_norm": tile4(p["gdn_norm"][l]),
        "rwkv_mu_prev": p["rwkv_mu_prev"][l].reshape(1, ZR_W),
        "rwkv_mu_next": p["rwkv_mu_next"][l].reshape(1, ZR_W),
        "rwkv_w0": p["rwkv_w0"][l],
        "rwkv_w2": jnp.stack([jnp.concatenate([w2[0], half]), jnp.concatenate([half, w2[1]])]).astype(BF16),
        "rwkv_a0": p["rwkv_a0"][l],
        "rwkv_a2": jnp.stack([jnp.concatenate([a2[0], half]), jnp.concatenate([half, a2[1]])]).astype(BF16),
        "rwkv_g2": p["rwkv_g2"][l].astype(BF16),
        "rwkv_k_k": p["rwkv_k_k"][l].reshape(1, MIX_W),
        "rwkv_k_a": p["rwkv_k_a"][l].reshape(1, MIX_W),
        "rwkv_r_k": p["rwkv_r_k"][l].reshape(1, MIX_W),
        "rwkv_gn_w": p["rwkv_gn_w"][l].reshape(1, MIX_W),
        "rwkv_gn_b": p["rwkv_gn_b"][l].reshape(1, MIX_W),
        "w_out": p["w_out"][l].astype(BF16),
        "norm2": p["norm2"][l].reshape(1, D_MODEL),
        "router_t": p["moe_router"][l].T,
        "router_b": p["moe_bias"][l].reshape(N_EXPERTS, 1),
        "moe_w_gu": p["moe_w_gu"][l],
        "moe_w_down": p["moe_w_down"][l],
        "shared_w_gu": p["shared_w_gu"][l].astype(BF16),
        "shared_w_down": p["shared_w_down"][l].astype(BF16),
    }


def _embed_block_diag(s):
    b = s.shape[0]
    eye = jnp.eye(HEADS, dtype=s.dtype)
    out = jnp.einsum("bdhkv,hg->bdhkgv", s, eye)
    return out.reshape(b, 2, MIX_W, MIX_W)


def _trunk(x2d, mods, weights, consts, norm_f, seq_len, mod_base, cache, tm, tq, tm_moe):
    outs = []
    for l in range(DEPTH):
        lw = weights[l]
        ml = mods[l].reshape(8, 1, 6 * D_MODEL)
        zm, zg, zab, zr = _inproj(x2d, ml, lw["norm1"], lw["w_in"], tm, seq_len, mod_base)
        if cache is None:
            o_mla, ckv = _mla(zm, lw, seq_len, tq)
            o_gdn, s_gdn = _gdn(zg, zab, lw, consts, seq_len)
            o_rwkv, s_rwkv = _rwkv(zr, lw, consts, seq_len)
        else:
            cckv, ckpe, rc, rs, sg, sr = cache
            o_mla, ckv = _mla(zm, lw, seq_len, tq, (cckv[:, l], ckpe[:, l], rc, rs))
            o_gdn, s_gdn = _gdn(zg, zab, lw, consts, seq_len, sg[:, l])
            o_rwkv, s_rwkv = _rwkv(zr, lw, consts, seq_len, sr[:, l])
        x1, h2, cw = _post(x2d, o_mla, o_gdn, o_rwkv, ml, lw, tm, seq_len, mod_base)
        x2d = _moe(h2, x1, cw, ml, lw, norm_f, tm_moe, seq_len, mod_base, final=(l == DEPTH - 1))
        kpe0 = Q_LORA + KV_LORA
        outs.append((ckv, zm[:, kpe0:kpe0 + MLA_ROPE], s_gdn, s_rwkv))
    return x2d, outs


def kernel(x_prompt, x_sample, cache_mla_ckv, cache_mla_kpe, state_gdn, state_rwkv, c, c_ctx, ada_w, ada_b, norm1, w_in, mla_q_norm, mla_w_uq, mla_kv_norm, mla_w_ukv, gdn_conv, gdn_a_log, gdn_dt_bias, gdn_norm, rwkv_mu_prev, rwkv_mu_next, rwkv_w0, rwkv_w2, rwkv_a0, rwkv_a2, rwkv_g2, rwkv_k_k, rwkv_k_a, rwkv_r_k, rwkv_gn_w, rwkv_gn_b, w_out, norm2, moe_router, moe_bias, moe_w_gu, moe_w_down, shared_w_gu, shared_w_down, norm_f):
    p = dict(norm1=norm1, w_in=w_in, mla_q_norm=mla_q_norm, mla_w_uq=mla_w_uq, mla_kv_norm=mla_kv_norm,
             mla_w_ukv=mla_w_ukv, gdn_conv=gdn_conv, gdn_a_log=gdn_a_log, gdn_dt_bias=gdn_dt_bias,
             gdn_norm=gdn_norm, rwkv_mu_prev=rwkv_mu_prev, rwkv_mu_next=rwkv_mu_next, rwkv_w0=rwkv_w0,
             rwkv_w2=rwkv_w2, rwkv_a0=rwkv_a0, rwkv_a2=rwkv_a2, rwkv_g2=rwkv_g2, rwkv_k_k=rwkv_k_k,
             rwkv_k_a=rwkv_k_a, rwkv_r_k=rwkv_r_k, rwkv_gn_w=rwkv_gn_w, rwkv_gn_b=rwkv_gn_b, w_out=w_out,
             norm2=norm2, moe_router=moe_router, moe_bias=moe_bias, moe_w_gu=moe_w_gu, moe_w_down=moe_w_down,
             shared_w_gu=shared_w_gu, shared_w_down=shared_w_down)
    weights = [_layer_weights(p, l) for l in range(DEPTH)]
    consts = _constants()
    nf = norm_f.reshape(1, D_MODEL)
    b_ctx, t_ctx, _ = x_prompt.shape
    b_lat, t_lat, _ = x_sample.shape

    cvec8 = jnp.concatenate([c_ctx[None, :], c, jnp.zeros((8 - 1 - b_lat, D_MODEL), F32)], axis=0)
    mods = _adaln(cvec8, ada_w, ada_b)

    xp, ctx_outs = _trunk(x_prompt.reshape(b_ctx * t_ctx, D_MODEL), mods, weights, consts, nf,
                          t_ctx, 0, None, tm=256, tq=t_ctx, tm_moe=1024)
    rc, rs = _rope_tables(t_lat)
    ckpe = jnp.pad(cache_mla_kpe, ((0, 0), (0, 0), (0, 0), (0, LANES - MLA_ROPE)))
    cache = (cache_mla_ckv, ckpe, rc, rs, _embed_block_diag_layers(state_gdn), _embed_block_diag_layers(state_rwkv))
    xs, _ = _trunk(x_sample.reshape(b_lat * t_lat, D_MODEL), mods, weights, consts, nf,
                   t_lat, 1, cache, tm=256, tq=256, tm_moe=1024)

    y_prompt = xp.reshape(b_ctx, t_ctx, D_MODEL)
    y_sample = xs.reshape(b_lat, t_lat, D_MODEL)
    new_ckv = jnp.stack([o[0].reshape(b_ctx, t_ctx, KV_LORA) for o in ctx_outs], axis=1)
    new_kpe = jnp.stack([o[1].reshape(b_ctx, t_ctx, MLA_ROPE) for o in ctx_outs], axis=1)
    new_gdn = jnp.stack([o[2] for o in ctx_outs], axis=1)
    new_rwkv = jnp.stack([o[3] for o in ctx_outs], axis=1)
    return (y_prompt, y_sample, new_ckv, new_kpe, new_gdn, new_rwkv)


def _embed_block_diag_layers(s):
    b = s.shape[0]
    return _embed_block_diag(s.reshape(b * DEPTH, 2, HEADS, HEAD_DIM, HEAD_DIM)).reshape(
        b, DEPTH, 2, MIX_W, MIX_W)
```

```python
import functools

import numpy as np
import jax
import jax.numpy as jnp
from jax import lax
from jax.experimental import pallas as pl
from jax.experimental.pallas import tpu as pltpu

F32 = jnp.float32
BF16 = jnp.bfloat16

D_MODEL = 1024
BATCH = 32
SEQ = 256
DEPTH = 2
DEC_BATCH = 2
DEC_SEQ = 1024
PAST_LEN = 512
GRID_W = 64
NORM_EPS = 1e-6

MLA_HEADS = 4
MLA_NOPE = 128
MLA_ROPE = 64
MLA_V = 128
Q_LORA = 384
KV_LORA = 256
ROPE_BASE = 10000.0
MLA_SCALE = (MLA_NOPE + MLA_ROPE) ** -0.5

HEADS = 4
HEAD_DIM = 64
MIX_W = HEADS * HEAD_DIM
GDN_CONV_CH = 3 * MIX_W
CHUNK = 64
RWKV_GN_EPS = 64e-5

N_EXPERTS = 64
TOP_K = 8
N_GROUPS = 8
GROUP_SIZE = N_EXPERTS // N_GROUPS
TOPK_GROUPS = 4
D_EXPERT = 256
ROUTE_SCALE = 2.5

P_MLA = Q_LORA + KV_LORA + MLA_ROPE
P_GDN = GDN_CONV_CH + MIX_W + 4 * HEADS
P_RWKV = 3 * MIX_W + 128 + 128 + 128

LANES = 128
ZM_W = Q_LORA + KV_LORA + 2 * LANES
ZG_W = GDN_CONV_CH + MIX_W
ZR_W = P_RWKV
QH_W = 2 * LANES
VMEM_LIMIT = 48 * 1024 * 1024

_ROPE_SWAP = np.concatenate([np.arange(16, 32), np.arange(0, 16), np.arange(48, 64), np.arange(32, 48)])


def _sigmoid(x):
    return 1.0 / (1.0 + jnp.exp(-x))


def _silu(x):
    return x * _sigmoid(x)


def _softplus(x):
    return jnp.maximum(x, 0.0) + jnp.log(1.0 + jnp.exp(-jnp.abs(x)))


def _rms(x, g, eps=NORM_EPS):
    return x * lax.rsqrt(jnp.mean(x * x, axis=-1, keepdims=True) + eps) * g


def _mm(a, b):
    return jnp.dot(a.astype(BF16), b.astype(BF16), preferred_element_type=F32)


def _mm_nt(a, b):
    return lax.dot_general(a.astype(BF16), b.astype(BF16), (((1,), (1,)), ((), ())),
                           preferred_element_type=F32)


def _mm_tn(a, b):
    return lax.dot_general(a.astype(BF16), b.astype(BF16), (((0,), (0,)), ((), ())),
                           preferred_element_type=F32)


def _split3(x):
    p1 = x.astype(BF16)
    r1 = x - p1.astype(F32)
    p2 = r1.astype(BF16)
    r2 = r1 - p2.astype(F32)
    return p1, p2, r2.astype(BF16)


def _mm_sel_l(sel, x):
    p1, p2, p3 = _split3(x)
    return _mm(sel, p1) + _mm(sel, p2) + _mm(sel, p3)


def _mm_sel_r(x, sel):
    p1, p2, p3 = _split3(x)
    return _mm(p1, sel) + _mm(p2, sel) + _mm(p3, sel)


def _iota(shape, dim):
    return lax.broadcasted_iota(jnp.int32, shape, dim)


def _bd(x, maskbd):
    xb = x.astype(BF16)
    return jnp.concatenate([xb] * HEADS, axis=0) * maskbd


def _chunk_masks(rev):
    row = _iota((CHUNK, MIX_W), 0)
    col = jnp.bitwise_and(_iota((CHUNK, MIX_W), 1), HEAD_DIM - 1)
    r2 = _iota((CHUNK, CHUNK), 0)
    c2 = _iota((CHUNK, CHUNK), 1)
    if rev:
        inc, strict, tri = row <= col, row < col, r2 <= c2
    else:
        inc, strict, tri = row >= col, row > col, r2 >= c2
    eye = jnp.where(row == col, 1.0, 0.0).astype(F32)
    return inc, strict, jnp.where(tri, 1.0, 0.0).astype(BF16), eye


def _neumann_inverse(a, eye, maskbd):
    b = -a
    m = eye + b
    p = _mm(b, _bd(b, maskbd))
    for _ in range(4):
        both = _mm(jnp.concatenate([m, p], axis=0), _bd(p, maskbd))
        m = m + both[:CHUNK]
        p = both[CHUNK:]
    return m + _mm(m, _bd(p, maskbd))


def _adaln_kernel(c_ref, w_ref, b_ref, o_ref):
    cv = c_ref[...]
    o_ref[0] = _mm(_silu(cv), w_ref[0]) + b_ref[0]


def _adaln(cvec8, ada_w, ada_b):
    tn = 768
    n_out = 6 * D_MODEL
    return pl.pallas_call(
        _adaln_kernel,
        grid=(DEPTH, n_out // tn),
        in_specs=[
            pl.BlockSpec((8, D_MODEL), lambda l, j: (0, 0)),
            pl.BlockSpec((1, D_MODEL, tn), lambda l, j: (l, 0, j)),
            pl.BlockSpec((1, 1, tn), lambda l, j: (l, 0, j)),
        ],
        out_specs=pl.BlockSpec((1, 8, tn), lambda l, j: (l, 0, j)),
        out_shape=jax.ShapeDtypeStruct((DEPTH, 8, n_out), F32),
        compiler_params=pltpu.CompilerParams(dimension_semantics=("arbitrary", "arbitrary"),
                                             vmem_limit_bytes=VMEM_LIMIT),
        name="adaln",
    )(cvec8, ada_w, ada_b.reshape(DEPTH, 1, n_out))


def _inproj_kernel(x_ref, mod_ref, n1_ref, w_ref, zm_ref, zg_ref, zab_ref, zr_ref):
    m = mod_ref[0]
    sh = m[:, 0:D_MODEL]
    sc = m[:, D_MODEL:2 * D_MODEL]
    h = _rms(x_ref[...], n1_ref[...]) * (1.0 + sc) + sh
    z = _mm(h, w_ref[...])
    o1 = ZM_W
    o2 = o1 + ZG_W
    o3 = o2 + LANES
    zm_ref[...] = z[:, :o1]
    zg_ref[...] = z[:, o1:o2]
    zab_ref[...] = z[:, o2:o3]
    zr_ref[...] = z[:, o3:]


def _inproj(x2d, mods, n1, w, tm, seq_len, mod_base):
    n = x2d.shape[0]
    wtot = w.shape[1]
    tiles_per_seq = seq_len // tm if mod_base else 1

    def mod_idx(i):
        return (mod_base + i // tiles_per_seq if mod_base else 0, 0, 0)

    return pl.pallas_call(
        _inproj_kernel,
        grid=(n // tm,),
        in_specs=[
            pl.BlockSpec((tm, D_MODEL), lambda i: (i, 0)),
            pl.BlockSpec((1, 1, 6 * D_MODEL), mod_idx),
            pl.BlockSpec((1, D_MODEL), lambda i: (0, 0)),
            pl.BlockSpec((D_MODEL, wtot), lambda i: (0, 0)),
        ],
        out_specs=[
            pl.BlockSpec((tm, ZM_W), lambda i: (i, 0)),
            pl.BlockSpec((tm, ZG_W), lambda i: (i, 0)),
            pl.BlockSpec((tm, LANES), lambda i: (i, 0)),
            pl.BlockSpec((tm, ZR_W), lambda i: (i, 0)),
        ],
        out_shape=[
            jax.ShapeDtypeStruct((n, ZM_W), F32),
            jax.ShapeDtypeStruct((n, ZG_W), F32),
            jax.ShapeDtypeStruct((n, LANES), F32),
            jax.ShapeDtypeStruct((n, ZR_W), F32),
        ],
        compiler_params=pltpu.CompilerParams(dimension_semantics=("arbitrary",),
                                             vmem_limit_bytes=VMEM_LIMIT),
        name="inproj",
    )(x2d, mods, n1, w)


def _mla_kernel(*refs, seq_len, tq, past, cached):
    if cached:
        (zm_ref, cckv_ref, ckpe_ref, rc_ref, rs_ref, qn_ref, wuq_ref, wuqs_ref, kvn_ref, wukv_ref,
         o_ref, ckv_ref, k_s, v_s) = refs
    else:
        (zm_ref, qn_ref, wuq_ref, kvn_ref, wukv_ref, o_ref, ckv_ref, k_s, v_s) = refs
    qi = pl.program_id(1)
    o_kpe = Q_LORA + KV_LORA

    @pl.when(qi == 0)
    def _():
        zm = zm_ref[...]
        ckv = _rms(zm[:, Q_LORA:o_kpe], kvn_ref[...])
        ckv_ref[...] = ckv
        kpe = zm[:, o_kpe:o_kpe + LANES]
        if cached:
            kpe = kpe * rc_ref[...] + zm[:, o_kpe + LANES:o_kpe + 2 * LANES] * rs_ref[...]
            kvc = _mm(cckv_ref[0], wukv_ref[...])
            kpc = ckpe_ref[0].astype(BF16)
        kv = _mm(ckv, wukv_ref[...])
        kpe = kpe.astype(BF16)
        for h in range(MLA_HEADS):
            c0 = h * QH_W
            if cached:
                k_s[0:past, c0:c0 + LANES] = kvc[:, c0:c0 + LANES].astype(BF16)
                k_s[0:past, c0 + LANES:c0 + QH_W] = kpc
                v_s[0:past, h * MLA_V:(h + 1) * MLA_V] = kvc[:, c0 + LANES:c0 + QH_W].astype(BF16)
            k_s[past:past + seq_len, c0:c0 + LANES] = kv[:, c0:c0 + LANES].astype(BF16)
            k_s[past:past + seq_len, c0 + LANES:c0 + QH_W] = kpe
            v_s[past:past + seq_len, h * MLA_V:(h + 1) * MLA_V] = kv[:, c0 + LANES:c0 + QH_W].astype(BF16)

    r0 = pl.multiple_of(qi * tq, tq)
    zq = zm_ref[pl.ds(r0, tq), :]
    cq = _rms(zq[:, :Q_LORA], qn_ref[...])
    q = _mm(cq, wuq_ref[...])
    if cached:
        qs = _mm(cq, wuqs_ref[...])
        ones = jnp.ones((tq, LANES), F32)
        zeros = jnp.zeros((tq, LANES), F32)
        qc = jnp.concatenate([ones, rc_ref[pl.ds(r0, tq), :]], axis=1)
        qsn = jnp.concatenate([zeros, rs_ref[pl.ds(r0, tq), :]], axis=1)
    for h in range(MLA_HEADS):
        c0 = h * QH_W
        qh = q[:, c0:c0 + QH_W]
        if cached:
            qh = qh * qc + qs[:, c0:c0 + QH_W] * qsn
        s = _mm_nt(qh, k_s[:, c0:c0 + QH_W]) * MLA_SCALE
        e = jnp.exp(s - jnp.max(s, axis=-1, keepdims=True))
        den = jnp.sum(e, axis=-1, keepdims=True)
        o_ref[:, h * MLA_V:(h + 1) * MLA_V] = _mm(e, v_s[:, h * MLA_V:(h + 1) * MLA_V]) / den


def _mla(zm, lw, seq_len, tq, cache=None):
    n = zm.shape[0]
    nb = n // seq_len
    cached = cache is not None
    past = PAST_LEN if cached else 0
    tk = past + seq_len
    seq_spec = lambda w: pl.BlockSpec((seq_len, w), lambda b, q: (b, 0))
    full = lambda a: pl.BlockSpec(a.shape, lambda b, q: (0,) * a.ndim)
    if cached:
        cckv, ckpe, rc, rs = cache
        args = [zm, cckv, ckpe, rc, rs, lw["q_norm"], lw["w_uq"], lw["w_uq_sw"], lw["kv_norm"], lw["w_ukv"]]
        in_specs = [seq_spec(ZM_W),
                    pl.BlockSpec((1, past, KV_LORA), lambda b, q: (b, 0, 0)),
                    pl.BlockSpec((1, past, LANES), lambda b, q: (b, 0, 0)),
                    full(rc), full(rs)] + [full(a) for a in args[5:]]
    else:
        args = [zm, lw["q_norm"], lw["w_uq"], lw["kv_norm"], lw["w_ukv"]]
        in_specs = [seq_spec(ZM_W)] + [full(a) for a in args[1:]]
    return pl.pallas_call(
        functools.partial(_mla_kernel, seq_len=seq_len, tq=tq, past=past, cached=cached),
        grid=(nb, seq_len // tq),
        in_specs=in_specs,
        out_specs=[
            pl.BlockSpec((tq, MLA_HEADS * MLA_V), lambda b, q: (b * (seq_len // tq) + q, 0)),
            pl.BlockSpec((seq_len, KV_LORA), lambda b, q: (b, 0)),
        ],
        out_shape=[
            jax.ShapeDtypeStruct((n, MLA_HEADS * MLA_V), F32),
            jax.ShapeDtypeStruct((n, KV_LORA), F32),
        ],
        scratch_shapes=[
            pltpu.VMEM((tk, MLA_HEADS * QH_W), BF16),
            pltpu.VMEM((tk, MLA_HEADS * MLA_V), BF16),
        ],
        compiler_params=pltpu.CompilerParams(dimension_semantics=("arbitrary", "arbitrary"),
                                             vmem_limit_bytes=VMEM_LIMIT),
        name="mla_lat" if cached else "mla_ctx",
    )(*args)


def _gdn_chunk(q, k, v, g, beta, s_bd, maskbd, masks, rev):
    inc, strict, tri, eye = masks
    gc = _mm_sel_l(tri, g)
    gc_row = jnp.sum(eye * gc, axis=0, keepdims=True)
    decay = jnp.where(inc, jnp.exp(jnp.where(inc, gc - gc_row, 0.0)), 0.0)
    kb = k * beta
    aq = _mm_nt(jnp.concatenate([kb, q], axis=0), _bd(k, maskbd))
    a_mat = jnp.where(strict, aq[:CHUNK] * decay, 0.0)
    qk = aq[CHUNK:] * decay
    t_inv = _neumann_inverse(a_mat, eye, maskbd)
    egc = jnp.exp(gc)
    u = _mm(t_inv, _bd(v * beta, maskbd))
    w = _mm(t_inv, _bd(kb * egc, maskbd))
    wq = _mm(jnp.concatenate([w, q * egc], axis=0), s_bd)
    v_new = u - wq[:CHUNK]
    o = wq[CHUNK:] + _mm(qk, _bd(v_new, maskbd))
    g_last = gc[0:1] if rev else gc[CHUNK - 1:CHUNK]
    k_dec = k * jnp.exp(g_last - gc)
    s_new = s_bd * jnp.exp(g_last) + _mm_tn(k_dec, v_new) * maskbd.astype(F32)
    return o, s_new


def _gdn_kernel(*refs, seq_len, cached):
    if cached:
        (zg_ref, zab_ref, s0_ref, conv_ref, alog_ref, dtb_ref, gn_ref, eexp_ref, ones_ref, maskbd_ref,
         o_ref, sout_ref, q_s, k_s, v_s, ge_s, oacc_s, st_s) = refs
    else:
        (zg_ref, zab_ref, conv_ref, alog_ref, dtb_ref, gn_ref, eexp_ref, ones_ref, maskbd_ref,
         o_ref, sout_ref, q_s, k_s, v_s, ge_s, oacc_s, st_s) = refs
    t = seq_len
    z = zg_ref[:, :GDN_CONV_CH]
    rowi = _iota((t, 1), 0)
    zp = jnp.where(rowi == 0, 0.0, pltpu.roll(z, 1, 0))
    zn = jnp.where(rowi == t - 1, 0.0, pltpu.roll(z, t - 1, 0))
    cw = conv_ref[...]
    qkv = _silu(zp * cw[0:1] + z * cw[1:2] + zn * cw[2:3])
    ones_bd = ones_ref[...]
    q = qkv[:, :MIX_W]
    k = qkv[:, MIX_W:2 * MIX_W]
    q_s[...] = q * lax.rsqrt(_mm_sel_r(q * q, ones_bd) + 1e-6) * (HEAD_DIM ** -0.5)
    k_s[...] = k * lax.rsqrt(_mm_sel_r(k * k, ones_bd) + 1e-6)
    v_s[...] = qkv[:, 2 * MIX_W:]
    ab = zab_ref[...]
    lane = _iota((t, LANES), 1)
    gb = jnp.where(lane < 2 * HEADS, -jnp.exp(alog_ref[...]) * _softplus(ab + dtb_ref[...]), _sigmoid(ab))
    ge_s[...] = _mm_sel_r(gb, eexp_ref[...])
    oacc_s[...] = jnp.zeros((t, MIX_W), F32)
    if cached:
        st_s[...] = s0_ref[0]
    else:
        st_s[...] = jnp.zeros((2, MIX_W, MIX_W), F32)
    maskbd = maskbd_ref[...]
    masks = (_chunk_masks(False), _chunk_masks(True))
    n_chunks = t // CHUNK

    def body(i, carry):
        for d in range(2):
            c = i if d == 0 else n_chunks - 1 - i
            rows = pl.ds(pl.multiple_of(c * CHUNK, CHUNK), CHUNK)
            o, s_new = _gdn_chunk(q_s[rows, :], k_s[rows, :], v_s[rows, :],
                                  ge_s[rows, d * MIX_W:(d + 1) * MIX_W],
                                  ge_s[rows, (2 + d) * MIX_W:(3 + d) * MIX_W],
                                  st_s[d], maskbd, masks[d], rev=(d == 1))
            oacc_s[rows, :] = oacc_s[rows, :] + o
            st_s[d] = s_new
        return carry

    lax.fori_loop(0, n_chunks, body, 0)
    o = oacc_s[...]
    ms = _mm_sel_r(o * o, ones_bd) * (1.0 / HEAD_DIM)
    gate = zg_ref[:, GDN_CONV_CH:]
    o_ref[...] = o * lax.rsqrt(ms + NORM_EPS) * gn_ref[...] * _silu(gate)
    for d in range(2):
        for h in range(HEADS):
            sl = slice(h * HEAD_DIM, (h + 1) * HEAD_DIM)
            sout_ref[0, d, h] = st_s[d, sl, sl]


def _gdn(zg, zab, lw, consts, seq_len, s0_bd=None):
    n = zg.shape[0]
    nb = n // seq_len
    cached = s0_bd is not None
    full = lambda a: pl.BlockSpec(a.shape, lambda b: (0,) * a.ndim)
    args = [zg, zab]
    in_specs = [pl.BlockSpec((seq_len, ZG_W), lambda b: (b, 0)),
                pl.BlockSpec((seq_len, LANES), lambda b: (b, 0))]
    if cached:
        args.append(s0_bd)
        in_specs.append(pl.BlockSpec((1, 2, MIX_W, MIX_W), lambda b: (b, 0, 0, 0)))
    tail = [lw["gdn_conv"], lw["gdn_alog"], lw["gdn_dtb"], lw["gdn_norm"], consts["eexp"], consts["ones_bd"],
            consts["maskbd"]]
    args += tail
    in_specs += [full(a) for a in tail]
    return pl.pallas_call(
        functools.partial(_gdn_kernel, seq_len=seq_len, cached=cached),
        grid=(nb,),
        in_specs=in_specs,
        out_specs=[
            pl.BlockSpec((seq_len, MIX_W), lambda b: (b, 0)),
            pl.BlockSpec((1, 2, HEADS, HEAD_DIM, HEAD_DIM), lambda b: (b, 0, 0, 0, 0)),
        ],
        out_shape=[
            jax.ShapeDtypeStruct((n, MIX_W), F32),
            jax.ShapeDtypeStruct((nb, 2, HEADS, HEAD_DIM, HEAD_DIM), F32),
        ],
        scratch_shapes=[
            pltpu.VMEM((seq_len, MIX_W), F32),
            pltpu.VMEM((seq_len, MIX_W), F32),
            pltpu.VMEM((seq_len, MIX_W), F32),
            pltpu.VMEM((seq_len, 4 * MIX_W), F32),
            pltpu.VMEM((seq_len, MIX_W), F32),
            pltpu.VMEM((2, MIX_W, MIX_W), F32),
        ],
        compiler_params=pltpu.CompilerParams(dimension_semantics=("arbitrary",),
                                             vmem_limit_bytes=VMEM_LIMIT),
        name="gdn_lat" if cached else "gdn_ctx",
    )(*args)


def _rwkv_chunk(r, kd, v, kk, b, lw, s_bd, maskbd, masks, rev):
    inc, strict, tri, eye = masks
    cum = _mm_sel_l(tri, lw)
    e = jnp.exp(cum)
    einv = jnp.exp(-cum)
    kr = jnp.concatenate([kk * jnp.exp(cum - lw), r * e], axis=0)
    lb_all = _mm_nt(kr, _bd(b * einv, maskbd))
    lk_all = _mm_nt(kr, _bd(kd * einv, maskbd))
    lb = jnp.where(strict, lb_all[:CHUNK], 0.0)
    rb = jnp.where(inc, lb_all[CHUNK:], 0.0)
    lk = jnp.where(strict, lk_all[:CHUNK], 0.0)
    rk = jnp.where(inc, lk_all[CHUNK:], 0.0)
    t_inv = _neumann_inverse(lb, eye, maskbd)
    ks_rs = _mm_nt(kr, s_bd)
    lv = _mm(jnp.concatenate([lk, rk], axis=0), _bd(v, maskbd))
    p = _mm(t_inv, _bd(ks_rs[:CHUNK] + lv[:CHUNK], maskbd))
    o = ks_rs[CHUNK:] + lv[CHUNK:] - _mm(rb, _bd(p, maskbd))
    c_last = cum[0:1] if rev else cum[CHUNK - 1:CHUNK]
    tail = jnp.exp(c_last - cum)
    upd = _mm_tn(jnp.concatenate([v, -p], axis=0), jnp.concatenate([kd * tail, b * tail], axis=0))
    s_new = s_bd * jnp.exp(c_last) + upd * maskbd.astype(F32)
    return o, s_new


def _rwkv_kernel(*refs, seq_len, cached):
    if cached:
        (zr_ref, s0_ref, mup_ref, mun_ref, w0_ref, w2_ref, a0_ref, a2_ref, g2_ref, kk_ref, ka_ref, rk_ref,
         gnw_ref, gnb_ref, ones_ref, maskbd_ref, o_ref, sout_ref,
         r_s, v_s, kk_s, dir_s, bg_s, oacc_s, st_s) = refs
    else:
        (zr_ref, mup_ref, mun_ref, w0_ref, w2_ref, a0_ref, a2_ref, g2_ref, kk_ref, ka_ref, rk_ref,
         gnw_ref, gnb_ref, ones_ref, maskbd_ref, o_ref, sout_ref,
         r_s, v_s, kk_s, dir_s, bg_s, oacc_s, st_s) = refs
    t = seq_len
    z = zr_ref[...]
    rowi = _iota((t, 1), 0)
    zp = jnp.where(rowi == 0, 0.0, pltpu.roll(z, 1, 0))
    zn = jnp.where(rowi == t - 1, 0.0, pltpu.roll(z, t - 1, 0))
    z = z + mup_ref[...] * (zp - z) + mun_ref[...] * (zn - z)
    w = MIX_W
    r = z[:, :w]
    k = z[:, w:2 * w]
    v = z[:, 2 * w:3 * w]
    wd = jnp.tanh(z[:, 3 * w:3 * w + LANES])
    ad = z[:, 3 * w + LANES:3 * w + 2 * LANES]
    gd = _sigmoid(z[:, 3 * w + 2 * LANES:])
    ones_bd = ones_ref[...]
    kk = k * kk_ref[...]
    kk = kk * lax.rsqrt(_mm_sel_r(kk * kk, ones_bd) + 1e-6)
    r_s[...] = r
    v_s[...] = v
    kk_s[...] = kk
    bonus = jnp.zeros((t, w), F32)
    for d in range(2):
        w_log = -_softplus(-(w0_ref[d:d + 1] + _mm(wd, w2_ref[d]))) - 0.5
        a = _sigmoid(a0_ref[d:d + 1] + _mm(ad, a2_ref[d]))
        kd = k * (1.0 + (a - 1.0) * ka_ref[...])
        dir_s[:, (3 * d) * w:(3 * d + 1) * w] = -jnp.exp(w_log)
        dir_s[:, (3 * d + 1) * w:(3 * d + 2) * w] = kd
        dir_s[:, (3 * d + 2) * w:(3 * d + 3) * w] = kk * a
        bonus = bonus + _mm_sel_r(r * kd * rk_ref[...], ones_bd) * v
    bg_s[:, :w] = bonus
    bg_s[:, w:] = _mm(gd, g2_ref[...])
    oacc_s[...] = jnp.zeros((t, w), F32)
    if cached:
        st_s[...] = s0_ref[0]
    else:
        st_s[...] = jnp.zeros((2, w, w), F32)
    maskbd = maskbd_ref[...]
    masks = (_chunk_masks(False), _chunk_masks(True))
    n_chunks = t // CHUNK

    def body(i, carry):
        for d in range(2):
            c = i if d == 0 else n_chunks - 1 - i
            rows = pl.ds(pl.multiple_of(c * CHUNK, CHUNK), CHUNK)
            o, s_new = _rwkv_chunk(r_s[rows, :], dir_s[rows, (3 * d + 1) * w:(3 * d + 2) * w], v_s[rows, :],
                                   kk_s[rows, :], dir_s[rows, (3 * d + 2) * w:(3 * d + 3) * w],
                                   dir_s[rows, (3 * d) * w:(3 * d + 1) * w],
                                   st_s[d], maskbd, masks[d], rev=(d == 1))
            oacc_s[rows, :] = oacc_s[rows, :] + o
            st_s[d] = s_new
        return carry

    lax.fori_loop(0, n_chunks, body, 0)
    o = oacc_s[...]
    inv_n = 1.0 / HEAD_DIM
    mu = _mm_sel_r(o, ones_bd) * inv_n
    oc = o - mu
    var = _mm_sel_r(oc * oc, ones_bd) * inv_n
    y = oc * lax.rsqrt(var + RWKV_GN_EPS) * gnw_ref[...] + gnb_ref[...]
    o_ref[...] = (y + bg_s[:, :w]) * bg_s[:, w:]
    for d in range(2):
        for h in range(HEADS):
            sl = slice(h * HEAD_DIM, (h + 1) * HEAD_DIM)
            sout_ref[0, d, h] = st_s[d, sl, sl]


def _rwkv(zr, lw, consts, seq_len, s0_bd=None):
    n = zr.shape[0]
    nb = n // seq_len
    cached = s0_bd is not None
    full = lambda a: pl.BlockSpec(a.shape, lambda b: (0,) * a.ndim)
    args = [zr]
    in_specs = [pl.BlockSpec((seq_len, ZR_W), lambda b: (b, 0))]
    if cached:
        args.append(s0_bd)
        in_specs.append(pl.BlockSpec((1, 2, MIX_W, MIX_W), lambda b: (b, 0, 0, 0)))
    tail = [lw["rwkv_mu_prev"], lw["rwkv_mu_next"], lw["rwkv_w0"], lw["rwkv_w2"], lw["rwkv_a0"], lw["rwkv_a2"],
            lw["rwkv_g2"], lw["rwkv_k_k"], lw["rwkv_k_a"], lw["rwkv_r_k"], lw["rwkv_gn_w"], lw["rwkv_gn_b"],
            consts["ones_bd"], consts["maskbd"]]
    args += tail
    in_specs += [full(a) for a in tail]
    return pl.pallas_call(
        functools.partial(_rwkv_kernel, seq_len=seq_len, cached=cached),
        grid=(nb,),
        in_specs=in_specs,
        out_specs=[
            pl.BlockSpec((seq_len, MIX_W), lambda b: (b, 0)),
            pl.BlockSpec((1, 2, HEADS, HEAD_DIM, HEAD_DIM), lambda b: (b, 0, 0, 0, 0)),
        ],
        out_shape=[
            jax.ShapeDtypeStruct((n, MIX_W), F32),
            jax.ShapeDtypeStruct((nb, 2, HEADS, HEAD_DIM, HEAD_DIM), F32),
        ],
        scratch_shapes=[
            pltpu.VMEM((seq_len, MIX_W), F32),
            pltpu.VMEM((seq_len, MIX_W), F32),
            pltpu.VMEM((seq_len, MIX_W), F32),
            pltpu.VMEM((seq_len, 6 * MIX_W), F32),
            pltpu.VMEM((seq_len, 2 * MIX_W), F32),
            pltpu.VMEM((seq_len, MIX_W), F32),
            pltpu.VMEM((2, MIX_W, MIX_W), F32),
        ],
        compiler_params=pltpu.CompilerParams(dimension_semantics=("arbitrary",),
                                             vmem_limit_bytes=VMEM_LIMIT),
        name="rwkv_lat" if cached else "rwkv_ctx",
    )(*args)


def _split2(x):
    hi = x.astype(BF16)
    return hi, (x - hi.astype(F32)).astype(BF16)


def _route(logits_t, bias):
    tm = logits_t.shape[1]
    neg = -jnp.inf
    sc = _sigmoid(logits_t)
    sel = (sc + bias).reshape(N_GROUPS, GROUP_SIZE, tm)
    si = _iota(sel.shape, 1).astype(F32)
    m1 = jnp.max(sel, axis=1, keepdims=True)
    f1 = jnp.min(jnp.where(sel == m1, si, float(GROUP_SIZE)), axis=1, keepdims=True)
    m2 = jnp.max(jnp.where(si == f1, neg, sel), axis=1, keepdims=True)
    grp = m1 + m2
    gi = _iota(grp.shape, 0).astype(F32)
    gsel = jnp.zeros(grp.shape, F32)
    for _ in range(TOPK_GROUPS):
        mx = jnp.max(grp, axis=0, keepdims=True)
        fi = jnp.min(jnp.where(grp == mx, gi, float(N_GROUPS)), axis=0, keepdims=True)
        hit = gi == fi
        gsel = jnp.where(hit, 1.0, gsel)
        grp = jnp.where(hit, neg, grp)
    cur = jnp.where(gsel > 0.0, sel, neg)
    ei = (_iota(cur.shape, 0) * GROUP_SIZE + _iota(cur.shape, 1)).astype(F32)
    chosen = jnp.zeros(cur.shape, F32)
    for _ in range(TOP_K):
        mx = jnp.max(jnp.max(cur, axis=0, keepdims=True), axis=1, keepdims=True)
        fi = jnp.min(jnp.min(jnp.where(cur == mx, ei, float(N_EXPERTS)), axis=0, keepdims=True),
                     axis=1, keepdims=True)
        hit = ei == fi
        chosen = jnp.where(hit, 1.0, chosen)
        cur = jnp.where(hit, neg, cur)
    wsel = chosen.reshape(N_EXPERTS, tm) * sc
    return wsel / jnp.sum(wsel, axis=0, keepdims=True) * ROUTE_SCALE


def _post_kernel(x_ref, om_ref, og_ref, or_ref, mod_ref, wo_ref, n2_ref, rt_ref, rb_ref,
                 x1_ref, h2_ref, cw_ref):
    m = mod_ref[0]
    g1 = m[:, 2 * D_MODEL:3 * D_MODEL]
    sh2 = m[:, 3 * D_MODEL:4 * D_MODEL]
    sc2 = m[:, 4 * D_MODEL:5 * D_MODEL]
    w_mla = MLA_HEADS * MLA_V
    mix = (_mm(om_ref[...], wo_ref[0:w_mla, :]) + _mm(og_ref[...], wo_ref[w_mla:w_mla + MIX_W, :])
           + _mm(or_ref[...], wo_ref[w_mla + MIX_W:, :]))
    x1 = x_ref[...] + g1 * mix
    x1_ref[...] = x1
    h2 = _rms(x1, n2_ref[...]) * (1.0 + sc2) + sh2
    h2_ref[...] = h2.astype(BF16)
    r_hi, r_lo = _split2(rt_ref[...])
    h_hi, h_lo = _split2(h2)
    logits_t = _mm_nt(r_hi, h_hi) + _mm_nt(r_hi, h_lo) + _mm_nt(r_lo, h_hi)
    cw_t = _route(logits_t, rb_ref[...])
    cw_ref[...] = jnp.concatenate([cw_t, jnp.zeros_like(cw_t)], axis=0).T


def _post(x2d, om, og, orw, mods, lw, tm, seq_len, mod_base):
    n = x2d.shape[0]
    tiles_per_seq = seq_len // tm if mod_base else 1

    def mod_idx(i):
        return (mod_base + i // tiles_per_seq if mod_base else 0, 0, 0)

    row = lambda w: pl.BlockSpec((tm, w), lambda i: (i, 0))
    full = lambda a: pl.BlockSpec(a.shape, lambda i: (0,) * a.ndim)
    tail = [lw["w_out"], lw["norm2"], lw["router_t"], lw["router_b"]]
    return pl.pallas_call(
        _post_kernel,
        grid=(n // tm,),
        in_specs=[row(D_MODEL), row(MLA_HEADS * MLA_V), row(MIX_W), row(MIX_W),
                  pl.BlockSpec((1, 1, 6 * D_MODEL), mod_idx)] + [full(a) for a in tail],
        out_specs=[row(D_MODEL), row(D_MODEL), row(LANES)],
        out_shape=[
            jax.ShapeDtypeStruct((n, D_MODEL), F32),
            jax.ShapeDtypeStruct((n, D_MODEL), BF16),
            jax.ShapeDtypeStruct((n, LANES), F32),
        ],
        compiler_params=pltpu.CompilerParams(dimension_semantics=("arbitrary",),
                                             vmem_limit_bytes=VMEM_LIMIT),
        name="post",
    )(x2d, om, og, orw, mods, *tail)


def _swiglu_act(gu):
    return _silu(gu[:, :D_EXPERT]) * gu[:, D_EXPERT:]


MOE_EXPERTS_PER_STEP = 2


def _moe_kernel(h_ref, x1_ref, cw_ref, mod_ref, wgu_ref, wdn_ref, sgu_ref, sdn_ref, nf_ref, o_ref, *, final):
    step = pl.program_id(1)
    h = h_ref[...]

    @pl.when(step == 0)
    def _():
        o_ref[...] = _mm(_swiglu_act(_mm(h, sgu_ref[...])), sdn_ref[...])

    cw = cw_ref[...]
    lane = _iota(cw.shape, 1)
    acts = []
    for g in range(MOE_EXPERTS_PER_STEP):
        act = _swiglu_act(_mm(h, wgu_ref[g]))
        scale = jnp.sum(jnp.where(lane == step * MOE_EXPERTS_PER_STEP + g, cw, 0.0), axis=-1, keepdims=True)
        acts.append((act * scale).astype(BF16))
    wdn = wdn_ref[...].reshape(MOE_EXPERTS_PER_STEP * D_EXPERT, D_MODEL)
    o_ref[...] += _mm(jnp.concatenate(acts, axis=1), wdn)

    @pl.when(step == N_EXPERTS // MOE_EXPERTS_PER_STEP - 1)
    def _():
        g2 = mod_ref[0][:, 5 * D_MODEL:]
        x2 = x1_ref[...] + g2 * o_ref[...]
        if final:
            x2 = _rms(x2, nf_ref[...])
        o_ref[...] = x2


def _moe(h2, x1, cw, mods, lw, norm_f, tm, seq_len, mod_base, final):
    n = h2.shape[0]
    tiles_per_seq = seq_len // tm if mod_base else 1
    g = MOE_EXPERTS_PER_STEP

    def mod_idx(i, e):
        return (mod_base + i // tiles_per_seq if mod_base else 0, 0, 0)

    row = lambda w, **kw: pl.BlockSpec((tm, w), lambda i, e: (i, 0), **kw)
    full = lambda a: pl.BlockSpec(a.shape, lambda i, e: (0,) * a.ndim, pipeline_mode=pl.Buffered(1))
    return pl.pallas_call(
        functools.partial(_moe_kernel, final=final),
        grid=(n // tm, N_EXPERTS // g),
        in_specs=[row(D_MODEL), row(D_MODEL, pipeline_mode=pl.Buffered(1)), row(LANES),
                  pl.BlockSpec((1, 1, 6 * D_MODEL), mod_idx),
                  pl.BlockSpec((g, D_MODEL, 2 * D_EXPERT), lambda i, e: (e, 0, 0)),
                  pl.BlockSpec((g, D_EXPERT, D_MODEL), lambda i, e: (e, 0, 0)),
                  full(lw["shared_w_gu"]), full(lw["shared_w_down"]), full(norm_f)],
        out_specs=row(D_MODEL),
        out_shape=jax.ShapeDtypeStruct((n, D_MODEL), F32),
        compiler_params=pltpu.CompilerParams(dimension_semantics=("arbitrary", "arbitrary"),
                                             vmem_limit_bytes=VMEM_LIMIT),
        name="moe_final" if final else "moe",
    )(h2, x1, cw, mods, lw["moe_w_gu"], lw["moe_w_down"], lw["shared_w_gu"], lw["shared_w_down"], norm_f)


def _constants():
    idx = np.arange(MIX_W)
    same_head = (idx[:, None] // HEAD_DIM) == (idx[None, :] // HEAD_DIM)
    maskbd = jnp.asarray(same_head, BF16)
    eexp = np.zeros((LANES, 4 * MIX_W), np.float32)
    for blk in range(4):
        kind, d = divmod(blk, 2)
        for h in range(HEADS):
            src = kind * 2 * HEADS + d * HEADS + h
            eexp[src, blk * MIX_W + h * HEAD_DIM: blk * MIX_W + (h + 1) * HEAD_DIM] = 1.0
    return {"maskbd": maskbd, "ones_bd": maskbd, "eexp": jnp.asarray(eexp, BF16)}


def _rope_tables(n):
    rows = n // GRID_W
    row = jnp.repeat(jnp.arange(rows, dtype=F32), GRID_W)
    col = jnp.tile(jnp.arange(GRID_W, dtype=F32), rows)
    axis_dim = MLA_ROPE // 2
    inv = jnp.power(ROPE_BASE, -jnp.arange(0, axis_dim, 2, dtype=F32) / axis_dim)
    ang_r = row[:, None] * inv
    ang_c = col[:, None] * inv
    cr, sr, cc, sc = jnp.cos(ang_r), jnp.sin(ang_r), jnp.cos(ang_c), jnp.sin(ang_c)
    zeros = jnp.zeros((n, LANES - MLA_ROPE), F32)
    cos_t = jnp.concatenate([cr, cr, cc, cc, zeros], axis=1)
    sin_t = jnp.concatenate([-sr, sr, -sc, sc, zeros], axis=1)
    return cos_t, sin_t


def _pad_lanes(v, width):
    v = v.reshape(1, -1)
    return jnp.pad(v, ((0, 0), (0, width - v.shape[1])))


def _layer_weights(p, l):
    w_in = p["w_in"][l]
    kpe0 = Q_LORA + KV_LORA
    zc = jnp.zeros((D_MODEL, LANES - MLA_ROPE), F32)
    w_in_p = jnp.concatenate([
        w_in[:, :P_MLA], zc, w_in[:, kpe0 + _ROPE_SWAP], zc,
        w_in[:, P_MLA:P_MLA + ZG_W],
        w_in[:, P_MLA + ZG_W:P_MLA + P_GDN], jnp.zeros((D_MODEL, LANES - 4 * HEADS), F32),
        w_in[:, P_MLA + P_GDN:]], axis=1).astype(BF16)
    w_uq = p["mla_w_uq"][l].reshape(Q_LORA, MLA_HEADS, MLA_NOPE + MLA_ROPE)
    zq = jnp.zeros((Q_LORA, MLA_HEADS, QH_W - MLA_NOPE - MLA_ROPE), F32)
    w_uq_a = jnp.concatenate([w_uq, zq], axis=2).reshape(Q_LORA, MLA_HEADS * QH_W).astype(BF16)
    w_uq_sw = jnp.concatenate([jnp.zeros((Q_LORA, MLA_HEADS, MLA_NOPE), F32),
                               w_uq[:, :, MLA_NOPE + _ROPE_SWAP], zq], axis=2)
    w_uq_sw = w_uq_sw.reshape(Q_LORA, MLA_HEADS * QH_W).astype(BF16)
    half = jnp.zeros((64, MIX_W), F32)
    w2 = p["rwkv_w2"][l]
    a2 = p["rwkv_a2"][l]
    tile4 = lambda v: jnp.tile(v.reshape(1, HEAD_DIM), (1, HEADS))
    return {
        "norm1": p["norm1"][l].reshape(1, D_MODEL),
        "w_in": w_in_p,
        "q_norm": p["mla_q_norm"][l].reshape(1, Q_LORA),
        "w_uq": w_uq_a, "w_uq_sw": w_uq_sw,
        "kv_norm": p["mla_kv_norm"][l].reshape(1, KV_LORA),
        "w_ukv": p["mla_w_ukv"][l].astype(BF16),
        "gdn_conv": p["gdn_conv"][l],
        "gdn_alog": _pad_lanes(p["gdn_a_log"][l], LANES),
        "gdn_dtb": _pad_lanes(p["gdn_dt_bias"][l], LANES),
        "gdn_norm": tile4(p["gdn_norm"][l]),
        "rwkv_mu_prev": p["rwkv_mu_prev"][l].reshape(1, ZR_W),
        "rwkv_mu_next": p["rwkv_mu_next"][l].reshape(1, ZR_W),
        "rwkv_w0": p["rwkv_w0"][l],
        "rwkv_w2": jnp.stack([jnp.concatenate([w2[0], half]), jnp.concatenate([half, w2[1]])]).astype(BF16),
        "rwkv_a0": p["rwkv_a0"][l],
        "rwkv_a2": jnp.stack([jnp.concatenate([a2[0], half]), jnp.concatenate([half, a2[1]])]).astype(BF16),
        "rwkv_g2": p["rwkv_g2"][l].astype(BF16),
        "rwkv_k_k": p["rwkv_k_k"][l].reshape(1, MIX_W),
        "rwkv_k_a": p["rwkv_k_a"][l].reshape(1, MIX_W),
        "rwkv_r_k": p["rwkv_r_k"][l].reshape(1, MIX_W),
        "rwkv_gn_w": p["rwkv_gn_w"][l].reshape(1, MIX_W),
        "rwkv_gn_b": p["rwkv_gn_b"][l].reshape(1, MIX_W),
        "w_out": p["w_out"][l].astype(BF16),
        "norm2": p["norm2"][l].reshape(1, D_MODEL),
        "router_t": p["moe_router"][l].T,
        "router_b": p["moe_bias"][l].reshape(N_EXPERTS, 1),
        "moe_w_gu": p["moe_w_gu"][l],
        "moe_w_down": p["moe_w_down"][l],
        "shared_w_gu": p["shared_w_gu"][l].astype(BF16),
        "shared_w_down": p["shared_w_down"][l].astype(BF16),
    }


def _embed_block_diag(s):
    b = s.shape[0]
    eye = jnp.eye(HEADS, dtype=s.dtype)
    out = jnp.einsum("bdhkv,hg->bdhkgv", s, eye)
    return out.reshape(b, 2, MIX_W, MIX_W)


def _trunk(x2d, mods, weights, consts, norm_f, seq_len, mod_base, cache, tm, tq, tm_moe):
    outs = []
    for l in range(DEPTH):
        lw = weights[l]
        ml = mods[l].reshape(8, 1, 6 * D_MODEL)
        zm, zg, zab, zr = _inproj(x2d, ml, lw["norm1"], lw["w_in"], tm, seq_len, mod_base)
        if cache is None:
            o_mla, ckv = _mla(zm, lw, seq_len, tq)
            o_gdn, s_gdn = _gdn(zg, zab, lw, consts, seq_len)
            o_rwkv, s_rwkv = _rwkv(zr, lw, consts, seq_len)
        else:
            cckv, ckpe, rc, rs, sg, sr = cache
            o_mla, ckv = _mla(zm, lw, seq_len, tq, (cckv[:, l], ckpe[:, l], rc, rs))
            o_gdn, s_gdn = _gdn(zg, zab, lw, consts, seq_len, sg[:, l])
            o_rwkv, s_rwkv = _rwkv(zr, lw, consts, seq_len, sr[:, l])
        x1, h2, cw = _post(x2d, o_mla, o_gdn, o_rwkv, ml, lw, tm, seq_len, mod_base)
        x2d = _moe(h2, x1, cw, ml, lw, norm_f, tm_moe, seq_len, mod_base, final=(l == DEPTH - 1))
        kpe0 = Q_LORA + KV_LORA
        outs.append((ckv, zm[:, kpe0:kpe0 + MLA_ROPE], s_gdn, s_rwkv))
    return x2d, outs


def kernel(x_prompt, x_sample, cache_mla_ckv, cache_mla_kpe, state_gdn, state_rwkv, c, c_ctx, ada_w, ada_b, norm1, w_in, mla_q_norm, mla_w_uq, mla_kv_norm, mla_w_ukv, gdn_conv, gdn_a_log, gdn_dt_bias, gdn_norm, rwkv_mu_prev, rwkv_mu_next, rwkv_w0, rwkv_w2, rwkv_a0, rwkv_a2, rwkv_g2, rwkv_k_k, rwkv_k_a, rwkv_r_k, rwkv_gn_w, rwkv_gn_b, w_out, norm2, moe_router, moe_bias, moe_w_gu, moe_w_down, shared_w_gu, shared_w_down, norm_f):
    p = dict(norm1=norm1, w_in=w_in, mla_q_norm=mla_q_norm, mla_w_uq=mla_w_uq, mla_kv_norm=mla_kv_norm,
             mla_w_ukv=mla_w_ukv, gdn_conv=gdn_conv, gdn_a_log=gdn_a_log, gdn_dt_bias=gdn_dt_bias,
             gdn_norm=gdn_norm, rwkv_mu_prev=rwkv_mu_prev, rwkv_mu_next=rwkv_mu_next, rwkv_w0=rwkv_w0,
             rwkv_w2=rwkv_w2, rwkv_a0=rwkv_a0, rwkv_a2=rwkv_a2, rwkv_g2=rwkv_g2, rwkv_k_k=rwkv_k_k,
             rwkv_k_a=rwkv_k_a, rwkv_r_k=rwkv_r_k, rwkv_gn_w=rwkv_gn_w, rwkv_gn_b=rwkv_gn_b, w_out=w_out,
             norm2=norm2, moe_router=moe_router, moe_bias=moe_bias, moe_w_gu=moe_w_gu, moe_w_down=moe_w_down,
             shared_w_gu=shared_w_gu, shared_w_down=shared_w_down)
    weights = [_layer_weights(p, l) for l in range(DEPTH)]
    consts = _constants()
    nf = norm_f.reshape(1, D_MODEL)
    b_ctx, t_ctx, _ = x_prompt.shape
    b_lat, t_lat, _ = x_sample.shape

    cvec8 = jnp.concatenate([c_ctx[None, :], c, jnp.zeros((8 - 1 - b_lat, D_MODEL), F32)], axis=0)
    mods = _adaln(cvec8, ada_w, ada_b)

    xp, ctx_outs = _trunk(x_prompt.reshape(b_ctx * t_ctx, D_MODEL), mods, weights, consts, nf,
                          t_ctx, 0, None, tm=256, tq=t_ctx, tm_moe=1024)
    rc, rs = _rope_tables(t_lat)
    ckpe = jnp.pad(cache_mla_kpe, ((0, 0), (0, 0), (0, 0), (0, LANES - MLA_ROPE)))
    cache = (cache_mla_ckv, ckpe, rc, rs, _embed_block_diag_layers(state_gdn), _embed_block_diag_layers(state_rwkv))
    xs, _ = _trunk(x_sample.reshape(b_lat * t_lat, D_MODEL), mods, weights, consts, nf,
                   t_lat, 1, cache, tm=256, tq=256, tm_moe=1024)

    y_prompt = xp.reshape(b_ctx, t_ctx, D_MODEL)
    y_sample = xs.reshape(b_lat, t_lat, D_MODEL)
    new_ckv = jnp.stack([o[0].reshape(b_ctx, t_ctx, KV_LORA) for o in ctx_outs], axis=1)
    new_kpe = jnp.stack([o[1].reshape(b_ctx, t_ctx, MLA_ROPE) for o in ctx_outs], axis=1)
    new_gdn = jnp.stack([o[2] for o in ctx_outs], axis=1)
    new_rwkv = jnp.stack([o[3] for o in ctx_outs], axis=1)
    return (y_prompt, y_sample, new_ckv, new_kpe, new_gdn, new_rwkv)


def _embed_block_diag_layers(s):
    b = s.shape[0]
    return _embed_block_diag(s.reshape(b * DEPTH, 2, HEADS, HEAD_DIM, HEAD_DIM)).reshape(
        b, DEPTH, 2, MIX_W, MIX_W)
```

```python
import functools

import numpy as np
import jax
import jax.numpy as jnp
from jax import lax
from jax.experimental import pallas as pl
from jax.experimental.pallas import tpu as pltpu

F32 = jnp.float32
BF16 = jnp.bfloat16

D_MODEL = 1024
BATCH = 32
SEQ = 256
DEPTH = 2
DEC_BATCH = 2
DEC_SEQ = 1024
PAST_LEN = 512
GRID_W = 64
NORM_EPS = 1e-6

MLA_HEADS = 4
MLA_NOPE = 128
MLA_ROPE = 64
MLA_V = 128
Q_LORA = 384
KV_LORA = 256
ROPE_BASE = 10000.0
MLA_SCALE = (MLA_NOPE + MLA_ROPE) ** -0.5

HEADS = 4
HEAD_DIM = 64
MIX_W = HEADS * HEAD_DIM
GDN_CONV_CH = 3 * MIX_W
CHUNK = 64
RWKV_GN_EPS = 64e-5

N_EXPERTS = 64
TOP_K = 8
N_GROUPS = 8
GROUP_SIZE = N_EXPERTS // N_GROUPS
TOPK_GROUPS = 4
D_EXPERT = 256
ROUTE_SCALE = 2.5

P_MLA = Q_LORA + KV_LORA + MLA_ROPE
P_GDN = GDN_CONV_CH + MIX_W + 4 * HEADS
P_RWKV = 3 * MIX_W + 128 + 128 + 128

LANES = 128
ZM_W = Q_LORA + KV_LORA + 2 * LANES
ZG_W = GDN_CONV_CH + MIX_W
ZR_W = P_RWKV
QH_W = 2 * LANES
VMEM_LIMIT = 56 * 1024 * 1024

_ROPE_SWAP = np.concatenate([np.arange(16, 32), np.arange(0, 16), np.arange(48, 64), np.arange(32, 48)])


def _sigmoid(x):
    return 1.0 / (1.0 + jnp.exp(-x))


def _silu(x):
    return x * _sigmoid(x)


def _softplus(x):
    return jnp.maximum(x, 0.0) + jnp.log(1.0 + jnp.exp(-jnp.abs(x)))


def _rms(x, g, eps=NORM_EPS):
    return x * lax.rsqrt(jnp.mean(x * x, axis=-1, keepdims=True) + eps) * g


def _mm(a, b):
    return jnp.dot(a.astype(BF16), b.astype(BF16), preferred_element_type=F32)


def _mm_nt(a, b):
    return lax.dot_general(a.astype(BF16), b.astype(BF16), (((1,), (1,)), ((), ())),
                           preferred_element_type=F32)


def _mm_tn(a, b):
    return lax.dot_general(a.astype(BF16), b.astype(BF16), (((0,), (0,)), ((), ())),
                           preferred_element_type=F32)


def _split3(x):
    p1 = x.astype(BF16)
    r1 = x - p1.astype(F32)
    p2 = r1.astype(BF16)
    r2 = r1 - p2.astype(F32)
    return p1, p2, r2.astype(BF16)


def _mm_sel_l(sel, x):
    p1, p2, p3 = _split3(x)
    return _mm(sel, p1) + _mm(sel, p2) + _mm(sel, p3)


def _mm_sel_r(x, sel):
    p1, p2, p3 = _split3(x)
    return _mm(p1, sel) + _mm(p2, sel) + _mm(p3, sel)


def _iota(shape, dim):
    return lax.broadcasted_iota(jnp.int32, shape, dim)


def _bd(x, maskbd):
    xb = x.astype(BF16)
    return jnp.concatenate([xb] * HEADS, axis=0) * maskbd


def _chunk_masks(rev):
    row = _iota((CHUNK, MIX_W), 0)
    col = jnp.bitwise_and(_iota((CHUNK, MIX_W), 1), HEAD_DIM - 1)
    r2 = _iota((CHUNK, CHUNK), 0)
    c2 = _iota((CHUNK, CHUNK), 1)
    if rev:
        inc, strict, tri = row <= col, row < col, r2 <= c2
    else:
        inc, strict, tri = row >= col, row > col, r2 >= c2
    eye = jnp.where(row == col, 1.0, 0.0).astype(F32)
    return inc, strict, jnp.where(tri, 1.0, 0.0).astype(BF16), eye


def _neumann_inverse(a_list, eye_list, maskbd):
    bs = [-a for a in a_list]
    ms = [eye + b for eye, b in zip(eye_list, bs)]
    ps = [_mm(b, _bd(b, maskbd)) for b in bs]
    for _ in range(4):
        boths = [_mm(jnp.concatenate([m, p], axis=0), _bd(p, maskbd)) for m, p in zip(ms, ps)]
        ms = [m + both[:CHUNK] for m, both in zip(ms, boths)]
        ps = [both[CHUNK:] for both in boths]
    return [m + _mm(m, _bd(p, maskbd)) for m, p in zip(ms, ps)]


def _adaln_kernel(c_ref, w_ref, b_ref, o_ref):
    cv = c_ref[...]
    o_ref[0] = _mm(_silu(cv), w_ref[0]) + b_ref[0]


def _adaln(cvec8, ada_w, ada_b):
    tn = 768
    n_out = 6 * D_MODEL
    return pl.pallas_call(
        _adaln_kernel,
        grid=(DEPTH, n_out // tn),
        in_specs=[
            pl.BlockSpec((8, D_MODEL), lambda l, j: (0, 0)),
            pl.BlockSpec((1, D_MODEL, tn), lambda l, j: (l, 0, j)),
            pl.BlockSpec((1, 1, tn), lambda l, j: (l, 0, j)),
        ],
        out_specs=pl.BlockSpec((1, 8, tn), lambda l, j: (l, 0, j)),
        out_shape=jax.ShapeDtypeStruct((DEPTH, 8, n_out), F32),
        compiler_params=pltpu.CompilerParams(dimension_semantics=("arbitrary", "arbitrary"),
                                             vmem_limit_bytes=VMEM_LIMIT),
        name="adaln",
    )(cvec8, ada_w, ada_b.reshape(DEPTH, 1, n_out))


def _inproj_kernel(x_ref, mod_ref, n1_ref, w_ref, zm_ref, zg_ref, zab_ref, zr_ref):
    m = mod_ref[0]
    sh = m[:, 0:D_MODEL]
    sc = m[:, D_MODEL:2 * D_MODEL]
    h = _rms(x_ref[...], n1_ref[...]) * (1.0 + sc) + sh
    z = _mm(h, w_ref[...])
    o1 = ZM_W
    o2 = o1 + ZG_W
    o3 = o2 + LANES
    zm_ref[...] = z[:, :o1]
    zg_ref[...] = z[:, o1:o2]
    zab_ref[...] = z[:, o2:o3]
    zr_ref[...] = z[:, o3:]


def _inproj(x2d, mods, n1, w, tm, seq_len, mod_base):
    n = x2d.shape[0]
    wtot = w.shape[1]
    tiles_per_seq = seq_len // tm if mod_base else 1

    def mod_idx(i):
        return (mod_base + i // tiles_per_seq if mod_base else 0, 0, 0)

    return pl.pallas_call(
        _inproj_kernel,
        grid=(n // tm,),
        in_specs=[
            pl.BlockSpec((tm, D_MODEL), lambda i: (i, 0)),
            pl.BlockSpec((1, 1, 6 * D_MODEL), mod_idx),
            pl.BlockSpec((1, D_MODEL), lambda i: (0, 0)),
            pl.BlockSpec((D_MODEL, wtot), lambda i: (0, 0)),
        ],
        out_specs=[
            pl.BlockSpec((tm, ZM_W), lambda i: (i, 0)),
            pl.BlockSpec((tm, ZG_W), lambda i: (i, 0)),
            pl.BlockSpec((tm, LANES), lambda i: (i, 0)),
            pl.BlockSpec((tm, ZR_W), lambda i: (i, 0)),
        ],
        out_shape=[
            jax.ShapeDtypeStruct((n, ZM_W), F32),
            jax.ShapeDtypeStruct((n, ZG_W), F32),
            jax.ShapeDtypeStruct((n, LANES), F32),
            jax.ShapeDtypeStruct((n, ZR_W), F32),
        ],
        compiler_params=pltpu.CompilerParams(dimension_semantics=("arbitrary",),
                                             vmem_limit_bytes=VMEM_LIMIT),
        name="inproj",
    )(x2d, mods, n1, w)


def _mla_kernel(*refs, seq_len, tq, past, cached):
    if cached:
        (zm_ref, cckv_ref, ckpe_ref, rc_ref, rs_ref, qn_ref, wuq_ref, wuqs_ref, kvn_ref, wukv_ref,
         o_ref, ckv_ref, k_s, v_s) = refs
    else:
        (zm_ref, qn_ref, wuq_ref, kvn_ref, wukv_ref, o_ref, ckv_ref, k_s, v_s) = refs
    qi = pl.program_id(1)
    o_kpe = Q_LORA + KV_LORA

    @pl.when(qi == 0)
    def _():
        zm = zm_ref[...]
        ckv = _rms(zm[:, Q_LORA:o_kpe], kvn_ref[...])
        ckv_ref[...] = ckv
        kpe = zm[:, o_kpe:o_kpe + LANES]
        if cached:
            kpe = kpe * rc_ref[...] + zm[:, o_kpe + LANES:o_kpe + 2 * LANES] * rs_ref[...]
            kvc = _mm(cckv_ref[0], wukv_ref[...])
            kpc = ckpe_ref[0].astype(BF16)
        kv = _mm(ckv, wukv_ref[...])
        kpe = kpe.astype(BF16)
        for h in range(MLA_HEADS):
            c0 = h * QH_W
            if cached:
                k_s[0:past, c0:c0 + LANES] = kvc[:, c0:c0 + LANES].astype(BF16)
                k_s[0:past, c0 + LANES:c0 + QH_W] = kpc
                v_s[0:past, h * MLA_V:(h + 1) * MLA_V] = kvc[:, c0 + LANES:c0 + QH_W].astype(BF16)
            k_s[past:past + seq_len, c0:c0 + LANES] = kv[:, c0:c0 + LANES].astype(BF16)
            k_s[past:past + seq_len, c0 + LANES:c0 + QH_W] = kpe
            v_s[past:past + seq_len, h * MLA_V:(h + 1) * MLA_V] = kv[:, c0 + LANES:c0 + QH_W].astype(BF16)

    r0 = pl.multiple_of(qi * tq, tq)
    zq = zm_ref[pl.ds(r0, tq), :]
    cq = _rms(zq[:, :Q_LORA], qn_ref[...])
    q = _mm(cq, wuq_ref[...])
    if cached:
        qs = _mm(cq, wuqs_ref[...])
        ones = jnp.ones((tq, LANES), F32)
        zeros = jnp.zeros((tq, LANES), F32)
        qc = jnp.concatenate([ones, rc_ref[pl.ds(r0, tq), :]], axis=1)
        qsn = jnp.concatenate([zeros, rs_ref[pl.ds(r0, tq), :]], axis=1)
    for h in range(MLA_HEADS):
        c0 = h * QH_W
        qh = q[:, c0:c0 + QH_W]
        if cached:
            qh = qh * qc + qs[:, c0:c0 + QH_W] * qsn
        s = _mm_nt(qh, k_s[:, c0:c0 + QH_W]) * MLA_SCALE
        e = jnp.exp(s - jnp.max(s, axis=-1, keepdims=True))
        den = jnp.sum(e, axis=-1, keepdims=True)
        o_ref[:, h * MLA_V:(h + 1) * MLA_V] = _mm(e, v_s[:, h * MLA_V:(h + 1) * MLA_V]) / den


def _mla(zm, lw, seq_len, tq, cache=None):
    n = zm.shape[0]
    nb = n // seq_len
    cached = cache is not None
    past = PAST_LEN if cached else 0
    tk = past + seq_len
    seq_spec = lambda w: pl.BlockSpec((seq_len, w), lambda b, q: (b, 0))
    full = lambda a: pl.BlockSpec(a.shape, lambda b, q: (0,) * a.ndim)
    if cached:
        cckv, ckpe, rc, rs = cache
        args = [zm, cckv, ckpe, rc, rs, lw["q_norm"], lw["w_uq"], lw["w_uq_sw"], lw["kv_norm"], lw["w_ukv"]]
        in_specs = [seq_spec(ZM_W),
                    pl.BlockSpec((1, past, KV_LORA), lambda b, q: (b, 0, 0)),
                    pl.BlockSpec((1, past, LANES), lambda b, q: (b, 0, 0)),
                    full(rc), full(rs)] + [full(a) for a in args[5:]]
    else:
        args = [zm, lw["q_norm"], lw["w_uq"], lw["kv_norm"], lw["w_ukv"]]
        in_specs = [seq_spec(ZM_W)] + [full(a) for a in args[1:]]
    return pl.pallas_call(
        functools.partial(_mla_kernel, seq_len=seq_len, tq=tq, past=past, cached=cached),
        grid=(nb, seq_len // tq),
        in_specs=in_specs,
        out_specs=[
            pl.BlockSpec((tq, MLA_HEADS * MLA_V), lambda b, q: (b * (seq_len // tq) + q, 0)),
            pl.BlockSpec((seq_len, KV_LORA), lambda b, q: (b, 0)),
        ],
        out_shape=[
            jax.ShapeDtypeStruct((n, MLA_HEADS * MLA_V), F32),
            jax.ShapeDtypeStruct((n, KV_LORA), F32),
        ],
        scratch_shapes=[
            pltpu.VMEM((tk, MLA_HEADS * QH_W), BF16),
            pltpu.VMEM((tk, MLA_HEADS * MLA_V), BF16),
        ],
        compiler_params=pltpu.CompilerParams(dimension_semantics=("arbitrary", "arbitrary"),
                                             vmem_limit_bytes=VMEM_LIMIT),
        name="mla_lat" if cached else "mla_ctx",
    )(*args)


CHUNK_GROUP = 4
GDN_PRE_W = 5 * MIX_W


def _for_chunk_groups(n_chunks, fn):
    if n_chunks == CHUNK_GROUP:
        fn(0)
    else:
        def body(gi, carry):
            fn(gi * CHUNK_GROUP)
            return carry
        lax.fori_loop(0, n_chunks // CHUNK_GROUP, body, 0)


def _gdn_prepare(items, maskbd):
    n = range(len(items))
    qs, ks, vs, gs, betas, masks, revs = zip(*items)
    gcs = [_mm_sel_l(masks[i][2], gs[i]) for i in n]
    decays = []
    for i in n:
        inc, eye = masks[i][0], masks[i][3]
        gc_row = jnp.sum(eye * gcs[i], axis=0, keepdims=True)
        decays.append(jnp.where(inc, jnp.exp(jnp.where(inc, gcs[i] - gc_row, 0.0)), 0.0))
    kbs = [ks[i] * betas[i] for i in n]
    aqs = [_mm_nt(jnp.concatenate([kbs[i], qs[i]], axis=0), _bd(ks[i], maskbd)) for i in n]
    a_mats = [jnp.where(masks[i][1], aqs[i][:CHUNK] * decays[i], 0.0) for i in n]
    t_invs = _neumann_inverse(a_mats, [m[3] for m in masks], maskbd)
    egcs = [jnp.exp(gc) for gc in gcs]
    uws = [_mm(t_invs[i], jnp.concatenate([_bd(vs[i] * betas[i], maskbd), _bd(kbs[i] * egcs[i], maskbd)], axis=1))
           for i in n]
    out = []
    for i in n:
        g_last = gcs[i][0:1] if revs[i] else gcs[i][CHUNK - 1:CHUNK]
        pre = jnp.concatenate([uws[i], qs[i] * egcs[i], aqs[i][CHUNK:] * decays[i],
                               ks[i] * jnp.exp(g_last - gcs[i])], axis=1)
        out.append((pre, jnp.broadcast_to(jnp.exp(g_last), (8, MIX_W))))
    return out


def _gdn_step(pre, egl, s_bd, maskbd):
    w = MIX_W
    u, wq_in, qk, kdec = pre[:, :w], pre[:, w:3 * w], pre[:, 3 * w:4 * w], pre[:, 4 * w:]
    wq = _mm(jnp.concatenate([wq_in[:, :w], wq_in[:, w:]], axis=0), s_bd)
    v_new = u - wq[:CHUNK]
    o = wq[CHUNK:] + _mm(qk, _bd(v_new, maskbd))
    s_new = s_bd * egl + _mm_tn(kdec, v_new) * maskbd.astype(F32)
    return o, s_new


def _gdn_kernel(*refs, seq_len, cached):
    if cached:
        (zg_ref, zab_ref, s0_ref, conv_ref, alog_ref, dtb_ref, gn_ref, eexp_ref, ones_ref, maskbd_ref,
         o_ref, sout_ref, q_s, k_s, v_s, ge_s, pre_s, gl_s, oacc_s, st_s) = refs
    else:
        (zg_ref, zab_ref, conv_ref, alog_ref, dtb_ref, gn_ref, eexp_ref, ones_ref, maskbd_ref,
         o_ref, sout_ref, q_s, k_s, v_s, ge_s, pre_s, gl_s, oacc_s, st_s) = refs
    t = seq_len
    z = zg_ref[:, :GDN_CONV_CH]
    rowi = _iota((t, 1), 0)
    zp = jnp.where(rowi == 0, 0.0, pltpu.roll(z, 1, 0))
    zn = jnp.where(rowi == t - 1, 0.0, pltpu.roll(z, t - 1, 0))
    cw = conv_ref[...]
    qkv = _silu(zp * cw[0:1] + z * cw[1:2] + zn * cw[2:3])
    ones_bd = ones_ref[...]
    q = qkv[:, :MIX_W]
    k = qkv[:, MIX_W:2 * MIX_W]
    q_s[...] = q * lax.rsqrt(_mm_sel_r(q * q, ones_bd) + 1e-6) * (HEAD_DIM ** -0.5)
    k_s[...] = k * lax.rsqrt(_mm_sel_r(k * k, ones_bd) + 1e-6)
    v_s[...] = qkv[:, 2 * MIX_W:]
    ab = zab_ref[...]
    lane = _iota((t, LANES), 1)
    gb = jnp.where(lane < 2 * HEADS, -jnp.exp(alog_ref[...]) * _softplus(ab + dtb_ref[...]), _sigmoid(ab))
    ge_s[...] = _mm_sel_r(gb, eexp_ref[...])
    oacc_s[...] = jnp.zeros((t, MIX_W), F32)
    if cached:
        st_s[...] = s0_ref[0]
    else:
        st_s[...] = jnp.zeros((2, MIX_W, MIX_W), F32)
    maskbd = maskbd_ref[...]
    masks = (_chunk_masks(False), _chunk_masks(True))
    n_chunks = t // CHUNK

    def prepare_group(c0):
        where, items = [], []
        for j in range(CHUNK_GROUP):
            c = c0 + j
            rows = pl.ds(pl.multiple_of(c * CHUNK, CHUNK), CHUNK)
            for d in range(2):
                where.append((d, c, rows))
                items.append((q_s[rows, :], k_s[rows, :], v_s[rows, :], ge_s[rows, d * MIX_W:(d + 1) * MIX_W],
                              ge_s[rows, (2 + d) * MIX_W:(3 + d) * MIX_W], masks[d], d == 1))
        for (d, c, rows), (pre, egl) in zip(where, _gdn_prepare(items, maskbd)):
            pre_s[d, rows, :] = pre
            gl_s[d, pl.ds(pl.multiple_of(c * 8, 8), 8), :] = egl

    _for_chunk_groups(n_chunks, prepare_group)

    def body(i, carry):
        for d in range(2):
            c = i if d == 0 else n_chunks - 1 - i
            rows = pl.ds(pl.multiple_of(c * CHUNK, CHUNK), CHUNK)
            egl = gl_s[d, pl.ds(pl.multiple_of(c * 8, 8), 8), :][0:1]
            o, s_new = _gdn_step(pre_s[d, rows, :], egl, st_s[d], maskbd)
            oacc_s[rows, :] = oacc_s[rows, :] + o
            st_s[d] = s_new
        return carry

    lax.fori_loop(0, n_chunks, body, 0)
    o = oacc_s[...]
    ms = _mm_sel_r(o * o, ones_bd) * (1.0 / HEAD_DIM)
    gate = zg_ref[:, GDN_CONV_CH:]
    o_ref[...] = o * lax.rsqrt(ms + NORM_EPS) * gn_ref[...] * _silu(gate)
    for d in range(2):
        for h in range(HEADS):
            sl = slice(h * HEAD_DIM, (h + 1) * HEAD_DIM)
            sout_ref[0, d, h] = st_s[d, sl, sl]


def _gdn(zg, zab, lw, consts, seq_len, s0_bd=None):
    n = zg.shape[0]
    nb = n // seq_len
    cached = s0_bd is not None
    full = lambda a: pl.BlockSpec(a.shape, lambda b: (0,) * a.ndim)
    args = [zg, zab]
    in_specs = [pl.BlockSpec((seq_len, ZG_W), lambda b: (b, 0)),
                pl.BlockSpec((seq_len, LANES), lambda b: (b, 0))]
    if cached:
        args.append(s0_bd)
        in_specs.append(pl.BlockSpec((1, 2, MIX_W, MIX_W), lambda b: (b, 0, 0, 0)))
    tail = [lw["gdn_conv"], lw["gdn_alog"], lw["gdn_dtb"], lw["gdn_norm"], consts["eexp"], consts["ones_bd"],
            consts["maskbd"]]
    args += tail
    in_specs += [full(a) for a in tail]
    return pl.pallas_call(
        functools.partial(_gdn_kernel, seq_len=seq_len, cached=cached),
        grid=(nb,),
        in_specs=in_specs,
        out_specs=[
            pl.BlockSpec((seq_len, MIX_W), lambda b: (b, 0)),
            pl.BlockSpec((1, 2, HEADS, HEAD_DIM, HEAD_DIM), lambda b: (b, 0, 0, 0, 0)),
        ],
        out_shape=[
            jax.ShapeDtypeStruct((n, MIX_W), F32),
            jax.ShapeDtypeStruct((nb, 2, HEADS, HEAD_DIM, HEAD_DIM), F32),
        ],
        scratch_shapes=[
            pltpu.VMEM((seq_len, MIX_W), F32),
            pltpu.VMEM((seq_len, MIX_W), F32),
            pltpu.VMEM((seq_len, MIX_W), F32),
            pltpu.VMEM((seq_len, 4 * MIX_W), F32),
            pltpu.VMEM((2, seq_len, GDN_PRE_W), F32),
            pltpu.VMEM((2, seq_len // CHUNK * 8, MIX_W), F32),
            pltpu.VMEM((seq_len, MIX_W), F32),
            pltpu.VMEM((2, MIX_W, MIX_W), F32),
        ],
        compiler_params=pltpu.CompilerParams(dimension_semantics=("arbitrary",),
                                             vmem_limit_bytes=VMEM_LIMIT),
        name="gdn_lat" if cached else "gdn_ctx",
    )(*args)


RWKV_PRE_W = 7 * MIX_W


def _rwkv_prepare(items, maskbd, eye_full):
    n = range(len(items))
    rs, kds, vs, kks, bs, lws, masks, revs = zip(*items)
    cums = [_mm_sel_l(masks[i][2], lws[i]) for i in n]
    einvs = [jnp.exp(-c) for c in cums]
    kts = [kks[i] * jnp.exp(cums[i] - lws[i]) for i in n]
    rts = [rs[i] * jnp.exp(cums[i]) for i in n]
    krs = [jnp.concatenate([kts[i], rts[i]], axis=0) for i in n]
    lb_alls = [_mm_nt(krs[i], _bd(bs[i] * einvs[i], maskbd)) for i in n]
    lk_alls = [_mm_nt(krs[i], _bd(kds[i] * einvs[i], maskbd)) for i in n]
    lbs = [jnp.where(masks[i][1], lb_alls[i][:CHUNK], 0.0) for i in n]
    t_invs = _neumann_inverse(lbs, [m[3] for m in masks], maskbd)
    lvs = [_mm(jnp.concatenate([jnp.where(masks[i][1], lk_alls[i][:CHUNK], 0.0),
                                jnp.where(masks[i][0], lk_alls[i][CHUNK:], 0.0)], axis=0), _bd(vs[i], maskbd))
           for i in n]
    tkps = [_mm(t_invs[i], jnp.concatenate([_bd(kts[i], maskbd), _bd(lvs[i][:CHUNK], maskbd)], axis=1)) for i in n]
    out = []
    for i in n:
        c_last = cums[i][0:1] if revs[i] else cums[i][CHUNK - 1:CHUNK]
        tail = jnp.exp(c_last - cums[i])
        rb = jnp.where(masks[i][0], lb_alls[i][CHUNK:], 0.0)
        pre = jnp.concatenate([tkps[i][:, :MIX_W], rts[i], tkps[i][:, MIX_W:], lvs[i][CHUNK:], rb,
                               kds[i] * tail, bs[i] * tail], axis=1)
        gcol = jnp.sum(eye_full * jnp.exp(c_last), axis=1, keepdims=True)
        out.append((pre, jnp.broadcast_to(gcol, (MIX_W, LANES))))
    return out


def _rwkv_step(pre, v, gcol, z_bd, maskbd):
    w = MIX_W
    pr = _mm(jnp.concatenate([pre[:, :w], pre[:, w:2 * w]], axis=0), z_bd)
    p = pr[:CHUNK] + pre[:, 2 * w:3 * w]
    o = pr[CHUNK:] + pre[:, 3 * w:4 * w] - _mm(pre[:, 4 * w:5 * w], _bd(p, maskbd))
    upd = _mm_tn(jnp.concatenate([pre[:, 5 * w:6 * w], pre[:, 6 * w:]], axis=0), jnp.concatenate([v, -p], axis=0))
    z_new = z_bd * jnp.concatenate([gcol, gcol], axis=1) + upd * maskbd.astype(F32)
    return o, z_new


def _rwkv_kernel(*refs, seq_len, cached):
    if cached:
        (zr_ref, s0_ref, mup_ref, mun_ref, w0_ref, w2_ref, a0_ref, a2_ref, g2_ref, kk_ref, ka_ref, rk_ref,
         gnw_ref, gnb_ref, ones_ref, maskbd_ref, o_ref, sout_ref,
         r_s, v_s, kk_s, dir_s, bg_s, pre_s, gcol_s, oacc_s, st_s) = refs
    else:
        (zr_ref, mup_ref, mun_ref, w0_ref, w2_ref, a0_ref, a2_ref, g2_ref, kk_ref, ka_ref, rk_ref,
         gnw_ref, gnb_ref, ones_ref, maskbd_ref, o_ref, sout_ref,
         r_s, v_s, kk_s, dir_s, bg_s, pre_s, gcol_s, oacc_s, st_s) = refs
    t = seq_len
    z = zr_ref[...]
    rowi = _iota((t, 1), 0)
    zp = jnp.where(rowi == 0, 0.0, pltpu.roll(z, 1, 0))
    zn = jnp.where(rowi == t - 1, 0.0, pltpu.roll(z, t - 1, 0))
    z = z + mup_ref[...] * (zp - z) + mun_ref[...] * (zn - z)
    w = MIX_W
    r = z[:, :w]
    k = z[:, w:2 * w]
    v = z[:, 2 * w:3 * w]
    wd = jnp.tanh(z[:, 3 * w:3 * w + LANES])
    ad = z[:, 3 * w + LANES:3 * w + 2 * LANES]
    gd = _sigmoid(z[:, 3 * w + 2 * LANES:])
    ones_bd = ones_ref[...]
    kk = k * kk_ref[...]
    kk = kk * lax.rsqrt(_mm_sel_r(kk * kk, ones_bd) + 1e-6)
    r_s[...] = r
    v_s[...] = v
    kk_s[...] = kk
    bonus = jnp.zeros((t, w), F32)
    for d in range(2):
        w_log = -_softplus(-(w0_ref[d:d + 1] + _mm(wd, w2_ref[d]))) - 0.5
        a = _sigmoid(a0_ref[d:d + 1] + _mm(ad, a2_ref[d]))
        kd = k * (1.0 + (a - 1.0) * ka_ref[...])
        dir_s[:, (3 * d) * w:(3 * d + 1) * w] = -jnp.exp(w_log)
        dir_s[:, (3 * d + 1) * w:(3 * d + 2) * w] = kd
        dir_s[:, (3 * d + 2) * w:(3 * d + 3) * w] = kk * a
        bonus = bonus + _mm_sel_r(r * kd * rk_ref[...], ones_bd) * v
    bg_s[:, :w] = bonus
    bg_s[:, w:] = _mm(gd, g2_ref[...])
    oacc_s[...] = jnp.zeros((t, w), F32)
    if cached:
        st_s[...] = s0_ref[0]
    else:
        st_s[...] = jnp.zeros((2, w, w), F32)
    maskbd = maskbd_ref[...]
    masks = (_chunk_masks(False), _chunk_masks(True))
    eye_full = jnp.where(_iota((w, w), 0) == _iota((w, w), 1), 1.0, 0.0).astype(F32)
    n_chunks = t // CHUNK

    def prepare_group(c0):
        where, items = [], []
        for j in range(CHUNK_GROUP):
            c = c0 + j
            rows = pl.ds(pl.multiple_of(c * CHUNK, CHUNK), CHUNK)
            for d in range(2):
                where.append((d, c, rows))
                items.append((r_s[rows, :], dir_s[rows, (3 * d + 1) * w:(3 * d + 2) * w], v_s[rows, :], kk_s[rows, :],
                              dir_s[rows, (3 * d + 2) * w:(3 * d + 3) * w], dir_s[rows, (3 * d) * w:(3 * d + 1) * w],
                              masks[d], d == 1))
        for (d, c, rows), (pre, gcol) in zip(where, _rwkv_prepare(items, maskbd, eye_full)):
            pre_s[d, rows, :] = pre
            gcol_s[d, pl.ds(pl.multiple_of(c * w, w), w), :] = gcol

    _for_chunk_groups(n_chunks, prepare_group)

    def body(i, carry):
        for d in range(2):
            c = i if d == 0 else n_chunks - 1 - i
            rows = pl.ds(pl.multiple_of(c * CHUNK, CHUNK), CHUNK)
            gcol = gcol_s[d, pl.ds(pl.multiple_of(c * w, w), w), :]
            o, z_new = _rwkv_step(pre_s[d, rows, :], v_s[rows, :], gcol, st_s[d], maskbd)
            oacc_s[rows, :] = oacc_s[rows, :] + o
            st_s[d] = z_new
        return carry

    lax.fori_loop(0, n_chunks, body, 0)
    o = oacc_s[...]
    inv_n = 1.0 / HEAD_DIM
    mu = _mm_sel_r(o, ones_bd) * inv_n
    oc = o - mu
    var = _mm_sel_r(oc * oc, ones_bd) * inv_n
    y = oc * lax.rsqrt(var + RWKV_GN_EPS) * gnw_ref[...] + gnb_ref[...]
    o_ref[...] = (y + bg_s[:, :w]) * bg_s[:, w:]
    for d in range(2):
        for h in range(HEADS):
            sl = slice(h * HEAD_DIM, (h + 1) * HEAD_DIM)
            sout_ref[0, d, h] = st_s[d, sl, sl]


def _rwkv(zr, lw, consts, seq_len, s0_bd=None):
    n = zr.shape[0]
    nb = n // seq_len
    cached = s0_bd is not None
    full = lambda a: pl.BlockSpec(a.shape, lambda b: (0,) * a.ndim)
    args = [zr]
    in_specs = [pl.BlockSpec((seq_len, ZR_W), lambda b: (b, 0))]
    if cached:
        args.append(s0_bd)
        in_specs.append(pl.BlockSpec((1, 2, MIX_W, MIX_W), lambda b: (b, 0, 0, 0)))
    tail = [lw["rwkv_mu_prev"], lw["rwkv_mu_next"], lw["rwkv_w0"], lw["rwkv_w2"], lw["rwkv_a0"], lw["rwkv_a2"],
            lw["rwkv_g2"], lw["rwkv_k_k"], lw["rwkv_k_a"], lw["rwkv_r_k"], lw["rwkv_gn_w"], lw["rwkv_gn_b"],
            consts["ones_bd"], consts["maskbd"]]
    args += tail
    in_specs += [full(a) for a in tail]
    return pl.pallas_call(
        functools.partial(_rwkv_kernel, seq_len=seq_len, cached=cached),
        grid=(nb,),
        in_specs=in_specs,
        out_specs=[
            pl.BlockSpec((seq_len, MIX_W), lambda b: (b, 0)),
            pl.BlockSpec((1, 2, HEADS, HEAD_DIM, HEAD_DIM), lambda b: (b, 0, 0, 0, 0)),
        ],
        out_shape=[
            jax.ShapeDtypeStruct((n, MIX_W), F32),
            jax.ShapeDtypeStruct((nb, 2, HEADS, HEAD_DIM, HEAD_DIM), F32),
        ],
        scratch_shapes=[
            pltpu.VMEM((seq_len, MIX_W), F32),
            pltpu.VMEM((seq_len, MIX_W), F32),
            pltpu.VMEM((seq_len, MIX_W), F32),
            pltpu.VMEM((seq_len, 6 * MIX_W), F32),
            pltpu.VMEM((seq_len, 2 * MIX_W), F32),
            pltpu.VMEM((2, seq_len, RWKV_PRE_W), F32),
            pltpu.VMEM((2, seq_len // CHUNK * MIX_W, LANES), F32),
            pltpu.VMEM((seq_len, MIX_W), F32),
            pltpu.VMEM((2, MIX_W, MIX_W), F32),
        ],
        compiler_params=pltpu.CompilerParams(dimension_semantics=("arbitrary",),
                                             vmem_limit_bytes=VMEM_LIMIT),
        name="rwkv_lat" if cached else "rwkv_ctx",
    )(*args)


def _split2(x):
    hi = x.astype(BF16)
    return hi, (x - hi.astype(F32)).astype(BF16)


def _route(logits_t, bias):
    tm = logits_t.shape[1]
    neg = -jnp.inf
    sc = _sigmoid(logits_t)
    sel = (sc + bias).reshape(N_GROUPS, GROUP_SIZE, tm)
    si = _iota(sel.shape, 1).astype(F32)
    m1 = jnp.max(sel, axis=1, keepdims=True)
    f1 = jnp.min(jnp.where(sel == m1, si, float(GROUP_SIZE)), axis=1, keepdims=True)
    m2 = jnp.max(jnp.where(si == f1, neg, sel), axis=1, keepdims=True)
    grp = m1 + m2
    gi = _iota(grp.shape, 0).astype(F32)
    gsel = jnp.zeros(grp.shape, F32)
    for _ in range(TOPK_GROUPS):
        mx = jnp.max(grp, axis=0, keepdims=True)
        fi = jnp.min(jnp.where(grp == mx, gi, float(N_GROUPS)), axis=0, keepdims=True)
        hit = gi == fi
        gsel = jnp.where(hit, 1.0, gsel)
        grp = jnp.where(hit, neg, grp)
    cur = jnp.where(gsel > 0.0, sel, neg)
    ei = (_iota(cur.shape, 0) * GROUP_SIZE + _iota(cur.shape, 1)).astype(F32)
    chosen = jnp.zeros(cur.shape, F32)
    for _ in range(TOP_K):
        mx = jnp.max(jnp.max(cur, axis=0, keepdims=True), axis=1, keepdims=True)
        fi = jnp.min(jnp.min(jnp.where(cur == mx, ei, float(N_EXPERTS)), axis=0, keepdims=True),
                     axis=1, keepdims=True)
        hit = ei == fi
        chosen = jnp.where(hit, 1.0, chosen)
        cur = jnp.where(hit, neg, cur)
    wsel = chosen.reshape(N_EXPERTS, tm) * sc
    return wsel / jnp.sum(wsel, axis=0, keepdims=True) * ROUTE_SCALE


def _post_kernel(x_ref, om_ref, og_ref, or_ref, mod_ref, wo_ref, n2_ref, rt_ref, rb_ref,
                 x1_ref, h2_ref, cw_ref):
    m = mod_ref[0]
    g1 = m[:, 2 * D_MODEL:3 * D_MODEL]
    sh2 = m[:, 3 * D_MODEL:4 * D_MODEL]
    sc2 = m[:, 4 * D_MODEL:5 * D_MODEL]
    w_mla = MLA_HEADS * MLA_V
    mix = (_mm(om_ref[...], wo_ref[0:w_mla, :]) + _mm(og_ref[...], wo_ref[w_mla:w_mla + MIX_W, :])
           + _mm(or_ref[...], wo_ref[w_mla + MIX_W:, :]))
    x1 = x_ref[...] + g1 * mix
    x1_ref[...] = x1
    h2 = _rms(x1, n2_ref[...]) * (1.0 + sc2) + sh2
    h2_ref[...] = h2.astype(BF16)
    r_hi, r_lo = _split2(rt_ref[...])
    h_hi, h_lo = _split2(h2)
    logits_t = _mm_nt(r_hi, h_hi) + _mm_nt(r_hi, h_lo) + _mm_nt(r_lo, h_hi)
    cw_t = _route(logits_t, rb_ref[...])
    cw_ref[...] = jnp.concatenate([cw_t, jnp.zeros_like(cw_t)], axis=0).T


def _post(x2d, om, og, orw, mods, lw, tm, seq_len, mod_base):
    n = x2d.shape[0]
    tiles_per_seq = seq_len // tm if mod_base else 1

    def mod_idx(i):
        return (mod_base + i // tiles_per_seq if mod_base else 0, 0, 0)

    row = lambda w: pl.BlockSpec((tm, w), lambda i: (i, 0))
    full = lambda a: pl.BlockSpec(a.shape, lambda i: (0,) * a.ndim)
    tail = [lw["w_out"], lw["norm2"], lw["router_t"], lw["router_b"]]
    return pl.pallas_call(
        _post_kernel,
        grid=(n // tm,),
        in_specs=[row(D_MODEL), row(MLA_HEADS * MLA_V), row(MIX_W), row(MIX_W),
                  pl.BlockSpec((1, 1, 6 * D_MODEL), mod_idx)] + [full(a) for a in tail],
        out_specs=[row(D_MODEL), row(D_MODEL), row(LANES)],
        out_shape=[
            jax.ShapeDtypeStruct((n, D_MODEL), F32),
            jax.ShapeDtypeStruct((n, D_MODEL), BF16),
            jax.ShapeDtypeStruct((n, LANES), F32),
        ],
        compiler_params=pltpu.CompilerParams(dimension_semantics=("arbitrary",),
                                             vmem_limit_bytes=VMEM_LIMIT),
        name="post",
    )(x2d, om, og, orw, mods, *tail)


def _swiglu_act(gu):
    return _silu(gu[:, :D_EXPERT]) * gu[:, D_EXPERT:]


MOE_EXPERTS_PER_STEP = 2


def _moe_kernel(h_ref, x1_ref, cw_ref, mod_ref, wgu_ref, wdn_ref, sgu_ref, sdn_ref, nf_ref, o_ref, *, final):
    step = pl.program_id(1)
    h = h_ref[...]

    @pl.when(step == 0)
    def _():
        o_ref[...] = _mm(_swiglu_act(_mm(h, sgu_ref[...])), sdn_ref[...])

    cw = cw_ref[...]
    lane = _iota(cw.shape, 1)
    acts = []
    for g in range(MOE_EXPERTS_PER_STEP):
        act = _swiglu_act(_mm(h, wgu_ref[g]))
        scale = jnp.sum(jnp.where(lane == step * MOE_EXPERTS_PER_STEP + g, cw, 0.0), axis=-1, keepdims=True)
        acts.append((act * scale).astype(BF16))
    wdn = wdn_ref[...].reshape(MOE_EXPERTS_PER_STEP * D_EXPERT, D_MODEL)
    o_ref[...] += _mm(jnp.concatenate(acts, axis=1), wdn)

    @pl.when(step == N_EXPERTS // MOE_EXPERTS_PER_STEP - 1)
    def _():
        g2 = mod_ref[0][:, 5 * D_MODEL:]
        x2 = x1_ref[...] + g2 * o_ref[...]
        if final:
            x2 = _rms(x2, nf_ref[...])
        o_ref[...] = x2


def _moe(h2, x1, cw, mods, lw, norm_f, tm, seq_len, mod_base, final):
    n = h2.shape[0]
    tiles_per_seq = seq_len // tm if mod_base else 1
    g = MOE_EXPERTS_PER_STEP

    def mod_idx(i, e):
        return (mod_base + i // tiles_per_seq if mod_base else 0, 0, 0)

    row = lambda w, **kw: pl.BlockSpec((tm, w), lambda i, e: (i, 0), **kw)
    full = lambda a: pl.BlockSpec(a.shape, lambda i, e: (0,) * a.ndim, pipeline_mode=pl.Buffered(1))
    return pl.pallas_call(
        functools.partial(_moe_kernel, final=final),
        grid=(n // tm, N_EXPERTS // g),
        in_specs=[row(D_MODEL), row(D_MODEL, pipeline_mode=pl.Buffered(1)), row(LANES),
                  pl.BlockSpec((1, 1, 6 * D_MODEL), mod_idx),
                  pl.BlockSpec((g, D_MODEL, 2 * D_EXPERT), lambda i, e: (e, 0, 0)),
                  pl.BlockSpec((g, D_EXPERT, D_MODEL), lambda i, e: (e, 0, 0)),
                  full(lw["shared_w_gu"]), full(lw["shared_w_down"]), full(norm_f)],
        out_specs=row(D_MODEL),
        out_shape=jax.ShapeDtypeStruct((n, D_MODEL), F32),
        compiler_params=pltpu.CompilerParams(dimension_semantics=("arbitrary", "arbitrary"),
                                             vmem_limit_bytes=VMEM_LIMIT),
        name="moe_final" if final else "moe",
    )(h2, x1, cw, mods, lw["moe_w_gu"], lw["moe_w_down"], lw["shared_w_gu"], lw["shared_w_down"], norm_f)


def _constants():
    idx = np.arange(MIX_W)
    same_head = (idx[:, None] // HEAD_DIM) == (idx[None, :] // HEAD_DIM)
    maskbd = jnp.asarray(same_head, BF16)
    eexp = np.zeros((LANES, 4 * MIX_W), np.float32)
    for blk in range(4):
        kind, d = divmod(blk, 2)
        for h in range(HEADS):
            src = kind * 2 * HEADS + d * HEADS + h
            eexp[src, blk * MIX_W + h * HEAD_DIM: blk * MIX_W + (h + 1) * HEAD_DIM] = 1.0
    return {"maskbd": maskbd, "ones_bd": maskbd, "eexp": jnp.asarray(eexp, BF16)}


def _rope_tables(n):
    rows = n // GRID_W
    row = jnp.repeat(jnp.arange(rows, dtype=F32), GRID_W)
    col = jnp.tile(jnp.arange(GRID_W, dtype=F32), rows)
    axis_dim = MLA_ROPE // 2
    inv = jnp.power(ROPE_BASE, -jnp.arange(0, axis_dim, 2, dtype=F32) / axis_dim)
    ang_r = row[:, None] * inv
    ang_c = col[:, None] * inv
    cr, sr, cc, sc = jnp.cos(ang_r), jnp.sin(ang_r), jnp.cos(ang_c), jnp.sin(ang_c)
    zeros = jnp.zeros((n, LANES - MLA_ROPE), F32)
    cos_t = jnp.concatenate([cr, cr, cc, cc, zeros], axis=1)
    sin_t = jnp.concatenate([-sr, sr, -sc, sc, zeros], axis=1)
    return cos_t, sin_t


def _pad_lanes(v, width):
    v = v.reshape(1, -1)
    return jnp.pad(v, ((0, 0), (0, width - v.shape[1])))


def _layer_weights(p, l):
    w_in = p["w_in"][l]
    kpe0 = Q_LORA + KV_LORA
    zc = jnp.zeros((D_MODEL, LANES - MLA_ROPE), F32)
    w_in_p = jnp.concatenate([
        w_in[:, :P_MLA], zc, w_in[:, kpe0 + _ROPE_SWAP], zc,
        w_in[:, P_MLA:P_MLA + ZG_W],
        w_in[:, P_MLA + ZG_W:P_MLA + P_GDN], jnp.zeros((D_MODEL, LANES - 4 * HEADS), F32),
        w_in[:, P_MLA + P_GDN:]], axis=1).astype(BF16)
    w_uq = p["mla_w_uq"][l].reshape(Q_LORA, MLA_HEADS, MLA_NOPE + MLA_ROPE)
    zq = jnp.zeros((Q_LORA, MLA_HEADS, QH_W - MLA_NOPE - MLA_ROPE), F32)
    w_uq_a = jnp.concatenate([w_uq, zq], axis=2).reshape(Q_LORA, MLA_HEADS * QH_W).astype(BF16)
    w_uq_sw = jnp.concatenate([jnp.zeros((Q_LORA, MLA_HEADS, MLA_NOPE), F32),
                               w_uq[:, :, MLA_NOPE + _ROPE_SWAP], zq], axis=2)
    w_uq_sw = w_uq_sw.reshape(Q_LORA, MLA_HEADS * QH_W).astype(BF16)
    half = jnp.zeros((64, MIX_W), F32)
    w2 = p["rwkv_w2"][l]
    a2 = p["rwkv_a2"][l]
    tile4 = lambda v: jnp.tile(v.reshape(1, HEAD_DIM), (1, HEADS))
    return {
        "norm1": p["norm1"][l].reshape(1, D_MODEL),
        "w_in": w_in_p,
        "q_norm": p["mla_q_norm"][l].reshape(1, Q_LORA),
        "w_uq": w_uq_a, "w_uq_sw": w_uq_sw,
        "kv_norm": p["mla_kv_norm"][l].reshape(1, KV_LORA),
        "w_ukv": p["mla_w_ukv"][l].astype(BF16),
        "gdn_conv": p["gdn_conv"][l],
        "gdn_alog": _pad_lanes(p["gdn_a_log"][l], LANES),
        "gdn_dtb": _pad_lanes(p["gdn_dt_bias"][l], LANES),
        "gdn_norm": tile4(p["gdn_norm"][l]),
        "rwkv_mu_prev": p["rwkv_mu_prev"][l].reshape(1, ZR_W),
        "rwkv_mu_next": p["rwkv_mu_next"][l].reshape(1, ZR_W),
        "rwkv_w0": p["rwkv_w0"][l],
        "rwkv_w2": jnp.stack([jnp.concatenate([w2[0], half]), jnp.concatenate([half, w2[1]])]).astype(BF16),
        "rwkv_a0": p["rwkv_a0"][l],
        "rwkv_a2": jnp.stack([jnp.concatenate([a2[0], half]), jnp.concatenate([half, a2[1]])]).astype(BF16),
        "rwkv_g2": p["rwkv_g2"][l].astype(BF16),
        "rwkv_k_k": p["rwkv_k_k"][l].reshape(1, MIX_W),
        "rwkv_k_a": p["rwkv_k_a"][l].reshape(1, MIX_W),
        "rwkv_r_k": p["rwkv_r_k"][l].reshape(1, MIX_W),
        "rwkv_gn_w": p["rwkv_gn_w"][l].reshape(1, MIX_W),
        "rwkv_gn_b": p["rwkv_gn_b"][l].reshape(1, MIX_W),
        "w_out": p["w_out"][l].astype(BF16),
        "norm2": p["norm2"][l].reshape(1, D_MODEL),
        "router_t": p["moe_router"][l].T,
        "router_b": p["moe_bias"][l].reshape(N_EXPERTS, 1),
        "moe_w_gu": p["moe_w_gu"][l],
        "moe_w_down": p["moe_w_down"][l],
        "shared_w_gu": p["shared_w_gu"][l].astype(BF16),
        "shared_w_down": p["shared_w_down"][l].astype(BF16),
    }


def _embed_block_diag(s):
    b = s.shape[0]
    eye = jnp.eye(HEADS, dtype=s.dtype)
    out = jnp.einsum("bdhkv,hg->bdhkgv", s, eye)
    return out.reshape(b, 2, MIX_W, MIX_W)


def _trunk(x2d, mods, weights, consts, norm_f, seq_len, mod_base, cache, tm, tq, tm_moe):
    outs = []
    for l in range(DEPTH):
        lw = weights[l]
        ml = mods[l].reshape(8, 1, 6 * D_MODEL)
        zm, zg, zab, zr = _inproj(x2d, ml, lw["norm1"], lw["w_in"], tm, seq_len, mod_base)
        if cache is None:
            o_mla, ckv = _mla(zm, lw, seq_len, tq)
            o_gdn, s_gdn = _gdn(zg, zab, lw, consts, seq_len)
            o_rwkv, s_rwkv = _rwkv(zr, lw, consts, seq_len)
            s_rwkv = jnp.swapaxes(s_rwkv, -1, -2)
        else:
            cckv, ckpe, rc, rs, sg, sr = cache
            o_mla, ckv = _mla(zm, lw, seq_len, tq, (cckv[:, l], ckpe[:, l], rc, rs))
            o_gdn, s_gdn = _gdn(zg, zab, lw, consts, seq_len, sg[:, l])
            o_rwkv, s_rwkv = _rwkv(zr, lw, consts, seq_len, sr[:, l])
        x1, h2, cw = _post(x2d, o_mla, o_gdn, o_rwkv, ml, lw, tm, seq_len, mod_base)
        x2d = _moe(h2, x1, cw, ml, lw, norm_f, tm_moe, seq_len, mod_base, final=(l == DEPTH - 1))
        kpe0 = Q_LORA + KV_LORA
        outs.append((ckv, zm[:, kpe0:kpe0 + MLA_ROPE], s_gdn, s_rwkv))
    return x2d, outs


def kernel(x_prompt, x_sample, cache_mla_ckv, cache_mla_kpe, state_gdn, state_rwkv, c, c_ctx, ada_w, ada_b, norm1, w_in, mla_q_norm, mla_w_uq, mla_kv_norm, mla_w_ukv, gdn_conv, gdn_a_log, gdn_dt_bias, gdn_norm, rwkv_mu_prev, rwkv_mu_next, rwkv_w0, rwkv_w2, rwkv_a0, rwkv_a2, rwkv_g2, rwkv_k_k, rwkv_k_a, rwkv_r_k, rwkv_gn_w, rwkv_gn_b, w_out, norm2, moe_router, moe_bias, moe_w_gu, moe_w_down, shared_w_gu, shared_w_down, norm_f):
    p = dict(norm1=norm1, w_in=w_in, mla_q_norm=mla_q_norm, mla_w_uq=mla_w_uq, mla_kv_norm=mla_kv_norm,
             mla_w_ukv=mla_w_ukv, gdn_conv=gdn_conv, gdn_a_log=gdn_a_log, gdn_dt_bias=gdn_dt_bias,
             gdn_norm=gdn_norm, rwkv_mu_prev=rwkv_mu_prev, rwkv_mu_next=rwkv_mu_next, rwkv_w0=rwkv_w0,
             rwkv_w2=rwkv_w2, rwkv_a0=rwkv_a0, rwkv_a2=rwkv_a2, rwkv_g2=rwkv_g2, rwkv_k_k=rwkv_k_k,
             rwkv_k_a=rwkv_k_a, rwkv_r_k=rwkv_r_k, rwkv_gn_w=rwkv_gn_w, rwkv_gn_b=rwkv_gn_b, w_out=w_out,
             norm2=norm2, moe_router=moe_router, moe_bias=moe_bias, moe_w_gu=moe_w_gu, moe_w_down=moe_w_down,
             shared_w_gu=shared_w_gu, shared_w_down=shared_w_down)
    weights = [_layer_weights(p, l) for l in range(DEPTH)]
    consts = _constants()
    nf = norm_f.reshape(1, D_MODEL)
    b_ctx, t_ctx, _ = x_prompt.shape
    b_lat, t_lat, _ = x_sample.shape

    cvec8 = jnp.concatenate([c_ctx[None, :], c, jnp.zeros((8 - 1 - b_lat, D_MODEL), F32)], axis=0)
    mods = _adaln(cvec8, ada_w, ada_b)

    xp, ctx_outs = _trunk(x_prompt.reshape(b_ctx * t_ctx, D_MODEL), mods, weights, consts, nf,
                          t_ctx, 0, None, tm=256, tq=t_ctx, tm_moe=1024)
    rc, rs = _rope_tables(t_lat)
    ckpe = jnp.pad(cache_mla_kpe, ((0, 0), (0, 0), (0, 0), (0, LANES - MLA_ROPE)))
    cache = (cache_mla_ckv, ckpe, rc, rs, _embed_block_diag_layers(state_gdn),
             _embed_block_diag_layers(jnp.swapaxes(state_rwkv, -1, -2)))
    xs, _ = _trunk(x_sample.reshape(b_lat * t_lat, D_MODEL), mods, weights, consts, nf,
                   t_lat, 1, cache, tm=256, tq=256, tm_moe=1024)

    y_prompt = xp.reshape(b_ctx, t_ctx, D_MODEL)
    y_sample = xs.reshape(b_lat, t_lat, D_MODEL)
    new_ckv = jnp.stack([o[0].reshape(b_ctx, t_ctx, KV_LORA) for o in ctx_outs], axis=1)
    new_kpe = jnp.stack([o[1].reshape(b_ctx, t_ctx, MLA_ROPE) for o in ctx_outs], axis=1)
    new_gdn = jnp.stack([o[2] for o in ctx_outs], axis=1)
    new_rwkv = jnp.stack([o[3] for o in ctx_outs], axis=1)
    return (y_prompt, y_sample, new_ckv, new_kpe, new_gdn, new_rwkv)


def _embed_block_diag_layers(s):
    b = s.shape[0]
    return _embed_block_diag(s.reshape(b * DEPTH, 2, HEADS, HEAD_DIM, HEAD_DIM)).reshape(
        b, DEPTH, 2, MIX_W, MIX_W)
```

```python
import functools

import numpy as np
import jax
import jax.numpy as jnp
from jax import lax
from jax.experimental import pallas as pl
from jax.experimental.pallas import tpu as pltpu

F32 = jnp.float32
BF16 = jnp.bfloat16

D_MODEL = 1024
BATCH = 32
SEQ = 256
DEPTH = 2
DEC_BATCH = 2
DEC_SEQ = 1024
PAST_LEN = 512
GRID_W = 64
NORM_EPS = 1e-6

MLA_HEADS = 4
MLA_NOPE = 128
MLA_ROPE = 64
MLA_V = 128
Q_LORA = 384
KV_LORA = 256
ROPE_BASE = 10000.0
MLA_SCALE = (MLA_NOPE + MLA_ROPE) ** -0.5

HEADS = 4
HEAD_DIM = 64
MIX_W = HEADS * HEAD_DIM
GDN_CONV_CH = 3 * MIX_W
CHUNK = 64
RWKV_GN_EPS = 64e-5

N_EXPERTS = 64
TOP_K = 8
N_GROUPS = 8
GROUP_SIZE = N_EXPERTS // N_GROUPS
TOPK_GROUPS = 4
D_EXPERT = 256
ROUTE_SCALE = 2.5

P_MLA = Q_LORA + KV_LORA + MLA_ROPE
P_GDN = GDN_CONV_CH + MIX_W + 4 * HEADS
P_RWKV = 3 * MIX_W + 128 + 128 + 128

LANES = 128
ZM_W = Q_LORA + KV_LORA + 2 * LANES
ZG_W = GDN_CONV_CH + MIX_W
ZR_W = P_RWKV
QH_W = 2 * LANES
VMEM_LIMIT = 56 * 1024 * 1024

_ROPE_SWAP = np.concatenate([np.arange(16, 32), np.arange(0, 16), np.arange(48, 64), np.arange(32, 48)])


def _sigmoid(x):
    return 1.0 / (1.0 + jnp.exp(-x))


def _silu(x):
    return x * _sigmoid(x)


def _softplus(x):
    return jnp.maximum(x, 0.0) + jnp.log(1.0 + jnp.exp(-jnp.abs(x)))


def _rms(x, g, eps=NORM_EPS):
    return x * lax.rsqrt(jnp.mean(x * x, axis=-1, keepdims=True) + eps) * g


def _mm(a, b):
    return jnp.dot(a.astype(BF16), b.astype(BF16), preferred_element_type=F32)


def _mm_nt(a, b):
    return lax.dot_general(a.astype(BF16), b.astype(BF16), (((1,), (1,)), ((), ())),
                           preferred_element_type=F32)


def _mm_tn(a, b):
    return lax.dot_general(a.astype(BF16), b.astype(BF16), (((0,), (0,)), ((), ())),
                           preferred_element_type=F32)


def _split3(x):
    p1 = x.astype(BF16)
    r1 = x - p1.astype(F32)
    p2 = r1.astype(BF16)
    r2 = r1 - p2.astype(F32)
    return p1, p2, r2.astype(BF16)


def _mm_sel_l(sel, x):
    p1, p2, p3 = _split3(x)
    return _mm(sel, p1) + _mm(sel, p2) + _mm(sel, p3)


def _mm_sel_r(x, sel):
    p1, p2, p3 = _split3(x)
    return _mm(p1, sel) + _mm(p2, sel) + _mm(p3, sel)


def _iota(shape, dim):
    return lax.broadcasted_iota(jnp.int32, shape, dim)


def _layer_spec(a, layer, **kw):
    nd = a.ndim - 1
    return pl.BlockSpec((None,) + a.shape[1:], lambda *_: (layer,) + (0,) * nd, **kw)


def _const_spec(a, **kw):
    return pl.BlockSpec(a.shape, lambda *_: (0,) * a.ndim, **kw)


def _bd(x, maskbd):
    xb = x.astype(BF16)
    return jnp.concatenate([xb] * HEADS, axis=0) * maskbd


def _chunk_masks(rev):
    row = _iota((CHUNK, MIX_W), 0)
    col = jnp.bitwise_and(_iota((CHUNK, MIX_W), 1), HEAD_DIM - 1)
    r2 = _iota((CHUNK, CHUNK), 0)
    c2 = _iota((CHUNK, CHUNK), 1)
    if rev:
        inc, strict, tri = row <= col, row < col, r2 <= c2
    else:
        inc, strict, tri = row >= col, row > col, r2 >= c2
    eye = jnp.where(row == col, 1.0, 0.0).astype(F32)
    return inc, strict, jnp.where(tri, 1.0, 0.0).astype(BF16), eye


def _split2(x):
    hi = x.astype(BF16)
    return hi, (x - hi.astype(F32)).astype(BF16)


def _mm_bd3(x, p, maskbd):
    n = x.shape[0]
    xh, xl = _split2(x)
    ph, pl_ = _split2(p)
    r = jnp.dot(jnp.concatenate([xh, xl], axis=0), _bd(ph, maskbd), preferred_element_type=F32)
    return r[:n] + r[n:] + jnp.dot(xh, _bd(pl_, maskbd), preferred_element_type=F32)


def _neumann_inverse(a_list, eye_list, maskbd):
    bs = [-a for a in a_list]
    ms = [eye + b for eye, b in zip(eye_list, bs)]
    ps = [_mm_bd3(b, b, maskbd) for b in bs]
    for _ in range(4):
        boths = [_mm_bd3(jnp.concatenate([m, p], axis=0), p, maskbd) for m, p in zip(ms, ps)]
        ms = [m + both[:CHUNK] for m, both in zip(ms, boths)]
        ps = [both[CHUNK:] for both in boths]
    return [m + _mm_bd3(m, p, maskbd) for m, p in zip(ms, ps)]


def _adaln_kernel(c_ref, w_ref, b_ref, o_ref):
    cv = c_ref[...]
    o_ref[0] = _mm(_silu(cv), w_ref[0]) + b_ref[0]


def _adaln(cvec8, ada_w, ada_b):
    tn = 768
    n_out = 6 * D_MODEL
    return pl.pallas_call(
        _adaln_kernel,
        grid=(DEPTH, n_out // tn),
        in_specs=[
            pl.BlockSpec((8, D_MODEL), lambda l, j: (0, 0)),
            pl.BlockSpec((1, D_MODEL, tn), lambda l, j: (l, 0, j)),
            pl.BlockSpec((1, 1, tn), lambda l, j: (l, 0, j)),
        ],
        out_specs=pl.BlockSpec((1, 8, tn), lambda l, j: (l, 0, j)),
        out_shape=jax.ShapeDtypeStruct((DEPTH, 8, n_out), F32),
        compiler_params=pltpu.CompilerParams(dimension_semantics=("arbitrary", "arbitrary"),
                                             vmem_limit_bytes=VMEM_LIMIT),
        name="adaln",
    )(cvec8, ada_w, ada_b.reshape(DEPTH, 1, n_out))


def _inproj_kernel(x_ref, mod_ref, n1_ref, w_ref, zm_ref, zg_ref, zab_ref, zr_ref):
    m = mod_ref[0]
    sh = m[:, 0:D_MODEL]
    sc = m[:, D_MODEL:2 * D_MODEL]
    h = _rms(x_ref[...], n1_ref[...]) * (1.0 + sc) + sh
    z = _mm(h, w_ref[...])
    o1 = ZM_W
    o2 = o1 + ZG_W
    o3 = o2 + LANES
    zm_ref[...] = z[:, :o1]
    zg_ref[...] = z[:, o1:o2]
    zab_ref[...] = z[:, o2:o3]
    zr_ref[...] = z[:, o3:]


def _inproj(x2d, mods, lw, layer, tm, seq_len, mod_base):
    n = x2d.shape[0]
    tiles_per_seq = seq_len // tm if mod_base else 1

    def mod_idx(i):
        return (layer, mod_base + i // tiles_per_seq if mod_base else 0, 0, 0)

    return pl.pallas_call(
        _inproj_kernel,
        grid=(n // tm,),
        in_specs=[
            pl.BlockSpec((tm, D_MODEL), lambda i: (i, 0)),
            pl.BlockSpec((None, 1, 1, 6 * D_MODEL), mod_idx),
            _layer_spec(lw["norm1"], layer),
            _layer_spec(lw["w_in"], layer),
        ],
        out_specs=[
            pl.BlockSpec((tm, ZM_W), lambda i: (i, 0)),
            pl.BlockSpec((tm, ZG_W), lambda i: (i, 0)),
            pl.BlockSpec((tm, LANES), lambda i: (i, 0)),
            pl.BlockSpec((tm, ZR_W), lambda i: (i, 0)),
        ],
        out_shape=[
            jax.ShapeDtypeStruct((n, ZM_W), F32),
            jax.ShapeDtypeStruct((n, ZG_W), F32),
            jax.ShapeDtypeStruct((n, LANES), F32),
            jax.ShapeDtypeStruct((n, ZR_W), F32),
        ],
        compiler_params=pltpu.CompilerParams(dimension_semantics=("arbitrary",),
                                             vmem_limit_bytes=VMEM_LIMIT),
        name="inproj",
    )(x2d, mods, lw["norm1"], lw["w_in"])


def _mla_kernel(*refs, seq_len, tq, past, cached):
    if cached:
        (zm_ref, cckv_ref, ckpe_ref, rc_ref, rs_ref, qn_ref, wuq_ref, wuqs_ref, kvn_ref, wukv_ref,
         o_ref, ckv_ref, k_s, v_s) = refs
    else:
        (zm_ref, qn_ref, wuq_ref, kvn_ref, wukv_ref, o_ref, ckv_ref, k_s, v_s) = refs
    qi = pl.program_id(1)
    o_kpe = Q_LORA + KV_LORA

    @pl.when(qi == 0)
    def _():
        zm = zm_ref[...]
        ckv = _rms(zm[:, Q_LORA:o_kpe], kvn_ref[...])
        ckv_ref[...] = ckv
        kpe = zm[:, o_kpe:o_kpe + LANES]
        if cached:
            kpe = kpe * rc_ref[...] + zm[:, o_kpe + LANES:o_kpe + 2 * LANES] * rs_ref[...]
            kvc = _mm(cckv_ref[0], wukv_ref[...])
            kpc = ckpe_ref[0].astype(BF16)
        kv = _mm(ckv, wukv_ref[...])
        kpe = kpe.astype(BF16)
        for h in range(MLA_HEADS):
            c0 = h * QH_W
            if cached:
                k_s[0:past, c0:c0 + LANES] = kvc[:, c0:c0 + LANES].astype(BF16)
                k_s[0:past, c0 + LANES:c0 + QH_W] = kpc
                v_s[0:past, h * MLA_V:(h + 1) * MLA_V] = kvc[:, c0 + LANES:c0 + QH_W].astype(BF16)
            k_s[past:past + seq_len, c0:c0 + LANES] = kv[:, c0:c0 + LANES].astype(BF16)
            k_s[past:past + seq_len, c0 + LANES:c0 + QH_W] = kpe
            v_s[past:past + seq_len, h * MLA_V:(h + 1) * MLA_V] = kv[:, c0 + LANES:c0 + QH_W].astype(BF16)

    r0 = pl.multiple_of(qi * tq, tq)
    zq = zm_ref[pl.ds(r0, tq), :]
    cq = _rms(zq[:, :Q_LORA], qn_ref[...])
    q = _mm(cq, wuq_ref[...])
    if cached:
        qs = _mm(cq, wuqs_ref[...])
        ones = jnp.ones((tq, LANES), F32)
        zeros = jnp.zeros((tq, LANES), F32)
        qc = jnp.concatenate([ones, rc_ref[pl.ds(r0, tq), :]], axis=1)
        qsn = jnp.concatenate([zeros, rs_ref[pl.ds(r0, tq), :]], axis=1)
    for h in range(MLA_HEADS):
        c0 = h * QH_W
        qh = q[:, c0:c0 + QH_W]
        if cached:
            qh = qh * qc + qs[:, c0:c0 + QH_W] * qsn
        s = _mm_nt(qh, k_s[:, c0:c0 + QH_W]) * MLA_SCALE
        e = jnp.exp(s - jnp.max(s, axis=-1, keepdims=True))
        den = jnp.sum(e, axis=-1, keepdims=True)
        o_ref[:, h * MLA_V:(h + 1) * MLA_V] = _mm(e, v_s[:, h * MLA_V:(h + 1) * MLA_V]) / den


def _mla(zm, lw, layer, seq_len, tq, cache=None):
    n = zm.shape[0]
    nb = n // seq_len
    cached = cache is not None
    past = PAST_LEN if cached else 0
    tk = past + seq_len
    seq_spec = lambda w: pl.BlockSpec((seq_len, w), lambda b, q: (b, 0))
    lay = lambda a: _layer_spec(a, layer)
    if cached:
        cckv, ckpe, rc, rs = cache
        args = [zm, cckv, ckpe, rc, rs, lw["q_norm"], lw["w_uq"], lw["w_uq_sw"], lw["kv_norm"], lw["w_ukv"]]
        in_specs = [seq_spec(ZM_W),
                    pl.BlockSpec((1, None, past, KV_LORA), lambda b, q: (b, layer, 0, 0)),
                    pl.BlockSpec((1, None, past, LANES), lambda b, q: (b, layer, 0, 0)),
                    _const_spec(rc), _const_spec(rs)] + [lay(a) for a in args[5:]]
    else:
        args = [zm, lw["q_norm"], lw["w_uq"], lw["kv_norm"], lw["w_ukv"]]
        in_specs = [seq_spec(ZM_W)] + [lay(a) for a in args[1:]]
    return pl.pallas_call(
        functools.partial(_mla_kernel, seq_len=seq_len, tq=tq, past=past, cached=cached),
        grid=(nb, seq_len // tq),
        in_specs=in_specs,
        out_specs=[
            pl.BlockSpec((tq, MLA_HEADS * MLA_V), lambda b, q: (b * (seq_len // tq) + q, 0)),
            pl.BlockSpec((seq_len, KV_LORA), lambda b, q: (b, 0)),
        ],
        out_shape=[
            jax.ShapeDtypeStruct((n, MLA_HEADS * MLA_V), F32),
            jax.ShapeDtypeStruct((n, KV_LORA), F32),
        ],
        scratch_shapes=[
            pltpu.VMEM((tk, MLA_HEADS * QH_W), BF16),
            pltpu.VMEM((tk, MLA_HEADS * MLA_V), BF16),
        ],
        compiler_params=pltpu.CompilerParams(dimension_semantics=("arbitrary", "arbitrary"),
                                             vmem_limit_bytes=VMEM_LIMIT),
        name="mla_lat" if cached else "mla_ctx",
    )(*args)


CHUNK_GROUP = 4
GDN_PRE_W = 5 * MIX_W


def _for_chunk_groups(n_chunks, fn):
    if n_chunks == CHUNK_GROUP:
        fn(0)
    else:
        def body(gi, carry):
            fn(gi * CHUNK_GROUP)
            return carry
        lax.fori_loop(0, n_chunks // CHUNK_GROUP, body, 0)


def _gdn_prepare(items, maskbd):
    n = range(len(items))
    qs, ks, vs, gs, betas, masks, revs = zip(*items)
    gcs = [_mm_sel_l(masks[i][2], gs[i]) for i in n]
    decays = []
    for i in n:
        inc, eye = masks[i][0], masks[i][3]
        gc_row = jnp.sum(eye * gcs[i], axis=0, keepdims=True)
        decays.append(jnp.where(inc, jnp.exp(jnp.where(inc, gcs[i] - gc_row, 0.0)), 0.0))
    kbs = [ks[i] * betas[i] for i in n]
    aqs = [_mm_nt(jnp.concatenate([kbs[i], qs[i]], axis=0), _bd(ks[i], maskbd)) for i in n]
    a_mats = [jnp.where(masks[i][1], aqs[i][:CHUNK] * decays[i], 0.0) for i in n]
    t_invs = _neumann_inverse(a_mats, [m[3] for m in masks], maskbd)
    egcs = [jnp.exp(gc) for gc in gcs]
    uws = [_mm(t_invs[i], jnp.concatenate([_bd(vs[i] * betas[i], maskbd), _bd(kbs[i] * egcs[i], maskbd)], axis=1))
           for i in n]
    out = []
    for i in n:
        g_last = gcs[i][0:1] if revs[i] else gcs[i][CHUNK - 1:CHUNK]
        pre = jnp.concatenate([uws[i], qs[i] * egcs[i], aqs[i][CHUNK:] * decays[i],
                               ks[i] * jnp.exp(g_last - gcs[i])], axis=1)
        out.append((pre, jnp.broadcast_to(jnp.exp(g_last), (8, MIX_W))))
    return out


def _gdn_step(pre, egl, s_bd, maskbd):
    w = MIX_W
    u, wq_in, qk, kdec = pre[:, :w], pre[:, w:3 * w], pre[:, 3 * w:4 * w], pre[:, 4 * w:]
    wq = _mm(jnp.concatenate([wq_in[:, :w], wq_in[:, w:]], axis=0), s_bd)
    v_new = u - wq[:CHUNK]
    o = wq[CHUNK:] + _mm(qk, _bd(v_new, maskbd))
    s_new = s_bd * egl + _mm_tn(kdec, v_new) * maskbd.astype(F32)
    return o, s_new


def _gdn_kernel(*refs, seq_len, cached):
    if cached:
        (zg_ref, zab_ref, s0_ref, conv_ref, alog_ref, dtb_ref, gn_ref, eexp_ref, ones_ref, maskbd_ref,
         o_ref, sout_ref, q_s, k_s, v_s, ge_s, pre_s, gl_s, oacc_s, st_s) = refs
    else:
        (zg_ref, zab_ref, conv_ref, alog_ref, dtb_ref, gn_ref, eexp_ref, ones_ref, maskbd_ref,
         o_ref, sout_ref, q_s, k_s, v_s, ge_s, pre_s, gl_s, oacc_s, st_s) = refs
    t = seq_len
    z = zg_ref[:, :GDN_CONV_CH]
    rowi = _iota((t, 1), 0)
    zp = jnp.where(rowi == 0, 0.0, pltpu.roll(z, 1, 0))
    zn = jnp.where(rowi == t - 1, 0.0, pltpu.roll(z, t - 1, 0))
    cw = conv_ref[...]
    qkv = _silu(zp * cw[0:1] + z * cw[1:2] + zn * cw[2:3])
    ones_bd = ones_ref[...]
    q = qkv[:, :MIX_W]
    k = qkv[:, MIX_W:2 * MIX_W]
    q_s[...] = q * lax.rsqrt(_mm_sel_r(q * q, ones_bd) + 1e-6) * (HEAD_DIM ** -0.5)
    k_s[...] = k * lax.rsqrt(_mm_sel_r(k * k, ones_bd) + 1e-6)
    v_s[...] = qkv[:, 2 * MIX_W:]
    ab = zab_ref[...]
    lane = _iota((t, LANES), 1)
    gb = jnp.where(lane < 2 * HEADS, -jnp.exp(alog_ref[...]) * _softplus(ab + dtb_ref[...]), _sigmoid(ab))
    ge_s[...] = _mm_sel_r(gb, eexp_ref[...])
    oacc_s[...] = jnp.zeros((t, MIX_W), F32)
    if cached:
        st_s[...] = s0_ref[0]
    else:
        st_s[...] = jnp.zeros((2, MIX_W, MIX_W), F32)
    maskbd = maskbd_ref[...]
    masks = (_chunk_masks(False), _chunk_masks(True))
    n_chunks = t // CHUNK

    def prepare_group(c0):
        where, items = [], []
        for j in range(CHUNK_GROUP):
            c = c0 + j
            rows = pl.ds(pl.multiple_of(c * CHUNK, CHUNK), CHUNK)
            for d in range(2):
                where.append((d, c, rows))
                items.append((q_s[rows, :], k_s[rows, :], v_s[rows, :], ge_s[rows, d * MIX_W:(d + 1) * MIX_W],
                              ge_s[rows, (2 + d) * MIX_W:(3 + d) * MIX_W], masks[d], d == 1))
        for (d, c, rows), (pre, egl) in zip(where, _gdn_prepare(items, maskbd)):
            pre_s[d, rows, :] = pre
            gl_s[d, pl.ds(pl.multiple_of(c * 8, 8), 8), :] = egl

    _for_chunk_groups(n_chunks, prepare_group)

    def body(i, carry):
        for d in range(2):
            c = i if d == 0 else n_chunks - 1 - i
            rows = pl.ds(pl.multiple_of(c * CHUNK, CHUNK), CHUNK)
            egl = gl_s[d, pl.ds(pl.multiple_of(c * 8, 8), 8), :][0:1]
            o, s_new = _gdn_step(pre_s[d, rows, :], egl, st_s[d], maskbd)
            oacc_s[rows, :] = oacc_s[rows, :] + o
            st_s[d] = s_new
        return carry

    lax.fori_loop(0, n_chunks, body, 0)
    o = oacc_s[...]
    ms = _mm_sel_r(o * o, ones_bd) * (1.0 / HEAD_DIM)
    gate = zg_ref[:, GDN_CONV_CH:]
    o_ref[...] = o * lax.rsqrt(ms + NORM_EPS) * gn_ref[...] * _silu(gate)
    for d in range(2):
        for h in range(HEADS):
            sl = slice(h * HEAD_DIM, (h + 1) * HEAD_DIM)
            sout_ref[0, d, h] = st_s[d, sl, sl]


def _gdn(zg, zab, lw, layer, consts, seq_len, s0_bd=None):
    n = zg.shape[0]
    nb = n // seq_len
    cached = s0_bd is not None
    args = [zg, zab]
    in_specs = [pl.BlockSpec((seq_len, ZG_W), lambda b: (b, 0)),
                pl.BlockSpec((seq_len, LANES), lambda b: (b, 0))]
    if cached:
        args.append(s0_bd)
        in_specs.append(pl.BlockSpec((1, None, 2, MIX_W, MIX_W), lambda b: (b, layer, 0, 0, 0)))
    layered = [lw["gdn_conv"], lw["gdn_alog"], lw["gdn_dtb"], lw["gdn_norm"]]
    const = [consts["eexp"], consts["ones_bd"], consts["maskbd"]]
    args += layered + const
    in_specs += [_layer_spec(a, layer) for a in layered] + [_const_spec(a) for a in const]
    return pl.pallas_call(
        functools.partial(_gdn_kernel, seq_len=seq_len, cached=cached),
        grid=(nb,),
        in_specs=in_specs,
        out_specs=[
            pl.BlockSpec((seq_len, MIX_W), lambda b: (b, 0)),
            pl.BlockSpec((1, 2, HEADS, HEAD_DIM, HEAD_DIM), lambda b: (b, 0, 0, 0, 0)),
        ],
        out_shape=[
            jax.ShapeDtypeStruct((n, MIX_W), F32),
            jax.ShapeDtypeStruct((nb, 2, HEADS, HEAD_DIM, HEAD_DIM), F32),
        ],
        scratch_shapes=[
            pltpu.VMEM((seq_len, MIX_W), F32),
            pltpu.VMEM((seq_len, MIX_W), F32),
            pltpu.VMEM((seq_len, MIX_W), F32),
            pltpu.VMEM((seq_len, 4 * MIX_W), F32),
            pltpu.VMEM((2, seq_len, GDN_PRE_W), F32),
            pltpu.VMEM((2, seq_len // CHUNK * 8, MIX_W), F32),
            pltpu.VMEM((seq_len, MIX_W), F32),
            pltpu.VMEM((2, MIX_W, MIX_W), F32),
        ],
        compiler_params=pltpu.CompilerParams(dimension_semantics=("arbitrary",),
                                             vmem_limit_bytes=VMEM_LIMIT),
        name="gdn_lat" if cached else "gdn_ctx",
    )(*args)


RWKV_PRE_W = 7 * MIX_W


def _rwkv_prepare(items, maskbd, eye_full):
    n = range(len(items))
    rs, kds, vs, kks, bs, lws, masks, revs = zip(*items)
    cums = [_mm_sel_l(masks[i][2], lws[i]) for i in n]
    einvs = [jnp.exp(-c) for c in cums]
    kts = [kks[i] * jnp.exp(cums[i] - lws[i]) for i in n]
    rts = [rs[i] * jnp.exp(cums[i]) for i in n]
    krs = [jnp.concatenate([kts[i], rts[i]], axis=0) for i in n]
    lb_alls = [_mm_nt(krs[i], _bd(bs[i] * einvs[i], maskbd)) for i in n]
    lk_alls = [_mm_nt(krs[i], _bd(kds[i] * einvs[i], maskbd)) for i in n]
    lbs = [jnp.where(masks[i][1], lb_alls[i][:CHUNK], 0.0) for i in n]
    t_invs = _neumann_inverse(lbs, [m[3] for m in masks], maskbd)
    lvs = [_mm(jnp.concatenate([jnp.where(masks[i][1], lk_alls[i][:CHUNK], 0.0),
                                jnp.where(masks[i][0], lk_alls[i][CHUNK:], 0.0)], axis=0), _bd(vs[i], maskbd))
           for i in n]
    tkps = [_mm(t_invs[i], jnp.concatenate([_bd(kts[i], maskbd), _bd(lvs[i][:CHUNK], maskbd)], axis=1)) for i in n]
    out = []
    for i in n:
        c_last = cums[i][0:1] if revs[i] else cums[i][CHUNK - 1:CHUNK]
        tail = jnp.exp(c_last - cums[i])
        rb = jnp.where(masks[i][0], lb_alls[i][CHUNK:], 0.0)
        pre = jnp.concatenate([tkps[i][:, :MIX_W], rts[i], tkps[i][:, MIX_W:], lvs[i][CHUNK:], rb,
                               kds[i] * tail, bs[i] * tail], axis=1)
        gcol = jnp.sum(eye_full * jnp.exp(c_last), axis=1, keepdims=True)
        out.append((pre, jnp.broadcast_to(gcol, (MIX_W, LANES))))
    return out


def _rwkv_step(pre, v, gcol, z_bd, maskbd):
    w = MIX_W
    pr = _mm(jnp.concatenate([pre[:, :w], pre[:, w:2 * w]], axis=0), z_bd)
    p = pr[:CHUNK] + pre[:, 2 * w:3 * w]
    o = pr[CHUNK:] + pre[:, 3 * w:4 * w] - _mm(pre[:, 4 * w:5 * w], _bd(p, maskbd))
    upd = _mm_tn(jnp.concatenate([pre[:, 5 * w:6 * w], pre[:, 6 * w:]], axis=0), jnp.concatenate([v, -p], axis=0))
    z_new = z_bd * jnp.concatenate([gcol, gcol], axis=1) + upd * maskbd.astype(F32)
    return o, z_new


def _rwkv_kernel(*refs, seq_len, cached):
    if cached:
        (zr_ref, s0_ref, mup_ref, mun_ref, w0_ref, w2_ref, a0_ref, a2_ref, g2_ref, kk_ref, ka_ref, rk_ref,
         gnw_ref, gnb_ref, ones_ref, maskbd_ref, o_ref, sout_ref,
         r_s, v_s, kk_s, dir_s, bg_s, pre_s, gcol_s, oacc_s, st_s) = refs
    else:
        (zr_ref, mup_ref, mun_ref, w0_ref, w2_ref, a0_ref, a2_ref, g2_ref, kk_ref, ka_ref, rk_ref,
         gnw_ref, gnb_ref, ones_ref, maskbd_ref, o_ref, sout_ref,
         r_s, v_s, kk_s, dir_s, bg_s, pre_s, gcol_s, oacc_s, st_s) = refs
    t = seq_len
    z = zr_ref[...]
    rowi = _iota((t, 1), 0)
    zp = jnp.where(rowi == 0, 0.0, pltpu.roll(z, 1, 0))
    zn = jnp.where(rowi == t - 1, 0.0, pltpu.roll(z, t - 1, 0))
    z = z + mup_ref[...] * (zp - z) + mun_ref[...] * (zn - z)
    w = MIX_W
    r = z[:, :w]
    k = z[:, w:2 * w]
    v = z[:, 2 * w:3 * w]
    wd = jnp.tanh(z[:, 3 * w:3 * w + LANES])
    ad = z[:, 3 * w + LANES:3 * w + 2 * LANES]
    gd = _sigmoid(z[:, 3 * w + 2 * LANES:])
    ones_bd = ones_ref[...]
    kk = k * kk_ref[...]
    kk = kk * lax.rsqrt(_mm_sel_r(kk * kk, ones_bd) + 1e-6)
    r_s[...] = r
    v_s[...] = v
    kk_s[...] = kk
    bonus = jnp.zeros((t, w), F32)
    for d in range(2):
        w_log = -_softplus(-(w0_ref[d:d + 1] + _mm(wd, w2_ref[d]))) - 0.5
        a = _sigmoid(a0_ref[d:d + 1] + _mm(ad, a2_ref[d]))
        kd = k * (1.0 + (a - 1.0) * ka_ref[...])
        dir_s[:, (3 * d) * w:(3 * d + 1) * w] = -jnp.exp(w_log)
        dir_s[:, (3 * d + 1) * w:(3 * d + 2) * w] = kd
        dir_s[:, (3 * d + 2) * w:(3 * d + 3) * w] = kk * a
        bonus = bonus + _mm_sel_r(r * kd * rk_ref[...], ones_bd) * v
    bg_s[:, :w] = bonus
    bg_s[:, w:] = _mm(gd, g2_ref[...])
    oacc_s[...] = jnp.zeros((t, w), F32)
    if cached:
        st_s[...] = s0_ref[0]
    else:
        st_s[...] = jnp.zeros((2, w, w), F32)
    maskbd = maskbd_ref[...]
    masks = (_chunk_masks(False), _chunk_masks(True))
    eye_full = jnp.where(_iota((w, w), 0) == _iota((w, w), 1), 1.0, 0.0).astype(F32)
    n_chunks = t // CHUNK

    def prepare_group(c0):
        where, items = [], []
        for j in range(CHUNK_GROUP):
            c = c0 + j
            rows = pl.ds(pl.multiple_of(c * CHUNK, CHUNK), CHUNK)
            for d in range(2):
                where.append((d, c, rows))
                items.append((r_s[rows, :], dir_s[rows, (3 * d + 1) * w:(3 * d + 2) * w], v_s[rows, :], kk_s[rows, :],
                              dir_s[rows, (3 * d + 2) * w:(3 * d + 3) * w], dir_s[rows, (3 * d) * w:(3 * d + 1) * w],
                              masks[d], d == 1))
        for (d, c, rows), (pre, gcol) in zip(where, _rwkv_prepare(items, maskbd, eye_full)):
            pre_s[d, rows, :] = pre
            gcol_s[d, pl.ds(pl.multiple_of(c * w, w), w), :] = gcol

    _for_chunk_groups(n_chunks, prepare_group)

    def body(i, carry):
        for d in range(2):
            c = i if d == 0 else n_chunks - 1 - i
            rows = pl.ds(pl.multiple_of(c * CHUNK, CHUNK), CHUNK)
            gcol = gcol_s[d, pl.ds(pl.multiple_of(c * w, w), w), :]
            o, z_new = _rwkv_step(pre_s[d, rows, :], v_s[rows, :], gcol, st_s[d], maskbd)
            oacc_s[rows, :] = oacc_s[rows, :] + o
            st_s[d] = z_new
        return carry

    lax.fori_loop(0, n_chunks, body, 0)
    o = oacc_s[...]
    inv_n = 1.0 / HEAD_DIM
    mu = _mm_sel_r(o, ones_bd) * inv_n
    oc = o - mu
    var = _mm_sel_r(oc * oc, ones_bd) * inv_n
    y = oc * lax.rsqrt(var + RWKV_GN_EPS) * gnw_ref[...] + gnb_ref[...]
    o_ref[...] = (y + bg_s[:, :w]) * bg_s[:, w:]
    for d in range(2):
        for h in range(HEADS):
            sl = slice(h * HEAD_DIM, (h + 1) * HEAD_DIM)
            sout_ref[0, d, h] = st_s[d, sl, sl]


def _rwkv(zr, lw, layer, consts, seq_len, s0_bd=None):
    n = zr.shape[0]
    nb = n // seq_len
    cached = s0_bd is not None
    args = [zr]
    in_specs = [pl.BlockSpec((seq_len, ZR_W), lambda b: (b, 0))]
    if cached:
        args.append(s0_bd)
        in_specs.append(pl.BlockSpec((1, None, 2, MIX_W, MIX_W), lambda b: (b, layer, 0, 0, 0)))
    layered = [lw["rwkv_mu_prev"], lw["rwkv_mu_next"], lw["rwkv_w0"], lw["rwkv_w2"], lw["rwkv_a0"], lw["rwkv_a2"],
               lw["rwkv_g2"], lw["rwkv_k_k"], lw["rwkv_k_a"], lw["rwkv_r_k"], lw["rwkv_gn_w"], lw["rwkv_gn_b"]]
    const = [consts["ones_bd"], consts["maskbd"]]
    args += layered + const
    in_specs += [_layer_spec(a, layer) for a in layered] + [_const_spec(a) for a in const]
    return pl.pallas_call(
        functools.partial(_rwkv_kernel, seq_len=seq_len, cached=cached),
        grid=(nb,),
        in_specs=in_specs,
        out_specs=[
            pl.BlockSpec((seq_len, MIX_W), lambda b: (b, 0)),
            pl.BlockSpec((1, 2, HEADS, HEAD_DIM, HEAD_DIM), lambda b: (b, 0, 0, 0, 0)),
        ],
        out_shape=[
            jax.ShapeDtypeStruct((n, MIX_W), F32),
            jax.ShapeDtypeStruct((nb, 2, HEADS, HEAD_DIM, HEAD_DIM), F32),
        ],
        scratch_shapes=[
            pltpu.VMEM((seq_len, MIX_W), F32),
            pltpu.VMEM((seq_len, MIX_W), F32),
            pltpu.VMEM((seq_len, MIX_W), F32),
            pltpu.VMEM((seq_len, 6 * MIX_W), F32),
            pltpu.VMEM((seq_len, 2 * MIX_W), F32),
            pltpu.VMEM((2, seq_len, RWKV_PRE_W), F32),
            pltpu.VMEM((2, seq_len // CHUNK * MIX_W, LANES), F32),
            pltpu.VMEM((seq_len, MIX_W), F32),
            pltpu.VMEM((2, MIX_W, MIX_W), F32),
        ],
        compiler_params=pltpu.CompilerParams(dimension_semantics=("arbitrary",),
                                             vmem_limit_bytes=VMEM_LIMIT),
        name="rwkv_lat" if cached else "rwkv_ctx",
    )(*args)


def _route(logits_t, bias):
    tm = logits_t.shape[1]
    neg = -jnp.inf
    sc = _sigmoid(logits_t)
    sel = (sc + bias).reshape(N_GROUPS, GROUP_SIZE, tm)
    si = _iota(sel.shape, 1).astype(F32)
    m1 = jnp.max(sel, axis=1, keepdims=True)
    f1 = jnp.min(jnp.where(sel == m1, si, float(GROUP_SIZE)), axis=1, keepdims=True)
    m2 = jnp.max(jnp.where(si == f1, neg, sel), axis=1, keepdims=True)
    grp = m1 + m2
    gi = _iota(grp.shape, 0).astype(F32)
    gsel = jnp.zeros(grp.shape, F32)
    for _ in range(TOPK_GROUPS):
        mx = jnp.max(grp, axis=0, keepdims=True)
        fi = jnp.min(jnp.where(grp == mx, gi, float(N_GROUPS)), axis=0, keepdims=True)
        hit = gi == fi
        gsel = jnp.where(hit, 1.0, gsel)
        grp = jnp.where(hit, neg, grp)
    cur = jnp.where(gsel > 0.0, sel, neg)
    ei = (_iota(cur.shape, 0) * GROUP_SIZE + _iota(cur.shape, 1)).astype(F32)
    chosen = jnp.zeros(cur.shape, F32)
    for _ in range(TOP_K):
        mx = jnp.max(jnp.max(cur, axis=0, keepdims=True), axis=1, keepdims=True)
        fi = jnp.min(jnp.min(jnp.where(cur == mx, ei, float(N_EXPERTS)), axis=0, keepdims=True),
                     axis=1, keepdims=True)
        hit = ei == fi
        chosen = jnp.where(hit, 1.0, chosen)
        cur = jnp.where(hit, neg, cur)
    wsel = chosen.reshape(N_EXPERTS, tm) * sc
    return wsel / jnp.sum(wsel, axis=0, keepdims=True) * ROUTE_SCALE


def _post_kernel(x_ref, om_ref, og_ref, or_ref, mod_ref, wo_ref, n2_ref, rt_ref, rb_ref,
                 x1_ref, h2_ref, cw_ref):
    m = mod_ref[0]
    g1 = m[:, 2 * D_MODEL:3 * D_MODEL]
    sh2 = m[:, 3 * D_MODEL:4 * D_MODEL]
    sc2 = m[:, 4 * D_MODEL:5 * D_MODEL]
    w_mla = MLA_HEADS * MLA_V
    mix = (_mm(om_ref[...], wo_ref[0:w_mla, :]) + _mm(og_ref[...], wo_ref[w_mla:w_mla + MIX_W, :])
           + _mm(or_ref[...], wo_ref[w_mla + MIX_W:, :]))
    x1 = x_ref[...] + g1 * mix
    x1_ref[...] = x1
    h2 = _rms(x1, n2_ref[...]) * (1.0 + sc2) + sh2
    h2_ref[...] = h2.astype(BF16)
    r_hi, r_lo = _split2(rt_ref[...])
    h_hi, h_lo = _split2(h2)
    logits_t = _mm_nt(r_hi, h_hi) + _mm_nt(r_hi, h_lo) + _mm_nt(r_lo, h_hi)
    cw_t = _route(logits_t, rb_ref[...])
    cw_ref[...] = jnp.concatenate([cw_t, jnp.zeros_like(cw_t)], axis=0).T


def _post(x2d, om, og, orw, mods, lw, layer, tm, seq_len, mod_base):
    n = x2d.shape[0]
    tiles_per_seq = seq_len // tm if mod_base else 1

    def mod_idx(i):
        return (layer, mod_base + i // tiles_per_seq if mod_base else 0, 0, 0)

    row = lambda w: pl.BlockSpec((tm, w), lambda i: (i, 0))
    full = lambda a: _layer_spec(a, layer)
    tail = [lw["w_out"], lw["norm2"], lw["router_t"], lw["router_b"]]
    return pl.pallas_call(
        _post_kernel,
        grid=(n // tm,),
        in_specs=[row(D_MODEL), row(MLA_HEADS * MLA_V), row(MIX_W), row(MIX_W),
                  pl.BlockSpec((None, 1, 1, 6 * D_MODEL), mod_idx)] + [full(a) for a in tail],
        out_specs=[row(D_MODEL), row(D_MODEL), row(LANES)],
        out_shape=[
            jax.ShapeDtypeStruct((n, D_MODEL), F32),
            jax.ShapeDtypeStruct((n, D_MODEL), BF16),
            jax.ShapeDtypeStruct((n, LANES), F32),
        ],
        compiler_params=pltpu.CompilerParams(dimension_semantics=("arbitrary",),
                                             vmem_limit_bytes=VMEM_LIMIT),
        name="post",
    )(x2d, om, og, orw, mods, *tail)


def _swiglu_act(gu):
    return _silu(gu[:, :D_EXPERT]) * gu[:, D_EXPERT:]


MOE_EXPERTS_PER_STEP = 2


def _moe_kernel(h_ref, x1_ref, cw_ref, mod_ref, wgu_ref, wdn_ref, sgu_ref, sdn_ref, nf_ref, o_ref, *, final):
    step = pl.program_id(1)
    h = h_ref[...]

    @pl.when(step == 0)
    def _():
        o_ref[...] = _mm(_swiglu_act(_mm(h, sgu_ref[...])), sdn_ref[...])

    cw = cw_ref[...]
    lane = _iota(cw.shape, 1)
    acts = []
    for g in range(MOE_EXPERTS_PER_STEP):
        act = _swiglu_act(_mm(h, wgu_ref[g]))
        scale = jnp.sum(jnp.where(lane == step * MOE_EXPERTS_PER_STEP + g, cw, 0.0), axis=-1, keepdims=True)
        acts.append((act * scale).astype(BF16))
    wdn = wdn_ref[...].reshape(MOE_EXPERTS_PER_STEP * D_EXPERT, D_MODEL)
    o_ref[...] += _mm(jnp.concatenate(acts, axis=1), wdn)

    @pl.when(step == N_EXPERTS // MOE_EXPERTS_PER_STEP - 1)
    def _():
        g2 = mod_ref[0][:, 5 * D_MODEL:]
        x2 = x1_ref[...] + g2 * o_ref[...]
        if final:
            x2 = _rms(x2, nf_ref[...])
        o_ref[...] = x2


def _moe(h2, x1, cw, mods, lw, layer, norm_f, tm, seq_len, mod_base, final):
    n = h2.shape[0]
    tiles_per_seq = seq_len // tm if mod_base else 1
    g = MOE_EXPERTS_PER_STEP

    def mod_idx(i, e):
        return (layer, mod_base + i // tiles_per_seq if mod_base else 0, 0, 0)

    row = lambda w, **kw: pl.BlockSpec((tm, w), lambda i, e: (i, 0), **kw)
    full = lambda a: _layer_spec(a, layer, pipeline_mode=pl.Buffered(1))
    return pl.pallas_call(
        functools.partial(_moe_kernel, final=final),
        grid=(n // tm, N_EXPERTS // g),
        in_specs=[row(D_MODEL), row(D_MODEL, pipeline_mode=pl.Buffered(1)), row(LANES),
                  pl.BlockSpec((None, 1, 1, 6 * D_MODEL), mod_idx),
                  pl.BlockSpec((None, g, D_MODEL, 2 * D_EXPERT), lambda i, e: (layer, e, 0, 0)),
                  pl.BlockSpec((None, g, D_EXPERT, D_MODEL), lambda i, e: (layer, e, 0, 0)),
                  full(lw["shared_w_gu"]), full(lw["shared_w_down"]), _const_spec(norm_f)],
        out_specs=row(D_MODEL),
        out_shape=jax.ShapeDtypeStruct((n, D_MODEL), F32),
        compiler_params=pltpu.CompilerParams(dimension_semantics=("arbitrary", "arbitrary"),
                                             vmem_limit_bytes=VMEM_LIMIT),
        name="moe_final" if final else "moe",
    )(h2, x1, cw, mods, lw["moe_w_gu"], lw["moe_w_down"], lw["shared_w_gu"], lw["shared_w_down"], norm_f)


def _constants():
    idx = np.arange(MIX_W)
    same_head = (idx[:, None] // HEAD_DIM) == (idx[None, :] // HEAD_DIM)
    maskbd = jnp.asarray(same_head, BF16)
    eexp = np.zeros((LANES, 4 * MIX_W), np.float32)
    for blk in range(4):
        kind, d = divmod(blk, 2)
        for h in range(HEADS):
            src = kind * 2 * HEADS + d * HEADS + h
            eexp[src, blk * MIX_W + h * HEAD_DIM: blk * MIX_W + (h + 1) * HEAD_DIM] = 1.0
    return {"maskbd": maskbd, "ones_bd": maskbd, "eexp": jnp.asarray(eexp, BF16)}


def _rope_tables(n):
    rows = n // GRID_W
    row = jnp.repeat(jnp.arange(rows, dtype=F32), GRID_W)
    col = jnp.tile(jnp.arange(GRID_W, dtype=F32), rows)
    axis_dim = MLA_ROPE // 2
    inv = jnp.power(ROPE_BASE, -jnp.arange(0, axis_dim, 2, dtype=F32) / axis_dim)
    ang_r = row[:, None] * inv
    ang_c = col[:, None] * inv
    cr, sr, cc, sc = jnp.cos(ang_r), jnp.sin(ang_r), jnp.cos(ang_c), jnp.sin(ang_c)
    zeros = jnp.zeros((n, LANES - MLA_ROPE), F32)
    cos_t = jnp.concatenate([cr, cr, cc, cc, zeros], axis=1)
    sin_t = jnp.concatenate([-sr, sr, -sc, sc, zeros], axis=1)
    return cos_t, sin_t


def _stacked_weights(p):
    w_in = p["w_in"]
    kpe0 = Q_LORA + KV_LORA
    zc = jnp.zeros((DEPTH, D_MODEL, LANES - MLA_ROPE), F32)
    w_in_p = jnp.concatenate([
        w_in[..., :P_MLA], zc, w_in[..., kpe0 + _ROPE_SWAP], zc,
        w_in[..., P_MLA:P_MLA + ZG_W],
        w_in[..., P_MLA + ZG_W:P_MLA + P_GDN], jnp.zeros((DEPTH, D_MODEL, LANES - 4 * HEADS), F32),
        w_in[..., P_MLA + P_GDN:]], axis=-1).astype(BF16)
    w_uq = p["mla_w_uq"].reshape(DEPTH, Q_LORA, MLA_HEADS, MLA_NOPE + MLA_ROPE)
    zq = jnp.zeros((DEPTH, Q_LORA, MLA_HEADS, QH_W - MLA_NOPE - MLA_ROPE), F32)
    w_uq_a = jnp.concatenate([w_uq, zq], axis=-1).reshape(DEPTH, Q_LORA, MLA_HEADS * QH_W).astype(BF16)
    w_uq_sw = jnp.concatenate([jnp.zeros((DEPTH, Q_LORA, MLA_HEADS, MLA_NOPE), F32),
                               w_uq[..., MLA_NOPE + _ROPE_SWAP], zq], axis=-1)
    w_uq_sw = w_uq_sw.reshape(DEPTH, Q_LORA, MLA_HEADS * QH_W).astype(BF16)

    def per_direction(w):
        half = jnp.zeros((DEPTH, 64, MIX_W), F32)
        return jnp.stack([jnp.concatenate([w[:, 0], half], axis=1),
                          jnp.concatenate([half, w[:, 1]], axis=1)], axis=1).astype(BF16)

    row = lambda v: v.reshape(DEPTH, 1, -1)
    pad_row = lambda v: jnp.pad(row(v), ((0, 0), (0, 0), (0, LANES - 2 * HEADS)))
    return {
        "norm1": row(p["norm1"]),
        "w_in": w_in_p,
        "q_norm": row(p["mla_q_norm"]),
        "w_uq": w_uq_a, "w_uq_sw": w_uq_sw,
        "kv_norm": row(p["mla_kv_norm"]),
        "w_ukv": p["mla_w_ukv"].astype(BF16),
        "gdn_conv": p["gdn_conv"],
        "gdn_alog": pad_row(p["gdn_a_log"]),
        "gdn_dtb": pad_row(p["gdn_dt_bias"]),
        "gdn_norm": jnp.tile(row(p["gdn_norm"]), (1, 1, HEADS)),
        "rwkv_mu_prev": row(p["rwkv_mu_prev"]),
        "rwkv_mu_next": row(p["rwkv_mu_next"]),
        "rwkv_w0": p["rwkv_w0"],
        "rwkv_w2": per_direction(p["rwkv_w2"]),
        "rwkv_a0": p["rwkv_a0"],
        "rwkv_a2": per_direction(p["rwkv_a2"]),
        "rwkv_g2": p["rwkv_g2"].astype(BF16),
        "rwkv_k_k": row(p["rwkv_k_k"]),
        "rwkv_k_a": row(p["rwkv_k_a"]),
        "rwkv_r_k": row(p["rwkv_r_k"]),
        "rwkv_gn_w": row(p["rwkv_gn_w"]),
        "rwkv_gn_b": row(p["rwkv_gn_b"]),
        "w_out": p["w_out"].astype(BF16),
        "norm2": row(p["norm2"]),
        "router_t": jnp.swapaxes(p["moe_router"], 1, 2),
        "router_b": p["moe_bias"].reshape(DEPTH, N_EXPERTS, 1),
        "moe_w_gu": p["moe_w_gu"],
        "moe_w_down": p["moe_w_down"],
        "shared_w_gu": p["shared_w_gu"].astype(BF16),
        "shared_w_down": p["shared_w_down"].astype(BF16),
    }


def _embed_block_diag(s):
    b = s.shape[0]
    eye = jnp.eye(HEADS, dtype=s.dtype)
    out = jnp.einsum("bdhkv,hg->bdhkgv", s, eye)
    return out.reshape(b, 2, MIX_W, MIX_W)


def _trunk(x2d, mods, lw, consts, norm_f, seq_len, mod_base, cache, tm, tq, tm_moe):
    outs = []
    for l in range(DEPTH):
        zm, zg, zab, zr = _inproj(x2d, mods, lw, l, tm, seq_len, mod_base)
        if cache is None:
            o_mla, ckv = _mla(zm, lw, l, seq_len, tq)
            o_gdn, s_gdn = _gdn(zg, zab, lw, l, consts, seq_len)
            o_rwkv, s_rwkv = _rwkv(zr, lw, l, consts, seq_len)
            s_rwkv = jnp.swapaxes(s_rwkv, -1, -2)
        else:
            cckv, ckpe, rc, rs, sg, sr = cache
            o_mla, ckv = _mla(zm, lw, l, seq_len, tq, (cckv, ckpe, rc, rs))
            o_gdn, s_gdn = _gdn(zg, zab, lw, l, consts, seq_len, sg)
            o_rwkv, s_rwkv = _rwkv(zr, lw, l, consts, seq_len, sr)
        x1, h2, cw = _post(x2d, o_mla, o_gdn, o_rwkv, mods, lw, l, tm, seq_len, mod_base)
        x2d = _moe(h2, x1, cw, mods, lw, l, norm_f, tm_moe, seq_len, mod_base, final=(l == DEPTH - 1))
        kpe0 = Q_LORA + KV_LORA
        outs.append((ckv, zm[:, kpe0:kpe0 + MLA_ROPE], s_gdn, s_rwkv))
    return x2d, outs


def kernel(x_prompt, x_sample, cache_mla_ckv, cache_mla_kpe, state_gdn, state_rwkv, c, c_ctx, ada_w, ada_b, norm1, w_in, mla_q_norm, mla_w_uq, mla_kv_norm, mla_w_ukv, gdn_conv, gdn_a_log, gdn_dt_bias, gdn_norm, rwkv_mu_prev, rwkv_mu_next, rwkv_w0, rwkv_w2, rwkv_a0, rwkv_a2, rwkv_g2, rwkv_k_k, rwkv_k_a, rwkv_r_k, rwkv_gn_w, rwkv_gn_b, w_out, norm2, moe_router, moe_bias, moe_w_gu, moe_w_down, shared_w_gu, shared_w_down, norm_f):
    p = dict(norm1=norm1, w_in=w_in, mla_q_norm=mla_q_norm, mla_w_uq=mla_w_uq, mla_kv_norm=mla_kv_norm,
             mla_w_ukv=mla_w_ukv, gdn_conv=gdn_conv, gdn_a_log=gdn_a_log, gdn_dt_bias=gdn_dt_bias,
             gdn_norm=gdn_norm, rwkv_mu_prev=rwkv_mu_prev, rwkv_mu_next=rwkv_mu_next, rwkv_w0=rwkv_w0,
             rwkv_w2=rwkv_w2, rwkv_a0=rwkv_a0, rwkv_a2=rwkv_a2, rwkv_g2=rwkv_g2, rwkv_k_k=rwkv_k_k,
             rwkv_k_a=rwkv_k_a, rwkv_r_k=rwkv_r_k, rwkv_gn_w=rwkv_gn_w, rwkv_gn_b=rwkv_gn_b, w_out=w_out,
             norm2=norm2, moe_router=moe_router, moe_bias=moe_bias, moe_w_gu=moe_w_gu, moe_w_down=moe_w_down,
             shared_w_gu=shared_w_gu, shared_w_down=shared_w_down)
    weights = _stacked_weights(p)
    consts = _constants()
    nf = norm_f.reshape(1, D_MODEL)
    b_ctx, t_ctx, _ = x_prompt.shape
    b_lat, t_lat, _ = x_sample.shape

    cvec8 = jnp.concatenate([c_ctx[None, :], c, jnp.zeros((8 - 1 - b_lat, D_MODEL), F32)], axis=0)
    mods = _adaln(cvec8, ada_w, ada_b)
    mods = mods.reshape(DEPTH, 8, 1, 6 * D_MODEL)

    xp, ctx_outs = _trunk(x_prompt.reshape(b_ctx * t_ctx, D_MODEL), mods, weights, consts, nf,
                          t_ctx, 0, None, tm=256, tq=t_ctx, tm_moe=1024)
    rc, rs = _rope_tables(t_lat)
    ckpe = jnp.pad(cache_mla_kpe, ((0, 0), (0, 0), (0, 0), (0, LANES - MLA_ROPE)))
    cache = (cache_mla_ckv, ckpe, rc, rs, _embed_block_diag_layers(state_gdn),
             _embed_block_diag_layers(jnp.swapaxes(state_rwkv, -1, -2)))
    xs, _ = _trunk(x_sample.reshape(b_lat * t_lat, D_MODEL), mods, weights, consts, nf,
                   t_lat, 1, cache, tm=256, tq=256, tm_moe=1024)

    y_prompt = xp.reshape(b_ctx, t_ctx, D_MODEL)
    y_sample = xs.reshape(b_lat, t_lat, D_MODEL)
    new_ckv = jnp.stack([o[0].reshape(b_ctx, t_ctx, KV_LORA) for o in ctx_outs], axis=1)
    new_kpe = jnp.stack([o[1].reshape(b_ctx, t_ctx, MLA_ROPE) for o in ctx_outs], axis=1)
    new_gdn = jnp.stack([o[2] for o in ctx_outs], axis=1)
    new_rwkv = jnp.stack([o[3] for o in ctx_outs], axis=1)
    return (y_prompt, y_sample, new_ckv, new_kpe, new_gdn, new_rwkv)


def _embed_block_diag_layers(s):
    b = s.shape[0]
    return _embed_block_diag(s.reshape(b * DEPTH, 2, HEADS, HEAD_DIM, HEAD_DIM)).reshape(
        b, DEPTH, 2, MIX_W, MIX_W)
```

```python
import functools

import numpy as np
import jax
import jax.numpy as jnp
from jax import lax
from jax.experimental import pallas as pl
from jax.experimental.pallas import tpu as pltpu

F32 = jnp.float32
BF16 = jnp.bfloat16

D_MODEL = 1024
BATCH = 32
SEQ = 256
DEPTH = 2
DEC_BATCH = 2
DEC_SEQ = 1024
PAST_LEN = 512
GRID_W = 64
NORM_EPS = 1e-6

MLA_HEADS = 4
MLA_NOPE = 128
MLA_ROPE = 64
MLA_V = 128
Q_LORA = 384
KV_LORA = 256
ROPE_BASE = 10000.0
MLA_SCALE = (MLA_NOPE + MLA_ROPE) ** -0.5

HEADS = 4
HEAD_DIM = 64
MIX_W = HEADS * HEAD_DIM
GDN_CONV_CH = 3 * MIX_W
CHUNK = 64
RWKV_GN_EPS = 64e-5

N_EXPERTS = 64
TOP_K = 8
N_GROUPS = 8
GROUP_SIZE = N_EXPERTS // N_GROUPS
TOPK_GROUPS = 4
D_EXPERT = 256
ROUTE_SCALE = 2.5

P_MLA = Q_LORA + KV_LORA + MLA_ROPE
P_GDN = GDN_CONV_CH + MIX_W + 4 * HEADS
P_RWKV = 3 * MIX_W + 128 + 128 + 128

LANES = 128
ZM_W = Q_LORA + KV_LORA + 2 * LANES
ZG_W = GDN_CONV_CH + MIX_W
ZR_W = P_RWKV
QH_W = 2 * LANES
VMEM_LIMIT = 56 * 1024 * 1024

_ROPE_SWAP = np.concatenate([np.arange(16, 32), np.arange(0, 16), np.arange(48, 64), np.arange(32, 48)])


def _sigmoid(x):
    return 1.0 / (1.0 + jnp.exp(-x))


def _silu(x):
    return x * _sigmoid(x)


def _softplus(x):
    return jnp.maximum(x, 0.0) + jnp.log(1.0 + jnp.exp(-jnp.abs(x)))


def _rms(x, g, eps=NORM_EPS):
    return x * lax.rsqrt(jnp.mean(x * x, axis=-1, keepdims=True) + eps) * g


def _mm(a, b):
    return jnp.dot(a.astype(BF16), b.astype(BF16), preferred_element_type=F32)


def _mm_nt(a, b):
    return lax.dot_general(a.astype(BF16), b.astype(BF16), (((1,), (1,)), ((), ())),
                           preferred_element_type=F32)


def _mm_tn(a, b):
    return lax.dot_general(a.astype(BF16), b.astype(BF16), (((0,), (0,)), ((), ())),
                           preferred_element_type=F32)


def _split3(x):
    p1 = x.astype(BF16)
    r1 = x - p1.astype(F32)
    p2 = r1.astype(BF16)
    r2 = r1 - p2.astype(F32)
    return p1, p2, r2.astype(BF16)


def _mm_sel_l(sel, x):
    p1, p2, p3 = _split3(x)
    return _mm(sel, p1) + _mm(sel, p2) + _mm(sel, p3)


def _mm_sel_r(x, sel):
    p1, p2, p3 = _split3(x)
    return _mm(p1, sel) + _mm(p2, sel) + _mm(p3, sel)


def _iota(shape, dim):
    return lax.broadcasted_iota(jnp.int32, shape, dim)


def _layer_spec(a, layer, **kw):
    nd = a.ndim - 1
    return pl.BlockSpec((None,) + a.shape[1:], lambda *_: (layer,) + (0,) * nd, **kw)


def _const_spec(a, **kw):
    return pl.BlockSpec(a.shape, lambda *_: (0,) * a.ndim, **kw)


def _bd(x, maskbd):
    xb = x.astype(BF16)
    return jnp.concatenate([xb] * HEADS, axis=0) * maskbd


def _chunk_masks(rev):
    row = _iota((CHUNK, MIX_W), 0)
    col = jnp.bitwise_and(_iota((CHUNK, MIX_W), 1), HEAD_DIM - 1)
    r2 = _iota((CHUNK, CHUNK), 0)
    c2 = _iota((CHUNK, CHUNK), 1)
    if rev:
        inc, strict, tri = row <= col, row < col, r2 <= c2
    else:
        inc, strict, tri = row >= col, row > col, r2 >= c2
    eye = jnp.where(row == col, 1.0, 0.0).astype(F32)
    return inc, strict, jnp.where(tri, 1.0, 0.0).astype(BF16), eye


def _split2(x):
    hi = x.astype(BF16)
    return hi, (x - hi.astype(F32)).astype(BF16)


def _mm_bd3(x, p, maskbd):
    n = x.shape[0]
    xh, xl = _split2(x)
    ph, pl_ = _split2(p)
    r = jnp.dot(jnp.concatenate([xh, xl], axis=0), _bd(ph, maskbd), preferred_element_type=F32)
    return r[:n] + r[n:] + jnp.dot(xh, _bd(pl_, maskbd), preferred_element_type=F32)


def _neumann_inverse(a_list, eye_list, maskbd):
    bs = [-a for a in a_list]
    ms = [eye + b for eye, b in zip(eye_list, bs)]
    ps = [_mm_bd3(b, b, maskbd) for b in bs]
    for _ in range(4):
        boths = [_mm_bd3(jnp.concatenate([m, p], axis=0), p, maskbd) for m, p in zip(ms, ps)]
        ms = [m + both[:CHUNK] for m, both in zip(ms, boths)]
        ps = [both[CHUNK:] for both in boths]
    return [m + _mm_bd3(m, p, maskbd) for m, p in zip(ms, ps)]


def _adaln_kernel(c_ref, w_ref, b_ref, o_ref):
    cv = c_ref[...]
    o_ref[0] = _mm(_silu(cv), w_ref[0]) + b_ref[0]


def _adaln(cvec8, ada_w, ada_b):
    tn = 768
    n_out = 6 * D_MODEL
    return pl.pallas_call(
        _adaln_kernel,
        grid=(DEPTH, n_out // tn),
        in_specs=[
            pl.BlockSpec((8, D_MODEL), lambda l, j: (0, 0)),
            pl.BlockSpec((1, D_MODEL, tn), lambda l, j: (l, 0, j)),
            pl.BlockSpec((1, 1, tn), lambda l, j: (l, 0, j)),
        ],
        out_specs=pl.BlockSpec((1, 8, tn), lambda l, j: (l, 0, j)),
        out_shape=jax.ShapeDtypeStruct((DEPTH, 8, n_out), F32),
        compiler_params=pltpu.CompilerParams(dimension_semantics=("arbitrary", "arbitrary"),
                                             vmem_limit_bytes=VMEM_LIMIT),
        name="adaln",
    )(cvec8, ada_w, ada_b.reshape(DEPTH, 1, n_out))


_KPE0 = Q_LORA + KV_LORA
_W_IN_MOVES = (
    [(0, 0, P_MLA)]
    + [(P_MLA + LANES - MLA_ROPE + 16 * j, _KPE0 + 16 * int(_ROPE_SWAP[16 * j] // 16), 16) for j in range(4)]
    + [(ZM_W, P_MLA, ZG_W), (ZM_W + ZG_W, P_MLA + ZG_W, 4 * HEADS), (ZM_W + ZG_W + LANES, P_MLA + P_GDN, P_RWKV)]
)
W_IN_PAD = ZM_W + ZG_W + LANES + ZR_W


def _inproj_kernel(x_ref, mod_ref, n1_ref, w_ref, zm_ref, zg_ref, zab_ref, zr_ref, w_s):
    @pl.when(pl.program_id(0) == 0)
    def _():
        w_s[...] = jnp.zeros(w_s.shape, BF16)
        for dst, src, width in _W_IN_MOVES:
            w_s[:, dst:dst + width] = w_ref[:, src:src + width].astype(BF16)

    m = mod_ref[0]
    sh = m[:, 0:D_MODEL]
    sc = m[:, D_MODEL:2 * D_MODEL]
    h = _rms(x_ref[...], n1_ref[...]) * (1.0 + sc) + sh
    z = _mm(h, w_s[...])
    o1 = ZM_W
    o2 = o1 + ZG_W
    o3 = o2 + LANES
    zm_ref[...] = z[:, :o1]
    zg_ref[...] = z[:, o1:o2]
    zab_ref[...] = z[:, o2:o3]
    zr_ref[...] = z[:, o3:]


def _inproj(x2d, mods, lw, layer, tm, seq_len, mod_base):
    n = x2d.shape[0]
    tiles_per_seq = seq_len // tm if mod_base else 1

    def mod_idx(i):
        return (layer, mod_base + i // tiles_per_seq if mod_base else 0, 0, 0)

    return pl.pallas_call(
        _inproj_kernel,
        grid=(n // tm,),
        in_specs=[
            pl.BlockSpec((tm, D_MODEL), lambda i: (i, 0)),
            pl.BlockSpec((None, 1, 1, 6 * D_MODEL), mod_idx),
            _layer_spec(lw["norm1"], layer),
            _layer_spec(lw["w_in"], layer, pipeline_mode=pl.Buffered(1)),
        ],
        out_specs=[
            pl.BlockSpec((tm, ZM_W), lambda i: (i, 0)),
            pl.BlockSpec((tm, ZG_W), lambda i: (i, 0)),
            pl.BlockSpec((tm, LANES), lambda i: (i, 0)),
            pl.BlockSpec((tm, ZR_W), lambda i: (i, 0)),
        ],
        out_shape=[
            jax.ShapeDtypeStruct((n, ZM_W), F32),
            jax.ShapeDtypeStruct((n, ZG_W), F32),
            jax.ShapeDtypeStruct((n, LANES), F32),
            jax.ShapeDtypeStruct((n, ZR_W), F32),
        ],
        scratch_shapes=[pltpu.VMEM((D_MODEL, W_IN_PAD), BF16)],
        compiler_params=pltpu.CompilerParams(dimension_semantics=("arbitrary",),
                                             vmem_limit_bytes=VMEM_LIMIT),
        name="inproj",
    )(x2d, mods, lw["norm1"], lw["w_in"])


def _mla_kernel(*refs, seq_len, tq, past, cached):
    if cached:
        (zm_ref, cckv_ref, ckpe_ref, rc_ref, rs_ref, qn_ref, wuq_ref, wuqs_ref, kvn_ref, wukv_ref,
         o_ref, ckv_ref, k_s, v_s) = refs
    else:
        (zm_ref, qn_ref, wuq_ref, kvn_ref, wukv_ref, o_ref, ckv_ref, k_s, v_s) = refs
    qi = pl.program_id(1)
    o_kpe = Q_LORA + KV_LORA

    @pl.when(qi == 0)
    def _():
        zm = zm_ref[...]
        ckv = _rms(zm[:, Q_LORA:o_kpe], kvn_ref[...])
        ckv_ref[...] = ckv
        kpe = zm[:, o_kpe:o_kpe + LANES]
        if cached:
            kpe = kpe * rc_ref[...] + zm[:, o_kpe + LANES:o_kpe + 2 * LANES] * rs_ref[...]
            kvc = _mm(cckv_ref[0], wukv_ref[...])
            kpc = ckpe_ref[0].astype(BF16)
        kv = _mm(ckv, wukv_ref[...])
        kpe = kpe.astype(BF16)
        for h in range(MLA_HEADS):
            c0 = h * QH_W
            if cached:
                k_s[0:past, c0:c0 + LANES] = kvc[:, c0:c0 + LANES].astype(BF16)
                k_s[0:past, c0 + LANES:c0 + QH_W] = kpc
                v_s[0:past, h * MLA_V:(h + 1) * MLA_V] = kvc[:, c0 + LANES:c0 + QH_W].astype(BF16)
            k_s[past:past + seq_len, c0:c0 + LANES] = kv[:, c0:c0 + LANES].astype(BF16)
            k_s[past:past + seq_len, c0 + LANES:c0 + QH_W] = kpe
            v_s[past:past + seq_len, h * MLA_V:(h + 1) * MLA_V] = kv[:, c0 + LANES:c0 + QH_W].astype(BF16)

    r0 = pl.multiple_of(qi * tq, tq)
    zq = zm_ref[pl.ds(r0, tq), :]
    cq = _rms(zq[:, :Q_LORA], qn_ref[...])
    q = _mm(cq, wuq_ref[...])
    if cached:
        qs = _mm(cq, wuqs_ref[...])
        ones = jnp.ones((tq, LANES), F32)
        zeros = jnp.zeros((tq, LANES), F32)
        qc = jnp.concatenate([ones, rc_ref[pl.ds(r0, tq), :]], axis=1)
        qsn = jnp.concatenate([zeros, rs_ref[pl.ds(r0, tq), :]], axis=1)
    for h in range(MLA_HEADS):
        c0 = h * QH_W
        qh = q[:, c0:c0 + QH_W]
        if cached:
            qh = qh * qc + qs[:, c0:c0 + QH_W] * qsn
        s = _mm_nt(qh, k_s[:, c0:c0 + QH_W]) * MLA_SCALE
        e = jnp.exp(s - jnp.max(s, axis=-1, keepdims=True))
        den = jnp.sum(e, axis=-1, keepdims=True)
        o_ref[:, h * MLA_V:(h + 1) * MLA_V] = _mm(e, v_s[:, h * MLA_V:(h + 1) * MLA_V]) / den


def _mla(zm, lw, layer, seq_len, tq, cache=None):
    n = zm.shape[0]
    nb = n // seq_len
    cached = cache is not None
    past = PAST_LEN if cached else 0
    tk = past + seq_len
    seq_spec = lambda w: pl.BlockSpec((seq_len, w), lambda b, q: (b, 0))
    lay = lambda a: _layer_spec(a, layer)
    if cached:
        cckv, ckpe, rc, rs = cache
        args = [zm, cckv, ckpe, rc, rs, lw["q_norm"], lw["w_uq"], lw["w_uq_sw"], lw["kv_norm"], lw["w_ukv"]]
        in_specs = [seq_spec(ZM_W),
                    pl.BlockSpec((1, None, past, KV_LORA), lambda b, q: (b, layer, 0, 0)),
                    pl.BlockSpec((1, None, past, LANES), lambda b, q: (b, layer, 0, 0)),
                    _const_spec(rc), _const_spec(rs)] + [lay(a) for a in args[5:]]
    else:
        args = [zm, lw["q_norm"], lw["w_uq"], lw["kv_norm"], lw["w_ukv"]]
        in_specs = [seq_spec(ZM_W)] + [lay(a) for a in args[1:]]
    return pl.pallas_call(
        functools.partial(_mla_kernel, seq_len=seq_len, tq=tq, past=past, cached=cached),
        grid=(nb, seq_len // tq),
        in_specs=in_specs,
        out_specs=[
            pl.BlockSpec((tq, MLA_HEADS * MLA_V), lambda b, q: (b * (seq_len // tq) + q, 0)),
            pl.BlockSpec((seq_len, KV_LORA), lambda b, q: (b, 0)),
        ],
        out_shape=[
            jax.ShapeDtypeStruct((n, MLA_HEADS * MLA_V), F32),
            jax.ShapeDtypeStruct((n, KV_LORA), F32),
        ],
        scratch_shapes=[
            pltpu.VMEM((tk, MLA_HEADS * QH_W), BF16),
            pltpu.VMEM((tk, MLA_HEADS * MLA_V), BF16),
        ],
        compiler_params=pltpu.CompilerParams(dimension_semantics=("arbitrary", "arbitrary"),
                                             vmem_limit_bytes=VMEM_LIMIT),
        name="mla_lat" if cached else "mla_ctx",
    )(*args)


CHUNK_GROUP = 4
GDN_PRE_W = 5 * MIX_W


def _for_chunk_groups(n_chunks, fn):
    if n_chunks == CHUNK_GROUP:
        fn(0)
    else:
        def body(gi, carry):
            fn(gi * CHUNK_GROUP)
            return carry
        lax.fori_loop(0, n_chunks // CHUNK_GROUP, body, 0)


def _gdn_prepare(items, maskbd):
    n = range(len(items))
    qs, ks, vs, gs, betas, masks, revs = zip(*items)
    gcs = [_mm_sel_l(masks[i][2], gs[i]) for i in n]
    decays = []
    for i in n:
        inc, eye = masks[i][0], masks[i][3]
        gc_row = jnp.sum(eye * gcs[i], axis=0, keepdims=True)
        decays.append(jnp.where(inc, jnp.exp(jnp.where(inc, gcs[i] - gc_row, 0.0)), 0.0))
    kbs = [ks[i] * betas[i] for i in n]
    aqs = [_mm_nt(jnp.concatenate([kbs[i], qs[i]], axis=0), _bd(ks[i], maskbd)) for i in n]
    a_mats = [jnp.where(masks[i][1], aqs[i][:CHUNK] * decays[i], 0.0) for i in n]
    t_invs = _neumann_inverse(a_mats, [m[3] for m in masks], maskbd)
    egcs = [jnp.exp(gc) for gc in gcs]
    uws = [_mm(t_invs[i], jnp.concatenate([_bd(vs[i] * betas[i], maskbd), _bd(kbs[i] * egcs[i], maskbd)], axis=1))
           for i in n]
    out = []
    for i in n:
        g_last = gcs[i][0:1] if revs[i] else gcs[i][CHUNK - 1:CHUNK]
        pre = jnp.concatenate([uws[i], qs[i] * egcs[i], aqs[i][CHUNK:] * decays[i],
                               ks[i] * jnp.exp(g_last - gcs[i])], axis=1)
        out.append((pre, jnp.broadcast_to(jnp.exp(g_last), (8, MIX_W))))
    return out


def _gdn_step(pre, egl, s_bd, maskbd):
    w = MIX_W
    u, wq_in, qk, kdec = pre[:, :w], pre[:, w:3 * w], pre[:, 3 * w:4 * w], pre[:, 4 * w:]
    wq = _mm(jnp.concatenate([wq_in[:, :w], wq_in[:, w:]], axis=0), s_bd)
    v_new = u - wq[:CHUNK]
    o = wq[CHUNK:] + _mm(qk, _bd(v_new, maskbd))
    s_new = s_bd * egl + _mm_tn(kdec, v_new) * maskbd.astype(F32)
    return o, s_new


def _gdn_kernel(*refs, seq_len, cached):
    if cached:
        (zg_ref, zab_ref, s0_ref, conv_ref, alog_ref, dtb_ref, gn_ref, eexp_ref, ones_ref, maskbd_ref,
         o_ref, sout_ref, q_s, k_s, v_s, ge_s, pre_s, gl_s, oacc_s, st_s) = refs
    else:
        (zg_ref, zab_ref, conv_ref, alog_ref, dtb_ref, gn_ref, eexp_ref, ones_ref, maskbd_ref,
         o_ref, sout_ref, q_s, k_s, v_s, ge_s, pre_s, gl_s, oacc_s, st_s) = refs
    t = seq_len
    z = zg_ref[:, :GDN_CONV_CH]
    rowi = _iota((t, 1), 0)
    zp = jnp.where(rowi == 0, 0.0, pltpu.roll(z, 1, 0))
    zn = jnp.where(rowi == t - 1, 0.0, pltpu.roll(z, t - 1, 0))
    cw = conv_ref[...]
    qkv = _silu(zp * cw[0:1] + z * cw[1:2] + zn * cw[2:3])
    ones_bd = ones_ref[...]
    q = qkv[:, :MIX_W]
    k = qkv[:, MIX_W:2 * MIX_W]
    q_s[...] = q * lax.rsqrt(_mm_sel_r(q * q, ones_bd) + 1e-6) * (HEAD_DIM ** -0.5)
    k_s[...] = k * lax.rsqrt(_mm_sel_r(k * k, ones_bd) + 1e-6)
    v_s[...] = qkv[:, 2 * MIX_W:]
    ab = zab_ref[...]
    lane = _iota((t, LANES), 1)
    gb = jnp.where(lane < 2 * HEADS, -jnp.exp(alog_ref[...]) * _softplus(ab + dtb_ref[...]), _sigmoid(ab))
    ge_s[...] = _mm_sel_r(gb, eexp_ref[...])
    oacc_s[...] = jnp.zeros((t, MIX_W), F32)
    if cached:
        st_s[...] = s0_ref[0]
    else:
        st_s[...] = jnp.zeros((2, MIX_W, MIX_W), F32)
    maskbd = maskbd_ref[...]
    masks = (_chunk_masks(False), _chunk_masks(True))
    n_chunks = t // CHUNK

    def prepare_group(c0):
        where, items = [], []
        for j in range(CHUNK_GROUP):
            c = c0 + j
            rows = pl.ds(pl.multiple_of(c * CHUNK, CHUNK), CHUNK)
            for d in range(2):
                where.append((d, c, rows))
                items.append((q_s[rows, :], k_s[rows, :], v_s[rows, :], ge_s[rows, d * MIX_W:(d + 1) * MIX_W],
                              ge_s[rows, (2 + d) * MIX_W:(3 + d) * MIX_W], masks[d], d == 1))
        for (d, c, rows), (pre, egl) in zip(where, _gdn_prepare(items, maskbd)):
            pre_s[d, rows, :] = pre
            gl_s[d, pl.ds(pl.multiple_of(c * 8, 8), 8), :] = egl

    _for_chunk_groups(n_chunks, prepare_group)

    def body(i, carry):
        for d in range(2):
            c = i if d == 0 else n_chunks - 1 - i
            rows = pl.ds(pl.multiple_of(c * CHUNK, CHUNK), CHUNK)
            egl = gl_s[d, pl.ds(pl.multiple_of(c * 8, 8), 8), :][0:1]
            o, s_new = _gdn_step(pre_s[d, rows, :], egl, st_s[d], maskbd)
            oacc_s[rows, :] = oacc_s[rows, :] + o
            st_s[d] = s_new
        return carry

    lax.fori_loop(0, n_chunks, body, 0)
    o = oacc_s[...]
    ms = _mm_sel_r(o * o, ones_bd) * (1.0 / HEAD_DIM)
    gate = zg_ref[:, GDN_CONV_CH:]
    o_ref[...] = o * lax.rsqrt(ms + NORM_EPS) * gn_ref[...] * _silu(gate)
    for d in range(2):
        for h in range(HEADS):
            sl = slice(h * HEAD_DIM, (h + 1) * HEAD_DIM)
            sout_ref[0, d, h] = st_s[d, sl, sl]


def _gdn(zg, zab, lw, layer, consts, seq_len, s0_bd=None):
    n = zg.shape[0]
    nb = n // seq_len
    cached = s0_bd is not None
    args = [zg, zab]
    in_specs = [pl.BlockSpec((seq_len, ZG_W), lambda b: (b, 0)),
                pl.BlockSpec((seq_len, LANES), lambda b: (b, 0))]
    if cached:
        args.append(s0_bd)
        in_specs.append(pl.BlockSpec((1, None, 2, MIX_W, MIX_W), lambda b: (b, layer, 0, 0, 0)))
    layered = [lw["gdn_conv"], lw["gdn_alog"], lw["gdn_dtb"], lw["gdn_norm"]]
    const = [consts["eexp"], consts["ones_bd"], consts["maskbd"]]
    args += layered + const
    in_specs += [_layer_spec(a, layer) for a in layered] + [_const_spec(a) for a in const]
    return pl.pallas_call(
        functools.partial(_gdn_kernel, seq_len=seq_len, cached=cached),
        grid=(nb,),
        in_specs=in_specs,
        out_specs=[
            pl.BlockSpec((seq_len, MIX_W), lambda b: (b, 0)),
            pl.BlockSpec((1, 2, HEADS, HEAD_DIM, HEAD_DIM), lambda b: (b, 0, 0, 0, 0)),
        ],
        out_shape=[
            jax.ShapeDtypeStruct((n, MIX_W), F32),
            jax.ShapeDtypeStruct((nb, 2, HEADS, HEAD_DIM, HEAD_DIM), F32),
        ],
        scratch_shapes=[
            pltpu.VMEM((seq_len, MIX_W), F32),
            pltpu.VMEM((seq_len, MIX_W), F32),
            pltpu.VMEM((seq_len, MIX_W), F32),
            pltpu.VMEM((seq_len, 4 * MIX_W), F32),
            pltpu.VMEM((2, seq_len, GDN_PRE_W), F32),
            pltpu.VMEM((2, seq_len // CHUNK * 8, MIX_W), F32),
            pltpu.VMEM((seq_len, MIX_W), F32),
            pltpu.VMEM((2, MIX_W, MIX_W), F32),
        ],
        compiler_params=pltpu.CompilerParams(dimension_semantics=("arbitrary",),
                                             vmem_limit_bytes=VMEM_LIMIT),
        name="gdn_lat" if cached else "gdn_ctx",
    )(*args)


RWKV_PRE_W = 7 * MIX_W


def _rwkv_prepare(items, maskbd, eye_full):
    n = range(len(items))
    rs, kds, vs, kks, bs, lws, masks, revs = zip(*items)
    cums = [_mm_sel_l(masks[i][2], lws[i]) for i in n]
    einvs = [jnp.exp(-c) for c in cums]
    kts = [kks[i] * jnp.exp(cums[i] - lws[i]) for i in n]
    rts = [rs[i] * jnp.exp(cums[i]) for i in n]
    krs = [jnp.concatenate([kts[i], rts[i]], axis=0) for i in n]
    lb_alls = [_mm_nt(krs[i], _bd(bs[i] * einvs[i], maskbd)) for i in n]
    lk_alls = [_mm_nt(krs[i], _bd(kds[i] * einvs[i], maskbd)) for i in n]
    lbs = [jnp.where(masks[i][1], lb_alls[i][:CHUNK], 0.0) for i in n]
    t_invs = _neumann_inverse(lbs, [m[3] for m in masks], maskbd)
    lvs = [_mm(jnp.concatenate([jnp.where(masks[i][1], lk_alls[i][:CHUNK], 0.0),
                                jnp.where(masks[i][0], lk_alls[i][CHUNK:], 0.0)], axis=0), _bd(vs[i], maskbd))
           for i in n]
    tkps = [_mm(t_invs[i], jnp.concatenate([_bd(kts[i], maskbd), _bd(lvs[i][:CHUNK], maskbd)], axis=1)) for i in n]
    out = []
    for i in n:
        c_last = cums[i][0:1] if revs[i] else cums[i][CHUNK - 1:CHUNK]
        tail = jnp.exp(c_last - cums[i])
        rb = jnp.where(masks[i][0], lb_alls[i][CHUNK:], 0.0)
        pre = jnp.concatenate([tkps[i][:, :MIX_W], rts[i], tkps[i][:, MIX_W:], lvs[i][CHUNK:], rb,
                               kds[i] * tail, bs[i] * tail], axis=1)
        gcol = jnp.sum(eye_full * jnp.exp(c_last), axis=1, keepdims=True)
        out.append((pre, jnp.broadcast_to(gcol, (MIX_W, LANES))))
    return out


def _rwkv_step(pre, v, gcol, z_bd, maskbd):
    w = MIX_W
    pr = _mm(jnp.concatenate([pre[:, :w], pre[:, w:2 * w]], axis=0), z_bd)
    p = pr[:CHUNK] + pre[:, 2 * w:3 * w]
    o = pr[CHUNK:] + pre[:, 3 * w:4 * w] - _mm(pre[:, 4 * w:5 * w], _bd(p, maskbd))
    upd = _mm_tn(jnp.concatenate([pre[:, 5 * w:6 * w], pre[:, 6 * w:]], axis=0), jnp.concatenate([v, -p], axis=0))
    z_new = z_bd * jnp.concatenate([gcol, gcol], axis=1) + upd * maskbd.astype(F32)
    return o, z_new


def _rwkv_kernel(*refs, seq_len, cached):
    if cached:
        (zr_ref, s0_ref, mup_ref, mun_ref, w0_ref, w2_ref, a0_ref, a2_ref, g2_ref, kk_ref, ka_ref, rk_ref,
         gnw_ref, gnb_ref, ones_ref, maskbd_ref, o_ref, sout_ref,
         r_s, v_s, kk_s, dir_s, bg_s, pre_s, gcol_s, oacc_s, st_s) = refs
    else:
        (zr_ref, mup_ref, mun_ref, w0_ref, w2_ref, a0_ref, a2_ref, g2_ref, kk_ref, ka_ref, rk_ref,
         gnw_ref, gnb_ref, ones_ref, maskbd_ref, o_ref, sout_ref,
         r_s, v_s, kk_s, dir_s, bg_s, pre_s, gcol_s, oacc_s, st_s) = refs
    t = seq_len
    z = zr_ref[...]
    rowi = _iota((t, 1), 0)
    zp = jnp.where(rowi == 0, 0.0, pltpu.roll(z, 1, 0))
    zn = jnp.where(rowi == t - 1, 0.0, pltpu.roll(z, t - 1, 0))
    z = z + mup_ref[...] * (zp - z) + mun_ref[...] * (zn - z)
    w = MIX_W
    r = z[:, :w]
    k = z[:, w:2 * w]
    v = z[:, 2 * w:3 * w]
    wd = jnp.tanh(z[:, 3 * w:3 * w + LANES])
    ad = z[:, 3 * w + LANES:3 * w + 2 * LANES]
    gd = _sigmoid(z[:, 3 * w + 2 * LANES:])
    ones_bd = ones_ref[...]
    kk = k * kk_ref[...]
    kk = kk * lax.rsqrt(_mm_sel_r(kk * kk, ones_bd) + 1e-6)
    r_s[...] = r
    v_s[...] = v
    kk_s[...] = kk
    bonus = jnp.zeros((t, w), F32)
    for d in range(2):
        w_log = -_softplus(-(w0_ref[d:d + 1] + _mm(wd, w2_ref[d]))) - 0.5
        a = _sigmoid(a0_ref[d:d + 1] + _mm(ad, a2_ref[d]))
        kd = k * (1.0 + (a - 1.0) * ka_ref[...])
        dir_s[:, (3 * d) * w:(3 * d + 1) * w] = -jnp.exp(w_log)
        dir_s[:, (3 * d + 1) * w:(3 * d + 2) * w] = kd
        dir_s[:, (3 * d + 2) * w:(3 * d + 3) * w] = kk * a
        bonus = bonus + _mm_sel_r(r * kd * rk_ref[...], ones_bd) * v
    bg_s[:, :w] = bonus
    bg_s[:, w:] = _mm(gd, g2_ref[...])
    oacc_s[...] = jnp.zeros((t, w), F32)
    if cached:
        st_s[...] = s0_ref[0]
    else:
        st_s[...] = jnp.zeros((2, w, w), F32)
    maskbd = maskbd_ref[...]
    masks = (_chunk_masks(False), _chunk_masks(True))
    eye_full = jnp.where(_iota((w, w), 0) == _iota((w, w), 1), 1.0, 0.0).astype(F32)
    n_chunks = t // CHUNK

    def prepare_group(c0):
        where, items = [], []
        for j in range(CHUNK_GROUP):
            c = c0 + j
            rows = pl.ds(pl.multiple_of(c * CHUNK, CHUNK), CHUNK)
            for d in range(2):
                where.append((d, c, rows))
                items.append((r_s[rows, :], dir_s[rows, (3 * d + 1) * w:(3 * d + 2) * w], v_s[rows, :], kk_s[rows, :],
                              dir_s[rows, (3 * d + 2) * w:(3 * d + 3) * w], dir_s[rows, (3 * d) * w:(3 * d + 1) * w],
                              masks[d], d == 1))
        for (d, c, rows), (pre, gcol) in zip(where, _rwkv_prepare(items, maskbd, eye_full)):
            pre_s[d, rows, :] = pre
            gcol_s[d, pl.ds(pl.multiple_of(c * w, w), w), :] = gcol

    _for_chunk_groups(n_chunks, prepare_group)

    def body(i, carry):
        for d in range(2):
            c = i if d == 0 else n_chunks - 1 - i
            rows = pl.ds(pl.multiple_of(c * CHUNK, CHUNK), CHUNK)
            gcol = gcol_s[d, pl.ds(pl.multiple_of(c * w, w), w), :]
            o, z_new = _rwkv_step(pre_s[d, rows, :], v_s[rows, :], gcol, st_s[d], maskbd)
            oacc_s[rows, :] = oacc_s[rows, :] + o
            st_s[d] = z_new
        return carry

    lax.fori_loop(0, n_chunks, body, 0)
    o = oacc_s[...]
    inv_n = 1.0 / HEAD_DIM
    mu = _mm_sel_r(o, ones_bd) * inv_n
    oc = o - mu
    var = _mm_sel_r(oc * oc, ones_bd) * inv_n
    y = oc * lax.rsqrt(var + RWKV_GN_EPS) * gnw_ref[...] + gnb_ref[...]
    o_ref[...] = (y + bg_s[:, :w]) * bg_s[:, w:]
    for d in range(2):
        for h in range(HEADS):
            sl = slice(h * HEAD_DIM, (h + 1) * HEAD_DIM)
            sout_ref[0, d, h] = st_s[d, sl, sl]


def _rwkv(zr, lw, layer, consts, seq_len, s0_bd=None):
    n = zr.shape[0]
    nb = n // seq_len
    cached = s0_bd is not None
    args = [zr]
    in_specs = [pl.BlockSpec((seq_len, ZR_W), lambda b: (b, 0))]
    if cached:
        args.append(s0_bd)
        in_specs.append(pl.BlockSpec((1, None, 2, MIX_W, MIX_W), lambda b: (b, layer, 0, 0, 0)))
    layered = [lw["rwkv_mu_prev"], lw["rwkv_mu_next"], lw["rwkv_w0"], lw["rwkv_w2"], lw["rwkv_a0"], lw["rwkv_a2"],
               lw["rwkv_g2"], lw["rwkv_k_k"], lw["rwkv_k_a"], lw["rwkv_r_k"], lw["rwkv_gn_w"], lw["rwkv_gn_b"]]
    const = [consts["ones_bd"], consts["maskbd"]]
    args += layered + const
    in_specs += [_layer_spec(a, layer) for a in layered] + [_const_spec(a) for a in const]
    return pl.pallas_call(
        functools.partial(_rwkv_kernel, seq_len=seq_len, cached=cached),
        grid=(nb,),
        in_specs=in_specs,
        out_specs=[
            pl.BlockSpec((seq_len, MIX_W), lambda b: (b, 0)),
            pl.BlockSpec((1, 2, HEADS, HEAD_DIM, HEAD_DIM), lambda b: (b, 0, 0, 0, 0)),
        ],
        out_shape=[
            jax.ShapeDtypeStruct((n, MIX_W), F32),
            jax.ShapeDtypeStruct((nb, 2, HEADS, HEAD_DIM, HEAD_DIM), F32),
        ],
        scratch_shapes=[
            pltpu.VMEM((seq_len, MIX_W), F32),
            pltpu.VMEM((seq_len, MIX_W), F32),
            pltpu.VMEM((seq_len, MIX_W), F32),
            pltpu.VMEM((seq_len, 6 * MIX_W), F32),
            pltpu.VMEM((seq_len, 2 * MIX_W), F32),
            pltpu.VMEM((2, seq_len, RWKV_PRE_W), F32),
            pltpu.VMEM((2, seq_len // CHUNK * MIX_W, LANES), F32),
            pltpu.VMEM((seq_len, MIX_W), F32),
            pltpu.VMEM((2, MIX_W, MIX_W), F32),
        ],
        compiler_params=pltpu.CompilerParams(dimension_semantics=("arbitrary",),
                                             vmem_limit_bytes=VMEM_LIMIT),
        name="rwkv_lat" if cached else "rwkv_ctx",
    )(*args)


def _route(logits_t, bias):
    tm = logits_t.shape[1]
    neg = -jnp.inf
    sc = _sigmoid(logits_t)
    sel = (sc + bias).reshape(N_GROUPS, GROUP_SIZE, tm)
    si = _iota(sel.shape, 1).astype(F32)
    m1 = jnp.max(sel, axis=1, keepdims=True)
    f1 = jnp.min(jnp.where(sel == m1, si, float(GROUP_SIZE)), axis=1, keepdims=True)
    m2 = jnp.max(jnp.where(si == f1, neg, sel), axis=1, keepdims=True)
    grp = m1 + m2
    gi = _iota(grp.shape, 0).astype(F32)
    gsel = jnp.zeros(grp.shape, F32)
    for _ in range(TOPK_GROUPS):
        mx = jnp.max(grp, axis=0, keepdims=True)
        fi = jnp.min(jnp.where(grp == mx, gi, float(N_GROUPS)), axis=0, keepdims=True)
        hit = gi == fi
        gsel = jnp.where(hit, 1.0, gsel)
        grp = jnp.where(hit, neg, grp)
    cur = jnp.where(gsel > 0.0, sel, neg)
    ei = (_iota(cur.shape, 0) * GROUP_SIZE + _iota(cur.shape, 1)).astype(F32)
    chosen = jnp.zeros(cur.shape, F32)
    for _ in range(TOP_K):
        mx = jnp.max(jnp.max(cur, axis=0, keepdims=True), axis=1, keepdims=True)
        fi = jnp.min(jnp.min(jnp.where(cur == mx, ei, float(N_EXPERTS)), axis=0, keepdims=True),
                     axis=1, keepdims=True)
        hit = ei == fi
        chosen = jnp.where(hit, 1.0, chosen)
        cur = jnp.where(hit, neg, cur)
    wsel = chosen.reshape(N_EXPERTS, tm) * sc
    return wsel / jnp.sum(wsel, axis=0, keepdims=True) * ROUTE_SCALE


def _post_kernel(x_ref, om_ref, og_ref, or_ref, mod_ref, wo_ref, n2_ref, rt_ref, rb_ref,
                 x1_ref, h2_ref, cw_ref):
    m = mod_ref[0]
    g1 = m[:, 2 * D_MODEL:3 * D_MODEL]
    sh2 = m[:, 3 * D_MODEL:4 * D_MODEL]
    sc2 = m[:, 4 * D_MODEL:5 * D_MODEL]
    w_mla = MLA_HEADS * MLA_V
    mix = (_mm(om_ref[...], wo_ref[0:w_mla, :]) + _mm(og_ref[...], wo_ref[w_mla:w_mla + MIX_W, :])
           + _mm(or_ref[...], wo_ref[w_mla + MIX_W:, :]))
    x1 = x_ref[...] + g1 * mix
    x1_ref[...] = x1
    h2 = _rms(x1, n2_ref[...]) * (1.0 + sc2) + sh2
    h2_ref[...] = h2.astype(BF16)
    r_hi, r_lo = _split2(rt_ref[...])
    h_hi, h_lo = _split2(h2)
    logits_t = _mm_nt(r_hi, h_hi) + _mm_nt(r_hi, h_lo) + _mm_nt(r_lo, h_hi)
    cw_t = _route(logits_t, rb_ref[...])
    cw_ref[...] = jnp.concatenate([cw_t, jnp.zeros_like(cw_t)], axis=0).T


def _post(x2d, om, og, orw, mods, lw, layer, tm, seq_len, mod_base):
    n = x2d.shape[0]
    tiles_per_seq = seq_len // tm if mod_base else 1

    def mod_idx(i):
        return (layer, mod_base + i // tiles_per_seq if mod_base else 0, 0, 0)

    row = lambda w: pl.BlockSpec((tm, w), lambda i: (i, 0))
    full = lambda a: _layer_spec(a, layer)
    tail = [lw["w_out"], lw["norm2"], lw["router_t"], lw["router_b"]]
    return pl.pallas_call(
        _post_kernel,
        grid=(n // tm,),
        in_specs=[row(D_MODEL), row(MLA_HEADS * MLA_V), row(MIX_W), row(MIX_W),
                  pl.BlockSpec((None, 1, 1, 6 * D_MODEL), mod_idx)] + [full(a) for a in tail],
        out_specs=[row(D_MODEL), row(D_MODEL), row(LANES)],
        out_shape=[
            jax.ShapeDtypeStruct((n, D_MODEL), F32),
            jax.ShapeDtypeStruct((n, D_MODEL), BF16),
            jax.ShapeDtypeStruct((n, LANES), F32),
        ],
        compiler_params=pltpu.CompilerParams(dimension_semantics=("arbitrary",),
                                             vmem_limit_bytes=VMEM_LIMIT),
        name="post",
    )(x2d, om, og, orw, mods, *tail)


def _swiglu_act(gu):
    return _silu(gu[:, :D_EXPERT]) * gu[:, D_EXPERT:]


MOE_EXPERTS_PER_STEP = 4


def _moe_kernel(h_ref, x1_ref, cw_ref, mod_ref, wgu_ref, wdn_ref, sgu_ref, sdn_ref, nf_ref, o_ref, *, final):
    step = pl.program_id(1)
    h = h_ref[...]

    @pl.when(step == 0)
    def _():
        o_ref[...] = _mm(_swiglu_act(_mm(h, sgu_ref[...])), sdn_ref[...])

    cw = cw_ref[...]
    lane = _iota(cw.shape, 1)
    acts = []
    for g in range(MOE_EXPERTS_PER_STEP):
        act = _swiglu_act(_mm(h, wgu_ref[g]))
        scale = jnp.sum(jnp.where(lane == step * MOE_EXPERTS_PER_STEP + g, cw, 0.0), axis=-1, keepdims=True)
        acts.append((act * scale).astype(BF16))
    wdn = wdn_ref[...].reshape(MOE_EXPERTS_PER_STEP * D_EXPERT, D_MODEL)
    o_ref[...] += _mm(jnp.concatenate(acts, axis=1), wdn)

    @pl.when(step == N_EXPERTS // MOE_EXPERTS_PER_STEP - 1)
    def _():
        g2 = mod_ref[0][:, 5 * D_MODEL:]
        x2 = x1_ref[...] + g2 * o_ref[...]
        if final:
            x2 = _rms(x2, nf_ref[...])
        o_ref[...] = x2


def _moe(h2, x1, cw, mods, lw, layer, norm_f, tm, seq_len, mod_base, final):
    n = h2.shape[0]
    tiles_per_seq = seq_len // tm if mod_base else 1
    g = MOE_EXPERTS_PER_STEP

    def mod_idx(i, e):
        return (layer, mod_base + i // tiles_per_seq if mod_base else 0, 0, 0)

    row = lambda w, **kw: pl.BlockSpec((tm, w), lambda i, e: (i, 0), **kw)
    full = lambda a: _layer_spec(a, layer, pipeline_mode=pl.Buffered(1))
    return pl.pallas_call(
        functools.partial(_moe_kernel, final=final),
        grid=(n // tm, N_EXPERTS // g),
        in_specs=[row(D_MODEL), row(D_MODEL, pipeline_mode=pl.Buffered(1)), row(LANES),
                  pl.BlockSpec((None, 1, 1, 6 * D_MODEL), mod_idx),
                  pl.BlockSpec((None, g, D_MODEL, 2 * D_EXPERT), lambda i, e: (layer, e, 0, 0)),
                  pl.BlockSpec((None, g, D_EXPERT, D_MODEL), lambda i, e: (layer, e, 0, 0)),
                  full(lw["shared_w_gu"]), full(lw["shared_w_down"]), _const_spec(norm_f)],
        out_specs=row(D_MODEL),
        out_shape=jax.ShapeDtypeStruct((n, D_MODEL), F32),
        compiler_params=pltpu.CompilerParams(dimension_semantics=("arbitrary", "arbitrary"),
                                             vmem_limit_bytes=VMEM_LIMIT),
        name="moe_final" if final else "moe",
    )(h2, x1, cw, mods, lw["moe_w_gu"], lw["moe_w_down"], lw["shared_w_gu"], lw["shared_w_down"], norm_f)


def _constants():
    idx = np.arange(MIX_W)
    same_head = (idx[:, None] // HEAD_DIM) == (idx[None, :] // HEAD_DIM)
    maskbd = jnp.asarray(same_head, BF16)
    eexp = np.zeros((LANES, 4 * MIX_W), np.float32)
    for blk in range(4):
        kind, d = divmod(blk, 2)
        for h in range(HEADS):
            src = kind * 2 * HEADS + d * HEADS + h
            eexp[src, blk * MIX_W + h * HEAD_DIM: blk * MIX_W + (h + 1) * HEAD_DIM] = 1.0
    return {"maskbd": maskbd, "ones_bd": maskbd, "eexp": jnp.asarray(eexp, BF16)}


def _rope_tables(n):
    rows = n // GRID_W
    row = jnp.repeat(jnp.arange(rows, dtype=F32), GRID_W)
    col = jnp.tile(jnp.arange(GRID_W, dtype=F32), rows)
    axis_dim = MLA_ROPE // 2
    inv = jnp.power(ROPE_BASE, -jnp.arange(0, axis_dim, 2, dtype=F32) / axis_dim)
    ang_r = row[:, None] * inv
    ang_c = col[:, None] * inv
    cr, sr, cc, sc = jnp.cos(ang_r), jnp.sin(ang_r), jnp.cos(ang_c), jnp.sin(ang_c)
    zeros = jnp.zeros((n, LANES - MLA_ROPE), F32)
    cos_t = jnp.concatenate([cr, cr, cc, cc, zeros], axis=1)
    sin_t = jnp.concatenate([-sr, sr, -sc, sc, zeros], axis=1)
    return cos_t, sin_t


def _stacked_weights(p):
    w_uq = p["mla_w_uq"].reshape(DEPTH, Q_LORA, MLA_HEADS, MLA_NOPE + MLA_ROPE)
    zq = jnp.zeros((DEPTH, Q_LORA, MLA_HEADS, QH_W - MLA_NOPE - MLA_ROPE), F32)
    w_uq_a = jnp.concatenate([w_uq, zq], axis=-1).reshape(DEPTH, Q_LORA, MLA_HEADS * QH_W).astype(BF16)
    w_uq_sw = jnp.concatenate([jnp.zeros((DEPTH, Q_LORA, MLA_HEADS, MLA_NOPE), F32),
                               w_uq[..., MLA_NOPE + _ROPE_SWAP], zq], axis=-1)
    w_uq_sw = w_uq_sw.reshape(DEPTH, Q_LORA, MLA_HEADS * QH_W).astype(BF16)

    def per_direction(w):
        half = jnp.zeros((DEPTH, 64, MIX_W), F32)
        return jnp.stack([jnp.concatenate([w[:, 0], half], axis=1),
                          jnp.concatenate([half, w[:, 1]], axis=1)], axis=1).astype(BF16)

    row = lambda v: v.reshape(DEPTH, 1, -1)
    pad_row = lambda v: jnp.pad(row(v), ((0, 0), (0, 0), (0, LANES - 2 * HEADS)))
    return {
        "norm1": row(p["norm1"]),
        "w_in": p["w_in"],
        "q_norm": row(p["mla_q_norm"]),
        "w_uq": w_uq_a, "w_uq_sw": w_uq_sw,
        "kv_norm": row(p["mla_kv_norm"]),
        "w_ukv": p["mla_w_ukv"].astype(BF16),
        "gdn_conv": p["gdn_conv"],
        "gdn_alog": pad_row(p["gdn_a_log"]),
        "gdn_dtb": pad_row(p["gdn_dt_bias"]),
        "gdn_norm": jnp.tile(row(p["gdn_norm"]), (1, 1, HEADS)),
        "rwkv_mu_prev": row(p["rwkv_mu_prev"]),
        "rwkv_mu_next": row(p["rwkv_mu_next"]),
        "rwkv_w0": p["rwkv_w0"],
        "rwkv_w2": per_direction(p["rwkv_w2"]),
        "rwkv_a0": p["rwkv_a0"],
        "rwkv_a2": per_direction(p["rwkv_a2"]),
        "rwkv_g2": p["rwkv_g2"].astype(BF16),
        "rwkv_k_k": row(p["rwkv_k_k"]),
        "rwkv_k_a": row(p["rwkv_k_a"]),
        "rwkv_r_k": row(p["rwkv_r_k"]),
        "rwkv_gn_w": row(p["rwkv_gn_w"]),
        "rwkv_gn_b": row(p["rwkv_gn_b"]),
        "w_out": p["w_out"].astype(BF16),
        "norm2": row(p["norm2"]),
        "router_t": jnp.swapaxes(p["moe_router"], 1, 2),
        "router_b": p["moe_bias"].reshape(DEPTH, N_EXPERTS, 1),
        "moe_w_gu": p["moe_w_gu"],
        "moe_w_down": p["moe_w_down"],
        "shared_w_gu": p["shared_w_gu"].astype(BF16),
        "shared_w_down": p["shared_w_down"].astype(BF16),
    }


def _embed_block_diag(s):
    b = s.shape[0]
    eye = jnp.eye(HEADS, dtype=s.dtype)
    out = jnp.einsum("bdhkv,hg->bdhkgv", s, eye)
    return out.reshape(b, 2, MIX_W, MIX_W)


def _trunk(x2d, mods, lw, consts, norm_f, seq_len, mod_base, cache, tm, tq, tm_moe):
    outs = []
    for l in range(DEPTH):
        zm, zg, zab, zr = _inproj(x2d, mods, lw, l, tm, seq_len, mod_base)
        if cache is None:
            o_mla, ckv = _mla(zm, lw, l, seq_len, tq)
            o_gdn, s_gdn = _gdn(zg, zab, lw, l, consts, seq_len)
            o_rwkv, s_rwkv = _rwkv(zr, lw, l, consts, seq_len)
            s_rwkv = jnp.swapaxes(s_rwkv, -1, -2)
        else:
            cckv, ckpe, rc, rs, sg, sr = cache
            o_mla, ckv = _mla(zm, lw, l, seq_len, tq, (cckv, ckpe, rc, rs))
            o_gdn, s_gdn = _gdn(zg, zab, lw, l, consts, seq_len, sg)
            o_rwkv, s_rwkv = _rwkv(zr, lw, l, consts, seq_len, sr)
        x1, h2, cw = _post(x2d, o_mla, o_gdn, o_rwkv, mods, lw, l, tm, seq_len, mod_base)
        x2d = _moe(h2, x1, cw, mods, lw, l, norm_f, tm_moe, seq_len, mod_base, final=(l == DEPTH - 1))
        kpe0 = Q_LORA + KV_LORA
        outs.append((ckv, zm[:, kpe0:kpe0 + MLA_ROPE], s_gdn, s_rwkv))
    return x2d, outs


def kernel(x_prompt, x_sample, cache_mla_ckv, cache_mla_kpe, state_gdn, state_rwkv, c, c_ctx, ada_w, ada_b, norm1, w_in, mla_q_norm, mla_w_uq, mla_kv_norm, mla_w_ukv, gdn_conv, gdn_a_log, gdn_dt_bias, gdn_norm, rwkv_mu_prev, rwkv_mu_next, rwkv_w0, rwkv_w2, rwkv_a0, rwkv_a2, rwkv_g2, rwkv_k_k, rwkv_k_a, rwkv_r_k, rwkv_gn_w, rwkv_gn_b, w_out, norm2, moe_router, moe_bias, moe_w_gu, moe_w_down, shared_w_gu, shared_w_down, norm_f):
    p = dict(norm1=norm1, w_in=w_in, mla_q_norm=mla_q_norm, mla_w_uq=mla_w_uq, mla_kv_norm=mla_kv_norm,
             mla_w_ukv=mla_w_ukv, gdn_conv=gdn_conv, gdn_a_log=gdn_a_log, gdn_dt_bias=gdn_dt_bias,
             gdn_norm=gdn_norm, rwkv_mu_prev=rwkv_mu_prev, rwkv_mu_next=rwkv_mu_next, rwkv_w0=rwkv_w0,
             rwkv_w2=rwkv_w2, rwkv_a0=rwkv_a0, rwkv_a2=rwkv_a2, rwkv_g2=rwkv_g2, rwkv_k_k=rwkv_k_k,
             rwkv_k_a=rwkv_k_a, rwkv_r_k=rwkv_r_k, rwkv_gn_w=rwkv_gn_w, rwkv_gn_b=rwkv_gn_b, w_out=w_out,
             norm2=norm2, moe_router=moe_router, moe_bias=moe_bias, moe_w_gu=moe_w_gu, moe_w_down=moe_w_down,
             shared_w_gu=shared_w_gu, shared_w_down=shared_w_down)
    weights = _stacked_weights(p)
    consts = _constants()
    nf = norm_f.reshape(1, D_MODEL)
    b_ctx, t_ctx, _ = x_prompt.shape
    b_lat, t_lat, _ = x_sample.shape

    cvec8 = jnp.concatenate([c_ctx[None, :], c, jnp.zeros((8 - 1 - b_lat, D_MODEL), F32)], axis=0)
    mods = _adaln(cvec8, ada_w, ada_b)
    mods = mods.reshape(DEPTH, 8, 1, 6 * D_MODEL)

    xp, ctx_outs = _trunk(x_prompt.reshape(b_ctx * t_ctx, D_MODEL), mods, weights, consts, nf,
                          t_ctx, 0, None, tm=512, tq=t_ctx, tm_moe=1024)
    rc, rs = _rope_tables(t_lat)
    ckpe = jnp.pad(cache_mla_kpe, ((0, 0), (0, 0), (0, 0), (0, LANES - MLA_ROPE)))
    cache = (cache_mla_ckv, ckpe, rc, rs, _embed_block_diag_layers(state_gdn),
             _embed_block_diag_layers(jnp.swapaxes(state_rwkv, -1, -2)))
    xs, _ = _trunk(x_sample.reshape(b_lat * t_lat, D_MODEL), mods, weights, consts, nf,
                   t_lat, 1, cache, tm=512, tq=256, tm_moe=1024)

    y_prompt = xp.reshape(b_ctx, t_ctx, D_MODEL)
    y_sample = xs.reshape(b_lat, t_lat, D_MODEL)
    new_ckv = jnp.stack([o[0].reshape(b_ctx, t_ctx, KV_LORA) for o in ctx_outs], axis=1)
    new_kpe = jnp.stack([o[1].reshape(b_ctx, t_ctx, MLA_ROPE) for o in ctx_outs], axis=1)
    new_gdn = jnp.stack([o[2] for o in ctx_outs], axis=1)
    new_rwkv = jnp.stack([o[3] for o in ctx_outs], axis=1)
    return (y_prompt, y_sample, new_ckv, new_kpe, new_gdn, new_rwkv)


def _embed_block_diag_layers(s):
    b = s.shape[0]
    return _embed_block_diag(s.reshape(b * DEPTH, 2, HEADS, HEAD_DIM, HEAD_DIM)).reshape(
        b, DEPTH, 2, MIX_W, MIX_W)
```

```python
import functools

import numpy as np
import jax
import jax.numpy as jnp
from jax import lax
from jax.experimental import pallas as pl
from jax.experimental.pallas import tpu as pltpu
from jax.experimental.pallas import tpu_sc as plsc

F32 = jnp.float32
BF16 = jnp.bfloat16

D_MODEL = 1024
BATCH = 32
SEQ = 256
DEPTH = 2
DEC_BATCH = 2
DEC_SEQ = 1024
PAST_LEN = 512
GRID_W = 64
NORM_EPS = 1e-6

MLA_HEADS = 4
MLA_NOPE = 128
MLA_ROPE = 64
MLA_V = 128
Q_LORA = 384
KV_LORA = 256
ROPE_BASE = 10000.0
MLA_SCALE = (MLA_NOPE + MLA_ROPE) ** -0.5

HEADS = 4
HEAD_DIM = 64
MIX_W = HEADS * HEAD_DIM
GDN_CONV_CH = 3 * MIX_W
CHUNK = 64
RWKV_GN_EPS = 64e-5

N_EXPERTS = 64
TOP_K = 8
N_GROUPS = 8
GROUP_SIZE = N_EXPERTS // N_GROUPS
TOPK_GROUPS = 4
D_EXPERT = 256
ROUTE_SCALE = 2.5

P_MLA = Q_LORA + KV_LORA + MLA_ROPE
P_GDN = GDN_CONV_CH + MIX_W + 4 * HEADS
P_RWKV = 3 * MIX_W + 128 + 128 + 128

LANES = 128
ZM_W = Q_LORA + KV_LORA + 2 * LANES
ZG_W = GDN_CONV_CH + MIX_W
ZR_W = P_RWKV
QH_W = 2 * LANES
VMEM_LIMIT = 56 * 1024 * 1024
POST_TM = 512

_ROPE_SWAP = np.concatenate([np.arange(16, 32), np.arange(0, 16), np.arange(48, 64), np.arange(32, 48)])


def _sigmoid(x):
    return 1.0 / (1.0 + jnp.exp(-x))


def _silu(x):
    return x * _sigmoid(x)


def _softplus(x):
    return jnp.maximum(x, 0.0) + jnp.log(1.0 + jnp.exp(-jnp.abs(x)))


def _rms(x, g, eps=NORM_EPS):
    return x * lax.rsqrt(jnp.mean(x * x, axis=-1, keepdims=True) + eps) * g


def _mm(a, b):
    return jnp.dot(a.astype(BF16), b.astype(BF16), preferred_element_type=F32)


def _mm_nt(a, b):
    return lax.dot_general(a.astype(BF16), b.astype(BF16), (((1,), (1,)), ((), ())),
                           preferred_element_type=F32)


def _mm_tn(a, b):
    return lax.dot_general(a.astype(BF16), b.astype(BF16), (((0,), (0,)), ((), ())),
                           preferred_element_type=F32)


def _split3(x):
    p1 = x.astype(BF16)
    r1 = x - p1.astype(F32)
    p2 = r1.astype(BF16)
    r2 = r1 - p2.astype(F32)
    return p1, p2, r2.astype(BF16)


def _mm_sel_l(sel, x):
    p1, p2, p3 = _split3(x)
    return _mm(sel, p1) + _mm(sel, p2) + _mm(sel, p3)


def _mm_sel_r(x, sel):
    p1, p2, p3 = _split3(x)
    return _mm(p1, sel) + _mm(p2, sel) + _mm(p3, sel)


def _iota(shape, dim):
    return lax.broadcasted_iota(jnp.int32, shape, dim)


def _layer_spec(a, layer, **kw):
    nd = a.ndim - 1
    return pl.BlockSpec((None,) + a.shape[1:], lambda *_: (layer,) + (0,) * nd, **kw)


def _const_spec(a, **kw):
    return pl.BlockSpec(a.shape, lambda *_: (0,) * a.ndim, **kw)


def _bd(x, maskbd):
    xb = x.astype(BF16)
    return jnp.concatenate([xb] * HEADS, axis=0) * maskbd


def _chunk_masks(rev):
    row = _iota((CHUNK, MIX_W), 0)
    col = jnp.bitwise_and(_iota((CHUNK, MIX_W), 1), HEAD_DIM - 1)
    r2 = _iota((CHUNK, CHUNK), 0)
    c2 = _iota((CHUNK, CHUNK), 1)
    if rev:
        inc, strict, tri = row <= col, row < col, r2 <= c2
    else:
        inc, strict, tri = row >= col, row > col, r2 >= c2
    eye = jnp.where(row == col, 1.0, 0.0).astype(F32)
    return inc, strict, jnp.where(tri, 1.0, 0.0).astype(BF16), eye


def _split2(x):
    hi = x.astype(BF16)
    return hi, (x - hi.astype(F32)).astype(BF16)


def _mm_bd3(x, p, maskbd):
    n = x.shape[0]
    xh, xl = _split2(x)
    ph, pl_ = _split2(p)
    r = jnp.dot(jnp.concatenate([xh, xl], axis=0), _bd(ph, maskbd), preferred_element_type=F32)
    return r[:n] + r[n:] + jnp.dot(xh, _bd(pl_, maskbd), preferred_element_type=F32)


def _neumann_inverse(a_list, eye_list, maskbd):
    bs = [-a for a in a_list]
    ms = [eye + b for eye, b in zip(eye_list, bs)]
    ps = [_mm_bd3(b, b, maskbd) for b in bs]
    for _ in range(4):
        boths = [_mm_bd3(jnp.concatenate([m, p], axis=0), p, maskbd) for m, p in zip(ms, ps)]
        ms = [m + both[:CHUNK] for m, both in zip(ms, boths)]
        ps = [both[CHUNK:] for both in boths]
    return [m + _mm_bd3(m, p, maskbd) for m, p in zip(ms, ps)]


def _adaln_kernel(c_ref, w_ref, b_ref, o_ref):
    cv = c_ref[...]
    o_ref[0] = _mm(_silu(cv), w_ref[0]) + b_ref[0]


def _adaln(cvec8, ada_w, ada_b):
    tn = 768
    n_out = 6 * D_MODEL
    return pl.pallas_call(
        _adaln_kernel,
        grid=(DEPTH, n_out // tn),
        in_specs=[
            pl.BlockSpec((8, D_MODEL), lambda l, j: (0, 0)),
            pl.BlockSpec((1, D_MODEL, tn), lambda l, j: (l, 0, j)),
            pl.BlockSpec((1, 1, tn), lambda l, j: (l, 0, j)),
        ],
        out_specs=pl.BlockSpec((1, 8, tn), lambda l, j: (l, 0, j)),
        out_shape=jax.ShapeDtypeStruct((DEPTH, 8, n_out), F32),
        compiler_params=pltpu.CompilerParams(dimension_semantics=("arbitrary", "arbitrary"),
                                             vmem_limit_bytes=VMEM_LIMIT),
        name="adaln",
    )(cvec8, ada_w, ada_b.reshape(DEPTH, 1, n_out))


_KPE0 = Q_LORA + KV_LORA
_W_IN_MOVES = (
    [(0, 0, P_MLA)]
    + [(P_MLA + LANES - MLA_ROPE + 16 * j, _KPE0 + 16 * int(_ROPE_SWAP[16 * j] // 16), 16) for j in range(4)]
    + [(ZM_W, P_MLA, ZG_W), (ZM_W + ZG_W, P_MLA + ZG_W, 4 * HEADS), (ZM_W + ZG_W + LANES, P_MLA + P_GDN, P_RWKV)]
)
W_IN_PAD = ZM_W + ZG_W + LANES + ZR_W


def _inproj_kernel(x_ref, mod_ref, n1_ref, w_ref, zm_ref, zg_ref, zab_ref, zr_ref, w_s):
    @pl.when(pl.program_id(0) == 0)
    def _():
        w_s[...] = jnp.zeros(w_s.shape, BF16)
        for dst, src, width in _W_IN_MOVES:
            w_s[:, dst:dst + width] = w_ref[:, src:src + width].astype(BF16)

    m = mod_ref[0]
    sh = m[:, 0:D_MODEL]
    sc = m[:, D_MODEL:2 * D_MODEL]
    h = _rms(x_ref[...], n1_ref[...]) * (1.0 + sc) + sh
    z = _mm(h, w_s[...])
    o1 = ZM_W
    o2 = o1 + ZG_W
    o3 = o2 + LANES
    zm_ref[...] = z[:, :o1]
    zg_ref[...] = z[:, o1:o2]
    zab_ref[...] = z[:, o2:o3]
    zr_ref[...] = z[:, o3:]


def _inproj(x2d, mods, lw, layer, tm, seq_len, mod_base):
    n = x2d.shape[0]
    tiles_per_seq = seq_len // tm if mod_base else 1

    def mod_idx(i):
        return (layer, mod_base + i // tiles_per_seq if mod_base else 0, 0, 0)

    return pl.pallas_call(
        _inproj_kernel,
        grid=(n // tm,),
        in_specs=[
            pl.BlockSpec((tm, D_MODEL), lambda i: (i, 0)),
            pl.BlockSpec((None, 1, 1, 6 * D_MODEL), mod_idx),
            _layer_spec(lw["norm1"], layer),
            _layer_spec(lw["w_in"], layer, pipeline_mode=pl.Buffered(1)),
        ],
        out_specs=[
            pl.BlockSpec((tm, ZM_W), lambda i: (i, 0)),
            pl.BlockSpec((tm, ZG_W), lambda i: (i, 0)),
            pl.BlockSpec((tm, LANES), lambda i: (i, 0)),
            pl.BlockSpec((tm, ZR_W), lambda i: (i, 0)),
        ],
        out_shape=[
            jax.ShapeDtypeStruct((n, ZM_W), F32),
            jax.ShapeDtypeStruct((n, ZG_W), F32),
            jax.ShapeDtypeStruct((n, LANES), F32),
            jax.ShapeDtypeStruct((n, ZR_W), F32),
        ],
        scratch_shapes=[pltpu.VMEM((D_MODEL, W_IN_PAD), BF16)],
        compiler_params=pltpu.CompilerParams(dimension_semantics=("arbitrary",),
                                             vmem_limit_bytes=VMEM_LIMIT),
        name="inproj",
    )(x2d, mods, lw["norm1"], lw["w_in"])


def _mla_kernel(*refs, seq_len, tq, past, cached):
    if cached:
        (zm_ref, cckv_ref, ckpe_ref, rc_ref, rs_ref, qn_ref, wuq_ref, wuqs_ref, kvn_ref, wukv_ref,
         o_ref, ckv_ref, k_s, v_s) = refs
    else:
        (zm_ref, qn_ref, wuq_ref, kvn_ref, wukv_ref, o_ref, ckv_ref, k_s, v_s) = refs
    qi = pl.program_id(1)
    o_kpe = Q_LORA + KV_LORA

    @pl.when(qi == 0)
    def _():
        zm = zm_ref[...]
        ckv = _rms(zm[:, Q_LORA:o_kpe], kvn_ref[...])
        ckv_ref[...] = ckv
        kpe = zm[:, o_kpe:o_kpe + LANES]
        if cached:
            kpe = kpe * rc_ref[...] + zm[:, o_kpe + LANES:o_kpe + 2 * LANES] * rs_ref[...]
            kvc = _mm(cckv_ref[0], wukv_ref[...])
            kpc = ckpe_ref[0].astype(BF16)
        kv = _mm(ckv, wukv_ref[...])
        kpe = kpe.astype(BF16)
        for h in range(MLA_HEADS):
            c0 = h * QH_W
            if cached:
                k_s[0:past, c0:c0 + LANES] = kvc[:, c0:c0 + LANES].astype(BF16)
                k_s[0:past, c0 + LANES:c0 + QH_W] = kpc
                v_s[0:past, h * MLA_V:(h + 1) * MLA_V] = kvc[:, c0 + LANES:c0 + QH_W].astype(BF16)
            k_s[past:past + seq_len, c0:c0 + LANES] = kv[:, c0:c0 + LANES].astype(BF16)
            k_s[past:past + seq_len, c0 + LANES:c0 + QH_W] = kpe
            v_s[past:past + seq_len, h * MLA_V:(h + 1) * MLA_V] = kv[:, c0 + LANES:c0 + QH_W].astype(BF16)

    r0 = pl.multiple_of(qi * tq, tq)
    zq = zm_ref[pl.ds(r0, tq), :]
    cq = _rms(zq[:, :Q_LORA], qn_ref[...])
    q = _mm(cq, wuq_ref[...])
    if cached:
        qs = _mm(cq, wuqs_ref[...])
        ones = jnp.ones((tq, LANES), F32)
        zeros = jnp.zeros((tq, LANES), F32)
        qc = jnp.concatenate([ones, rc_ref[pl.ds(r0, tq), :]], axis=1)
        qsn = jnp.concatenate([zeros, rs_ref[pl.ds(r0, tq), :]], axis=1)
    for h in range(MLA_HEADS):
        c0 = h * QH_W
        qh = q[:, c0:c0 + QH_W]
        if cached:
            qh = qh * qc + qs[:, c0:c0 + QH_W] * qsn
        s = _mm_nt(qh, k_s[:, c0:c0 + QH_W]) * MLA_SCALE
        e = jnp.exp(s - jnp.max(s, axis=-1, keepdims=True))
        den = jnp.sum(e, axis=-1, keepdims=True)
        o_ref[:, h * MLA_V:(h + 1) * MLA_V] = _mm(e, v_s[:, h * MLA_V:(h + 1) * MLA_V]) / den


def _mla(zm, lw, layer, seq_len, tq, cache=None):
    n = zm.shape[0]
    nb = n // seq_len
    cached = cache is not None
    past = PAST_LEN if cached else 0
    tk = past + seq_len
    seq_spec = lambda w: pl.BlockSpec((seq_len, w), lambda b, q: (b, 0))
    lay = lambda a: _layer_spec(a, layer)
    if cached:
        cckv, ckpe, rc, rs = cache
        args = [zm, cckv, ckpe, rc, rs, lw["q_norm"], lw["w_uq"], lw["w_uq_sw"], lw["kv_norm"], lw["w_ukv"]]
        in_specs = [seq_spec(ZM_W),
                    pl.BlockSpec((1, None, past, KV_LORA), lambda b, q: (b, layer, 0, 0)),
                    pl.BlockSpec((1, None, past, LANES), lambda b, q: (b, layer, 0, 0)),
                    _const_spec(rc), _const_spec(rs)] + [lay(a) for a in args[5:]]
    else:
        args = [zm, lw["q_norm"], lw["w_uq"], lw["kv_norm"], lw["w_ukv"]]
        in_specs = [seq_spec(ZM_W)] + [lay(a) for a in args[1:]]
    return pl.pallas_call(
        functools.partial(_mla_kernel, seq_len=seq_len, tq=tq, past=past, cached=cached),
        grid=(nb, seq_len // tq),
        in_specs=in_specs,
        out_specs=[
            pl.BlockSpec((tq, MLA_HEADS * MLA_V), lambda b, q: (b * (seq_len // tq) + q, 0)),
            pl.BlockSpec((seq_len, KV_LORA), lambda b, q: (b, 0)),
        ],
        out_shape=[
            jax.ShapeDtypeStruct((n, MLA_HEADS * MLA_V), F32),
            jax.ShapeDtypeStruct((n, KV_LORA), F32),
        ],
        scratch_shapes=[
            pltpu.VMEM((tk, MLA_HEADS * QH_W), BF16),
            pltpu.VMEM((tk, MLA_HEADS * MLA_V), BF16),
        ],
        compiler_params=pltpu.CompilerParams(dimension_semantics=("arbitrary", "arbitrary"),
                                             vmem_limit_bytes=VMEM_LIMIT),
        name="mla_lat" if cached else "mla_ctx",
    )(*args)


CHUNK_GROUP = 4
GDN_PRE_W = 5 * MIX_W


def _for_chunk_groups(n_chunks, fn):
    if n_chunks == CHUNK_GROUP:
        fn(0)
    else:
        def body(gi, carry):
            fn(gi * CHUNK_GROUP)
            return carry
        lax.fori_loop(0, n_chunks // CHUNK_GROUP, body, 0)


def _gdn_prepare(items, maskbd):
    n = range(len(items))
    qs, ks, vs, gs, betas, masks, revs = zip(*items)
    gcs = [_mm_sel_l(masks[i][2], gs[i]) for i in n]
    decays = []
    for i in n:
        inc, eye = masks[i][0], masks[i][3]
        gc_row = jnp.sum(eye * gcs[i], axis=0, keepdims=True)
        decays.append(jnp.where(inc, jnp.exp(jnp.where(inc, gcs[i] - gc_row, 0.0)), 0.0))
    kbs = [ks[i] * betas[i] for i in n]
    aqs = [_mm_nt(jnp.concatenate([kbs[i], qs[i]], axis=0), _bd(ks[i], maskbd)) for i in n]
    a_mats = [jnp.where(masks[i][1], aqs[i][:CHUNK] * decays[i], 0.0) for i in n]
    t_invs = _neumann_inverse(a_mats, [m[3] for m in masks], maskbd)
    egcs = [jnp.exp(gc) for gc in gcs]
    uws = [_mm(t_invs[i], jnp.concatenate([_bd(vs[i] * betas[i], maskbd), _bd(kbs[i] * egcs[i], maskbd)], axis=1))
           for i in n]
    out = []
    for i in n:
        g_last = gcs[i][0:1] if revs[i] else gcs[i][CHUNK - 1:CHUNK]
        pre = jnp.concatenate([uws[i], qs[i] * egcs[i], aqs[i][CHUNK:] * decays[i],
                               ks[i] * jnp.exp(g_last - gcs[i])], axis=1)
        out.append((pre, jnp.broadcast_to(jnp.exp(g_last), (8, MIX_W))))
    return out


def _gdn_step(pre, egl, s_bd, maskbd):
    w = MIX_W
    u, wq_in, qk, kdec = pre[:, :w], pre[:, w:3 * w], pre[:, 3 * w:4 * w], pre[:, 4 * w:]
    wq = _mm(jnp.concatenate([wq_in[:, :w], wq_in[:, w:]], axis=0), s_bd)
    v_new = u - wq[:CHUNK]
    o = wq[CHUNK:] + _mm(qk, _bd(v_new, maskbd))
    s_new = s_bd * egl + _mm_tn(kdec, v_new) * maskbd.astype(F32)
    return o, s_new


def _gdn_kernel(*refs, seq_len, cached):
    if cached:
        (zg_ref, zab_ref, s0_ref, conv_ref, alog_ref, dtb_ref, gn_ref, eexp_ref, ones_ref, maskbd_ref,
         o_ref, sout_ref, q_s, k_s, v_s, ge_s, pre_s, gl_s, oacc_s, st_s) = refs
    else:
        (zg_ref, zab_ref, conv_ref, alog_ref, dtb_ref, gn_ref, eexp_ref, ones_ref, maskbd_ref,
         o_ref, sout_ref, q_s, k_s, v_s, ge_s, pre_s, gl_s, oacc_s, st_s) = refs
    t = seq_len
    z = zg_ref[:, :GDN_CONV_CH]
    rowi = _iota((t, 1), 0)
    zp = jnp.where(rowi == 0, 0.0, pltpu.roll(z, 1, 0))
    zn = jnp.where(rowi == t - 1, 0.0, pltpu.roll(z, t - 1, 0))
    cw = conv_ref[...]
    qkv = _silu(zp * cw[0:1] + z * cw[1:2] + zn * cw[2:3])
    ones_bd = ones_ref[...]
    q = qkv[:, :MIX_W]
    k = qkv[:, MIX_W:2 * MIX_W]
    q_s[...] = q * lax.rsqrt(_mm_sel_r(q * q, ones_bd) + 1e-6) * (HEAD_DIM ** -0.5)
    k_s[...] = k * lax.rsqrt(_mm_sel_r(k * k, ones_bd) + 1e-6)
    v_s[...] = qkv[:, 2 * MIX_W:]
    ab = zab_ref[...]
    lane = _iota((t, LANES), 1)
    gb = jnp.where(lane < 2 * HEADS, -jnp.exp(alog_ref[...]) * _softplus(ab + dtb_ref[...]), _sigmoid(ab))
    ge_s[...] = _mm_sel_r(gb, eexp_ref[...])
    oacc_s[...] = jnp.zeros((t, MIX_W), F32)
    if cached:
        st_s[...] = s0_ref[0]
    else:
        st_s[...] = jnp.zeros((2, MIX_W, MIX_W), F32)
    maskbd = maskbd_ref[...]
    masks = (_chunk_masks(False), _chunk_masks(True))
    n_chunks = t // CHUNK

    def prepare_group(c0):
        where, items = [], []
        for j in range(CHUNK_GROUP):
            c = c0 + j
            rows = pl.ds(pl.multiple_of(c * CHUNK, CHUNK), CHUNK)
            for d in range(2):
                where.append((d, c, rows))
                items.append((q_s[rows, :], k_s[rows, :], v_s[rows, :], ge_s[rows, d * MIX_W:(d + 1) * MIX_W],
                              ge_s[rows, (2 + d) * MIX_W:(3 + d) * MIX_W], masks[d], d == 1))
        for (d, c, rows), (pre, egl) in zip(where, _gdn_prepare(items, maskbd)):
            pre_s[d, rows, :] = pre
            gl_s[d, pl.ds(pl.multiple_of(c * 8, 8), 8), :] = egl

    _for_chunk_groups(n_chunks, prepare_group)

    def body(i, carry):
        for d in range(2):
            c = i if d == 0 else n_chunks - 1 - i
            rows = pl.ds(pl.multiple_of(c * CHUNK, CHUNK), CHUNK)
            egl = gl_s[d, pl.ds(pl.multiple_of(c * 8, 8), 8), :][0:1]
            o, s_new = _gdn_step(pre_s[d, rows, :], egl, st_s[d], maskbd)
            oacc_s[rows, :] = oacc_s[rows, :] + o
            st_s[d] = s_new
        return carry

    lax.fori_loop(0, n_chunks, body, 0)
    o = oacc_s[...]
    ms = _mm_sel_r(o * o, ones_bd) * (1.0 / HEAD_DIM)
    gate = zg_ref[:, GDN_CONV_CH:]
    o_ref[...] = o * lax.rsqrt(ms + NORM_EPS) * gn_ref[...] * _silu(gate)
    for d in range(2):
        for h in range(HEADS):
            sl = slice(h * HEAD_DIM, (h + 1) * HEAD_DIM)
            sout_ref[0, d, h] = st_s[d, sl, sl]


def _gdn(zg, zab, lw, layer, consts, seq_len, s0_bd=None):
    n = zg.shape[0]
    nb = n // seq_len
    cached = s0_bd is not None
    args = [zg, zab]
    in_specs = [pl.BlockSpec((seq_len, ZG_W), lambda b: (b, 0)),
                pl.BlockSpec((seq_len, LANES), lambda b: (b, 0))]
    if cached:
        args.append(s0_bd)
        in_specs.append(pl.BlockSpec((1, None, 2, MIX_W, MIX_W), lambda b: (b, layer, 0, 0, 0)))
    layered = [lw["gdn_conv"], lw["gdn_alog"], lw["gdn_dtb"], lw["gdn_norm"]]
    const = [consts["eexp"], consts["ones_bd"], consts["maskbd"]]
    args += layered + const
    in_specs += [_layer_spec(a, layer) for a in layered] + [_const_spec(a) for a in const]
    return pl.pallas_call(
        functools.partial(_gdn_kernel, seq_len=seq_len, cached=cached),
        grid=(nb,),
        in_specs=in_specs,
        out_specs=[
            pl.BlockSpec((seq_len, MIX_W), lambda b: (b, 0)),
            pl.BlockSpec((1, 2, HEADS, HEAD_DIM, HEAD_DIM), lambda b: (b, 0, 0, 0, 0)),
        ],
        out_shape=[
            jax.ShapeDtypeStruct((n, MIX_W), F32),
            jax.ShapeDtypeStruct((nb, 2, HEADS, HEAD_DIM, HEAD_DIM), F32),
        ],
        scratch_shapes=[
            pltpu.VMEM((seq_len, MIX_W), F32),
            pltpu.VMEM((seq_len, MIX_W), F32),
            pltpu.VMEM((seq_len, MIX_W), F32),
            pltpu.VMEM((seq_len, 4 * MIX_W), F32),
            pltpu.VMEM((2, seq_len, GDN_PRE_W), F32),
            pltpu.VMEM((2, seq_len // CHUNK * 8, MIX_W), F32),
            pltpu.VMEM((seq_len, MIX_W), F32),
            pltpu.VMEM((2, MIX_W, MIX_W), F32),
        ],
        compiler_params=pltpu.CompilerParams(dimension_semantics=("arbitrary",),
                                             vmem_limit_bytes=VMEM_LIMIT),
        name="gdn_lat" if cached else "gdn_ctx",
    )(*args)


RWKV_PRE_W = 7 * MIX_W


def _rwkv_prepare(items, maskbd, eye_full):
    n = range(len(items))
    rs, kds, vs, kks, bs, lws, masks, revs = zip(*items)
    cums = [_mm_sel_l(masks[i][2], lws[i]) for i in n]
    einvs = [jnp.exp(-c) for c in cums]
    kts = [kks[i] * jnp.exp(cums[i] - lws[i]) for i in n]
    rts = [rs[i] * jnp.exp(cums[i]) for i in n]
    krs = [jnp.concatenate([kts[i], rts[i]], axis=0) for i in n]
    lb_alls = [_mm_nt(krs[i], _bd(bs[i] * einvs[i], maskbd)) for i in n]
    lk_alls = [_mm_nt(krs[i], _bd(kds[i] * einvs[i], maskbd)) for i in n]
    lbs = [jnp.where(masks[i][1], lb_alls[i][:CHUNK], 0.0) for i in n]
    t_invs = _neumann_inverse(lbs, [m[3] for m in masks], maskbd)
    lvs = [_mm(jnp.concatenate([jnp.where(masks[i][1], lk_alls[i][:CHUNK], 0.0),
                                jnp.where(masks[i][0], lk_alls[i][CHUNK:], 0.0)], axis=0), _bd(vs[i], maskbd))
           for i in n]
    tkps = [_mm(t_invs[i], jnp.concatenate([_bd(kts[i], maskbd), _bd(lvs[i][:CHUNK], maskbd)], axis=1)) for i in n]
    out = []
    for i in n:
        c_last = cums[i][0:1] if revs[i] else cums[i][CHUNK - 1:CHUNK]
        tail = jnp.exp(c_last - cums[i])
        rb = jnp.where(masks[i][0], lb_alls[i][CHUNK:], 0.0)
        pre = jnp.concatenate([tkps[i][:, :MIX_W], rts[i], tkps[i][:, MIX_W:], lvs[i][CHUNK:], rb,
                               kds[i] * tail, bs[i] * tail], axis=1)
        gcol = jnp.sum(eye_full * jnp.exp(c_last), axis=1, keepdims=True)
        out.append((pre, jnp.broadcast_to(gcol, (MIX_W, LANES))))
    return out


def _rwkv_step(pre, v, gcol, z_bd, maskbd):
    w = MIX_W
    pr = _mm(jnp.concatenate([pre[:, :w], pre[:, w:2 * w]], axis=0), z_bd)
    p = pr[:CHUNK] + pre[:, 2 * w:3 * w]
    o = pr[CHUNK:] + pre[:, 3 * w:4 * w] - _mm(pre[:, 4 * w:5 * w], _bd(p, maskbd))
    upd = _mm_tn(jnp.concatenate([pre[:, 5 * w:6 * w], pre[:, 6 * w:]], axis=0), jnp.concatenate([v, -p], axis=0))
    z_new = z_bd * jnp.concatenate([gcol, gcol], axis=1) + upd * maskbd.astype(F32)
    return o, z_new


def _rwkv_kernel(*refs, seq_len, cached):
    if cached:
        (zr_ref, s0_ref, mup_ref, mun_ref, w0_ref, w2_ref, a0_ref, a2_ref, g2_ref, kk_ref, ka_ref, rk_ref,
         gnw_ref, gnb_ref, ones_ref, maskbd_ref, o_ref, sout_ref,
         r_s, v_s, kk_s, dir_s, bg_s, pre_s, gcol_s, oacc_s, st_s) = refs
    else:
        (zr_ref, mup_ref, mun_ref, w0_ref, w2_ref, a0_ref, a2_ref, g2_ref, kk_ref, ka_ref, rk_ref,
         gnw_ref, gnb_ref, ones_ref, maskbd_ref, o_ref, sout_ref,
         r_s, v_s, kk_s, dir_s, bg_s, pre_s, gcol_s, oacc_s, st_s) = refs
    t = seq_len
    z = zr_ref[...]
    rowi = _iota((t, 1), 0)
    zp = jnp.where(rowi == 0, 0.0, pltpu.roll(z, 1, 0))
    zn = jnp.where(rowi == t - 1, 0.0, pltpu.roll(z, t - 1, 0))
    z = z + mup_ref[...] * (zp - z) + mun_ref[...] * (zn - z)
    w = MIX_W
    r = z[:, :w]
    k = z[:, w:2 * w]
    v = z[:, 2 * w:3 * w]
    wd = jnp.tanh(z[:, 3 * w:3 * w + LANES])
    ad = z[:, 3 * w + LANES:3 * w + 2 * LANES]
    gd = _sigmoid(z[:, 3 * w + 2 * LANES:])
    ones_bd = ones_ref[...]
    kk = k * kk_ref[...]
    kk = kk * lax.rsqrt(_mm_sel_r(kk * kk, ones_bd) + 1e-6)
    r_s[...] = r
    v_s[...] = v
    kk_s[...] = kk
    bonus = jnp.zeros((t, w), F32)
    for d in range(2):
        w_log = -_softplus(-(w0_ref[d:d + 1] + _mm(wd, w2_ref[d]))) - 0.5
        a = _sigmoid(a0_ref[d:d + 1] + _mm(ad, a2_ref[d]))
        kd = k * (1.0 + (a - 1.0) * ka_ref[...])
        dir_s[:, (3 * d) * w:(3 * d + 1) * w] = -jnp.exp(w_log)
        dir_s[:, (3 * d + 1) * w:(3 * d + 2) * w] = kd
        dir_s[:, (3 * d + 2) * w:(3 * d + 3) * w] = kk * a
        bonus = bonus + _mm_sel_r(r * kd * rk_ref[...], ones_bd) * v
    bg_s[:, :w] = bonus
    bg_s[:, w:] = _mm(gd, g2_ref[...])
    oacc_s[...] = jnp.zeros((t, w), F32)
    if cached:
        st_s[...] = s0_ref[0]
    else:
        st_s[...] = jnp.zeros((2, w, w), F32)
    maskbd = maskbd_ref[...]
    masks = (_chunk_masks(False), _chunk_masks(True))
    eye_full = jnp.where(_iota((w, w), 0) == _iota((w, w), 1), 1.0, 0.0).astype(F32)
    n_chunks = t // CHUNK

    def prepare_group(c0):
        where, items = [], []
        for j in range(CHUNK_GROUP):
            c = c0 + j
            rows = pl.ds(pl.multiple_of(c * CHUNK, CHUNK), CHUNK)
            for d in range(2):
                where.append((d, c, rows))
                items.append((r_s[rows, :], dir_s[rows, (3 * d + 1) * w:(3 * d + 2) * w], v_s[rows, :], kk_s[rows, :],
                              dir_s[rows, (3 * d + 2) * w:(3 * d + 3) * w], dir_s[rows, (3 * d) * w:(3 * d + 1) * w],
                              masks[d], d == 1))
        for (d, c, rows), (pre, gcol) in zip(where, _rwkv_prepare(items, maskbd, eye_full)):
            pre_s[d, rows, :] = pre
            gcol_s[d, pl.ds(pl.multiple_of(c * w, w), w), :] = gcol

    _for_chunk_groups(n_chunks, prepare_group)

    def body(i, carry):
        for d in range(2):
            c = i if d == 0 else n_chunks - 1 - i
            rows = pl.ds(pl.multiple_of(c * CHUNK, CHUNK), CHUNK)
            gcol = gcol_s[d, pl.ds(pl.multiple_of(c * w, w), w), :]
            o, z_new = _rwkv_step(pre_s[d, rows, :], v_s[rows, :], gcol, st_s[d], maskbd)
            oacc_s[rows, :] = oacc_s[rows, :] + o
            st_s[d] = z_new
        return carry

    lax.fori_loop(0, n_chunks, body, 0)
    o = oacc_s[...]
    inv_n = 1.0 / HEAD_DIM
    mu = _mm_sel_r(o, ones_bd) * inv_n
    oc = o - mu
    var = _mm_sel_r(oc * oc, ones_bd) * inv_n
    y = oc * lax.rsqrt(var + RWKV_GN_EPS) * gnw_ref[...] + gnb_ref[...]
    o_ref[...] = (y + bg_s[:, :w]) * bg_s[:, w:]
    for d in range(2):
        for h in range(HEADS):
            sl = slice(h * HEAD_DIM, (h + 1) * HEAD_DIM)
            sout_ref[0, d, h] = st_s[d, sl, sl]


def _rwkv(zr, lw, layer, consts, seq_len, s0_bd=None):
    n = zr.shape[0]
    nb = n // seq_len
    cached = s0_bd is not None
    args = [zr]
    in_specs = [pl.BlockSpec((seq_len, ZR_W), lambda b: (b, 0))]
    if cached:
        args.append(s0_bd)
        in_specs.append(pl.BlockSpec((1, None, 2, MIX_W, MIX_W), lambda b: (b, layer, 0, 0, 0)))
    layered = [lw["rwkv_mu_prev"], lw["rwkv_mu_next"], lw["rwkv_w0"], lw["rwkv_w2"], lw["rwkv_a0"], lw["rwkv_a2"],
               lw["rwkv_g2"], lw["rwkv_k_k"], lw["rwkv_k_a"], lw["rwkv_r_k"], lw["rwkv_gn_w"], lw["rwkv_gn_b"]]
    const = [consts["ones_bd"], consts["maskbd"]]
    args += layered + const
    in_specs += [_layer_spec(a, layer) for a in layered] + [_const_spec(a) for a in const]
    return pl.pallas_call(
        functools.partial(_rwkv_kernel, seq_len=seq_len, cached=cached),
        grid=(nb,),
        in_specs=in_specs,
        out_specs=[
            pl.BlockSpec((seq_len, MIX_W), lambda b: (b, 0)),
            pl.BlockSpec((1, 2, HEADS, HEAD_DIM, HEAD_DIM), lambda b: (b, 0, 0, 0, 0)),
        ],
        out_shape=[
            jax.ShapeDtypeStruct((n, MIX_W), F32),
            jax.ShapeDtypeStruct((nb, 2, HEADS, HEAD_DIM, HEAD_DIM), F32),
        ],
        scratch_shapes=[
            pltpu.VMEM((seq_len, MIX_W), F32),
            pltpu.VMEM((seq_len, MIX_W), F32),
            pltpu.VMEM((seq_len, MIX_W), F32),
            pltpu.VMEM((seq_len, 6 * MIX_W), F32),
            pltpu.VMEM((seq_len, 2 * MIX_W), F32),
            pltpu.VMEM((2, seq_len, RWKV_PRE_W), F32),
            pltpu.VMEM((2, seq_len // CHUNK * MIX_W, LANES), F32),
            pltpu.VMEM((seq_len, MIX_W), F32),
            pltpu.VMEM((2, MIX_W, MIX_W), F32),
        ],
        compiler_params=pltpu.CompilerParams(dimension_semantics=("arbitrary",),
                                             vmem_limit_bytes=VMEM_LIMIT),
        name="rwkv_lat" if cached else "rwkv_ctx",
    )(*args)


def _route(logits_t, bias):
    tm = logits_t.shape[1]
    neg = -jnp.inf
    sc = _sigmoid(logits_t)
    sc3 = sc.reshape(N_GROUPS, GROUP_SIZE, tm)
    sel = (sc + bias).reshape(N_GROUPS, GROUP_SIZE, tm)
    si = _iota(sel.shape, 1).astype(F32)
    m1 = jnp.max(sel, axis=1, keepdims=True)
    f1 = jnp.min(jnp.where(sel == m1, si, float(GROUP_SIZE)), axis=1, keepdims=True)
    m2 = jnp.max(jnp.where(si == f1, neg, sel), axis=1, keepdims=True)
    grp = m1 + m2
    gi = _iota(grp.shape, 0).astype(F32)
    gsel = jnp.zeros(grp.shape, F32)
    for _ in range(TOPK_GROUPS):
        mx = jnp.max(grp, axis=0, keepdims=True)
        fi = jnp.min(jnp.where(grp == mx, gi, float(N_GROUPS)), axis=0, keepdims=True)
        hit = gi == fi
        gsel = jnp.where(hit, 1.0, gsel)
        grp = jnp.where(hit, neg, grp)
    cur = jnp.where(gsel > 0.0, sel, neg)
    ei = (_iota(cur.shape, 0) * GROUP_SIZE + _iota(cur.shape, 1)).astype(F32)
    chosen = jnp.zeros(cur.shape, F32)
    ids, wts = [], []
    for _ in range(TOP_K):
        mx = jnp.max(jnp.max(cur, axis=0, keepdims=True), axis=1, keepdims=True)
        fi = jnp.min(jnp.min(jnp.where(cur == mx, ei, float(N_EXPERTS)), axis=0, keepdims=True),
                     axis=1, keepdims=True)
        hit = ei == fi
        chosen = jnp.where(hit, 1.0, chosen)
        cur = jnp.where(hit, neg, cur)
        ids.append(fi.reshape(1, tm))
        wts.append(jnp.sum(jnp.sum(jnp.where(hit, sc3, 0.0), axis=0, keepdims=True), axis=1, keepdims=True)
                   .reshape(1, tm))
    w = jnp.concatenate(wts, axis=0)
    w = w / jnp.sum(w, axis=0, keepdims=True) * ROUTE_SCALE
    return chosen.reshape(N_EXPERTS, tm), jnp.concatenate(ids, axis=0), w


def _pack_halves(x):
    half = x.shape[1] // 2
    bits = lax.bitcast_convert_type(x.astype(BF16).astype(F32), jnp.int32)
    lo = lax.shift_right_logical(bits[:, :half], jnp.int32(16))
    return jnp.bitwise_or(lo, jnp.bitwise_and(bits[:, half:], jnp.int32(-65536)))


def _unpack_halves(word):
    lo = lax.bitcast_convert_type(lax.shift_left(word, jnp.int32(16)), F32)
    hi = lax.bitcast_convert_type(jnp.bitwise_and(word, jnp.int32(-65536)), F32)
    return lo, hi


def _post_kernel(x_ref, om_ref, og_ref, or_ref, mod_ref, wo_ref, n2_ref, rt_ref, rb_ref, tri_ref,
                 x1_ref, h2_ref, eid_ref, rank_ref, ew_ref, cnt_ref, carry_s):
    @pl.when(pl.program_id(0) == 0)
    def _():
        carry_s[...] = jnp.zeros(carry_s.shape, F32)

    m = mod_ref[0]
    g1 = m[:, 2 * D_MODEL:3 * D_MODEL]
    sh2 = m[:, 3 * D_MODEL:4 * D_MODEL]
    sc2 = m[:, 4 * D_MODEL:5 * D_MODEL]
    w_mla = MLA_HEADS * MLA_V
    mix = (_mm(om_ref[...], wo_ref[0:w_mla, :]) + _mm(og_ref[...], wo_ref[w_mla:w_mla + MIX_W, :])
           + _mm(or_ref[...], wo_ref[w_mla + MIX_W:, :]))
    x1 = x_ref[...] + g1 * mix
    x1_ref[...] = x1
    h2 = _rms(x1, n2_ref[...]) * (1.0 + sc2) + sh2
    h2_ref[...] = _pack_halves(h2)
    r_hi, r_lo = _split2(rt_ref[...])
    h_hi, h_lo = _split2(h2)
    logits_t = _mm_nt(r_hi, h_hi) + _mm_nt(r_hi, h_lo) + _mm_nt(r_lo, h_hi)
    chosen, ids, w = _route(logits_t, rb_ref[...])
    tm = chosen.shape[1]
    rank_et = (carry_s[:, 0:1] + _mm(chosen, tri_ref[...])).reshape(N_GROUPS, GROUP_SIZE, tm)
    ei = (_iota(rank_et.shape, 0) * GROUP_SIZE + _iota(rank_et.shape, 1)).astype(F32)
    ranks = []
    for k in range(TOP_K):
        pick = jnp.where(ei == ids[k:k + 1].reshape(1, 1, tm), rank_et, 0.0)
        ranks.append(jnp.sum(jnp.sum(pick, axis=0, keepdims=True), axis=1, keepdims=True).reshape(1, tm))
    eid_ref[...] = ids.astype(jnp.int32)
    rank_ref[...] = jnp.concatenate(ranks, axis=0).astype(jnp.int32)
    ew_ref[...] = jnp.concatenate([w, jnp.zeros((LANES - TOP_K, tm), F32)], axis=0).T
    total = carry_s[...] + jnp.sum(chosen, axis=1, keepdims=True)
    carry_s[...] = total
    cnt_ref[...] = total


def _post(x2d, om, og, orw, mods, lw, layer, consts, tm, seq_len, mod_base):
    n = x2d.shape[0]
    tiles_per_seq = seq_len // tm if mod_base else 1

    def mod_idx(i):
        return (layer, mod_base + i // tiles_per_seq if mod_base else 0, 0, 0)

    row = lambda w: pl.BlockSpec((tm, w), lambda i: (i, 0))
    col = lambda h: pl.BlockSpec((h, tm), lambda i: (0, i))
    full = lambda a: _layer_spec(a, layer)
    tail = [lw["w_out"], lw["norm2"], lw["router_t"], lw["router_b"]]
    tri = consts["tri_tokens"]
    return pl.pallas_call(
        _post_kernel,
        grid=(n // tm,),
        in_specs=[row(D_MODEL), row(MLA_HEADS * MLA_V), row(MIX_W), row(MIX_W),
                  pl.BlockSpec((None, 1, 1, 6 * D_MODEL), mod_idx)] + [full(a) for a in tail] + [_const_spec(tri)],
        out_specs=[row(D_MODEL), row(D_MODEL // 2), col(TOP_K), col(TOP_K), row(LANES),
                   pl.BlockSpec((N_EXPERTS, LANES), lambda i: (0, 0))],
        out_shape=[
            jax.ShapeDtypeStruct((n, D_MODEL), F32),
            jax.ShapeDtypeStruct((n, D_MODEL // 2), jnp.int32),
            jax.ShapeDtypeStruct((TOP_K, n), jnp.int32),
            jax.ShapeDtypeStruct((TOP_K, n), jnp.int32),
            jax.ShapeDtypeStruct((n, LANES), F32),
            jax.ShapeDtypeStruct((N_EXPERTS, LANES), F32),
        ],
        scratch_shapes=[pltpu.VMEM((N_EXPERTS, LANES), F32)],
        compiler_params=pltpu.CompilerParams(dimension_semantics=("arbitrary",),
                                             vmem_limit_bytes=VMEM_LIMIT),
        name="post",
    )(x2d, om, og, orw, mods, *tail, tri)


MOE_ROWS = 256
SC_ROWS = 128
SC_SUBCORES = 32


def _swiglu_act(gu):
    return _silu(gu[:, :D_EXPERT]) * gu[:, D_EXPERT:]


def _dispatch_plan(eid, rank, counts, n):
    n_blocks = n * TOP_K // MOE_ROWS + N_EXPERTS
    cnt = counts[:, 0].astype(jnp.int32)
    blocks = (cnt + MOE_ROWS - 1) // MOE_ROWS
    block_end = jnp.cumsum(blocks)
    offset = (block_end - blocks) * MOE_ROWS
    dest = jnp.take(offset, eid) + rank
    block_expert = jnp.minimum(jnp.searchsorted(block_end, jnp.arange(n_blocks, dtype=jnp.int32), side="right"),
                               N_EXPERTS - 1).astype(jnp.int32)
    return dest, block_expert, block_end[-1:].astype(jnp.int32), n_blocks


def _sc_mesh():
    return plsc.VectorSubcoreMesh(core_axis_name="core", subcore_axis_name="subcore")


def _sc_dispatch(x, dest, n_rows):
    n, w = x.shape
    n_chunks = n // SC_ROWS

    @functools.partial(pl.kernel, out_type=jax.ShapeDtypeStruct((n_rows, w), x.dtype), mesh=_sc_mesh(),
                       scratch_types=[pltpu.VMEM((SC_ROWS, w), x.dtype), pltpu.VMEM((TOP_K, SC_ROWS), jnp.int32)])
    def kern(x_hbm, d_hbm, o_hbm, xv, dv):
        sid = lax.axis_index("core") * (SC_SUBCORES // 2) + lax.axis_index("subcore")

        @pl.loop(sid, n_chunks, step=SC_SUBCORES)
        def _(c):
            r0 = pl.multiple_of(c * SC_ROWS, SC_ROWS)
            pltpu.sync_copy(x_hbm.at[pl.ds(r0, SC_ROWS)], xv)
            pltpu.sync_copy(d_hbm.at[:, pl.ds(r0, SC_ROWS)], dv)
            for k in range(TOP_K):
                pltpu.sync_copy(xv, o_hbm.at[dv.at[k]])

    return kern(x, dest)


def _sc_gather(y, idx):
    w = y.shape[1]
    n_chunks = idx.shape[0]

    @functools.partial(pl.kernel, out_type=jax.ShapeDtypeStruct((n_chunks * SC_ROWS, w), y.dtype), mesh=_sc_mesh(),
                       scratch_types=[pltpu.VMEM((SC_ROWS, w), y.dtype), pltpu.VMEM((1, SC_ROWS), jnp.int32)])
    def kern(y_hbm, i_hbm, o_hbm, ov, iv):
        sid = lax.axis_index("core") * (SC_SUBCORES // 2) + lax.axis_index("subcore")

        @pl.loop(sid, n_chunks, step=SC_SUBCORES)
        def _(c):
            pltpu.sync_copy(i_hbm.at[pl.ds(c, 1)], iv)
            pltpu.sync_copy(y_hbm.at[iv.at[0]], ov)
            pltpu.sync_copy(ov, o_hbm.at[pl.ds(pl.multiple_of(c * SC_ROWS, SC_ROWS), SC_ROWS)])

    return kern(y, idx)


def _moe_rows_kernel(be_ref, nu_ref, x_ref, wgu_ref, wdn_ref, y_ref):
    @pl.when(pl.program_id(0) < nu_ref[0])
    def _():
        half = D_MODEL // 2
        lo, hi = _unpack_halves(x_ref[...])
        gu = _mm(lo, wgu_ref[0:half, :]) + _mm(hi, wgu_ref[half:, :])
        y_ref[...] = _pack_halves(_mm(_swiglu_act(gu), wdn_ref[...]))


def _moe_rows(xs, block_expert, n_used, lw, layer, n_blocks):
    half = D_MODEL // 2
    last = lambda b, be, nu: jnp.minimum(b, nu[0] - 1)
    return pl.pallas_call(
        _moe_rows_kernel,
        grid_spec=pltpu.PrefetchScalarGridSpec(
            num_scalar_prefetch=2,
            grid=(n_blocks,),
            in_specs=[pl.BlockSpec((MOE_ROWS, half), lambda b, be, nu: (last(b, be, nu), 0)),
                      pl.BlockSpec((None, None, D_MODEL, 2 * D_EXPERT),
                                   lambda b, be, nu: (layer, be[last(b, be, nu)], 0, 0)),
                      pl.BlockSpec((None, None, D_EXPERT, D_MODEL),
                                   lambda b, be, nu: (layer, be[last(b, be, nu)], 0, 0))],
            out_specs=pl.BlockSpec((MOE_ROWS, half), lambda b, be, nu: (last(b, be, nu), 0)),
        ),
        out_shape=jax.ShapeDtypeStruct(xs.shape, jnp.int32),
        compiler_params=pltpu.CompilerParams(dimension_semantics=("arbitrary",),
                                             vmem_limit_bytes=VMEM_LIMIT),
        name="moe_rows",
    )(block_expert, n_used, xs, lw["moe_w_gu"], lw["moe_w_down"])


def _moe_combine_kernel(yg_ref, ew_ref, h_ref, x1_ref, mod_ref, sgu_ref, sdn_ref, nf_ref, o_ref, *, final):
    half = D_MODEL // 2
    lo, hi = _unpack_halves(h_ref[...])
    gu = _mm(lo, sgu_ref[0:half, :]) + _mm(hi, sgu_ref[half:, :])
    acc = _mm(_swiglu_act(gu), sdn_ref[...])
    ew = ew_ref[...]
    acc_lo = acc[:, :half]
    acc_hi = acc[:, half:]
    for k in range(TOP_K):
        lo, hi = _unpack_halves(yg_ref[k])
        wk = ew[:, k:k + 1]
        acc_lo = acc_lo + wk * lo
        acc_hi = acc_hi + wk * hi
    g2 = mod_ref[0][:, 5 * D_MODEL:]
    x2 = x1_ref[...] + g2 * jnp.concatenate([acc_lo, acc_hi], axis=1)
    if final:
        x2 = _rms(x2, nf_ref[...])
    o_ref[...] = x2


def _moe_combine(yg, ew, h2p, x1, mods, lw, layer, norm_f, tm, seq_len, mod_base, final):
    n = x1.shape[0]
    half = D_MODEL // 2
    tiles_per_seq = seq_len // tm if mod_base else 1

    def mod_idx(i):
        return (layer, mod_base + i // tiles_per_seq if mod_base else 0, 0, 0)

    row = lambda w: pl.BlockSpec((tm, w), lambda i: (i, 0))
    return pl.pallas_call(
        functools.partial(_moe_combine_kernel, final=final),
        grid=(n // tm,),
        in_specs=[pl.BlockSpec((TOP_K, tm, half), lambda i: (0, i, 0)), row(LANES), row(half), row(D_MODEL),
                  pl.BlockSpec((None, 1, 1, 6 * D_MODEL), mod_idx),
                  _layer_spec(lw["shared_w_gu"], layer), _layer_spec(lw["shared_w_down"], layer),
                  _const_spec(norm_f)],
        out_specs=row(D_MODEL),
        out_shape=jax.ShapeDtypeStruct((n, D_MODEL), F32),
        compiler_params=pltpu.CompilerParams(dimension_semantics=("arbitrary",),
                                             vmem_limit_bytes=VMEM_LIMIT),
        name="moe_combine_final" if final else "moe_combine",
    )(yg, ew, h2p, x1, mods, lw["shared_w_gu"], lw["shared_w_down"], norm_f)


def _moe(h2p, x1, eid, rank, ew, counts, mods, lw, layer, norm_f, tm, seq_len, mod_base, final):
    n = x1.shape[0]
    dest, block_expert, n_used, n_blocks = _dispatch_plan(eid, rank, counts, n)
    xs = _sc_dispatch(h2p, dest, n_blocks * MOE_ROWS)
    y = _moe_rows(xs, block_expert, n_used, lw, layer, n_blocks)
    yg = _sc_gather(y, dest.reshape(n * TOP_K // SC_ROWS, SC_ROWS)).reshape(TOP_K, n, D_MODEL // 2)
    return _moe_combine(yg, ew, h2p, x1, mods, lw, layer, norm_f, tm, seq_len, mod_base, final)


def _constants():
    idx = np.arange(MIX_W)
    same_head = (idx[:, None] // HEAD_DIM) == (idx[None, :] // HEAD_DIM)
    maskbd = jnp.asarray(same_head, BF16)
    eexp = np.zeros((LANES, 4 * MIX_W), np.float32)
    for blk in range(4):
        kind, d = divmod(blk, 2)
        for h in range(HEADS):
            src = kind * 2 * HEADS + d * HEADS + h
            eexp[src, blk * MIX_W + h * HEAD_DIM: blk * MIX_W + (h + 1) * HEAD_DIM] = 1.0
    tri = np.triu(np.ones((POST_TM, POST_TM), np.float32), 1)
    return {"maskbd": maskbd, "ones_bd": maskbd, "eexp": jnp.asarray(eexp, BF16), "tri_tokens": jnp.asarray(tri, BF16)}


def _rope_tables(n):
    rows = n // GRID_W
    row = jnp.repeat(jnp.arange(rows, dtype=F32), GRID_W)
    col = jnp.tile(jnp.arange(GRID_W, dtype=F32), rows)
    axis_dim = MLA_ROPE // 2
    inv = jnp.power(ROPE_BASE, -jnp.arange(0, axis_dim, 2, dtype=F32) / axis_dim)
    ang_r = row[:, None] * inv
    ang_c = col[:, None] * inv
    cr, sr, cc, sc = jnp.cos(ang_r), jnp.sin(ang_r), jnp.cos(ang_c), jnp.sin(ang_c)
    zeros = jnp.zeros((n, LANES - MLA_ROPE), F32)
    cos_t = jnp.concatenate([cr, cr, cc, cc, zeros], axis=1)
    sin_t = jnp.concatenate([-sr, sr, -sc, sc, zeros], axis=1)
    return cos_t, sin_t


def _stacked_weights(p):
    w_uq = p["mla_w_uq"].reshape(DEPTH, Q_LORA, MLA_HEADS, MLA_NOPE + MLA_ROPE)
    zq = jnp.zeros((DEPTH, Q_LORA, MLA_HEADS, QH_W - MLA_NOPE - MLA_ROPE), F32)
    w_uq_a = jnp.concatenate([w_uq, zq], axis=-1).reshape(DEPTH, Q_LORA, MLA_HEADS * QH_W).astype(BF16)
    w_uq_sw = jnp.concatenate([jnp.zeros((DEPTH, Q_LORA, MLA_HEADS, MLA_NOPE), F32),
                               w_uq[..., MLA_NOPE + _ROPE_SWAP], zq], axis=-1)
    w_uq_sw = w_uq_sw.reshape(DEPTH, Q_LORA, MLA_HEADS * QH_W).astype(BF16)

    def per_direction(w):
        half = jnp.zeros((DEPTH, 64, MIX_W), F32)
        return jnp.stack([jnp.concatenate([w[:, 0], half], axis=1),
                          jnp.concatenate([half, w[:, 1]], axis=1)], axis=1).astype(BF16)

    row = lambda v: v.reshape(DEPTH, 1, -1)
    pad_row = lambda v: jnp.pad(row(v), ((0, 0), (0, 0), (0, LANES - 2 * HEADS)))
    return {
        "norm1": row(p["norm1"]),
        "w_in": p["w_in"],
        "q_norm": row(p["mla_q_norm"]),
        "w_uq": w_uq_a, "w_uq_sw": w_uq_sw,
        "kv_norm": row(p["mla_kv_norm"]),
        "w_ukv": p["mla_w_ukv"].astype(BF16),
        "gdn_conv": p["gdn_conv"],
        "gdn_alog": pad_row(p["gdn_a_log"]),
        "gdn_dtb": pad_row(p["gdn_dt_bias"]),
        "gdn_norm": jnp.tile(row(p["gdn_norm"]), (1, 1, HEADS)),
        "rwkv_mu_prev": row(p["rwkv_mu_prev"]),
        "rwkv_mu_next": row(p["rwkv_mu_next"]),
        "rwkv_w0": p["rwkv_w0"],
        "rwkv_w2": per_direction(p["rwkv_w2"]),
        "rwkv_a0": p["rwkv_a0"],
        "rwkv_a2": per_direction(p["rwkv_a2"]),
        "rwkv_g2": p["rwkv_g2"].astype(BF16),
        "rwkv_k_k": row(p["rwkv_k_k"]),
        "rwkv_k_a": row(p["rwkv_k_a"]),
        "rwkv_r_k": row(p["rwkv_r_k"]),
        "rwkv_gn_w": row(p["rwkv_gn_w"]),
        "rwkv_gn_b": row(p["rwkv_gn_b"]),
        "w_out": p["w_out"].astype(BF16),
        "norm2": row(p["norm2"]),
        "router_t": jnp.swapaxes(p["moe_router"], 1, 2),
        "router_b": p["moe_bias"].reshape(DEPTH, N_EXPERTS, 1),
        "moe_w_gu": p["moe_w_gu"],
        "moe_w_down": p["moe_w_down"],
        "shared_w_gu": p["shared_w_gu"].astype(BF16),
        "shared_w_down": p["shared_w_down"].astype(BF16),
    }


def _embed_block_diag(s):
    b = s.shape[0]
    eye = jnp.eye(HEADS, dtype=s.dtype)
    out = jnp.einsum("bdhkv,hg->bdhkgv", s, eye)
    return out.reshape(b, 2, MIX_W, MIX_W)


def _trunk(x2d, mods, lw, consts, norm_f, seq_len, mod_base, cache, tm, tq):
    outs = []
    for l in range(DEPTH):
        zm, zg, zab, zr = _inproj(x2d, mods, lw, l, tm, seq_len, mod_base)
        if cache is None:
            o_mla, ckv = _mla(zm, lw, l, seq_len, tq)
            o_gdn, s_gdn = _gdn(zg, zab, lw, l, consts, seq_len)
            o_rwkv, s_rwkv = _rwkv(zr, lw, l, consts, seq_len)
            s_rwkv = jnp.swapaxes(s_rwkv, -1, -2)
        else:
            cckv, ckpe, rc, rs, sg, sr = cache
            o_mla, ckv = _mla(zm, lw, l, seq_len, tq, (cckv, ckpe, rc, rs))
            o_gdn, s_gdn = _gdn(zg, zab, lw, l, consts, seq_len, sg)
            o_rwkv, s_rwkv = _rwkv(zr, lw, l, consts, seq_len, sr)
        x1, h2p, eid, rank, ew, counts = _post(x2d, o_mla, o_gdn, o_rwkv, mods, lw, l, consts, tm, seq_len, mod_base)
        x2d = _moe(h2p, x1, eid, rank, ew, counts, mods, lw, l, norm_f, tm, seq_len, mod_base,
                   final=(l == DEPTH - 1))
        kpe0 = Q_LORA + KV_LORA
        outs.append((ckv, zm[:, kpe0:kpe0 + MLA_ROPE], s_gdn, s_rwkv))
    return x2d, outs


def kernel(x_prompt, x_sample, cache_mla_ckv, cache_mla_kpe, state_gdn, state_rwkv, c, c_ctx, ada_w, ada_b, norm1, w_in, mla_q_norm, mla_w_uq, mla_kv_norm, mla_w_ukv, gdn_conv, gdn_a_log, gdn_dt_bias, gdn_norm, rwkv_mu_prev, rwkv_mu_next, rwkv_w0, rwkv_w2, rwkv_a0, rwkv_a2, rwkv_g2, rwkv_k_k, rwkv_k_a, rwkv_r_k, rwkv_gn_w, rwkv_gn_b, w_out, norm2, moe_router, moe_bias, moe_w_gu, moe_w_down, shared_w_gu, shared_w_down, norm_f):
    p = dict(norm1=norm1, w_in=w_in, mla_q_norm=mla_q_norm, mla_w_uq=mla_w_uq, mla_kv_norm=mla_kv_norm,
             mla_w_ukv=mla_w_ukv, gdn_conv=gdn_conv, gdn_a_log=gdn_a_log, gdn_dt_bias=gdn_dt_bias,
             gdn_norm=gdn_norm, rwkv_mu_prev=rwkv_mu_prev, rwkv_mu_next=rwkv_mu_next, rwkv_w0=rwkv_w0,
             rwkv_w2=rwkv_w2, rwkv_a0=rwkv_a0, rwkv_a2=rwkv_a2, rwkv_g2=rwkv_g2, rwkv_k_k=rwkv_k_k,
             rwkv_k_a=rwkv_k_a, rwkv_r_k=rwkv_r_k, rwkv_gn_w=rwkv_gn_w, rwkv_gn_b=rwkv_gn_b, w_out=w_out,
             norm2=norm2, moe_router=moe_router, moe_bias=moe_bias, moe_w_gu=moe_w_gu, moe_w_down=moe_w_down,
             shared_w_gu=shared_w_gu, shared_w_down=shared_w_down)
    weights = _stacked_weights(p)
    consts = _constants()
    nf = norm_f.reshape(1, D_MODEL)
    b_ctx, t_ctx, _ = x_prompt.shape
    b_lat, t_lat, _ = x_sample.shape

    cvec8 = jnp.concatenate([c_ctx[None, :], c, jnp.zeros((8 - 1 - b_lat, D_MODEL), F32)], axis=0)
    mods = _adaln(cvec8, ada_w, ada_b)
    mods = mods.reshape(DEPTH, 8, 1, 6 * D_MODEL)

    xp, ctx_outs = _trunk(x_prompt.reshape(b_ctx * t_ctx, D_MODEL), mods, weights, consts, nf,
                          t_ctx, 0, None, tm=POST_TM, tq=t_ctx)
    rc, rs = _rope_tables(t_lat)
    ckpe = jnp.pad(cache_mla_kpe, ((0, 0), (0, 0), (0, 0), (0, LANES - MLA_ROPE)))
    cache = (cache_mla_ckv, ckpe, rc, rs, _embed_block_diag_layers(state_gdn),
             _embed_block_diag_layers(jnp.swapaxes(state_rwkv, -1, -2)))
    xs, _ = _trunk(x_sample.reshape(b_lat * t_lat, D_MODEL), mods, weights, consts, nf,
                   t_lat, 1, cache, tm=POST_TM, tq=256)

    y_prompt = xp.reshape(b_ctx, t_ctx, D_MODEL)
    y_sample = xs.reshape(b_lat, t_lat, D_MODEL)
    new_ckv = jnp.stack([o[0].reshape(b_ctx, t_ctx, KV_LORA) for o in ctx_outs], axis=1)
    new_kpe = jnp.stack([o[1].reshape(b_ctx, t_ctx, MLA_ROPE) for o in ctx_outs], axis=1)
    new_gdn = jnp.stack([o[2] for o in ctx_outs], axis=1)
    new_rwkv = jnp.stack([o[3] for o in ctx_outs], axis=1)
    return (y_prompt, y_sample, new_ckv, new_kpe, new_gdn, new_rwkv)


def _embed_block_diag_layers(s):
    b = s.shape[0]
    return _embed_block_diag(s.reshape(b * DEPTH, 2, HEADS, HEAD_DIM, HEAD_DIM)).reshape(
        b, DEPTH, 2, MIX_W, MIX_W)
```

```python
import functools

import numpy as np
import jax
import jax.numpy as jnp
from jax import lax
from jax.experimental import pallas as pl
from jax.experimental.pallas import tpu as pltpu
from jax.experimental.pallas import tpu_sc as plsc

F32 = jnp.float32
BF16 = jnp.bfloat16

D_MODEL = 1024
BATCH = 32
SEQ = 256
DEPTH = 2
DEC_BATCH = 2
DEC_SEQ = 1024
PAST_LEN = 512
GRID_W = 64
NORM_EPS = 1e-6

MLA_HEADS = 4
MLA_NOPE = 128
MLA_ROPE = 64
MLA_V = 128
Q_LORA = 384
KV_LORA = 256
ROPE_BASE = 10000.0
MLA_SCALE = (MLA_NOPE + MLA_ROPE) ** -0.5

HEADS = 4
HEAD_DIM = 64
MIX_W = HEADS * HEAD_DIM
GDN_CONV_CH = 3 * MIX_W
CHUNK = 64
RWKV_GN_EPS = 64e-5

N_EXPERTS = 64
TOP_K = 8
N_GROUPS = 8
GROUP_SIZE = N_EXPERTS // N_GROUPS
TOPK_GROUPS = 4
D_EXPERT = 256
ROUTE_SCALE = 2.5

P_MLA = Q_LORA + KV_LORA + MLA_ROPE
P_GDN = GDN_CONV_CH + MIX_W + 4 * HEADS
P_RWKV = 3 * MIX_W + 128 + 128 + 128

LANES = 128
ZM_W = Q_LORA + KV_LORA + 2 * LANES
ZG_W = GDN_CONV_CH + MIX_W
ZR_W = P_RWKV
QH_W = 2 * LANES
VMEM_LIMIT = 56 * 1024 * 1024
POST_TM = 512

_ROPE_SWAP = np.concatenate([np.arange(16, 32), np.arange(0, 16), np.arange(48, 64), np.arange(32, 48)])


def _sigmoid(x):
    return 1.0 / (1.0 + jnp.exp(-x))


def _silu(x):
    return x * _sigmoid(x)


def _softplus(x):
    return jnp.maximum(x, 0.0) + jnp.log(1.0 + jnp.exp(-jnp.abs(x)))


def _rms(x, g, eps=NORM_EPS):
    return x * lax.rsqrt(jnp.mean(x * x, axis=-1, keepdims=True) + eps) * g


def _mm(a, b):
    return jnp.dot(a.astype(BF16), b.astype(BF16), preferred_element_type=F32)


def _mm_nt(a, b):
    return lax.dot_general(a.astype(BF16), b.astype(BF16), (((1,), (1,)), ((), ())),
                           preferred_element_type=F32)


def _mm_tn(a, b):
    return lax.dot_general(a.astype(BF16), b.astype(BF16), (((0,), (0,)), ((), ())),
                           preferred_element_type=F32)


def _split3(x):
    p1 = x.astype(BF16)
    r1 = x - p1.astype(F32)
    p2 = r1.astype(BF16)
    r2 = r1 - p2.astype(F32)
    return p1, p2, r2.astype(BF16)


def _mm_sel_l(sel, x):
    p1, p2, p3 = _split3(x)
    return _mm(sel, p1) + _mm(sel, p2) + _mm(sel, p3)


def _mm_sel_r(x, sel):
    p1, p2, p3 = _split3(x)
    return _mm(p1, sel) + _mm(p2, sel) + _mm(p3, sel)


def _iota(shape, dim):
    return lax.broadcasted_iota(jnp.int32, shape, dim)


def _layer_spec(a, layer, **kw):
    nd = a.ndim - 1
    return pl.BlockSpec((None,) + a.shape[1:], lambda *_: (layer,) + (0,) * nd, **kw)


def _const_spec(a, **kw):
    return pl.BlockSpec(a.shape, lambda *_: (0,) * a.ndim, **kw)


def _bd(x, maskbd):
    xb = x.astype(BF16)
    return jnp.concatenate([xb] * HEADS, axis=0) * maskbd


def _chunk_masks(rev):
    row = _iota((CHUNK, MIX_W), 0)
    col = jnp.bitwise_and(_iota((CHUNK, MIX_W), 1), HEAD_DIM - 1)
    r2 = _iota((CHUNK, CHUNK), 0)
    c2 = _iota((CHUNK, CHUNK), 1)
    if rev:
        inc, strict, tri = row <= col, row < col, r2 <= c2
    else:
        inc, strict, tri = row >= col, row > col, r2 >= c2
    eye = jnp.where(row == col, 1.0, 0.0).astype(F32)
    return inc, strict, jnp.where(tri, 1.0, 0.0).astype(BF16), eye


def _split2(x):
    hi = x.astype(BF16)
    return hi, (x - hi.astype(F32)).astype(BF16)


def _mm_bd3(x, p, maskbd):
    n = x.shape[0]
    xh, xl = _split2(x)
    ph, pl_ = _split2(p)
    r = jnp.dot(jnp.concatenate([xh, xl], axis=0), _bd(ph, maskbd), preferred_element_type=F32)
    return r[:n] + r[n:] + jnp.dot(xh, _bd(pl_, maskbd), preferred_element_type=F32)


def _neumann_inverse(a_list, eye_list, maskbd):
    bs = [-a for a in a_list]
    ms = [eye + b for eye, b in zip(eye_list, bs)]
    ps = [_mm_bd3(b, b, maskbd) for b in bs]
    for _ in range(4):
        boths = [_mm_bd3(jnp.concatenate([m, p], axis=0), p, maskbd) for m, p in zip(ms, ps)]
        ms = [m + both[:CHUNK] for m, both in zip(ms, boths)]
        ps = [both[CHUNK:] for both in boths]
    return [m + _mm_bd3(m, p, maskbd) for m, p in zip(ms, ps)]


def _adaln_kernel(c_ref, w_ref, b_ref, o_ref):
    cv = c_ref[...]
    o_ref[0] = _mm(_silu(cv), w_ref[0]) + b_ref[0]


def _adaln(cvec8, ada_w, ada_b):
    tn = 768
    n_out = 6 * D_MODEL
    return pl.pallas_call(
        _adaln_kernel,
        grid=(DEPTH, n_out // tn),
        in_specs=[
            pl.BlockSpec((8, D_MODEL), lambda l, j: (0, 0)),
            pl.BlockSpec((1, D_MODEL, tn), lambda l, j: (l, 0, j)),
            pl.BlockSpec((1, 1, tn), lambda l, j: (l, 0, j)),
        ],
        out_specs=pl.BlockSpec((1, 8, tn), lambda l, j: (l, 0, j)),
        out_shape=jax.ShapeDtypeStruct((DEPTH, 8, n_out), F32),
        compiler_params=pltpu.CompilerParams(dimension_semantics=("arbitrary", "arbitrary"),
                                             vmem_limit_bytes=VMEM_LIMIT),
        name="adaln",
    )(cvec8, ada_w, ada_b.reshape(DEPTH, 1, n_out))


_KPE0 = Q_LORA + KV_LORA
_W_IN_MOVES = (
    [(0, 0, P_MLA)]
    + [(P_MLA + LANES - MLA_ROPE + 16 * j, _KPE0 + 16 * int(_ROPE_SWAP[16 * j] // 16), 16) for j in range(4)]
    + [(ZM_W, P_MLA, ZG_W), (ZM_W + ZG_W, P_MLA + ZG_W, 4 * HEADS), (ZM_W + ZG_W + LANES, P_MLA + P_GDN, P_RWKV)]
)
W_IN_PAD = ZM_W + ZG_W + LANES + ZR_W


def _inproj_kernel(x_ref, mod_ref, n1_ref, w_ref, zm_ref, zg_ref, zab_ref, zr_ref, w_s):
    @pl.when(pl.program_id(0) == 0)
    def _():
        w_s[...] = jnp.zeros(w_s.shape, BF16)
        for dst, src, width in _W_IN_MOVES:
            w_s[:, dst:dst + width] = w_ref[:, src:src + width].astype(BF16)

    m = mod_ref[0]
    sh = m[:, 0:D_MODEL]
    sc = m[:, D_MODEL:2 * D_MODEL]
    h = _rms(x_ref[...], n1_ref[...]) * (1.0 + sc) + sh
    z = _mm(h, w_s[...])
    o1 = ZM_W
    o2 = o1 + ZG_W
    o3 = o2 + LANES
    zm_ref[...] = z[:, :o1]
    zg_ref[...] = z[:, o1:o2]
    zab_ref[...] = z[:, o2:o3]
    zr_ref[...] = z[:, o3:]


def _inproj(x2d, mods, lw, layer, tm, seq_len, mod_base):
    n = x2d.shape[0]
    tiles_per_seq = seq_len // tm if mod_base else 1

    def mod_idx(i):
        return (layer, mod_base + i // tiles_per_seq if mod_base else 0, 0, 0)

    return pl.pallas_call(
        _inproj_kernel,
        grid=(n // tm,),
        in_specs=[
            pl.BlockSpec((tm, D_MODEL), lambda i: (i, 0)),
            pl.BlockSpec((None, 1, 1, 6 * D_MODEL), mod_idx),
            _layer_spec(lw["norm1"], layer),
            _layer_spec(lw["w_in"], layer, pipeline_mode=pl.Buffered(1)),
        ],
        out_specs=[
            pl.BlockSpec((tm, ZM_W), lambda i: (i, 0)),
            pl.BlockSpec((tm, ZG_W), lambda i: (i, 0)),
            pl.BlockSpec((tm, LANES), lambda i: (i, 0)),
            pl.BlockSpec((tm, ZR_W), lambda i: (i, 0)),
        ],
        out_shape=[
            jax.ShapeDtypeStruct((n, ZM_W), F32),
            jax.ShapeDtypeStruct((n, ZG_W), F32),
            jax.ShapeDtypeStruct((n, LANES), F32),
            jax.ShapeDtypeStruct((n, ZR_W), F32),
        ],
        scratch_shapes=[pltpu.VMEM((D_MODEL, W_IN_PAD), BF16)],
        compiler_params=pltpu.CompilerParams(dimension_semantics=("arbitrary",),
                                             vmem_limit_bytes=VMEM_LIMIT),
        name="inproj",
    )(x2d, mods, lw["norm1"], lw["w_in"])


def _mla_kernel(*refs, seq_len, tq, past, cached):
    if cached:
        (zm_ref, cckv_ref, ckpe_ref, rc_ref, rs_ref, qn_ref, wuq_ref, wuqs_ref, kvn_ref, wukv_ref,
         o_ref, ckv_ref, k_s, v_s) = refs
    else:
        (zm_ref, qn_ref, wuq_ref, kvn_ref, wukv_ref, o_ref, ckv_ref, k_s, v_s) = refs
    qi = pl.program_id(1)
    o_kpe = Q_LORA + KV_LORA

    @pl.when(qi == 0)
    def _():
        zm = zm_ref[...]
        ckv = _rms(zm[:, Q_LORA:o_kpe], kvn_ref[...])
        ckv_ref[...] = ckv
        kpe = zm[:, o_kpe:o_kpe + LANES]
        if cached:
            kpe = kpe * rc_ref[...] + zm[:, o_kpe + LANES:o_kpe + 2 * LANES] * rs_ref[...]
            kvc = _mm(cckv_ref[0], wukv_ref[...])
            kpc = ckpe_ref[0].astype(BF16)
        kv = _mm(ckv, wukv_ref[...])
        kpe = kpe.astype(BF16)
        for h in range(MLA_HEADS):
            c0 = h * QH_W
            if cached:
                k_s[0:past, c0:c0 + LANES] = kvc[:, c0:c0 + LANES].astype(BF16)
                k_s[0:past, c0 + LANES:c0 + QH_W] = kpc
                v_s[0:past, h * MLA_V:(h + 1) * MLA_V] = kvc[:, c0 + LANES:c0 + QH_W].astype(BF16)
            k_s[past:past + seq_len, c0:c0 + LANES] = kv[:, c0:c0 + LANES].astype(BF16)
            k_s[past:past + seq_len, c0 + LANES:c0 + QH_W] = kpe
            v_s[past:past + seq_len, h * MLA_V:(h + 1) * MLA_V] = kv[:, c0 + LANES:c0 + QH_W].astype(BF16)

    r0 = pl.multiple_of(qi * tq, tq)
    zq = zm_ref[pl.ds(r0, tq), :]
    cq = _rms(zq[:, :Q_LORA], qn_ref[...])
    q = _mm(cq, wuq_ref[...])
    if cached:
        qs = _mm(cq, wuqs_ref[...])
        ones = jnp.ones((tq, LANES), F32)
        zeros = jnp.zeros((tq, LANES), F32)
        qc = jnp.concatenate([ones, rc_ref[pl.ds(r0, tq), :]], axis=1)
        qsn = jnp.concatenate([zeros, rs_ref[pl.ds(r0, tq), :]], axis=1)
    for h in range(MLA_HEADS):
        c0 = h * QH_W
        qh = q[:, c0:c0 + QH_W]
        if cached:
            qh = qh * qc + qs[:, c0:c0 + QH_W] * qsn
        s = _mm_nt(qh, k_s[:, c0:c0 + QH_W]) * MLA_SCALE
        e = jnp.exp(s - jnp.max(s, axis=-1, keepdims=True))
        den = jnp.sum(e, axis=-1, keepdims=True)
        o_ref[:, h * MLA_V:(h + 1) * MLA_V] = _mm(e, v_s[:, h * MLA_V:(h + 1) * MLA_V]) / den


def _mla(zm, lw, layer, seq_len, tq, cache=None):
    n = zm.shape[0]
    nb = n // seq_len
    cached = cache is not None
    past = PAST_LEN if cached else 0
    tk = past + seq_len
    seq_spec = lambda w: pl.BlockSpec((seq_len, w), lambda b, q: (b, 0))
    lay = lambda a: _layer_spec(a, layer)
    if cached:
        cckv, ckpe, rc, rs = cache
        args = [zm, cckv, ckpe, rc, rs, lw["q_norm"], lw["w_uq"], lw["w_uq_sw"], lw["kv_norm"], lw["w_ukv"]]
        in_specs = [seq_spec(ZM_W),
                    pl.BlockSpec((1, None, past, KV_LORA), lambda b, q: (b, layer, 0, 0)),
                    pl.BlockSpec((1, None, past, LANES), lambda b, q: (b, layer, 0, 0)),
                    _const_spec(rc), _const_spec(rs)] + [lay(a) for a in args[5:]]
    else:
        args = [zm, lw["q_norm"], lw["w_uq"], lw["kv_norm"], lw["w_ukv"]]
        in_specs = [seq_spec(ZM_W)] + [lay(a) for a in args[1:]]
    return pl.pallas_call(
        functools.partial(_mla_kernel, seq_len=seq_len, tq=tq, past=past, cached=cached),
        grid=(nb, seq_len // tq),
        in_specs=in_specs,
        out_specs=[
            pl.BlockSpec((tq, MLA_HEADS * MLA_V), lambda b, q: (b * (seq_len // tq) + q, 0)),
            pl.BlockSpec((seq_len, KV_LORA), lambda b, q: (b, 0)),
        ],
        out_shape=[
            jax.ShapeDtypeStruct((n, MLA_HEADS * MLA_V), F32),
            jax.ShapeDtypeStruct((n, KV_LORA), F32),
        ],
        scratch_shapes=[
            pltpu.VMEM((tk, MLA_HEADS * QH_W), BF16),
            pltpu.VMEM((tk, MLA_HEADS * MLA_V), BF16),
        ],
        compiler_params=pltpu.CompilerParams(dimension_semantics=("arbitrary", "arbitrary"),
                                             vmem_limit_bytes=VMEM_LIMIT),
        name="mla_lat" if cached else "mla_ctx",
    )(*args)


CHUNK_GROUP = 4
GDN_PRE_W = 5 * MIX_W


def _for_chunk_groups(n_chunks, fn):
    if n_chunks == CHUNK_GROUP:
        fn(0)
    else:
        def body(gi, carry):
            fn(gi * CHUNK_GROUP)
            return carry
        lax.fori_loop(0, n_chunks // CHUNK_GROUP, body, 0)


def _gdn_prepare(items, maskbd):
    n = range(len(items))
    qs, ks, vs, gs, betas, masks, revs = zip(*items)
    gcs = [_mm_sel_l(masks[i][2], gs[i]) for i in n]
    decays = []
    for i in n:
        inc, eye = masks[i][0], masks[i][3]
        gc_row = jnp.sum(eye * gcs[i], axis=0, keepdims=True)
        decays.append(jnp.where(inc, jnp.exp(jnp.where(inc, gcs[i] - gc_row, 0.0)), 0.0))
    kbs = [ks[i] * betas[i] for i in n]
    aqs = [_mm_nt(jnp.concatenate([kbs[i], qs[i]], axis=0), _bd(ks[i], maskbd)) for i in n]
    a_mats = [jnp.where(masks[i][1], aqs[i][:CHUNK] * decays[i], 0.0) for i in n]
    t_invs = _neumann_inverse(a_mats, [m[3] for m in masks], maskbd)
    egcs = [jnp.exp(gc) for gc in gcs]
    uws = [_mm(t_invs[i], jnp.concatenate([_bd(vs[i] * betas[i], maskbd), _bd(kbs[i] * egcs[i], maskbd)], axis=1))
           for i in n]
    out = []
    for i in n:
        g_last = gcs[i][0:1] if revs[i] else gcs[i][CHUNK - 1:CHUNK]
        pre = jnp.concatenate([uws[i], qs[i] * egcs[i], aqs[i][CHUNK:] * decays[i],
                               ks[i] * jnp.exp(g_last - gcs[i])], axis=1)
        out.append((pre, jnp.broadcast_to(jnp.exp(g_last), (8, MIX_W))))
    return out


def _gdn_step(pre, egl, s_bd, maskbd):
    w = MIX_W
    u, wq_in, qk, kdec = pre[:, :w], pre[:, w:3 * w], pre[:, 3 * w:4 * w], pre[:, 4 * w:]
    wq = _mm(jnp.concatenate([wq_in[:, :w], wq_in[:, w:]], axis=0), s_bd)
    v_new = u - wq[:CHUNK]
    o = wq[CHUNK:] + _mm(qk, _bd(v_new, maskbd))
    s_new = s_bd * egl + _mm_tn(kdec, v_new) * maskbd.astype(F32)
    return o, s_new


def _gdn_kernel(*refs, seq_len, cached):
    if cached:
        (zg_ref, zab_ref, s0_ref, conv_ref, alog_ref, dtb_ref, gn_ref, eexp_ref, ones_ref, maskbd_ref,
         o_ref, sout_ref, q_s, k_s, v_s, ge_s, pre_s, gl_s, oacc_s, st_s) = refs
    else:
        (zg_ref, zab_ref, conv_ref, alog_ref, dtb_ref, gn_ref, eexp_ref, ones_ref, maskbd_ref,
         o_ref, sout_ref, q_s, k_s, v_s, ge_s, pre_s, gl_s, oacc_s, st_s) = refs
    t = seq_len
    z = zg_ref[:, :GDN_CONV_CH]
    rowi = _iota((t, 1), 0)
    zp = jnp.where(rowi == 0, 0.0, pltpu.roll(z, 1, 0))
    zn = jnp.where(rowi == t - 1, 0.0, pltpu.roll(z, t - 1, 0))
    cw = conv_ref[...]
    qkv = _silu(zp * cw[0:1] + z * cw[1:2] + zn * cw[2:3])
    ones_bd = ones_ref[...]
    q = qkv[:, :MIX_W]
    k = qkv[:, MIX_W:2 * MIX_W]
    q_s[...] = q * lax.rsqrt(_mm_sel_r(q * q, ones_bd) + 1e-6) * (HEAD_DIM ** -0.5)
    k_s[...] = k * lax.rsqrt(_mm_sel_r(k * k, ones_bd) + 1e-6)
    v_s[...] = qkv[:, 2 * MIX_W:]
    ab = zab_ref[...]
    lane = _iota((t, LANES), 1)
    gb = jnp.where(lane < 2 * HEADS, -jnp.exp(alog_ref[...]) * _softplus(ab + dtb_ref[...]), _sigmoid(ab))
    ge_s[...] = _mm_sel_r(gb, eexp_ref[...])
    oacc_s[...] = jnp.zeros((t, MIX_W), F32)
    if cached:
        st_s[...] = s0_ref[0]
    else:
        st_s[...] = jnp.zeros((2, MIX_W, MIX_W), F32)
    maskbd = maskbd_ref[...]
    masks = (_chunk_masks(False), _chunk_masks(True))
    n_chunks = t // CHUNK

    def prepare_group(c0):
        where, items = [], []
        for j in range(CHUNK_GROUP):
            c = c0 + j
            rows = pl.ds(pl.multiple_of(c * CHUNK, CHUNK), CHUNK)
            for d in range(2):
                where.append((d, c, rows))
                items.append((q_s[rows, :], k_s[rows, :], v_s[rows, :], ge_s[rows, d * MIX_W:(d + 1) * MIX_W],
                              ge_s[rows, (2 + d) * MIX_W:(3 + d) * MIX_W], masks[d], d == 1))
        for (d, c, rows), (pre, egl) in zip(where, _gdn_prepare(items, maskbd)):
            pre_s[d, rows, :] = pre
            gl_s[d, pl.ds(pl.multiple_of(c * 8, 8), 8), :] = egl

    _for_chunk_groups(n_chunks, prepare_group)

    def body(i, carry):
        for d in range(2):
            c = i if d == 0 else n_chunks - 1 - i
            rows = pl.ds(pl.multiple_of(c * CHUNK, CHUNK), CHUNK)
            egl = gl_s[d, pl.ds(pl.multiple_of(c * 8, 8), 8), :][0:1]
            o, s_new = _gdn_step(pre_s[d, rows, :], egl, st_s[d], maskbd)
            oacc_s[rows, :] = oacc_s[rows, :] + o
            st_s[d] = s_new
        return carry

    lax.fori_loop(0, n_chunks, body, 0)
    o = oacc_s[...]
    ms = _mm_sel_r(o * o, ones_bd) * (1.0 / HEAD_DIM)
    gate = zg_ref[:, GDN_CONV_CH:]
    o_ref[...] = o * lax.rsqrt(ms + NORM_EPS) * gn_ref[...] * _silu(gate)
    for d in range(2):
        for h in range(HEADS):
            sl = slice(h * HEAD_DIM, (h + 1) * HEAD_DIM)
            sout_ref[0, d, h] = st_s[d, sl, sl]


def _gdn(zg, zab, lw, layer, consts, seq_len, s0_bd=None):
    n = zg.shape[0]
    nb = n // seq_len
    cached = s0_bd is not None
    args = [zg, zab]
    in_specs = [pl.BlockSpec((seq_len, ZG_W), lambda b: (b, 0)),
                pl.BlockSpec((seq_len, LANES), lambda b: (b, 0))]
    if cached:
        args.append(s0_bd)
        in_specs.append(pl.BlockSpec((1, None, 2, MIX_W, MIX_W), lambda b: (b, layer, 0, 0, 0)))
    layered = [lw["gdn_conv"], lw["gdn_alog"], lw["gdn_dtb"], lw["gdn_norm"]]
    const = [consts["eexp"], consts["ones_bd"], consts["maskbd"]]
    args += layered + const
    in_specs += [_layer_spec(a, layer) for a in layered] + [_const_spec(a) for a in const]
    return pl.pallas_call(
        functools.partial(_gdn_kernel, seq_len=seq_len, cached=cached),
        grid=(nb,),
        in_specs=in_specs,
        out_specs=[
            pl.BlockSpec((seq_len, MIX_W), lambda b: (b, 0)),
            pl.BlockSpec((1, 2, HEADS, HEAD_DIM, HEAD_DIM), lambda b: (b, 0, 0, 0, 0)),
        ],
        out_shape=[
            jax.ShapeDtypeStruct((n, MIX_W), F32),
            jax.ShapeDtypeStruct((nb, 2, HEADS, HEAD_DIM, HEAD_DIM), F32),
        ],
        scratch_shapes=[
            pltpu.VMEM((seq_len, MIX_W), F32),
            pltpu.VMEM((seq_len, MIX_W), F32),
            pltpu.VMEM((seq_len, MIX_W), F32),
            pltpu.VMEM((seq_len, 4 * MIX_W), F32),
            pltpu.VMEM((2, seq_len, GDN_PRE_W), F32),
            pltpu.VMEM((2, seq_len // CHUNK * 8, MIX_W), F32),
            pltpu.VMEM((seq_len, MIX_W), F32),
            pltpu.VMEM((2, MIX_W, MIX_W), F32),
        ],
        compiler_params=pltpu.CompilerParams(dimension_semantics=("arbitrary",),
                                             vmem_limit_bytes=VMEM_LIMIT),
        name="gdn_lat" if cached else "gdn_ctx",
    )(*args)


RWKV_PRE_W = 7 * MIX_W


def _rwkv_prepare(items, maskbd, eye_full):
    n = range(len(items))
    rs, kds, vs, kks, bs, lws, masks, revs = zip(*items)
    cums = [_mm_sel_l(masks[i][2], lws[i]) for i in n]
    einvs = [jnp.exp(-c) for c in cums]
    kts = [kks[i] * jnp.exp(cums[i] - lws[i]) for i in n]
    rts = [rs[i] * jnp.exp(cums[i]) for i in n]
    krs = [jnp.concatenate([kts[i], rts[i]], axis=0) for i in n]
    lb_alls = [_mm_nt(krs[i], _bd(bs[i] * einvs[i], maskbd)) for i in n]
    lk_alls = [_mm_nt(krs[i], _bd(kds[i] * einvs[i], maskbd)) for i in n]
    lbs = [jnp.where(masks[i][1], lb_alls[i][:CHUNK], 0.0) for i in n]
    t_invs = _neumann_inverse(lbs, [m[3] for m in masks], maskbd)
    lvs = [_mm(jnp.concatenate([jnp.where(masks[i][1], lk_alls[i][:CHUNK], 0.0),
                                jnp.where(masks[i][0], lk_alls[i][CHUNK:], 0.0)], axis=0), _bd(vs[i], maskbd))
           for i in n]
    tkps = [_mm(t_invs[i], jnp.concatenate([_bd(kts[i], maskbd), _bd(lvs[i][:CHUNK], maskbd)], axis=1)) for i in n]
    out = []
    for i in n:
        c_last = cums[i][0:1] if revs[i] else cums[i][CHUNK - 1:CHUNK]
        tail = jnp.exp(c_last - cums[i])
        rb = jnp.where(masks[i][0], lb_alls[i][CHUNK:], 0.0)
        pre = jnp.concatenate([tkps[i][:, :MIX_W], rts[i], tkps[i][:, MIX_W:], lvs[i][CHUNK:], rb,
                               kds[i] * tail, bs[i] * tail], axis=1)
        gcol = jnp.sum(eye_full * jnp.exp(c_last), axis=1, keepdims=True)
        out.append((pre, jnp.broadcast_to(gcol, (MIX_W, LANES))))
    return out


def _rwkv_step(pre, v, gcol, z_bd, maskbd):
    w = MIX_W
    pr = _mm(jnp.concatenate([pre[:, :w], pre[:, w:2 * w]], axis=0), z_bd)
    p = pr[:CHUNK] + pre[:, 2 * w:3 * w]
    o = pr[CHUNK:] + pre[:, 3 * w:4 * w] - _mm(pre[:, 4 * w:5 * w], _bd(p, maskbd))
    upd = _mm_tn(jnp.concatenate([pre[:, 5 * w:6 * w], pre[:, 6 * w:]], axis=0), jnp.concatenate([v, -p], axis=0))
    z_new = z_bd * jnp.concatenate([gcol, gcol], axis=1) + upd * maskbd.astype(F32)
    return o, z_new


def _rwkv_kernel(*refs, seq_len, cached):
    if cached:
        (zr_ref, s0_ref, mup_ref, mun_ref, w0_ref, w2_ref, a0_ref, a2_ref, g2_ref, kk_ref, ka_ref, rk_ref,
         gnw_ref, gnb_ref, ones_ref, maskbd_ref, o_ref, sout_ref,
         r_s, v_s, kk_s, dir_s, bg_s, pre_s, gcol_s, oacc_s, st_s) = refs
    else:
        (zr_ref, mup_ref, mun_ref, w0_ref, w2_ref, a0_ref, a2_ref, g2_ref, kk_ref, ka_ref, rk_ref,
         gnw_ref, gnb_ref, ones_ref, maskbd_ref, o_ref, sout_ref,
         r_s, v_s, kk_s, dir_s, bg_s, pre_s, gcol_s, oacc_s, st_s) = refs
    t = seq_len
    z = zr_ref[...]
    rowi = _iota((t, 1), 0)
    zp = jnp.where(rowi == 0, 0.0, pltpu.roll(z, 1, 0))
    zn = jnp.where(rowi == t - 1, 0.0, pltpu.roll(z, t - 1, 0))
    z = z + mup_ref[...] * (zp - z) + mun_ref[...] * (zn - z)
    w = MIX_W
    r = z[:, :w]
    k = z[:, w:2 * w]
    v = z[:, 2 * w:3 * w]
    wd = jnp.tanh(z[:, 3 * w:3 * w + LANES])
    ad = z[:, 3 * w + LANES:3 * w + 2 * LANES]
    gd = _sigmoid(z[:, 3 * w + 2 * LANES:])
    ones_bd = ones_ref[...]
    kk = k * kk_ref[...]
    kk = kk * lax.rsqrt(_mm_sel_r(kk * kk, ones_bd) + 1e-6)
    r_s[...] = r
    v_s[...] = v
    kk_s[...] = kk
    bonus = jnp.zeros((t, w), F32)
    for d in range(2):
        w_log = -_softplus(-(w0_ref[d:d + 1] + _mm(wd, w2_ref[d]))) - 0.5
        a = _sigmoid(a0_ref[d:d + 1] + _mm(ad, a2_ref[d]))
        kd = k * (1.0 + (a - 1.0) * ka_ref[...])
        dir_s[:, (3 * d) * w:(3 * d + 1) * w] = -jnp.exp(w_log)
        dir_s[:, (3 * d + 1) * w:(3 * d + 2) * w] = kd
        dir_s[:, (3 * d + 2) * w:(3 * d + 3) * w] = kk * a
        bonus = bonus + _mm_sel_r(r * kd * rk_ref[...], ones_bd) * v
    bg_s[:, :w] = bonus
    bg_s[:, w:] = _mm(gd, g2_ref[...])
    oacc_s[...] = jnp.zeros((t, w), F32)
    if cached:
        st_s[...] = s0_ref[0]
    else:
        st_s[...] = jnp.zeros((2, w, w), F32)
    maskbd = maskbd_ref[...]
    masks = (_chunk_masks(False), _chunk_masks(True))
    eye_full = jnp.where(_iota((w, w), 0) == _iota((w, w), 1), 1.0, 0.0).astype(F32)
    n_chunks = t // CHUNK

    def prepare_group(c0):
        where, items = [], []
        for j in range(CHUNK_GROUP):
            c = c0 + j
            rows = pl.ds(pl.multiple_of(c * CHUNK, CHUNK), CHUNK)
            for d in range(2):
                where.append((d, c, rows))
                items.append((r_s[rows, :], dir_s[rows, (3 * d + 1) * w:(3 * d + 2) * w], v_s[rows, :], kk_s[rows, :],
                              dir_s[rows, (3 * d + 2) * w:(3 * d + 3) * w], dir_s[rows, (3 * d) * w:(3 * d + 1) * w],
                              masks[d], d == 1))
        for (d, c, rows), (pre, gcol) in zip(where, _rwkv_prepare(items, maskbd, eye_full)):
            pre_s[d, rows, :] = pre
            gcol_s[d, pl.ds(pl.multiple_of(c * w, w), w), :] = gcol

    _for_chunk_groups(n_chunks, prepare_group)

    def body(i, carry):
        for d in range(2):
            c = i if d == 0 else n_chunks - 1 - i
            rows = pl.ds(pl.multiple_of(c * CHUNK, CHUNK), CHUNK)
            gcol = gcol_s[d, pl.ds(pl.multiple_of(c * w, w), w), :]
            o, z_new = _rwkv_step(pre_s[d, rows, :], v_s[rows, :], gcol, st_s[d], maskbd)
            oacc_s[rows, :] = oacc_s[rows, :] + o
            st_s[d] = z_new
        return carry

    lax.fori_loop(0, n_chunks, body, 0)
    o = oacc_s[...]
    inv_n = 1.0 / HEAD_DIM
    mu = _mm_sel_r(o, ones_bd) * inv_n
    oc = o - mu
    var = _mm_sel_r(oc * oc, ones_bd) * inv_n
    y = oc * lax.rsqrt(var + RWKV_GN_EPS) * gnw_ref[...] + gnb_ref[...]
    o_ref[...] = (y + bg_s[:, :w]) * bg_s[:, w:]
    for d in range(2):
        for h in range(HEADS):
            sl = slice(h * HEAD_DIM, (h + 1) * HEAD_DIM)
            sout_ref[0, d, h] = st_s[d, sl, sl]


def _rwkv(zr, lw, layer, consts, seq_len, s0_bd=None):
    n = zr.shape[0]
    nb = n // seq_len
    cached = s0_bd is not None
    args = [zr]
    in_specs = [pl.BlockSpec((seq_len, ZR_W), lambda b: (b, 0))]
    if cached:
        args.append(s0_bd)
        in_specs.append(pl.BlockSpec((1, None, 2, MIX_W, MIX_W), lambda b: (b, layer, 0, 0, 0)))
    layered = [lw["rwkv_mu_prev"], lw["rwkv_mu_next"], lw["rwkv_w0"], lw["rwkv_w2"], lw["rwkv_a0"], lw["rwkv_a2"],
               lw["rwkv_g2"], lw["rwkv_k_k"], lw["rwkv_k_a"], lw["rwkv_r_k"], lw["rwkv_gn_w"], lw["rwkv_gn_b"]]
    const = [consts["ones_bd"], consts["maskbd"]]
    args += layered + const
    in_specs += [_layer_spec(a, layer) for a in layered] + [_const_spec(a) for a in const]
    return pl.pallas_call(
        functools.partial(_rwkv_kernel, seq_len=seq_len, cached=cached),
        grid=(nb,),
        in_specs=in_specs,
        out_specs=[
            pl.BlockSpec((seq_len, MIX_W), lambda b: (b, 0)),
            pl.BlockSpec((1, 2, HEADS, HEAD_DIM, HEAD_DIM), lambda b: (b, 0, 0, 0, 0)),
        ],
        out_shape=[
            jax.ShapeDtypeStruct((n, MIX_W), F32),
            jax.ShapeDtypeStruct((nb, 2, HEADS, HEAD_DIM, HEAD_DIM), F32),
        ],
        scratch_shapes=[
            pltpu.VMEM((seq_len, MIX_W), F32),
            pltpu.VMEM((seq_len, MIX_W), F32),
            pltpu.VMEM((seq_len, MIX_W), F32),
            pltpu.VMEM((seq_len, 6 * MIX_W), F32),
            pltpu.VMEM((seq_len, 2 * MIX_W), F32),
            pltpu.VMEM((2, seq_len, RWKV_PRE_W), F32),
            pltpu.VMEM((2, seq_len // CHUNK * MIX_W, LANES), F32),
            pltpu.VMEM((seq_len, MIX_W), F32),
            pltpu.VMEM((2, MIX_W, MIX_W), F32),
        ],
        compiler_params=pltpu.CompilerParams(dimension_semantics=("arbitrary",),
                                             vmem_limit_bytes=VMEM_LIMIT),
        name="rwkv_lat" if cached else "rwkv_ctx",
    )(*args)


def _route(logits_t, bias):
    tm = logits_t.shape[1]
    neg = -jnp.inf
    sc = _sigmoid(logits_t)
    sc3 = sc.reshape(N_GROUPS, GROUP_SIZE, tm)
    sel = (sc + bias).reshape(N_GROUPS, GROUP_SIZE, tm)
    si = _iota(sel.shape, 1).astype(F32)
    m1 = jnp.max(sel, axis=1, keepdims=True)
    f1 = jnp.min(jnp.where(sel == m1, si, float(GROUP_SIZE)), axis=1, keepdims=True)
    m2 = jnp.max(jnp.where(si == f1, neg, sel), axis=1, keepdims=True)
    grp = m1 + m2
    gi = _iota(grp.shape, 0).astype(F32)
    gsel = jnp.zeros(grp.shape, F32)
    for _ in range(TOPK_GROUPS):
        mx = jnp.max(grp, axis=0, keepdims=True)
        fi = jnp.min(jnp.where(grp == mx, gi, float(N_GROUPS)), axis=0, keepdims=True)
        hit = gi == fi
        gsel = jnp.where(hit, 1.0, gsel)
        grp = jnp.where(hit, neg, grp)
    cur = jnp.where(gsel > 0.0, sel, neg)
    ei = (_iota(cur.shape, 0) * GROUP_SIZE + _iota(cur.shape, 1)).astype(F32)
    chosen = jnp.zeros(cur.shape, F32)
    ids, wts = [], []
    for _ in range(TOP_K):
        mx = jnp.max(jnp.max(cur, axis=0, keepdims=True), axis=1, keepdims=True)
        fi = jnp.min(jnp.min(jnp.where(cur == mx, ei, float(N_EXPERTS)), axis=0, keepdims=True),
                     axis=1, keepdims=True)
        hit = ei == fi
        chosen = jnp.where(hit, 1.0, chosen)
        cur = jnp.where(hit, neg, cur)
        ids.append(fi.reshape(1, tm))
        wts.append(jnp.sum(jnp.sum(jnp.where(hit, sc3, 0.0), axis=0, keepdims=True), axis=1, keepdims=True)
                   .reshape(1, tm))
    w = jnp.concatenate(wts, axis=0)
    w = w / jnp.sum(w, axis=0, keepdims=True) * ROUTE_SCALE
    return chosen.reshape(N_EXPERTS, tm), jnp.concatenate(ids, axis=0), w


def _pack_halves(x):
    half = x.shape[1] // 2
    bits = lax.bitcast_convert_type(x.astype(BF16).astype(F32), jnp.int32)
    lo = lax.shift_right_logical(bits[:, :half], jnp.int32(16))
    return jnp.bitwise_or(lo, jnp.bitwise_and(bits[:, half:], jnp.int32(-65536)))


def _unpack_halves(word):
    lo = lax.bitcast_convert_type(lax.shift_left(word, jnp.int32(16)), F32)
    hi = lax.bitcast_convert_type(jnp.bitwise_and(word, jnp.int32(-65536)), F32)
    return lo, hi


def _post_kernel(x_ref, om_ref, og_ref, or_ref, mod_ref, wo_ref, n2_ref, rt_ref, rb_ref, tri_ref,
                 x1_ref, h2_ref, eid_ref, rank_ref, ew_ref, cnt_ref, carry_s):
    @pl.when(pl.program_id(0) == 0)
    def _():
        carry_s[...] = jnp.zeros(carry_s.shape, F32)

    m = mod_ref[0]
    g1 = m[:, 2 * D_MODEL:3 * D_MODEL]
    sh2 = m[:, 3 * D_MODEL:4 * D_MODEL]
    sc2 = m[:, 4 * D_MODEL:5 * D_MODEL]
    w_mla = MLA_HEADS * MLA_V
    mix = (_mm(om_ref[...], wo_ref[0:w_mla, :]) + _mm(og_ref[...], wo_ref[w_mla:w_mla + MIX_W, :])
           + _mm(or_ref[...], wo_ref[w_mla + MIX_W:, :]))
    x1 = x_ref[...] + g1 * mix
    x1_ref[...] = x1
    h2 = _rms(x1, n2_ref[...]) * (1.0 + sc2) + sh2
    h2_ref[...] = _pack_halves(h2)
    r_hi, r_lo = _split2(rt_ref[...])
    h_hi, h_lo = _split2(h2)
    logits_t = _mm_nt(r_hi, h_hi) + _mm_nt(r_hi, h_lo) + _mm_nt(r_lo, h_hi)
    chosen, ids, w = _route(logits_t, rb_ref[...])
    tm = chosen.shape[1]
    rank_et = (carry_s[:, 0:1] + _mm(chosen, tri_ref[...])).reshape(N_GROUPS, GROUP_SIZE, tm)
    ei = (_iota(rank_et.shape, 0) * GROUP_SIZE + _iota(rank_et.shape, 1)).astype(F32)
    ranks = []
    for k in range(TOP_K):
        pick = jnp.where(ei == ids[k:k + 1].reshape(1, 1, tm), rank_et, 0.0)
        ranks.append(jnp.sum(jnp.sum(pick, axis=0, keepdims=True), axis=1, keepdims=True).reshape(1, tm))
    eid_ref[...] = ids.astype(jnp.int32)
    rank_ref[...] = jnp.concatenate(ranks, axis=0).astype(jnp.int32)
    ew_ref[...] = jnp.concatenate([w, jnp.zeros((LANES - TOP_K, tm), F32)], axis=0).T
    total = carry_s[...] + jnp.sum(chosen, axis=1, keepdims=True)
    carry_s[...] = total
    cnt_ref[...] = total


def _post(x2d, om, og, orw, mods, lw, layer, consts, tm, seq_len, mod_base):
    n = x2d.shape[0]
    tiles_per_seq = seq_len // tm if mod_base else 1

    def mod_idx(i):
        return (layer, mod_base + i // tiles_per_seq if mod_base else 0, 0, 0)

    row = lambda w: pl.BlockSpec((tm, w), lambda i: (i, 0))
    col = lambda h: pl.BlockSpec((h, tm), lambda i: (0, i))
    full = lambda a: _layer_spec(a, layer)
    tail = [lw["w_out"], lw["norm2"], lw["router_t"], lw["router_b"]]
    tri = consts["tri_tokens"]
    return pl.pallas_call(
        _post_kernel,
        grid=(n // tm,),
        in_specs=[row(D_MODEL), row(MLA_HEADS * MLA_V), row(MIX_W), row(MIX_W),
                  pl.BlockSpec((None, 1, 1, 6 * D_MODEL), mod_idx)] + [full(a) for a in tail] + [_const_spec(tri)],
        out_specs=[row(D_MODEL), row(D_MODEL // 2), col(TOP_K), col(TOP_K), row(LANES),
                   pl.BlockSpec((N_EXPERTS, LANES), lambda i: (0, 0))],
        out_shape=[
            jax.ShapeDtypeStruct((n, D_MODEL), F32),
            jax.ShapeDtypeStruct((n, D_MODEL // 2), jnp.int32),
            jax.ShapeDtypeStruct((TOP_K, n), jnp.int32),
            jax.ShapeDtypeStruct((TOP_K, n), jnp.int32),
            jax.ShapeDtypeStruct((n, LANES), F32),
            jax.ShapeDtypeStruct((N_EXPERTS, LANES), F32),
        ],
        scratch_shapes=[pltpu.VMEM((N_EXPERTS, LANES), F32)],
        compiler_params=pltpu.CompilerParams(dimension_semantics=("arbitrary",),
                                             vmem_limit_bytes=VMEM_LIMIT),
        name="post",
    )(x2d, om, og, orw, mods, *tail, tri)


MOE_ROWS = 512
SC_ROWS = 128
SC_SUBCORES = 32


def _swiglu_act(gu):
    return _silu(gu[:, :D_EXPERT]) * gu[:, D_EXPERT:]


def _dispatch_plan(eid, rank, counts, n, rows):
    n_blocks = n * TOP_K // rows + N_EXPERTS
    cnt = counts[:, 0].astype(jnp.int32)
    blocks = (cnt + rows - 1) // rows
    block_end = jnp.cumsum(blocks)
    offset = (block_end - blocks) * rows
    experts = jnp.arange(N_EXPERTS, dtype=jnp.int32)
    dest = jnp.sum(jnp.where(eid[..., None] == experts, offset, 0), axis=-1) + rank
    block_ids = jnp.arange(n_blocks, dtype=jnp.int32)
    block_expert = jnp.minimum(jnp.sum((block_end[None, :] <= block_ids[:, None]).astype(jnp.int32), axis=1),
                               N_EXPERTS - 1)
    return dest, block_expert, block_end[-1:].astype(jnp.int32), n_blocks


def _sc_mesh():
    return plsc.VectorSubcoreMesh(core_axis_name="core", subcore_axis_name="subcore")


def _sc_dispatch(x, dest, n_rows):
    n, w = x.shape
    n_chunks = n // SC_ROWS

    @functools.partial(pl.kernel, out_type=jax.ShapeDtypeStruct((n_rows, w), x.dtype), mesh=_sc_mesh(),
                       scratch_types=[pltpu.VMEM((SC_ROWS, w), x.dtype), pltpu.VMEM((TOP_K, SC_ROWS), jnp.int32)])
    def kern(x_hbm, d_hbm, o_hbm, xv, dv):
        sid = lax.axis_index("core") * (SC_SUBCORES // 2) + lax.axis_index("subcore")

        @pl.loop(sid, n_chunks, step=SC_SUBCORES)
        def _(c):
            r0 = pl.multiple_of(c * SC_ROWS, SC_ROWS)
            pltpu.sync_copy(x_hbm.at[pl.ds(r0, SC_ROWS)], xv)
            pltpu.sync_copy(d_hbm.at[:, pl.ds(r0, SC_ROWS)], dv)
            for k in range(TOP_K):
                pltpu.sync_copy(xv, o_hbm.at[dv.at[k]])

    return kern(x, dest)


def _sc_gather(y, idx):
    w = y.shape[1]
    n_chunks = idx.shape[0]

    @functools.partial(pl.kernel, out_type=jax.ShapeDtypeStruct((n_chunks * SC_ROWS, w), y.dtype), mesh=_sc_mesh(),
                       scratch_types=[pltpu.VMEM((SC_ROWS, w), y.dtype), pltpu.VMEM((1, SC_ROWS), jnp.int32)])
    def kern(y_hbm, i_hbm, o_hbm, ov, iv):
        sid = lax.axis_index("core") * (SC_SUBCORES // 2) + lax.axis_index("subcore")

        @pl.loop(sid, n_chunks, step=SC_SUBCORES)
        def _(c):
            pltpu.sync_copy(i_hbm.at[pl.ds(c, 1)], iv)
            pltpu.sync_copy(y_hbm.at[iv.at[0]], ov)
            pltpu.sync_copy(ov, o_hbm.at[pl.ds(pl.multiple_of(c * SC_ROWS, SC_ROWS), SC_ROWS)])

    return kern(y, idx)


def _moe_rows_kernel(be_ref, nu_ref, x_ref, wgu_ref, wdn_ref, y_ref):
    @pl.when(pl.program_id(0) < nu_ref[0])
    def _():
        half = D_MODEL // 2
        lo, hi = _unpack_halves(x_ref[...])
        gu = _mm(lo, wgu_ref[0:half, :]) + _mm(hi, wgu_ref[half:, :])
        y_ref[...] = _pack_halves(_mm(_swiglu_act(gu), wdn_ref[...]))


def _moe_rows(xs, block_expert, n_used, lw, layer, n_blocks, rows):
    half = D_MODEL // 2
    last = lambda b, be, nu: jnp.minimum(b, nu[0] - 1)
    return pl.pallas_call(
        _moe_rows_kernel,
        grid_spec=pltpu.PrefetchScalarGridSpec(
            num_scalar_prefetch=2,
            grid=(n_blocks,),
            in_specs=[pl.BlockSpec((rows, half), lambda b, be, nu: (last(b, be, nu), 0)),
                      pl.BlockSpec((None, None, D_MODEL, 2 * D_EXPERT),
                                   lambda b, be, nu: (layer, be[last(b, be, nu)], 0, 0)),
                      pl.BlockSpec((None, None, D_EXPERT, D_MODEL),
                                   lambda b, be, nu: (layer, be[last(b, be, nu)], 0, 0))],
            out_specs=pl.BlockSpec((rows, half), lambda b, be, nu: (last(b, be, nu), 0)),
        ),
        out_shape=jax.ShapeDtypeStruct(xs.shape, jnp.int32),
        compiler_params=pltpu.CompilerParams(dimension_semantics=("arbitrary",),
                                             vmem_limit_bytes=VMEM_LIMIT),
        name="moe_rows",
    )(block_expert, n_used, xs, lw["moe_w_gu"], lw["moe_w_down"])


def _moe_combine_kernel(yg_ref, ew_ref, h_ref, x1_ref, mod_ref, sgu_ref, sdn_ref, nf_ref, o_ref, *, final):
    half = D_MODEL // 2
    lo, hi = _unpack_halves(h_ref[...])
    gu = _mm(lo, sgu_ref[0:half, :]) + _mm(hi, sgu_ref[half:, :])
    acc = _mm(_swiglu_act(gu), sdn_ref[...])
    ew = ew_ref[...]
    acc_lo = acc[:, :half]
    acc_hi = acc[:, half:]
    for k in range(TOP_K):
        lo, hi = _unpack_halves(yg_ref[k])
        wk = ew[:, k:k + 1]
        acc_lo = acc_lo + wk * lo
        acc_hi = acc_hi + wk * hi
    g2 = mod_ref[0][:, 5 * D_MODEL:]
    x2 = x1_ref[...] + g2 * jnp.concatenate([acc_lo, acc_hi], axis=1)
    if final:
        x2 = _rms(x2, nf_ref[...])
    o_ref[...] = x2


def _moe_combine(yg, ew, h2p, x1, mods, lw, layer, norm_f, tm, seq_len, mod_base, final):
    n = x1.shape[0]
    half = D_MODEL // 2
    tiles_per_seq = seq_len // tm if mod_base else 1

    def mod_idx(i):
        return (layer, mod_base + i // tiles_per_seq if mod_base else 0, 0, 0)

    row = lambda w: pl.BlockSpec((tm, w), lambda i: (i, 0))
    return pl.pallas_call(
        functools.partial(_moe_combine_kernel, final=final),
        grid=(n // tm,),
        in_specs=[pl.BlockSpec((TOP_K, tm, half), lambda i: (0, i, 0)), row(LANES), row(half), row(D_MODEL),
                  pl.BlockSpec((None, 1, 1, 6 * D_MODEL), mod_idx),
                  _layer_spec(lw["shared_w_gu"], layer), _layer_spec(lw["shared_w_down"], layer),
                  _const_spec(norm_f)],
        out_specs=row(D_MODEL),
        out_shape=jax.ShapeDtypeStruct((n, D_MODEL), F32),
        compiler_params=pltpu.CompilerParams(dimension_semantics=("arbitrary",),
                                             vmem_limit_bytes=VMEM_LIMIT),
        name="moe_combine_final" if final else "moe_combine",
    )(yg, ew, h2p, x1, mods, lw["shared_w_gu"], lw["shared_w_down"], norm_f)


def _moe(h2p, x1, eid, rank, ew, counts, mods, lw, layer, norm_f, tm, seq_len, mod_base, final):
    n = x1.shape[0]
    rows = MOE_ROWS if n >= MOE_ROWS * N_EXPERTS // TOP_K * 2 else MOE_ROWS // 2
    dest, block_expert, n_used, n_blocks = _dispatch_plan(eid, rank, counts, n, rows)
    xs = _sc_dispatch(h2p, dest, n_blocks * rows)
    y = _moe_rows(xs, block_expert, n_used, lw, layer, n_blocks, rows)
    yg = _sc_gather(y, dest.reshape(n * TOP_K // SC_ROWS, SC_ROWS)).reshape(TOP_K, n, D_MODEL // 2)
    return _moe_combine(yg, ew, h2p, x1, mods, lw, layer, norm_f, tm, seq_len, mod_base, final)


def _constants():
    idx = np.arange(MIX_W)
    same_head = (idx[:, None] // HEAD_DIM) == (idx[None, :] // HEAD_DIM)
    maskbd = jnp.asarray(same_head, BF16)
    eexp = np.zeros((LANES, 4 * MIX_W), np.float32)
    for blk in range(4):
        kind, d = divmod(blk, 2)
        for h in range(HEADS):
            src = kind * 2 * HEADS + d * HEADS + h
            eexp[src, blk * MIX_W + h * HEAD_DIM: blk * MIX_W + (h + 1) * HEAD_DIM] = 1.0
    tri = np.triu(np.ones((POST_TM, POST_TM), np.float32), 1)
    return {"maskbd": maskbd, "ones_bd": maskbd, "eexp": jnp.asarray(eexp, BF16), "tri_tokens": jnp.asarray(tri, BF16)}


def _rope_tables(n):
    rows = n // GRID_W
    row = jnp.repeat(jnp.arange(rows, dtype=F32), GRID_W)
    col = jnp.tile(jnp.arange(GRID_W, dtype=F32), rows)
    axis_dim = MLA_ROPE // 2
    inv = jnp.power(ROPE_BASE, -jnp.arange(0, axis_dim, 2, dtype=F32) / axis_dim)
    ang_r = row[:, None] * inv
    ang_c = col[:, None] * inv
    cr, sr, cc, sc = jnp.cos(ang_r), jnp.sin(ang_r), jnp.cos(ang_c), jnp.sin(ang_c)
    zeros = jnp.zeros((n, LANES - MLA_ROPE), F32)
    cos_t = jnp.concatenate([cr, cr, cc, cc, zeros], axis=1)
    sin_t = jnp.concatenate([-sr, sr, -sc, sc, zeros], axis=1)
    return cos_t, sin_t


def _stacked_weights(p):
    w_uq = p["mla_w_uq"].reshape(DEPTH, Q_LORA, MLA_HEADS, MLA_NOPE + MLA_ROPE)
    zq = jnp.zeros((DEPTH, Q_LORA, MLA_HEADS, QH_W - MLA_NOPE - MLA_ROPE), F32)
    w_uq_a = jnp.concatenate([w_uq, zq], axis=-1).reshape(DEPTH, Q_LORA, MLA_HEADS * QH_W).astype(BF16)
    w_uq_sw = jnp.concatenate([jnp.zeros((DEPTH, Q_LORA, MLA_HEADS, MLA_NOPE), F32),
                               w_uq[..., MLA_NOPE + _ROPE_SWAP], zq], axis=-1)
    w_uq_sw = w_uq_sw.reshape(DEPTH, Q_LORA, MLA_HEADS * QH_W).astype(BF16)

    def per_direction(w):
        half = jnp.zeros((DEPTH, 64, MIX_W), F32)
        return jnp.stack([jnp.concatenate([w[:, 0], half], axis=1),
                          jnp.concatenate([half, w[:, 1]], axis=1)], axis=1).astype(BF16)

    row = lambda v: v.reshape(DEPTH, 1, -1)
    pad_row = lambda v: jnp.pad(row(v), ((0, 0), (0, 0), (0, LANES - 2 * HEADS)))
    return {
        "norm1": row(p["norm1"]),
        "w_in": p["w_in"],
        "q_norm": row(p["mla_q_norm"]),
        "w_uq": w_uq_a, "w_uq_sw": w_uq_sw,
        "kv_norm": row(p["mla_kv_norm"]),
        "w_ukv": p["mla_w_ukv"].astype(BF16),
        "gdn_conv": p["gdn_conv"],
        "gdn_alog": pad_row(p["gdn_a_log"]),
        "gdn_dtb": pad_row(p["gdn_dt_bias"]),
        "gdn_norm": jnp.tile(row(p["gdn_norm"]), (1, 1, HEADS)),
        "rwkv_mu_prev": row(p["rwkv_mu_prev"]),
        "rwkv_mu_next": row(p["rwkv_mu_next"]),
        "rwkv_w0": p["rwkv_w0"],
        "rwkv_w2": per_direction(p["rwkv_w2"]),
        "rwkv_a0": p["rwkv_a0"],
        "rwkv_a2": per_direction(p["rwkv_a2"]),
        "rwkv_g2": p["rwkv_g2"].astype(BF16),
        "rwkv_k_k": row(p["rwkv_k_k"]),
        "rwkv_k_a": row(p["rwkv_k_a"]),
        "rwkv_r_k": row(p["rwkv_r_k"]),
        "rwkv_gn_w": row(p["rwkv_gn_w"]),
        "rwkv_gn_b": row(p["rwkv_gn_b"]),
        "w_out": p["w_out"].astype(BF16),
        "norm2": row(p["norm2"]),
        "router_t": jnp.swapaxes(p["moe_router"], 1, 2),
        "router_b": p["moe_bias"].reshape(DEPTH, N_EXPERTS, 1),
        "moe_w_gu": p["moe_w_gu"],
        "moe_w_down": p["moe_w_down"],
        "shared_w_gu": p["shared_w_gu"].astype(BF16),
        "shared_w_down": p["shared_w_down"].astype(BF16),
    }


def _embed_block_diag(s):
    b = s.shape[0]
    eye = jnp.eye(HEADS, dtype=s.dtype)
    out = jnp.einsum("bdhkv,hg->bdhkgv", s, eye)
    return out.reshape(b, 2, MIX_W, MIX_W)


def _trunk(x2d, mods, lw, consts, norm_f, seq_len, mod_base, cache, tm, tq):
    outs = []
    for l in range(DEPTH):
        zm, zg, zab, zr = _inproj(x2d, mods, lw, l, tm, seq_len, mod_base)
        if cache is None:
            o_mla, ckv = _mla(zm, lw, l, seq_len, tq)
            o_gdn, s_gdn = _gdn(zg, zab, lw, l, consts, seq_len)
            o_rwkv, s_rwkv = _rwkv(zr, lw, l, consts, seq_len)
            s_rwkv = jnp.swapaxes(s_rwkv, -1, -2)
        else:
            cckv, ckpe, rc, rs, sg, sr = cache
            o_mla, ckv = _mla(zm, lw, l, seq_len, tq, (cckv, ckpe, rc, rs))
            o_gdn, s_gdn = _gdn(zg, zab, lw, l, consts, seq_len, sg)
            o_rwkv, s_rwkv = _rwkv(zr, lw, l, consts, seq_len, sr)
        x1, h2p, eid, rank, ew, counts = _post(x2d, o_mla, o_gdn, o_rwkv, mods, lw, l, consts, tm, seq_len, mod_base)
        x2d = _moe(h2p, x1, eid, rank, ew, counts, mods, lw, l, norm_f, tm, seq_len, mod_base,
                   final=(l == DEPTH - 1))
        kpe0 = Q_LORA + KV_LORA
        outs.append((ckv, zm[:, kpe0:kpe0 + MLA_ROPE], s_gdn, s_rwkv))
    return x2d, outs


def kernel(x_prompt, x_sample, cache_mla_ckv, cache_mla_kpe, state_gdn, state_rwkv, c, c_ctx, ada_w, ada_b, norm1, w_in, mla_q_norm, mla_w_uq, mla_kv_norm, mla_w_ukv, gdn_conv, gdn_a_log, gdn_dt_bias, gdn_norm, rwkv_mu_prev, rwkv_mu_next, rwkv_w0, rwkv_w2, rwkv_a0, rwkv_a2, rwkv_g2, rwkv_k_k, rwkv_k_a, rwkv_r_k, rwkv_gn_w, rwkv_gn_b, w_out, norm2, moe_router, moe_bias, moe_w_gu, moe_w_down, shared_w_gu, shared_w_down, norm_f):
    p = dict(norm1=norm1, w_in=w_in, mla_q_norm=mla_q_norm, mla_w_uq=mla_w_uq, mla_kv_norm=mla_kv_norm,
             mla_w_ukv=mla_w_ukv, gdn_conv=gdn_conv, gdn_a_log=gdn_a_log, gdn_dt_bias=gdn_dt_bias,
             gdn_norm=gdn_norm, rwkv_mu_prev=rwkv_mu_prev, rwkv_mu_next=rwkv_mu_next, rwkv_w0=rwkv_w0,
             rwkv_w2=rwkv_w2, rwkv_a0=rwkv_a0, rwkv_a2=rwkv_a2, rwkv_g2=rwkv_g2, rwkv_k_k=rwkv_k_k,
             rwkv_k_a=rwkv_k_a, rwkv_r_k=rwkv_r_k, rwkv_gn_w=rwkv_gn_w, rwkv_gn_b=rwkv_gn_b, w_out=w_out,
             norm2=norm2, moe_router=moe_router, moe_bias=moe_bias, moe_w_gu=moe_w_gu, moe_w_down=moe_w_down,
             shared_w_gu=shared_w_gu, shared_w_down=shared_w_down)
    weights = _stacked_weights(p)
    consts = _constants()
    nf = norm_f.reshape(1, D_MODEL)
    b_ctx, t_ctx, _ = x_prompt.shape
    b_lat, t_lat, _ = x_sample.shape

    cvec8 = jnp.concatenate([c_ctx[None, :], c, jnp.zeros((8 - 1 - b_lat, D_MODEL), F32)], axis=0)
    mods = _adaln(cvec8, ada_w, ada_b)
    mods = mods.reshape(DEPTH, 8, 1, 6 * D_MODEL)

    xp, ctx_outs = _trunk(x_prompt.reshape(b_ctx * t_ctx, D_MODEL), mods, weights, consts, nf,
                          t_ctx, 0, None, tm=POST_TM, tq=t_ctx)
    rc, rs = _rope_tables(t_lat)
    ckpe = jnp.pad(cache_mla_kpe, ((0, 0), (0, 0), (0, 0), (0, LANES - MLA_ROPE)))
    cache = (cache_mla_ckv, ckpe, rc, rs, _embed_block_diag_layers(state_gdn),
             _embed_block_diag_layers(jnp.swapaxes(state_rwkv, -1, -2)))
    xs, _ = _trunk(x_sample.reshape(b_lat * t_lat, D_MODEL), mods, weights, consts, nf,
                   t_lat, 1, cache, tm=POST_TM, tq=256)

    y_prompt = xp.reshape(b_ctx, t_ctx, D_MODEL)
    y_sample = xs.reshape(b_lat, t_lat, D_MODEL)
    new_ckv = jnp.stack([o[0].reshape(b_ctx, t_ctx, KV_LORA) for o in ctx_outs], axis=1)
    new_kpe = jnp.stack([o[1].reshape(b_ctx, t_ctx, MLA_ROPE) for o in ctx_outs], axis=1)
    new_gdn = jnp.stack([o[2] for o in ctx_outs], axis=1)
    new_rwkv = jnp.stack([o[3] for o in ctx_outs], axis=1)
    return (y_prompt, y_sample, new_ckv, new_kpe, new_gdn, new_rwkv)


def _embed_block_diag_layers(s):
    b = s.shape[0]
    return _embed_block_diag(s.reshape(b * DEPTH, 2, HEADS, HEAD_DIM, HEAD_DIM)).reshape(
        b, DEPTH, 2, MIX_W, MIX_W)
```

```python
import functools

import numpy as np
import jax
import jax.numpy as jnp
from jax import lax
from jax.experimental import pallas as pl
from jax.experimental.pallas import tpu as pltpu
from jax.experimental.pallas import tpu_sc as plsc

F32 = jnp.float32
BF16 = jnp.bfloat16

D_MODEL = 1024
BATCH = 32
SEQ = 256
DEPTH = 2
DEC_BATCH = 2
DEC_SEQ = 1024
PAST_LEN = 512
GRID_W = 64
NORM_EPS = 1e-6

MLA_HEADS = 4
MLA_NOPE = 128
MLA_ROPE = 64
MLA_V = 128
Q_LORA = 384
KV_LORA = 256
ROPE_BASE = 10000.0
MLA_SCALE = (MLA_NOPE + MLA_ROPE) ** -0.5

HEADS = 4
HEAD_DIM = 64
MIX_W = HEADS * HEAD_DIM
GDN_CONV_CH = 3 * MIX_W
CHUNK = 64
RWKV_GN_EPS = 64e-5

N_EXPERTS = 64
TOP_K = 8
N_GROUPS = 8
GROUP_SIZE = N_EXPERTS // N_GROUPS
TOPK_GROUPS = 4
D_EXPERT = 256
ROUTE_SCALE = 2.5

P_MLA = Q_LORA + KV_LORA + MLA_ROPE
P_GDN = GDN_CONV_CH + MIX_W + 4 * HEADS
P_RWKV = 3 * MIX_W + 128 + 128 + 128

LANES = 128
ZM_W = Q_LORA + KV_LORA + 2 * LANES
ZG_W = GDN_CONV_CH + MIX_W
ZR_W = P_RWKV
QH_W = 2 * LANES
VMEM_LIMIT = 56 * 1024 * 1024
POST_TM = 512

_ROPE_SWAP = np.concatenate([np.arange(16, 32), np.arange(0, 16), np.arange(48, 64), np.arange(32, 48)])


def _sigmoid(x):
    return 1.0 / (1.0 + jnp.exp(-x))


def _silu(x):
    return x * _sigmoid(x)


def _softplus(x):
    return jnp.maximum(x, 0.0) + jnp.log(1.0 + jnp.exp(-jnp.abs(x)))


def _rms(x, g, eps=NORM_EPS):
    return x * lax.rsqrt(jnp.mean(x * x, axis=-1, keepdims=True) + eps) * g


def _mm(a, b):
    return jnp.dot(a.astype(BF16), b.astype(BF16), preferred_element_type=F32)


def _mm_nt(a, b):
    return lax.dot_general(a.astype(BF16), b.astype(BF16), (((1,), (1,)), ((), ())),
                           preferred_element_type=F32)


def _mm_tn(a, b):
    return lax.dot_general(a.astype(BF16), b.astype(BF16), (((0,), (0,)), ((), ())),
                           preferred_element_type=F32)


def _split3(x):
    p1 = x.astype(BF16)
    r1 = x - p1.astype(F32)
    p2 = r1.astype(BF16)
    r2 = r1 - p2.astype(F32)
    return p1, p2, r2.astype(BF16)


def _mm_sel_l(sel, x):
    p1, p2, p3 = _split3(x)
    return _mm(sel, p1) + _mm(sel, p2) + _mm(sel, p3)


def _mm_sel_r(x, sel):
    p1, p2, p3 = _split3(x)
    return _mm(p1, sel) + _mm(p2, sel) + _mm(p3, sel)


def _iota(shape, dim):
    return lax.broadcasted_iota(jnp.int32, shape, dim)


def _layer_spec(a, layer, **kw):
    nd = a.ndim - 1
    return pl.BlockSpec((None,) + a.shape[1:], lambda *_: (layer,) + (0,) * nd, **kw)


def _const_spec(a, **kw):
    return pl.BlockSpec(a.shape, lambda *_: (0,) * a.ndim, **kw)


def _bd(x, maskbd):
    xb = x.astype(BF16)
    return jnp.concatenate([xb] * HEADS, axis=0) * maskbd


def _chunk_masks(rev):
    row = _iota((CHUNK, MIX_W), 0)
    col = jnp.bitwise_and(_iota((CHUNK, MIX_W), 1), HEAD_DIM - 1)
    r2 = _iota((CHUNK, CHUNK), 0)
    c2 = _iota((CHUNK, CHUNK), 1)
    if rev:
        inc, strict, tri = row <= col, row < col, r2 <= c2
    else:
        inc, strict, tri = row >= col, row > col, r2 >= c2
    eye = jnp.where(row == col, 1.0, 0.0).astype(F32)
    return inc, strict, jnp.where(tri, 1.0, 0.0).astype(BF16), eye


def _split2(x):
    hi = x.astype(BF16)
    return hi, (x - hi.astype(F32)).astype(BF16)


def _mm_bd3(x, p, maskbd):
    n = x.shape[0]
    xh, xl = _split2(x)
    ph, pl_ = _split2(p)
    r = jnp.dot(jnp.concatenate([xh, xl], axis=0), _bd(ph, maskbd), preferred_element_type=F32)
    return r[:n] + r[n:] + jnp.dot(xh, _bd(pl_, maskbd), preferred_element_type=F32)


def _neumann_inverse(a_list, eye_list, maskbd):
    bs = [-a for a in a_list]
    ms = [eye + b for eye, b in zip(eye_list, bs)]
    ps = [_mm_bd3(b, b, maskbd) for b in bs]
    for _ in range(4):
        boths = [_mm_bd3(jnp.concatenate([m, p], axis=0), p, maskbd) for m, p in zip(ms, ps)]
        ms = [m + both[:CHUNK] for m, both in zip(ms, boths)]
        ps = [both[CHUNK:] for both in boths]
    return [m + _mm_bd3(m, p, maskbd) for m, p in zip(ms, ps)]


def _adaln_kernel(c_ref, w_ref, b_ref, o_ref):
    cv = c_ref[...]
    o_ref[0] = _mm(_silu(cv), w_ref[0]) + b_ref[0]


def _adaln(cvec8, ada_w, ada_b):
    tn = 768
    n_out = 6 * D_MODEL
    return pl.pallas_call(
        _adaln_kernel,
        grid=(DEPTH, n_out // tn),
        in_specs=[
            pl.BlockSpec((8, D_MODEL), lambda l, j: (0, 0)),
            pl.BlockSpec((1, D_MODEL, tn), lambda l, j: (l, 0, j)),
            pl.BlockSpec((1, 1, tn), lambda l, j: (l, 0, j)),
        ],
        out_specs=pl.BlockSpec((1, 8, tn), lambda l, j: (l, 0, j)),
        out_shape=jax.ShapeDtypeStruct((DEPTH, 8, n_out), F32),
        compiler_params=pltpu.CompilerParams(dimension_semantics=("arbitrary", "arbitrary"),
                                             vmem_limit_bytes=VMEM_LIMIT),
        name="adaln",
    )(cvec8, ada_w, ada_b.reshape(DEPTH, 1, n_out))


_KPE0 = Q_LORA + KV_LORA
_W_IN_MOVES = (
    [(0, 0, P_MLA)]
    + [(P_MLA + LANES - MLA_ROPE + 16 * j, _KPE0 + 16 * int(_ROPE_SWAP[16 * j] // 16), 16) for j in range(4)]
    + [(ZM_W, P_MLA, ZG_W), (ZM_W + ZG_W, P_MLA + ZG_W, 4 * HEADS), (ZM_W + ZG_W + LANES, P_MLA + P_GDN, P_RWKV)]
)
W_IN_PAD = ZM_W + ZG_W + LANES + ZR_W


def _inproj_kernel(x_ref, mod_ref, n1_ref, w_ref, zm_ref, zg_ref, zab_ref, zr_ref, w_s):
    @pl.when(pl.program_id(0) == 0)
    def _():
        w_s[...] = jnp.zeros(w_s.shape, BF16)
        for dst, src, width in _W_IN_MOVES:
            w_s[:, dst:dst + width] = w_ref[:, src:src + width].astype(BF16)

    m = mod_ref[0]
    sh = m[:, 0:D_MODEL]
    sc = m[:, D_MODEL:2 * D_MODEL]
    h = _rms(x_ref[...], n1_ref[...]) * (1.0 + sc) + sh
    z = _mm(h, w_s[...])
    o1 = ZM_W
    o2 = o1 + ZG_W
    o3 = o2 + LANES
    zm_ref[...] = z[:, :o1]
    zg_ref[...] = z[:, o1:o2]
    zab_ref[...] = z[:, o2:o3]
    zr_ref[...] = z[:, o3:]


def _inproj(x2d, mods, lw, layer, tm, seq_len, mod_base):
    n = x2d.shape[0]
    tiles_per_seq = seq_len // tm if mod_base else 1

    def mod_idx(i):
        return (layer, mod_base + i // tiles_per_seq if mod_base else 0, 0, 0)

    return pl.pallas_call(
        _inproj_kernel,
        grid=(n // tm,),
        in_specs=[
            pl.BlockSpec((tm, D_MODEL), lambda i: (i, 0)),
            pl.BlockSpec((None, 1, 1, 6 * D_MODEL), mod_idx),
            _layer_spec(lw["norm1"], layer),
            _layer_spec(lw["w_in"], layer, pipeline_mode=pl.Buffered(1)),
        ],
        out_specs=[
            pl.BlockSpec((tm, ZM_W), lambda i: (i, 0)),
            pl.BlockSpec((tm, ZG_W), lambda i: (i, 0)),
            pl.BlockSpec((tm, LANES), lambda i: (i, 0)),
            pl.BlockSpec((tm, ZR_W), lambda i: (i, 0)),
        ],
        out_shape=[
            jax.ShapeDtypeStruct((n, ZM_W), F32),
            jax.ShapeDtypeStruct((n, ZG_W), F32),
            jax.ShapeDtypeStruct((n, LANES), F32),
            jax.ShapeDtypeStruct((n, ZR_W), F32),
        ],
        scratch_shapes=[pltpu.VMEM((D_MODEL, W_IN_PAD), BF16)],
        compiler_params=pltpu.CompilerParams(dimension_semantics=("arbitrary",),
                                             vmem_limit_bytes=VMEM_LIMIT),
        name="inproj",
    )(x2d, mods, lw["norm1"], lw["w_in"])


def _mla_kernel(*refs, seq_len, tq, past, cached):
    if cached:
        (zm_ref, cckv_ref, ckpe_ref, rc_ref, rs_ref, qn_ref, wuq_ref, wuqs_ref, kvn_ref, wukv_ref,
         o_ref, ckv_ref, k_s, v_s) = refs
    else:
        (zm_ref, qn_ref, wuq_ref, kvn_ref, wukv_ref, o_ref, ckv_ref, k_s, v_s) = refs
    qi = pl.program_id(1)
    o_kpe = Q_LORA + KV_LORA

    @pl.when(qi == 0)
    def _():
        zm = zm_ref[...]
        ckv = _rms(zm[:, Q_LORA:o_kpe], kvn_ref[...])
        ckv_ref[...] = ckv
        kpe = zm[:, o_kpe:o_kpe + LANES]
        if cached:
            kpe = kpe * rc_ref[...] + zm[:, o_kpe + LANES:o_kpe + 2 * LANES] * rs_ref[...]
            kvc = _mm(cckv_ref[0], wukv_ref[...])
            kpc = ckpe_ref[0].astype(BF16)
        kv = _mm(ckv, wukv_ref[...])
        kpe = kpe.astype(BF16)
        for h in range(MLA_HEADS):
            c0 = h * QH_W
            if cached:
                k_s[0:past, c0:c0 + LANES] = kvc[:, c0:c0 + LANES].astype(BF16)
                k_s[0:past, c0 + LANES:c0 + QH_W] = kpc
                v_s[0:past, h * MLA_V:(h + 1) * MLA_V] = kvc[:, c0 + LANES:c0 + QH_W].astype(BF16)
            k_s[past:past + seq_len, c0:c0 + LANES] = kv[:, c0:c0 + LANES].astype(BF16)
            k_s[past:past + seq_len, c0 + LANES:c0 + QH_W] = kpe
            v_s[past:past + seq_len, h * MLA_V:(h + 1) * MLA_V] = kv[:, c0 + LANES:c0 + QH_W].astype(BF16)

    r0 = pl.multiple_of(qi * tq, tq)
    zq = zm_ref[pl.ds(r0, tq), :]
    cq = _rms(zq[:, :Q_LORA], qn_ref[...])
    q = _mm(cq, wuq_ref[...])
    if cached:
        qs = _mm(cq, wuqs_ref[...])
        ones = jnp.ones((tq, LANES), F32)
        zeros = jnp.zeros((tq, LANES), F32)
        qc = jnp.concatenate([ones, rc_ref[pl.ds(r0, tq), :]], axis=1)
        qsn = jnp.concatenate([zeros, rs_ref[pl.ds(r0, tq), :]], axis=1)
    for h in range(MLA_HEADS):
        c0 = h * QH_W
        qh = q[:, c0:c0 + QH_W]
        if cached:
            qh = qh * qc + qs[:, c0:c0 + QH_W] * qsn
        s = _mm_nt(qh, k_s[:, c0:c0 + QH_W]) * MLA_SCALE
        e = jnp.exp(s - jnp.max(s, axis=-1, keepdims=True))
        den = jnp.sum(e, axis=-1, keepdims=True)
        o_ref[:, h * MLA_V:(h + 1) * MLA_V] = _mm(e, v_s[:, h * MLA_V:(h + 1) * MLA_V]) / den


def _mla(zm, lw, layer, seq_len, tq, cache=None):
    n = zm.shape[0]
    nb = n // seq_len
    cached = cache is not None
    past = PAST_LEN if cached else 0
    tk = past + seq_len
    seq_spec = lambda w: pl.BlockSpec((seq_len, w), lambda b, q: (b, 0))
    lay = lambda a: _layer_spec(a, layer)
    if cached:
        cckv, ckpe, rc, rs = cache
        args = [zm, cckv, ckpe, rc, rs, lw["q_norm"], lw["w_uq"], lw["w_uq_sw"], lw["kv_norm"], lw["w_ukv"]]
        in_specs = [seq_spec(ZM_W),
                    pl.BlockSpec((1, None, past, KV_LORA), lambda b, q: (b, layer, 0, 0)),
                    pl.BlockSpec((1, None, past, LANES), lambda b, q: (b, layer, 0, 0)),
                    _const_spec(rc), _const_spec(rs)] + [lay(a) for a in args[5:]]
    else:
        args = [zm, lw["q_norm"], lw["w_uq"], lw["kv_norm"], lw["w_ukv"]]
        in_specs = [seq_spec(ZM_W)] + [lay(a) for a in args[1:]]
    return pl.pallas_call(
        functools.partial(_mla_kernel, seq_len=seq_len, tq=tq, past=past, cached=cached),
        grid=(nb, seq_len // tq),
        in_specs=in_specs,
        out_specs=[
            pl.BlockSpec((tq, MLA_HEADS * MLA_V), lambda b, q: (b * (seq_len // tq) + q, 0)),
            pl.BlockSpec((seq_len, KV_LORA), lambda b, q: (b, 0)),
        ],
        out_shape=[
            jax.ShapeDtypeStruct((n, MLA_HEADS * MLA_V), F32),
            jax.ShapeDtypeStruct((n, KV_LORA), F32),
        ],
        scratch_shapes=[
            pltpu.VMEM((tk, MLA_HEADS * QH_W), BF16),
            pltpu.VMEM((tk, MLA_HEADS * MLA_V), BF16),
        ],
        compiler_params=pltpu.CompilerParams(dimension_semantics=("arbitrary", "arbitrary"),
                                             vmem_limit_bytes=VMEM_LIMIT),
        name="mla_lat" if cached else "mla_ctx",
    )(*args)


CHUNK_GROUP = 4
GDN_PRE_W = 5 * MIX_W


def _for_chunk_groups(n_chunks, fn):
    if n_chunks == CHUNK_GROUP:
        fn(0)
    else:
        def body(gi, carry):
            fn(gi * CHUNK_GROUP)
            return carry
        lax.fori_loop(0, n_chunks // CHUNK_GROUP, body, 0)


def _gdn_prepare(items, maskbd):
    n = range(len(items))
    qs, ks, vs, gs, betas, masks, revs = zip(*items)
    gcs = [_mm_sel_l(masks[i][2], gs[i]) for i in n]
    decays = []
    for i in n:
        inc, eye = masks[i][0], masks[i][3]
        gc_row = jnp.sum(eye * gcs[i], axis=0, keepdims=True)
        decays.append(jnp.where(inc, jnp.exp(jnp.where(inc, gcs[i] - gc_row, 0.0)), 0.0))
    kbs = [ks[i] * betas[i] for i in n]
    aqs = [_mm_nt(jnp.concatenate([kbs[i], qs[i]], axis=0), _bd(ks[i], maskbd)) for i in n]
    a_mats = [jnp.where(masks[i][1], aqs[i][:CHUNK] * decays[i], 0.0) for i in n]
    t_invs = _neumann_inverse(a_mats, [m[3] for m in masks], maskbd)
    egcs = [jnp.exp(gc) for gc in gcs]
    uws = [_mm(t_invs[i], jnp.concatenate([_bd(vs[i] * betas[i], maskbd), _bd(kbs[i] * egcs[i], maskbd)], axis=1))
           for i in n]
    out = []
    for i in n:
        g_last = gcs[i][0:1] if revs[i] else gcs[i][CHUNK - 1:CHUNK]
        pre = jnp.concatenate([uws[i], qs[i] * egcs[i], aqs[i][CHUNK:] * decays[i],
                               ks[i] * jnp.exp(g_last - gcs[i])], axis=1)
        out.append((pre, jnp.broadcast_to(jnp.exp(g_last), (8, MIX_W))))
    return out


def _gdn_step(pre, egl, s_bd, maskbd):
    w = MIX_W
    u, wq_in, qk, kdec = pre[:, :w], pre[:, w:3 * w], pre[:, 3 * w:4 * w], pre[:, 4 * w:]
    wq = _mm(jnp.concatenate([wq_in[:, :w], wq_in[:, w:]], axis=0), s_bd)
    v_new = u - wq[:CHUNK]
    o = wq[CHUNK:] + _mm(qk, _bd(v_new, maskbd))
    s_new = s_bd * egl + _mm_tn(kdec, v_new) * maskbd.astype(F32)
    return o, s_new


def _gdn_kernel(*refs, seq_len, cached):
    if cached:
        (zg_ref, zab_ref, s0_ref, conv_ref, alog_ref, dtb_ref, gn_ref, eexp_ref, ones_ref, maskbd_ref,
         o_ref, sout_ref, q_s, k_s, v_s, ge_s, pre_s, gl_s, oacc_s, st_s) = refs
    else:
        (zg_ref, zab_ref, conv_ref, alog_ref, dtb_ref, gn_ref, eexp_ref, ones_ref, maskbd_ref,
         o_ref, sout_ref, q_s, k_s, v_s, ge_s, pre_s, gl_s, oacc_s, st_s) = refs
    t = seq_len
    z = zg_ref[:, :GDN_CONV_CH]
    rowi = _iota((t, 1), 0)
    zp = jnp.where(rowi == 0, 0.0, pltpu.roll(z, 1, 0))
    zn = jnp.where(rowi == t - 1, 0.0, pltpu.roll(z, t - 1, 0))
    cw = conv_ref[...]
    qkv = _silu(zp * cw[0:1] + z * cw[1:2] + zn * cw[2:3])
    ones_bd = ones_ref[...]
    q = qkv[:, :MIX_W]
    k = qkv[:, MIX_W:2 * MIX_W]
    q_s[...] = q * lax.rsqrt(_mm_sel_r(q * q, ones_bd) + 1e-6) * (HEAD_DIM ** -0.5)
    k_s[...] = k * lax.rsqrt(_mm_sel_r(k * k, ones_bd) + 1e-6)
    v_s[...] = qkv[:, 2 * MIX_W:]
    ab = zab_ref[...]
    lane = _iota((t, LANES), 1)
    gb = jnp.where(lane < 2 * HEADS, -jnp.exp(alog_ref[...]) * _softplus(ab + dtb_ref[...]), _sigmoid(ab))
    ge_s[...] = _mm_sel_r(gb, eexp_ref[...])
    oacc_s[...] = jnp.zeros((t, MIX_W), F32)
    if cached:
        st_s[...] = s0_ref[0]
    else:
        st_s[...] = jnp.zeros((2, MIX_W, MIX_W), F32)
    maskbd = maskbd_ref[...]
    masks = (_chunk_masks(False), _chunk_masks(True))
    n_chunks = t // CHUNK

    def prepare_group(c0):
        where, items = [], []
        for j in range(CHUNK_GROUP):
            c = c0 + j
            rows = pl.ds(pl.multiple_of(c * CHUNK, CHUNK), CHUNK)
            for d in range(2):
                where.append((d, c, rows))
                items.append((q_s[rows, :], k_s[rows, :], v_s[rows, :], ge_s[rows, d * MIX_W:(d + 1) * MIX_W],
                              ge_s[rows, (2 + d) * MIX_W:(3 + d) * MIX_W], masks[d], d == 1))
        for (d, c, rows), (pre, egl) in zip(where, _gdn_prepare(items, maskbd)):
            pre_s[d, rows, :] = pre
            gl_s[d, pl.ds(pl.multiple_of(c * 8, 8), 8), :] = egl

    _for_chunk_groups(n_chunks, prepare_group)

    def body(i, carry):
        for d in range(2):
            c = i if d == 0 else n_chunks - 1 - i
            rows = pl.ds(pl.multiple_of(c * CHUNK, CHUNK), CHUNK)
            egl = gl_s[d, pl.ds(pl.multiple_of(c * 8, 8), 8), :][0:1]
            o, s_new = _gdn_step(pre_s[d, rows, :], egl, st_s[d], maskbd)
            oacc_s[rows, :] = oacc_s[rows, :] + o
            st_s[d] = s_new
        return carry

    lax.fori_loop(0, n_chunks, body, 0)
    o = oacc_s[...]
    ms = _mm_sel_r(o * o, ones_bd) * (1.0 / HEAD_DIM)
    gate = zg_ref[:, GDN_CONV_CH:]
    o_ref[...] = o * lax.rsqrt(ms + NORM_EPS) * gn_ref[...] * _silu(gate)
    for d in range(2):
        for h in range(HEADS):
            sl = slice(h * HEAD_DIM, (h + 1) * HEAD_DIM)
            sout_ref[0, d, h] = st_s[d, sl, sl]


def _gdn(zg, zab, lw, layer, consts, seq_len, s0_bd=None):
    n = zg.shape[0]
    nb = n // seq_len
    cached = s0_bd is not None
    args = [zg, zab]
    in_specs = [pl.BlockSpec((seq_len, ZG_W), lambda b: (b, 0)),
                pl.BlockSpec((seq_len, LANES), lambda b: (b, 0))]
    if cached:
        args.append(s0_bd)
        in_specs.append(pl.BlockSpec((1, None, 2, MIX_W, MIX_W), lambda b: (b, layer, 0, 0, 0)))
    layered = [lw["gdn_conv"], lw["gdn_alog"], lw["gdn_dtb"], lw["gdn_norm"]]
    const = [consts["eexp"], consts["ones_bd"], consts["maskbd"]]
    args += layered + const
    in_specs += [_layer_spec(a, layer) for a in layered] + [_const_spec(a) for a in const]
    return pl.pallas_call(
        functools.partial(_gdn_kernel, seq_len=seq_len, cached=cached),
        grid=(nb,),
        in_specs=in_specs,
        out_specs=[
            pl.BlockSpec((seq_len, MIX_W), lambda b: (b, 0)),
            pl.BlockSpec((1, 2, HEADS, HEAD_DIM, HEAD_DIM), lambda b: (b, 0, 0, 0, 0)),
        ],
        out_shape=[
            jax.ShapeDtypeStruct((n, MIX_W), F32),
            jax.ShapeDtypeStruct((nb, 2, HEADS, HEAD_DIM, HEAD_DIM), F32),
        ],
        scratch_shapes=[
            pltpu.VMEM((seq_len, MIX_W), F32),
            pltpu.VMEM((seq_len, MIX_W), F32),
            pltpu.VMEM((seq_len, MIX_W), F32),
            pltpu.VMEM((seq_len, 4 * MIX_W), F32),
            pltpu.VMEM((2, seq_len, GDN_PRE_W), F32),
            pltpu.VMEM((2, seq_len // CHUNK * 8, MIX_W), F32),
            pltpu.VMEM((seq_len, MIX_W), F32),
            pltpu.VMEM((2, MIX_W, MIX_W), F32),
        ],
        compiler_params=pltpu.CompilerParams(dimension_semantics=("arbitrary",),
                                             vmem_limit_bytes=VMEM_LIMIT),
        name="gdn_lat" if cached else "gdn_ctx",
    )(*args)


RWKV_PRE_W = 7 * MIX_W


def _rwkv_prepare(items, maskbd, eye_full):
    n = range(len(items))
    rs, kds, vs, kks, bs, lws, masks, revs = zip(*items)
    cums = [_mm_sel_l(masks[i][2], lws[i]) for i in n]
    einvs = [jnp.exp(-c) for c in cums]
    kts = [kks[i] * jnp.exp(cums[i] - lws[i]) for i in n]
    rts = [rs[i] * jnp.exp(cums[i]) for i in n]
    krs = [jnp.concatenate([kts[i], rts[i]], axis=0) for i in n]
    lb_alls = [_mm_nt(krs[i], _bd(bs[i] * einvs[i], maskbd)) for i in n]
    lk_alls = [_mm_nt(krs[i], _bd(kds[i] * einvs[i], maskbd)) for i in n]
    lbs = [jnp.where(masks[i][1], lb_alls[i][:CHUNK], 0.0) for i in n]
    t_invs = _neumann_inverse(lbs, [m[3] for m in masks], maskbd)
    lvs = [_mm(jnp.concatenate([jnp.where(masks[i][1], lk_alls[i][:CHUNK], 0.0),
                                jnp.where(masks[i][0], lk_alls[i][CHUNK:], 0.0)], axis=0), _bd(vs[i], maskbd))
           for i in n]
    tkps = [_mm(t_invs[i], jnp.concatenate([_bd(kts[i], maskbd), _bd(lvs[i][:CHUNK], maskbd)], axis=1)) for i in n]
    out = []
    for i in n:
        c_last = cums[i][0:1] if revs[i] else cums[i][CHUNK - 1:CHUNK]
        tail = jnp.exp(c_last - cums[i])
        rb = jnp.where(masks[i][0], lb_alls[i][CHUNK:], 0.0)
        pre = jnp.concatenate([tkps[i][:, :MIX_W], rts[i], tkps[i][:, MIX_W:], lvs[i][CHUNK:], rb,
                               kds[i] * tail, bs[i] * tail], axis=1)
        gcol = jnp.sum(eye_full * jnp.exp(c_last), axis=1, keepdims=True)
        out.append((pre, jnp.broadcast_to(gcol, (MIX_W, LANES))))
    return out


def _rwkv_step(pre, v, gcol, z_bd, maskbd):
    w = MIX_W
    pr = _mm(jnp.concatenate([pre[:, :w], pre[:, w:2 * w]], axis=0), z_bd)
    p = pr[:CHUNK] + pre[:, 2 * w:3 * w]
    o = pr[CHUNK:] + pre[:, 3 * w:4 * w] - _mm(pre[:, 4 * w:5 * w], _bd(p, maskbd))
    upd = _mm_tn(jnp.concatenate([pre[:, 5 * w:6 * w], pre[:, 6 * w:]], axis=0), jnp.concatenate([v, -p], axis=0))
    z_new = z_bd * jnp.concatenate([gcol, gcol], axis=1) + upd * maskbd.astype(F32)
    return o, z_new


def _rwkv_kernel(*refs, seq_len, cached):
    if cached:
        (zr_ref, s0_ref, mup_ref, mun_ref, w0_ref, w2_ref, a0_ref, a2_ref, g2_ref, kk_ref, ka_ref, rk_ref,
         gnw_ref, gnb_ref, ones_ref, maskbd_ref, o_ref, sout_ref,
         r_s, v_s, kk_s, dir_s, bg_s, pre_s, gcol_s, oacc_s, st_s) = refs
    else:
        (zr_ref, mup_ref, mun_ref, w0_ref, w2_ref, a0_ref, a2_ref, g2_ref, kk_ref, ka_ref, rk_ref,
         gnw_ref, gnb_ref, ones_ref, maskbd_ref, o_ref, sout_ref,
         r_s, v_s, kk_s, dir_s, bg_s, pre_s, gcol_s, oacc_s, st_s) = refs
    t = seq_len
    z = zr_ref[...]
    rowi = _iota((t, 1), 0)
    zp = jnp.where(rowi == 0, 0.0, pltpu.roll(z, 1, 0))
    zn = jnp.where(rowi == t - 1, 0.0, pltpu.roll(z, t - 1, 0))
    z = z + mup_ref[...] * (zp - z) + mun_ref[...] * (zn - z)
    w = MIX_W
    r = z[:, :w]
    k = z[:, w:2 * w]
    v = z[:, 2 * w:3 * w]
    wd = jnp.tanh(z[:, 3 * w:3 * w + LANES])
    ad = z[:, 3 * w + LANES:3 * w + 2 * LANES]
    gd = _sigmoid(z[:, 3 * w + 2 * LANES:])
    ones_bd = ones_ref[...]
    kk = k * kk_ref[...]
    kk = kk * lax.rsqrt(_mm_sel_r(kk * kk, ones_bd) + 1e-6)
    r_s[...] = r
    v_s[...] = v
    kk_s[...] = kk
    bonus = jnp.zeros((t, w), F32)
    for d in range(2):
        w_log = -_softplus(-(w0_ref[d:d + 1] + _mm(wd, w2_ref[d]))) - 0.5
        a = _sigmoid(a0_ref[d:d + 1] + _mm(ad, a2_ref[d]))
        kd = k * (1.0 + (a - 1.0) * ka_ref[...])
        dir_s[:, (3 * d) * w:(3 * d + 1) * w] = -jnp.exp(w_log)
        dir_s[:, (3 * d + 1) * w:(3 * d + 2) * w] = kd
        dir_s[:, (3 * d + 2) * w:(3 * d + 3) * w] = kk * a
        bonus = bonus + _mm_sel_r(r * kd * rk_ref[...], ones_bd) * v
    bg_s[:, :w] = bonus
    bg_s[:, w:] = _mm(gd, g2_ref[...])
    oacc_s[...] = jnp.zeros((t, w), F32)
    if cached:
        st_s[...] = s0_ref[0]
    else:
        st_s[...] = jnp.zeros((2, w, w), F32)
    maskbd = maskbd_ref[...]
    masks = (_chunk_masks(False), _chunk_masks(True))
    eye_full = jnp.where(_iota((w, w), 0) == _iota((w, w), 1), 1.0, 0.0).astype(F32)
    n_chunks = t // CHUNK

    def prepare_group(c0):
        where, items = [], []
        for j in range(CHUNK_GROUP):
            c = c0 + j
            rows = pl.ds(pl.multiple_of(c * CHUNK, CHUNK), CHUNK)
            for d in range(2):
                where.append((d, c, rows))
                items.append((r_s[rows, :], dir_s[rows, (3 * d + 1) * w:(3 * d + 2) * w], v_s[rows, :], kk_s[rows, :],
                              dir_s[rows, (3 * d + 2) * w:(3 * d + 3) * w], dir_s[rows, (3 * d) * w:(3 * d + 1) * w],
                              masks[d], d == 1))
        for (d, c, rows), (pre, gcol) in zip(where, _rwkv_prepare(items, maskbd, eye_full)):
            pre_s[d, rows, :] = pre
            gcol_s[d, pl.ds(pl.multiple_of(c * w, w), w), :] = gcol

    _for_chunk_groups(n_chunks, prepare_group)

    def body(i, carry):
        for d in range(2):
            c = i if d == 0 else n_chunks - 1 - i
            rows = pl.ds(pl.multiple_of(c * CHUNK, CHUNK), CHUNK)
            gcol = gcol_s[d, pl.ds(pl.multiple_of(c * w, w), w), :]
            o, z_new = _rwkv_step(pre_s[d, rows, :], v_s[rows, :], gcol, st_s[d], maskbd)
            oacc_s[rows, :] = oacc_s[rows, :] + o
            st_s[d] = z_new
        return carry

    lax.fori_loop(0, n_chunks, body, 0)
    o = oacc_s[...]
    inv_n = 1.0 / HEAD_DIM
    mu = _mm_sel_r(o, ones_bd) * inv_n
    oc = o - mu
    var = _mm_sel_r(oc * oc, ones_bd) * inv_n
    y = oc * lax.rsqrt(var + RWKV_GN_EPS) * gnw_ref[...] + gnb_ref[...]
    o_ref[...] = (y + bg_s[:, :w]) * bg_s[:, w:]
    for d in range(2):
        for h in range(HEADS):
            sl = slice(h * HEAD_DIM, (h + 1) * HEAD_DIM)
            sout_ref[0, d, h] = st_s[d, sl, sl]


def _rwkv(zr, lw, layer, consts, seq_len, s0_bd=None):
    n = zr.shape[0]
    nb = n // seq_len
    cached = s0_bd is not None
    args = [zr]
    in_specs = [pl.BlockSpec((seq_len, ZR_W), lambda b: (b, 0))]
    if cached:
        args.append(s0_bd)
        in_specs.append(pl.BlockSpec((1, None, 2, MIX_W, MIX_W), lambda b: (b, layer, 0, 0, 0)))
    layered = [lw["rwkv_mu_prev"], lw["rwkv_mu_next"], lw["rwkv_w0"], lw["rwkv_w2"], lw["rwkv_a0"], lw["rwkv_a2"],
               lw["rwkv_g2"], lw["rwkv_k_k"], lw["rwkv_k_a"], lw["rwkv_r_k"], lw["rwkv_gn_w"], lw["rwkv_gn_b"]]
    const = [consts["ones_bd"], consts["maskbd"]]
    args += layered + const
    in_specs += [_layer_spec(a, layer) for a in layered] + [_const_spec(a) for a in const]
    return pl.pallas_call(
        functools.partial(_rwkv_kernel, seq_len=seq_len, cached=cached),
        grid=(nb,),
        in_specs=in_specs,
        out_specs=[
            pl.BlockSpec((seq_len, MIX_W), lambda b: (b, 0)),
            pl.BlockSpec((1, 2, HEADS, HEAD_DIM, HEAD_DIM), lambda b: (b, 0, 0, 0, 0)),
        ],
        out_shape=[
            jax.ShapeDtypeStruct((n, MIX_W), F32),
            jax.ShapeDtypeStruct((nb, 2, HEADS, HEAD_DIM, HEAD_DIM), F32),
        ],
        scratch_shapes=[
            pltpu.VMEM((seq_len, MIX_W), F32),
            pltpu.VMEM((seq_len, MIX_W), F32),
            pltpu.VMEM((seq_len, MIX_W), F32),
            pltpu.VMEM((seq_len, 6 * MIX_W), F32),
            pltpu.VMEM((seq_len, 2 * MIX_W), F32),
            pltpu.VMEM((2, seq_len, RWKV_PRE_W), F32),
            pltpu.VMEM((2, seq_len // CHUNK * MIX_W, LANES), F32),
            pltpu.VMEM((seq_len, MIX_W), F32),
            pltpu.VMEM((2, MIX_W, MIX_W), F32),
        ],
        compiler_params=pltpu.CompilerParams(dimension_semantics=("arbitrary",),
                                             vmem_limit_bytes=VMEM_LIMIT),
        name="rwkv_lat" if cached else "rwkv_ctx",
    )(*args)


def _route(logits_t, bias):
    tm = logits_t.shape[1]
    neg = -jnp.inf
    sc = _sigmoid(logits_t)
    sc3 = sc.reshape(N_GROUPS, GROUP_SIZE, tm)
    sel = (sc + bias).reshape(N_GROUPS, GROUP_SIZE, tm)
    si = _iota(sel.shape, 1).astype(F32)
    m1 = jnp.max(sel, axis=1, keepdims=True)
    f1 = jnp.min(jnp.where(sel == m1, si, float(GROUP_SIZE)), axis=1, keepdims=True)
    m2 = jnp.max(jnp.where(si == f1, neg, sel), axis=1, keepdims=True)
    grp = m1 + m2
    gi = _iota(grp.shape, 0).astype(F32)
    gsel = jnp.zeros(grp.shape, F32)
    for _ in range(TOPK_GROUPS):
        mx = jnp.max(grp, axis=0, keepdims=True)
        fi = jnp.min(jnp.where(grp == mx, gi, float(N_GROUPS)), axis=0, keepdims=True)
        hit = gi == fi
        gsel = jnp.where(hit, 1.0, gsel)
        grp = jnp.where(hit, neg, grp)
    cur = jnp.where(gsel > 0.0, sel, neg)
    ei = (_iota(cur.shape, 0) * GROUP_SIZE + _iota(cur.shape, 1)).astype(F32)
    chosen = jnp.zeros(cur.shape, F32)
    ids, wts = [], []
    for _ in range(TOP_K):
        mx = jnp.max(jnp.max(cur, axis=0, keepdims=True), axis=1, keepdims=True)
        fi = jnp.min(jnp.min(jnp.where(cur == mx, ei, float(N_EXPERTS)), axis=0, keepdims=True),
                     axis=1, keepdims=True)
        hit = ei == fi
        chosen = jnp.where(hit, 1.0, chosen)
        cur = jnp.where(hit, neg, cur)
        ids.append(fi.reshape(1, tm))
        wts.append(jnp.sum(jnp.sum(jnp.where(hit, sc3, 0.0), axis=0, keepdims=True), axis=1, keepdims=True)
                   .reshape(1, tm))
    w = jnp.concatenate(wts, axis=0)
    w = w / jnp.sum(w, axis=0, keepdims=True) * ROUTE_SCALE
    return chosen.reshape(N_EXPERTS, tm), jnp.concatenate(ids, axis=0), w


def _pack_halves(x):
    half = x.shape[1] // 2
    bits = lax.bitcast_convert_type(x.astype(BF16).astype(F32), jnp.int32)
    lo = lax.shift_right_logical(bits[:, :half], jnp.int32(16))
    return jnp.bitwise_or(lo, jnp.bitwise_and(bits[:, half:], jnp.int32(-65536)))


def _unpack_halves(word):
    lo = lax.bitcast_convert_type(lax.shift_left(word, jnp.int32(16)), F32)
    hi = lax.bitcast_convert_type(jnp.bitwise_and(word, jnp.int32(-65536)), F32)
    return lo, hi


def _post_kernel(x_ref, om_ref, og_ref, or_ref, mod_ref, wo_ref, n2_ref, rt_ref, rb_ref, tri_ref, cin_ref,
                 x1_ref, h2_ref, eid_ref, rank_ref, ew_ref, cnt_ref, carry_s):
    @pl.when(pl.program_id(0) == 0)
    def _():
        carry_s[...] = cin_ref[...]

    m = mod_ref[0]
    g1 = m[:, 2 * D_MODEL:3 * D_MODEL]
    sh2 = m[:, 3 * D_MODEL:4 * D_MODEL]
    sc2 = m[:, 4 * D_MODEL:5 * D_MODEL]
    w_mla = MLA_HEADS * MLA_V
    mix = (_mm(om_ref[...], wo_ref[0:w_mla, :]) + _mm(og_ref[...], wo_ref[w_mla:w_mla + MIX_W, :])
           + _mm(or_ref[...], wo_ref[w_mla + MIX_W:, :]))
    x1 = x_ref[...] + g1 * mix
    x1_ref[...] = x1
    h2 = _rms(x1, n2_ref[...]) * (1.0 + sc2) + sh2
    h2_ref[...] = _pack_halves(h2)
    r_hi, r_lo = _split2(rt_ref[...])
    h_hi, h_lo = _split2(h2)
    logits_t = _mm_nt(r_hi, h_hi) + _mm_nt(r_hi, h_lo) + _mm_nt(r_lo, h_hi)
    chosen, ids, w = _route(logits_t, rb_ref[...])
    tm = chosen.shape[1]
    rank_et = (carry_s[:, 0:1] + _mm(chosen, tri_ref[...])).reshape(N_GROUPS, GROUP_SIZE, tm)
    ei = (_iota(rank_et.shape, 0) * GROUP_SIZE + _iota(rank_et.shape, 1)).astype(F32)
    ranks = []
    for k in range(TOP_K):
        pick = jnp.where(ei == ids[k:k + 1].reshape(1, 1, tm), rank_et, 0.0)
        ranks.append(jnp.sum(jnp.sum(pick, axis=0, keepdims=True), axis=1, keepdims=True).reshape(1, tm))
    eid_ref[...] = ids.astype(jnp.int32)
    rank_ref[...] = jnp.concatenate(ranks, axis=0).astype(jnp.int32)
    ew_ref[...] = jnp.concatenate([w, jnp.zeros((LANES - TOP_K, tm), F32)], axis=0).T
    total = carry_s[...] + jnp.sum(chosen, axis=1, keepdims=True)
    carry_s[...] = total
    cnt_ref[...] = total


def _post(x2d, om, og, orw, mods, lw, layer, consts, tm, seq_len, mod_base, counts_in):
    n = x2d.shape[0]
    tiles_per_seq = seq_len // tm if mod_base else 1

    def mod_idx(i):
        return (layer, mod_base + i // tiles_per_seq if mod_base else 0, 0, 0)

    row = lambda w: pl.BlockSpec((tm, w), lambda i: (i, 0))
    col = lambda h: pl.BlockSpec((h, tm), lambda i: (0, i))
    full = lambda a: _layer_spec(a, layer)
    tail = [lw["w_out"], lw["norm2"], lw["router_t"], lw["router_b"]]
    tri = consts["tri_tokens"]
    return pl.pallas_call(
        _post_kernel,
        grid=(n // tm,),
        in_specs=[row(D_MODEL), row(MLA_HEADS * MLA_V), row(MIX_W), row(MIX_W),
                  pl.BlockSpec((None, 1, 1, 6 * D_MODEL), mod_idx)] + [full(a) for a in tail]
        + [_const_spec(tri), _const_spec(counts_in)],
        out_specs=[row(D_MODEL), row(D_MODEL // 2), col(TOP_K), col(TOP_K), row(LANES),
                   pl.BlockSpec((N_EXPERTS, LANES), lambda i: (0, 0))],
        out_shape=[
            jax.ShapeDtypeStruct((n, D_MODEL), F32),
            jax.ShapeDtypeStruct((n, D_MODEL // 2), jnp.int32),
            jax.ShapeDtypeStruct((TOP_K, n), jnp.int32),
            jax.ShapeDtypeStruct((TOP_K, n), jnp.int32),
            jax.ShapeDtypeStruct((n, LANES), F32),
            jax.ShapeDtypeStruct((N_EXPERTS, LANES), F32),
        ],
        scratch_shapes=[pltpu.VMEM((N_EXPERTS, LANES), F32)],
        compiler_params=pltpu.CompilerParams(dimension_semantics=("arbitrary",),
                                             vmem_limit_bytes=VMEM_LIMIT),
        name="post",
    )(x2d, om, og, orw, mods, *tail, tri, counts_in)


MOE_ROWS = 512
SC_ROWS = 128
SC_SUBCORES = 32


def _swiglu_act(gu):
    return _silu(gu[:, :D_EXPERT]) * gu[:, D_EXPERT:]


def _dispatch_plan(eid, rank, counts, n, rows):
    n_blocks = n * TOP_K // rows + N_EXPERTS
    cnt = counts[:, 0].astype(jnp.int32)
    blocks = (cnt + rows - 1) // rows
    block_end = jnp.cumsum(blocks)
    offset = (block_end - blocks) * rows
    experts = jnp.arange(N_EXPERTS, dtype=jnp.int32)
    dest = jnp.sum(jnp.where(eid[..., None] == experts, offset, 0), axis=-1) + rank
    block_ids = jnp.arange(n_blocks, dtype=jnp.int32)
    block_expert = jnp.minimum(jnp.sum((block_end[None, :] <= block_ids[:, None]).astype(jnp.int32), axis=1),
                               N_EXPERTS - 1)
    return dest, block_expert, block_end[-1:].astype(jnp.int32), n_blocks


def _sc_mesh():
    return plsc.VectorSubcoreMesh(core_axis_name="core", subcore_axis_name="subcore")


def _sc_dispatch(x, dest, n_rows):
    n, w = x.shape
    n_chunks = n // SC_ROWS

    @functools.partial(pl.kernel, out_type=jax.ShapeDtypeStruct((n_rows, w), x.dtype), mesh=_sc_mesh(),
                       scratch_types=[pltpu.VMEM((SC_ROWS, w), x.dtype), pltpu.VMEM((TOP_K, SC_ROWS), jnp.int32)])
    def kern(x_hbm, d_hbm, o_hbm, xv, dv):
        sid = lax.axis_index("core") * (SC_SUBCORES // 2) + lax.axis_index("subcore")

        @pl.loop(sid, n_chunks, step=SC_SUBCORES)
        def _(c):
            r0 = pl.multiple_of(c * SC_ROWS, SC_ROWS)
            pltpu.sync_copy(x_hbm.at[pl.ds(r0, SC_ROWS)], xv)
            pltpu.sync_copy(d_hbm.at[:, pl.ds(r0, SC_ROWS)], dv)
            for k in range(TOP_K):
                pltpu.sync_copy(xv, o_hbm.at[dv.at[k]])

    return kern(x, dest)


def _sc_gather(y, idx):
    w = y.shape[1]
    n_chunks = idx.shape[0]

    @functools.partial(pl.kernel, out_type=jax.ShapeDtypeStruct((n_chunks * SC_ROWS, w), y.dtype), mesh=_sc_mesh(),
                       scratch_types=[pltpu.VMEM((SC_ROWS, w), y.dtype), pltpu.VMEM((1, SC_ROWS), jnp.int32)])
    def kern(y_hbm, i_hbm, o_hbm, ov, iv):
        sid = lax.axis_index("core") * (SC_SUBCORES // 2) + lax.axis_index("subcore")

        @pl.loop(sid, n_chunks, step=SC_SUBCORES)
        def _(c):
            pltpu.sync_copy(i_hbm.at[pl.ds(c, 1)], iv)
            pltpu.sync_copy(y_hbm.at[iv.at[0]], ov)
            pltpu.sync_copy(ov, o_hbm.at[pl.ds(pl.multiple_of(c * SC_ROWS, SC_ROWS), SC_ROWS)])

    return kern(y, idx)


def _moe_rows_kernel(be_ref, nu_ref, x_ref, wgu_ref, wdn_ref, y_ref):
    @pl.when(pl.program_id(0) < nu_ref[0])
    def _():
        half = D_MODEL // 2
        lo, hi = _unpack_halves(x_ref[...])
        gu = _mm(lo, wgu_ref[0:half, :]) + _mm(hi, wgu_ref[half:, :])
        y_ref[...] = _pack_halves(_mm(_swiglu_act(gu), wdn_ref[...]))


def _moe_rows(xs, block_expert, n_used, lw, layer, n_blocks, rows):
    half = D_MODEL // 2
    last = lambda b, be, nu: jnp.minimum(b, nu[0] - 1)
    return pl.pallas_call(
        _moe_rows_kernel,
        grid_spec=pltpu.PrefetchScalarGridSpec(
            num_scalar_prefetch=2,
            grid=(n_blocks,),
            in_specs=[pl.BlockSpec((rows, half), lambda b, be, nu: (last(b, be, nu), 0)),
                      pl.BlockSpec((None, None, D_MODEL, 2 * D_EXPERT),
                                   lambda b, be, nu: (layer, be[last(b, be, nu)], 0, 0)),
                      pl.BlockSpec((None, None, D_EXPERT, D_MODEL),
                                   lambda b, be, nu: (layer, be[last(b, be, nu)], 0, 0))],
            out_specs=pl.BlockSpec((rows, half), lambda b, be, nu: (last(b, be, nu), 0)),
        ),
        out_shape=jax.ShapeDtypeStruct(xs.shape, jnp.int32),
        compiler_params=pltpu.CompilerParams(dimension_semantics=("arbitrary",),
                                             vmem_limit_bytes=VMEM_LIMIT),
        name="moe_rows",
    )(block_expert, n_used, xs, lw["moe_w_gu"], lw["moe_w_down"])


def _moe_combine_kernel(yg_ref, ew_ref, h_ref, x1_ref, mod_ref, sgu_ref, sdn_ref, nf_ref, o_ref, *, final):
    half = D_MODEL // 2
    lo, hi = _unpack_halves(h_ref[...])
    gu = _mm(lo, sgu_ref[0:half, :]) + _mm(hi, sgu_ref[half:, :])
    acc = _mm(_swiglu_act(gu), sdn_ref[...])
    ew = ew_ref[...]
    acc_lo = acc[:, :half]
    acc_hi = acc[:, half:]
    for k in range(TOP_K):
        lo, hi = _unpack_halves(yg_ref[k])
        wk = ew[:, k:k + 1]
        acc_lo = acc_lo + wk * lo
        acc_hi = acc_hi + wk * hi
    g2 = mod_ref[0][:, 5 * D_MODEL:]
    x2 = x1_ref[...] + g2 * jnp.concatenate([acc_lo, acc_hi], axis=1)
    if final:
        x2 = _rms(x2, nf_ref[...])
    o_ref[...] = x2


def _moe_combine(yg, row0, ew, h2p, x1, mods, lw, layer, norm_f, tm, seq_len, mod_base, final):
    n = x1.shape[0]
    half = D_MODEL // 2
    tiles_per_seq = seq_len // tm if mod_base else 1
    tile0 = row0 // tm

    def mod_idx(i):
        return (layer, mod_base + i // tiles_per_seq if mod_base else 0, 0, 0)

    row = lambda w: pl.BlockSpec((tm, w), lambda i: (i, 0))
    return pl.pallas_call(
        functools.partial(_moe_combine_kernel, final=final),
        grid=(n // tm,),
        in_specs=[pl.BlockSpec((TOP_K, tm, half), lambda i: (0, tile0 + i, 0)), row(LANES), row(half), row(D_MODEL),
                  pl.BlockSpec((None, 1, 1, 6 * D_MODEL), mod_idx),
                  _layer_spec(lw["shared_w_gu"], layer), _layer_spec(lw["shared_w_down"], layer),
                  _const_spec(norm_f)],
        out_specs=row(D_MODEL),
        out_shape=jax.ShapeDtypeStruct((n, D_MODEL), F32),
        compiler_params=pltpu.CompilerParams(dimension_semantics=("arbitrary",),
                                             vmem_limit_bytes=VMEM_LIMIT),
        name="moe_combine_final" if final else "moe_combine",
    )(yg, ew, h2p, x1, mods, lw["shared_w_gu"], lw["shared_w_down"], norm_f)


def _moe_experts(h2p, eid, rank, counts, lw, layer):
    n = h2p.shape[0]
    dest, block_expert, n_used, n_blocks = _dispatch_plan(eid, rank, counts, n, MOE_ROWS)
    xs = _sc_dispatch(h2p, dest, n_blocks * MOE_ROWS)
    y = _moe_rows(xs, block_expert, n_used, lw, layer, n_blocks, MOE_ROWS)
    return _sc_gather(y, dest.reshape(n * TOP_K // SC_ROWS, SC_ROWS)).reshape(TOP_K, n, D_MODEL // 2)


def _constants():
    idx = np.arange(MIX_W)
    same_head = (idx[:, None] // HEAD_DIM) == (idx[None, :] // HEAD_DIM)
    maskbd = jnp.asarray(same_head, BF16)
    eexp = np.zeros((LANES, 4 * MIX_W), np.float32)
    for blk in range(4):
        kind, d = divmod(blk, 2)
        for h in range(HEADS):
            src = kind * 2 * HEADS + d * HEADS + h
            eexp[src, blk * MIX_W + h * HEAD_DIM: blk * MIX_W + (h + 1) * HEAD_DIM] = 1.0
    tri = np.triu(np.ones((POST_TM, POST_TM), np.float32), 1)
    return {"maskbd": maskbd, "ones_bd": maskbd, "eexp": jnp.asarray(eexp, BF16), "tri_tokens": jnp.asarray(tri, BF16)}


def _rope_tables(n):
    rows = n // GRID_W
    row = jnp.repeat(jnp.arange(rows, dtype=F32), GRID_W)
    col = jnp.tile(jnp.arange(GRID_W, dtype=F32), rows)
    axis_dim = MLA_ROPE // 2
    inv = jnp.power(ROPE_BASE, -jnp.arange(0, axis_dim, 2, dtype=F32) / axis_dim)
    ang_r = row[:, None] * inv
    ang_c = col[:, None] * inv
    cr, sr, cc, sc = jnp.cos(ang_r), jnp.sin(ang_r), jnp.cos(ang_c), jnp.sin(ang_c)
    zeros = jnp.zeros((n, LANES - MLA_ROPE), F32)
    cos_t = jnp.concatenate([cr, cr, cc, cc, zeros], axis=1)
    sin_t = jnp.concatenate([-sr, sr, -sc, sc, zeros], axis=1)
    return cos_t, sin_t


def _stacked_weights(p):
    w_uq = p["mla_w_uq"].reshape(DEPTH, Q_LORA, MLA_HEADS, MLA_NOPE + MLA_ROPE)
    zq = jnp.zeros((DEPTH, Q_LORA, MLA_HEADS, QH_W - MLA_NOPE - MLA_ROPE), F32)
    w_uq_a = jnp.concatenate([w_uq, zq], axis=-1).reshape(DEPTH, Q_LORA, MLA_HEADS * QH_W).astype(BF16)
    w_uq_sw = jnp.concatenate([jnp.zeros((DEPTH, Q_LORA, MLA_HEADS, MLA_NOPE), F32),
                               w_uq[..., MLA_NOPE + _ROPE_SWAP], zq], axis=-1)
    w_uq_sw = w_uq_sw.reshape(DEPTH, Q_LORA, MLA_HEADS * QH_W).astype(BF16)

    def per_direction(w):
        half = jnp.zeros((DEPTH, 64, MIX_W), F32)
        return jnp.stack([jnp.concatenate([w[:, 0], half], axis=1),
                          jnp.concatenate([half, w[:, 1]], axis=1)], axis=1).astype(BF16)

    row = lambda v: v.reshape(DEPTH, 1, -1)
    pad_row = lambda v: jnp.pad(row(v), ((0, 0), (0, 0), (0, LANES - 2 * HEADS)))
    return {
        "norm1": row(p["norm1"]),
        "w_in": p["w_in"],
        "q_norm": row(p["mla_q_norm"]),
        "w_uq": w_uq_a, "w_uq_sw": w_uq_sw,
        "kv_norm": row(p["mla_kv_norm"]),
        "w_ukv": p["mla_w_ukv"].astype(BF16),
        "gdn_conv": p["gdn_conv"],
        "gdn_alog": pad_row(p["gdn_a_log"]),
        "gdn_dtb": pad_row(p["gdn_dt_bias"]),
        "gdn_norm": jnp.tile(row(p["gdn_norm"]), (1, 1, HEADS)),
        "rwkv_mu_prev": row(p["rwkv_mu_prev"]),
        "rwkv_mu_next": row(p["rwkv_mu_next"]),
        "rwkv_w0": p["rwkv_w0"],
        "rwkv_w2": per_direction(p["rwkv_w2"]),
        "rwkv_a0": p["rwkv_a0"],
        "rwkv_a2": per_direction(p["rwkv_a2"]),
        "rwkv_g2": p["rwkv_g2"].astype(BF16),
        "rwkv_k_k": row(p["rwkv_k_k"]),
        "rwkv_k_a": row(p["rwkv_k_a"]),
        "rwkv_r_k": row(p["rwkv_r_k"]),
        "rwkv_gn_w": row(p["rwkv_gn_w"]),
        "rwkv_gn_b": row(p["rwkv_gn_b"]),
        "w_out": p["w_out"].astype(BF16),
        "norm2": row(p["norm2"]),
        "router_t": jnp.swapaxes(p["moe_router"], 1, 2),
        "router_b": p["moe_bias"].reshape(DEPTH, N_EXPERTS, 1),
        "moe_w_gu": p["moe_w_gu"],
        "moe_w_down": p["moe_w_down"],
        "shared_w_gu": p["shared_w_gu"].astype(BF16),
        "shared_w_down": p["shared_w_down"].astype(BF16),
    }


def _embed_block_diag(s):
    b = s.shape[0]
    eye = jnp.eye(HEADS, dtype=s.dtype)
    out = jnp.einsum("bdhkv,hg->bdhkgv", s, eye)
    return out.reshape(b, 2, MIX_W, MIX_W)


def _layer_front(x2d, mods, lw, l, consts, seq_len, mod_base, cache, tm, tq, counts_in):
    zm, zg, zab, zr = _inproj(x2d, mods, lw, l, tm, seq_len, mod_base)
    if cache is None:
        o_mla, ckv = _mla(zm, lw, l, seq_len, tq)
        o_gdn, s_gdn = _gdn(zg, zab, lw, l, consts, seq_len)
        o_rwkv, s_rwkv = _rwkv(zr, lw, l, consts, seq_len)
        s_rwkv = jnp.swapaxes(s_rwkv, -1, -2)
    else:
        cckv, ckpe, rc, rs, sg, sr = cache
        o_mla, ckv = _mla(zm, lw, l, seq_len, tq, (cckv, ckpe, rc, rs))
        o_gdn, s_gdn = _gdn(zg, zab, lw, l, consts, seq_len, sg)
        o_rwkv, s_rwkv = _rwkv(zr, lw, l, consts, seq_len, sr)
    routed = _post(x2d, o_mla, o_gdn, o_rwkv, mods, lw, l, consts, tm, seq_len, mod_base, counts_in)
    kpe0 = Q_LORA + KV_LORA
    return routed, (ckv, zm[:, kpe0:kpe0 + MLA_ROPE], s_gdn, s_rwkv)


def kernel(x_prompt, x_sample, cache_mla_ckv, cache_mla_kpe, state_gdn, state_rwkv, c, c_ctx, ada_w, ada_b, norm1, w_in, mla_q_norm, mla_w_uq, mla_kv_norm, mla_w_ukv, gdn_conv, gdn_a_log, gdn_dt_bias, gdn_norm, rwkv_mu_prev, rwkv_mu_next, rwkv_w0, rwkv_w2, rwkv_a0, rwkv_a2, rwkv_g2, rwkv_k_k, rwkv_k_a, rwkv_r_k, rwkv_gn_w, rwkv_gn_b, w_out, norm2, moe_router, moe_bias, moe_w_gu, moe_w_down, shared_w_gu, shared_w_down, norm_f):
    p = dict(norm1=norm1, w_in=w_in, mla_q_norm=mla_q_norm, mla_w_uq=mla_w_uq, mla_kv_norm=mla_kv_norm,
             mla_w_ukv=mla_w_ukv, gdn_conv=gdn_conv, gdn_a_log=gdn_a_log, gdn_dt_bias=gdn_dt_bias,
             gdn_norm=gdn_norm, rwkv_mu_prev=rwkv_mu_prev, rwkv_mu_next=rwkv_mu_next, rwkv_w0=rwkv_w0,
             rwkv_w2=rwkv_w2, rwkv_a0=rwkv_a0, rwkv_a2=rwkv_a2, rwkv_g2=rwkv_g2, rwkv_k_k=rwkv_k_k,
             rwkv_k_a=rwkv_k_a, rwkv_r_k=rwkv_r_k, rwkv_gn_w=rwkv_gn_w, rwkv_gn_b=rwkv_gn_b, w_out=w_out,
             norm2=norm2, moe_router=moe_router, moe_bias=moe_bias, moe_w_gu=moe_w_gu, moe_w_down=moe_w_down,
             shared_w_gu=shared_w_gu, shared_w_down=shared_w_down)
    weights = _stacked_weights(p)
    consts = _constants()
    nf = norm_f.reshape(1, D_MODEL)
    b_ctx, t_ctx, _ = x_prompt.shape
    b_lat, t_lat, _ = x_sample.shape

    cvec8 = jnp.concatenate([c_ctx[None, :], c, jnp.zeros((8 - 1 - b_lat, D_MODEL), F32)], axis=0)
    mods = _adaln(cvec8, ada_w, ada_b)
    mods = mods.reshape(DEPTH, 8, 1, 6 * D_MODEL)

    rc, rs = _rope_tables(t_lat)
    ckpe = jnp.pad(cache_mla_kpe, ((0, 0), (0, 0), (0, 0), (0, LANES - MLA_ROPE)))
    cache = (cache_mla_ckv, ckpe, rc, rs, _embed_block_diag_layers(state_gdn),
             _embed_block_diag_layers(jnp.swapaxes(state_rwkv, -1, -2)))
    xp = x_prompt.reshape(b_ctx * t_ctx, D_MODEL)
    xs = x_sample.reshape(b_lat * t_lat, D_MODEL)
    n_ctx = xp.shape[0]
    tm = POST_TM
    ctx_outs = []
    for l in range(DEPTH):
        final = l == DEPTH - 1
        no_pairs = jnp.zeros((N_EXPERTS, LANES), F32)
        (x1c, hc, eidc, rankc, ewc, cnt_c), outs = _layer_front(xp, mods, weights, l, consts, t_ctx, 0, None,
                                                                tm, t_ctx, no_pairs)
        (x1s, hs, eids, ranks, ews, cnt), _ = _layer_front(xs, mods, weights, l, consts, t_lat, 1, cache,
                                                           tm, 256, cnt_c)
        ctx_outs.append(outs)
        yg = _moe_experts(jnp.concatenate([hc, hs], axis=0), jnp.concatenate([eidc, eids], axis=1),
                          jnp.concatenate([rankc, ranks], axis=1), cnt, weights, l)
        xp = _moe_combine(yg, 0, ewc, hc, x1c, mods, weights, l, nf, tm, t_ctx, 0, final)
        xs = _moe_combine(yg, n_ctx, ews, hs, x1s, mods, weights, l, nf, tm, t_lat, 1, final)

    y_prompt = xp.reshape(b_ctx, t_ctx, D_MODEL)
    y_sample = xs.reshape(b_lat, t_lat, D_MODEL)
    new_ckv = jnp.stack([o[0].reshape(b_ctx, t_ctx, KV_LORA) for o in ctx_outs], axis=1)
    new_kpe = jnp.stack([o[1].reshape(b_ctx, t_ctx, MLA_ROPE) for o in ctx_outs], axis=1)
    new_gdn = jnp.stack([o[2] for o in ctx_outs], axis=1)
    new_rwkv = jnp.stack([o[3] for o in ctx_outs], axis=1)
    return (y_prompt, y_sample, new_ckv, new_kpe, new_gdn, new_rwkv)


def _embed_block_diag_layers(s):
    b = s.shape[0]
    return _embed_block_diag(s.reshape(b * DEPTH, 2, HEADS, HEAD_DIM, HEAD_DIM)).reshape(
        b, DEPTH, 2, MIX_W, MIX_W)
```

```python
import functools

import numpy as np
import jax
import jax.numpy as jnp
from jax import lax
from jax.experimental import pallas as pl
from jax.experimental.pallas import tpu as pltpu
from jax.experimental.pallas import tpu_sc as plsc

F32 = jnp.float32
BF16 = jnp.bfloat16

D_MODEL = 1024
BATCH = 32
SEQ = 256
DEPTH = 2
DEC_BATCH = 2
DEC_SEQ = 1024
PAST_LEN = 512
GRID_W = 64
NORM_EPS = 1e-6

MLA_HEADS = 4
MLA_NOPE = 128
MLA_ROPE = 64
MLA_V = 128
Q_LORA = 384
KV_LORA = 256
ROPE_BASE = 10000.0
MLA_SCALE = (MLA_NOPE + MLA_ROPE) ** -0.5

HEADS = 4
HEAD_DIM = 64
MIX_W = HEADS * HEAD_DIM
GDN_CONV_CH = 3 * MIX_W
CHUNK = 64
RWKV_GN_EPS = 64e-5

N_EXPERTS = 64
TOP_K = 8
N_GROUPS = 8
GROUP_SIZE = N_EXPERTS // N_GROUPS
TOPK_GROUPS = 4
D_EXPERT = 256
ROUTE_SCALE = 2.5

P_MLA = Q_LORA + KV_LORA + MLA_ROPE
P_GDN = GDN_CONV_CH + MIX_W + 4 * HEADS
P_RWKV = 3 * MIX_W + 128 + 128 + 128

LANES = 128
ZM_W = Q_LORA + KV_LORA + 2 * LANES
ZG_W = GDN_CONV_CH + MIX_W
ZR_W = P_RWKV
QH_W = 2 * LANES
VMEM_LIMIT = 56 * 1024 * 1024
POST_TM = 512

_ROPE_SWAP = np.concatenate([np.arange(16, 32), np.arange(0, 16), np.arange(48, 64), np.arange(32, 48)])


def _sigmoid(x):
    return 1.0 / (1.0 + jnp.exp(-x))


def _silu(x):
    return x * _sigmoid(x)


def _softplus(x):
    return jnp.maximum(x, 0.0) + jnp.log(1.0 + jnp.exp(-jnp.abs(x)))


def _rms(x, g, eps=NORM_EPS):
    return x * lax.rsqrt(jnp.mean(x * x, axis=-1, keepdims=True) + eps) * g


def _mm(a, b):
    return jnp.dot(a.astype(BF16), b.astype(BF16), preferred_element_type=F32)


def _mm_nt(a, b):
    return lax.dot_general(a.astype(BF16), b.astype(BF16), (((1,), (1,)), ((), ())),
                           preferred_element_type=F32)


def _mm_tn(a, b):
    return lax.dot_general(a.astype(BF16), b.astype(BF16), (((0,), (0,)), ((), ())),
                           preferred_element_type=F32)


def _split3(x):
    p1 = x.astype(BF16)
    r1 = x - p1.astype(F32)
    p2 = r1.astype(BF16)
    r2 = r1 - p2.astype(F32)
    return p1, p2, r2.astype(BF16)


def _mm_sel_l(sel, x):
    p1, p2, p3 = _split3(x)
    return _mm(sel, p1) + _mm(sel, p2) + _mm(sel, p3)


def _mm_sel_r(x, sel):
    p1, p2, p3 = _split3(x)
    return _mm(p1, sel) + _mm(p2, sel) + _mm(p3, sel)


def _iota(shape, dim):
    return lax.broadcasted_iota(jnp.int32, shape, dim)


def _layer_spec(a, layer, **kw):
    nd = a.ndim - 1
    return pl.BlockSpec((None,) + a.shape[1:], lambda *_: (layer,) + (0,) * nd, **kw)


def _const_spec(a, **kw):
    return pl.BlockSpec(a.shape, lambda *_: (0,) * a.ndim, **kw)


def _bd(x, maskbd):
    xb = x.astype(BF16)
    return jnp.concatenate([xb] * HEADS, axis=0) * maskbd


def _chunk_masks(rev):
    row = _iota((CHUNK, MIX_W), 0)
    col = jnp.bitwise_and(_iota((CHUNK, MIX_W), 1), HEAD_DIM - 1)
    r2 = _iota((CHUNK, CHUNK), 0)
    c2 = _iota((CHUNK, CHUNK), 1)
    if rev:
        inc, strict, tri = row <= col, row < col, r2 <= c2
    else:
        inc, strict, tri = row >= col, row > col, r2 >= c2
    eye = jnp.where(row == col, 1.0, 0.0).astype(F32)
    return inc, strict, jnp.where(tri, 1.0, 0.0).astype(BF16), eye


def _split2(x):
    hi = x.astype(BF16)
    return hi, (x - hi.astype(F32)).astype(BF16)


def _mm_bd3(x, p, maskbd):
    n = x.shape[0]
    xh, xl = _split2(x)
    ph, pl_ = _split2(p)
    r = jnp.dot(jnp.concatenate([xh, xl], axis=0), _bd(ph, maskbd), preferred_element_type=F32)
    return r[:n] + r[n:] + jnp.dot(xh, _bd(pl_, maskbd), preferred_element_type=F32)


def _neumann_inverse(a_list, eye_list, maskbd):
    bs = [-a for a in a_list]
    ms = [eye + b for eye, b in zip(eye_list, bs)]
    ps = [_mm_bd3(b, b, maskbd) for b in bs]
    for _ in range(4):
        boths = [_mm_bd3(jnp.concatenate([m, p], axis=0), p, maskbd) for m, p in zip(ms, ps)]
        ms = [m + both[:CHUNK] for m, both in zip(ms, boths)]
        ps = [both[CHUNK:] for both in boths]
    return [m + _mm_bd3(m, p, maskbd) for m, p in zip(ms, ps)]


def _adaln_kernel(c_ref, w_ref, b_ref, o_ref):
    cv = c_ref[...]
    o_ref[0] = _mm(_silu(cv), w_ref[0]) + b_ref[0]


def _adaln(cvec8, ada_w, ada_b):
    tn = 768
    n_out = 6 * D_MODEL
    return pl.pallas_call(
        _adaln_kernel,
        grid=(DEPTH, n_out // tn),
        in_specs=[
            pl.BlockSpec((8, D_MODEL), lambda l, j: (0, 0)),
            pl.BlockSpec((1, D_MODEL, tn), lambda l, j: (l, 0, j)),
            pl.BlockSpec((1, 1, tn), lambda l, j: (l, 0, j)),
        ],
        out_specs=pl.BlockSpec((1, 8, tn), lambda l, j: (l, 0, j)),
        out_shape=jax.ShapeDtypeStruct((DEPTH, 8, n_out), F32),
        compiler_params=pltpu.CompilerParams(dimension_semantics=("arbitrary", "arbitrary"),
                                             vmem_limit_bytes=VMEM_LIMIT),
        name="adaln",
    )(cvec8, ada_w, ada_b.reshape(DEPTH, 1, n_out))


_KPE0 = Q_LORA + KV_LORA
_W_IN_MOVES = (
    [(0, 0, P_MLA)]
    + [(P_MLA + LANES - MLA_ROPE + 16 * j, _KPE0 + 16 * int(_ROPE_SWAP[16 * j] // 16), 16) for j in range(4)]
    + [(ZM_W, P_MLA, ZG_W), (ZM_W + ZG_W, P_MLA + ZG_W, 4 * HEADS), (ZM_W + ZG_W + LANES, P_MLA + P_GDN, P_RWKV)]
)
W_IN_PAD = ZM_W + ZG_W + LANES + ZR_W


def _inproj_kernel(x_ref, mod_ref, n1_ref, w_ref, zm_ref, zg_ref, zab_ref, zr_ref, w_s):
    @pl.when(pl.program_id(0) == 0)
    def _():
        w_s[...] = jnp.zeros(w_s.shape, BF16)
        for dst, src, width in _W_IN_MOVES:
            w_s[:, dst:dst + width] = w_ref[:, src:src + width].astype(BF16)

    m = mod_ref[0]
    sh = m[:, 0:D_MODEL]
    sc = m[:, D_MODEL:2 * D_MODEL]
    h = _rms(x_ref[...], n1_ref[...]) * (1.0 + sc) + sh
    z = _mm(h, w_s[...])
    o1 = ZM_W
    o2 = o1 + ZG_W
    o3 = o2 + LANES
    zm_ref[...] = z[:, :o1]
    zg_ref[...] = z[:, o1:o2]
    zab_ref[...] = z[:, o2:o3]
    zr_ref[...] = z[:, o3:]


def _inproj(x2d, mods, lw, layer, tm, seq_len, mod_base):
    n = x2d.shape[0]
    tiles_per_seq = seq_len // tm if mod_base else 1

    def mod_idx(i):
        return (layer, mod_base + i // tiles_per_seq if mod_base else 0, 0, 0)

    return pl.pallas_call(
        _inproj_kernel,
        grid=(n // tm,),
        in_specs=[
            pl.BlockSpec((tm, D_MODEL), lambda i: (i, 0)),
            pl.BlockSpec((None, 1, 1, 6 * D_MODEL), mod_idx),
            _layer_spec(lw["norm1"], layer),
            _layer_spec(lw["w_in"], layer, pipeline_mode=pl.Buffered(1)),
        ],
        out_specs=[
            pl.BlockSpec((tm, ZM_W), lambda i: (i, 0)),
            pl.BlockSpec((tm, ZG_W), lambda i: (i, 0)),
            pl.BlockSpec((tm, LANES), lambda i: (i, 0)),
            pl.BlockSpec((tm, ZR_W), lambda i: (i, 0)),
        ],
        out_shape=[
            jax.ShapeDtypeStruct((n, ZM_W), F32),
            jax.ShapeDtypeStruct((n, ZG_W), F32),
            jax.ShapeDtypeStruct((n, LANES), F32),
            jax.ShapeDtypeStruct((n, ZR_W), F32),
        ],
        scratch_shapes=[pltpu.VMEM((D_MODEL, W_IN_PAD), BF16)],
        compiler_params=pltpu.CompilerParams(dimension_semantics=("arbitrary",),
                                             vmem_limit_bytes=VMEM_LIMIT),
        name="inproj",
    )(x2d, mods, lw["norm1"], lw["w_in"])


def _mla_kernel(*refs, seq_len, tq, past, cached):
    if cached:
        (zm_ref, cckv_ref, ckpe_ref, rc_ref, rs_ref, qn_ref, wuq_ref, wuqs_ref, kvn_ref, wukv_ref,
         o_ref, ckv_ref, k_s, v_s) = refs
    else:
        (zm_ref, qn_ref, wuq_ref, kvn_ref, wukv_ref, o_ref, ckv_ref, k_s, v_s) = refs
    qi = pl.program_id(1)
    o_kpe = Q_LORA + KV_LORA

    @pl.when(qi == 0)
    def _():
        zm = zm_ref[...]
        ckv = _rms(zm[:, Q_LORA:o_kpe], kvn_ref[...])
        ckv_ref[...] = ckv
        kpe = zm[:, o_kpe:o_kpe + LANES]
        if cached:
            kpe = kpe * rc_ref[...] + zm[:, o_kpe + LANES:o_kpe + 2 * LANES] * rs_ref[...]
            kvc = _mm(cckv_ref[0], wukv_ref[...])
            kpc = ckpe_ref[0].astype(BF16)
        kv = _mm(ckv, wukv_ref[...])
        kpe = kpe.astype(BF16)
        for h in range(MLA_HEADS):
            c0 = h * QH_W
            if cached:
                k_s[0:past, c0:c0 + LANES] = kvc[:, c0:c0 + LANES].astype(BF16)
                k_s[0:past, c0 + LANES:c0 + QH_W] = kpc
                v_s[0:past, h * MLA_V:(h + 1) * MLA_V] = kvc[:, c0 + LANES:c0 + QH_W].astype(BF16)
            k_s[past:past + seq_len, c0:c0 + LANES] = kv[:, c0:c0 + LANES].astype(BF16)
            k_s[past:past + seq_len, c0 + LANES:c0 + QH_W] = kpe
            v_s[past:past + seq_len, h * MLA_V:(h + 1) * MLA_V] = kv[:, c0 + LANES:c0 + QH_W].astype(BF16)

    r0 = pl.multiple_of(qi * tq, tq)
    zq = zm_ref[pl.ds(r0, tq), :]
    cq = _rms(zq[:, :Q_LORA], qn_ref[...])
    q = _mm(cq, wuq_ref[...])
    if cached:
        qs = _mm(cq, wuqs_ref[...])
        ones = jnp.ones((tq, LANES), F32)
        zeros = jnp.zeros((tq, LANES), F32)
        qc = jnp.concatenate([ones, rc_ref[pl.ds(r0, tq), :]], axis=1)
        qsn = jnp.concatenate([zeros, rs_ref[pl.ds(r0, tq), :]], axis=1)
    for h in range(MLA_HEADS):
        c0 = h * QH_W
        qh = q[:, c0:c0 + QH_W]
        if cached:
            qh = qh * qc + qs[:, c0:c0 + QH_W] * qsn
        s = _mm_nt(qh, k_s[:, c0:c0 + QH_W]) * MLA_SCALE
        e = jnp.exp(s - jnp.max(s, axis=-1, keepdims=True))
        den = jnp.sum(e, axis=-1, keepdims=True)
        o_ref[:, h * MLA_V:(h + 1) * MLA_V] = _mm(e, v_s[:, h * MLA_V:(h + 1) * MLA_V]) / den


def _mla(zm, lw, layer, seq_len, tq, cache=None):
    n = zm.shape[0]
    nb = n // seq_len
    cached = cache is not None
    past = PAST_LEN if cached else 0
    tk = past + seq_len
    seq_spec = lambda w: pl.BlockSpec((seq_len, w), lambda b, q: (b, 0))
    lay = lambda a: _layer_spec(a, layer)
    if cached:
        cckv, ckpe, rc, rs = cache
        args = [zm, cckv, ckpe, rc, rs, lw["q_norm"], lw["w_uq"], lw["w_uq_sw"], lw["kv_norm"], lw["w_ukv"]]
        in_specs = [seq_spec(ZM_W),
                    pl.BlockSpec((1, None, past, KV_LORA), lambda b, q: (b, layer, 0, 0)),
                    pl.BlockSpec((1, None, past, LANES), lambda b, q: (b, layer, 0, 0)),
                    _const_spec(rc), _const_spec(rs)] + [lay(a) for a in args[5:]]
    else:
        args = [zm, lw["q_norm"], lw["w_uq"], lw["kv_norm"], lw["w_ukv"]]
        in_specs = [seq_spec(ZM_W)] + [lay(a) for a in args[1:]]
    return pl.pallas_call(
        functools.partial(_mla_kernel, seq_len=seq_len, tq=tq, past=past, cached=cached),
        grid=(nb, seq_len // tq),
        in_specs=in_specs,
        out_specs=[
            pl.BlockSpec((tq, MLA_HEADS * MLA_V), lambda b, q: (b * (seq_len // tq) + q, 0)),
            pl.BlockSpec((seq_len, KV_LORA), lambda b, q: (b, 0)),
        ],
        out_shape=[
            jax.ShapeDtypeStruct((n, MLA_HEADS * MLA_V), F32),
            jax.ShapeDtypeStruct((n, KV_LORA), F32),
        ],
        scratch_shapes=[
            pltpu.VMEM((tk, MLA_HEADS * QH_W), BF16),
            pltpu.VMEM((tk, MLA_HEADS * MLA_V), BF16),
        ],
        compiler_params=pltpu.CompilerParams(dimension_semantics=("arbitrary", "arbitrary"),
                                             vmem_limit_bytes=VMEM_LIMIT),
        name="mla_lat" if cached else "mla_ctx",
    )(*args)


SEQ_PER_STEP = 2
CHUNK_GROUP = 4
GDN_PRE_W = 5 * MIX_W


def _for_chunk_groups(n_chunks, fn):
    if n_chunks == CHUNK_GROUP:
        fn(0)
    else:
        def body(gi, carry):
            fn(gi * CHUNK_GROUP)
            return carry
        lax.fori_loop(0, n_chunks // CHUNK_GROUP, body, 0)


def _gdn_prepare(items, maskbd):
    n = range(len(items))
    qs, ks, vs, gs, betas, masks, revs = zip(*items)
    gcs = [_mm_sel_l(masks[i][2], gs[i]) for i in n]
    decays = []
    for i in n:
        inc, eye = masks[i][0], masks[i][3]
        gc_row = jnp.sum(eye * gcs[i], axis=0, keepdims=True)
        decays.append(jnp.where(inc, jnp.exp(jnp.where(inc, gcs[i] - gc_row, 0.0)), 0.0))
    kbs = [ks[i] * betas[i] for i in n]
    aqs = [_mm_nt(jnp.concatenate([kbs[i], qs[i]], axis=0), _bd(ks[i], maskbd)) for i in n]
    a_mats = [jnp.where(masks[i][1], aqs[i][:CHUNK] * decays[i], 0.0) for i in n]
    t_invs = _neumann_inverse(a_mats, [m[3] for m in masks], maskbd)
    egcs = [jnp.exp(gc) for gc in gcs]
    uws = [_mm(t_invs[i], jnp.concatenate([_bd(vs[i] * betas[i], maskbd), _bd(kbs[i] * egcs[i], maskbd)], axis=1))
           for i in n]
    out = []
    for i in n:
        g_last = gcs[i][0:1] if revs[i] else gcs[i][CHUNK - 1:CHUNK]
        pre = jnp.concatenate([uws[i], qs[i] * egcs[i], aqs[i][CHUNK:] * decays[i],
                               ks[i] * jnp.exp(g_last - gcs[i])], axis=1)
        out.append((pre, jnp.broadcast_to(jnp.exp(g_last), (8, MIX_W))))
    return out


def _gdn_step(items, maskbd):
    w = MIX_W
    n = range(len(items))
    pres, egls, states = zip(*items)
    wqs = [_mm(jnp.concatenate([pres[i][:, w:2 * w], pres[i][:, 2 * w:3 * w]], axis=0), states[i]) for i in n]
    v_news = [pres[i][:, :w] - wqs[i][:CHUNK] for i in n]
    outs = [wqs[i][CHUNK:] + _mm(pres[i][:, 3 * w:4 * w], _bd(v_news[i], maskbd)) for i in n]
    upds = [_mm_tn(pres[i][:, 4 * w:], v_news[i]) for i in n]
    mask_f = maskbd.astype(F32)
    return [(outs[i], states[i] * egls[i] + upds[i] * mask_f) for i in n]


def _gdn_kernel(*refs, seq_len, n_seq, cached):
    if cached:
        (zg_ref, zab_ref, s0_ref, conv_ref, alog_ref, dtb_ref, gn_ref, eexp_ref, ones_ref, maskbd_ref,
         o_ref, sout_ref, q_s, k_s, v_s, ge_s, pre_s, gl_s, oacc_s, st_s) = refs
    else:
        (zg_ref, zab_ref, conv_ref, alog_ref, dtb_ref, gn_ref, eexp_ref, ones_ref, maskbd_ref,
         o_ref, sout_ref, q_s, k_s, v_s, ge_s, pre_s, gl_s, oacc_s, st_s) = refs
    t = seq_len * n_seq
    z = zg_ref[:, :GDN_CONV_CH]
    rowi = jnp.bitwise_and(_iota((t, 1), 0), seq_len - 1)
    zp = jnp.where(rowi == 0, 0.0, pltpu.roll(z, 1, 0))
    zn = jnp.where(rowi == seq_len - 1, 0.0, pltpu.roll(z, t - 1, 0))
    cw = conv_ref[...]
    qkv = _silu(zp * cw[0:1] + z * cw[1:2] + zn * cw[2:3])
    ones_bd = ones_ref[...]
    q = qkv[:, :MIX_W]
    k = qkv[:, MIX_W:2 * MIX_W]
    q_s[...] = q * lax.rsqrt(_mm_sel_r(q * q, ones_bd) + 1e-6) * (HEAD_DIM ** -0.5)
    k_s[...] = k * lax.rsqrt(_mm_sel_r(k * k, ones_bd) + 1e-6)
    v_s[...] = qkv[:, 2 * MIX_W:]
    ab = zab_ref[...]
    lane = _iota((t, LANES), 1)
    gb = jnp.where(lane < 2 * HEADS, -jnp.exp(alog_ref[...]) * _softplus(ab + dtb_ref[...]), _sigmoid(ab))
    ge_s[...] = _mm_sel_r(gb, eexp_ref[...])
    oacc_s[...] = jnp.zeros((t, MIX_W), F32)
    if cached:
        st_s[...] = s0_ref[...]
    else:
        st_s[...] = jnp.zeros((n_seq, 2, MIX_W, MIX_W), F32)
    maskbd = maskbd_ref[...]
    masks = (_chunk_masks(False), _chunk_masks(True))
    n_chunks = seq_len // CHUNK

    def prepare_group(c0):
        where, items = [], []
        for j in range(CHUNK_GROUP):
            c = c0 + j
            rows = pl.ds(pl.multiple_of(c * CHUNK, CHUNK), CHUNK)
            for d in range(2):
                where.append((d, c, rows))
                items.append((q_s[rows, :], k_s[rows, :], v_s[rows, :], ge_s[rows, d * MIX_W:(d + 1) * MIX_W],
                              ge_s[rows, (2 + d) * MIX_W:(3 + d) * MIX_W], masks[d], d == 1))
        for (d, c, rows), (pre, egl) in zip(where, _gdn_prepare(items, maskbd)):
            pre_s[d, rows, :] = pre
            gl_s[d, pl.ds(pl.multiple_of(c * 8, 8), 8), :] = egl

    _for_chunk_groups(n_chunks * n_seq, prepare_group)

    def body(i, carry):
        where, items = [], []
        for q in range(n_seq):
            for d in range(2):
                c = q * n_chunks + (i if d == 0 else n_chunks - 1 - i)
                rows = pl.ds(pl.multiple_of(c * CHUNK, CHUNK), CHUNK)
                where.append((q, d, rows))
                items.append((pre_s[d, rows, :], gl_s[d, pl.ds(pl.multiple_of(c * 8, 8), 8), :][0:1], st_s[q, d]))
        for (q, d, rows), (o, s_new) in zip(where, _gdn_step(items, maskbd)):
            oacc_s[rows, :] = oacc_s[rows, :] + o
            st_s[q, d] = s_new
        return carry

    lax.fori_loop(0, n_chunks, body, 0)
    o = oacc_s[...]
    ms = _mm_sel_r(o * o, ones_bd) * (1.0 / HEAD_DIM)
    gate = zg_ref[:, GDN_CONV_CH:]
    o_ref[...] = o * lax.rsqrt(ms + NORM_EPS) * gn_ref[...] * _silu(gate)
    for q in range(n_seq):
        for d in range(2):
            for h in range(HEADS):
                sl = slice(h * HEAD_DIM, (h + 1) * HEAD_DIM)
                sout_ref[q, d, h] = st_s[q, d, sl, sl]


def _gdn(zg, zab, lw, layer, consts, seq_len, s0_bd=None):
    n = zg.shape[0]
    nb = n // seq_len
    cached = s0_bd is not None
    n_seq = 1 if cached else SEQ_PER_STEP
    rows = seq_len * n_seq
    args = [zg, zab]
    in_specs = [pl.BlockSpec((rows, ZG_W), lambda b: (b, 0)),
                pl.BlockSpec((rows, LANES), lambda b: (b, 0))]
    if cached:
        args.append(s0_bd)
        in_specs.append(pl.BlockSpec((1, None, 2, MIX_W, MIX_W), lambda b: (b, layer, 0, 0, 0)))
    layered = [lw["gdn_conv"], lw["gdn_alog"], lw["gdn_dtb"], lw["gdn_norm"]]
    const = [consts["eexp"], consts["ones_bd"], consts["maskbd"]]
    args += layered + const
    in_specs += [_layer_spec(a, layer) for a in layered] + [_const_spec(a) for a in const]
    return pl.pallas_call(
        functools.partial(_gdn_kernel, seq_len=seq_len, n_seq=n_seq, cached=cached),
        grid=(nb // n_seq,),
        in_specs=in_specs,
        out_specs=[
            pl.BlockSpec((rows, MIX_W), lambda b: (b, 0)),
            pl.BlockSpec((n_seq, 2, HEADS, HEAD_DIM, HEAD_DIM), lambda b: (b, 0, 0, 0, 0)),
        ],
        out_shape=[
            jax.ShapeDtypeStruct((n, MIX_W), F32),
            jax.ShapeDtypeStruct((nb, 2, HEADS, HEAD_DIM, HEAD_DIM), F32),
        ],
        scratch_shapes=[
            pltpu.VMEM((rows, MIX_W), F32),
            pltpu.VMEM((rows, MIX_W), F32),
            pltpu.VMEM((rows, MIX_W), F32),
            pltpu.VMEM((rows, 4 * MIX_W), F32),
            pltpu.VMEM((2, rows, GDN_PRE_W), F32),
            pltpu.VMEM((2, rows // CHUNK * 8, MIX_W), F32),
            pltpu.VMEM((rows, MIX_W), F32),
            pltpu.VMEM((n_seq, 2, MIX_W, MIX_W), F32),
        ],
        compiler_params=pltpu.CompilerParams(dimension_semantics=("arbitrary",),
                                             vmem_limit_bytes=VMEM_LIMIT),
        name="gdn_lat" if cached else "gdn_ctx",
    )(*args)


RWKV_PRE_W = 7 * MIX_W


def _rwkv_prepare(items, maskbd, eye_full):
    n = range(len(items))
    rs, kds, vs, kks, bs, lws, masks, revs = zip(*items)
    cums = [_mm_sel_l(masks[i][2], lws[i]) for i in n]
    einvs = [jnp.exp(-c) for c in cums]
    kts = [kks[i] * jnp.exp(cums[i] - lws[i]) for i in n]
    rts = [rs[i] * jnp.exp(cums[i]) for i in n]
    krs = [jnp.concatenate([kts[i], rts[i]], axis=0) for i in n]
    lb_alls = [_mm_nt(krs[i], _bd(bs[i] * einvs[i], maskbd)) for i in n]
    lk_alls = [_mm_nt(krs[i], _bd(kds[i] * einvs[i], maskbd)) for i in n]
    lbs = [jnp.where(masks[i][1], lb_alls[i][:CHUNK], 0.0) for i in n]
    t_invs = _neumann_inverse(lbs, [m[3] for m in masks], maskbd)
    lvs = [_mm(jnp.concatenate([jnp.where(masks[i][1], lk_alls[i][:CHUNK], 0.0),
                                jnp.where(masks[i][0], lk_alls[i][CHUNK:], 0.0)], axis=0), _bd(vs[i], maskbd))
           for i in n]
    tkps = [_mm(t_invs[i], jnp.concatenate([_bd(kts[i], maskbd), _bd(lvs[i][:CHUNK], maskbd)], axis=1)) for i in n]
    out = []
    for i in n:
        c_last = cums[i][0:1] if revs[i] else cums[i][CHUNK - 1:CHUNK]
        tail = jnp.exp(c_last - cums[i])
        rb = jnp.where(masks[i][0], lb_alls[i][CHUNK:], 0.0)
        pre = jnp.concatenate([tkps[i][:, :MIX_W], rts[i], tkps[i][:, MIX_W:], lvs[i][CHUNK:], rb,
                               kds[i] * tail, bs[i] * tail], axis=1)
        gcol = jnp.sum(eye_full * jnp.exp(c_last), axis=1, keepdims=True)
        out.append((pre, jnp.broadcast_to(gcol, (MIX_W, LANES))))
    return out


def _rwkv_step(items, maskbd):
    w = MIX_W
    n = range(len(items))
    pres, vs, gcols, states = zip(*items)
    prs = [_mm(jnp.concatenate([pres[i][:, :w], pres[i][:, w:2 * w]], axis=0), states[i]) for i in n]
    ps = [prs[i][:CHUNK] + pres[i][:, 2 * w:3 * w] for i in n]
    outs = [prs[i][CHUNK:] + pres[i][:, 3 * w:4 * w] - _mm(pres[i][:, 4 * w:5 * w], _bd(ps[i], maskbd)) for i in n]
    upds = [_mm_tn(jnp.concatenate([pres[i][:, 5 * w:6 * w], pres[i][:, 6 * w:]], axis=0),
                   jnp.concatenate([vs[i], -ps[i]], axis=0)) for i in n]
    mask_f = maskbd.astype(F32)
    return [(outs[i], states[i] * jnp.concatenate([gcols[i], gcols[i]], axis=1) + upds[i] * mask_f) for i in n]


def _rwkv_kernel(*refs, seq_len, n_seq, cached):
    if cached:
        (zr_ref, s0_ref, mup_ref, mun_ref, w0_ref, w2_ref, a0_ref, a2_ref, g2_ref, kk_ref, ka_ref, rk_ref,
         gnw_ref, gnb_ref, ones_ref, maskbd_ref, o_ref, sout_ref,
         r_s, v_s, kk_s, dir_s, bg_s, pre_s, gcol_s, oacc_s, st_s) = refs
    else:
        (zr_ref, mup_ref, mun_ref, w0_ref, w2_ref, a0_ref, a2_ref, g2_ref, kk_ref, ka_ref, rk_ref,
         gnw_ref, gnb_ref, ones_ref, maskbd_ref, o_ref, sout_ref,
         r_s, v_s, kk_s, dir_s, bg_s, pre_s, gcol_s, oacc_s, st_s) = refs
    t = seq_len * n_seq
    z = zr_ref[...]
    rowi = jnp.bitwise_and(_iota((t, 1), 0), seq_len - 1)
    zp = jnp.where(rowi == 0, 0.0, pltpu.roll(z, 1, 0))
    zn = jnp.where(rowi == seq_len - 1, 0.0, pltpu.roll(z, t - 1, 0))
    z = z + mup_ref[...] * (zp - z) + mun_ref[...] * (zn - z)
    w = MIX_W
    r = z[:, :w]
    k = z[:, w:2 * w]
    v = z[:, 2 * w:3 * w]
    wd = jnp.tanh(z[:, 3 * w:3 * w + LANES])
    ad = z[:, 3 * w + LANES:3 * w + 2 * LANES]
    gd = _sigmoid(z[:, 3 * w + 2 * LANES:])
    ones_bd = ones_ref[...]
    kk = k * kk_ref[...]
    kk = kk * lax.rsqrt(_mm_sel_r(kk * kk, ones_bd) + 1e-6)
    r_s[...] = r
    v_s[...] = v
    kk_s[...] = kk
    bonus = jnp.zeros((t, w), F32)
    for d in range(2):
        w_log = -_softplus(-(w0_ref[d:d + 1] + _mm(wd, w2_ref[d]))) - 0.5
        a = _sigmoid(a0_ref[d:d + 1] + _mm(ad, a2_ref[d]))
        kd = k * (1.0 + (a - 1.0) * ka_ref[...])
        dir_s[:, (3 * d) * w:(3 * d + 1) * w] = -jnp.exp(w_log)
        dir_s[:, (3 * d + 1) * w:(3 * d + 2) * w] = kd
        dir_s[:, (3 * d + 2) * w:(3 * d + 3) * w] = kk * a
        bonus = bonus + _mm_sel_r(r * kd * rk_ref[...], ones_bd) * v
    bg_s[:, :w] = bonus
    bg_s[:, w:] = _mm(gd, g2_ref[...])
    oacc_s[...] = jnp.zeros((t, w), F32)
    if cached:
        st_s[...] = s0_ref[...]
    else:
        st_s[...] = jnp.zeros((n_seq, 2, w, w), F32)
    maskbd = maskbd_ref[...]
    masks = (_chunk_masks(False), _chunk_masks(True))
    eye_full = jnp.where(_iota((w, w), 0) == _iota((w, w), 1), 1.0, 0.0).astype(F32)
    n_chunks = seq_len // CHUNK

    def prepare_group(c0):
        where, items = [], []
        for j in range(CHUNK_GROUP):
            c = c0 + j
            rows = pl.ds(pl.multiple_of(c * CHUNK, CHUNK), CHUNK)
            for d in range(2):
                where.append((d, c, rows))
                items.append((r_s[rows, :], dir_s[rows, (3 * d + 1) * w:(3 * d + 2) * w], v_s[rows, :], kk_s[rows, :],
                              dir_s[rows, (3 * d + 2) * w:(3 * d + 3) * w], dir_s[rows, (3 * d) * w:(3 * d + 1) * w],
                              masks[d], d == 1))
        for (d, c, rows), (pre, gcol) in zip(where, _rwkv_prepare(items, maskbd, eye_full)):
            pre_s[d, rows, :] = pre
            gcol_s[d, pl.ds(pl.multiple_of(c * w, w), w), :] = gcol

    _for_chunk_groups(n_chunks * n_seq, prepare_group)

    def body(i, carry):
        where, items = [], []
        for q in range(n_seq):
            for d in range(2):
                c = q * n_chunks + (i if d == 0 else n_chunks - 1 - i)
                rows = pl.ds(pl.multiple_of(c * CHUNK, CHUNK), CHUNK)
                where.append((q, d, rows))
                items.append((pre_s[d, rows, :], v_s[rows, :], gcol_s[d, pl.ds(pl.multiple_of(c * w, w), w), :],
                              st_s[q, d]))
        for (q, d, rows), (o, z_new) in zip(where, _rwkv_step(items, maskbd)):
            oacc_s[rows, :] = oacc_s[rows, :] + o
            st_s[q, d] = z_new
        return carry

    lax.fori_loop(0, n_chunks, body, 0)
    o = oacc_s[...]
    inv_n = 1.0 / HEAD_DIM
    mu = _mm_sel_r(o, ones_bd) * inv_n
    oc = o - mu
    var = _mm_sel_r(oc * oc, ones_bd) * inv_n
    y = oc * lax.rsqrt(var + RWKV_GN_EPS) * gnw_ref[...] + gnb_ref[...]
    o_ref[...] = (y + bg_s[:, :w]) * bg_s[:, w:]
    for q in range(n_seq):
        for d in range(2):
            for h in range(HEADS):
                sl = slice(h * HEAD_DIM, (h + 1) * HEAD_DIM)
                sout_ref[q, d, h] = st_s[q, d, sl, sl]


def _rwkv(zr, lw, layer, consts, seq_len, s0_bd=None):
    n = zr.shape[0]
    nb = n // seq_len
    cached = s0_bd is not None
    n_seq = 1 if cached else SEQ_PER_STEP
    rows = seq_len * n_seq
    args = [zr]
    in_specs = [pl.BlockSpec((rows, ZR_W), lambda b: (b, 0))]
    if cached:
        args.append(s0_bd)
        in_specs.append(pl.BlockSpec((1, None, 2, MIX_W, MIX_W), lambda b: (b, layer, 0, 0, 0)))
    layered = [lw["rwkv_mu_prev"], lw["rwkv_mu_next"], lw["rwkv_w0"], lw["rwkv_w2"], lw["rwkv_a0"], lw["rwkv_a2"],
               lw["rwkv_g2"], lw["rwkv_k_k"], lw["rwkv_k_a"], lw["rwkv_r_k"], lw["rwkv_gn_w"], lw["rwkv_gn_b"]]
    const = [consts["ones_bd"], consts["maskbd"]]
    args += layered + const
    in_specs += [_layer_spec(a, layer) for a in layered] + [_const_spec(a) for a in const]
    return pl.pallas_call(
        functools.partial(_rwkv_kernel, seq_len=seq_len, n_seq=n_seq, cached=cached),
        grid=(nb // n_seq,),
        in_specs=in_specs,
        out_specs=[
            pl.BlockSpec((rows, MIX_W), lambda b: (b, 0)),
            pl.BlockSpec((n_seq, 2, HEADS, HEAD_DIM, HEAD_DIM), lambda b: (b, 0, 0, 0, 0)),
        ],
        out_shape=[
            jax.ShapeDtypeStruct((n, MIX_W), F32),
            jax.ShapeDtypeStruct((nb, 2, HEADS, HEAD_DIM, HEAD_DIM), F32),
        ],
        scratch_shapes=[
            pltpu.VMEM((rows, MIX_W), F32),
            pltpu.VMEM((rows, MIX_W), F32),
            pltpu.VMEM((rows, MIX_W), F32),
            pltpu.VMEM((rows, 6 * MIX_W), F32),
            pltpu.VMEM((rows, 2 * MIX_W), F32),
            pltpu.VMEM((2, rows, RWKV_PRE_W), F32),
            pltpu.VMEM((2, rows // CHUNK * MIX_W, LANES), F32),
            pltpu.VMEM((rows, MIX_W), F32),
            pltpu.VMEM((n_seq, 2, MIX_W, MIX_W), F32),
        ],
        compiler_params=pltpu.CompilerParams(dimension_semantics=("arbitrary",),
                                             vmem_limit_bytes=VMEM_LIMIT),
        name="rwkv_lat" if cached else "rwkv_ctx",
    )(*args)


def _route(logits_t, bias):
    tm = logits_t.shape[1]
    neg = -jnp.inf
    sc = _sigmoid(logits_t)
    sc3 = sc.reshape(N_GROUPS, GROUP_SIZE, tm)
    sel = (sc + bias).reshape(N_GROUPS, GROUP_SIZE, tm)
    si = _iota(sel.shape, 1).astype(F32)
    m1 = jnp.max(sel, axis=1, keepdims=True)
    f1 = jnp.min(jnp.where(sel == m1, si, float(GROUP_SIZE)), axis=1, keepdims=True)
    m2 = jnp.max(jnp.where(si == f1, neg, sel), axis=1, keepdims=True)
    grp = m1 + m2
    gi = _iota(grp.shape, 0).astype(F32)
    gsel = jnp.zeros(grp.shape, F32)
    for _ in range(TOPK_GROUPS):
        mx = jnp.max(grp, axis=0, keepdims=True)
        fi = jnp.min(jnp.where(grp == mx, gi, float(N_GROUPS)), axis=0, keepdims=True)
        hit = gi == fi
        gsel = jnp.where(hit, 1.0, gsel)
        grp = jnp.where(hit, neg, grp)
    cur = jnp.where(gsel > 0.0, sel, neg)
    ei = (_iota(cur.shape, 0) * GROUP_SIZE + _iota(cur.shape, 1)).astype(F32)
    chosen = jnp.zeros(cur.shape, F32)
    ids, wts = [], []
    for _ in range(TOP_K):
        mx = jnp.max(jnp.max(cur, axis=0, keepdims=True), axis=1, keepdims=True)
        fi = jnp.min(jnp.min(jnp.where(cur == mx, ei, float(N_EXPERTS)), axis=0, keepdims=True),
                     axis=1, keepdims=True)
        hit = ei == fi
        chosen = jnp.where(hit, 1.0, chosen)
        cur = jnp.where(hit, neg, cur)
        ids.append(fi.reshape(1, tm))
        wts.append(jnp.sum(jnp.sum(jnp.where(hit, sc3, 0.0), axis=0, keepdims=True), axis=1, keepdims=True)
                   .reshape(1, tm))
    w = jnp.concatenate(wts, axis=0)
    w = w / jnp.sum(w, axis=0, keepdims=True) * ROUTE_SCALE
    return chosen.reshape(N_EXPERTS, tm), jnp.concatenate(ids, axis=0), w


def _pack_halves(x):
    half = x.shape[1] // 2
    bits = lax.bitcast_convert_type(x.astype(BF16).astype(F32), jnp.int32)
    lo = lax.shift_right_logical(bits[:, :half], jnp.int32(16))
    return jnp.bitwise_or(lo, jnp.bitwise_and(bits[:, half:], jnp.int32(-65536)))


def _unpack_halves(word):
    lo = lax.bitcast_convert_type(lax.shift_left(word, jnp.int32(16)), F32)
    hi = lax.bitcast_convert_type(jnp.bitwise_and(word, jnp.int32(-65536)), F32)
    return lo, hi


def _post_kernel(x_ref, om_ref, og_ref, or_ref, mod_ref, wo_ref, n2_ref, rt_ref, rb_ref, tri_ref, cin_ref,
                 x1_ref, h2_ref, eid_ref, rank_ref, ew_ref, cnt_ref, carry_s):
    @pl.when(pl.program_id(0) == 0)
    def _():
        carry_s[...] = cin_ref[...]

    m = mod_ref[0]
    g1 = m[:, 2 * D_MODEL:3 * D_MODEL]
    sh2 = m[:, 3 * D_MODEL:4 * D_MODEL]
    sc2 = m[:, 4 * D_MODEL:5 * D_MODEL]
    w_mla = MLA_HEADS * MLA_V
    mix = (_mm(om_ref[...], wo_ref[0:w_mla, :]) + _mm(og_ref[...], wo_ref[w_mla:w_mla + MIX_W, :])
           + _mm(or_ref[...], wo_ref[w_mla + MIX_W:, :]))
    x1 = x_ref[...] + g1 * mix
    x1_ref[...] = x1
    h2 = _rms(x1, n2_ref[...]) * (1.0 + sc2) + sh2
    h2_ref[...] = _pack_halves(h2)
    r_hi, r_lo = _split2(rt_ref[...])
    h_hi, h_lo = _split2(h2)
    logits_t = _mm_nt(r_hi, h_hi) + _mm_nt(r_hi, h_lo) + _mm_nt(r_lo, h_hi)
    chosen, ids, w = _route(logits_t, rb_ref[...])
    tm = chosen.shape[1]
    rank_et = (carry_s[:, 0:1] + _mm(chosen, tri_ref[...])).reshape(N_GROUPS, GROUP_SIZE, tm)
    ei = (_iota(rank_et.shape, 0) * GROUP_SIZE + _iota(rank_et.shape, 1)).astype(F32)
    ranks = []
    for k in range(TOP_K):
        pick = jnp.where(ei == ids[k:k + 1].reshape(1, 1, tm), rank_et, 0.0)
        ranks.append(jnp.sum(jnp.sum(pick, axis=0, keepdims=True), axis=1, keepdims=True).reshape(1, tm))
    eid_ref[...] = ids.astype(jnp.int32)
    rank_ref[...] = jnp.concatenate(ranks, axis=0).astype(jnp.int32)
    ew_ref[...] = jnp.concatenate([w, jnp.zeros((LANES - TOP_K, tm), F32)], axis=0).T
    total = carry_s[...] + jnp.sum(chosen, axis=1, keepdims=True)
    carry_s[...] = total
    cnt_ref[...] = total


def _post(x2d, om, og, orw, mods, lw, layer, consts, tm, seq_len, mod_base, counts_in):
    n = x2d.shape[0]
    tiles_per_seq = seq_len // tm if mod_base else 1

    def mod_idx(i):
        return (layer, mod_base + i // tiles_per_seq if mod_base else 0, 0, 0)

    row = lambda w: pl.BlockSpec((tm, w), lambda i: (i, 0))
    col = lambda h: pl.BlockSpec((h, tm), lambda i: (0, i))
    full = lambda a: _layer_spec(a, layer)
    tail = [lw["w_out"], lw["norm2"], lw["router_t"], lw["router_b"]]
    tri = consts["tri_tokens"]
    return pl.pallas_call(
        _post_kernel,
        grid=(n // tm,),
        in_specs=[row(D_MODEL), row(MLA_HEADS * MLA_V), row(MIX_W), row(MIX_W),
                  pl.BlockSpec((None, 1, 1, 6 * D_MODEL), mod_idx)] + [full(a) for a in tail]
        + [_const_spec(tri), _const_spec(counts_in)],
        out_specs=[row(D_MODEL), row(D_MODEL // 2), col(TOP_K), col(TOP_K), row(LANES),
                   pl.BlockSpec((N_EXPERTS, LANES), lambda i: (0, 0))],
        out_shape=[
            jax.ShapeDtypeStruct((n, D_MODEL), F32),
            jax.ShapeDtypeStruct((n, D_MODEL // 2), jnp.int32),
            jax.ShapeDtypeStruct((TOP_K, n), jnp.int32),
            jax.ShapeDtypeStruct((TOP_K, n), jnp.int32),
            jax.ShapeDtypeStruct((n, LANES), F32),
            jax.ShapeDtypeStruct((N_EXPERTS, LANES), F32),
        ],
        scratch_shapes=[pltpu.VMEM((N_EXPERTS, LANES), F32)],
        compiler_params=pltpu.CompilerParams(dimension_semantics=("arbitrary",),
                                             vmem_limit_bytes=VMEM_LIMIT),
        name="post",
    )(x2d, om, og, orw, mods, *tail, tri, counts_in)


MOE_ROWS = 512
SC_ROWS = 128
SC_SUBCORES = 32


def _swiglu_act(gu):
    return _silu(gu[:, :D_EXPERT]) * gu[:, D_EXPERT:]


def _dispatch_plan(eid, rank, counts, n, rows):
    n_blocks = n * TOP_K // rows + N_EXPERTS
    cnt = counts[:, 0].astype(jnp.int32)
    blocks = (cnt + rows - 1) // rows
    block_end = jnp.cumsum(blocks)
    offset = (block_end - blocks) * rows
    experts = jnp.arange(N_EXPERTS, dtype=jnp.int32)
    dest = jnp.sum(jnp.where(eid[..., None] == experts, offset, 0), axis=-1) + rank
    block_ids = jnp.arange(n_blocks, dtype=jnp.int32)
    block_expert = jnp.minimum(jnp.sum((block_end[None, :] <= block_ids[:, None]).astype(jnp.int32), axis=1),
                               N_EXPERTS - 1)
    return dest, block_expert, block_end[-1:].astype(jnp.int32), n_blocks


def _sc_mesh():
    return plsc.VectorSubcoreMesh(core_axis_name="core", subcore_axis_name="subcore")


def _sc_dispatch(x, dest, n_rows):
    n, w = x.shape
    n_chunks = n // SC_ROWS

    @functools.partial(pl.kernel, out_type=jax.ShapeDtypeStruct((n_rows, w), x.dtype), mesh=_sc_mesh(),
                       scratch_types=[pltpu.VMEM((SC_ROWS, w), x.dtype), pltpu.VMEM((TOP_K, SC_ROWS), jnp.int32)])
    def kern(x_hbm, d_hbm, o_hbm, xv, dv):
        sid = lax.axis_index("core") * (SC_SUBCORES // 2) + lax.axis_index("subcore")

        @pl.loop(sid, n_chunks, step=SC_SUBCORES)
        def _(c):
            r0 = pl.multiple_of(c * SC_ROWS, SC_ROWS)
            pltpu.sync_copy(x_hbm.at[pl.ds(r0, SC_ROWS)], xv)
            pltpu.sync_copy(d_hbm.at[:, pl.ds(r0, SC_ROWS)], dv)
            for k in range(TOP_K):
                pltpu.sync_copy(xv, o_hbm.at[dv.at[k]])

    return kern(x, dest)


def _sc_gather(y, idx):
    w = y.shape[1]
    n_chunks = idx.shape[0]

    @functools.partial(pl.kernel, out_type=jax.ShapeDtypeStruct((n_chunks * SC_ROWS, w), y.dtype), mesh=_sc_mesh(),
                       scratch_types=[pltpu.VMEM((SC_ROWS, w), y.dtype), pltpu.VMEM((1, SC_ROWS), jnp.int32)])
    def kern(y_hbm, i_hbm, o_hbm, ov, iv):
        sid = lax.axis_index("core") * (SC_SUBCORES // 2) + lax.axis_index("subcore")

        @pl.loop(sid, n_chunks, step=SC_SUBCORES)
        def _(c):
            pltpu.sync_copy(i_hbm.at[pl.ds(c, 1)], iv)
            pltpu.sync_copy(y_hbm.at[iv.at[0]], ov)
            pltpu.sync_copy(ov, o_hbm.at[pl.ds(pl.multiple_of(c * SC_ROWS, SC_ROWS), SC_ROWS)])

    return kern(y, idx)


def _moe_rows_kernel(be_ref, nu_ref, x_ref, wgu_ref, wdn_ref, y_ref):
    @pl.when(pl.program_id(0) < nu_ref[0])
    def _():
        half = D_MODEL // 2
        lo, hi = _unpack_halves(x_ref[...])
        gu = _mm(lo, wgu_ref[0:half, :]) + _mm(hi, wgu_ref[half:, :])
        y_ref[...] = _pack_halves(_mm(_swiglu_act(gu), wdn_ref[...]))


def _moe_rows(xs, block_expert, n_used, lw, layer, n_blocks, rows):
    half = D_MODEL // 2
    last = lambda b, be, nu: jnp.minimum(b, nu[0] - 1)
    return pl.pallas_call(
        _moe_rows_kernel,
        grid_spec=pltpu.PrefetchScalarGridSpec(
            num_scalar_prefetch=2,
            grid=(n_blocks,),
            in_specs=[pl.BlockSpec((rows, half), lambda b, be, nu: (last(b, be, nu), 0)),
                      pl.BlockSpec((None, None, D_MODEL, 2 * D_EXPERT),
                                   lambda b, be, nu: (layer, be[last(b, be, nu)], 0, 0)),
                      pl.BlockSpec((None, None, D_EXPERT, D_MODEL),
                                   lambda b, be, nu: (layer, be[last(b, be, nu)], 0, 0))],
            out_specs=pl.BlockSpec((rows, half), lambda b, be, nu: (last(b, be, nu), 0)),
        ),
        out_shape=jax.ShapeDtypeStruct(xs.shape, jnp.int32),
        compiler_params=pltpu.CompilerParams(dimension_semantics=("arbitrary",),
                                             vmem_limit_bytes=VMEM_LIMIT),
        name="moe_rows",
    )(block_expert, n_used, xs, lw["moe_w_gu"], lw["moe_w_down"])


def _moe_combine_kernel(yg_ref, ew_ref, h_ref, x1_ref, mod_ref, sgu_ref, sdn_ref, nf_ref, o_ref, *, final):
    half = D_MODEL // 2
    lo, hi = _unpack_halves(h_ref[...])
    gu = _mm(lo, sgu_ref[0:half, :]) + _mm(hi, sgu_ref[half:, :])
    acc = _mm(_swiglu_act(gu), sdn_ref[...])
    ew = ew_ref[...]
    acc_lo = acc[:, :half]
    acc_hi = acc[:, half:]
    for k in range(TOP_K):
        lo, hi = _unpack_halves(yg_ref[k])
        wk = ew[:, k:k + 1]
        acc_lo = acc_lo + wk * lo
        acc_hi = acc_hi + wk * hi
    g2 = mod_ref[0][:, 5 * D_MODEL:]
    x2 = x1_ref[...] + g2 * jnp.concatenate([acc_lo, acc_hi], axis=1)
    if final:
        x2 = _rms(x2, nf_ref[...])
    o_ref[...] = x2


def _moe_combine(yg, row0, ew, h2p, x1, mods, lw, layer, norm_f, tm, seq_len, mod_base, final):
    n = x1.shape[0]
    half = D_MODEL // 2
    tiles_per_seq = seq_len // tm if mod_base else 1
    tile0 = row0 // tm

    def mod_idx(i):
        return (layer, mod_base + i // tiles_per_seq if mod_base else 0, 0, 0)

    row = lambda w: pl.BlockSpec((tm, w), lambda i: (i, 0))
    return pl.pallas_call(
        functools.partial(_moe_combine_kernel, final=final),
        grid=(n // tm,),
        in_specs=[pl.BlockSpec((TOP_K, tm, half), lambda i: (0, tile0 + i, 0)), row(LANES), row(half), row(D_MODEL),
                  pl.BlockSpec((None, 1, 1, 6 * D_MODEL), mod_idx),
                  _layer_spec(lw["shared_w_gu"], layer), _layer_spec(lw["shared_w_down"], layer),
                  _const_spec(norm_f)],
        out_specs=row(D_MODEL),
        out_shape=jax.ShapeDtypeStruct((n, D_MODEL), F32),
        compiler_params=pltpu.CompilerParams(dimension_semantics=("arbitrary",),
                                             vmem_limit_bytes=VMEM_LIMIT),
        name="moe_combine_final" if final else "moe_combine",
    )(yg, ew, h2p, x1, mods, lw["shared_w_gu"], lw["shared_w_down"], norm_f)


def _moe_experts(h2p, eid, rank, counts, lw, layer, group_sizes):
    n = h2p.shape[0]
    dest, block_expert, n_used, n_blocks = _dispatch_plan(eid, rank, counts, n, MOE_ROWS)
    xs = _sc_dispatch(h2p, dest, n_blocks * MOE_ROWS)
    y = _moe_rows(xs, block_expert, n_used, lw, layer, n_blocks, MOE_ROWS)
    outs, start = [], 0
    for size in group_sizes:
        idx = dest[:, start:start + size].reshape(size * TOP_K // SC_ROWS, SC_ROWS)
        outs.append(_sc_gather(y, idx).reshape(TOP_K, size, D_MODEL // 2))
        start += size
    return outs


def _constants():
    idx = np.arange(MIX_W)
    same_head = (idx[:, None] // HEAD_DIM) == (idx[None, :] // HEAD_DIM)
    maskbd = jnp.asarray(same_head, BF16)
    eexp = np.zeros((LANES, 4 * MIX_W), np.float32)
    for blk in range(4):
        kind, d = divmod(blk, 2)
        for h in range(HEADS):
            src = kind * 2 * HEADS + d * HEADS + h
            eexp[src, blk * MIX_W + h * HEAD_DIM: blk * MIX_W + (h + 1) * HEAD_DIM] = 1.0
    tri = np.triu(np.ones((POST_TM, POST_TM), np.float32), 1)
    return {"maskbd": maskbd, "ones_bd": maskbd, "eexp": jnp.asarray(eexp, BF16), "tri_tokens": jnp.asarray(tri, BF16)}


def _rope_tables(n):
    rows = n // GRID_W
    row = jnp.repeat(jnp.arange(rows, dtype=F32), GRID_W)
    col = jnp.tile(jnp.arange(GRID_W, dtype=F32), rows)
    axis_dim = MLA_ROPE // 2
    inv = jnp.power(ROPE_BASE, -jnp.arange(0, axis_dim, 2, dtype=F32) / axis_dim)
    ang_r = row[:, None] * inv
    ang_c = col[:, None] * inv
    cr, sr, cc, sc = jnp.cos(ang_r), jnp.sin(ang_r), jnp.cos(ang_c), jnp.sin(ang_c)
    zeros = jnp.zeros((n, LANES - MLA_ROPE), F32)
    cos_t = jnp.concatenate([cr, cr, cc, cc, zeros], axis=1)
    sin_t = jnp.concatenate([-sr, sr, -sc, sc, zeros], axis=1)
    return cos_t, sin_t


def _stacked_weights(p):
    w_uq = p["mla_w_uq"].reshape(DEPTH, Q_LORA, MLA_HEADS, MLA_NOPE + MLA_ROPE)
    zq = jnp.zeros((DEPTH, Q_LORA, MLA_HEADS, QH_W - MLA_NOPE - MLA_ROPE), F32)
    w_uq_a = jnp.concatenate([w_uq, zq], axis=-1).reshape(DEPTH, Q_LORA, MLA_HEADS * QH_W).astype(BF16)
    w_uq_sw = jnp.concatenate([jnp.zeros((DEPTH, Q_LORA, MLA_HEADS, MLA_NOPE), F32),
                               w_uq[..., MLA_NOPE + _ROPE_SWAP], zq], axis=-1)
    w_uq_sw = w_uq_sw.reshape(DEPTH, Q_LORA, MLA_HEADS * QH_W).astype(BF16)

    def per_direction(w):
        half = jnp.zeros((DEPTH, 64, MIX_W), F32)
        return jnp.stack([jnp.concatenate([w[:, 0], half], axis=1),
                          jnp.concatenate([half, w[:, 1]], axis=1)], axis=1).astype(BF16)

    row = lambda v: v.reshape(DEPTH, 1, -1)
    pad_row = lambda v: jnp.pad(row(v), ((0, 0), (0, 0), (0, LANES - 2 * HEADS)))
    return {
        "norm1": row(p["norm1"]),
        "w_in": p["w_in"],
        "q_norm": row(p["mla_q_norm"]),
        "w_uq": w_uq_a, "w_uq_sw": w_uq_sw,
        "kv_norm": row(p["mla_kv_norm"]),
        "w_ukv": p["mla_w_ukv"].astype(BF16),
        "gdn_conv": p["gdn_conv"],
        "gdn_alog": pad_row(p["gdn_a_log"]),
        "gdn_dtb": pad_row(p["gdn_dt_bias"]),
        "gdn_norm": jnp.tile(row(p["gdn_norm"]), (1, 1, HEADS)),
        "rwkv_mu_prev": row(p["rwkv_mu_prev"]),
        "rwkv_mu_next": row(p["rwkv_mu_next"]),
        "rwkv_w0": p["rwkv_w0"],
        "rwkv_w2": per_direction(p["rwkv_w2"]),
        "rwkv_a0": p["rwkv_a0"],
        "rwkv_a2": per_direction(p["rwkv_a2"]),
        "rwkv_g2": p["rwkv_g2"].astype(BF16),
        "rwkv_k_k": row(p["rwkv_k_k"]),
        "rwkv_k_a": row(p["rwkv_k_a"]),
        "rwkv_r_k": row(p["rwkv_r_k"]),
        "rwkv_gn_w": row(p["rwkv_gn_w"]),
        "rwkv_gn_b": row(p["rwkv_gn_b"]),
        "w_out": p["w_out"].astype(BF16),
        "norm2": row(p["norm2"]),
        "router_t": jnp.swapaxes(p["moe_router"], 1, 2),
        "router_b": p["moe_bias"].reshape(DEPTH, N_EXPERTS, 1),
        "moe_w_gu": p["moe_w_gu"],
        "moe_w_down": p["moe_w_down"],
        "shared_w_gu": p["shared_w_gu"].astype(BF16),
        "shared_w_down": p["shared_w_down"].astype(BF16),
    }


def _embed_block_diag(s):
    b = s.shape[0]
    eye = jnp.eye(HEADS, dtype=s.dtype)
    out = jnp.einsum("bdhkv,hg->bdhkgv", s, eye)
    return out.reshape(b, 2, MIX_W, MIX_W)


def _layer_front(x2d, mods, lw, l, consts, seq_len, mod_base, cache, tm, tq, counts_in):
    zm, zg, zab, zr = _inproj(x2d, mods, lw, l, tm, seq_len, mod_base)
    if cache is None:
        o_mla, ckv = _mla(zm, lw, l, seq_len, tq)
        o_gdn, s_gdn = _gdn(zg, zab, lw, l, consts, seq_len)
        o_rwkv, s_rwkv = _rwkv(zr, lw, l, consts, seq_len)
        s_rwkv = jnp.swapaxes(s_rwkv, -1, -2)
    else:
        cckv, ckpe, rc, rs, sg, sr = cache
        o_mla, ckv = _mla(zm, lw, l, seq_len, tq, (cckv, ckpe, rc, rs))
        o_gdn, s_gdn = _gdn(zg, zab, lw, l, consts, seq_len, sg)
        o_rwkv, s_rwkv = _rwkv(zr, lw, l, consts, seq_len, sr)
    routed = _post(x2d, o_mla, o_gdn, o_rwkv, mods, lw, l, consts, tm, seq_len, mod_base, counts_in)
    kpe0 = Q_LORA + KV_LORA
    return routed, (ckv, zm[:, kpe0:kpe0 + MLA_ROPE], s_gdn, s_rwkv)


def kernel(x_prompt, x_sample, cache_mla_ckv, cache_mla_kpe, state_gdn, state_rwkv, c, c_ctx, ada_w, ada_b, norm1, w_in, mla_q_norm, mla_w_uq, mla_kv_norm, mla_w_ukv, gdn_conv, gdn_a_log, gdn_dt_bias, gdn_norm, rwkv_mu_prev, rwkv_mu_next, rwkv_w0, rwkv_w2, rwkv_a0, rwkv_a2, rwkv_g2, rwkv_k_k, rwkv_k_a, rwkv_r_k, rwkv_gn_w, rwkv_gn_b, w_out, norm2, moe_router, moe_bias, moe_w_gu, moe_w_down, shared_w_gu, shared_w_down, norm_f):
    p = dict(norm1=norm1, w_in=w_in, mla_q_norm=mla_q_norm, mla_w_uq=mla_w_uq, mla_kv_norm=mla_kv_norm,
             mla_w_ukv=mla_w_ukv, gdn_conv=gdn_conv, gdn_a_log=gdn_a_log, gdn_dt_bias=gdn_dt_bias,
             gdn_norm=gdn_norm, rwkv_mu_prev=rwkv_mu_prev, rwkv_mu_next=rwkv_mu_next, rwkv_w0=rwkv_w0,
             rwkv_w2=rwkv_w2, rwkv_a0=rwkv_a0, rwkv_a2=rwkv_a2, rwkv_g2=rwkv_g2, rwkv_k_k=rwkv_k_k,
             rwkv_k_a=rwkv_k_a, rwkv_r_k=rwkv_r_k, rwkv_gn_w=rwkv_gn_w, rwkv_gn_b=rwkv_gn_b, w_out=w_out,
             norm2=norm2, moe_router=moe_router, moe_bias=moe_bias, moe_w_gu=moe_w_gu, moe_w_down=moe_w_down,
             shared_w_gu=shared_w_gu, shared_w_down=shared_w_down)
    weights = _stacked_weights(p)
    consts = _constants()
    nf = norm_f.reshape(1, D_MODEL)
    b_ctx, t_ctx, _ = x_prompt.shape
    b_lat, t_lat, _ = x_sample.shape

    cvec8 = jnp.concatenate([c_ctx[None, :], c, jnp.zeros((8 - 1 - b_lat, D_MODEL), F32)], axis=0)
    mods = _adaln(cvec8, ada_w, ada_b)
    mods = mods.reshape(DEPTH, 8, 1, 6 * D_MODEL)

    rc, rs = _rope_tables(t_lat)
    ckpe = jnp.pad(cache_mla_kpe, ((0, 0), (0, 0), (0, 0), (0, LANES - MLA_ROPE)))
    cache = (cache_mla_ckv, ckpe, rc, rs, _embed_block_diag_layers(state_gdn),
             _embed_block_diag_layers(jnp.swapaxes(state_rwkv, -1, -2)))
    xp = x_prompt.reshape(b_ctx * t_ctx, D_MODEL)
    xs = x_sample.reshape(b_lat * t_lat, D_MODEL)
    n_ctx = xp.shape[0]
    tm = POST_TM
    ctx_outs = []
    for l in range(DEPTH):
        final = l == DEPTH - 1
        no_pairs = jnp.zeros((N_EXPERTS, LANES), F32)
        (x1c, hc, eidc, rankc, ewc, cnt_c), outs = _layer_front(xp, mods, weights, l, consts, t_ctx, 0, None,
                                                                tm, t_ctx, no_pairs)
        (x1s, hs, eids, ranks, ews, cnt), _ = _layer_front(xs, mods, weights, l, consts, t_lat, 1, cache,
                                                           tm, 256, cnt_c)
        ctx_outs.append(outs)
        yg_c, yg_s = _moe_experts(jnp.concatenate([hc, hs], axis=0), jnp.concatenate([eidc, eids], axis=1),
                                  jnp.concatenate([rankc, ranks], axis=1), cnt, weights, l,
                                  (n_ctx, xs.shape[0]))
        xp = _moe_combine(yg_c, 0, ewc, hc, x1c, mods, weights, l, nf, tm, t_ctx, 0, final)
        xs = _moe_combine(yg_s, 0, ews, hs, x1s, mods, weights, l, nf, tm, t_lat, 1, final)

    y_prompt = xp.reshape(b_ctx, t_ctx, D_MODEL)
    y_sample = xs.reshape(b_lat, t_lat, D_MODEL)
    new_ckv = jnp.stack([o[0].reshape(b_ctx, t_ctx, KV_LORA) for o in ctx_outs], axis=1)
    new_kpe = jnp.stack([o[1].reshape(b_ctx, t_ctx, MLA_ROPE) for o in ctx_outs], axis=1)
    new_gdn = jnp.stack([o[2] for o in ctx_outs], axis=1)
    new_rwkv = jnp.stack([o[3] for o in ctx_outs], axis=1)
    return (y_prompt, y_sample, new_ckv, new_kpe, new_gdn, new_rwkv)


def _embed_block_diag_layers(s):
    b = s.shape[0]
    return _embed_block_diag(s.reshape(b * DEPTH, 2, HEADS, HEAD_DIM, HEAD_DIM)).reshape(
        b, DEPTH, 2, MIX_W, MIX_W)
```

```python
import functools

import numpy as np
import jax
import jax.numpy as jnp
from jax import lax
from jax.experimental import pallas as pl
from jax.experimental.pallas import tpu as pltpu
from jax.experimental.pallas import tpu_sc as plsc

F32 = jnp.float32
BF16 = jnp.bfloat16

D_MODEL = 1024
BATCH = 32
SEQ = 256
DEPTH = 2
DEC_BATCH = 2
DEC_SEQ = 1024
PAST_LEN = 512
GRID_W = 64
NORM_EPS = 1e-6

MLA_HEADS = 4
MLA_NOPE = 128
MLA_ROPE = 64
MLA_V = 128
Q_LORA = 384
KV_LORA = 256
ROPE_BASE = 10000.0
MLA_SCALE = (MLA_NOPE + MLA_ROPE) ** -0.5

HEADS = 4
HEAD_DIM = 64
MIX_W = HEADS * HEAD_DIM
GDN_CONV_CH = 3 * MIX_W
CHUNK = 64
RWKV_GN_EPS = 64e-5

N_EXPERTS = 64
TOP_K = 8
N_GROUPS = 8
GROUP_SIZE = N_EXPERTS // N_GROUPS
TOPK_GROUPS = 4
D_EXPERT = 256
ROUTE_SCALE = 2.5

P_MLA = Q_LORA + KV_LORA + MLA_ROPE
P_GDN = GDN_CONV_CH + MIX_W + 4 * HEADS
P_RWKV = 3 * MIX_W + 128 + 128 + 128

LANES = 128
ZM_W = Q_LORA + KV_LORA + 2 * LANES
ZG_W = GDN_CONV_CH + MIX_W
ZR_W = P_RWKV
QH_W = 2 * LANES
VMEM_LIMIT = 56 * 1024 * 1024
POST_TM = 512

_ROPE_SWAP = np.concatenate([np.arange(16, 32), np.arange(0, 16), np.arange(48, 64), np.arange(32, 48)])


def _sigmoid(x):
    return 1.0 / (1.0 + jnp.exp(-x))


def _silu(x):
    return x * _sigmoid(x)


def _softplus(x):
    return jnp.maximum(x, 0.0) + jnp.log(1.0 + jnp.exp(-jnp.abs(x)))


def _rms(x, g, eps=NORM_EPS):
    return x * lax.rsqrt(jnp.mean(x * x, axis=-1, keepdims=True) + eps) * g


def _mm(a, b):
    return jnp.dot(a.astype(BF16), b.astype(BF16), preferred_element_type=F32)


def _mm_nt(a, b):
    return lax.dot_general(a.astype(BF16), b.astype(BF16), (((1,), (1,)), ((), ())),
                           preferred_element_type=F32)


def _mm_tn(a, b):
    return lax.dot_general(a.astype(BF16), b.astype(BF16), (((0,), (0,)), ((), ())),
                           preferred_element_type=F32)


def _split3(x):
    p1 = x.astype(BF16)
    r1 = x - p1.astype(F32)
    p2 = r1.astype(BF16)
    r2 = r1 - p2.astype(F32)
    return p1, p2, r2.astype(BF16)


def _mm_sel_l(sel, x):
    p1, p2, p3 = _split3(x)
    return _mm(sel, p1) + _mm(sel, p2) + _mm(sel, p3)


def _mm_sel_r(x, sel):
    p1, p2, p3 = _split3(x)
    return _mm(p1, sel) + _mm(p2, sel) + _mm(p3, sel)


def _iota(shape, dim):
    return lax.broadcasted_iota(jnp.int32, shape, dim)


def _layer_spec(a, layer, **kw):
    nd = a.ndim - 1
    return pl.BlockSpec((None,) + a.shape[1:], lambda *_: (layer,) + (0,) * nd, **kw)


def _const_spec(a, **kw):
    return pl.BlockSpec(a.shape, lambda *_: (0,) * a.ndim, **kw)


def _bd(x, maskbd):
    xb = x.astype(BF16)
    return jnp.concatenate([xb] * HEADS, axis=0) * maskbd


def _chunk_masks(rev):
    row = _iota((CHUNK, MIX_W), 0)
    col = jnp.bitwise_and(_iota((CHUNK, MIX_W), 1), HEAD_DIM - 1)
    r2 = _iota((CHUNK, CHUNK), 0)
    c2 = _iota((CHUNK, CHUNK), 1)
    if rev:
        inc, strict, tri = row <= col, row < col, r2 <= c2
    else:
        inc, strict, tri = row >= col, row > col, r2 >= c2
    eye = jnp.where(row == col, 1.0, 0.0).astype(F32)
    return inc, strict, jnp.where(tri, 1.0, 0.0).astype(BF16), eye


def _split2(x):
    hi = x.astype(BF16)
    return hi, (x - hi.astype(F32)).astype(BF16)


def _mm_bd3(x, p, maskbd):
    n = x.shape[0]
    xh, xl = _split2(x)
    ph, pl_ = _split2(p)
    r = jnp.dot(jnp.concatenate([xh, xl], axis=0), _bd(ph, maskbd), preferred_element_type=F32)
    return r[:n] + r[n:] + jnp.dot(xh, _bd(pl_, maskbd), preferred_element_type=F32)


def _neumann_inverse(a_list, eye_list, maskbd):
    bs = [-a for a in a_list]
    ms = [eye + b for eye, b in zip(eye_list, bs)]
    ps = [_mm_bd3(b, b, maskbd) for b in bs]
    for _ in range(4):
        boths = [_mm_bd3(jnp.concatenate([m, p], axis=0), p, maskbd) for m, p in zip(ms, ps)]
        ms = [m + both[:CHUNK] for m, both in zip(ms, boths)]
        ps = [both[CHUNK:] for both in boths]
    return [m + _mm_bd3(m, p, maskbd) for m, p in zip(ms, ps)]


def _adaln_kernel(c_ref, w_ref, b_ref, o_ref):
    cv = c_ref[...]
    o_ref[0] = _mm(_silu(cv), w_ref[0]) + b_ref[0]


def _adaln(cvec8, ada_w, ada_b):
    tn = 768
    n_out = 6 * D_MODEL
    return pl.pallas_call(
        _adaln_kernel,
        grid=(DEPTH, n_out // tn),
        in_specs=[
            pl.BlockSpec((8, D_MODEL), lambda l, j: (0, 0)),
            pl.BlockSpec((1, D_MODEL, tn), lambda l, j: (l, 0, j)),
            pl.BlockSpec((1, 1, tn), lambda l, j: (l, 0, j)),
        ],
        out_specs=pl.BlockSpec((1, 8, tn), lambda l, j: (l, 0, j)),
        out_shape=jax.ShapeDtypeStruct((DEPTH, 8, n_out), F32),
        compiler_params=pltpu.CompilerParams(dimension_semantics=("arbitrary", "arbitrary"),
                                             vmem_limit_bytes=VMEM_LIMIT),
        name="adaln",
    )(cvec8, ada_w, ada_b.reshape(DEPTH, 1, n_out))


_KPE0 = Q_LORA + KV_LORA
_W_IN_MOVES = (
    [(0, 0, P_MLA)]
    + [(P_MLA + LANES - MLA_ROPE + 16 * j, _KPE0 + 16 * int(_ROPE_SWAP[16 * j] // 16), 16) for j in range(4)]
    + [(ZM_W, P_MLA, ZG_W), (ZM_W + ZG_W, P_MLA + ZG_W, 4 * HEADS), (ZM_W + ZG_W + LANES, P_MLA + P_GDN, P_RWKV)]
)
W_IN_PAD = ZM_W + ZG_W + LANES + ZR_W


def _inproj_kernel(x_ref, mod_ref, n1_ref, w_ref, zm_ref, zg_ref, zab_ref, zr_ref, w_s):
    @pl.when(pl.program_id(0) == 0)
    def _():
        w_s[...] = jnp.zeros(w_s.shape, BF16)
        for dst, src, width in _W_IN_MOVES:
            w_s[:, dst:dst + width] = w_ref[:, src:src + width].astype(BF16)

    m = mod_ref[0]
    sh = m[:, 0:D_MODEL]
    sc = m[:, D_MODEL:2 * D_MODEL]
    h = _rms(x_ref[...], n1_ref[...]) * (1.0 + sc) + sh
    z = _mm(h, w_s[...])
    o1 = ZM_W
    o2 = o1 + ZG_W
    o3 = o2 + LANES
    zm_ref[...] = z[:, :o1]
    zg_ref[...] = z[:, o1:o2]
    zab_ref[...] = z[:, o2:o3]
    zr_ref[...] = z[:, o3:]


def _inproj(x2d, mods, lw, layer, tm, seq_len, mod_base):
    n = x2d.shape[0]
    tiles_per_seq = seq_len // tm if mod_base else 1

    def mod_idx(i):
        return (layer, mod_base + i // tiles_per_seq if mod_base else 0, 0, 0)

    return pl.pallas_call(
        _inproj_kernel,
        grid=(n // tm,),
        in_specs=[
            pl.BlockSpec((tm, D_MODEL), lambda i: (i, 0)),
            pl.BlockSpec((None, 1, 1, 6 * D_MODEL), mod_idx),
            _layer_spec(lw["norm1"], layer),
            _layer_spec(lw["w_in"], layer, pipeline_mode=pl.Buffered(1)),
        ],
        out_specs=[
            pl.BlockSpec((tm, ZM_W), lambda i: (i, 0)),
            pl.BlockSpec((tm, ZG_W), lambda i: (i, 0)),
            pl.BlockSpec((tm, LANES), lambda i: (i, 0)),
            pl.BlockSpec((tm, ZR_W), lambda i: (i, 0)),
        ],
        out_shape=[
            jax.ShapeDtypeStruct((n, ZM_W), F32),
            jax.ShapeDtypeStruct((n, ZG_W), F32),
            jax.ShapeDtypeStruct((n, LANES), F32),
            jax.ShapeDtypeStruct((n, ZR_W), F32),
        ],
        scratch_shapes=[pltpu.VMEM((D_MODEL, W_IN_PAD), BF16)],
        compiler_params=pltpu.CompilerParams(dimension_semantics=("arbitrary",),
                                             vmem_limit_bytes=VMEM_LIMIT),
        name="inproj",
    )(x2d, mods, lw["norm1"], lw["w_in"])


def _mla_kernel(*refs, seq_len, tq, past, cached):
    if cached:
        (zm_ref, cckv_ref, ckpe_ref, rc_ref, rs_ref, qn_ref, wuq_ref, wuqs_ref, kvn_ref, wukv_ref,
         o_ref, ckv_ref, k_s, v_s) = refs
    else:
        (zm_ref, qn_ref, wuq_ref, kvn_ref, wukv_ref, o_ref, ckv_ref, k_s, v_s) = refs
    qi = pl.program_id(1)
    o_kpe = Q_LORA + KV_LORA

    @pl.when(qi == 0)
    def _():
        zm = zm_ref[...]
        ckv = _rms(zm[:, Q_LORA:o_kpe], kvn_ref[...])
        ckv_ref[...] = ckv
        kpe = zm[:, o_kpe:o_kpe + LANES]
        if cached:
            kpe = kpe * rc_ref[...] + zm[:, o_kpe + LANES:o_kpe + 2 * LANES] * rs_ref[...]
            kvc = _mm(cckv_ref[0], wukv_ref[...])
            kpc = ckpe_ref[0].astype(BF16)
        kv = _mm(ckv, wukv_ref[...])
        kpe = kpe.astype(BF16)
        for h in range(MLA_HEADS):
            c0 = h * QH_W
            if cached:
                k_s[0:past, c0:c0 + LANES] = kvc[:, c0:c0 + LANES].astype(BF16)
                k_s[0:past, c0 + LANES:c0 + QH_W] = kpc
                v_s[0:past, h * MLA_V:(h + 1) * MLA_V] = kvc[:, c0 + LANES:c0 + QH_W].astype(BF16)
            k_s[past:past + seq_len, c0:c0 + LANES] = kv[:, c0:c0 + LANES].astype(BF16)
            k_s[past:past + seq_len, c0 + LANES:c0 + QH_W] = kpe
            v_s[past:past + seq_len, h * MLA_V:(h + 1) * MLA_V] = kv[:, c0 + LANES:c0 + QH_W].astype(BF16)

    r0 = pl.multiple_of(qi * tq, tq)
    zq = zm_ref[pl.ds(r0, tq), :]
    cq = _rms(zq[:, :Q_LORA], qn_ref[...])
    q = _mm(cq, wuq_ref[...])
    if cached:
        qs = _mm(cq, wuqs_ref[...])
        ones = jnp.ones((tq, LANES), F32)
        zeros = jnp.zeros((tq, LANES), F32)
        qc = jnp.concatenate([ones, rc_ref[pl.ds(r0, tq), :]], axis=1)
        qsn = jnp.concatenate([zeros, rs_ref[pl.ds(r0, tq), :]], axis=1)
    for h in range(MLA_HEADS):
        c0 = h * QH_W
        qh = q[:, c0:c0 + QH_W]
        if cached:
            qh = qh * qc + qs[:, c0:c0 + QH_W] * qsn
        s = _mm_nt(qh, k_s[:, c0:c0 + QH_W]) * MLA_SCALE
        e = jnp.exp(s - jnp.max(s, axis=-1, keepdims=True))
        den = jnp.sum(e, axis=-1, keepdims=True)
        o_ref[:, h * MLA_V:(h + 1) * MLA_V] = _mm(e, v_s[:, h * MLA_V:(h + 1) * MLA_V]) / den


def _mla(zm, lw, layer, seq_len, tq, cache=None):
    n = zm.shape[0]
    nb = n // seq_len
    cached = cache is not None
    past = PAST_LEN if cached else 0
    tk = past + seq_len
    seq_spec = lambda w: pl.BlockSpec((seq_len, w), lambda b, q: (b, 0))
    lay = lambda a: _layer_spec(a, layer)
    if cached:
        cckv, ckpe, rc, rs = cache
        args = [zm, cckv, ckpe, rc, rs, lw["q_norm"], lw["w_uq"], lw["w_uq_sw"], lw["kv_norm"], lw["w_ukv"]]
        in_specs = [seq_spec(ZM_W),
                    pl.BlockSpec((1, None, past, KV_LORA), lambda b, q: (b, layer, 0, 0)),
                    pl.BlockSpec((1, None, past, LANES), lambda b, q: (b, layer, 0, 0)),
                    _const_spec(rc), _const_spec(rs)] + [lay(a) for a in args[5:]]
    else:
        args = [zm, lw["q_norm"], lw["w_uq"], lw["kv_norm"], lw["w_ukv"]]
        in_specs = [seq_spec(ZM_W)] + [lay(a) for a in args[1:]]
    return pl.pallas_call(
        functools.partial(_mla_kernel, seq_len=seq_len, tq=tq, past=past, cached=cached),
        grid=(nb, seq_len // tq),
        in_specs=in_specs,
        out_specs=[
            pl.BlockSpec((tq, MLA_HEADS * MLA_V), lambda b, q: (b * (seq_len // tq) + q, 0)),
            pl.BlockSpec((seq_len, KV_LORA), lambda b, q: (b, 0)),
        ],
        out_shape=[
            jax.ShapeDtypeStruct((n, MLA_HEADS * MLA_V), F32),
            jax.ShapeDtypeStruct((n, KV_LORA), F32),
        ],
        scratch_shapes=[
            pltpu.VMEM((tk, MLA_HEADS * QH_W), BF16),
            pltpu.VMEM((tk, MLA_HEADS * MLA_V), BF16),
        ],
        compiler_params=pltpu.CompilerParams(dimension_semantics=("arbitrary", "arbitrary"),
                                             vmem_limit_bytes=VMEM_LIMIT),
        name="mla_lat" if cached else "mla_ctx",
    )(*args)


SEQ_PER_STEP = 4
CHUNK_GROUP = 4
GDN_PRE_W = 5 * MIX_W


def _for_chunk_groups(n_chunks, fn):
    if n_chunks == CHUNK_GROUP:
        fn(0)
    else:
        def body(gi, carry):
            fn(gi * CHUNK_GROUP)
            return carry
        lax.fori_loop(0, n_chunks // CHUNK_GROUP, body, 0)


def _gdn_prepare(items, maskbd):
    n = range(len(items))
    qs, ks, vs, gs, betas, masks, revs = zip(*items)
    gcs = [_mm_sel_l(masks[i][2], gs[i]) for i in n]
    decays = []
    for i in n:
        inc, eye = masks[i][0], masks[i][3]
        gc_row = jnp.sum(eye * gcs[i], axis=0, keepdims=True)
        decays.append(jnp.where(inc, jnp.exp(jnp.where(inc, gcs[i] - gc_row, 0.0)), 0.0))
    kbs = [ks[i] * betas[i] for i in n]
    aqs = [_mm_nt(jnp.concatenate([kbs[i], qs[i]], axis=0), _bd(ks[i], maskbd)) for i in n]
    a_mats = [jnp.where(masks[i][1], aqs[i][:CHUNK] * decays[i], 0.0) for i in n]
    t_invs = _neumann_inverse(a_mats, [m[3] for m in masks], maskbd)
    egcs = [jnp.exp(gc) for gc in gcs]
    uws = [_mm(t_invs[i], jnp.concatenate([_bd(vs[i] * betas[i], maskbd), _bd(kbs[i] * egcs[i], maskbd)], axis=1))
           for i in n]
    out = []
    for i in n:
        g_last = gcs[i][0:1] if revs[i] else gcs[i][CHUNK - 1:CHUNK]
        pre = jnp.concatenate([uws[i], qs[i] * egcs[i], aqs[i][CHUNK:] * decays[i],
                               ks[i] * jnp.exp(g_last - gcs[i])], axis=1)
        out.append((pre, jnp.broadcast_to(jnp.exp(g_last), (8, MIX_W))))
    return out


def _gdn_step(items, maskbd):
    w = MIX_W
    n = range(len(items))
    pres, egls, states = zip(*items)
    wqs = [_mm(jnp.concatenate([pres[i][:, w:2 * w], pres[i][:, 2 * w:3 * w]], axis=0), states[i]) for i in n]
    v_news = [pres[i][:, :w] - wqs[i][:CHUNK] for i in n]
    outs = [wqs[i][CHUNK:] + _mm(pres[i][:, 3 * w:4 * w], _bd(v_news[i], maskbd)) for i in n]
    upds = [_mm_tn(pres[i][:, 4 * w:], v_news[i]) for i in n]
    mask_f = maskbd.astype(F32)
    return [(outs[i], states[i] * egls[i] + upds[i] * mask_f) for i in n]


def _gdn_kernel(*refs, seq_len, n_seq, cached):
    if cached:
        (zg_ref, zab_ref, s0_ref, conv_ref, alog_ref, dtb_ref, gn_ref, eexp_ref, ones_ref, maskbd_ref,
         o_ref, sout_ref, q_s, k_s, v_s, ge_s, pre_s, gl_s, oacc_s, st_s) = refs
    else:
        (zg_ref, zab_ref, conv_ref, alog_ref, dtb_ref, gn_ref, eexp_ref, ones_ref, maskbd_ref,
         o_ref, sout_ref, q_s, k_s, v_s, ge_s, pre_s, gl_s, oacc_s, st_s) = refs
    t = seq_len * n_seq
    z = zg_ref[:, :GDN_CONV_CH]
    rowi = jnp.bitwise_and(_iota((t, 1), 0), seq_len - 1)
    zp = jnp.where(rowi == 0, 0.0, pltpu.roll(z, 1, 0))
    zn = jnp.where(rowi == seq_len - 1, 0.0, pltpu.roll(z, t - 1, 0))
    cw = conv_ref[...]
    qkv = _silu(zp * cw[0:1] + z * cw[1:2] + zn * cw[2:3])
    ones_bd = ones_ref[...]
    q = qkv[:, :MIX_W]
    k = qkv[:, MIX_W:2 * MIX_W]
    q_s[...] = q * lax.rsqrt(_mm_sel_r(q * q, ones_bd) + 1e-6) * (HEAD_DIM ** -0.5)
    k_s[...] = k * lax.rsqrt(_mm_sel_r(k * k, ones_bd) + 1e-6)
    v_s[...] = qkv[:, 2 * MIX_W:]
    ab = zab_ref[...]
    lane = _iota((t, LANES), 1)
    gb = jnp.where(lane < 2 * HEADS, -jnp.exp(alog_ref[...]) * _softplus(ab + dtb_ref[...]), _sigmoid(ab))
    ge_s[...] = _mm_sel_r(gb, eexp_ref[...])
    oacc_s[...] = jnp.zeros((t, MIX_W), F32)
    if cached:
        st_s[...] = s0_ref[...]
    else:
        st_s[...] = jnp.zeros((n_seq, 2, MIX_W, MIX_W), F32)
    maskbd = maskbd_ref[...]
    masks = (_chunk_masks(False), _chunk_masks(True))
    n_chunks = seq_len // CHUNK

    def prepare_group(c0):
        where, items = [], []
        for j in range(CHUNK_GROUP):
            c = c0 + j
            rows = pl.ds(pl.multiple_of(c * CHUNK, CHUNK), CHUNK)
            for d in range(2):
                where.append((d, c, rows))
                items.append((q_s[rows, :], k_s[rows, :], v_s[rows, :], ge_s[rows, d * MIX_W:(d + 1) * MIX_W],
                              ge_s[rows, (2 + d) * MIX_W:(3 + d) * MIX_W], masks[d], d == 1))
        for (d, c, rows), (pre, egl) in zip(where, _gdn_prepare(items, maskbd)):
            pre_s[d, rows, :] = pre
            gl_s[d, pl.ds(pl.multiple_of(c * 8, 8), 8), :] = egl

    _for_chunk_groups(n_chunks * n_seq, prepare_group)

    def body(i, carry):
        where, items = [], []
        for q in range(n_seq):
            for d in range(2):
                c = q * n_chunks + (i if d == 0 else n_chunks - 1 - i)
                rows = pl.ds(pl.multiple_of(c * CHUNK, CHUNK), CHUNK)
                where.append((q, d, rows))
                items.append((pre_s[d, rows, :], gl_s[d, pl.ds(pl.multiple_of(c * 8, 8), 8), :][0:1], st_s[q, d]))
        for (q, d, rows), (o, s_new) in zip(where, _gdn_step(items, maskbd)):
            oacc_s[rows, :] = oacc_s[rows, :] + o
            st_s[q, d] = s_new
        return carry

    lax.fori_loop(0, n_chunks, body, 0)
    o = oacc_s[...]
    ms = _mm_sel_r(o * o, ones_bd) * (1.0 / HEAD_DIM)
    gate = zg_ref[:, GDN_CONV_CH:]
    o_ref[...] = o * lax.rsqrt(ms + NORM_EPS) * gn_ref[...] * _silu(gate)
    for q in range(n_seq):
        for d in range(2):
            for h in range(HEADS):
                sl = slice(h * HEAD_DIM, (h + 1) * HEAD_DIM)
                sout_ref[q, d, h] = st_s[q, d, sl, sl]


def _gdn(zg, zab, lw, layer, consts, seq_len, s0_bd=None):
    n = zg.shape[0]
    nb = n // seq_len
    cached = s0_bd is not None
    n_seq = 1 if cached else SEQ_PER_STEP
    rows = seq_len * n_seq
    args = [zg, zab]
    in_specs = [pl.BlockSpec((rows, ZG_W), lambda b: (b, 0)),
                pl.BlockSpec((rows, LANES), lambda b: (b, 0))]
    if cached:
        args.append(s0_bd)
        in_specs.append(pl.BlockSpec((1, None, 2, MIX_W, MIX_W), lambda b: (b, layer, 0, 0, 0)))
    layered = [lw["gdn_conv"], lw["gdn_alog"], lw["gdn_dtb"], lw["gdn_norm"]]
    const = [consts["eexp"], consts["ones_bd"], consts["maskbd"]]
    args += layered + const
    in_specs += [_layer_spec(a, layer) for a in layered] + [_const_spec(a) for a in const]
    return pl.pallas_call(
        functools.partial(_gdn_kernel, seq_len=seq_len, n_seq=n_seq, cached=cached),
        grid=(nb // n_seq,),
        in_specs=in_specs,
        out_specs=[
            pl.BlockSpec((rows, MIX_W), lambda b: (b, 0)),
            pl.BlockSpec((n_seq, 2, HEADS, HEAD_DIM, HEAD_DIM), lambda b: (b, 0, 0, 0, 0)),
        ],
        out_shape=[
            jax.ShapeDtypeStruct((n, MIX_W), F32),
            jax.ShapeDtypeStruct((nb, 2, HEADS, HEAD_DIM, HEAD_DIM), F32),
        ],
        scratch_shapes=[
            pltpu.VMEM((rows, MIX_W), F32),
            pltpu.VMEM((rows, MIX_W), F32),
            pltpu.VMEM((rows, MIX_W), F32),
            pltpu.VMEM((rows, 4 * MIX_W), F32),
            pltpu.VMEM((2, rows, GDN_PRE_W), F32),
            pltpu.VMEM((2, rows // CHUNK * 8, MIX_W), F32),
            pltpu.VMEM((rows, MIX_W), F32),
            pltpu.VMEM((n_seq, 2, MIX_W, MIX_W), F32),
        ],
        compiler_params=pltpu.CompilerParams(dimension_semantics=("arbitrary",),
                                             vmem_limit_bytes=VMEM_LIMIT),
        name="gdn_lat" if cached else "gdn_ctx",
    )(*args)


RWKV_PRE_W = 7 * MIX_W


def _rwkv_prepare(items, maskbd, eye_full):
    n = range(len(items))
    rs, kds, vs, kks, bs, lws, masks, revs = zip(*items)
    cums = [_mm_sel_l(masks[i][2], lws[i]) for i in n]
    einvs = [jnp.exp(-c) for c in cums]
    kts = [kks[i] * jnp.exp(cums[i] - lws[i]) for i in n]
    rts = [rs[i] * jnp.exp(cums[i]) for i in n]
    krs = [jnp.concatenate([kts[i], rts[i]], axis=0) for i in n]
    lb_alls = [_mm_nt(krs[i], _bd(bs[i] * einvs[i], maskbd)) for i in n]
    lk_alls = [_mm_nt(krs[i], _bd(kds[i] * einvs[i], maskbd)) for i in n]
    lbs = [jnp.where(masks[i][1], lb_alls[i][:CHUNK], 0.0) for i in n]
    t_invs = _neumann_inverse(lbs, [m[3] for m in masks], maskbd)
    lvs = [_mm(jnp.concatenate([jnp.where(masks[i][1], lk_alls[i][:CHUNK], 0.0),
                                jnp.where(masks[i][0], lk_alls[i][CHUNK:], 0.0)], axis=0), _bd(vs[i], maskbd))
           for i in n]
    tkps = [_mm(t_invs[i], jnp.concatenate([_bd(kts[i], maskbd), _bd(lvs[i][:CHUNK], maskbd)], axis=1)) for i in n]
    out = []
    for i in n:
        c_last = cums[i][0:1] if revs[i] else cums[i][CHUNK - 1:CHUNK]
        tail = jnp.exp(c_last - cums[i])
        rb = jnp.where(masks[i][0], lb_alls[i][CHUNK:], 0.0)
        pre = jnp.concatenate([tkps[i][:, :MIX_W], rts[i], tkps[i][:, MIX_W:], lvs[i][CHUNK:], rb,
                               kds[i] * tail, bs[i] * tail], axis=1)
        gcol = jnp.sum(eye_full * jnp.exp(c_last), axis=1, keepdims=True)
        out.append((pre, jnp.broadcast_to(gcol, (MIX_W, LANES))))
    return out


def _rwkv_step(items, maskbd):
    w = MIX_W
    n = range(len(items))
    pres, vs, gcols, states = zip(*items)
    prs = [_mm(jnp.concatenate([pres[i][:, :w], pres[i][:, w:2 * w]], axis=0), states[i]) for i in n]
    ps = [prs[i][:CHUNK] + pres[i][:, 2 * w:3 * w] for i in n]
    outs = [prs[i][CHUNK:] + pres[i][:, 3 * w:4 * w] - _mm(pres[i][:, 4 * w:5 * w], _bd(ps[i], maskbd)) for i in n]
    upds = [_mm_tn(jnp.concatenate([pres[i][:, 5 * w:6 * w], pres[i][:, 6 * w:]], axis=0),
                   jnp.concatenate([vs[i], -ps[i]], axis=0)) for i in n]
    mask_f = maskbd.astype(F32)
    return [(outs[i], states[i] * jnp.concatenate([gcols[i], gcols[i]], axis=1) + upds[i] * mask_f) for i in n]


def _rwkv_kernel(*refs, seq_len, n_seq, cached):
    if cached:
        (zr_ref, s0_ref, mup_ref, mun_ref, w0_ref, w2_ref, a0_ref, a2_ref, g2_ref, kk_ref, ka_ref, rk_ref,
         gnw_ref, gnb_ref, ones_ref, maskbd_ref, o_ref, sout_ref,
         r_s, v_s, kk_s, dir_s, bg_s, pre_s, gcol_s, oacc_s, st_s) = refs
    else:
        (zr_ref, mup_ref, mun_ref, w0_ref, w2_ref, a0_ref, a2_ref, g2_ref, kk_ref, ka_ref, rk_ref,
         gnw_ref, gnb_ref, ones_ref, maskbd_ref, o_ref, sout_ref,
         r_s, v_s, kk_s, dir_s, bg_s, pre_s, gcol_s, oacc_s, st_s) = refs
    t = seq_len * n_seq
    z = zr_ref[...]
    rowi = jnp.bitwise_and(_iota((t, 1), 0), seq_len - 1)
    zp = jnp.where(rowi == 0, 0.0, pltpu.roll(z, 1, 0))
    zn = jnp.where(rowi == seq_len - 1, 0.0, pltpu.roll(z, t - 1, 0))
    z = z + mup_ref[...] * (zp - z) + mun_ref[...] * (zn - z)
    w = MIX_W
    r = z[:, :w]
    k = z[:, w:2 * w]
    v = z[:, 2 * w:3 * w]
    wd = jnp.tanh(z[:, 3 * w:3 * w + LANES])
    ad = z[:, 3 * w + LANES:3 * w + 2 * LANES]
    gd = _sigmoid(z[:, 3 * w + 2 * LANES:])
    ones_bd = ones_ref[...]
    kk = k * kk_ref[...]
    kk = kk * lax.rsqrt(_mm_sel_r(kk * kk, ones_bd) + 1e-6)
    r_s[...] = r
    v_s[...] = v
    kk_s[...] = kk
    bonus = jnp.zeros((t, w), F32)
    for d in range(2):
        w_log = -_softplus(-(w0_ref[d:d + 1] + _mm(wd, w2_ref[d]))) - 0.5
        a = _sigmoid(a0_ref[d:d + 1] + _mm(ad, a2_ref[d]))
        kd = k * (1.0 + (a - 1.0) * ka_ref[...])
        dir_s[:, (3 * d) * w:(3 * d + 1) * w] = -jnp.exp(w_log)
        dir_s[:, (3 * d + 1) * w:(3 * d + 2) * w] = kd
        dir_s[:, (3 * d + 2) * w:(3 * d + 3) * w] = kk * a
        bonus = bonus + _mm_sel_r(r * kd * rk_ref[...], ones_bd) * v
    bg_s[:, :w] = bonus
    bg_s[:, w:] = _mm(gd, g2_ref[...])
    oacc_s[...] = jnp.zeros((t, w), F32)
    if cached:
        st_s[...] = s0_ref[...]
    else:
        st_s[...] = jnp.zeros((n_seq, 2, w, w), F32)
    maskbd = maskbd_ref[...]
    masks = (_chunk_masks(False), _chunk_masks(True))
    eye_full = jnp.where(_iota((w, w), 0) == _iota((w, w), 1), 1.0, 0.0).astype(F32)
    n_chunks = seq_len // CHUNK

    def prepare_group(c0):
        where, items = [], []
        for j in range(CHUNK_GROUP):
            c = c0 + j
            rows = pl.ds(pl.multiple_of(c * CHUNK, CHUNK), CHUNK)
            for d in range(2):
                where.append((d, c, rows))
                items.append((r_s[rows, :], dir_s[rows, (3 * d + 1) * w:(3 * d + 2) * w], v_s[rows, :], kk_s[rows, :],
                              dir_s[rows, (3 * d + 2) * w:(3 * d + 3) * w], dir_s[rows, (3 * d) * w:(3 * d + 1) * w],
                              masks[d], d == 1))
        for (d, c, rows), (pre, gcol) in zip(where, _rwkv_prepare(items, maskbd, eye_full)):
            pre_s[d, rows, :] = pre
            gcol_s[d, pl.ds(pl.multiple_of(c * w, w), w), :] = gcol

    _for_chunk_groups(n_chunks * n_seq, prepare_group)

    def body(i, carry):
        where, items = [], []
        for q in range(n_seq):
            for d in range(2):
                c = q * n_chunks + (i if d == 0 else n_chunks - 1 - i)
                rows = pl.ds(pl.multiple_of(c * CHUNK, CHUNK), CHUNK)
                where.append((q, d, rows))
                items.append((pre_s[d, rows, :], v_s[rows, :], gcol_s[d, pl.ds(pl.multiple_of(c * w, w), w), :],
                              st_s[q, d]))
        for (q, d, rows), (o, z_new) in zip(where, _rwkv_step(items, maskbd)):
            oacc_s[rows, :] = oacc_s[rows, :] + o
            st_s[q, d] = z_new
        return carry

    lax.fori_loop(0, n_chunks, body, 0)
    o = oacc_s[...]
    inv_n = 1.0 / HEAD_DIM
    mu = _mm_sel_r(o, ones_bd) * inv_n
    oc = o - mu
    var = _mm_sel_r(oc * oc, ones_bd) * inv_n
    y = oc * lax.rsqrt(var + RWKV_GN_EPS) * gnw_ref[...] + gnb_ref[...]
    o_ref[...] = (y + bg_s[:, :w]) * bg_s[:, w:]
    for q in range(n_seq):
        for d in range(2):
            for h in range(HEADS):
                sl = slice(h * HEAD_DIM, (h + 1) * HEAD_DIM)
                sout_ref[q, d, h] = st_s[q, d, sl, sl]


def _rwkv(zr, lw, layer, consts, seq_len, s0_bd=None):
    n = zr.shape[0]
    nb = n // seq_len
    cached = s0_bd is not None
    n_seq = 1 if cached else SEQ_PER_STEP
    rows = seq_len * n_seq
    args = [zr]
    in_specs = [pl.BlockSpec((rows, ZR_W), lambda b: (b, 0))]
    if cached:
        args.append(s0_bd)
        in_specs.append(pl.BlockSpec((1, None, 2, MIX_W, MIX_W), lambda b: (b, layer, 0, 0, 0)))
    layered = [lw["rwkv_mu_prev"], lw["rwkv_mu_next"], lw["rwkv_w0"], lw["rwkv_w2"], lw["rwkv_a0"], lw["rwkv_a2"],
               lw["rwkv_g2"], lw["rwkv_k_k"], lw["rwkv_k_a"], lw["rwkv_r_k"], lw["rwkv_gn_w"], lw["rwkv_gn_b"]]
    const = [consts["ones_bd"], consts["maskbd"]]
    args += layered + const
    in_specs += [_layer_spec(a, layer) for a in layered] + [_const_spec(a) for a in const]
    return pl.pallas_call(
        functools.partial(_rwkv_kernel, seq_len=seq_len, n_seq=n_seq, cached=cached),
        grid=(nb // n_seq,),
        in_specs=in_specs,
        out_specs=[
            pl.BlockSpec((rows, MIX_W), lambda b: (b, 0)),
            pl.BlockSpec((n_seq, 2, HEADS, HEAD_DIM, HEAD_DIM), lambda b: (b, 0, 0, 0, 0)),
        ],
        out_shape=[
            jax.ShapeDtypeStruct((n, MIX_W), F32),
            jax.ShapeDtypeStruct((nb, 2, HEADS, HEAD_DIM, HEAD_DIM), F32),
        ],
        scratch_shapes=[
            pltpu.VMEM((rows, MIX_W), F32),
            pltpu.VMEM((rows, MIX_W), F32),
            pltpu.VMEM((rows, MIX_W), F32),
            pltpu.VMEM((rows, 6 * MIX_W), F32),
            pltpu.VMEM((rows, 2 * MIX_W), F32),
            pltpu.VMEM((2, rows, RWKV_PRE_W), F32),
            pltpu.VMEM((2, rows // CHUNK * MIX_W, LANES), F32),
            pltpu.VMEM((rows, MIX_W), F32),
            pltpu.VMEM((n_seq, 2, MIX_W, MIX_W), F32),
        ],
        compiler_params=pltpu.CompilerParams(dimension_semantics=("arbitrary",),
                                             vmem_limit_bytes=VMEM_LIMIT),
        name="rwkv_lat" if cached else "rwkv_ctx",
    )(*args)


def _route(logits_t, bias):
    tm = logits_t.shape[1]
    neg = -jnp.inf
    sc = _sigmoid(logits_t)
    sc3 = sc.reshape(N_GROUPS, GROUP_SIZE, tm)
    sel = (sc + bias).reshape(N_GROUPS, GROUP_SIZE, tm)
    si = _iota(sel.shape, 1).astype(F32)
    m1 = jnp.max(sel, axis=1, keepdims=True)
    f1 = jnp.min(jnp.where(sel == m1, si, float(GROUP_SIZE)), axis=1, keepdims=True)
    m2 = jnp.max(jnp.where(si == f1, neg, sel), axis=1, keepdims=True)
    grp = m1 + m2
    gi = _iota(grp.shape, 0).astype(F32)
    gsel = jnp.zeros(grp.shape, F32)
    for _ in range(TOPK_GROUPS):
        mx = jnp.max(grp, axis=0, keepdims=True)
        fi = jnp.min(jnp.where(grp == mx, gi, float(N_GROUPS)), axis=0, keepdims=True)
        hit = gi == fi
        gsel = jnp.where(hit, 1.0, gsel)
        grp = jnp.where(hit, neg, grp)
    cur = jnp.where(gsel > 0.0, sel, neg)
    ei = (_iota(cur.shape, 0) * GROUP_SIZE + _iota(cur.shape, 1)).astype(F32)
    chosen = jnp.zeros(cur.shape, F32)
    ids, wts = [], []
    for _ in range(TOP_K):
        mx = jnp.max(jnp.max(cur, axis=0, keepdims=True), axis=1, keepdims=True)
        fi = jnp.min(jnp.min(jnp.where(cur == mx, ei, float(N_EXPERTS)), axis=0, keepdims=True),
                     axis=1, keepdims=True)
        hit = ei == fi
        chosen = jnp.where(hit, 1.0, chosen)
        cur = jnp.where(hit, neg, cur)
        ids.append(fi.reshape(1, tm))
        wts.append(jnp.sum(jnp.sum(jnp.where(hit, sc3, 0.0), axis=0, keepdims=True), axis=1, keepdims=True)
                   .reshape(1, tm))
    w = jnp.concatenate(wts, axis=0)
    w = w / jnp.sum(w, axis=0, keepdims=True) * ROUTE_SCALE
    return chosen.reshape(N_EXPERTS, tm), jnp.concatenate(ids, axis=0), w


def _pack_halves(x):
    half = x.shape[1] // 2
    bits = lax.bitcast_convert_type(x.astype(BF16).astype(F32), jnp.int32)
    lo = lax.shift_right_logical(bits[:, :half], jnp.int32(16))
    return jnp.bitwise_or(lo, jnp.bitwise_and(bits[:, half:], jnp.int32(-65536)))


def _unpack_halves(word):
    lo = lax.bitcast_convert_type(lax.shift_left(word, jnp.int32(16)), F32)
    hi = lax.bitcast_convert_type(jnp.bitwise_and(word, jnp.int32(-65536)), F32)
    return lo, hi


def _post_kernel(x_ref, om_ref, og_ref, or_ref, mod_ref, wo_ref, n2_ref, rt_ref, rb_ref, tri_ref, cin_ref,
                 x1_ref, h2_ref, eid_ref, rank_ref, ew_ref, cnt_ref, carry_s):
    @pl.when(pl.program_id(0) == 0)
    def _():
        carry_s[...] = cin_ref[...]

    m = mod_ref[0]
    g1 = m[:, 2 * D_MODEL:3 * D_MODEL]
    sh2 = m[:, 3 * D_MODEL:4 * D_MODEL]
    sc2 = m[:, 4 * D_MODEL:5 * D_MODEL]
    w_mla = MLA_HEADS * MLA_V
    mix = (_mm(om_ref[...], wo_ref[0:w_mla, :]) + _mm(og_ref[...], wo_ref[w_mla:w_mla + MIX_W, :])
           + _mm(or_ref[...], wo_ref[w_mla + MIX_W:, :]))
    x1 = x_ref[...] + g1 * mix
    x1_ref[...] = x1
    h2 = _rms(x1, n2_ref[...]) * (1.0 + sc2) + sh2
    h2_ref[...] = _pack_halves(h2)
    r_hi, r_lo = _split2(rt_ref[...])
    h_hi, h_lo = _split2(h2)
    logits_t = _mm_nt(r_hi, h_hi) + _mm_nt(r_hi, h_lo) + _mm_nt(r_lo, h_hi)
    chosen, ids, w = _route(logits_t, rb_ref[...])
    tm = chosen.shape[1]
    rank_et = (carry_s[:, 0:1] + _mm(chosen, tri_ref[...])).reshape(N_GROUPS, GROUP_SIZE, tm)
    ei = (_iota(rank_et.shape, 0) * GROUP_SIZE + _iota(rank_et.shape, 1)).astype(F32)
    ranks = []
    for k in range(TOP_K):
        pick = jnp.where(ei == ids[k:k + 1].reshape(1, 1, tm), rank_et, 0.0)
        ranks.append(jnp.sum(jnp.sum(pick, axis=0, keepdims=True), axis=1, keepdims=True).reshape(1, tm))
    eid_ref[...] = ids.astype(jnp.int32)
    rank_ref[...] = jnp.concatenate(ranks, axis=0).astype(jnp.int32)
    ew_ref[...] = jnp.concatenate([w, jnp.zeros((LANES - TOP_K, tm), F32)], axis=0).T
    total = carry_s[...] + jnp.sum(chosen, axis=1, keepdims=True)
    carry_s[...] = total
    cnt_ref[...] = total


def _post(x2d, om, og, orw, mods, lw, layer, consts, tm, seq_len, mod_base, counts_in):
    n = x2d.shape[0]
    tiles_per_seq = seq_len // tm if mod_base else 1

    def mod_idx(i):
        return (layer, mod_base + i // tiles_per_seq if mod_base else 0, 0, 0)

    row = lambda w: pl.BlockSpec((tm, w), lambda i: (i, 0))
    col = lambda h: pl.BlockSpec((h, tm), lambda i: (0, i))
    full = lambda a: _layer_spec(a, layer)
    tail = [lw["w_out"], lw["norm2"], lw["router_t"], lw["router_b"]]
    tri = consts["tri_tokens"]
    return pl.pallas_call(
        _post_kernel,
        grid=(n // tm,),
        in_specs=[row(D_MODEL), row(MLA_HEADS * MLA_V), row(MIX_W), row(MIX_W),
                  pl.BlockSpec((None, 1, 1, 6 * D_MODEL), mod_idx)] + [full(a) for a in tail]
        + [_const_spec(tri), _const_spec(counts_in)],
        out_specs=[row(D_MODEL), row(D_MODEL // 2), col(TOP_K), col(TOP_K), row(LANES),
                   pl.BlockSpec((N_EXPERTS, LANES), lambda i: (0, 0))],
        out_shape=[
            jax.ShapeDtypeStruct((n, D_MODEL), F32),
            jax.ShapeDtypeStruct((n, D_MODEL // 2), jnp.int32),
            jax.ShapeDtypeStruct((TOP_K, n), jnp.int32),
            jax.ShapeDtypeStruct((TOP_K, n), jnp.int32),
            jax.ShapeDtypeStruct((n, LANES), F32),
            jax.ShapeDtypeStruct((N_EXPERTS, LANES), F32),
        ],
        scratch_shapes=[pltpu.VMEM((N_EXPERTS, LANES), F32)],
        compiler_params=pltpu.CompilerParams(dimension_semantics=("arbitrary",),
                                             vmem_limit_bytes=VMEM_LIMIT),
        name="post",
    )(x2d, om, og, orw, mods, *tail, tri, counts_in)


MOE_ROWS = 512
SC_ROWS = 128
SC_SUBCORES = 32


def _swiglu_act(gu):
    return _silu(gu[:, :D_EXPERT]) * gu[:, D_EXPERT:]


def _dispatch_plan(eid, rank, counts, n, rows):
    n_blocks = n * TOP_K // rows + N_EXPERTS
    cnt = counts[:, 0].astype(jnp.int32)
    blocks = (cnt + rows - 1) // rows
    block_end = jnp.cumsum(blocks)
    offset = (block_end - blocks) * rows
    experts = jnp.arange(N_EXPERTS, dtype=jnp.int32)
    dest = jnp.sum(jnp.where(eid[..., None] == experts, offset, 0), axis=-1) + rank
    block_ids = jnp.arange(n_blocks, dtype=jnp.int32)
    block_expert = jnp.minimum(jnp.sum((block_end[None, :] <= block_ids[:, None]).astype(jnp.int32), axis=1),
                               N_EXPERTS - 1)
    return dest, block_expert, block_end[-1:].astype(jnp.int32), n_blocks


def _sc_mesh():
    return plsc.VectorSubcoreMesh(core_axis_name="core", subcore_axis_name="subcore")


def _sc_dispatch(groups, dest, n_rows):
    w = groups[0].shape[1]
    dtype = groups[0].dtype

    @functools.partial(pl.kernel, out_type=jax.ShapeDtypeStruct((n_rows, w), dtype), mesh=_sc_mesh(),
                       scratch_types=[pltpu.VMEM((SC_ROWS, w), dtype), pltpu.VMEM((TOP_K, SC_ROWS), jnp.int32)])
    def kern(*refs):
        x_refs, d_hbm, o_hbm, xv, dv = refs[:len(groups)], *refs[len(groups):]
        sid = lax.axis_index("core") * (SC_SUBCORES // 2) + lax.axis_index("subcore")
        start = 0
        for x_hbm, x in zip(x_refs, groups):
            def chunk(c, x_hbm=x_hbm, start=start):
                r0 = pl.multiple_of(c * SC_ROWS, SC_ROWS)
                pltpu.sync_copy(x_hbm.at[pl.ds(r0, SC_ROWS)], xv)
                pltpu.sync_copy(d_hbm.at[:, pl.ds(start + r0, SC_ROWS)], dv)
                for k in range(TOP_K):
                    pltpu.sync_copy(xv, o_hbm.at[dv.at[k]])

            pl.loop(sid, x.shape[0] // SC_ROWS, step=SC_SUBCORES)(chunk)
            start += x.shape[0]

    return kern(*groups, dest)


def _sc_gather(y, idx):
    w = y.shape[1]
    n_chunks = idx.shape[0]

    @functools.partial(pl.kernel, out_type=jax.ShapeDtypeStruct((n_chunks * SC_ROWS, w), y.dtype), mesh=_sc_mesh(),
                       scratch_types=[pltpu.VMEM((SC_ROWS, w), y.dtype), pltpu.VMEM((1, SC_ROWS), jnp.int32)])
    def kern(y_hbm, i_hbm, o_hbm, ov, iv):
        sid = lax.axis_index("core") * (SC_SUBCORES // 2) + lax.axis_index("subcore")

        @pl.loop(sid, n_chunks, step=SC_SUBCORES)
        def _(c):
            pltpu.sync_copy(i_hbm.at[pl.ds(c, 1)], iv)
            pltpu.sync_copy(y_hbm.at[iv.at[0]], ov)
            pltpu.sync_copy(ov, o_hbm.at[pl.ds(pl.multiple_of(c * SC_ROWS, SC_ROWS), SC_ROWS)])

    return kern(y, idx)


def _moe_rows_kernel(be_ref, nu_ref, x_ref, wgu_ref, wdn_ref, y_ref):
    @pl.when(pl.program_id(0) < nu_ref[0])
    def _():
        half = D_MODEL // 2
        lo, hi = _unpack_halves(x_ref[...])
        gu = _mm(lo, wgu_ref[0:half, :]) + _mm(hi, wgu_ref[half:, :])
        y_ref[...] = _pack_halves(_mm(_swiglu_act(gu), wdn_ref[...]))


def _moe_rows(xs, block_expert, n_used, lw, layer, n_blocks, rows):
    half = D_MODEL // 2
    last = lambda b, be, nu: jnp.minimum(b, nu[0] - 1)
    return pl.pallas_call(
        _moe_rows_kernel,
        grid_spec=pltpu.PrefetchScalarGridSpec(
            num_scalar_prefetch=2,
            grid=(n_blocks,),
            in_specs=[pl.BlockSpec((rows, half), lambda b, be, nu: (last(b, be, nu), 0)),
                      pl.BlockSpec((None, None, D_MODEL, 2 * D_EXPERT),
                                   lambda b, be, nu: (layer, be[last(b, be, nu)], 0, 0)),
                      pl.BlockSpec((None, None, D_EXPERT, D_MODEL),
                                   lambda b, be, nu: (layer, be[last(b, be, nu)], 0, 0))],
            out_specs=pl.BlockSpec((rows, half), lambda b, be, nu: (last(b, be, nu), 0)),
        ),
        out_shape=jax.ShapeDtypeStruct(xs.shape, jnp.int32),
        compiler_params=pltpu.CompilerParams(dimension_semantics=("arbitrary",),
                                             vmem_limit_bytes=VMEM_LIMIT),
        name="moe_rows",
    )(block_expert, n_used, xs, lw["moe_w_gu"], lw["moe_w_down"])


def _moe_combine_kernel(yg_ref, ew_ref, h_ref, x1_ref, mod_ref, sgu_ref, sdn_ref, nf_ref, o_ref, *, final):
    half = D_MODEL // 2
    lo, hi = _unpack_halves(h_ref[...])
    gu = _mm(lo, sgu_ref[0:half, :]) + _mm(hi, sgu_ref[half:, :])
    acc = _mm(_swiglu_act(gu), sdn_ref[...])
    ew = ew_ref[...]
    acc_lo = acc[:, :half]
    acc_hi = acc[:, half:]
    for k in range(TOP_K):
        lo, hi = _unpack_halves(yg_ref[k])
        wk = ew[:, k:k + 1]
        acc_lo = acc_lo + wk * lo
        acc_hi = acc_hi + wk * hi
    g2 = mod_ref[0][:, 5 * D_MODEL:]
    x2 = x1_ref[...] + g2 * jnp.concatenate([acc_lo, acc_hi], axis=1)
    if final:
        x2 = _rms(x2, nf_ref[...])
    o_ref[...] = x2


def _moe_combine(yg, row0, ew, h2p, x1, mods, lw, layer, norm_f, tm, seq_len, mod_base, final):
    n = x1.shape[0]
    half = D_MODEL // 2
    tiles_per_seq = seq_len // tm if mod_base else 1
    tile0 = row0 // tm

    def mod_idx(i):
        return (layer, mod_base + i // tiles_per_seq if mod_base else 0, 0, 0)

    row = lambda w: pl.BlockSpec((tm, w), lambda i: (i, 0))
    return pl.pallas_call(
        functools.partial(_moe_combine_kernel, final=final),
        grid=(n // tm,),
        in_specs=[pl.BlockSpec((TOP_K, tm, half), lambda i: (0, tile0 + i, 0)), row(LANES), row(half), row(D_MODEL),
                  pl.BlockSpec((None, 1, 1, 6 * D_MODEL), mod_idx),
                  _layer_spec(lw["shared_w_gu"], layer), _layer_spec(lw["shared_w_down"], layer),
                  _const_spec(norm_f)],
        out_specs=row(D_MODEL),
        out_shape=jax.ShapeDtypeStruct((n, D_MODEL), F32),
        compiler_params=pltpu.CompilerParams(dimension_semantics=("arbitrary",),
                                             vmem_limit_bytes=VMEM_LIMIT),
        name="moe_combine_final" if final else "moe_combine",
    )(yg, ew, h2p, x1, mods, lw["shared_w_gu"], lw["shared_w_down"], norm_f)


def _moe_experts(groups, eid, rank, counts, lw, layer):
    group_sizes = [g.shape[0] for g in groups]
    n = sum(group_sizes)
    dest, block_expert, n_used, n_blocks = _dispatch_plan(eid, rank, counts, n, MOE_ROWS)
    xs = _sc_dispatch(groups, dest, n_blocks * MOE_ROWS)
    y = _moe_rows(xs, block_expert, n_used, lw, layer, n_blocks, MOE_ROWS)
    outs, start = [], 0
    for size in group_sizes:
        idx = dest[:, start:start + size].reshape(size * TOP_K // SC_ROWS, SC_ROWS)
        outs.append(_sc_gather(y, idx).reshape(TOP_K, size, D_MODEL // 2))
        start += size
    return outs


def _constants():
    idx = np.arange(MIX_W)
    same_head = (idx[:, None] // HEAD_DIM) == (idx[None, :] // HEAD_DIM)
    maskbd = jnp.asarray(same_head, BF16)
    eexp = np.zeros((LANES, 4 * MIX_W), np.float32)
    for blk in range(4):
        kind, d = divmod(blk, 2)
        for h in range(HEADS):
            src = kind * 2 * HEADS + d * HEADS + h
            eexp[src, blk * MIX_W + h * HEAD_DIM: blk * MIX_W + (h + 1) * HEAD_DIM] = 1.0
    tri = np.triu(np.ones((POST_TM, POST_TM), np.float32), 1)
    return {"maskbd": maskbd, "ones_bd": maskbd, "eexp": jnp.asarray(eexp, BF16), "tri_tokens": jnp.asarray(tri, BF16)}


def _rope_tables(n):
    rows = n // GRID_W
    row = jnp.repeat(jnp.arange(rows, dtype=F32), GRID_W)
    col = jnp.tile(jnp.arange(GRID_W, dtype=F32), rows)
    axis_dim = MLA_ROPE // 2
    inv = jnp.power(ROPE_BASE, -jnp.arange(0, axis_dim, 2, dtype=F32) / axis_dim)
    ang_r = row[:, None] * inv
    ang_c = col[:, None] * inv
    cr, sr, cc, sc = jnp.cos(ang_r), jnp.sin(ang_r), jnp.cos(ang_c), jnp.sin(ang_c)
    zeros = jnp.zeros((n, LANES - MLA_ROPE), F32)
    cos_t = jnp.concatenate([cr, cr, cc, cc, zeros], axis=1)
    sin_t = jnp.concatenate([-sr, sr, -sc, sc, zeros], axis=1)
    return cos_t, sin_t


def _stacked_weights(p):
    w_uq = p["mla_w_uq"].reshape(DEPTH, Q_LORA, MLA_HEADS, MLA_NOPE + MLA_ROPE)
    zq = jnp.zeros((DEPTH, Q_LORA, MLA_HEADS, QH_W - MLA_NOPE - MLA_ROPE), F32)
    w_uq_a = jnp.concatenate([w_uq, zq], axis=-1).reshape(DEPTH, Q_LORA, MLA_HEADS * QH_W).astype(BF16)
    w_uq_sw = jnp.concatenate([jnp.zeros((DEPTH, Q_LORA, MLA_HEADS, MLA_NOPE), F32),
                               w_uq[..., MLA_NOPE + _ROPE_SWAP], zq], axis=-1)
    w_uq_sw = w_uq_sw.reshape(DEPTH, Q_LORA, MLA_HEADS * QH_W).astype(BF16)

    def per_direction(w):
        half = jnp.zeros((DEPTH, 64, MIX_W), F32)
        return jnp.stack([jnp.concatenate([w[:, 0], half], axis=1),
                          jnp.concatenate([half, w[:, 1]], axis=1)], axis=1).astype(BF16)

    row = lambda v: v.reshape(DEPTH, 1, -1)
    pad_row = lambda v: jnp.pad(row(v), ((0, 0), (0, 0), (0, LANES - 2 * HEADS)))
    return {
        "norm1": row(p["norm1"]),
        "w_in": p["w_in"],
        "q_norm": row(p["mla_q_norm"]),
        "w_uq": w_uq_a, "w_uq_sw": w_uq_sw,
        "kv_norm": row(p["mla_kv_norm"]),
        "w_ukv": p["mla_w_ukv"].astype(BF16),
        "gdn_conv": p["gdn_conv"],
        "gdn_alog": pad_row(p["gdn_a_log"]),
        "gdn_dtb": pad_row(p["gdn_dt_bias"]),
        "gdn_norm": jnp.tile(row(p["gdn_norm"]), (1, 1, HEADS)),
        "rwkv_mu_prev": row(p["rwkv_mu_prev"]),
        "rwkv_mu_next": row(p["rwkv_mu_next"]),
        "rwkv_w0": p["rwkv_w0"],
        "rwkv_w2": per_direction(p["rwkv_w2"]),
        "rwkv_a0": p["rwkv_a0"],
        "rwkv_a2": per_direction(p["rwkv_a2"]),
        "rwkv_g2": p["rwkv_g2"].astype(BF16),
        "rwkv_k_k": row(p["rwkv_k_k"]),
        "rwkv_k_a": row(p["rwkv_k_a"]),
        "rwkv_r_k": row(p["rwkv_r_k"]),
        "rwkv_gn_w": row(p["rwkv_gn_w"]),
        "rwkv_gn_b": row(p["rwkv_gn_b"]),
        "w_out": p["w_out"].astype(BF16),
        "norm2": row(p["norm2"]),
        "router_t": jnp.swapaxes(p["moe_router"], 1, 2),
        "router_b": p["moe_bias"].reshape(DEPTH, N_EXPERTS, 1),
        "moe_w_gu": p["moe_w_gu"],
        "moe_w_down": p["moe_w_down"],
        "shared_w_gu": p["shared_w_gu"].astype(BF16),
        "shared_w_down": p["shared_w_down"].astype(BF16),
    }


def _embed_block_diag(s):
    b = s.shape[0]
    eye = jnp.eye(HEADS, dtype=s.dtype)
    out = jnp.einsum("bdhkv,hg->bdhkgv", s, eye)
    return out.reshape(b, 2, MIX_W, MIX_W)


def _layer_front(x2d, mods, lw, l, consts, seq_len, mod_base, cache, tm, tq, counts_in):
    zm, zg, zab, zr = _inproj(x2d, mods, lw, l, tm, seq_len, mod_base)
    if cache is None:
        o_mla, ckv = _mla(zm, lw, l, seq_len, tq)
        o_gdn, s_gdn = _gdn(zg, zab, lw, l, consts, seq_len)
        o_rwkv, s_rwkv = _rwkv(zr, lw, l, consts, seq_len)
        s_rwkv = jnp.swapaxes(s_rwkv, -1, -2)
    else:
        cckv, ckpe, rc, rs, sg, sr = cache
        o_mla, ckv = _mla(zm, lw, l, seq_len, tq, (cckv, ckpe, rc, rs))
        o_gdn, s_gdn = _gdn(zg, zab, lw, l, consts, seq_len, sg)
        o_rwkv, s_rwkv = _rwkv(zr, lw, l, consts, seq_len, sr)
    routed = _post(x2d, o_mla, o_gdn, o_rwkv, mods, lw, l, consts, tm, seq_len, mod_base, counts_in)
    kpe0 = Q_LORA + KV_LORA
    return routed, (ckv, zm[:, kpe0:kpe0 + MLA_ROPE], s_gdn, s_rwkv)


def kernel(x_prompt, x_sample, cache_mla_ckv, cache_mla_kpe, state_gdn, state_rwkv, c, c_ctx, ada_w, ada_b, norm1, w_in, mla_q_norm, mla_w_uq, mla_kv_norm, mla_w_ukv, gdn_conv, gdn_a_log, gdn_dt_bias, gdn_norm, rwkv_mu_prev, rwkv_mu_next, rwkv_w0, rwkv_w2, rwkv_a0, rwkv_a2, rwkv_g2, rwkv_k_k, rwkv_k_a, rwkv_r_k, rwkv_gn_w, rwkv_gn_b, w_out, norm2, moe_router, moe_bias, moe_w_gu, moe_w_down, shared_w_gu, shared_w_down, norm_f):
    p = dict(norm1=norm1, w_in=w_in, mla_q_norm=mla_q_norm, mla_w_uq=mla_w_uq, mla_kv_norm=mla_kv_norm,
             mla_w_ukv=mla_w_ukv, gdn_conv=gdn_conv, gdn_a_log=gdn_a_log, gdn_dt_bias=gdn_dt_bias,
             gdn_norm=gdn_norm, rwkv_mu_prev=rwkv_mu_prev, rwkv_mu_next=rwkv_mu_next, rwkv_w0=rwkv_w0,
             rwkv_w2=rwkv_w2, rwkv_a0=rwkv_a0, rwkv_a2=rwkv_a2, rwkv_g2=rwkv_g2, rwkv_k_k=rwkv_k_k,
             rwkv_k_a=rwkv_k_a, rwkv_r_k=rwkv_r_k, rwkv_gn_w=rwkv_gn_w, rwkv_gn_b=rwkv_gn_b, w_out=w_out,
             norm2=norm2, moe_router=moe_router, moe_bias=moe_bias, moe_w_gu=moe_w_gu, moe_w_down=moe_w_down,
             shared_w_gu=shared_w_gu, shared_w_down=shared_w_down)
    weights = _stacked_weights(p)
    consts = _constants()
    nf = norm_f.reshape(1, D_MODEL)
    b_ctx, t_ctx, _ = x_prompt.shape
    b_lat, t_lat, _ = x_sample.shape

    cvec8 = jnp.concatenate([c_ctx[None, :], c, jnp.zeros((8 - 1 - b_lat, D_MODEL), F32)], axis=0)
    mods = _adaln(cvec8, ada_w, ada_b)
    mods = mods.reshape(DEPTH, 8, 1, 6 * D_MODEL)

    rc, rs = _rope_tables(t_lat)
    ckpe = jnp.pad(cache_mla_kpe, ((0, 0), (0, 0), (0, 0), (0, LANES - MLA_ROPE)))
    cache = (cache_mla_ckv, ckpe, rc, rs, _embed_block_diag_layers(state_gdn),
             _embed_block_diag_layers(jnp.swapaxes(state_rwkv, -1, -2)))
    xp = x_prompt.reshape(b_ctx * t_ctx, D_MODEL)
    xs = x_sample.reshape(b_lat * t_lat, D_MODEL)
    n_ctx = xp.shape[0]
    tm = POST_TM
    ctx_outs = []
    for l in range(DEPTH):
        final = l == DEPTH - 1
        no_pairs = jnp.zeros((N_EXPERTS, LANES), F32)
        (x1c, hc, eidc, rankc, ewc, cnt_c), outs = _layer_front(xp, mods, weights, l, consts, t_ctx, 0, None,
                                                                tm, t_ctx, no_pairs)
        (x1s, hs, eids, ranks, ews, cnt), _ = _layer_front(xs, mods, weights, l, consts, t_lat, 1, cache,
                                                           tm, 256, cnt_c)
        ctx_outs.append(outs)
        yg_c, yg_s = _moe_experts([hc, hs], jnp.concatenate([eidc, eids], axis=1),
                                  jnp.concatenate([rankc, ranks], axis=1), cnt, weights, l)
        xp = _moe_combine(yg_c, 0, ewc, hc, x1c, mods, weights, l, nf, tm, t_ctx, 0, final)
        xs = _moe_combine(yg_s, 0, ews, hs, x1s, mods, weights, l, nf, tm, t_lat, 1, final)

    y_prompt = xp.reshape(b_ctx, t_ctx, D_MODEL)
    y_sample = xs.reshape(b_lat, t_lat, D_MODEL)
    new_ckv = jnp.stack([o[0].reshape(b_ctx, t_ctx, KV_LORA) for o in ctx_outs], axis=1)
    new_kpe = jnp.stack([o[1].reshape(b_ctx, t_ctx, MLA_ROPE) for o in ctx_outs], axis=1)
    new_gdn = jnp.stack([o[2] for o in ctx_outs], axis=1)
    new_rwkv = jnp.stack([o[3] for o in ctx_outs], axis=1)
    return (y_prompt, y_sample, new_ckv, new_kpe, new_gdn, new_rwkv)


def _embed_block_diag_layers(s):
    b = s.shape[0]
    return _embed_block_diag(s.reshape(b * DEPTH, 2, HEADS, HEAD_DIM, HEAD_DIM)).reshape(
        b, DEPTH, 2, MIX_W, MIX_W)
```

```python
import functools

import numpy as np
import jax
import jax.numpy as jnp
from jax import lax
from jax.experimental import pallas as pl
from jax.experimental.pallas import tpu as pltpu
from jax.experimental.pallas import tpu_sc as plsc

F32 = jnp.float32
BF16 = jnp.bfloat16

D_MODEL = 1024
BATCH = 32
SEQ = 256
DEPTH = 2
DEC_BATCH = 2
DEC_SEQ = 1024
PAST_LEN = 512
GRID_W = 64
NORM_EPS = 1e-6

MLA_HEADS = 4
MLA_NOPE = 128
MLA_ROPE = 64
MLA_V = 128
Q_LORA = 384
KV_LORA = 256
ROPE_BASE = 10000.0
MLA_SCALE = (MLA_NOPE + MLA_ROPE) ** -0.5

HEADS = 4
HEAD_DIM = 64
MIX_W = HEADS * HEAD_DIM
GDN_CONV_CH = 3 * MIX_W
CHUNK = 64
RWKV_GN_EPS = 64e-5

N_EXPERTS = 64
TOP_K = 8
N_GROUPS = 8
GROUP_SIZE = N_EXPERTS // N_GROUPS
TOPK_GROUPS = 4
D_EXPERT = 256
ROUTE_SCALE = 2.5

P_MLA = Q_LORA + KV_LORA + MLA_ROPE
P_GDN = GDN_CONV_CH + MIX_W + 4 * HEADS
P_RWKV = 3 * MIX_W + 128 + 128 + 128

LANES = 128
ZM_W = Q_LORA + KV_LORA + 2 * LANES
ZG_W = GDN_CONV_CH + MIX_W
ZR_W = P_RWKV
QH_W = 2 * LANES
VMEM_LIMIT = 56 * 1024 * 1024
POST_TM = 512

_ROPE_SWAP = np.concatenate([np.arange(16, 32), np.arange(0, 16), np.arange(48, 64), np.arange(32, 48)])


def _sigmoid(x):
    return 1.0 / (1.0 + jnp.exp(-x))


def _silu(x):
    return x * _sigmoid(x)


def _softplus(x):
    return jnp.maximum(x, 0.0) + jnp.log(1.0 + jnp.exp(-jnp.abs(x)))


def _rms(x, g, eps=NORM_EPS):
    return x * lax.rsqrt(jnp.mean(x * x, axis=-1, keepdims=True) + eps) * g


def _mm(a, b):
    return jnp.dot(a.astype(BF16), b.astype(BF16), preferred_element_type=F32)


def _mm_nt(a, b):
    return lax.dot_general(a.astype(BF16), b.astype(BF16), (((1,), (1,)), ((), ())),
                           preferred_element_type=F32)


def _mm_tn(a, b):
    return lax.dot_general(a.astype(BF16), b.astype(BF16), (((0,), (0,)), ((), ())),
                           preferred_element_type=F32)


def _split3(x):
    p1 = x.astype(BF16)
    r1 = x - p1.astype(F32)
    p2 = r1.astype(BF16)
    r2 = r1 - p2.astype(F32)
    return p1, p2, r2.astype(BF16)


def _mm_sel_l(sel, x):
    p1, p2, p3 = _split3(x)
    return _mm(sel, p1) + _mm(sel, p2) + _mm(sel, p3)


def _mm_sel_r(x, sel):
    p1, p2, p3 = _split3(x)
    return _mm(p1, sel) + _mm(p2, sel) + _mm(p3, sel)


def _iota(shape, dim):
    return lax.broadcasted_iota(jnp.int32, shape, dim)


def _layer_spec(a, layer, **kw):
    nd = a.ndim - 1
    return pl.BlockSpec((None,) + a.shape[1:], lambda *_: (layer,) + (0,) * nd, **kw)


def _const_spec(a, **kw):
    return pl.BlockSpec(a.shape, lambda *_: (0,) * a.ndim, **kw)


def _bd(x, maskbd):
    xb = x.astype(BF16)
    return jnp.concatenate([xb] * HEADS, axis=0) * maskbd


def _chunk_masks(rev):
    row = _iota((CHUNK, MIX_W), 0)
    col = jnp.bitwise_and(_iota((CHUNK, MIX_W), 1), HEAD_DIM - 1)
    r2 = _iota((CHUNK, CHUNK), 0)
    c2 = _iota((CHUNK, CHUNK), 1)
    if rev:
        inc, strict, tri = row <= col, row < col, r2 <= c2
    else:
        inc, strict, tri = row >= col, row > col, r2 >= c2
    eye = jnp.where(row == col, 1.0, 0.0).astype(F32)
    return inc, strict, jnp.where(tri, 1.0, 0.0).astype(BF16), eye


def _split2(x):
    hi = x.astype(BF16)
    return hi, (x - hi.astype(F32)).astype(BF16)


def _mm_bd3(x, p, maskbd):
    n = x.shape[0]
    xh, xl = _split2(x)
    ph, pl_ = _split2(p)
    r = jnp.dot(jnp.concatenate([xh, xl], axis=0), _bd(ph, maskbd), preferred_element_type=F32)
    return r[:n] + r[n:] + jnp.dot(xh, _bd(pl_, maskbd), preferred_element_type=F32)


def _neumann_inverse(a_list, eye_list, maskbd):
    bs = [-a for a in a_list]
    ms = [eye + b for eye, b in zip(eye_list, bs)]
    ps = [_mm_bd3(b, b, maskbd) for b in bs]
    for _ in range(4):
        boths = [_mm_bd3(jnp.concatenate([m, p], axis=0), p, maskbd) for m, p in zip(ms, ps)]
        ms = [m + both[:CHUNK] for m, both in zip(ms, boths)]
        ps = [both[CHUNK:] for both in boths]
    return [m + _mm_bd3(m, p, maskbd) for m, p in zip(ms, ps)]


def _adaln_kernel(c_ref, w_ref, b_ref, o_ref):
    cv = c_ref[...]
    o_ref[0] = _mm(_silu(cv), w_ref[0]) + b_ref[0]


def _adaln(cvec8, ada_w, ada_b):
    tn = 768
    n_out = 6 * D_MODEL
    return pl.pallas_call(
        _adaln_kernel,
        grid=(DEPTH, n_out // tn),
        in_specs=[
            pl.BlockSpec((8, D_MODEL), lambda l, j: (0, 0)),
            pl.BlockSpec((1, D_MODEL, tn), lambda l, j: (l, 0, j)),
            pl.BlockSpec((1, 1, tn), lambda l, j: (l, 0, j)),
        ],
        out_specs=pl.BlockSpec((1, 8, tn), lambda l, j: (l, 0, j)),
        out_shape=jax.ShapeDtypeStruct((DEPTH, 8, n_out), F32),
        compiler_params=pltpu.CompilerParams(dimension_semantics=("arbitrary", "arbitrary"),
                                             vmem_limit_bytes=VMEM_LIMIT),
        name="adaln",
    )(cvec8, ada_w, ada_b.reshape(DEPTH, 1, n_out))


_KPE0 = Q_LORA + KV_LORA
_W_IN_MOVES = (
    [(0, 0, P_MLA)]
    + [(P_MLA + LANES - MLA_ROPE + 16 * j, _KPE0 + 16 * int(_ROPE_SWAP[16 * j] // 16), 16) for j in range(4)]
    + [(ZM_W, P_MLA, ZG_W), (ZM_W + ZG_W, P_MLA + ZG_W, 4 * HEADS), (ZM_W + ZG_W + LANES, P_MLA + P_GDN, P_RWKV)]
)
W_IN_PAD = ZM_W + ZG_W + LANES + ZR_W


def _inproj_kernel(x_ref, mod_ref, n1_ref, w_ref, zm_ref, zg_ref, zab_ref, zr_ref, w_s):
    @pl.when(pl.program_id(0) == 0)
    def _():
        w_s[...] = jnp.zeros(w_s.shape, BF16)
        for dst, src, width in _W_IN_MOVES:
            w_s[:, dst:dst + width] = w_ref[:, src:src + width].astype(BF16)

    m = mod_ref[0]
    sh = m[:, 0:D_MODEL]
    sc = m[:, D_MODEL:2 * D_MODEL]
    h = _rms(x_ref[...], n1_ref[...]) * (1.0 + sc) + sh
    z = _mm(h, w_s[...])
    o1 = ZM_W
    o2 = o1 + ZG_W
    o3 = o2 + LANES
    zm_ref[...] = z[:, :o1]
    zg_ref[...] = z[:, o1:o2]
    zab_ref[...] = z[:, o2:o3]
    zr_ref[...] = z[:, o3:]


def _inproj(x2d, mods, lw, layer, tm, seq_len, mod_base):
    n = x2d.shape[0]
    tiles_per_seq = seq_len // tm if mod_base else 1

    def mod_idx(i):
        return (layer, mod_base + i // tiles_per_seq if mod_base else 0, 0, 0)

    return pl.pallas_call(
        _inproj_kernel,
        grid=(n // tm,),
        in_specs=[
            pl.BlockSpec((tm, D_MODEL), lambda i: (i, 0)),
            pl.BlockSpec((None, 1, 1, 6 * D_MODEL), mod_idx),
            _layer_spec(lw["norm1"], layer),
            _layer_spec(lw["w_in"], layer, pipeline_mode=pl.Buffered(1)),
        ],
        out_specs=[
            pl.BlockSpec((tm, ZM_W), lambda i: (i, 0)),
            pl.BlockSpec((tm, ZG_W), lambda i: (i, 0)),
            pl.BlockSpec((tm, LANES), lambda i: (i, 0)),
            pl.BlockSpec((tm, ZR_W), lambda i: (i, 0)),
        ],
        out_shape=[
            jax.ShapeDtypeStruct((n, ZM_W), F32),
            jax.ShapeDtypeStruct((n, ZG_W), F32),
            jax.ShapeDtypeStruct((n, LANES), F32),
            jax.ShapeDtypeStruct((n, ZR_W), F32),
        ],
        scratch_shapes=[pltpu.VMEM((D_MODEL, W_IN_PAD), BF16)],
        compiler_params=pltpu.CompilerParams(dimension_semantics=("arbitrary",),
                                             vmem_limit_bytes=VMEM_LIMIT),
        name="inproj",
    )(x2d, mods, lw["norm1"], lw["w_in"])


def _mla_kernel(*refs, seq_len, tq, past, cached):
    if cached:
        (zm_ref, cckv_ref, ckpe_ref, rc_ref, rs_ref, qn_ref, wuq_ref, wuqs_ref, kvn_ref, wukv_ref,
         o_ref, ckv_ref, k_s, v_s) = refs
    else:
        (zm_ref, qn_ref, wuq_ref, kvn_ref, wukv_ref, o_ref, ckv_ref, k_s, v_s) = refs
    qi = pl.program_id(1)
    o_kpe = Q_LORA + KV_LORA

    @pl.when(qi == 0)
    def _():
        zm = zm_ref[...]
        ckv = _rms(zm[:, Q_LORA:o_kpe], kvn_ref[...])
        ckv_ref[...] = ckv
        kpe = zm[:, o_kpe:o_kpe + LANES]
        if cached:
            kpe = kpe * rc_ref[...] + zm[:, o_kpe + LANES:o_kpe + 2 * LANES] * rs_ref[...]
            kvc = _mm(cckv_ref[0], wukv_ref[...])
            kpc = ckpe_ref[0].astype(BF16)
        kv = _mm(ckv, wukv_ref[...])
        kpe = kpe.astype(BF16)
        for h in range(MLA_HEADS):
            c0 = h * QH_W
            if cached:
                k_s[0:past, c0:c0 + LANES] = kvc[:, c0:c0 + LANES].astype(BF16)
                k_s[0:past, c0 + LANES:c0 + QH_W] = kpc
                v_s[0:past, h * MLA_V:(h + 1) * MLA_V] = kvc[:, c0 + LANES:c0 + QH_W].astype(BF16)
            k_s[past:past + seq_len, c0:c0 + LANES] = kv[:, c0:c0 + LANES].astype(BF16)
            k_s[past:past + seq_len, c0 + LANES:c0 + QH_W] = kpe
            v_s[past:past + seq_len, h * MLA_V:(h + 1) * MLA_V] = kv[:, c0 + LANES:c0 + QH_W].astype(BF16)

    r0 = pl.multiple_of(qi * tq, tq)
    zq = zm_ref[pl.ds(r0, tq), :]
    cq = _rms(zq[:, :Q_LORA], qn_ref[...])
    q = _mm(cq, wuq_ref[...])
    if cached:
        qs = _mm(cq, wuqs_ref[...])
        ones = jnp.ones((tq, LANES), F32)
        zeros = jnp.zeros((tq, LANES), F32)
        qc = jnp.concatenate([ones, rc_ref[pl.ds(r0, tq), :]], axis=1)
        qsn = jnp.concatenate([zeros, rs_ref[pl.ds(r0, tq), :]], axis=1)
    for h in range(MLA_HEADS):
        c0 = h * QH_W
        qh = q[:, c0:c0 + QH_W]
        if cached:
            qh = qh * qc + qs[:, c0:c0 + QH_W] * qsn
        s = _mm_nt(qh, k_s[:, c0:c0 + QH_W]) * MLA_SCALE
        e = jnp.exp(s - jnp.max(s, axis=-1, keepdims=True))
        den = jnp.sum(e, axis=-1, keepdims=True)
        o_ref[:, h * MLA_V:(h + 1) * MLA_V] = _mm(e, v_s[:, h * MLA_V:(h + 1) * MLA_V]) / den


def _mla(zm, lw, layer, seq_len, tq, cache=None):
    n = zm.shape[0]
    nb = n // seq_len
    cached = cache is not None
    past = PAST_LEN if cached else 0
    tk = past + seq_len
    seq_spec = lambda w: pl.BlockSpec((seq_len, w), lambda b, q: (b, 0))
    lay = lambda a: _layer_spec(a, layer)
    if cached:
        cckv, ckpe, rc, rs = cache
        args = [zm, cckv, ckpe, rc, rs, lw["q_norm"], lw["w_uq"], lw["w_uq_sw"], lw["kv_norm"], lw["w_ukv"]]
        in_specs = [seq_spec(ZM_W),
                    pl.BlockSpec((1, None, past, KV_LORA), lambda b, q: (b, layer, 0, 0)),
                    pl.BlockSpec((1, None, past, LANES), lambda b, q: (b, layer, 0, 0)),
                    _const_spec(rc), _const_spec(rs)] + [lay(a) for a in args[5:]]
    else:
        args = [zm, lw["q_norm"], lw["w_uq"], lw["kv_norm"], lw["w_ukv"]]
        in_specs = [seq_spec(ZM_W)] + [lay(a) for a in args[1:]]
    return pl.pallas_call(
        functools.partial(_mla_kernel, seq_len=seq_len, tq=tq, past=past, cached=cached),
        grid=(nb, seq_len // tq),
        in_specs=in_specs,
        out_specs=[
            pl.BlockSpec((tq, MLA_HEADS * MLA_V), lambda b, q: (b * (seq_len // tq) + q, 0)),
            pl.BlockSpec((seq_len, KV_LORA), lambda b, q: (b, 0)),
        ],
        out_shape=[
            jax.ShapeDtypeStruct((n, MLA_HEADS * MLA_V), F32),
            jax.ShapeDtypeStruct((n, KV_LORA), F32),
        ],
        scratch_shapes=[
            pltpu.VMEM((tk, MLA_HEADS * QH_W), BF16),
            pltpu.VMEM((tk, MLA_HEADS * MLA_V), BF16),
        ],
        compiler_params=pltpu.CompilerParams(dimension_semantics=("arbitrary", "arbitrary"),
                                             vmem_limit_bytes=VMEM_LIMIT),
        name="mla_lat" if cached else "mla_ctx",
    )(*args)


SEQ_PER_STEP = 4
CHUNK_GROUP = 4
GDN_PRE_W = 5 * MIX_W


def _for_chunk_groups(n_chunks, fn):
    if n_chunks == CHUNK_GROUP:
        fn(0)
    else:
        def body(gi, carry):
            fn(gi * CHUNK_GROUP)
            return carry
        lax.fori_loop(0, n_chunks // CHUNK_GROUP, body, 0)


def _gdn_prepare(items, maskbd):
    n = range(len(items))
    qs, ks, vs, gs, betas, masks, revs = zip(*items)
    gcs = [_mm_sel_l(masks[i][2], gs[i]) for i in n]
    decays = []
    for i in n:
        inc, eye = masks[i][0], masks[i][3]
        gc_row = jnp.sum(eye * gcs[i], axis=0, keepdims=True)
        decays.append(jnp.where(inc, jnp.exp(jnp.where(inc, gcs[i] - gc_row, 0.0)), 0.0))
    kbs = [ks[i] * betas[i] for i in n]
    aqs = [_mm_nt(jnp.concatenate([kbs[i], qs[i]], axis=0), _bd(ks[i], maskbd)) for i in n]
    a_mats = [jnp.where(masks[i][1], aqs[i][:CHUNK] * decays[i], 0.0) for i in n]
    t_invs = _neumann_inverse(a_mats, [m[3] for m in masks], maskbd)
    egcs = [jnp.exp(gc) for gc in gcs]
    uws = [_mm(t_invs[i], jnp.concatenate([_bd(vs[i] * betas[i], maskbd), _bd(kbs[i] * egcs[i], maskbd)], axis=1))
           for i in n]
    out = []
    for i in n:
        g_last = gcs[i][0:1] if revs[i] else gcs[i][CHUNK - 1:CHUNK]
        pre = jnp.concatenate([uws[i], qs[i] * egcs[i], aqs[i][CHUNK:] * decays[i],
                               ks[i] * jnp.exp(g_last - gcs[i])], axis=1)
        out.append((pre, jnp.broadcast_to(jnp.exp(g_last), (8, MIX_W))))
    return out


def _gdn_step(items, maskbd):
    w = MIX_W
    n = range(len(items))
    pres, egls, states = zip(*items)
    wqs = [_mm(jnp.concatenate([pres[i][:, w:2 * w], pres[i][:, 2 * w:3 * w]], axis=0), states[i]) for i in n]
    v_news = [pres[i][:, :w] - wqs[i][:CHUNK] for i in n]
    outs = [wqs[i][CHUNK:] + _mm(pres[i][:, 3 * w:4 * w], _bd(v_news[i], maskbd)) for i in n]
    upds = [_mm_tn(pres[i][:, 4 * w:], v_news[i]) for i in n]
    mask_f = maskbd.astype(F32)
    return [(outs[i], states[i] * egls[i] + upds[i] * mask_f) for i in n]


def _gdn_kernel(*refs, seq_len, n_seq, cached):
    if cached:
        (zg_ref, zab_ref, s0_ref, conv_ref, alog_ref, dtb_ref, gn_ref, eexp_ref, ones_ref, maskbd_ref,
         o_ref, sout_ref, q_s, k_s, v_s, ge_s, pre_s, gl_s, oacc_s, st_s) = refs
    else:
        (zg_ref, zab_ref, conv_ref, alog_ref, dtb_ref, gn_ref, eexp_ref, ones_ref, maskbd_ref,
         o_ref, sout_ref, q_s, k_s, v_s, ge_s, pre_s, gl_s, oacc_s, st_s) = refs
    t = seq_len * n_seq
    z = zg_ref[:, :GDN_CONV_CH]
    rowi = jnp.bitwise_and(_iota((t, 1), 0), seq_len - 1)
    zp = jnp.where(rowi == 0, 0.0, pltpu.roll(z, 1, 0))
    zn = jnp.where(rowi == seq_len - 1, 0.0, pltpu.roll(z, t - 1, 0))
    cw = conv_ref[...]
    qkv = _silu(zp * cw[0:1] + z * cw[1:2] + zn * cw[2:3])
    ones_bd = ones_ref[...]
    q = qkv[:, :MIX_W]
    k = qkv[:, MIX_W:2 * MIX_W]
    q_s[...] = q * lax.rsqrt(_mm_sel_r(q * q, ones_bd) + 1e-6) * (HEAD_DIM ** -0.5)
    k_s[...] = k * lax.rsqrt(_mm_sel_r(k * k, ones_bd) + 1e-6)
    v_s[...] = qkv[:, 2 * MIX_W:]
    ab = zab_ref[...]
    lane = _iota((t, LANES), 1)
    gb = jnp.where(lane < 2 * HEADS, -jnp.exp(alog_ref[...]) * _softplus(ab + dtb_ref[...]), _sigmoid(ab))
    ge_s[...] = _mm_sel_r(gb, eexp_ref[...])
    oacc_s[...] = jnp.zeros((t, MIX_W), F32)
    if cached:
        st_s[...] = s0_ref[...]
    else:
        st_s[...] = jnp.zeros((n_seq, 2, MIX_W, MIX_W), F32)
    maskbd = maskbd_ref[...]
    masks = (_chunk_masks(False), _chunk_masks(True))
    n_chunks = seq_len // CHUNK

    def prepare_group(c0):
        where, items = [], []
        for j in range(CHUNK_GROUP):
            c = c0 + j
            rows = pl.ds(pl.multiple_of(c * CHUNK, CHUNK), CHUNK)
            for d in range(2):
                where.append((d, c, rows))
                items.append((q_s[rows, :], k_s[rows, :], v_s[rows, :], ge_s[rows, d * MIX_W:(d + 1) * MIX_W],
                              ge_s[rows, (2 + d) * MIX_W:(3 + d) * MIX_W], masks[d], d == 1))
        for (d, c, rows), (pre, egl) in zip(where, _gdn_prepare(items, maskbd)):
            pre_s[d, rows, :] = pre
            gl_s[d, pl.ds(pl.multiple_of(c * 8, 8), 8), :] = egl

    _for_chunk_groups(n_chunks * n_seq, prepare_group)

    def body(i, carry):
        where, items = [], []
        for q in range(n_seq):
            for d in range(2):
                c = q * n_chunks + (i if d == 0 else n_chunks - 1 - i)
                rows = pl.ds(pl.multiple_of(c * CHUNK, CHUNK), CHUNK)
                where.append((q, d, rows))
                items.append((pre_s[d, rows, :], gl_s[d, pl.ds(pl.multiple_of(c * 8, 8), 8), :][0:1], st_s[q, d]))
        for (q, d, rows), (o, s_new) in zip(where, _gdn_step(items, maskbd)):
            oacc_s[rows, :] = oacc_s[rows, :] + o
            st_s[q, d] = s_new
        return carry

    lax.fori_loop(0, n_chunks, body, 0)
    o = oacc_s[...]
    ms = _mm_sel_r(o * o, ones_bd) * (1.0 / HEAD_DIM)
    gate = zg_ref[:, GDN_CONV_CH:]
    o_ref[...] = o * lax.rsqrt(ms + NORM_EPS) * gn_ref[...] * _silu(gate)
    for q in range(n_seq):
        for d in range(2):
            for h in range(HEADS):
                sl = slice(h * HEAD_DIM, (h + 1) * HEAD_DIM)
                sout_ref[q, d, h] = st_s[q, d, sl, sl]


def _gdn(zg, zab, lw, layer, consts, seq_len, s0_bd=None):
    n = zg.shape[0]
    nb = n // seq_len
    cached = s0_bd is not None
    n_seq = 1 if cached else SEQ_PER_STEP
    rows = seq_len * n_seq
    args = [zg, zab]
    in_specs = [pl.BlockSpec((rows, ZG_W), lambda b: (b, 0)),
                pl.BlockSpec((rows, LANES), lambda b: (b, 0))]
    if cached:
        args.append(s0_bd)
        in_specs.append(pl.BlockSpec((1, None, 2, MIX_W, MIX_W), lambda b: (b, layer, 0, 0, 0)))
    layered = [lw["gdn_conv"], lw["gdn_alog"], lw["gdn_dtb"], lw["gdn_norm"]]
    const = [consts["eexp"], consts["ones_bd"], consts["maskbd"]]
    args += layered + const
    in_specs += [_layer_spec(a, layer) for a in layered] + [_const_spec(a) for a in const]
    return pl.pallas_call(
        functools.partial(_gdn_kernel, seq_len=seq_len, n_seq=n_seq, cached=cached),
        grid=(nb // n_seq,),
        in_specs=in_specs,
        out_specs=[
            pl.BlockSpec((rows, MIX_W), lambda b: (b, 0)),
            pl.BlockSpec((n_seq, 2, HEADS, HEAD_DIM, HEAD_DIM), lambda b: (b, 0, 0, 0, 0)),
        ],
        out_shape=[
            jax.ShapeDtypeStruct((n, MIX_W), F32),
            jax.ShapeDtypeStruct((nb, 2, HEADS, HEAD_DIM, HEAD_DIM), F32),
        ],
        scratch_shapes=[
            pltpu.VMEM((rows, MIX_W), F32),
            pltpu.VMEM((rows, MIX_W), F32),
            pltpu.VMEM((rows, MIX_W), F32),
            pltpu.VMEM((rows, 4 * MIX_W), F32),
            pltpu.VMEM((2, rows, GDN_PRE_W), F32),
            pltpu.VMEM((2, rows // CHUNK * 8, MIX_W), F32),
            pltpu.VMEM((rows, MIX_W), F32),
            pltpu.VMEM((n_seq, 2, MIX_W, MIX_W), F32),
        ],
        compiler_params=pltpu.CompilerParams(dimension_semantics=("arbitrary",),
                                             vmem_limit_bytes=VMEM_LIMIT),
        name="gdn_lat" if cached else "gdn_ctx",
    )(*args)


RWKV_PRE_W = 7 * MIX_W


def _rwkv_prepare(items, maskbd, eye_full):
    n = range(len(items))
    rs, kds, vs, kks, bs, lws, masks, revs = zip(*items)
    cums = [_mm_sel_l(masks[i][2], lws[i]) for i in n]
    einvs = [jnp.exp(-c) for c in cums]
    kts = [kks[i] * jnp.exp(cums[i] - lws[i]) for i in n]
    rts = [rs[i] * jnp.exp(cums[i]) for i in n]
    krs = [jnp.concatenate([kts[i], rts[i]], axis=0) for i in n]
    lb_alls = [_mm_nt(krs[i], _bd(bs[i] * einvs[i], maskbd)) for i in n]
    lk_alls = [_mm_nt(krs[i], _bd(kds[i] * einvs[i], maskbd)) for i in n]
    lbs = [jnp.where(masks[i][1], lb_alls[i][:CHUNK], 0.0) for i in n]
    t_invs = _neumann_inverse(lbs, [m[3] for m in masks], maskbd)
    lvs = [_mm(jnp.concatenate([jnp.where(masks[i][1], lk_alls[i][:CHUNK], 0.0),
                                jnp.where(masks[i][0], lk_alls[i][CHUNK:], 0.0)], axis=0), _bd(vs[i], maskbd))
           for i in n]
    tkps = [_mm(t_invs[i], jnp.concatenate([_bd(kts[i], maskbd), _bd(lvs[i][:CHUNK], maskbd)], axis=1)) for i in n]
    out = []
    for i in n:
        c_last = cums[i][0:1] if revs[i] else cums[i][CHUNK - 1:CHUNK]
        tail = jnp.exp(c_last - cums[i])
        rb = jnp.where(masks[i][0], lb_alls[i][CHUNK:], 0.0)
        pre = jnp.concatenate([tkps[i][:, :MIX_W], rts[i], tkps[i][:, MIX_W:], lvs[i][CHUNK:], rb,
                               kds[i] * tail, bs[i] * tail], axis=1)
        gcol = jnp.sum(eye_full * jnp.exp(c_last), axis=1, keepdims=True)
        out.append((pre, jnp.broadcast_to(gcol, (MIX_W, LANES))))
    return out


def _rwkv_step(items, maskbd):
    w = MIX_W
    n = range(len(items))
    pres, vs, gcols, states = zip(*items)
    prs = [_mm(jnp.concatenate([pres[i][:, :w], pres[i][:, w:2 * w]], axis=0), states[i]) for i in n]
    ps = [prs[i][:CHUNK] + pres[i][:, 2 * w:3 * w] for i in n]
    outs = [prs[i][CHUNK:] + pres[i][:, 3 * w:4 * w] - _mm(pres[i][:, 4 * w:5 * w], _bd(ps[i], maskbd)) for i in n]
    upds = [_mm_tn(jnp.concatenate([pres[i][:, 5 * w:6 * w], pres[i][:, 6 * w:]], axis=0),
                   jnp.concatenate([vs[i], -ps[i]], axis=0)) for i in n]
    mask_f = maskbd.astype(F32)
    return [(outs[i], states[i] * jnp.concatenate([gcols[i], gcols[i]], axis=1) + upds[i] * mask_f) for i in n]


def _rwkv_kernel(*refs, seq_len, n_seq, cached):
    if cached:
        (zr_ref, s0_ref, mup_ref, mun_ref, w0_ref, w2_ref, a0_ref, a2_ref, g2_ref, kk_ref, ka_ref, rk_ref,
         gnw_ref, gnb_ref, ones_ref, maskbd_ref, o_ref, sout_ref,
         r_s, v_s, kk_s, dir_s, bg_s, pre_s, gcol_s, oacc_s, st_s) = refs
    else:
        (zr_ref, mup_ref, mun_ref, w0_ref, w2_ref, a0_ref, a2_ref, g2_ref, kk_ref, ka_ref, rk_ref,
         gnw_ref, gnb_ref, ones_ref, maskbd_ref, o_ref, sout_ref,
         r_s, v_s, kk_s, dir_s, bg_s, pre_s, gcol_s, oacc_s, st_s) = refs
    t = seq_len * n_seq
    z = zr_ref[...]
    rowi = jnp.bitwise_and(_iota((t, 1), 0), seq_len - 1)
    zp = jnp.where(rowi == 0, 0.0, pltpu.roll(z, 1, 0))
    zn = jnp.where(rowi == seq_len - 1, 0.0, pltpu.roll(z, t - 1, 0))
    z = z + mup_ref[...] * (zp - z) + mun_ref[...] * (zn - z)
    w = MIX_W
    r = z[:, :w]
    k = z[:, w:2 * w]
    v = z[:, 2 * w:3 * w]
    wd = jnp.tanh(z[:, 3 * w:3 * w + LANES])
    ad = z[:, 3 * w + LANES:3 * w + 2 * LANES]
    gd = _sigmoid(z[:, 3 * w + 2 * LANES:])
    ones_bd = ones_ref[...]
    kk = k * kk_ref[...]
    kk = kk * lax.rsqrt(_mm_sel_r(kk * kk, ones_bd) + 1e-6)
    r_s[...] = r
    v_s[...] = v
    kk_s[...] = kk
    bonus = jnp.zeros((t, w), F32)
    for d in range(2):
        w_log = -_softplus(-(w0_ref[d:d + 1] + _mm(wd, w2_ref[d]))) - 0.5
        a = _sigmoid(a0_ref[d:d + 1] + _mm(ad, a2_ref[d]))
        kd = k * (1.0 + (a - 1.0) * ka_ref[...])
        dir_s[:, (3 * d) * w:(3 * d + 1) * w] = -jnp.exp(w_log)
        dir_s[:, (3 * d + 1) * w:(3 * d + 2) * w] = kd
        dir_s[:, (3 * d + 2) * w:(3 * d + 3) * w] = kk * a
        bonus = bonus + _mm_sel_r(r * kd * rk_ref[...], ones_bd) * v
    bg_s[:, :w] = bonus
    bg_s[:, w:] = _mm(gd, g2_ref[...])
    oacc_s[...] = jnp.zeros((t, w), F32)
    if cached:
        st_s[...] = s0_ref[...]
    else:
        st_s[...] = jnp.zeros((n_seq, 2, w, w), F32)
    maskbd = maskbd_ref[...]
    masks = (_chunk_masks(False), _chunk_masks(True))
    eye_full = jnp.where(_iota((w, w), 0) == _iota((w, w), 1), 1.0, 0.0).astype(F32)
    n_chunks = seq_len // CHUNK

    def prepare_group(c0):
        where, items = [], []
        for j in range(CHUNK_GROUP):
            c = c0 + j
            rows = pl.ds(pl.multiple_of(c * CHUNK, CHUNK), CHUNK)
            for d in range(2):
                where.append((d, c, rows))
                items.append((r_s[rows, :], dir_s[rows, (3 * d + 1) * w:(3 * d + 2) * w], v_s[rows, :], kk_s[rows, :],
                              dir_s[rows, (3 * d + 2) * w:(3 * d + 3) * w], dir_s[rows, (3 * d) * w:(3 * d + 1) * w],
                              masks[d], d == 1))
        for (d, c, rows), (pre, gcol) in zip(where, _rwkv_prepare(items, maskbd, eye_full)):
            pre_s[d, rows, :] = pre
            gcol_s[d, pl.ds(pl.multiple_of(c * w, w), w), :] = gcol

    _for_chunk_groups(n_chunks * n_seq, prepare_group)

    def body(i, carry):
        where, items = [], []
        for q in range(n_seq):
            for d in range(2):
                c = q * n_chunks + (i if d == 0 else n_chunks - 1 - i)
                rows = pl.ds(pl.multiple_of(c * CHUNK, CHUNK), CHUNK)
                where.append((q, d, rows))
                items.append((pre_s[d, rows, :], v_s[rows, :], gcol_s[d, pl.ds(pl.multiple_of(c * w, w), w), :],
                              st_s[q, d]))
        for (q, d, rows), (o, z_new) in zip(where, _rwkv_step(items, maskbd)):
            oacc_s[rows, :] = oacc_s[rows, :] + o
            st_s[q, d] = z_new
        return carry

    lax.fori_loop(0, n_chunks, body, 0)
    o = oacc_s[...]
    inv_n = 1.0 / HEAD_DIM
    mu = _mm_sel_r(o, ones_bd) * inv_n
    oc = o - mu
    var = _mm_sel_r(oc * oc, ones_bd) * inv_n
    y = oc * lax.rsqrt(var + RWKV_GN_EPS) * gnw_ref[...] + gnb_ref[...]
    o_ref[...] = (y + bg_s[:, :w]) * bg_s[:, w:]
    for q in range(n_seq):
        for d in range(2):
            for h in range(HEADS):
                sl = slice(h * HEAD_DIM, (h + 1) * HEAD_DIM)
                sout_ref[q, d, h] = st_s[q, d, sl, sl]


def _rwkv(zr, lw, layer, consts, seq_len, s0_bd=None):
    n = zr.shape[0]
    nb = n // seq_len
    cached = s0_bd is not None
    n_seq = 1 if cached else SEQ_PER_STEP
    rows = seq_len * n_seq
    args = [zr]
    in_specs = [pl.BlockSpec((rows, ZR_W), lambda b: (b, 0))]
    if cached:
        args.append(s0_bd)
        in_specs.append(pl.BlockSpec((1, None, 2, MIX_W, MIX_W), lambda b: (b, layer, 0, 0, 0)))
    layered = [lw["rwkv_mu_prev"], lw["rwkv_mu_next"], lw["rwkv_w0"], lw["rwkv_w2"], lw["rwkv_a0"], lw["rwkv_a2"],
               lw["rwkv_g2"], lw["rwkv_k_k"], lw["rwkv_k_a"], lw["rwkv_r_k"], lw["rwkv_gn_w"], lw["rwkv_gn_b"]]
    const = [consts["ones_bd"], consts["maskbd"]]
    args += layered + const
    in_specs += [_layer_spec(a, layer) for a in layered] + [_const_spec(a) for a in const]
    return pl.pallas_call(
        functools.partial(_rwkv_kernel, seq_len=seq_len, n_seq=n_seq, cached=cached),
        grid=(nb // n_seq,),
        in_specs=in_specs,
        out_specs=[
            pl.BlockSpec((rows, MIX_W), lambda b: (b, 0)),
            pl.BlockSpec((n_seq, 2, HEADS, HEAD_DIM, HEAD_DIM), lambda b: (b, 0, 0, 0, 0)),
        ],
        out_shape=[
            jax.ShapeDtypeStruct((n, MIX_W), F32),
            jax.ShapeDtypeStruct((nb, 2, HEADS, HEAD_DIM, HEAD_DIM), F32),
        ],
        scratch_shapes=[
            pltpu.VMEM((rows, MIX_W), F32),
            pltpu.VMEM((rows, MIX_W), F32),
            pltpu.VMEM((rows, MIX_W), F32),
            pltpu.VMEM((rows, 6 * MIX_W), F32),
            pltpu.VMEM((rows, 2 * MIX_W), F32),
            pltpu.VMEM((2, rows, RWKV_PRE_W), F32),
            pltpu.VMEM((2, rows // CHUNK * MIX_W, LANES), F32),
            pltpu.VMEM((rows, MIX_W), F32),
            pltpu.VMEM((n_seq, 2, MIX_W, MIX_W), F32),
        ],
        compiler_params=pltpu.CompilerParams(dimension_semantics=("arbitrary",),
                                             vmem_limit_bytes=VMEM_LIMIT),
        name="rwkv_lat" if cached else "rwkv_ctx",
    )(*args)


def _route(logits_t, bias):
    tm = logits_t.shape[1]
    neg = -jnp.inf
    sc = _sigmoid(logits_t)
    sc3 = sc.reshape(N_GROUPS, GROUP_SIZE, tm)
    sel = (sc + bias).reshape(N_GROUPS, GROUP_SIZE, tm)
    si = _iota(sel.shape, 1).astype(F32)
    m1 = jnp.max(sel, axis=1, keepdims=True)
    f1 = jnp.min(jnp.where(sel == m1, si, float(GROUP_SIZE)), axis=1, keepdims=True)
    m2 = jnp.max(jnp.where(si == f1, neg, sel), axis=1, keepdims=True)
    grp = m1 + m2
    gi = _iota(grp.shape, 0).astype(F32)
    gsel = jnp.zeros(grp.shape, F32)
    for _ in range(TOPK_GROUPS):
        mx = jnp.max(grp, axis=0, keepdims=True)
        fi = jnp.min(jnp.where(grp == mx, gi, float(N_GROUPS)), axis=0, keepdims=True)
        hit = gi == fi
        gsel = jnp.where(hit, 1.0, gsel)
        grp = jnp.where(hit, neg, grp)
    cur = jnp.where(gsel > 0.0, sel, neg)
    ei = (_iota(cur.shape, 0) * GROUP_SIZE + _iota(cur.shape, 1)).astype(F32)
    chosen = jnp.zeros(cur.shape, F32)
    ids, wts = [], []
    for _ in range(TOP_K):
        mx = jnp.max(jnp.max(cur, axis=0, keepdims=True), axis=1, keepdims=True)
        fi = jnp.min(jnp.min(jnp.where(cur == mx, ei, float(N_EXPERTS)), axis=0, keepdims=True),
                     axis=1, keepdims=True)
        hit = ei == fi
        chosen = jnp.where(hit, 1.0, chosen)
        cur = jnp.where(hit, neg, cur)
        ids.append(fi.reshape(1, tm))
        wts.append(jnp.sum(jnp.sum(jnp.where(hit, sc3, 0.0), axis=0, keepdims=True), axis=1, keepdims=True)
                   .reshape(1, tm))
    w = jnp.concatenate(wts, axis=0)
    w = w / jnp.sum(w, axis=0, keepdims=True) * ROUTE_SCALE
    return chosen.reshape(N_EXPERTS, tm), jnp.concatenate(ids, axis=0), w


def _pack_halves(x):
    half = x.shape[1] // 2
    bits = lax.bitcast_convert_type(x.astype(BF16).astype(F32), jnp.int32)
    lo = lax.shift_right_logical(bits[:, :half], jnp.int32(16))
    return jnp.bitwise_or(lo, jnp.bitwise_and(bits[:, half:], jnp.int32(-65536)))


def _unpack_halves(word):
    lo = lax.bitcast_convert_type(lax.shift_left(word, jnp.int32(16)), F32)
    hi = lax.bitcast_convert_type(jnp.bitwise_and(word, jnp.int32(-65536)), F32)
    return lo, hi


def _post_kernel(x_ref, om_ref, og_ref, or_ref, mod_ref, wo_ref, n2_ref, rt_ref, rb_ref, sgu_ref, sdn_ref,
                 tri_ref, cin_ref, x1_ref, h2_ref, eid_ref, rank_ref, ew_ref, cnt_ref, carry_s):
    @pl.when(pl.program_id(0) == 0)
    def _():
        carry_s[...] = cin_ref[...]

    m = mod_ref[0]
    g1 = m[:, 2 * D_MODEL:3 * D_MODEL]
    sh2 = m[:, 3 * D_MODEL:4 * D_MODEL]
    sc2 = m[:, 4 * D_MODEL:5 * D_MODEL]
    w_mla = MLA_HEADS * MLA_V
    mix = (_mm(om_ref[...], wo_ref[0:w_mla, :]) + _mm(og_ref[...], wo_ref[w_mla:w_mla + MIX_W, :])
           + _mm(or_ref[...], wo_ref[w_mla + MIX_W:, :]))
    x1 = x_ref[...] + g1 * mix
    h2 = _rms(x1, n2_ref[...]) * (1.0 + sc2) + sh2
    h2_ref[...] = _pack_halves(h2)
    g2 = m[:, 5 * D_MODEL:]
    x1_ref[...] = x1 + g2 * _mm(_swiglu_act(_mm(h2, sgu_ref[...])), sdn_ref[...])
    r_hi, r_lo = _split2(rt_ref[...])
    h_hi, h_lo = _split2(h2)
    logits_t = _mm_nt(r_hi, h_hi) + _mm_nt(r_hi, h_lo) + _mm_nt(r_lo, h_hi)
    chosen, ids, w = _route(logits_t, rb_ref[...])
    tm = chosen.shape[1]
    rank_et = (carry_s[:, 0:1] + _mm(chosen, tri_ref[...])).reshape(N_GROUPS, GROUP_SIZE, tm)
    ei = (_iota(rank_et.shape, 0) * GROUP_SIZE + _iota(rank_et.shape, 1)).astype(F32)
    ranks = []
    for k in range(TOP_K):
        pick = jnp.where(ei == ids[k:k + 1].reshape(1, 1, tm), rank_et, 0.0)
        ranks.append(jnp.sum(jnp.sum(pick, axis=0, keepdims=True), axis=1, keepdims=True).reshape(1, tm))
    eid_ref[...] = ids.astype(jnp.int32)
    rank_ref[...] = jnp.concatenate(ranks, axis=0).astype(jnp.int32)
    ew_ref[...] = jnp.concatenate([w, jnp.zeros((LANES - TOP_K, tm), F32)], axis=0).T
    total = carry_s[...] + jnp.sum(chosen, axis=1, keepdims=True)
    carry_s[...] = total
    cnt_ref[...] = total


def _post(x2d, om, og, orw, mods, lw, layer, consts, tm, seq_len, mod_base, counts_in):
    n = x2d.shape[0]
    tiles_per_seq = seq_len // tm if mod_base else 1

    def mod_idx(i):
        return (layer, mod_base + i // tiles_per_seq if mod_base else 0, 0, 0)

    row = lambda w: pl.BlockSpec((tm, w), lambda i: (i, 0))
    col = lambda h: pl.BlockSpec((h, tm), lambda i: (0, i))
    full = lambda a: _layer_spec(a, layer)
    tail = [lw["w_out"], lw["norm2"], lw["router_t"], lw["router_b"], lw["shared_w_gu"], lw["shared_w_down"]]
    tri = consts["tri_tokens"]
    return pl.pallas_call(
        _post_kernel,
        grid=(n // tm,),
        in_specs=[row(D_MODEL), row(MLA_HEADS * MLA_V), row(MIX_W), row(MIX_W),
                  pl.BlockSpec((None, 1, 1, 6 * D_MODEL), mod_idx)] + [full(a) for a in tail]
        + [_const_spec(tri), _const_spec(counts_in)],
        out_specs=[row(D_MODEL), row(D_MODEL // 2), col(TOP_K), col(TOP_K), row(LANES),
                   pl.BlockSpec((N_EXPERTS, LANES), lambda i: (0, 0))],
        out_shape=[
            jax.ShapeDtypeStruct((n, D_MODEL), F32),
            jax.ShapeDtypeStruct((n, D_MODEL // 2), jnp.int32),
            jax.ShapeDtypeStruct((TOP_K, n), jnp.int32),
            jax.ShapeDtypeStruct((TOP_K, n), jnp.int32),
            jax.ShapeDtypeStruct((n, LANES), F32),
            jax.ShapeDtypeStruct((N_EXPERTS, LANES), F32),
        ],
        scratch_shapes=[pltpu.VMEM((N_EXPERTS, LANES), F32)],
        compiler_params=pltpu.CompilerParams(dimension_semantics=("arbitrary",),
                                             vmem_limit_bytes=VMEM_LIMIT),
        name="post",
    )(x2d, om, og, orw, mods, *tail, tri, counts_in)


MOE_ROWS = 512
SC_ROWS = 128
SC_SUBCORES = 32


def _swiglu_act(gu):
    return _silu(gu[:, :D_EXPERT]) * gu[:, D_EXPERT:]


def _dispatch_plan(eid, rank, counts, n, rows):
    n_blocks = n * TOP_K // rows + N_EXPERTS
    cnt = counts[:, 0].astype(jnp.int32)
    blocks = (cnt + rows - 1) // rows
    block_end = jnp.cumsum(blocks)
    offset = (block_end - blocks) * rows
    experts = jnp.arange(N_EXPERTS, dtype=jnp.int32)
    dest = jnp.sum(jnp.where(eid[..., None] == experts, offset, 0), axis=-1) + rank
    block_ids = jnp.arange(n_blocks, dtype=jnp.int32)
    block_expert = jnp.minimum(jnp.sum((block_end[None, :] <= block_ids[:, None]).astype(jnp.int32), axis=1),
                               N_EXPERTS - 1)
    return dest, block_expert, block_end[-1:].astype(jnp.int32), n_blocks


def _sc_mesh():
    return plsc.VectorSubcoreMesh(core_axis_name="core", subcore_axis_name="subcore")


def _sc_dispatch(groups, dest, n_rows):
    w = groups[0].shape[1]
    dtype = groups[0].dtype

    @functools.partial(pl.kernel, out_type=jax.ShapeDtypeStruct((n_rows, w), dtype), mesh=_sc_mesh(),
                       scratch_types=[pltpu.VMEM((SC_ROWS, w), dtype), pltpu.VMEM((TOP_K, SC_ROWS), jnp.int32)])
    def kern(*refs):
        x_refs, d_hbm, o_hbm, xv, dv = refs[:len(groups)], *refs[len(groups):]
        sid = lax.axis_index("core") * (SC_SUBCORES // 2) + lax.axis_index("subcore")
        start = 0
        for x_hbm, x in zip(x_refs, groups):
            def chunk(c, x_hbm=x_hbm, start=start):
                r0 = pl.multiple_of(c * SC_ROWS, SC_ROWS)
                pltpu.sync_copy(x_hbm.at[pl.ds(r0, SC_ROWS)], xv)
                pltpu.sync_copy(d_hbm.at[:, pl.ds(start + r0, SC_ROWS)], dv)
                for k in range(TOP_K):
                    pltpu.sync_copy(xv, o_hbm.at[dv.at[k]])

            pl.loop(sid, x.shape[0] // SC_ROWS, step=SC_SUBCORES)(chunk)
            start += x.shape[0]

    return kern(*groups, dest)


def _sc_gather(y, idx):
    w = y.shape[1]
    n_chunks = idx.shape[0]

    @functools.partial(pl.kernel, out_type=jax.ShapeDtypeStruct((n_chunks * SC_ROWS, w), y.dtype), mesh=_sc_mesh(),
                       scratch_types=[pltpu.VMEM((SC_ROWS, w), y.dtype), pltpu.VMEM((1, SC_ROWS), jnp.int32)])
    def kern(y_hbm, i_hbm, o_hbm, ov, iv):
        sid = lax.axis_index("core") * (SC_SUBCORES // 2) + lax.axis_index("subcore")

        @pl.loop(sid, n_chunks, step=SC_SUBCORES)
        def _(c):
            pltpu.sync_copy(i_hbm.at[pl.ds(c, 1)], iv)
            pltpu.sync_copy(y_hbm.at[iv.at[0]], ov)
            pltpu.sync_copy(ov, o_hbm.at[pl.ds(pl.multiple_of(c * SC_ROWS, SC_ROWS), SC_ROWS)])

    return kern(y, idx)


def _moe_rows_kernel(be_ref, nu_ref, x_ref, wgu_ref, wdn_ref, y_ref):
    @pl.when(pl.program_id(0) < nu_ref[0])
    def _():
        half = D_MODEL // 2
        lo, hi = _unpack_halves(x_ref[...])
        gu = _mm(lo, wgu_ref[0:half, :]) + _mm(hi, wgu_ref[half:, :])
        y_ref[...] = _pack_halves(_mm(_swiglu_act(gu), wdn_ref[...]))


def _moe_rows(xs, block_expert, n_used, lw, layer, n_blocks, rows):
    half = D_MODEL // 2
    last = lambda b, be, nu: jnp.minimum(b, nu[0] - 1)
    return pl.pallas_call(
        _moe_rows_kernel,
        grid_spec=pltpu.PrefetchScalarGridSpec(
            num_scalar_prefetch=2,
            grid=(n_blocks,),
            in_specs=[pl.BlockSpec((rows, half), lambda b, be, nu: (last(b, be, nu), 0)),
                      pl.BlockSpec((None, None, D_MODEL, 2 * D_EXPERT),
                                   lambda b, be, nu: (layer, be[last(b, be, nu)], 0, 0)),
                      pl.BlockSpec((None, None, D_EXPERT, D_MODEL),
                                   lambda b, be, nu: (layer, be[last(b, be, nu)], 0, 0))],
            out_specs=pl.BlockSpec((rows, half), lambda b, be, nu: (last(b, be, nu), 0)),
        ),
        out_shape=jax.ShapeDtypeStruct(xs.shape, jnp.int32),
        compiler_params=pltpu.CompilerParams(dimension_semantics=("arbitrary",),
                                             vmem_limit_bytes=VMEM_LIMIT),
        name="moe_rows",
    )(block_expert, n_used, xs, lw["moe_w_gu"], lw["moe_w_down"])


def _moe_combine_kernel(yg_ref, ew_ref, x1_ref, mod_ref, nf_ref, o_ref, *, final):
    ew = ew_ref[...]
    acc_lo = acc_hi = None
    for k in range(TOP_K):
        lo, hi = _unpack_halves(yg_ref[k])
        wk = ew[:, k:k + 1]
        acc_lo = wk * lo if acc_lo is None else acc_lo + wk * lo
        acc_hi = wk * hi if acc_hi is None else acc_hi + wk * hi
    g2 = mod_ref[0][:, 5 * D_MODEL:]
    x2 = x1_ref[...] + g2 * jnp.concatenate([acc_lo, acc_hi], axis=1)
    if final:
        x2 = _rms(x2, nf_ref[...])
    o_ref[...] = x2


def _moe_combine(yg, row0, ew, x1, mods, layer, norm_f, tm, seq_len, mod_base, final):
    n = x1.shape[0]
    half = D_MODEL // 2
    tiles_per_seq = seq_len // tm if mod_base else 1
    tile0 = row0 // tm

    def mod_idx(i):
        return (layer, mod_base + i // tiles_per_seq if mod_base else 0, 0, 0)

    row = lambda w: pl.BlockSpec((tm, w), lambda i: (i, 0))
    return pl.pallas_call(
        functools.partial(_moe_combine_kernel, final=final),
        grid=(n // tm,),
        in_specs=[pl.BlockSpec((TOP_K, tm, half), lambda i: (0, tile0 + i, 0)), row(LANES), row(D_MODEL),
                  pl.BlockSpec((None, 1, 1, 6 * D_MODEL), mod_idx), _const_spec(norm_f)],
        out_specs=row(D_MODEL),
        out_shape=jax.ShapeDtypeStruct((n, D_MODEL), F32),
        compiler_params=pltpu.CompilerParams(dimension_semantics=("arbitrary",),
                                             vmem_limit_bytes=VMEM_LIMIT),
        name="moe_combine_final" if final else "moe_combine",
    )(yg, ew, x1, mods, norm_f)


def _moe_experts(groups, eid, rank, counts, lw, layer):
    group_sizes = [g.shape[0] for g in groups]
    n = sum(group_sizes)
    dest, block_expert, n_used, n_blocks = _dispatch_plan(eid, rank, counts, n, MOE_ROWS)
    xs = _sc_dispatch(groups, dest, n_blocks * MOE_ROWS)
    y = _moe_rows(xs, block_expert, n_used, lw, layer, n_blocks, MOE_ROWS)
    outs, start = [], 0
    for size in group_sizes:
        idx = dest[:, start:start + size].reshape(size * TOP_K // SC_ROWS, SC_ROWS)
        outs.append(_sc_gather(y, idx).reshape(TOP_K, size, D_MODEL // 2))
        start += size
    return outs


def _constants():
    idx = np.arange(MIX_W)
    same_head = (idx[:, None] // HEAD_DIM) == (idx[None, :] // HEAD_DIM)
    maskbd = jnp.asarray(same_head, BF16)
    eexp = np.zeros((LANES, 4 * MIX_W), np.float32)
    for blk in range(4):
        kind, d = divmod(blk, 2)
        for h in range(HEADS):
            src = kind * 2 * HEADS + d * HEADS + h
            eexp[src, blk * MIX_W + h * HEAD_DIM: blk * MIX_W + (h + 1) * HEAD_DIM] = 1.0
    tri = np.triu(np.ones((POST_TM, POST_TM), np.float32), 1)
    return {"maskbd": maskbd, "ones_bd": maskbd, "eexp": jnp.asarray(eexp, BF16), "tri_tokens": jnp.asarray(tri, BF16)}


def _rope_tables(n):
    rows = n // GRID_W
    row = jnp.repeat(jnp.arange(rows, dtype=F32), GRID_W)
    col = jnp.tile(jnp.arange(GRID_W, dtype=F32), rows)
    axis_dim = MLA_ROPE // 2
    inv = jnp.power(ROPE_BASE, -jnp.arange(0, axis_dim, 2, dtype=F32) / axis_dim)
    ang_r = row[:, None] * inv
    ang_c = col[:, None] * inv
    cr, sr, cc, sc = jnp.cos(ang_r), jnp.sin(ang_r), jnp.cos(ang_c), jnp.sin(ang_c)
    zeros = jnp.zeros((n, LANES - MLA_ROPE), F32)
    cos_t = jnp.concatenate([cr, cr, cc, cc, zeros], axis=1)
    sin_t = jnp.concatenate([-sr, sr, -sc, sc, zeros], axis=1)
    return cos_t, sin_t


def _stacked_weights(p):
    w_uq = p["mla_w_uq"].reshape(DEPTH, Q_LORA, MLA_HEADS, MLA_NOPE + MLA_ROPE)
    zq = jnp.zeros((DEPTH, Q_LORA, MLA_HEADS, QH_W - MLA_NOPE - MLA_ROPE), F32)
    w_uq_a = jnp.concatenate([w_uq, zq], axis=-1).reshape(DEPTH, Q_LORA, MLA_HEADS * QH_W).astype(BF16)
    w_uq_sw = jnp.concatenate([jnp.zeros((DEPTH, Q_LORA, MLA_HEADS, MLA_NOPE), F32),
                               w_uq[..., MLA_NOPE + _ROPE_SWAP], zq], axis=-1)
    w_uq_sw = w_uq_sw.reshape(DEPTH, Q_LORA, MLA_HEADS * QH_W).astype(BF16)

    def per_direction(w):
        half = jnp.zeros((DEPTH, 64, MIX_W), F32)
        return jnp.stack([jnp.concatenate([w[:, 0], half], axis=1),
                          jnp.concatenate([half, w[:, 1]], axis=1)], axis=1).astype(BF16)

    row = lambda v: v.reshape(DEPTH, 1, -1)
    pad_row = lambda v: jnp.pad(row(v), ((0, 0), (0, 0), (0, LANES - 2 * HEADS)))
    return {
        "norm1": row(p["norm1"]),
        "w_in": p["w_in"],
        "q_norm": row(p["mla_q_norm"]),
        "w_uq": w_uq_a, "w_uq_sw": w_uq_sw,
        "kv_norm": row(p["mla_kv_norm"]),
        "w_ukv": p["mla_w_ukv"].astype(BF16),
        "gdn_conv": p["gdn_conv"],
        "gdn_alog": pad_row(p["gdn_a_log"]),
        "gdn_dtb": pad_row(p["gdn_dt_bias"]),
        "gdn_norm": jnp.tile(row(p["gdn_norm"]), (1, 1, HEADS)),
        "rwkv_mu_prev": row(p["rwkv_mu_prev"]),
        "rwkv_mu_next": row(p["rwkv_mu_next"]),
        "rwkv_w0": p["rwkv_w0"],
        "rwkv_w2": per_direction(p["rwkv_w2"]),
        "rwkv_a0": p["rwkv_a0"],
        "rwkv_a2": per_direction(p["rwkv_a2"]),
        "rwkv_g2": p["rwkv_g2"].astype(BF16),
        "rwkv_k_k": row(p["rwkv_k_k"]),
        "rwkv_k_a": row(p["rwkv_k_a"]),
        "rwkv_r_k": row(p["rwkv_r_k"]),
        "rwkv_gn_w": row(p["rwkv_gn_w"]),
        "rwkv_gn_b": row(p["rwkv_gn_b"]),
        "w_out": p["w_out"].astype(BF16),
        "norm2": row(p["norm2"]),
        "router_t": jnp.swapaxes(p["moe_router"], 1, 2),
        "router_b": p["moe_bias"].reshape(DEPTH, N_EXPERTS, 1),
        "moe_w_gu": p["moe_w_gu"],
        "moe_w_down": p["moe_w_down"],
        "shared_w_gu": p["shared_w_gu"].astype(BF16),
        "shared_w_down": p["shared_w_down"].astype(BF16),
    }


def _embed_block_diag(s):
    b = s.shape[0]
    eye = jnp.eye(HEADS, dtype=s.dtype)
    out = jnp.einsum("bdhkv,hg->bdhkgv", s, eye)
    return out.reshape(b, 2, MIX_W, MIX_W)


def _layer_front(x2d, mods, lw, l, consts, seq_len, mod_base, cache, tm, tq, counts_in):
    zm, zg, zab, zr = _inproj(x2d, mods, lw, l, tm, seq_len, mod_base)
    if cache is None:
        o_mla, ckv = _mla(zm, lw, l, seq_len, tq)
        o_gdn, s_gdn = _gdn(zg, zab, lw, l, consts, seq_len)
        o_rwkv, s_rwkv = _rwkv(zr, lw, l, consts, seq_len)
        s_rwkv = jnp.swapaxes(s_rwkv, -1, -2)
    else:
        cckv, ckpe, rc, rs, sg, sr = cache
        o_mla, ckv = _mla(zm, lw, l, seq_len, tq, (cckv, ckpe, rc, rs))
        o_gdn, s_gdn = _gdn(zg, zab, lw, l, consts, seq_len, sg)
        o_rwkv, s_rwkv = _rwkv(zr, lw, l, consts, seq_len, sr)
    routed = _post(x2d, o_mla, o_gdn, o_rwkv, mods, lw, l, consts, tm, seq_len, mod_base, counts_in)
    kpe0 = Q_LORA + KV_LORA
    return routed, (ckv, zm[:, kpe0:kpe0 + MLA_ROPE], s_gdn, s_rwkv)


def kernel(x_prompt, x_sample, cache_mla_ckv, cache_mla_kpe, state_gdn, state_rwkv, c, c_ctx, ada_w, ada_b, norm1, w_in, mla_q_norm, mla_w_uq, mla_kv_norm, mla_w_ukv, gdn_conv, gdn_a_log, gdn_dt_bias, gdn_norm, rwkv_mu_prev, rwkv_mu_next, rwkv_w0, rwkv_w2, rwkv_a0, rwkv_a2, rwkv_g2, rwkv_k_k, rwkv_k_a, rwkv_r_k, rwkv_gn_w, rwkv_gn_b, w_out, norm2, moe_router, moe_bias, moe_w_gu, moe_w_down, shared_w_gu, shared_w_down, norm_f):
    p = dict(norm1=norm1, w_in=w_in, mla_q_norm=mla_q_norm, mla_w_uq=mla_w_uq, mla_kv_norm=mla_kv_norm,
             mla_w_ukv=mla_w_ukv, gdn_conv=gdn_conv, gdn_a_log=gdn_a_log, gdn_dt_bias=gdn_dt_bias,
             gdn_norm=gdn_norm, rwkv_mu_prev=rwkv_mu_prev, rwkv_mu_next=rwkv_mu_next, rwkv_w0=rwkv_w0,
             rwkv_w2=rwkv_w2, rwkv_a0=rwkv_a0, rwkv_a2=rwkv_a2, rwkv_g2=rwkv_g2, rwkv_k_k=rwkv_k_k,
             rwkv_k_a=rwkv_k_a, rwkv_r_k=rwkv_r_k, rwkv_gn_w=rwkv_gn_w, rwkv_gn_b=rwkv_gn_b, w_out=w_out,
             norm2=norm2, moe_router=moe_router, moe_bias=moe_bias, moe_w_gu=moe_w_gu, moe_w_down=moe_w_down,
             shared_w_gu=shared_w_gu, shared_w_down=shared_w_down)
    weights = _stacked_weights(p)
    consts = _constants()
    nf = norm_f.reshape(1, D_MODEL)
    b_ctx, t_ctx, _ = x_prompt.shape
    b_lat, t_lat, _ = x_sample.shape

    cvec8 = jnp.concatenate([c_ctx[None, :], c, jnp.zeros((8 - 1 - b_lat, D_MODEL), F32)], axis=0)
    mods = _adaln(cvec8, ada_w, ada_b)
    mods = mods.reshape(DEPTH, 8, 1, 6 * D_MODEL)

    rc, rs = _rope_tables(t_lat)
    ckpe = jnp.pad(cache_mla_kpe, ((0, 0), (0, 0), (0, 0), (0, LANES - MLA_ROPE)))
    cache = (cache_mla_ckv, ckpe, rc, rs, _embed_block_diag_layers(state_gdn),
             _embed_block_diag_layers(jnp.swapaxes(state_rwkv, -1, -2)))
    xp = x_prompt.reshape(b_ctx * t_ctx, D_MODEL)
    xs = x_sample.reshape(b_lat * t_lat, D_MODEL)
    n_ctx = xp.shape[0]
    tm = POST_TM
    ctx_outs = []
    for l in range(DEPTH):
        final = l == DEPTH - 1
        no_pairs = jnp.zeros((N_EXPERTS, LANES), F32)
        (x1c, hc, eidc, rankc, ewc, cnt_c), outs = _layer_front(xp, mods, weights, l, consts, t_ctx, 0, None,
                                                                tm, t_ctx, no_pairs)
        (x1s, hs, eids, ranks, ews, cnt), _ = _layer_front(xs, mods, weights, l, consts, t_lat, 1, cache,
                                                           tm, 256, cnt_c)
        ctx_outs.append(outs)
        yg_c, yg_s = _moe_experts([hc, hs], jnp.concatenate([eidc, eids], axis=1),
                                  jnp.concatenate([rankc, ranks], axis=1), cnt, weights, l)
        xp = _moe_combine(yg_c, 0, ewc, x1c, mods, l, nf, tm, t_ctx, 0, final)
        xs = _moe_combine(yg_s, 0, ews, x1s, mods, l, nf, tm, t_lat, 1, final)

    y_prompt = xp.reshape(b_ctx, t_ctx, D_MODEL)
    y_sample = xs.reshape(b_lat, t_lat, D_MODEL)
    new_ckv = jnp.stack([o[0].reshape(b_ctx, t_ctx, KV_LORA) for o in ctx_outs], axis=1)
    new_kpe = jnp.stack([o[1].reshape(b_ctx, t_ctx, MLA_ROPE) for o in ctx_outs], axis=1)
    new_gdn = jnp.stack([o[2] for o in ctx_outs], axis=1)
    new_rwkv = jnp.stack([o[3] for o in ctx_outs], axis=1)
    return (y_prompt, y_sample, new_ckv, new_kpe, new_gdn, new_rwkv)


def _embed_block_diag_layers(s):
    b = s.shape[0]
    return _embed_block_diag(s.reshape(b * DEPTH, 2, HEADS, HEAD_DIM, HEAD_DIM)).reshape(
        b, DEPTH, 2, MIX_W, MIX_W)
```

```python
import functools

import numpy as np
import jax
import jax.numpy as jnp
from jax import lax
from jax.experimental import pallas as pl
from jax.experimental.pallas import tpu as pltpu
from jax.experimental.pallas import tpu_sc as plsc

F32 = jnp.float32
BF16 = jnp.bfloat16

D_MODEL = 1024
BATCH = 32
SEQ = 256
DEPTH = 2
DEC_BATCH = 2
DEC_SEQ = 1024
PAST_LEN = 512
GRID_W = 64
NORM_EPS = 1e-6

MLA_HEADS = 4
MLA_NOPE = 128
MLA_ROPE = 64
MLA_V = 128
Q_LORA = 384
KV_LORA = 256
ROPE_BASE = 10000.0
MLA_SCALE = (MLA_NOPE + MLA_ROPE) ** -0.5

HEADS = 4
HEAD_DIM = 64
MIX_W = HEADS * HEAD_DIM
GDN_CONV_CH = 3 * MIX_W
CHUNK = 64
RWKV_GN_EPS = 64e-5

N_EXPERTS = 64
TOP_K = 8
N_GROUPS = 8
GROUP_SIZE = N_EXPERTS // N_GROUPS
TOPK_GROUPS = 4
D_EXPERT = 256
ROUTE_SCALE = 2.5

P_MLA = Q_LORA + KV_LORA + MLA_ROPE
P_GDN = GDN_CONV_CH + MIX_W + 4 * HEADS
P_RWKV = 3 * MIX_W + 128 + 128 + 128

LANES = 128
ZM_W = Q_LORA + KV_LORA + 2 * LANES
ZG_W = GDN_CONV_CH + MIX_W
ZR_W = P_RWKV
QH_W = 2 * LANES
VMEM_LIMIT = 56 * 1024 * 1024
POST_TM = 512

_ROPE_SWAP = np.concatenate([np.arange(16, 32), np.arange(0, 16), np.arange(48, 64), np.arange(32, 48)])


def _sigmoid(x):
    return 1.0 / (1.0 + jnp.exp(-x))


def _silu(x):
    return x * _sigmoid(x)


def _softplus(x):
    return jnp.maximum(x, 0.0) + jnp.log(1.0 + jnp.exp(-jnp.abs(x)))


def _rms(x, g, eps=NORM_EPS):
    return x * lax.rsqrt(jnp.mean(x * x, axis=-1, keepdims=True) + eps) * g


def _mm(a, b):
    return jnp.dot(a.astype(BF16), b.astype(BF16), preferred_element_type=F32)


def _mm_nt(a, b):
    return lax.dot_general(a.astype(BF16), b.astype(BF16), (((1,), (1,)), ((), ())),
                           preferred_element_type=F32)


def _mm_tn(a, b):
    return lax.dot_general(a.astype(BF16), b.astype(BF16), (((0,), (0,)), ((), ())),
                           preferred_element_type=F32)


def _split3(x):
    p1 = x.astype(BF16)
    r1 = x - p1.astype(F32)
    p2 = r1.astype(BF16)
    r2 = r1 - p2.astype(F32)
    return p1, p2, r2.astype(BF16)


def _mm_sel_l(sel, x):
    p1, p2, p3 = _split3(x)
    return _mm(sel, p1) + _mm(sel, p2) + _mm(sel, p3)


def _mm_sel_r(x, sel):
    p1, p2, p3 = _split3(x)
    return _mm(p1, sel) + _mm(p2, sel) + _mm(p3, sel)


def _iota(shape, dim):
    return lax.broadcasted_iota(jnp.int32, shape, dim)


def _layer_spec(a, layer, **kw):
    nd = a.ndim - 1
    return pl.BlockSpec((None,) + a.shape[1:], lambda *_: (layer,) + (0,) * nd, **kw)


def _const_spec(a, **kw):
    return pl.BlockSpec(a.shape, lambda *_: (0,) * a.ndim, **kw)


def _bd(x, maskbd):
    xb = x.astype(BF16)
    return jnp.concatenate([xb] * HEADS, axis=0) * maskbd


def _chunk_masks(rev):
    row = _iota((CHUNK, MIX_W), 0)
    col = jnp.bitwise_and(_iota((CHUNK, MIX_W), 1), HEAD_DIM - 1)
    r2 = _iota((CHUNK, CHUNK), 0)
    c2 = _iota((CHUNK, CHUNK), 1)
    if rev:
        inc, strict, tri = row <= col, row < col, r2 <= c2
    else:
        inc, strict, tri = row >= col, row > col, r2 >= c2
    eye = jnp.where(row == col, 1.0, 0.0).astype(F32)
    return inc, strict, jnp.where(tri, 1.0, 0.0).astype(BF16), eye


def _split2(x):
    hi = x.astype(BF16)
    return hi, (x - hi.astype(F32)).astype(BF16)


def _mm_bd3(x, p, maskbd):
    n = x.shape[0]
    xh, xl = _split2(x)
    ph, pl_ = _split2(p)
    r = jnp.dot(jnp.concatenate([xh, xl], axis=0), _bd(ph, maskbd), preferred_element_type=F32)
    return r[:n] + r[n:] + jnp.dot(xh, _bd(pl_, maskbd), preferred_element_type=F32)


def _neumann_inverse(a_list, eye_list, maskbd):
    bs = [-a for a in a_list]
    ms = [eye + b for eye, b in zip(eye_list, bs)]
    ps = [_mm_bd3(b, b, maskbd) for b in bs]
    for _ in range(4):
        boths = [_mm_bd3(jnp.concatenate([m, p], axis=0), p, maskbd) for m, p in zip(ms, ps)]
        ms = [m + both[:CHUNK] for m, both in zip(ms, boths)]
        ps = [both[CHUNK:] for both in boths]
    return [m + _mm_bd3(m, p, maskbd) for m, p in zip(ms, ps)]


def _adaln_kernel(c_ref, w_ref, b_ref, o_ref):
    cv = c_ref[...]
    o_ref[0] = _mm(_silu(cv), w_ref[0]) + b_ref[0]


def _adaln(cvec8, ada_w, ada_b):
    tn = 768
    n_out = 6 * D_MODEL
    return pl.pallas_call(
        _adaln_kernel,
        grid=(DEPTH, n_out // tn),
        in_specs=[
            pl.BlockSpec((8, D_MODEL), lambda l, j: (0, 0)),
            pl.BlockSpec((1, D_MODEL, tn), lambda l, j: (l, 0, j)),
            pl.BlockSpec((1, 1, tn), lambda l, j: (l, 0, j)),
        ],
        out_specs=pl.BlockSpec((1, 8, tn), lambda l, j: (l, 0, j)),
        out_shape=jax.ShapeDtypeStruct((DEPTH, 8, n_out), F32),
        compiler_params=pltpu.CompilerParams(dimension_semantics=("arbitrary", "arbitrary"),
                                             vmem_limit_bytes=VMEM_LIMIT),
        name="adaln",
    )(cvec8, ada_w, ada_b.reshape(DEPTH, 1, n_out))


_KPE0 = Q_LORA + KV_LORA
_W_IN_MOVES = (
    [(0, 0, P_MLA)]
    + [(P_MLA + LANES - MLA_ROPE + 16 * j, _KPE0 + 16 * int(_ROPE_SWAP[16 * j] // 16), 16) for j in range(4)]
    + [(ZM_W, P_MLA, ZG_W), (ZM_W + ZG_W, P_MLA + ZG_W, 4 * HEADS), (ZM_W + ZG_W + LANES, P_MLA + P_GDN, P_RWKV)]
)
W_IN_PAD = ZM_W + ZG_W + LANES + ZR_W


def _inproj_kernel(x_ref, mod_ref, n1_ref, w_ref, zm_ref, zg_ref, zab_ref, zr_ref, w_s):
    @pl.when(pl.program_id(0) == 0)
    def _():
        w_s[...] = jnp.zeros(w_s.shape, BF16)
        for dst, src, width in _W_IN_MOVES:
            w_s[:, dst:dst + width] = w_ref[:, src:src + width].astype(BF16)

    m = mod_ref[0]
    sh = m[:, 0:D_MODEL]
    sc = m[:, D_MODEL:2 * D_MODEL]
    h = _rms(x_ref[...], n1_ref[...]) * (1.0 + sc) + sh
    z = _mm(h, w_s[...])
    o1 = ZM_W
    o2 = o1 + ZG_W
    o3 = o2 + LANES
    zm_ref[...] = z[:, :o1]
    zg_ref[...] = z[:, o1:o2]
    zab_ref[...] = z[:, o2:o3]
    zr_ref[...] = z[:, o3:]


def _inproj(x2d, mods, lw, layer, tm, seq_len, mod_base):
    n = x2d.shape[0]
    tiles_per_seq = seq_len // tm if mod_base else 1

    def mod_idx(i):
        return (layer, mod_base + i // tiles_per_seq if mod_base else 0, 0, 0)

    return pl.pallas_call(
        _inproj_kernel,
        grid=(n // tm,),
        in_specs=[
            pl.BlockSpec((tm, D_MODEL), lambda i: (i, 0)),
            pl.BlockSpec((None, 1, 1, 6 * D_MODEL), mod_idx),
            _layer_spec(lw["norm1"], layer),
            _layer_spec(lw["w_in"], layer, pipeline_mode=pl.Buffered(1)),
        ],
        out_specs=[
            pl.BlockSpec((tm, ZM_W), lambda i: (i, 0)),
            pl.BlockSpec((tm, ZG_W), lambda i: (i, 0)),
            pl.BlockSpec((tm, LANES), lambda i: (i, 0)),
            pl.BlockSpec((tm, ZR_W), lambda i: (i, 0)),
        ],
        out_shape=[
            jax.ShapeDtypeStruct((n, ZM_W), F32),
            jax.ShapeDtypeStruct((n, ZG_W), F32),
            jax.ShapeDtypeStruct((n, LANES), F32),
            jax.ShapeDtypeStruct((n, ZR_W), F32),
        ],
        scratch_shapes=[pltpu.VMEM((D_MODEL, W_IN_PAD), BF16)],
        compiler_params=pltpu.CompilerParams(dimension_semantics=("arbitrary",),
                                             vmem_limit_bytes=VMEM_LIMIT),
        name="inproj",
    )(x2d, mods, lw["norm1"], lw["w_in"])


def _mla_kernel(*refs, seq_len, tq, past, cached):
    if cached:
        (zm_ref, cckv_ref, ckpe_ref, rc_ref, rs_ref, qn_ref, wuq_ref, wuqs_ref, kvn_ref, wukv_ref,
         o_ref, ckv_ref, k_s, v_s) = refs
    else:
        zm_ref, qn_ref, wuq_ref, kvn_ref, wukv_ref = refs[:5]
        o_ref, ckv_ref, kpe_ref, k_s, v_s = refs[-5:]
    qi = pl.program_id(1)
    o_kpe = Q_LORA + KV_LORA

    @pl.when(qi == 0)
    def _():
        zm = zm_ref[...]
        ckv = _rms(zm[:, Q_LORA:o_kpe], kvn_ref[...])
        ckv_ref[...] = ckv
        kpe = zm[:, o_kpe:o_kpe + LANES]
        if not cached:
            kpe_ref[...] = kpe[:, :MLA_ROPE]
        if cached:
            kpe = kpe * rc_ref[...] + zm[:, o_kpe + LANES:o_kpe + 2 * LANES] * rs_ref[...]
            kvc = _mm(cckv_ref[0], wukv_ref[...])
            kpc = ckpe_ref[0].astype(BF16)
        kv = _mm(ckv, wukv_ref[...])
        kpe = kpe.astype(BF16)
        for h in range(MLA_HEADS):
            c0 = h * QH_W
            if cached:
                k_s[0:past, c0:c0 + LANES] = kvc[:, c0:c0 + LANES].astype(BF16)
                k_s[0:past, c0 + LANES:c0 + QH_W] = kpc
                v_s[0:past, h * MLA_V:(h + 1) * MLA_V] = kvc[:, c0 + LANES:c0 + QH_W].astype(BF16)
            k_s[past:past + seq_len, c0:c0 + LANES] = kv[:, c0:c0 + LANES].astype(BF16)
            k_s[past:past + seq_len, c0 + LANES:c0 + QH_W] = kpe
            v_s[past:past + seq_len, h * MLA_V:(h + 1) * MLA_V] = kv[:, c0 + LANES:c0 + QH_W].astype(BF16)

    r0 = pl.multiple_of(qi * tq, tq)
    zq = zm_ref[pl.ds(r0, tq), :]
    cq = _rms(zq[:, :Q_LORA], qn_ref[...])
    q = _mm(cq, wuq_ref[...])
    if cached:
        qs = _mm(cq, wuqs_ref[...])
        ones = jnp.ones((tq, LANES), F32)
        zeros = jnp.zeros((tq, LANES), F32)
        qc = jnp.concatenate([ones, rc_ref[pl.ds(r0, tq), :]], axis=1)
        qsn = jnp.concatenate([zeros, rs_ref[pl.ds(r0, tq), :]], axis=1)
    for h in range(MLA_HEADS):
        c0 = h * QH_W
        qh = q[:, c0:c0 + QH_W]
        if cached:
            qh = qh * qc + qs[:, c0:c0 + QH_W] * qsn
        s = _mm_nt(qh, k_s[:, c0:c0 + QH_W]) * MLA_SCALE
        e = jnp.exp(s - jnp.max(s, axis=-1, keepdims=True))
        den = jnp.sum(e, axis=-1, keepdims=True)
        o_ref[:, h * MLA_V:(h + 1) * MLA_V] = _mm(e, v_s[:, h * MLA_V:(h + 1) * MLA_V]) / den


def _mla(zm, lw, layer, seq_len, tq, cache=None, prev=None):
    n = zm.shape[0]
    nb = n // seq_len
    cached = cache is not None
    past = PAST_LEN if cached else 0
    tk = past + seq_len
    seq_spec = lambda w: pl.BlockSpec((seq_len, w), lambda b, q: (b, 0))
    lay = lambda a: _layer_spec(a, layer)
    if cached:
        cckv, ckpe, rc, rs = cache
        args = [zm, cckv, ckpe, rc, rs, lw["q_norm"], lw["w_uq"], lw["w_uq_sw"], lw["kv_norm"], lw["w_ukv"]]
        in_specs = [seq_spec(ZM_W),
                    pl.BlockSpec((1, None, past, KV_LORA), lambda b, q: (b, layer, 0, 0)),
                    pl.BlockSpec((1, None, past, LANES), lambda b, q: (b, layer, 0, 0)),
                    _const_spec(rc), _const_spec(rs)] + [lay(a) for a in args[5:]]
    else:
        args = [zm, lw["q_norm"], lw["w_uq"], lw["kv_norm"], lw["w_ukv"]]
        in_specs = [seq_spec(ZM_W)] + [lay(a) for a in args[1:]]
    o_spec = pl.BlockSpec((tq, MLA_HEADS * MLA_V), lambda b, q: (b * (seq_len // tq) + q, 0))
    o_shape = jax.ShapeDtypeStruct((n, MLA_HEADS * MLA_V), F32)
    aliases = {}
    if cached:
        out_specs = [o_spec, pl.BlockSpec((seq_len, KV_LORA), lambda b, q: (b, 0))]
        out_shape = [o_shape, jax.ShapeDtypeStruct((n, KV_LORA), F32)]
    else:
        out_specs = [o_spec,
                     pl.BlockSpec((None, None, seq_len, KV_LORA), lambda b, q: (b, layer, 0, 0)),
                     pl.BlockSpec((None, None, seq_len, MLA_ROPE), lambda b, q: (b, layer, 0, 0))]
        out_shape = [o_shape, jax.ShapeDtypeStruct((nb, DEPTH, seq_len, KV_LORA), F32),
                     jax.ShapeDtypeStruct((nb, DEPTH, seq_len, MLA_ROPE), F32)]
        aliases = {len(args): 1, len(args) + 1: 2}
        args += list(prev)
        in_specs += [pl.BlockSpec(memory_space=pl.ANY)] * 2
    return pl.pallas_call(
        functools.partial(_mla_kernel, seq_len=seq_len, tq=tq, past=past, cached=cached),
        grid=(nb, seq_len // tq),
        in_specs=in_specs,
        out_specs=out_specs,
        out_shape=out_shape,
        input_output_aliases=aliases,
        scratch_shapes=[
            pltpu.VMEM((tk, MLA_HEADS * QH_W), BF16),
            pltpu.VMEM((tk, MLA_HEADS * MLA_V), BF16),
        ],
        compiler_params=pltpu.CompilerParams(dimension_semantics=("arbitrary", "arbitrary"),
                                             vmem_limit_bytes=VMEM_LIMIT),
        name="mla_lat" if cached else "mla_ctx",
    )(*args)


SEQ_PER_STEP = 4
CHUNK_GROUP = 4
GDN_PRE_W = 5 * MIX_W


def _for_chunk_groups(n_chunks, fn):
    if n_chunks == CHUNK_GROUP:
        fn(0)
    else:
        def body(gi, carry):
            fn(gi * CHUNK_GROUP)
            return carry
        lax.fori_loop(0, n_chunks // CHUNK_GROUP, body, 0)


def _gdn_prepare(items, maskbd):
    n = range(len(items))
    qs, ks, vs, gs, betas, masks, revs = zip(*items)
    gcs = [_mm_sel_l(masks[i][2], gs[i]) for i in n]
    decays = []
    for i in n:
        inc, eye = masks[i][0], masks[i][3]
        gc_row = jnp.sum(eye * gcs[i], axis=0, keepdims=True)
        decays.append(jnp.where(inc, jnp.exp(jnp.where(inc, gcs[i] - gc_row, 0.0)), 0.0))
    kbs = [ks[i] * betas[i] for i in n]
    aqs = [_mm_nt(jnp.concatenate([kbs[i], qs[i]], axis=0), _bd(ks[i], maskbd)) for i in n]
    a_mats = [jnp.where(masks[i][1], aqs[i][:CHUNK] * decays[i], 0.0) for i in n]
    t_invs = _neumann_inverse(a_mats, [m[3] for m in masks], maskbd)
    egcs = [jnp.exp(gc) for gc in gcs]
    uws = [_mm(t_invs[i], jnp.concatenate([_bd(vs[i] * betas[i], maskbd), _bd(kbs[i] * egcs[i], maskbd)], axis=1))
           for i in n]
    out = []
    for i in n:
        g_last = gcs[i][0:1] if revs[i] else gcs[i][CHUNK - 1:CHUNK]
        pre = jnp.concatenate([uws[i], qs[i] * egcs[i], aqs[i][CHUNK:] * decays[i],
                               ks[i] * jnp.exp(g_last - gcs[i])], axis=1)
        out.append((pre, jnp.broadcast_to(jnp.exp(g_last), (8, MIX_W))))
    return out


def _gdn_step(items, maskbd):
    w = MIX_W
    n = range(len(items))
    pres, egls, states = zip(*items)
    wqs = [_mm(jnp.concatenate([pres[i][:, w:2 * w], pres[i][:, 2 * w:3 * w]], axis=0), states[i]) for i in n]
    v_news = [pres[i][:, :w] - wqs[i][:CHUNK] for i in n]
    outs = [wqs[i][CHUNK:] + _mm(pres[i][:, 3 * w:4 * w], _bd(v_news[i], maskbd)) for i in n]
    upds = [_mm_tn(pres[i][:, 4 * w:], v_news[i]) for i in n]
    mask_f = maskbd.astype(F32)
    return [(outs[i], states[i] * egls[i] + upds[i] * mask_f) for i in n]


def _gdn_kernel(*refs, seq_len, n_seq, cached):
    if cached:
        (zg_ref, zab_ref, s0_ref, conv_ref, alog_ref, dtb_ref, gn_ref, eexp_ref, ones_ref, maskbd_ref,
         o_ref, sout_ref, q_s, k_s, v_s, ge_s, pre_s, gl_s, oacc_s, st_s) = refs
    else:
        zg_ref, zab_ref, conv_ref, alog_ref, dtb_ref, gn_ref, eexp_ref, ones_ref, maskbd_ref = refs[:9]
        o_ref, sout_ref, q_s, k_s, v_s, ge_s, pre_s, gl_s, oacc_s, st_s = refs[-10:]
    t = seq_len * n_seq
    z = zg_ref[:, :GDN_CONV_CH]
    rowi = jnp.bitwise_and(_iota((t, 1), 0), seq_len - 1)
    zp = jnp.where(rowi == 0, 0.0, pltpu.roll(z, 1, 0))
    zn = jnp.where(rowi == seq_len - 1, 0.0, pltpu.roll(z, t - 1, 0))
    cw = conv_ref[...]
    qkv = _silu(zp * cw[0:1] + z * cw[1:2] + zn * cw[2:3])
    ones_bd = ones_ref[...]
    q = qkv[:, :MIX_W]
    k = qkv[:, MIX_W:2 * MIX_W]
    q_s[...] = q * lax.rsqrt(_mm_sel_r(q * q, ones_bd) + 1e-6) * (HEAD_DIM ** -0.5)
    k_s[...] = k * lax.rsqrt(_mm_sel_r(k * k, ones_bd) + 1e-6)
    v_s[...] = qkv[:, 2 * MIX_W:]
    ab = zab_ref[...]
    lane = _iota((t, LANES), 1)
    gb = jnp.where(lane < 2 * HEADS, -jnp.exp(alog_ref[...]) * _softplus(ab + dtb_ref[...]), _sigmoid(ab))
    ge_s[...] = _mm_sel_r(gb, eexp_ref[...])
    oacc_s[...] = jnp.zeros((t, MIX_W), F32)
    if cached:
        st_s[...] = s0_ref[...]
    else:
        st_s[...] = jnp.zeros((n_seq, 2, MIX_W, MIX_W), F32)
    maskbd = maskbd_ref[...]
    masks = (_chunk_masks(False), _chunk_masks(True))
    n_chunks = seq_len // CHUNK

    def prepare_group(c0):
        where, items = [], []
        for j in range(CHUNK_GROUP):
            c = c0 + j
            rows = pl.ds(pl.multiple_of(c * CHUNK, CHUNK), CHUNK)
            for d in range(2):
                where.append((d, c, rows))
                items.append((q_s[rows, :], k_s[rows, :], v_s[rows, :], ge_s[rows, d * MIX_W:(d + 1) * MIX_W],
                              ge_s[rows, (2 + d) * MIX_W:(3 + d) * MIX_W], masks[d], d == 1))
        for (d, c, rows), (pre, egl) in zip(where, _gdn_prepare(items, maskbd)):
            pre_s[d, rows, :] = pre
            gl_s[d, pl.ds(pl.multiple_of(c * 8, 8), 8), :] = egl

    _for_chunk_groups(n_chunks * n_seq, prepare_group)

    def body(i, carry):
        where, items = [], []
        for q in range(n_seq):
            for d in range(2):
                c = q * n_chunks + (i if d == 0 else n_chunks - 1 - i)
                rows = pl.ds(pl.multiple_of(c * CHUNK, CHUNK), CHUNK)
                where.append((q, d, rows))
                items.append((pre_s[d, rows, :], gl_s[d, pl.ds(pl.multiple_of(c * 8, 8), 8), :][0:1], st_s[q, d]))
        for (q, d, rows), (o, s_new) in zip(where, _gdn_step(items, maskbd)):
            oacc_s[rows, :] = oacc_s[rows, :] + o
            st_s[q, d] = s_new
        return carry

    lax.fori_loop(0, n_chunks, body, 0)
    o = oacc_s[...]
    ms = _mm_sel_r(o * o, ones_bd) * (1.0 / HEAD_DIM)
    gate = zg_ref[:, GDN_CONV_CH:]
    o_ref[...] = o * lax.rsqrt(ms + NORM_EPS) * gn_ref[...] * _silu(gate)
    for q in range(n_seq):
        for d in range(2):
            for h in range(HEADS):
                sl = slice(h * HEAD_DIM, (h + 1) * HEAD_DIM)
                sout_ref[q, d, h] = st_s[q, d, sl, sl]


def _state_out(nb, n_seq, layer, cached, prev, args, in_specs):
    if cached:
        return (pl.BlockSpec((n_seq, 2, HEADS, HEAD_DIM, HEAD_DIM), lambda b: (b, 0, 0, 0, 0)),
                jax.ShapeDtypeStruct((nb, 2, HEADS, HEAD_DIM, HEAD_DIM), F32), {})
    aliases = {len(args): 1}
    args.append(prev)
    in_specs.append(pl.BlockSpec(memory_space=pl.ANY))
    return (pl.BlockSpec((n_seq, None, 2, HEADS, HEAD_DIM, HEAD_DIM), lambda b: (b, layer, 0, 0, 0, 0)),
            jax.ShapeDtypeStruct((nb, DEPTH, 2, HEADS, HEAD_DIM, HEAD_DIM), F32), aliases)


def _gdn(zg, zab, lw, layer, consts, seq_len, s0_bd=None, prev=None):
    n = zg.shape[0]
    nb = n // seq_len
    cached = s0_bd is not None
    n_seq = 1 if cached else SEQ_PER_STEP
    rows = seq_len * n_seq
    args = [zg, zab]
    in_specs = [pl.BlockSpec((rows, ZG_W), lambda b: (b, 0)),
                pl.BlockSpec((rows, LANES), lambda b: (b, 0))]
    if cached:
        args.append(s0_bd)
        in_specs.append(pl.BlockSpec((1, None, 2, MIX_W, MIX_W), lambda b: (b, layer, 0, 0, 0)))
    layered = [lw["gdn_conv"], lw["gdn_alog"], lw["gdn_dtb"], lw["gdn_norm"]]
    const = [consts["eexp"], consts["ones_bd"], consts["maskbd"]]
    args += layered + const
    in_specs += [_layer_spec(a, layer) for a in layered] + [_const_spec(a) for a in const]
    s_spec, s_shape, aliases = _state_out(nb, n_seq, layer, cached, prev, args, in_specs)
    return pl.pallas_call(
        functools.partial(_gdn_kernel, seq_len=seq_len, n_seq=n_seq, cached=cached),
        grid=(nb // n_seq,),
        in_specs=in_specs,
        out_specs=[
            pl.BlockSpec((rows, MIX_W), lambda b: (b, 0)),
            s_spec,
        ],
        out_shape=[jax.ShapeDtypeStruct((n, MIX_W), F32), s_shape],
        input_output_aliases=aliases,
        scratch_shapes=[
            pltpu.VMEM((rows, MIX_W), F32),
            pltpu.VMEM((rows, MIX_W), F32),
            pltpu.VMEM((rows, MIX_W), F32),
            pltpu.VMEM((rows, 4 * MIX_W), F32),
            pltpu.VMEM((2, rows, GDN_PRE_W), F32),
            pltpu.VMEM((2, rows // CHUNK * 8, MIX_W), F32),
            pltpu.VMEM((rows, MIX_W), F32),
            pltpu.VMEM((n_seq, 2, MIX_W, MIX_W), F32),
        ],
        compiler_params=pltpu.CompilerParams(dimension_semantics=("arbitrary",),
                                             vmem_limit_bytes=VMEM_LIMIT),
        name="gdn_lat" if cached else "gdn_ctx",
    )(*args)


RWKV_PRE_W = 7 * MIX_W


def _rwkv_prepare(items, maskbd, eye_full):
    n = range(len(items))
    rs, kds, vs, kks, bs, lws, masks, revs = zip(*items)
    cums = [_mm_sel_l(masks[i][2], lws[i]) for i in n]
    einvs = [jnp.exp(-c) for c in cums]
    kts = [kks[i] * jnp.exp(cums[i] - lws[i]) for i in n]
    rts = [rs[i] * jnp.exp(cums[i]) for i in n]
    krs = [jnp.concatenate([kts[i], rts[i]], axis=0) for i in n]
    lb_alls = [_mm_nt(krs[i], _bd(bs[i] * einvs[i], maskbd)) for i in n]
    lk_alls = [_mm_nt(krs[i], _bd(kds[i] * einvs[i], maskbd)) for i in n]
    lbs = [jnp.where(masks[i][1], lb_alls[i][:CHUNK], 0.0) for i in n]
    t_invs = _neumann_inverse(lbs, [m[3] for m in masks], maskbd)
    lvs = [_mm(jnp.concatenate([jnp.where(masks[i][1], lk_alls[i][:CHUNK], 0.0),
                                jnp.where(masks[i][0], lk_alls[i][CHUNK:], 0.0)], axis=0), _bd(vs[i], maskbd))
           for i in n]
    tkps = [_mm(t_invs[i], jnp.concatenate([_bd(kts[i], maskbd), _bd(lvs[i][:CHUNK], maskbd)], axis=1)) for i in n]
    out = []
    for i in n:
        c_last = cums[i][0:1] if revs[i] else cums[i][CHUNK - 1:CHUNK]
        tail = jnp.exp(c_last - cums[i])
        rb = jnp.where(masks[i][0], lb_alls[i][CHUNK:], 0.0)
        pre = jnp.concatenate([tkps[i][:, :MIX_W], rts[i], tkps[i][:, MIX_W:], lvs[i][CHUNK:], rb,
                               kds[i] * tail, bs[i] * tail], axis=1)
        gcol = jnp.sum(eye_full * jnp.exp(c_last), axis=1, keepdims=True)
        out.append((pre, jnp.broadcast_to(gcol, (MIX_W, LANES))))
    return out


def _rwkv_step(items, maskbd):
    w = MIX_W
    n = range(len(items))
    pres, vs, gcols, states = zip(*items)
    prs = [_mm(jnp.concatenate([pres[i][:, :w], pres[i][:, w:2 * w]], axis=0), states[i]) for i in n]
    ps = [prs[i][:CHUNK] + pres[i][:, 2 * w:3 * w] for i in n]
    outs = [prs[i][CHUNK:] + pres[i][:, 3 * w:4 * w] - _mm(pres[i][:, 4 * w:5 * w], _bd(ps[i], maskbd)) for i in n]
    upds = [_mm_tn(jnp.concatenate([pres[i][:, 5 * w:6 * w], pres[i][:, 6 * w:]], axis=0),
                   jnp.concatenate([vs[i], -ps[i]], axis=0)) for i in n]
    mask_f = maskbd.astype(F32)
    return [(outs[i], states[i] * jnp.concatenate([gcols[i], gcols[i]], axis=1) + upds[i] * mask_f) for i in n]


def _rwkv_kernel(*refs, seq_len, n_seq, cached):
    if cached:
        (zr_ref, s0_ref, mup_ref, mun_ref, w0_ref, w2_ref, a0_ref, a2_ref, g2_ref, kk_ref, ka_ref, rk_ref,
         gnw_ref, gnb_ref, ones_ref, maskbd_ref, o_ref, sout_ref,
         r_s, v_s, kk_s, dir_s, bg_s, pre_s, gcol_s, oacc_s, st_s) = refs
    else:
        (zr_ref, mup_ref, mun_ref, w0_ref, w2_ref, a0_ref, a2_ref, g2_ref, kk_ref, ka_ref, rk_ref,
         gnw_ref, gnb_ref, ones_ref, maskbd_ref) = refs[:15]
        o_ref, sout_ref, r_s, v_s, kk_s, dir_s, bg_s, pre_s, gcol_s, oacc_s, st_s = refs[-11:]
    t = seq_len * n_seq
    z = zr_ref[...]
    rowi = jnp.bitwise_and(_iota((t, 1), 0), seq_len - 1)
    zp = jnp.where(rowi == 0, 0.0, pltpu.roll(z, 1, 0))
    zn = jnp.where(rowi == seq_len - 1, 0.0, pltpu.roll(z, t - 1, 0))
    z = z + mup_ref[...] * (zp - z) + mun_ref[...] * (zn - z)
    w = MIX_W
    r = z[:, :w]
    k = z[:, w:2 * w]
    v = z[:, 2 * w:3 * w]
    wd = jnp.tanh(z[:, 3 * w:3 * w + LANES])
    ad = z[:, 3 * w + LANES:3 * w + 2 * LANES]
    gd = _sigmoid(z[:, 3 * w + 2 * LANES:])
    ones_bd = ones_ref[...]
    kk = k * kk_ref[...]
    kk = kk * lax.rsqrt(_mm_sel_r(kk * kk, ones_bd) + 1e-6)
    r_s[...] = r
    v_s[...] = v
    kk_s[...] = kk
    bonus = jnp.zeros((t, w), F32)
    for d in range(2):
        w_log = -_softplus(-(w0_ref[d:d + 1] + _mm(wd, w2_ref[d]))) - 0.5
        a = _sigmoid(a0_ref[d:d + 1] + _mm(ad, a2_ref[d]))
        kd = k * (1.0 + (a - 1.0) * ka_ref[...])
        dir_s[:, (3 * d) * w:(3 * d + 1) * w] = -jnp.exp(w_log)
        dir_s[:, (3 * d + 1) * w:(3 * d + 2) * w] = kd
        dir_s[:, (3 * d + 2) * w:(3 * d + 3) * w] = kk * a
        bonus = bonus + _mm_sel_r(r * kd * rk_ref[...], ones_bd) * v
    bg_s[:, :w] = bonus
    bg_s[:, w:] = _mm(gd, g2_ref[...])
    oacc_s[...] = jnp.zeros((t, w), F32)
    if cached:
        st_s[...] = s0_ref[...]
    else:
        st_s[...] = jnp.zeros((n_seq, 2, w, w), F32)
    maskbd = maskbd_ref[...]
    masks = (_chunk_masks(False), _chunk_masks(True))
    eye_full = jnp.where(_iota((w, w), 0) == _iota((w, w), 1), 1.0, 0.0).astype(F32)
    n_chunks = seq_len // CHUNK

    def prepare_group(c0):
        where, items = [], []
        for j in range(CHUNK_GROUP):
            c = c0 + j
            rows = pl.ds(pl.multiple_of(c * CHUNK, CHUNK), CHUNK)
            for d in range(2):
                where.append((d, c, rows))
                items.append((r_s[rows, :], dir_s[rows, (3 * d + 1) * w:(3 * d + 2) * w], v_s[rows, :], kk_s[rows, :],
                              dir_s[rows, (3 * d + 2) * w:(3 * d + 3) * w], dir_s[rows, (3 * d) * w:(3 * d + 1) * w],
                              masks[d], d == 1))
        for (d, c, rows), (pre, gcol) in zip(where, _rwkv_prepare(items, maskbd, eye_full)):
            pre_s[d, rows, :] = pre
            gcol_s[d, pl.ds(pl.multiple_of(c * w, w), w), :] = gcol

    _for_chunk_groups(n_chunks * n_seq, prepare_group)

    def body(i, carry):
        where, items = [], []
        for q in range(n_seq):
            for d in range(2):
                c = q * n_chunks + (i if d == 0 else n_chunks - 1 - i)
                rows = pl.ds(pl.multiple_of(c * CHUNK, CHUNK), CHUNK)
                where.append((q, d, rows))
                items.append((pre_s[d, rows, :], v_s[rows, :], gcol_s[d, pl.ds(pl.multiple_of(c * w, w), w), :],
                              st_s[q, d]))
        for (q, d, rows), (o, z_new) in zip(where, _rwkv_step(items, maskbd)):
            oacc_s[rows, :] = oacc_s[rows, :] + o
            st_s[q, d] = z_new
        return carry

    lax.fori_loop(0, n_chunks, body, 0)
    o = oacc_s[...]
    inv_n = 1.0 / HEAD_DIM
    mu = _mm_sel_r(o, ones_bd) * inv_n
    oc = o - mu
    var = _mm_sel_r(oc * oc, ones_bd) * inv_n
    y = oc * lax.rsqrt(var + RWKV_GN_EPS) * gnw_ref[...] + gnb_ref[...]
    o_ref[...] = (y + bg_s[:, :w]) * bg_s[:, w:]
    for q in range(n_seq):
        for d in range(2):
            state = st_s[q, d] if cached else st_s[q, d].T
            for h in range(HEADS):
                sl = slice(h * HEAD_DIM, (h + 1) * HEAD_DIM)
                sout_ref[q, d, h] = state[sl, sl]


def _rwkv(zr, lw, layer, consts, seq_len, s0_bd=None, prev=None):
    n = zr.shape[0]
    nb = n // seq_len
    cached = s0_bd is not None
    n_seq = 1 if cached else SEQ_PER_STEP
    rows = seq_len * n_seq
    args = [zr]
    in_specs = [pl.BlockSpec((rows, ZR_W), lambda b: (b, 0))]
    if cached:
        args.append(s0_bd)
        in_specs.append(pl.BlockSpec((1, None, 2, MIX_W, MIX_W), lambda b: (b, layer, 0, 0, 0)))
    layered = [lw["rwkv_mu_prev"], lw["rwkv_mu_next"], lw["rwkv_w0"], lw["rwkv_w2"], lw["rwkv_a0"], lw["rwkv_a2"],
               lw["rwkv_g2"], lw["rwkv_k_k"], lw["rwkv_k_a"], lw["rwkv_r_k"], lw["rwkv_gn_w"], lw["rwkv_gn_b"]]
    const = [consts["ones_bd"], consts["maskbd"]]
    args += layered + const
    in_specs += [_layer_spec(a, layer) for a in layered] + [_const_spec(a) for a in const]
    s_spec, s_shape, aliases = _state_out(nb, n_seq, layer, cached, prev, args, in_specs)
    return pl.pallas_call(
        functools.partial(_rwkv_kernel, seq_len=seq_len, n_seq=n_seq, cached=cached),
        grid=(nb // n_seq,),
        in_specs=in_specs,
        out_specs=[
            pl.BlockSpec((rows, MIX_W), lambda b: (b, 0)),
            s_spec,
        ],
        out_shape=[jax.ShapeDtypeStruct((n, MIX_W), F32), s_shape],
        input_output_aliases=aliases,
        scratch_shapes=[
            pltpu.VMEM((rows, MIX_W), F32),
            pltpu.VMEM((rows, MIX_W), F32),
            pltpu.VMEM((rows, MIX_W), F32),
            pltpu.VMEM((rows, 6 * MIX_W), F32),
            pltpu.VMEM((rows, 2 * MIX_W), F32),
            pltpu.VMEM((2, rows, RWKV_PRE_W), F32),
            pltpu.VMEM((2, rows // CHUNK * MIX_W, LANES), F32),
            pltpu.VMEM((rows, MIX_W), F32),
            pltpu.VMEM((n_seq, 2, MIX_W, MIX_W), F32),
        ],
        compiler_params=pltpu.CompilerParams(dimension_semantics=("arbitrary",),
                                             vmem_limit_bytes=VMEM_LIMIT),
        name="rwkv_lat" if cached else "rwkv_ctx",
    )(*args)


def _route(logits_t, bias):
    tm = logits_t.shape[1]
    neg = -jnp.inf
    sc = _sigmoid(logits_t)
    sc3 = sc.reshape(N_GROUPS, GROUP_SIZE, tm)
    sel = (sc + bias).reshape(N_GROUPS, GROUP_SIZE, tm)
    si = _iota(sel.shape, 1).astype(F32)
    m1 = jnp.max(sel, axis=1, keepdims=True)
    f1 = jnp.min(jnp.where(sel == m1, si, float(GROUP_SIZE)), axis=1, keepdims=True)
    m2 = jnp.max(jnp.where(si == f1, neg, sel), axis=1, keepdims=True)
    grp = m1 + m2
    gi = _iota(grp.shape, 0).astype(F32)
    gsel = jnp.zeros(grp.shape, F32)
    for _ in range(TOPK_GROUPS):
        mx = jnp.max(grp, axis=0, keepdims=True)
        fi = jnp.min(jnp.where(grp == mx, gi, float(N_GROUPS)), axis=0, keepdims=True)
        hit = gi == fi
        gsel = jnp.where(hit, 1.0, gsel)
        grp = jnp.where(hit, neg, grp)
    cur = jnp.where(gsel > 0.0, sel, neg)
    ei = (_iota(cur.shape, 0) * GROUP_SIZE + _iota(cur.shape, 1)).astype(F32)
    chosen = jnp.zeros(cur.shape, F32)
    ids, wts = [], []
    for _ in range(TOP_K):
        mx = jnp.max(jnp.max(cur, axis=0, keepdims=True), axis=1, keepdims=True)
        fi = jnp.min(jnp.min(jnp.where(cur == mx, ei, float(N_EXPERTS)), axis=0, keepdims=True),
                     axis=1, keepdims=True)
        hit = ei == fi
        chosen = jnp.where(hit, 1.0, chosen)
        cur = jnp.where(hit, neg, cur)
        ids.append(fi.reshape(1, tm))
        wts.append(jnp.sum(jnp.sum(jnp.where(hit, sc3, 0.0), axis=0, keepdims=True), axis=1, keepdims=True)
                   .reshape(1, tm))
    w = jnp.concatenate(wts, axis=0)
    w = w / jnp.sum(w, axis=0, keepdims=True) * ROUTE_SCALE
    return chosen.reshape(N_EXPERTS, tm), jnp.concatenate(ids, axis=0), w


def _pack_halves(x):
    half = x.shape[1] // 2
    bits = lax.bitcast_convert_type(x.astype(BF16).astype(F32), jnp.int32)
    lo = lax.shift_right_logical(bits[:, :half], jnp.int32(16))
    return jnp.bitwise_or(lo, jnp.bitwise_and(bits[:, half:], jnp.int32(-65536)))


def _unpack_halves(word):
    lo = lax.bitcast_convert_type(lax.shift_left(word, jnp.int32(16)), F32)
    hi = lax.bitcast_convert_type(jnp.bitwise_and(word, jnp.int32(-65536)), F32)
    return lo, hi


def _post_kernel(x_ref, om_ref, og_ref, or_ref, mod_ref, wo_ref, n2_ref, rt_ref, rb_ref, sgu_ref, sdn_ref,
                 tri_ref, cin_ref, x1_ref, h2_ref, eid_ref, rank_ref, ew_ref, cnt_ref, carry_s):
    @pl.when(pl.program_id(0) == 0)
    def _():
        carry_s[...] = cin_ref[...]

    m = mod_ref[0]
    g1 = m[:, 2 * D_MODEL:3 * D_MODEL]
    sh2 = m[:, 3 * D_MODEL:4 * D_MODEL]
    sc2 = m[:, 4 * D_MODEL:5 * D_MODEL]
    w_mla = MLA_HEADS * MLA_V
    mix = (_mm(om_ref[...], wo_ref[0:w_mla, :]) + _mm(og_ref[...], wo_ref[w_mla:w_mla + MIX_W, :])
           + _mm(or_ref[...], wo_ref[w_mla + MIX_W:, :]))
    x1 = x_ref[...] + g1 * mix
    h2 = _rms(x1, n2_ref[...]) * (1.0 + sc2) + sh2
    h2_ref[...] = _pack_halves(h2)
    g2 = m[:, 5 * D_MODEL:]
    x1_ref[...] = x1 + g2 * _mm(_swiglu_act(_mm(h2, sgu_ref[...])), sdn_ref[...])
    r_hi, r_lo = _split2(rt_ref[...])
    h_hi, h_lo = _split2(h2)
    logits_t = _mm_nt(r_hi, h_hi) + _mm_nt(r_hi, h_lo) + _mm_nt(r_lo, h_hi)
    chosen, ids, w = _route(logits_t, rb_ref[...])
    tm = chosen.shape[1]
    rank_et = (carry_s[:, 0:1] + _mm(chosen, tri_ref[...])).reshape(N_GROUPS, GROUP_SIZE, tm)
    ei = (_iota(rank_et.shape, 0) * GROUP_SIZE + _iota(rank_et.shape, 1)).astype(F32)
    ranks = []
    for k in range(TOP_K):
        pick = jnp.where(ei == ids[k:k + 1].reshape(1, 1, tm), rank_et, 0.0)
        ranks.append(jnp.sum(jnp.sum(pick, axis=0, keepdims=True), axis=1, keepdims=True).reshape(1, tm))
    eid_ref[...] = ids.astype(jnp.int32)
    rank_ref[...] = jnp.concatenate(ranks, axis=0).astype(jnp.int32)
    ew_ref[...] = jnp.concatenate([w, jnp.zeros((LANES - TOP_K, tm), F32)], axis=0).T
    total = carry_s[...] + jnp.sum(chosen, axis=1, keepdims=True)
    carry_s[...] = total
    cnt_ref[...] = total


def _post(x2d, om, og, orw, mods, lw, layer, consts, tm, seq_len, mod_base, counts_in):
    n = x2d.shape[0]
    tiles_per_seq = seq_len // tm if mod_base else 1

    def mod_idx(i):
        return (layer, mod_base + i // tiles_per_seq if mod_base else 0, 0, 0)

    row = lambda w: pl.BlockSpec((tm, w), lambda i: (i, 0))
    col = lambda h: pl.BlockSpec((h, tm), lambda i: (0, i))
    full = lambda a: _layer_spec(a, layer)
    tail = [lw["w_out"], lw["norm2"], lw["router_t"], lw["router_b"], lw["shared_w_gu"], lw["shared_w_down"]]
    tri = consts["tri_tokens"]
    return pl.pallas_call(
        _post_kernel,
        grid=(n // tm,),
        in_specs=[row(D_MODEL), row(MLA_HEADS * MLA_V), row(MIX_W), row(MIX_W),
                  pl.BlockSpec((None, 1, 1, 6 * D_MODEL), mod_idx)] + [full(a) for a in tail]
        + [_const_spec(tri), _const_spec(counts_in)],
        out_specs=[row(D_MODEL), row(D_MODEL // 2), col(TOP_K), col(TOP_K), row(LANES),
                   pl.BlockSpec((N_EXPERTS, LANES), lambda i: (0, 0))],
        out_shape=[
            jax.ShapeDtypeStruct((n, D_MODEL), F32),
            jax.ShapeDtypeStruct((n, D_MODEL // 2), jnp.int32),
            jax.ShapeDtypeStruct((TOP_K, n), jnp.int32),
            jax.ShapeDtypeStruct((TOP_K, n), jnp.int32),
            jax.ShapeDtypeStruct((n, LANES), F32),
            jax.ShapeDtypeStruct((N_EXPERTS, LANES), F32),
        ],
        scratch_shapes=[pltpu.VMEM((N_EXPERTS, LANES), F32)],
        compiler_params=pltpu.CompilerParams(dimension_semantics=("arbitrary",),
                                             vmem_limit_bytes=VMEM_LIMIT),
        name="post",
    )(x2d, om, og, orw, mods, *tail, tri, counts_in)


MOE_ROWS = 512
SC_ROWS = 128
SC_SUBCORES = 32


def _swiglu_act(gu):
    return _silu(gu[:, :D_EXPERT]) * gu[:, D_EXPERT:]


def _dispatch_plan(eid, rank, counts, n, rows):
    n_blocks = n * TOP_K // rows + N_EXPERTS
    cnt = counts[:, 0].astype(jnp.int32)
    blocks = (cnt + rows - 1) // rows
    block_end = jnp.cumsum(blocks)
    offset = (block_end - blocks) * rows
    experts = jnp.arange(N_EXPERTS, dtype=jnp.int32)
    dest = jnp.sum(jnp.where(eid[..., None] == experts, offset, 0), axis=-1) + rank
    block_ids = jnp.arange(n_blocks, dtype=jnp.int32)
    block_expert = jnp.minimum(jnp.sum((block_end[None, :] <= block_ids[:, None]).astype(jnp.int32), axis=1),
                               N_EXPERTS - 1)
    return dest, block_expert, block_end[-1:].astype(jnp.int32), n_blocks


def _sc_mesh():
    return plsc.VectorSubcoreMesh(core_axis_name="core", subcore_axis_name="subcore")


def _sc_dispatch(groups, dest, n_rows):
    w = groups[0].shape[1]
    dtype = groups[0].dtype

    @functools.partial(pl.kernel, out_type=jax.ShapeDtypeStruct((n_rows, w), dtype), mesh=_sc_mesh(),
                       scratch_types=[pltpu.VMEM((SC_ROWS, w), dtype), pltpu.VMEM((TOP_K, SC_ROWS), jnp.int32)])
    def kern(*refs):
        x_refs, d_hbm, o_hbm, xv, dv = refs[:len(groups)], *refs[len(groups):]
        sid = lax.axis_index("core") * (SC_SUBCORES // 2) + lax.axis_index("subcore")
        start = 0
        for x_hbm, x in zip(x_refs, groups):
            def chunk(c, x_hbm=x_hbm, start=start):
                r0 = pl.multiple_of(c * SC_ROWS, SC_ROWS)
                pltpu.sync_copy(x_hbm.at[pl.ds(r0, SC_ROWS)], xv)
                pltpu.sync_copy(d_hbm.at[:, pl.ds(start + r0, SC_ROWS)], dv)
                for k in range(TOP_K):
                    pltpu.sync_copy(xv, o_hbm.at[dv.at[k]])

            pl.loop(sid, x.shape[0] // SC_ROWS, step=SC_SUBCORES)(chunk)
            start += x.shape[0]

    return kern(*groups, dest)


def _sc_gather(y, idx):
    w = y.shape[1]
    n_chunks = idx.shape[0]

    @functools.partial(pl.kernel, out_type=jax.ShapeDtypeStruct((n_chunks * SC_ROWS, w), y.dtype), mesh=_sc_mesh(),
                       scratch_types=[pltpu.VMEM((SC_ROWS, w), y.dtype), pltpu.VMEM((1, SC_ROWS), jnp.int32)])
    def kern(y_hbm, i_hbm, o_hbm, ov, iv):
        sid = lax.axis_index("core") * (SC_SUBCORES // 2) + lax.axis_index("subcore")

        @pl.loop(sid, n_chunks, step=SC_SUBCORES)
        def _(c):
            pltpu.sync_copy(i_hbm.at[pl.ds(c, 1)], iv)
            pltpu.sync_copy(y_hbm.at[iv.at[0]], ov)
            pltpu.sync_copy(ov, o_hbm.at[pl.ds(pl.multiple_of(c * SC_ROWS, SC_ROWS), SC_ROWS)])

    return kern(y, idx)


def _moe_rows_kernel(be_ref, nu_ref, x_ref, wgu_ref, wdn_ref, y_ref):
    @pl.when(pl.program_id(0) < nu_ref[0])
    def _():
        half = D_MODEL // 2
        lo, hi = _unpack_halves(x_ref[...])
        gu = _mm(lo, wgu_ref[0:half, :]) + _mm(hi, wgu_ref[half:, :])
        y_ref[...] = _pack_halves(_mm(_swiglu_act(gu), wdn_ref[...]))


def _moe_rows(xs, block_expert, n_used, lw, layer, n_blocks, rows):
    half = D_MODEL // 2
    last = lambda b, be, nu: jnp.minimum(b, nu[0] - 1)
    return pl.pallas_call(
        _moe_rows_kernel,
        grid_spec=pltpu.PrefetchScalarGridSpec(
            num_scalar_prefetch=2,
            grid=(n_blocks,),
            in_specs=[pl.BlockSpec((rows, half), lambda b, be, nu: (last(b, be, nu), 0)),
                      pl.BlockSpec((None, None, D_MODEL, 2 * D_EXPERT),
                                   lambda b, be, nu: (layer, be[last(b, be, nu)], 0, 0)),
                      pl.BlockSpec((None, None, D_EXPERT, D_MODEL),
                                   lambda b, be, nu: (layer, be[last(b, be, nu)], 0, 0))],
            out_specs=pl.BlockSpec((rows, half), lambda b, be, nu: (last(b, be, nu), 0)),
        ),
        out_shape=jax.ShapeDtypeStruct(xs.shape, jnp.int32),
        compiler_params=pltpu.CompilerParams(dimension_semantics=("arbitrary",),
                                             vmem_limit_bytes=VMEM_LIMIT),
        name="moe_rows",
    )(block_expert, n_used, xs, lw["moe_w_gu"], lw["moe_w_down"])


def _moe_combine_kernel(yg_ref, ew_ref, x1_ref, mod_ref, nf_ref, o_ref, *, final):
    ew = ew_ref[...]
    acc_lo = acc_hi = None
    for k in range(TOP_K):
        lo, hi = _unpack_halves(yg_ref[k])
        wk = ew[:, k:k + 1]
        acc_lo = wk * lo if acc_lo is None else acc_lo + wk * lo
        acc_hi = wk * hi if acc_hi is None else acc_hi + wk * hi
    g2 = mod_ref[0][:, 5 * D_MODEL:]
    x2 = x1_ref[...] + g2 * jnp.concatenate([acc_lo, acc_hi], axis=1)
    if final:
        x2 = _rms(x2, nf_ref[...])
    o_ref[...] = x2


def _moe_combine(yg, row0, ew, x1, mods, layer, norm_f, tm, seq_len, mod_base, final):
    n = x1.shape[0]
    half = D_MODEL // 2
    tiles_per_seq = seq_len // tm if mod_base else 1
    tile0 = row0 // tm

    def mod_idx(i):
        return (layer, mod_base + i // tiles_per_seq if mod_base else 0, 0, 0)

    row = lambda w: pl.BlockSpec((tm, w), lambda i: (i, 0))
    return pl.pallas_call(
        functools.partial(_moe_combine_kernel, final=final),
        grid=(n // tm,),
        in_specs=[pl.BlockSpec((TOP_K, tm, half), lambda i: (0, tile0 + i, 0)), row(LANES), row(D_MODEL),
                  pl.BlockSpec((None, 1, 1, 6 * D_MODEL), mod_idx), _const_spec(norm_f)],
        out_specs=row(D_MODEL),
        out_shape=jax.ShapeDtypeStruct((n, D_MODEL), F32),
        compiler_params=pltpu.CompilerParams(dimension_semantics=("arbitrary",),
                                             vmem_limit_bytes=VMEM_LIMIT),
        name="moe_combine_final" if final else "moe_combine",
    )(yg, ew, x1, mods, norm_f)


def _moe_experts(groups, eid, rank, counts, lw, layer):
    group_sizes = [g.shape[0] for g in groups]
    n = sum(group_sizes)
    dest, block_expert, n_used, n_blocks = _dispatch_plan(eid, rank, counts, n, MOE_ROWS)
    xs = _sc_dispatch(groups, dest, n_blocks * MOE_ROWS)
    y = _moe_rows(xs, block_expert, n_used, lw, layer, n_blocks, MOE_ROWS)
    outs, start = [], 0
    for size in group_sizes:
        idx = dest[:, start:start + size].reshape(size * TOP_K // SC_ROWS, SC_ROWS)
        outs.append(_sc_gather(y, idx).reshape(TOP_K, size, D_MODEL // 2))
        start += size
    return outs


def _constants():
    idx = np.arange(MIX_W)
    same_head = (idx[:, None] // HEAD_DIM) == (idx[None, :] // HEAD_DIM)
    maskbd = jnp.asarray(same_head, BF16)
    eexp = np.zeros((LANES, 4 * MIX_W), np.float32)
    for blk in range(4):
        kind, d = divmod(blk, 2)
        for h in range(HEADS):
            src = kind * 2 * HEADS + d * HEADS + h
            eexp[src, blk * MIX_W + h * HEAD_DIM: blk * MIX_W + (h + 1) * HEAD_DIM] = 1.0
    tri = np.triu(np.ones((POST_TM, POST_TM), np.float32), 1)
    return {"maskbd": maskbd, "ones_bd": maskbd, "eexp": jnp.asarray(eexp, BF16), "tri_tokens": jnp.asarray(tri, BF16)}


def _rope_tables(n):
    rows = n // GRID_W
    row = jnp.repeat(jnp.arange(rows, dtype=F32), GRID_W)
    col = jnp.tile(jnp.arange(GRID_W, dtype=F32), rows)
    axis_dim = MLA_ROPE // 2
    inv = jnp.power(ROPE_BASE, -jnp.arange(0, axis_dim, 2, dtype=F32) / axis_dim)
    ang_r = row[:, None] * inv
    ang_c = col[:, None] * inv
    cr, sr, cc, sc = jnp.cos(ang_r), jnp.sin(ang_r), jnp.cos(ang_c), jnp.sin(ang_c)
    zeros = jnp.zeros((n, LANES - MLA_ROPE), F32)
    cos_t = jnp.concatenate([cr, cr, cc, cc, zeros], axis=1)
    sin_t = jnp.concatenate([-sr, sr, -sc, sc, zeros], axis=1)
    return cos_t, sin_t


def _stacked_weights(p):
    w_uq = p["mla_w_uq"].reshape(DEPTH, Q_LORA, MLA_HEADS, MLA_NOPE + MLA_ROPE)
    zq = jnp.zeros((DEPTH, Q_LORA, MLA_HEADS, QH_W - MLA_NOPE - MLA_ROPE), F32)
    w_uq_a = jnp.concatenate([w_uq, zq], axis=-1).reshape(DEPTH, Q_LORA, MLA_HEADS * QH_W).astype(BF16)
    w_uq_sw = jnp.concatenate([jnp.zeros((DEPTH, Q_LORA, MLA_HEADS, MLA_NOPE), F32),
                               w_uq[..., MLA_NOPE + _ROPE_SWAP], zq], axis=-1)
    w_uq_sw = w_uq_sw.reshape(DEPTH, Q_LORA, MLA_HEADS * QH_W).astype(BF16)

    def per_direction(w):
        half = jnp.zeros((DEPTH, 64, MIX_W), F32)
        return jnp.stack([jnp.concatenate([w[:, 0], half], axis=1),
                          jnp.concatenate([half, w[:, 1]], axis=1)], axis=1).astype(BF16)

    row = lambda v: v.reshape(DEPTH, 1, -1)
    pad_row = lambda v: jnp.pad(row(v), ((0, 0), (0, 0), (0, LANES - 2 * HEADS)))
    return {
        "norm1": row(p["norm1"]),
        "w_in": p["w_in"],
        "q_norm": row(p["mla_q_norm"]),
        "w_uq": w_uq_a, "w_uq_sw": w_uq_sw,
        "kv_norm": row(p["mla_kv_norm"]),
        "w_ukv": p["mla_w_ukv"].astype(BF16),
        "gdn_conv": p["gdn_conv"],
        "gdn_alog": pad_row(p["gdn_a_log"]),
        "gdn_dtb": pad_row(p["gdn_dt_bias"]),
        "gdn_norm": jnp.tile(row(p["gdn_norm"]), (1, 1, HEADS)),
        "rwkv_mu_prev": row(p["rwkv_mu_prev"]),
        "rwkv_mu_next": row(p["rwkv_mu_next"]),
        "rwkv_w0": p["rwkv_w0"],
        "rwkv_w2": per_direction(p["rwkv_w2"]),
        "rwkv_a0": p["rwkv_a0"],
        "rwkv_a2": per_direction(p["rwkv_a2"]),
        "rwkv_g2": p["rwkv_g2"].astype(BF16),
        "rwkv_k_k": row(p["rwkv_k_k"]),
        "rwkv_k_a": row(p["rwkv_k_a"]),
        "rwkv_r_k": row(p["rwkv_r_k"]),
        "rwkv_gn_w": row(p["rwkv_gn_w"]),
        "rwkv_gn_b": row(p["rwkv_gn_b"]),
        "w_out": p["w_out"].astype(BF16),
        "norm2": row(p["norm2"]),
        "router_t": jnp.swapaxes(p["moe_router"], 1, 2),
        "router_b": p["moe_bias"].reshape(DEPTH, N_EXPERTS, 1),
        "moe_w_gu": p["moe_w_gu"],
        "moe_w_down": p["moe_w_down"],
        "shared_w_gu": p["shared_w_gu"].astype(BF16),
        "shared_w_down": p["shared_w_down"].astype(BF16),
    }


def _embed_block_diag(s):
    b = s.shape[0]
    eye = jnp.eye(HEADS, dtype=s.dtype)
    out = jnp.einsum("bdhkv,hg->bdhkgv", s, eye)
    return out.reshape(b, 2, MIX_W, MIX_W)


def _layer_front(x2d, mods, lw, l, consts, seq_len, mod_base, cache, tm, tq, counts_in, prev=None):
    zm, zg, zab, zr = _inproj(x2d, mods, lw, l, tm, seq_len, mod_base)
    if cache is None:
        o_mla, ckv, kpe = _mla(zm, lw, l, seq_len, tq, prev=prev[:2])
        o_gdn, s_gdn = _gdn(zg, zab, lw, l, consts, seq_len, prev=prev[2])
        o_rwkv, s_rwkv = _rwkv(zr, lw, l, consts, seq_len, prev=prev[3])
        new = (ckv, kpe, s_gdn, s_rwkv)
    else:
        cckv, ckpe, rc, rs, sg, sr = cache
        o_mla, _ = _mla(zm, lw, l, seq_len, tq, (cckv, ckpe, rc, rs))
        o_gdn, _ = _gdn(zg, zab, lw, l, consts, seq_len, sg)
        o_rwkv, _ = _rwkv(zr, lw, l, consts, seq_len, sr)
        new = None
    routed = _post(x2d, o_mla, o_gdn, o_rwkv, mods, lw, l, consts, tm, seq_len, mod_base, counts_in)
    return routed, new


def kernel(x_prompt, x_sample, cache_mla_ckv, cache_mla_kpe, state_gdn, state_rwkv, c, c_ctx, ada_w, ada_b, norm1, w_in, mla_q_norm, mla_w_uq, mla_kv_norm, mla_w_ukv, gdn_conv, gdn_a_log, gdn_dt_bias, gdn_norm, rwkv_mu_prev, rwkv_mu_next, rwkv_w0, rwkv_w2, rwkv_a0, rwkv_a2, rwkv_g2, rwkv_k_k, rwkv_k_a, rwkv_r_k, rwkv_gn_w, rwkv_gn_b, w_out, norm2, moe_router, moe_bias, moe_w_gu, moe_w_down, shared_w_gu, shared_w_down, norm_f):
    p = dict(norm1=norm1, w_in=w_in, mla_q_norm=mla_q_norm, mla_w_uq=mla_w_uq, mla_kv_norm=mla_kv_norm,
             mla_w_ukv=mla_w_ukv, gdn_conv=gdn_conv, gdn_a_log=gdn_a_log, gdn_dt_bias=gdn_dt_bias,
             gdn_norm=gdn_norm, rwkv_mu_prev=rwkv_mu_prev, rwkv_mu_next=rwkv_mu_next, rwkv_w0=rwkv_w0,
             rwkv_w2=rwkv_w2, rwkv_a0=rwkv_a0, rwkv_a2=rwkv_a2, rwkv_g2=rwkv_g2, rwkv_k_k=rwkv_k_k,
             rwkv_k_a=rwkv_k_a, rwkv_r_k=rwkv_r_k, rwkv_gn_w=rwkv_gn_w, rwkv_gn_b=rwkv_gn_b, w_out=w_out,
             norm2=norm2, moe_router=moe_router, moe_bias=moe_bias, moe_w_gu=moe_w_gu, moe_w_down=moe_w_down,
             shared_w_gu=shared_w_gu, shared_w_down=shared_w_down)
    weights = _stacked_weights(p)
    consts = _constants()
    nf = norm_f.reshape(1, D_MODEL)
    b_ctx, t_ctx, _ = x_prompt.shape
    b_lat, t_lat, _ = x_sample.shape

    cvec8 = jnp.concatenate([c_ctx[None, :], c, jnp.zeros((8 - 1 - b_lat, D_MODEL), F32)], axis=0)
    mods = _adaln(cvec8, ada_w, ada_b)
    mods = mods.reshape(DEPTH, 8, 1, 6 * D_MODEL)

    rc, rs = _rope_tables(t_lat)
    ckpe = jnp.pad(cache_mla_kpe, ((0, 0), (0, 0), (0, 0), (0, LANES - MLA_ROPE)))
    cache = (cache_mla_ckv, ckpe, rc, rs, _embed_block_diag_layers(state_gdn),
             _embed_block_diag_layers(jnp.swapaxes(state_rwkv, -1, -2)))
    xp = x_prompt.reshape(b_ctx * t_ctx, D_MODEL)
    xs = x_sample.reshape(b_lat * t_lat, D_MODEL)
    n_ctx = xp.shape[0]
    tm = POST_TM
    state_shape = (b_ctx, DEPTH, 2, HEADS, HEAD_DIM, HEAD_DIM)
    ctx_outs = (jnp.zeros((b_ctx, DEPTH, t_ctx, KV_LORA), F32), jnp.zeros((b_ctx, DEPTH, t_ctx, MLA_ROPE), F32),
                jnp.zeros(state_shape, F32), jnp.zeros(state_shape, F32))
    for l in range(DEPTH):
        final = l == DEPTH - 1
        no_pairs = jnp.zeros((N_EXPERTS, LANES), F32)
        (x1c, hc, eidc, rankc, ewc, cnt_c), ctx_outs = _layer_front(xp, mods, weights, l, consts, t_ctx, 0, None,
                                                                    tm, t_ctx, no_pairs, ctx_outs)
        (x1s, hs, eids, ranks, ews, cnt), _ = _layer_front(xs, mods, weights, l, consts, t_lat, 1, cache,
                                                           tm, 256, cnt_c)
        yg_c, yg_s = _moe_experts([hc, hs], jnp.concatenate([eidc, eids], axis=1),
                                  jnp.concatenate([rankc, ranks], axis=1), cnt, weights, l)
        xp = _moe_combine(yg_c, 0, ewc, x1c, mods, l, nf, tm, t_ctx, 0, final)
        xs = _moe_combine(yg_s, 0, ews, x1s, mods, l, nf, tm, t_lat, 1, final)

    y_prompt = xp.reshape(b_ctx, t_ctx, D_MODEL)
    y_sample = xs.reshape(b_lat, t_lat, D_MODEL)
    new_ckv, new_kpe, new_gdn, new_rwkv = ctx_outs
    return (y_prompt, y_sample, new_ckv, new_kpe, new_gdn, new_rwkv)


def _embed_block_diag_layers(s):
    b = s.shape[0]
    return _embed_block_diag(s.reshape(b * DEPTH, 2, HEADS, HEAD_DIM, HEAD_DIM)).reshape(
        b, DEPTH, 2, MIX_W, MIX_W)
```

```python
import functools

import numpy as np
import jax
import jax.numpy as jnp
from jax import lax
from jax.experimental import pallas as pl
from jax.experimental.pallas import tpu as pltpu
from jax.experimental.pallas import tpu_sc as plsc

F32 = jnp.float32
BF16 = jnp.bfloat16

D_MODEL = 1024
BATCH = 32
SEQ = 256
DEPTH = 2
DEC_BATCH = 2
DEC_SEQ = 1024
PAST_LEN = 512
GRID_W = 64
NORM_EPS = 1e-6

MLA_HEADS = 4
MLA_NOPE = 128
MLA_ROPE = 64
MLA_V = 128
Q_LORA = 384
KV_LORA = 256
ROPE_BASE = 10000.0
MLA_SCALE = (MLA_NOPE + MLA_ROPE) ** -0.5

HEADS = 4
HEAD_DIM = 64
MIX_W = HEADS * HEAD_DIM
GDN_CONV_CH = 3 * MIX_W
CHUNK = 64
RWKV_GN_EPS = 64e-5

N_EXPERTS = 64
TOP_K = 8
N_GROUPS = 8
GROUP_SIZE = N_EXPERTS // N_GROUPS
TOPK_GROUPS = 4
D_EXPERT = 256
ROUTE_SCALE = 2.5

P_MLA = Q_LORA + KV_LORA + MLA_ROPE
P_GDN = GDN_CONV_CH + MIX_W + 4 * HEADS
P_RWKV = 3 * MIX_W + 128 + 128 + 128

LANES = 128
ZM_W = Q_LORA + KV_LORA + 2 * LANES
ZG_W = GDN_CONV_CH + MIX_W
ZR_W = P_RWKV
QH_W = 2 * LANES
VMEM_LIMIT = 56 * 1024 * 1024
POST_TM = 512

_ROPE_SWAP = np.concatenate([np.arange(16, 32), np.arange(0, 16), np.arange(48, 64), np.arange(32, 48)])


def _sigmoid(x):
    return 1.0 / (1.0 + jnp.exp(-x))


def _silu(x):
    return x * _sigmoid(x)


def _softplus(x):
    return jnp.maximum(x, 0.0) + jnp.log(1.0 + jnp.exp(-jnp.abs(x)))


def _rms(x, g, eps=NORM_EPS):
    return x * lax.rsqrt(jnp.mean(x * x, axis=-1, keepdims=True) + eps) * g


def _mm(a, b):
    return jnp.dot(a.astype(BF16), b.astype(BF16), preferred_element_type=F32)


def _mm_nt(a, b):
    return lax.dot_general(a.astype(BF16), b.astype(BF16), (((1,), (1,)), ((), ())),
                           preferred_element_type=F32)


def _mm_tn(a, b):
    return lax.dot_general(a.astype(BF16), b.astype(BF16), (((0,), (0,)), ((), ())),
                           preferred_element_type=F32)


def _split3(x):
    p1 = x.astype(BF16)
    r1 = x - p1.astype(F32)
    p2 = r1.astype(BF16)
    r2 = r1 - p2.astype(F32)
    return p1, p2, r2.astype(BF16)


def _mm_sel_l(sel, x):
    p1, p2, p3 = _split3(x)
    return _mm(sel, p1) + _mm(sel, p2) + _mm(sel, p3)


def _mm_sel_r(x, sel):
    p1, p2, p3 = _split3(x)
    return _mm(p1, sel) + _mm(p2, sel) + _mm(p3, sel)


def _iota(shape, dim):
    return lax.broadcasted_iota(jnp.int32, shape, dim)


def _layer_spec(a, layer, **kw):
    nd = a.ndim - 1
    return pl.BlockSpec((None,) + a.shape[1:], lambda *_: (layer,) + (0,) * nd, **kw)


def _const_spec(a, **kw):
    return pl.BlockSpec(a.shape, lambda *_: (0,) * a.ndim, **kw)


def _bd(x, maskbd):
    xb = x.astype(BF16)
    return jnp.concatenate([xb] * HEADS, axis=0) * maskbd


def _chunk_masks(rev):
    row = _iota((CHUNK, MIX_W), 0)
    col = jnp.bitwise_and(_iota((CHUNK, MIX_W), 1), HEAD_DIM - 1)
    r2 = _iota((CHUNK, CHUNK), 0)
    c2 = _iota((CHUNK, CHUNK), 1)
    if rev:
        inc, strict, tri = row <= col, row < col, r2 <= c2
    else:
        inc, strict, tri = row >= col, row > col, r2 >= c2
    eye = jnp.where(row == col, 1.0, 0.0).astype(F32)
    return inc, strict, jnp.where(tri, 1.0, 0.0).astype(BF16), eye


def _split2(x):
    hi = x.astype(BF16)
    return hi, (x - hi.astype(F32)).astype(BF16)


def _mm_bd3(x, p, maskbd):
    n = x.shape[0]
    xh, xl = _split2(x)
    ph, pl_ = _split2(p)
    r = jnp.dot(jnp.concatenate([xh, xl], axis=0), _bd(ph, maskbd), preferred_element_type=F32)
    return r[:n] + r[n:] + jnp.dot(xh, _bd(pl_, maskbd), preferred_element_type=F32)


def _neumann_inverse(a_list, eye_list, maskbd):
    bs = [-a for a in a_list]
    ms = [eye + b for eye, b in zip(eye_list, bs)]
    ps = [_mm_bd3(b, b, maskbd) for b in bs]
    for _ in range(4):
        boths = [_mm_bd3(jnp.concatenate([m, p], axis=0), p, maskbd) for m, p in zip(ms, ps)]
        ms = [m + both[:CHUNK] for m, both in zip(ms, boths)]
        ps = [both[CHUNK:] for both in boths]
    return [m + _mm_bd3(m, p, maskbd) for m, p in zip(ms, ps)]


def _adaln_kernel(c_ref, w_ref, b_ref, o_ref):
    cv = c_ref[...]
    o_ref[0] = _mm(_silu(cv), w_ref[0]) + b_ref[0]


def _adaln(cvec8, ada_w, ada_b):
    tn = 768
    n_out = 6 * D_MODEL
    return pl.pallas_call(
        _adaln_kernel,
        grid=(DEPTH, n_out // tn),
        in_specs=[
            pl.BlockSpec((8, D_MODEL), lambda l, j: (0, 0)),
            pl.BlockSpec((1, D_MODEL, tn), lambda l, j: (l, 0, j)),
            pl.BlockSpec((1, 1, tn), lambda l, j: (l, 0, j)),
        ],
        out_specs=pl.BlockSpec((1, 8, tn), lambda l, j: (l, 0, j)),
        out_shape=jax.ShapeDtypeStruct((DEPTH, 8, n_out), F32),
        compiler_params=pltpu.CompilerParams(dimension_semantics=("arbitrary", "arbitrary"),
                                             vmem_limit_bytes=VMEM_LIMIT),
        name="adaln",
    )(cvec8, ada_w, ada_b.reshape(DEPTH, 1, n_out))


_KPE0 = Q_LORA + KV_LORA
_W_IN_MOVES = (
    [(0, 0, P_MLA)]
    + [(P_MLA + LANES - MLA_ROPE + 16 * j, _KPE0 + 16 * int(_ROPE_SWAP[16 * j] // 16), 16) for j in range(4)]
    + [(ZM_W, P_MLA, ZG_W), (ZM_W + ZG_W, P_MLA + ZG_W, 4 * HEADS), (ZM_W + ZG_W + LANES, P_MLA + P_GDN, P_RWKV)]
)
W_IN_PAD = ZM_W + ZG_W + LANES + ZR_W


def _inproj_kernel(x_ref, mod_ref, n1_ref, w_ref, zm_ref, zg_ref, zab_ref, zr_ref, w_s):
    @pl.when(pl.program_id(0) == 0)
    def _():
        w_s[...] = jnp.zeros(w_s.shape, BF16)
        for dst, src, width in _W_IN_MOVES:
            w_s[:, dst:dst + width] = w_ref[:, src:src + width].astype(BF16)

    m = mod_ref[0]
    sh = m[:, 0:D_MODEL]
    sc = m[:, D_MODEL:2 * D_MODEL]
    h = _rms(x_ref[...], n1_ref[...]) * (1.0 + sc) + sh
    z = _mm(h, w_s[...])
    o1 = ZM_W
    o2 = o1 + ZG_W
    o3 = o2 + LANES
    zm_ref[...] = z[:, :o1]
    zg_ref[...] = z[:, o1:o2]
    zab_ref[...] = z[:, o2:o3]
    zr_ref[...] = z[:, o3:]


def _inproj(x2d, mods, lw, layer, tm, seq_len, mod_base):
    n = x2d.shape[0]
    tiles_per_seq = seq_len // tm if mod_base else 1

    def mod_idx(i):
        return (layer, mod_base + i // tiles_per_seq if mod_base else 0, 0, 0)

    return pl.pallas_call(
        _inproj_kernel,
        grid=(n // tm,),
        in_specs=[
            pl.BlockSpec((tm, D_MODEL), lambda i: (i, 0)),
            pl.BlockSpec((None, 1, 1, 6 * D_MODEL), mod_idx),
            _layer_spec(lw["norm1"], layer),
            _layer_spec(lw["w_in"], layer, pipeline_mode=pl.Buffered(1)),
        ],
        out_specs=[
            pl.BlockSpec((tm, ZM_W), lambda i: (i, 0)),
            pl.BlockSpec((tm, ZG_W), lambda i: (i, 0)),
            pl.BlockSpec((tm, LANES), lambda i: (i, 0)),
            pl.BlockSpec((tm, ZR_W), lambda i: (i, 0)),
        ],
        out_shape=[
            jax.ShapeDtypeStruct((n, ZM_W), F32),
            jax.ShapeDtypeStruct((n, ZG_W), F32),
            jax.ShapeDtypeStruct((n, LANES), F32),
            jax.ShapeDtypeStruct((n, ZR_W), F32),
        ],
        scratch_shapes=[pltpu.VMEM((D_MODEL, W_IN_PAD), BF16)],
        compiler_params=pltpu.CompilerParams(dimension_semantics=("arbitrary",),
                                             vmem_limit_bytes=VMEM_LIMIT),
        name="inproj",
    )(x2d, mods, lw["norm1"], lw["w_in"])


MLA_SEQ_PER_STEP = 4


def _mla_ctx_kernel(zm_ref, qn_ref, wuq_ref, kvn_ref, wukv_ref, prev_ckv, prev_kpe, o_ref, ckv_ref, kpe_ref,
                    *, seq_len, n_seq):
    del prev_ckv, prev_kpe
    o_kpe = Q_LORA + KV_LORA
    zm = zm_ref[...]
    ckv = _rms(zm[:, Q_LORA:o_kpe], kvn_ref[...])
    kpe = zm[:, o_kpe:o_kpe + LANES]
    kv = _mm(ckv, wukv_ref[...])
    q = _mm(_rms(zm[:, :Q_LORA], qn_ref[...]), wuq_ref[...])
    kpe_b = kpe.astype(BF16)
    seqs = [slice(i * seq_len, (i + 1) * seq_len) for i in range(n_seq)]
    for i, r in enumerate(seqs):
        ckv_ref[i] = ckv[r]
        kpe_ref[i] = kpe[r, :MLA_ROPE]
    pairs = [(r, h * QH_W) for r in seqs for h in range(MLA_HEADS)]
    scores = [_mm_nt(q[r, c0:c0 + QH_W], jnp.concatenate([kv[r, c0:c0 + LANES].astype(BF16), kpe_b[r]], axis=1))
              * MLA_SCALE for r, c0 in pairs]
    exps = [jnp.exp(s - jnp.max(s, axis=-1, keepdims=True)) for s in scores]
    outs = [_mm(e, kv[r, c0 + LANES:c0 + QH_W]) / jnp.sum(e, axis=-1, keepdims=True)
            for e, (r, c0) in zip(exps, pairs)]
    for o, (r, c0) in zip(outs, pairs):
        h = c0 // QH_W
        o_ref[r, h * MLA_V:(h + 1) * MLA_V] = o


def _mla_ctx(zm, lw, layer, seq_len, prev):
    n = zm.shape[0]
    nb = n // seq_len
    n_seq = MLA_SEQ_PER_STEP
    rows = n_seq * seq_len
    weights = [lw["q_norm"], lw["w_uq"], lw["kv_norm"], lw["w_ukv"]]
    return pl.pallas_call(
        functools.partial(_mla_ctx_kernel, seq_len=seq_len, n_seq=n_seq),
        grid=(nb // n_seq,),
        in_specs=[pl.BlockSpec((rows, ZM_W), lambda b: (b, 0))] + [_layer_spec(a, layer) for a in weights]
        + [pl.BlockSpec(memory_space=pl.ANY)] * 2,
        out_specs=[
            pl.BlockSpec((rows, MLA_HEADS * MLA_V), lambda b: (b, 0)),
            pl.BlockSpec((n_seq, None, seq_len, KV_LORA), lambda b: (b, layer, 0, 0)),
            pl.BlockSpec((n_seq, None, seq_len, MLA_ROPE), lambda b: (b, layer, 0, 0)),
        ],
        out_shape=[
            jax.ShapeDtypeStruct((n, MLA_HEADS * MLA_V), F32),
            jax.ShapeDtypeStruct((nb, DEPTH, seq_len, KV_LORA), F32),
            jax.ShapeDtypeStruct((nb, DEPTH, seq_len, MLA_ROPE), F32),
        ],
        input_output_aliases={5: 1, 6: 2},
        compiler_params=pltpu.CompilerParams(dimension_semantics=("arbitrary",),
                                             vmem_limit_bytes=VMEM_LIMIT),
        name="mla_ctx",
    )(zm, *weights, *prev)


def _mla_lat_kernel(zm_ref, cckv_ref, ckpe_ref, rc_ref, rs_ref, qn_ref, wuq_ref, wuqs_ref, kvn_ref, wukv_ref,
                    o_ref, k_s, v_s, *, seq_len, tq, past):
    qi = pl.program_id(1)
    o_kpe = Q_LORA + KV_LORA

    @pl.when(qi == 0)
    def _():
        zm = zm_ref[...]
        ckv = _rms(zm[:, Q_LORA:o_kpe], kvn_ref[...])
        kpe = zm[:, o_kpe:o_kpe + LANES] * rc_ref[...] + zm[:, o_kpe + LANES:o_kpe + 2 * LANES] * rs_ref[...]
        kvc = _mm(cckv_ref[0], wukv_ref[...])
        kpc = ckpe_ref[0].astype(BF16)
        kv = _mm(ckv, wukv_ref[...])
        kpe = kpe.astype(BF16)
        for h in range(MLA_HEADS):
            c0 = h * QH_W
            k_s[0:past, c0:c0 + LANES] = kvc[:, c0:c0 + LANES].astype(BF16)
            k_s[0:past, c0 + LANES:c0 + QH_W] = kpc
            v_s[0:past, h * MLA_V:(h + 1) * MLA_V] = kvc[:, c0 + LANES:c0 + QH_W].astype(BF16)
            k_s[past:past + seq_len, c0:c0 + LANES] = kv[:, c0:c0 + LANES].astype(BF16)
            k_s[past:past + seq_len, c0 + LANES:c0 + QH_W] = kpe
            v_s[past:past + seq_len, h * MLA_V:(h + 1) * MLA_V] = kv[:, c0 + LANES:c0 + QH_W].astype(BF16)

    r0 = pl.multiple_of(qi * tq, tq)
    zq = zm_ref[pl.ds(r0, tq), :]
    cq = _rms(zq[:, :Q_LORA], qn_ref[...])
    q = _mm(cq, wuq_ref[...])
    qs = _mm(cq, wuqs_ref[...])
    qc = jnp.concatenate([jnp.ones((tq, LANES), F32), rc_ref[pl.ds(r0, tq), :]], axis=1)
    qsn = jnp.concatenate([jnp.zeros((tq, LANES), F32), rs_ref[pl.ds(r0, tq), :]], axis=1)
    for h in range(MLA_HEADS):
        c0 = h * QH_W
        qh = q[:, c0:c0 + QH_W] * qc + qs[:, c0:c0 + QH_W] * qsn
        s = _mm_nt(qh, k_s[:, c0:c0 + QH_W]) * MLA_SCALE
        e = jnp.exp(s - jnp.max(s, axis=-1, keepdims=True))
        den = jnp.sum(e, axis=-1, keepdims=True)
        o_ref[:, h * MLA_V:(h + 1) * MLA_V] = _mm(e, v_s[:, h * MLA_V:(h + 1) * MLA_V]) / den


def _mla_lat(zm, lw, layer, seq_len, tq, cache):
    n = zm.shape[0]
    nb = n // seq_len
    past = PAST_LEN
    tk = past + seq_len
    cckv, ckpe, rc, rs = cache
    weights = [lw["q_norm"], lw["w_uq"], lw["w_uq_sw"], lw["kv_norm"], lw["w_ukv"]]
    return pl.pallas_call(
        functools.partial(_mla_lat_kernel, seq_len=seq_len, tq=tq, past=past),
        grid=(nb, seq_len // tq),
        in_specs=[pl.BlockSpec((seq_len, ZM_W), lambda b, q: (b, 0)),
                  pl.BlockSpec((1, None, past, KV_LORA), lambda b, q: (b, layer, 0, 0)),
                  pl.BlockSpec((1, None, past, LANES), lambda b, q: (b, layer, 0, 0)),
                  _const_spec(rc), _const_spec(rs)] + [_layer_spec(a, layer) for a in weights],
        out_specs=pl.BlockSpec((tq, MLA_HEADS * MLA_V), lambda b, q: (b * (seq_len // tq) + q, 0)),
        out_shape=jax.ShapeDtypeStruct((n, MLA_HEADS * MLA_V), F32),
        scratch_shapes=[
            pltpu.VMEM((tk, MLA_HEADS * QH_W), BF16),
            pltpu.VMEM((tk, MLA_HEADS * MLA_V), BF16),
        ],
        compiler_params=pltpu.CompilerParams(dimension_semantics=("arbitrary", "arbitrary"),
                                             vmem_limit_bytes=VMEM_LIMIT),
        name="mla_lat",
    )(zm, cckv, ckpe, rc, rs, *weights)


SEQ_PER_STEP = 4
CHUNK_GROUP = 4
GDN_PRE_W = 5 * MIX_W


def _for_chunk_groups(n_chunks, fn):
    if n_chunks == CHUNK_GROUP:
        fn(0)
    else:
        def body(gi, carry):
            fn(gi * CHUNK_GROUP)
            return carry
        lax.fori_loop(0, n_chunks // CHUNK_GROUP, body, 0)


def _gdn_prepare(items, maskbd):
    n = range(len(items))
    qs, ks, vs, gs, betas, masks, revs = zip(*items)
    gcs = [_mm_sel_l(masks[i][2], gs[i]) for i in n]
    decays = []
    for i in n:
        inc, eye = masks[i][0], masks[i][3]
        gc_row = jnp.sum(eye * gcs[i], axis=0, keepdims=True)
        decays.append(jnp.where(inc, jnp.exp(jnp.where(inc, gcs[i] - gc_row, 0.0)), 0.0))
    kbs = [ks[i] * betas[i] for i in n]
    aqs = [_mm_nt(jnp.concatenate([kbs[i], qs[i]], axis=0), _bd(ks[i], maskbd)) for i in n]
    a_mats = [jnp.where(masks[i][1], aqs[i][:CHUNK] * decays[i], 0.0) for i in n]
    t_invs = _neumann_inverse(a_mats, [m[3] for m in masks], maskbd)
    egcs = [jnp.exp(gc) for gc in gcs]
    uws = [_mm(t_invs[i], jnp.concatenate([_bd(vs[i] * betas[i], maskbd), _bd(kbs[i] * egcs[i], maskbd)], axis=1))
           for i in n]
    out = []
    for i in n:
        g_last = gcs[i][0:1] if revs[i] else gcs[i][CHUNK - 1:CHUNK]
        pre = jnp.concatenate([uws[i], qs[i] * egcs[i], aqs[i][CHUNK:] * decays[i],
                               ks[i] * jnp.exp(g_last - gcs[i])], axis=1)
        out.append((pre, jnp.broadcast_to(jnp.exp(g_last), (8, MIX_W))))
    return out


def _gdn_step(items, maskbd):
    w = MIX_W
    n = range(len(items))
    pres, egls, states = zip(*items)
    wqs = [_mm(jnp.concatenate([pres[i][:, w:2 * w], pres[i][:, 2 * w:3 * w]], axis=0), states[i]) for i in n]
    v_news = [pres[i][:, :w] - wqs[i][:CHUNK] for i in n]
    outs = [wqs[i][CHUNK:] + _mm(pres[i][:, 3 * w:4 * w], _bd(v_news[i], maskbd)) for i in n]
    upds = [_mm_tn(pres[i][:, 4 * w:], v_news[i]) for i in n]
    mask_f = maskbd.astype(F32)
    return [(outs[i], states[i] * egls[i] + upds[i] * mask_f) for i in n]


def _gdn_kernel(*refs, seq_len, n_seq, cached):
    if cached:
        (zg_ref, zab_ref, s0_ref, conv_ref, alog_ref, dtb_ref, gn_ref, eexp_ref, ones_ref, maskbd_ref,
         o_ref, sout_ref, q_s, k_s, v_s, ge_s, pre_s, gl_s, oacc_s, st_s) = refs
    else:
        zg_ref, zab_ref, conv_ref, alog_ref, dtb_ref, gn_ref, eexp_ref, ones_ref, maskbd_ref = refs[:9]
        o_ref, sout_ref, q_s, k_s, v_s, ge_s, pre_s, gl_s, oacc_s, st_s = refs[-10:]
    t = seq_len * n_seq
    z = zg_ref[:, :GDN_CONV_CH]
    rowi = jnp.bitwise_and(_iota((t, 1), 0), seq_len - 1)
    zp = jnp.where(rowi == 0, 0.0, pltpu.roll(z, 1, 0))
    zn = jnp.where(rowi == seq_len - 1, 0.0, pltpu.roll(z, t - 1, 0))
    cw = conv_ref[...]
    qkv = _silu(zp * cw[0:1] + z * cw[1:2] + zn * cw[2:3])
    ones_bd = ones_ref[...]
    q = qkv[:, :MIX_W]
    k = qkv[:, MIX_W:2 * MIX_W]
    q_s[...] = q * lax.rsqrt(_mm_sel_r(q * q, ones_bd) + 1e-6) * (HEAD_DIM ** -0.5)
    k_s[...] = k * lax.rsqrt(_mm_sel_r(k * k, ones_bd) + 1e-6)
    v_s[...] = qkv[:, 2 * MIX_W:]
    ab = zab_ref[...]
    lane = _iota((t, LANES), 1)
    gb = jnp.where(lane < 2 * HEADS, -jnp.exp(alog_ref[...]) * _softplus(ab + dtb_ref[...]), _sigmoid(ab))
    ge_s[...] = _mm_sel_r(gb, eexp_ref[...])
    oacc_s[...] = jnp.zeros((t, MIX_W), F32)
    if cached:
        st_s[...] = s0_ref[...]
    else:
        st_s[...] = jnp.zeros((n_seq, 2, MIX_W, MIX_W), F32)
    maskbd = maskbd_ref[...]
    masks = (_chunk_masks(False), _chunk_masks(True))
    n_chunks = seq_len // CHUNK

    def prepare_group(c0):
        where, items = [], []
        for j in range(CHUNK_GROUP):
            c = c0 + j
            rows = pl.ds(pl.multiple_of(c * CHUNK, CHUNK), CHUNK)
            for d in range(2):
                where.append((d, c, rows))
                items.append((q_s[rows, :], k_s[rows, :], v_s[rows, :], ge_s[rows, d * MIX_W:(d + 1) * MIX_W],
                              ge_s[rows, (2 + d) * MIX_W:(3 + d) * MIX_W], masks[d], d == 1))
        for (d, c, rows), (pre, egl) in zip(where, _gdn_prepare(items, maskbd)):
            pre_s[d, rows, :] = pre
            gl_s[d, pl.ds(pl.multiple_of(c * 8, 8), 8), :] = egl

    _for_chunk_groups(n_chunks * n_seq, prepare_group)

    def body(i, carry):
        where, items = [], []
        for q in range(n_seq):
            for d in range(2):
                c = q * n_chunks + (i if d == 0 else n_chunks - 1 - i)
                rows = pl.ds(pl.multiple_of(c * CHUNK, CHUNK), CHUNK)
                where.append((q, d, rows))
                items.append((pre_s[d, rows, :], gl_s[d, pl.ds(pl.multiple_of(c * 8, 8), 8), :][0:1], st_s[q, d]))
        for (q, d, rows), (o, s_new) in zip(where, _gdn_step(items, maskbd)):
            oacc_s[rows, :] = oacc_s[rows, :] + o
            st_s[q, d] = s_new
        return carry

    lax.fori_loop(0, n_chunks, body, 0)
    o = oacc_s[...]
    ms = _mm_sel_r(o * o, ones_bd) * (1.0 / HEAD_DIM)
    gate = zg_ref[:, GDN_CONV_CH:]
    o_ref[...] = o * lax.rsqrt(ms + NORM_EPS) * gn_ref[...] * _silu(gate)
    for q in range(n_seq):
        for d in range(2):
            for h in range(HEADS):
                sl = slice(h * HEAD_DIM, (h + 1) * HEAD_DIM)
                sout_ref[q, d, h] = st_s[q, d, sl, sl]


def _state_out(nb, n_seq, layer, cached, prev, args, in_specs):
    if cached:
        return (pl.BlockSpec((n_seq, 2, HEADS, HEAD_DIM, HEAD_DIM), lambda b: (b, 0, 0, 0, 0)),
                jax.ShapeDtypeStruct((nb, 2, HEADS, HEAD_DIM, HEAD_DIM), F32), {})
    aliases = {len(args): 1}
    args.append(prev)
    in_specs.append(pl.BlockSpec(memory_space=pl.ANY))
    return (pl.BlockSpec((n_seq, None, 2, HEADS, HEAD_DIM, HEAD_DIM), lambda b: (b, layer, 0, 0, 0, 0)),
            jax.ShapeDtypeStruct((nb, DEPTH, 2, HEADS, HEAD_DIM, HEAD_DIM), F32), aliases)


def _gdn(zg, zab, lw, layer, consts, seq_len, s0_bd=None, prev=None):
    n = zg.shape[0]
    nb = n // seq_len
    cached = s0_bd is not None
    n_seq = 1 if cached else SEQ_PER_STEP
    rows = seq_len * n_seq
    args = [zg, zab]
    in_specs = [pl.BlockSpec((rows, ZG_W), lambda b: (b, 0)),
                pl.BlockSpec((rows, LANES), lambda b: (b, 0))]
    if cached:
        args.append(s0_bd)
        in_specs.append(pl.BlockSpec((1, None, 2, MIX_W, MIX_W), lambda b: (b, layer, 0, 0, 0)))
    layered = [lw["gdn_conv"], lw["gdn_alog"], lw["gdn_dtb"], lw["gdn_norm"]]
    const = [consts["eexp"], consts["ones_bd"], consts["maskbd"]]
    args += layered + const
    in_specs += [_layer_spec(a, layer) for a in layered] + [_const_spec(a) for a in const]
    s_spec, s_shape, aliases = _state_out(nb, n_seq, layer, cached, prev, args, in_specs)
    return pl.pallas_call(
        functools.partial(_gdn_kernel, seq_len=seq_len, n_seq=n_seq, cached=cached),
        grid=(nb // n_seq,),
        in_specs=in_specs,
        out_specs=[
            pl.BlockSpec((rows, MIX_W), lambda b: (b, 0)),
            s_spec,
        ],
        out_shape=[jax.ShapeDtypeStruct((n, MIX_W), F32), s_shape],
        input_output_aliases=aliases,
        scratch_shapes=[
            pltpu.VMEM((rows, MIX_W), F32),
            pltpu.VMEM((rows, MIX_W), F32),
            pltpu.VMEM((rows, MIX_W), F32),
            pltpu.VMEM((rows, 4 * MIX_W), F32),
            pltpu.VMEM((2, rows, GDN_PRE_W), F32),
            pltpu.VMEM((2, rows // CHUNK * 8, MIX_W), F32),
            pltpu.VMEM((rows, MIX_W), F32),
            pltpu.VMEM((n_seq, 2, MIX_W, MIX_W), F32),
        ],
        compiler_params=pltpu.CompilerParams(dimension_semantics=("arbitrary",),
                                             vmem_limit_bytes=VMEM_LIMIT),
        name="gdn_lat" if cached else "gdn_ctx",
    )(*args)


RWKV_PRE_W = 7 * MIX_W


def _rwkv_prepare(items, maskbd, eye_full):
    n = range(len(items))
    rs, kds, vs, kks, bs, lws, masks, revs = zip(*items)
    cums = [_mm_sel_l(masks[i][2], lws[i]) for i in n]
    einvs = [jnp.exp(-c) for c in cums]
    kts = [kks[i] * jnp.exp(cums[i] - lws[i]) for i in n]
    rts = [rs[i] * jnp.exp(cums[i]) for i in n]
    krs = [jnp.concatenate([kts[i], rts[i]], axis=0) for i in n]
    lb_alls = [_mm_nt(krs[i], _bd(bs[i] * einvs[i], maskbd)) for i in n]
    lk_alls = [_mm_nt(krs[i], _bd(kds[i] * einvs[i], maskbd)) for i in n]
    lbs = [jnp.where(masks[i][1], lb_alls[i][:CHUNK], 0.0) for i in n]
    t_invs = _neumann_inverse(lbs, [m[3] for m in masks], maskbd)
    lvs = [_mm(jnp.concatenate([jnp.where(masks[i][1], lk_alls[i][:CHUNK], 0.0),
                                jnp.where(masks[i][0], lk_alls[i][CHUNK:], 0.0)], axis=0), _bd(vs[i], maskbd))
           for i in n]
    tkps = [_mm(t_invs[i], jnp.concatenate([_bd(kts[i], maskbd), _bd(lvs[i][:CHUNK], maskbd)], axis=1)) for i in n]
    out = []
    for i in n:
        c_last = cums[i][0:1] if revs[i] else cums[i][CHUNK - 1:CHUNK]
        tail = jnp.exp(c_last - cums[i])
        rb = jnp.where(masks[i][0], lb_alls[i][CHUNK:], 0.0)
        pre = jnp.concatenate([tkps[i][:, :MIX_W], rts[i], tkps[i][:, MIX_W:], lvs[i][CHUNK:], rb,
                               kds[i] * tail, bs[i] * tail], axis=1)
        gcol = jnp.sum(eye_full * jnp.exp(c_last), axis=1, keepdims=True)
        out.append((pre, jnp.broadcast_to(gcol, (MIX_W, LANES))))
    return out


def _rwkv_step(items, maskbd):
    w = MIX_W
    n = range(len(items))
    pres, vs, gcols, states = zip(*items)
    prs = [_mm(jnp.concatenate([pres[i][:, :w], pres[i][:, w:2 * w]], axis=0), states[i]) for i in n]
    ps = [prs[i][:CHUNK] + pres[i][:, 2 * w:3 * w] for i in n]
    outs = [prs[i][CHUNK:] + pres[i][:, 3 * w:4 * w] - _mm(pres[i][:, 4 * w:5 * w], _bd(ps[i], maskbd)) for i in n]
    upds = [_mm_tn(jnp.concatenate([pres[i][:, 5 * w:6 * w], pres[i][:, 6 * w:]], axis=0),
                   jnp.concatenate([vs[i], -ps[i]], axis=0)) for i in n]
    mask_f = maskbd.astype(F32)
    return [(outs[i], states[i] * jnp.concatenate([gcols[i], gcols[i]], axis=1) + upds[i] * mask_f) for i in n]


def _rwkv_kernel(*refs, seq_len, n_seq, cached):
    if cached:
        (zr_ref, s0_ref, mup_ref, mun_ref, w0_ref, w2_ref, a0_ref, a2_ref, g2_ref, kk_ref, ka_ref, rk_ref,
         gnw_ref, gnb_ref, ones_ref, maskbd_ref, o_ref, sout_ref,
         r_s, v_s, kk_s, dir_s, bg_s, pre_s, gcol_s, oacc_s, st_s) = refs
    else:
        (zr_ref, mup_ref, mun_ref, w0_ref, w2_ref, a0_ref, a2_ref, g2_ref, kk_ref, ka_ref, rk_ref,
         gnw_ref, gnb_ref, ones_ref, maskbd_ref) = refs[:15]
        o_ref, sout_ref, r_s, v_s, kk_s, dir_s, bg_s, pre_s, gcol_s, oacc_s, st_s = refs[-11:]
    t = seq_len * n_seq
    z = zr_ref[...]
    rowi = jnp.bitwise_and(_iota((t, 1), 0), seq_len - 1)
    zp = jnp.where(rowi == 0, 0.0, pltpu.roll(z, 1, 0))
    zn = jnp.where(rowi == seq_len - 1, 0.0, pltpu.roll(z, t - 1, 0))
    z = z + mup_ref[...] * (zp - z) + mun_ref[...] * (zn - z)
    w = MIX_W
    r = z[:, :w]
    k = z[:, w:2 * w]
    v = z[:, 2 * w:3 * w]
    wd = jnp.tanh(z[:, 3 * w:3 * w + LANES])
    ad = z[:, 3 * w + LANES:3 * w + 2 * LANES]
    gd = _sigmoid(z[:, 3 * w + 2 * LANES:])
    ones_bd = ones_ref[...]
    kk = k * kk_ref[...]
    kk = kk * lax.rsqrt(_mm_sel_r(kk * kk, ones_bd) + 1e-6)
    r_s[...] = r
    v_s[...] = v
    kk_s[...] = kk
    bonus = jnp.zeros((t, w), F32)
    for d in range(2):
        w_log = -_softplus(-(w0_ref[d:d + 1] + _mm(wd, w2_ref[d]))) - 0.5
        a = _sigmoid(a0_ref[d:d + 1] + _mm(ad, a2_ref[d]))
        kd = k * (1.0 + (a - 1.0) * ka_ref[...])
        dir_s[:, (3 * d) * w:(3 * d + 1) * w] = -jnp.exp(w_log)
        dir_s[:, (3 * d + 1) * w:(3 * d + 2) * w] = kd
        dir_s[:, (3 * d + 2) * w:(3 * d + 3) * w] = kk * a
        bonus = bonus + _mm_sel_r(r * kd * rk_ref[...], ones_bd) * v
    bg_s[:, :w] = bonus
    bg_s[:, w:] = _mm(gd, g2_ref[...])
    oacc_s[...] = jnp.zeros((t, w), F32)
    if cached:
        st_s[...] = s0_ref[...]
    else:
        st_s[...] = jnp.zeros((n_seq, 2, w, w), F32)
    maskbd = maskbd_ref[...]
    masks = (_chunk_masks(False), _chunk_masks(True))
    eye_full = jnp.where(_iota((w, w), 0) == _iota((w, w), 1), 1.0, 0.0).astype(F32)
    n_chunks = seq_len // CHUNK

    def prepare_group(c0):
        where, items = [], []
        for j in range(CHUNK_GROUP):
            c = c0 + j
            rows = pl.ds(pl.multiple_of(c * CHUNK, CHUNK), CHUNK)
            for d in range(2):
                where.append((d, c, rows))
                items.append((r_s[rows, :], dir_s[rows, (3 * d + 1) * w:(3 * d + 2) * w], v_s[rows, :], kk_s[rows, :],
                              dir_s[rows, (3 * d + 2) * w:(3 * d + 3) * w], dir_s[rows, (3 * d) * w:(3 * d + 1) * w],
                              masks[d], d == 1))
        for (d, c, rows), (pre, gcol) in zip(where, _rwkv_prepare(items, maskbd, eye_full)):
            pre_s[d, rows, :] = pre
            gcol_s[d, pl.ds(pl.multiple_of(c * w, w), w), :] = gcol

    _for_chunk_groups(n_chunks * n_seq, prepare_group)

    def body(i, carry):
        where, items = [], []
        for q in range(n_seq):
            for d in range(2):
                c = q * n_chunks + (i if d == 0 else n_chunks - 1 - i)
                rows = pl.ds(pl.multiple_of(c * CHUNK, CHUNK), CHUNK)
                where.append((q, d, rows))
                items.append((pre_s[d, rows, :], v_s[rows, :], gcol_s[d, pl.ds(pl.multiple_of(c * w, w), w), :],
                              st_s[q, d]))
        for (q, d, rows), (o, z_new) in zip(where, _rwkv_step(items, maskbd)):
            oacc_s[rows, :] = oacc_s[rows, :] + o
            st_s[q, d] = z_new
        return carry

    lax.fori_loop(0, n_chunks, body, 0)
    o = oacc_s[...]
    inv_n = 1.0 / HEAD_DIM
    mu = _mm_sel_r(o, ones_bd) * inv_n
    oc = o - mu
    var = _mm_sel_r(oc * oc, ones_bd) * inv_n
    y = oc * lax.rsqrt(var + RWKV_GN_EPS) * gnw_ref[...] + gnb_ref[...]
    o_ref[...] = (y + bg_s[:, :w]) * bg_s[:, w:]
    for q in range(n_seq):
        for d in range(2):
            state = st_s[q, d] if cached else st_s[q, d].T
            for h in range(HEADS):
                sl = slice(h * HEAD_DIM, (h + 1) * HEAD_DIM)
                sout_ref[q, d, h] = state[sl, sl]


def _rwkv(zr, lw, layer, consts, seq_len, s0_bd=None, prev=None):
    n = zr.shape[0]
    nb = n // seq_len
    cached = s0_bd is not None
    n_seq = 1 if cached else SEQ_PER_STEP
    rows = seq_len * n_seq
    args = [zr]
    in_specs = [pl.BlockSpec((rows, ZR_W), lambda b: (b, 0))]
    if cached:
        args.append(s0_bd)
        in_specs.append(pl.BlockSpec((1, None, 2, MIX_W, MIX_W), lambda b: (b, layer, 0, 0, 0)))
    layered = [lw["rwkv_mu_prev"], lw["rwkv_mu_next"], lw["rwkv_w0"], lw["rwkv_w2"], lw["rwkv_a0"], lw["rwkv_a2"],
               lw["rwkv_g2"], lw["rwkv_k_k"], lw["rwkv_k_a"], lw["rwkv_r_k"], lw["rwkv_gn_w"], lw["rwkv_gn_b"]]
    const = [consts["ones_bd"], consts["maskbd"]]
    args += layered + const
    in_specs += [_layer_spec(a, layer) for a in layered] + [_const_spec(a) for a in const]
    s_spec, s_shape, aliases = _state_out(nb, n_seq, layer, cached, prev, args, in_specs)
    return pl.pallas_call(
        functools.partial(_rwkv_kernel, seq_len=seq_len, n_seq=n_seq, cached=cached),
        grid=(nb // n_seq,),
        in_specs=in_specs,
        out_specs=[
            pl.BlockSpec((rows, MIX_W), lambda b: (b, 0)),
            s_spec,
        ],
        out_shape=[jax.ShapeDtypeStruct((n, MIX_W), F32), s_shape],
        input_output_aliases=aliases,
        scratch_shapes=[
            pltpu.VMEM((rows, MIX_W), F32),
            pltpu.VMEM((rows, MIX_W), F32),
            pltpu.VMEM((rows, MIX_W), F32),
            pltpu.VMEM((rows, 6 * MIX_W), F32),
            pltpu.VMEM((rows, 2 * MIX_W), F32),
            pltpu.VMEM((2, rows, RWKV_PRE_W), F32),
            pltpu.VMEM((2, rows // CHUNK * MIX_W, LANES), F32),
            pltpu.VMEM((rows, MIX_W), F32),
            pltpu.VMEM((n_seq, 2, MIX_W, MIX_W), F32),
        ],
        compiler_params=pltpu.CompilerParams(dimension_semantics=("arbitrary",),
                                             vmem_limit_bytes=VMEM_LIMIT),
        name="rwkv_lat" if cached else "rwkv_ctx",
    )(*args)


def _route(logits_t, bias):
    tm = logits_t.shape[1]
    neg = -jnp.inf
    sc = _sigmoid(logits_t)
    sc3 = sc.reshape(N_GROUPS, GROUP_SIZE, tm)
    sel = (sc + bias).reshape(N_GROUPS, GROUP_SIZE, tm)
    si = _iota(sel.shape, 1).astype(F32)
    m1 = jnp.max(sel, axis=1, keepdims=True)
    f1 = jnp.min(jnp.where(sel == m1, si, float(GROUP_SIZE)), axis=1, keepdims=True)
    m2 = jnp.max(jnp.where(si == f1, neg, sel), axis=1, keepdims=True)
    grp = m1 + m2
    gi = _iota(grp.shape, 0).astype(F32)
    gsel = jnp.zeros(grp.shape, F32)
    for _ in range(TOPK_GROUPS):
        mx = jnp.max(grp, axis=0, keepdims=True)
        fi = jnp.min(jnp.where(grp == mx, gi, float(N_GROUPS)), axis=0, keepdims=True)
        hit = gi == fi
        gsel = jnp.where(hit, 1.0, gsel)
        grp = jnp.where(hit, neg, grp)
    cur = jnp.where(gsel > 0.0, sel, neg)
    ei = (_iota(cur.shape, 0) * GROUP_SIZE + _iota(cur.shape, 1)).astype(F32)
    chosen = jnp.zeros(cur.shape, F32)
    ids, wts = [], []
    for _ in range(TOP_K):
        mx = jnp.max(jnp.max(cur, axis=0, keepdims=True), axis=1, keepdims=True)
        fi = jnp.min(jnp.min(jnp.where(cur == mx, ei, float(N_EXPERTS)), axis=0, keepdims=True),
                     axis=1, keepdims=True)
        hit = ei == fi
        chosen = jnp.where(hit, 1.0, chosen)
        cur = jnp.where(hit, neg, cur)
        ids.append(fi.reshape(1, tm))
        wts.append(jnp.sum(jnp.sum(jnp.where(hit, sc3, 0.0), axis=0, keepdims=True), axis=1, keepdims=True)
                   .reshape(1, tm))
    w = jnp.concatenate(wts, axis=0)
    w = w / jnp.sum(w, axis=0, keepdims=True) * ROUTE_SCALE
    return chosen.reshape(N_EXPERTS, tm), jnp.concatenate(ids, axis=0), w


def _pack_halves(x):
    half = x.shape[1] // 2
    bits = lax.bitcast_convert_type(x.astype(BF16).astype(F32), jnp.int32)
    lo = lax.shift_right_logical(bits[:, :half], jnp.int32(16))
    return jnp.bitwise_or(lo, jnp.bitwise_and(bits[:, half:], jnp.int32(-65536)))


def _unpack_halves(word):
    lo = lax.bitcast_convert_type(lax.shift_left(word, jnp.int32(16)), F32)
    hi = lax.bitcast_convert_type(jnp.bitwise_and(word, jnp.int32(-65536)), F32)
    return lo, hi


def _post_kernel(x_ref, om_ref, og_ref, or_ref, mod_ref, wo_ref, n2_ref, rt_ref, rb_ref, sgu_ref, sdn_ref,
                 tri_ref, cin_ref, x1_ref, h2_ref, eid_ref, rank_ref, ew_ref, cnt_ref, carry_s):
    @pl.when(pl.program_id(0) == 0)
    def _():
        carry_s[...] = cin_ref[...]

    m = mod_ref[0]
    g1 = m[:, 2 * D_MODEL:3 * D_MODEL]
    sh2 = m[:, 3 * D_MODEL:4 * D_MODEL]
    sc2 = m[:, 4 * D_MODEL:5 * D_MODEL]
    w_mla = MLA_HEADS * MLA_V
    mix = (_mm(om_ref[...], wo_ref[0:w_mla, :]) + _mm(og_ref[...], wo_ref[w_mla:w_mla + MIX_W, :])
           + _mm(or_ref[...], wo_ref[w_mla + MIX_W:, :]))
    x1 = x_ref[...] + g1 * mix
    h2 = _rms(x1, n2_ref[...]) * (1.0 + sc2) + sh2
    h2_ref[...] = _pack_halves(h2)
    g2 = m[:, 5 * D_MODEL:]
    x1_ref[...] = x1 + g2 * _mm(_swiglu_act(_mm(h2, sgu_ref[...])), sdn_ref[...])
    r_hi, r_lo = _split2(rt_ref[...])
    h_hi, h_lo = _split2(h2)
    logits_t = _mm_nt(r_hi, h_hi) + _mm_nt(r_hi, h_lo) + _mm_nt(r_lo, h_hi)
    chosen, ids, w = _route(logits_t, rb_ref[...])
    tm = chosen.shape[1]
    rank_et = (carry_s[:, 0:1] + _mm(chosen, tri_ref[...])).reshape(N_GROUPS, GROUP_SIZE, tm)
    ei = (_iota(rank_et.shape, 0) * GROUP_SIZE + _iota(rank_et.shape, 1)).astype(F32)
    ranks = []
    for k in range(TOP_K):
        pick = jnp.where(ei == ids[k:k + 1].reshape(1, 1, tm), rank_et, 0.0)
        ranks.append(jnp.sum(jnp.sum(pick, axis=0, keepdims=True), axis=1, keepdims=True).reshape(1, tm))
    eid_ref[...] = ids.astype(jnp.int32)
    rank_ref[...] = jnp.concatenate(ranks, axis=0).astype(jnp.int32)
    ew_ref[...] = jnp.concatenate([w, jnp.zeros((LANES - TOP_K, tm), F32)], axis=0).T
    total = carry_s[...] + jnp.sum(chosen, axis=1, keepdims=True)
    carry_s[...] = total
    cnt_ref[...] = total


def _post(x2d, om, og, orw, mods, lw, layer, consts, tm, seq_len, mod_base, counts_in):
    n = x2d.shape[0]
    tiles_per_seq = seq_len // tm if mod_base else 1

    def mod_idx(i):
        return (layer, mod_base + i // tiles_per_seq if mod_base else 0, 0, 0)

    row = lambda w: pl.BlockSpec((tm, w), lambda i: (i, 0))
    col = lambda h: pl.BlockSpec((h, tm), lambda i: (0, i))
    full = lambda a: _layer_spec(a, layer)
    tail = [lw["w_out"], lw["norm2"], lw["router_t"], lw["router_b"], lw["shared_w_gu"], lw["shared_w_down"]]
    tri = consts["tri_tokens"]
    return pl.pallas_call(
        _post_kernel,
        grid=(n // tm,),
        in_specs=[row(D_MODEL), row(MLA_HEADS * MLA_V), row(MIX_W), row(MIX_W),
                  pl.BlockSpec((None, 1, 1, 6 * D_MODEL), mod_idx)] + [full(a) for a in tail]
        + [_const_spec(tri), _const_spec(counts_in)],
        out_specs=[row(D_MODEL), row(D_MODEL // 2), col(TOP_K), col(TOP_K), row(LANES),
                   pl.BlockSpec((N_EXPERTS, LANES), lambda i: (0, 0))],
        out_shape=[
            jax.ShapeDtypeStruct((n, D_MODEL), F32),
            jax.ShapeDtypeStruct((n, D_MODEL // 2), jnp.int32),
            jax.ShapeDtypeStruct((TOP_K, n), jnp.int32),
            jax.ShapeDtypeStruct((TOP_K, n), jnp.int32),
            jax.ShapeDtypeStruct((n, LANES), F32),
            jax.ShapeDtypeStruct((N_EXPERTS, LANES), F32),
        ],
        scratch_shapes=[pltpu.VMEM((N_EXPERTS, LANES), F32)],
        compiler_params=pltpu.CompilerParams(dimension_semantics=("arbitrary",),
                                             vmem_limit_bytes=VMEM_LIMIT),
        name="post",
    )(x2d, om, og, orw, mods, *tail, tri, counts_in)


MOE_ROWS = 512
SC_ROWS = 128
SC_SUBCORES = 32


def _swiglu_act(gu):
    return _silu(gu[:, :D_EXPERT]) * gu[:, D_EXPERT:]


def _dispatch_plan(eid, rank, counts, n, rows):
    n_blocks = n * TOP_K // rows + N_EXPERTS
    cnt = counts[:, 0].astype(jnp.int32)
    blocks = (cnt + rows - 1) // rows
    block_end = jnp.cumsum(blocks)
    offset = (block_end - blocks) * rows
    experts = jnp.arange(N_EXPERTS, dtype=jnp.int32)
    dest = jnp.sum(jnp.where(eid[..., None] == experts, offset, 0), axis=-1) + rank
    block_ids = jnp.arange(n_blocks, dtype=jnp.int32)
    block_expert = jnp.minimum(jnp.sum((block_end[None, :] <= block_ids[:, None]).astype(jnp.int32), axis=1),
                               N_EXPERTS - 1)
    return dest, block_expert, block_end[-1:].astype(jnp.int32), n_blocks


def _sc_mesh():
    return plsc.VectorSubcoreMesh(core_axis_name="core", subcore_axis_name="subcore")


def _sc_dispatch(groups, dest, n_rows):
    w = groups[0].shape[1]
    dtype = groups[0].dtype

    @functools.partial(pl.kernel, out_type=jax.ShapeDtypeStruct((n_rows, w), dtype), mesh=_sc_mesh(),
                       scratch_types=[pltpu.VMEM((SC_ROWS, w), dtype), pltpu.VMEM((TOP_K, SC_ROWS), jnp.int32)])
    def kern(*refs):
        x_refs, d_hbm, o_hbm, xv, dv = refs[:len(groups)], *refs[len(groups):]
        sid = lax.axis_index("core") * (SC_SUBCORES // 2) + lax.axis_index("subcore")
        start = 0
        for x_hbm, x in zip(x_refs, groups):
            def chunk(c, x_hbm=x_hbm, start=start):
                r0 = pl.multiple_of(c * SC_ROWS, SC_ROWS)
                pltpu.sync_copy(x_hbm.at[pl.ds(r0, SC_ROWS)], xv)
                pltpu.sync_copy(d_hbm.at[:, pl.ds(start + r0, SC_ROWS)], dv)
                for k in range(TOP_K):
                    pltpu.sync_copy(xv, o_hbm.at[dv.at[k]])

            pl.loop(sid, x.shape[0] // SC_ROWS, step=SC_SUBCORES)(chunk)
            start += x.shape[0]

    return kern(*groups, dest)


def _sc_gather(y, idx):
    w = y.shape[1]
    n_chunks = idx.shape[0]

    @functools.partial(pl.kernel, out_type=jax.ShapeDtypeStruct((n_chunks * SC_ROWS, w), y.dtype), mesh=_sc_mesh(),
                       scratch_types=[pltpu.VMEM((SC_ROWS, w), y.dtype), pltpu.VMEM((1, SC_ROWS), jnp.int32)])
    def kern(y_hbm, i_hbm, o_hbm, ov, iv):
        sid = lax.axis_index("core") * (SC_SUBCORES // 2) + lax.axis_index("subcore")

        @pl.loop(sid, n_chunks, step=SC_SUBCORES)
        def _(c):
            pltpu.sync_copy(i_hbm.at[pl.ds(c, 1)], iv)
            pltpu.sync_copy(y_hbm.at[iv.at[0]], ov)
            pltpu.sync_copy(ov, o_hbm.at[pl.ds(pl.multiple_of(c * SC_ROWS, SC_ROWS), SC_ROWS)])

    return kern(y, idx)


def _moe_rows_kernel(be_ref, nu_ref, x_ref, wgu_ref, wdn_ref, y_ref):
    @pl.when(pl.program_id(0) < nu_ref[0])
    def _():
        half = D_MODEL // 2
        lo, hi = _unpack_halves(x_ref[...])
        gu = _mm(lo, wgu_ref[0:half, :]) + _mm(hi, wgu_ref[half:, :])
        y_ref[...] = _pack_halves(_mm(_swiglu_act(gu), wdn_ref[...]))


def _moe_rows(xs, block_expert, n_used, lw, layer, n_blocks, rows):
    half = D_MODEL // 2
    last = lambda b, be, nu: jnp.minimum(b, nu[0] - 1)
    return pl.pallas_call(
        _moe_rows_kernel,
        grid_spec=pltpu.PrefetchScalarGridSpec(
            num_scalar_prefetch=2,
            grid=(n_blocks,),
            in_specs=[pl.BlockSpec((rows, half), lambda b, be, nu: (last(b, be, nu), 0)),
                      pl.BlockSpec((None, None, D_MODEL, 2 * D_EXPERT),
                                   lambda b, be, nu: (layer, be[last(b, be, nu)], 0, 0)),
                      pl.BlockSpec((None, None, D_EXPERT, D_MODEL),
                                   lambda b, be, nu: (layer, be[last(b, be, nu)], 0, 0))],
            out_specs=pl.BlockSpec((rows, half), lambda b, be, nu: (last(b, be, nu), 0)),
        ),
        out_shape=jax.ShapeDtypeStruct(xs.shape, jnp.int32),
        compiler_params=pltpu.CompilerParams(dimension_semantics=("arbitrary",),
                                             vmem_limit_bytes=VMEM_LIMIT),
        name="moe_rows",
    )(block_expert, n_used, xs, lw["moe_w_gu"], lw["moe_w_down"])


def _moe_combine_kernel(yg_ref, ew_ref, x1_ref, mod_ref, nf_ref, o_ref, *, final):
    ew = ew_ref[...]
    acc_lo = acc_hi = None
    for k in range(TOP_K):
        lo, hi = _unpack_halves(yg_ref[k])
        wk = ew[:, k:k + 1]
        acc_lo = wk * lo if acc_lo is None else acc_lo + wk * lo
        acc_hi = wk * hi if acc_hi is None else acc_hi + wk * hi
    g2 = mod_ref[0][:, 5 * D_MODEL:]
    x2 = x1_ref[...] + g2 * jnp.concatenate([acc_lo, acc_hi], axis=1)
    if final:
        x2 = _rms(x2, nf_ref[...])
    o_ref[...] = x2


def _moe_combine(yg, row0, ew, x1, mods, layer, norm_f, tm, seq_len, mod_base, final):
    n = x1.shape[0]
    half = D_MODEL // 2
    tiles_per_seq = seq_len // tm if mod_base else 1
    tile0 = row0 // tm

    def mod_idx(i):
        return (layer, mod_base + i // tiles_per_seq if mod_base else 0, 0, 0)

    row = lambda w: pl.BlockSpec((tm, w), lambda i: (i, 0))
    return pl.pallas_call(
        functools.partial(_moe_combine_kernel, final=final),
        grid=(n // tm,),
        in_specs=[pl.BlockSpec((TOP_K, tm, half), lambda i: (0, tile0 + i, 0)), row(LANES), row(D_MODEL),
                  pl.BlockSpec((None, 1, 1, 6 * D_MODEL), mod_idx), _const_spec(norm_f)],
        out_specs=row(D_MODEL),
        out_shape=jax.ShapeDtypeStruct((n, D_MODEL), F32),
        compiler_params=pltpu.CompilerParams(dimension_semantics=("arbitrary",),
                                             vmem_limit_bytes=VMEM_LIMIT),
        name="moe_combine_final" if final else "moe_combine",
    )(yg, ew, x1, mods, norm_f)


def _moe_experts(groups, eid, rank, counts, lw, layer):
    group_sizes = [g.shape[0] for g in groups]
    n = sum(group_sizes)
    dest, block_expert, n_used, n_blocks = _dispatch_plan(eid, rank, counts, n, MOE_ROWS)
    xs = _sc_dispatch(groups, dest, n_blocks * MOE_ROWS)
    y = _moe_rows(xs, block_expert, n_used, lw, layer, n_blocks, MOE_ROWS)
    outs, start = [], 0
    for size in group_sizes:
        idx = dest[:, start:start + size].reshape(size * TOP_K // SC_ROWS, SC_ROWS)
        outs.append(_sc_gather(y, idx).reshape(TOP_K, size, D_MODEL // 2))
        start += size
    return outs


def _constants():
    idx = np.arange(MIX_W)
    same_head = (idx[:, None] // HEAD_DIM) == (idx[None, :] // HEAD_DIM)
    maskbd = jnp.asarray(same_head, BF16)
    eexp = np.zeros((LANES, 4 * MIX_W), np.float32)
    for blk in range(4):
        kind, d = divmod(blk, 2)
        for h in range(HEADS):
            src = kind * 2 * HEADS + d * HEADS + h
            eexp[src, blk * MIX_W + h * HEAD_DIM: blk * MIX_W + (h + 1) * HEAD_DIM] = 1.0
    tri = np.triu(np.ones((POST_TM, POST_TM), np.float32), 1)
    return {"maskbd": maskbd, "ones_bd": maskbd, "eexp": jnp.asarray(eexp, BF16), "tri_tokens": jnp.asarray(tri, BF16)}


def _rope_tables(n):
    rows = n // GRID_W
    row = jnp.repeat(jnp.arange(rows, dtype=F32), GRID_W)
    col = jnp.tile(jnp.arange(GRID_W, dtype=F32), rows)
    axis_dim = MLA_ROPE // 2
    inv = jnp.power(ROPE_BASE, -jnp.arange(0, axis_dim, 2, dtype=F32) / axis_dim)
    ang_r = row[:, None] * inv
    ang_c = col[:, None] * inv
    cr, sr, cc, sc = jnp.cos(ang_r), jnp.sin(ang_r), jnp.cos(ang_c), jnp.sin(ang_c)
    zeros = jnp.zeros((n, LANES - MLA_ROPE), F32)
    cos_t = jnp.concatenate([cr, cr, cc, cc, zeros], axis=1)
    sin_t = jnp.concatenate([-sr, sr, -sc, sc, zeros], axis=1)
    return cos_t, sin_t


def _stacked_weights(p):
    w_uq = p["mla_w_uq"].reshape(DEPTH, Q_LORA, MLA_HEADS, MLA_NOPE + MLA_ROPE)
    zq = jnp.zeros((DEPTH, Q_LORA, MLA_HEADS, QH_W - MLA_NOPE - MLA_ROPE), F32)
    w_uq_a = jnp.concatenate([w_uq, zq], axis=-1).reshape(DEPTH, Q_LORA, MLA_HEADS * QH_W).astype(BF16)
    w_uq_sw = jnp.concatenate([jnp.zeros((DEPTH, Q_LORA, MLA_HEADS, MLA_NOPE), F32),
                               w_uq[..., MLA_NOPE + _ROPE_SWAP], zq], axis=-1)
    w_uq_sw = w_uq_sw.reshape(DEPTH, Q_LORA, MLA_HEADS * QH_W).astype(BF16)

    def per_direction(w):
        half = jnp.zeros((DEPTH, 64, MIX_W), F32)
        return jnp.stack([jnp.concatenate([w[:, 0], half], axis=1),
                          jnp.concatenate([half, w[:, 1]], axis=1)], axis=1).astype(BF16)

    row = lambda v: v.reshape(DEPTH, 1, -1)
    pad_row = lambda v: jnp.pad(row(v), ((0, 0), (0, 0), (0, LANES - 2 * HEADS)))
    return {
        "norm1": row(p["norm1"]),
        "w_in": p["w_in"],
        "q_norm": row(p["mla_q_norm"]),
        "w_uq": w_uq_a, "w_uq_sw": w_uq_sw,
        "kv_norm": row(p["mla_kv_norm"]),
        "w_ukv": p["mla_w_ukv"].astype(BF16),
        "gdn_conv": p["gdn_conv"],
        "gdn_alog": pad_row(p["gdn_a_log"]),
        "gdn_dtb": pad_row(p["gdn_dt_bias"]),
        "gdn_norm": jnp.tile(row(p["gdn_norm"]), (1, 1, HEADS)),
        "rwkv_mu_prev": row(p["rwkv_mu_prev"]),
        "rwkv_mu_next": row(p["rwkv_mu_next"]),
        "rwkv_w0": p["rwkv_w0"],
        "rwkv_w2": per_direction(p["rwkv_w2"]),
        "rwkv_a0": p["rwkv_a0"],
        "rwkv_a2": per_direction(p["rwkv_a2"]),
        "rwkv_g2": p["rwkv_g2"].astype(BF16),
        "rwkv_k_k": row(p["rwkv_k_k"]),
        "rwkv_k_a": row(p["rwkv_k_a"]),
        "rwkv_r_k": row(p["rwkv_r_k"]),
        "rwkv_gn_w": row(p["rwkv_gn_w"]),
        "rwkv_gn_b": row(p["rwkv_gn_b"]),
        "w_out": p["w_out"].astype(BF16),
        "norm2": row(p["norm2"]),
        "router_t": jnp.swapaxes(p["moe_router"], 1, 2),
        "router_b": p["moe_bias"].reshape(DEPTH, N_EXPERTS, 1),
        "moe_w_gu": p["moe_w_gu"],
        "moe_w_down": p["moe_w_down"],
        "shared_w_gu": p["shared_w_gu"].astype(BF16),
        "shared_w_down": p["shared_w_down"].astype(BF16),
    }


def _embed_block_diag(s):
    b = s.shape[0]
    eye = jnp.eye(HEADS, dtype=s.dtype)
    out = jnp.einsum("bdhkv,hg->bdhkgv", s, eye)
    return out.reshape(b, 2, MIX_W, MIX_W)


def _layer_front(x2d, mods, lw, l, consts, seq_len, mod_base, cache, tm, tq, counts_in, prev=None):
    zm, zg, zab, zr = _inproj(x2d, mods, lw, l, tm, seq_len, mod_base)
    if cache is None:
        o_mla, ckv, kpe = _mla_ctx(zm, lw, l, seq_len, prev[:2])
        o_gdn, s_gdn = _gdn(zg, zab, lw, l, consts, seq_len, prev=prev[2])
        o_rwkv, s_rwkv = _rwkv(zr, lw, l, consts, seq_len, prev=prev[3])
        new = (ckv, kpe, s_gdn, s_rwkv)
    else:
        cckv, ckpe, rc, rs, sg, sr = cache
        o_mla = _mla_lat(zm, lw, l, seq_len, tq, (cckv, ckpe, rc, rs))
        o_gdn, _ = _gdn(zg, zab, lw, l, consts, seq_len, sg)
        o_rwkv, _ = _rwkv(zr, lw, l, consts, seq_len, sr)
        new = None
    routed = _post(x2d, o_mla, o_gdn, o_rwkv, mods, lw, l, consts, tm, seq_len, mod_base, counts_in)
    return routed, new


def kernel(x_prompt, x_sample, cache_mla_ckv, cache_mla_kpe, state_gdn, state_rwkv, c, c_ctx, ada_w, ada_b, norm1, w_in, mla_q_norm, mla_w_uq, mla_kv_norm, mla_w_ukv, gdn_conv, gdn_a_log, gdn_dt_bias, gdn_norm, rwkv_mu_prev, rwkv_mu_next, rwkv_w0, rwkv_w2, rwkv_a0, rwkv_a2, rwkv_g2, rwkv_k_k, rwkv_k_a, rwkv_r_k, rwkv_gn_w, rwkv_gn_b, w_out, norm2, moe_router, moe_bias, moe_w_gu, moe_w_down, shared_w_gu, shared_w_down, norm_f):
    p = dict(norm1=norm1, w_in=w_in, mla_q_norm=mla_q_norm, mla_w_uq=mla_w_uq, mla_kv_norm=mla_kv_norm,
             mla_w_ukv=mla_w_ukv, gdn_conv=gdn_conv, gdn_a_log=gdn_a_log, gdn_dt_bias=gdn_dt_bias,
             gdn_norm=gdn_norm, rwkv_mu_prev=rwkv_mu_prev, rwkv_mu_next=rwkv_mu_next, rwkv_w0=rwkv_w0,
             rwkv_w2=rwkv_w2, rwkv_a0=rwkv_a0, rwkv_a2=rwkv_a2, rwkv_g2=rwkv_g2, rwkv_k_k=rwkv_k_k,
             rwkv_k_a=rwkv_k_a, rwkv_r_k=rwkv_r_k, rwkv_gn_w=rwkv_gn_w, rwkv_gn_b=rwkv_gn_b, w_out=w_out,
             norm2=norm2, moe_router=moe_router, moe_bias=moe_bias, moe_w_gu=moe_w_gu, moe_w_down=moe_w_down,
             shared_w_gu=shared_w_gu, shared_w_down=shared_w_down)
    weights = _stacked_weights(p)
    consts = _constants()
    nf = norm_f.reshape(1, D_MODEL)
    b_ctx, t_ctx, _ = x_prompt.shape
    b_lat, t_lat, _ = x_sample.shape

    cvec8 = jnp.concatenate([c_ctx[None, :], c, jnp.zeros((8 - 1 - b_lat, D_MODEL), F32)], axis=0)
    mods = _adaln(cvec8, ada_w, ada_b)
    mods = mods.reshape(DEPTH, 8, 1, 6 * D_MODEL)

    rc, rs = _rope_tables(t_lat)
    ckpe = jnp.pad(cache_mla_kpe, ((0, 0), (0, 0), (0, 0), (0, LANES - MLA_ROPE)))
    cache = (cache_mla_ckv, ckpe, rc, rs, _embed_block_diag_layers(state_gdn),
             _embed_block_diag_layers(jnp.swapaxes(state_rwkv, -1, -2)))
    xp = x_prompt.reshape(b_ctx * t_ctx, D_MODEL)
    xs = x_sample.reshape(b_lat * t_lat, D_MODEL)
    n_ctx = xp.shape[0]
    tm = POST_TM
    state_shape = (b_ctx, DEPTH, 2, HEADS, HEAD_DIM, HEAD_DIM)
    ctx_outs = (jnp.zeros((b_ctx, DEPTH, t_ctx, KV_LORA), F32), jnp.zeros((b_ctx, DEPTH, t_ctx, MLA_ROPE), F32),
                jnp.zeros(state_shape, F32), jnp.zeros(state_shape, F32))
    for l in range(DEPTH):
        final = l == DEPTH - 1
        no_pairs = jnp.zeros((N_EXPERTS, LANES), F32)
        (x1c, hc, eidc, rankc, ewc, cnt_c), ctx_outs = _layer_front(xp, mods, weights, l, consts, t_ctx, 0, None,
                                                                    tm, t_ctx, no_pairs, ctx_outs)
        (x1s, hs, eids, ranks, ews, cnt), _ = _layer_front(xs, mods, weights, l, consts, t_lat, 1, cache,
                                                           tm, 256, cnt_c)
        yg_c, yg_s = _moe_experts([hc, hs], jnp.concatenate([eidc, eids], axis=1),
                                  jnp.concatenate([rankc, ranks], axis=1), cnt, weights, l)
        xp = _moe_combine(yg_c, 0, ewc, x1c, mods, l, nf, tm, t_ctx, 0, final)
        xs = _moe_combine(yg_s, 0, ews, x1s, mods, l, nf, tm, t_lat, 1, final)

    y_prompt = xp.reshape(b_ctx, t_ctx, D_MODEL)
    y_sample = xs.reshape(b_lat, t_lat, D_MODEL)
    new_ckv, new_kpe, new_gdn, new_rwkv = ctx_outs
    return (y_prompt, y_sample, new_ckv, new_kpe, new_gdn, new_rwkv)


def _embed_block_diag_layers(s):
    b = s.shape[0]
    return _embed_block_diag(s.reshape(b * DEPTH, 2, HEADS, HEAD_DIM, HEAD_DIM)).reshape(
        b, DEPTH, 2, MIX_W, MIX_W)
```

```python
import functools

import numpy as np
import jax
import jax.numpy as jnp
from jax import lax
from jax.experimental import pallas as pl
from jax.experimental.pallas import tpu as pltpu
from jax.experimental.pallas import tpu_sc as plsc

F32 = jnp.float32
BF16 = jnp.bfloat16

D_MODEL = 1024
DEPTH = 2
PAST_LEN = 512
GRID_W = 64
NORM_EPS = 1e-6

MLA_HEADS = 4
MLA_NOPE = 128
MLA_ROPE = 64
MLA_V = 128
Q_LORA = 384
KV_LORA = 256
ROPE_BASE = 10000.0
MLA_SCALE = (MLA_NOPE + MLA_ROPE) ** -0.5

HEADS = 4
HEAD_DIM = 64
MIX_W = HEADS * HEAD_DIM
GDN_CONV_CH = 3 * MIX_W
CHUNK = 64
RWKV_GN_EPS = 64e-5

N_EXPERTS = 64
TOP_K = 8
N_GROUPS = 8
GROUP_SIZE = N_EXPERTS // N_GROUPS
TOPK_GROUPS = 4
D_EXPERT = 256
ROUTE_SCALE = 2.5

P_MLA = Q_LORA + KV_LORA + MLA_ROPE
P_GDN = GDN_CONV_CH + MIX_W + 4 * HEADS
P_RWKV = 3 * MIX_W + 128 + 128 + 128

LANES = 128
ZM_W = Q_LORA + KV_LORA + 2 * LANES
ZG_W = GDN_CONV_CH + MIX_W
ZR_W = P_RWKV
QH_W = 2 * LANES
VMEM_LIMIT = 56 * 1024 * 1024
POST_TM = 512

_ROPE_SWAP = np.concatenate([np.arange(16, 32), np.arange(0, 16), np.arange(48, 64), np.arange(32, 48)])


def _sigmoid(x):
    return 1.0 / (1.0 + jnp.exp(-x))


def _silu(x):
    return x * _sigmoid(x)


def _softplus(x):
    return jnp.maximum(x, 0.0) + jnp.log(1.0 + jnp.exp(-jnp.abs(x)))


def _rms(x, g, eps=NORM_EPS):
    return x * lax.rsqrt(jnp.mean(x * x, axis=-1, keepdims=True) + eps) * g


def _mm(a, b):
    return jnp.dot(a.astype(BF16), b.astype(BF16), preferred_element_type=F32)


def _mm_nt(a, b):
    return lax.dot_general(a.astype(BF16), b.astype(BF16), (((1,), (1,)), ((), ())),
                           preferred_element_type=F32)


def _mm_tn(a, b):
    return lax.dot_general(a.astype(BF16), b.astype(BF16), (((0,), (0,)), ((), ())),
                           preferred_element_type=F32)


def _split3(x):
    p1 = x.astype(BF16)
    r1 = x - p1.astype(F32)
    p2 = r1.astype(BF16)
    r2 = r1 - p2.astype(F32)
    return p1, p2, r2.astype(BF16)


def _mm_sel_l(sel, x):
    p1, p2, p3 = _split3(x)
    return _mm(sel, p1) + _mm(sel, p2) + _mm(sel, p3)


def _mm_sel_r(x, sel):
    p1, p2, p3 = _split3(x)
    return _mm(p1, sel) + _mm(p2, sel) + _mm(p3, sel)


def _iota(shape, dim):
    return lax.broadcasted_iota(jnp.int32, shape, dim)


def _layer_spec(a, layer, **kw):
    nd = a.ndim - 1
    return pl.BlockSpec((None,) + a.shape[1:], lambda *_: (layer,) + (0,) * nd, **kw)


def _const_spec(a, **kw):
    return pl.BlockSpec(a.shape, lambda *_: (0,) * a.ndim, **kw)


def _bd(x, maskbd):
    xb = x.astype(BF16)
    return jnp.concatenate([xb] * HEADS, axis=0) * maskbd


def _chunk_masks(rev):
    row = _iota((CHUNK, MIX_W), 0)
    col = jnp.bitwise_and(_iota((CHUNK, MIX_W), 1), HEAD_DIM - 1)
    r2 = _iota((CHUNK, CHUNK), 0)
    c2 = _iota((CHUNK, CHUNK), 1)
    if rev:
        inc, strict, tri = row <= col, row < col, r2 <= c2
    else:
        inc, strict, tri = row >= col, row > col, r2 >= c2
    eye = jnp.where(row == col, 1.0, 0.0).astype(F32)
    return inc, strict, jnp.where(tri, 1.0, 0.0).astype(BF16), eye


def _split2(x):
    hi = x.astype(BF16)
    return hi, (x - hi.astype(F32)).astype(BF16)


def _mm_bd3(x, p, maskbd):
    n = x.shape[0]
    xh, xl = _split2(x)
    ph, pl_ = _split2(p)
    r = jnp.dot(jnp.concatenate([xh, xl], axis=0), _bd(ph, maskbd), preferred_element_type=F32)
    return r[:n] + r[n:] + jnp.dot(xh, _bd(pl_, maskbd), preferred_element_type=F32)


def _neumann_inverse(a_list, eye_list, maskbd):
    bs = [-a for a in a_list]
    ms = [eye + b for eye, b in zip(eye_list, bs)]
    ps = [_mm_bd3(b, b, maskbd) for b in bs]
    for _ in range(4):
        boths = [_mm_bd3(jnp.concatenate([m, p], axis=0), p, maskbd) for m, p in zip(ms, ps)]
        ms = [m + both[:CHUNK] for m, both in zip(ms, boths)]
        ps = [both[CHUNK:] for both in boths]
    return [m + _mm_bd3(m, p, maskbd) for m, p in zip(ms, ps)]


def _adaln_kernel(c_ref, w_ref, b_ref, o_ref):
    cv = c_ref[...]
    o_ref[0] = _mm(_silu(cv), w_ref[0]) + b_ref[0]


def _adaln(cvec8, ada_w, ada_b):
    tn = 768
    n_out = 6 * D_MODEL
    return pl.pallas_call(
        _adaln_kernel,
        grid=(DEPTH, n_out // tn),
        in_specs=[
            pl.BlockSpec((8, D_MODEL), lambda l, j: (0, 0)),
            pl.BlockSpec((1, D_MODEL, tn), lambda l, j: (l, 0, j)),
            pl.BlockSpec((1, 1, tn), lambda l, j: (l, 0, j)),
        ],
        out_specs=pl.BlockSpec((1, 8, tn), lambda l, j: (l, 0, j)),
        out_shape=jax.ShapeDtypeStruct((DEPTH, 8, n_out), F32),
        compiler_params=pltpu.CompilerParams(dimension_semantics=("arbitrary", "arbitrary"),
                                             vmem_limit_bytes=VMEM_LIMIT),
        name="adaln",
    )(cvec8, ada_w, ada_b.reshape(DEPTH, 1, n_out))


_KPE0 = Q_LORA + KV_LORA
_W_IN_MOVES = (
    [(0, 0, P_MLA)]
    + [(P_MLA + LANES - MLA_ROPE + 16 * j, _KPE0 + 16 * int(_ROPE_SWAP[16 * j] // 16), 16) for j in range(4)]
    + [(ZM_W, P_MLA, ZG_W), (ZM_W + ZG_W, P_MLA + ZG_W, 4 * HEADS), (ZM_W + ZG_W + LANES, P_MLA + P_GDN, P_RWKV)]
)
W_IN_PAD = ZM_W + ZG_W + LANES + ZR_W


def _inproj_kernel(x_ref, mod_ref, n1_ref, w_ref, zm_ref, zg_ref, zab_ref, zr_ref, w_s):
    @pl.when(pl.program_id(0) == 0)
    def _():
        w_s[...] = jnp.zeros(w_s.shape, BF16)
        for dst, src, width in _W_IN_MOVES:
            w_s[:, dst:dst + width] = w_ref[:, src:src + width].astype(BF16)

    m = mod_ref[0]
    sh = m[:, 0:D_MODEL]
    sc = m[:, D_MODEL:2 * D_MODEL]
    h = _rms(x_ref[...], n1_ref[...]) * (1.0 + sc) + sh
    z = _mm(h, w_s[...])
    o1 = ZM_W
    o2 = o1 + ZG_W
    o3 = o2 + LANES
    zm_ref[...] = z[:, :o1]
    zg_ref[...] = z[:, o1:o2]
    zab_ref[...] = z[:, o2:o3]
    zr_ref[...] = z[:, o3:]


def _inproj(x2d, mods, lw, layer, tm, seq_len, mod_base):
    n = x2d.shape[0]
    tiles_per_seq = seq_len // tm if mod_base else 1

    def mod_idx(i):
        return (layer, mod_base + i // tiles_per_seq if mod_base else 0, 0, 0)

    return pl.pallas_call(
        _inproj_kernel,
        grid=(n // tm,),
        in_specs=[
            pl.BlockSpec((tm, D_MODEL), lambda i: (i, 0)),
            pl.BlockSpec((None, 1, 1, 6 * D_MODEL), mod_idx),
            _layer_spec(lw["norm1"], layer),
            _layer_spec(lw["w_in"], layer, pipeline_mode=pl.Buffered(1)),
        ],
        out_specs=[
            pl.BlockSpec((tm, ZM_W), lambda i: (i, 0)),
            pl.BlockSpec((tm, ZG_W), lambda i: (i, 0)),
            pl.BlockSpec((tm, LANES), lambda i: (i, 0)),
            pl.BlockSpec((tm, ZR_W), lambda i: (i, 0)),
        ],
        out_shape=[
            jax.ShapeDtypeStruct((n, ZM_W), F32),
            jax.ShapeDtypeStruct((n, ZG_W), F32),
            jax.ShapeDtypeStruct((n, LANES), F32),
            jax.ShapeDtypeStruct((n, ZR_W), F32),
        ],
        scratch_shapes=[pltpu.VMEM((D_MODEL, W_IN_PAD), BF16)],
        compiler_params=pltpu.CompilerParams(dimension_semantics=("arbitrary",),
                                             vmem_limit_bytes=VMEM_LIMIT),
        name="inproj",
    )(x2d, mods, lw["norm1"], lw["w_in"])


MLA_SEQ_PER_STEP = 4


def _mla_ctx_kernel(zm_ref, qn_ref, wuq_ref, kvn_ref, wukv_ref, prev_ckv, prev_kpe, o_ref, ckv_ref, kpe_ref,
                    *, seq_len, n_seq):
    del prev_ckv, prev_kpe
    o_kpe = Q_LORA + KV_LORA
    zm = zm_ref[...]
    ckv = _rms(zm[:, Q_LORA:o_kpe], kvn_ref[...])
    kpe = zm[:, o_kpe:o_kpe + LANES]
    kv = _mm(ckv, wukv_ref[...])
    q = _mm(_rms(zm[:, :Q_LORA], qn_ref[...]), wuq_ref[...])
    kpe_b = kpe.astype(BF16)
    seqs = [slice(i * seq_len, (i + 1) * seq_len) for i in range(n_seq)]
    for i, r in enumerate(seqs):
        ckv_ref[i] = ckv[r]
        kpe_ref[i] = kpe[r, :MLA_ROPE]
    pairs = [(r, h * QH_W) for r in seqs for h in range(MLA_HEADS)]
    scores = [_mm_nt(q[r, c0:c0 + QH_W], jnp.concatenate([kv[r, c0:c0 + LANES].astype(BF16), kpe_b[r]], axis=1))
              * MLA_SCALE for r, c0 in pairs]
    exps = [jnp.exp(s - jnp.max(s, axis=-1, keepdims=True)) for s in scores]
    outs = [_mm(e, kv[r, c0 + LANES:c0 + QH_W]) / jnp.sum(e, axis=-1, keepdims=True)
            for e, (r, c0) in zip(exps, pairs)]
    for o, (r, c0) in zip(outs, pairs):
        h = c0 // QH_W
        o_ref[r, h * MLA_V:(h + 1) * MLA_V] = o


def _mla_ctx(zm, lw, layer, seq_len, prev):
    n = zm.shape[0]
    nb = n // seq_len
    n_seq = MLA_SEQ_PER_STEP
    rows = n_seq * seq_len
    weights = [lw["q_norm"], lw["w_uq"], lw["kv_norm"], lw["w_ukv"]]
    return pl.pallas_call(
        functools.partial(_mla_ctx_kernel, seq_len=seq_len, n_seq=n_seq),
        grid=(nb // n_seq,),
        in_specs=[pl.BlockSpec((rows, ZM_W), lambda b: (b, 0))] + [_layer_spec(a, layer) for a in weights]
        + [pl.BlockSpec(memory_space=pl.ANY)] * 2,
        out_specs=[
            pl.BlockSpec((rows, MLA_HEADS * MLA_V), lambda b: (b, 0)),
            pl.BlockSpec((n_seq, None, seq_len, KV_LORA), lambda b: (b, layer, 0, 0)),
            pl.BlockSpec((n_seq, None, seq_len, MLA_ROPE), lambda b: (b, layer, 0, 0)),
        ],
        out_shape=[
            jax.ShapeDtypeStruct((n, MLA_HEADS * MLA_V), F32),
            jax.ShapeDtypeStruct((nb, DEPTH, seq_len, KV_LORA), F32),
            jax.ShapeDtypeStruct((nb, DEPTH, seq_len, MLA_ROPE), F32),
        ],
        input_output_aliases={5: 1, 6: 2},
        compiler_params=pltpu.CompilerParams(dimension_semantics=("arbitrary",),
                                             vmem_limit_bytes=VMEM_LIMIT),
        name="mla_ctx",
    )(zm, *weights, *prev)


def _mla_lat_kernel(zm_ref, cckv_ref, ckpe_ref, rc_ref, rs_ref, qn_ref, wuq_ref, wuqs_ref, kvn_ref, wukv_ref,
                    o_ref, k_s, v_s, *, seq_len, tq, past):
    qi = pl.program_id(1)
    o_kpe = Q_LORA + KV_LORA

    @pl.when(qi == 0)
    def _():
        zm = zm_ref[...]
        ckv = _rms(zm[:, Q_LORA:o_kpe], kvn_ref[...])
        kpe = zm[:, o_kpe:o_kpe + LANES] * rc_ref[...] + zm[:, o_kpe + LANES:o_kpe + 2 * LANES] * rs_ref[...]
        kvc = _mm(cckv_ref[0], wukv_ref[...])
        kpc = ckpe_ref[0].astype(BF16)
        kv = _mm(ckv, wukv_ref[...])
        kpe = kpe.astype(BF16)
        for h in range(MLA_HEADS):
            c0 = h * QH_W
            k_s[0:past, c0:c0 + LANES] = kvc[:, c0:c0 + LANES].astype(BF16)
            k_s[0:past, c0 + LANES:c0 + QH_W] = kpc
            v_s[0:past, h * MLA_V:(h + 1) * MLA_V] = kvc[:, c0 + LANES:c0 + QH_W].astype(BF16)
            k_s[past:past + seq_len, c0:c0 + LANES] = kv[:, c0:c0 + LANES].astype(BF16)
            k_s[past:past + seq_len, c0 + LANES:c0 + QH_W] = kpe
            v_s[past:past + seq_len, h * MLA_V:(h + 1) * MLA_V] = kv[:, c0 + LANES:c0 + QH_W].astype(BF16)

    r0 = pl.multiple_of(qi * tq, tq)
    zq = zm_ref[pl.ds(r0, tq), :]
    cq = _rms(zq[:, :Q_LORA], qn_ref[...])
    q = _mm(cq, wuq_ref[...])
    qs = _mm(cq, wuqs_ref[...])
    qc = jnp.concatenate([jnp.ones((tq, LANES), F32), rc_ref[pl.ds(r0, tq), :]], axis=1)
    qsn = jnp.concatenate([jnp.zeros((tq, LANES), F32), rs_ref[pl.ds(r0, tq), :]], axis=1)
    for h in range(MLA_HEADS):
        c0 = h * QH_W
        qh = q[:, c0:c0 + QH_W] * qc + qs[:, c0:c0 + QH_W] * qsn
        s = _mm_nt(qh, k_s[:, c0:c0 + QH_W]) * MLA_SCALE
        e = jnp.exp(s - jnp.max(s, axis=-1, keepdims=True))
        den = jnp.sum(e, axis=-1, keepdims=True)
        o_ref[:, h * MLA_V:(h + 1) * MLA_V] = _mm(e, v_s[:, h * MLA_V:(h + 1) * MLA_V]) / den


def _mla_lat(zm, lw, layer, seq_len, tq, cache):
    n = zm.shape[0]
    nb = n // seq_len
    past = PAST_LEN
    tk = past + seq_len
    cckv, ckpe, rc, rs = cache
    weights = [lw["q_norm"], lw["w_uq"], lw["w_uq_sw"], lw["kv_norm"], lw["w_ukv"]]
    return pl.pallas_call(
        functools.partial(_mla_lat_kernel, seq_len=seq_len, tq=tq, past=past),
        grid=(nb, seq_len // tq),
        in_specs=[pl.BlockSpec((seq_len, ZM_W), lambda b, q: (b, 0)),
                  pl.BlockSpec((1, None, past, KV_LORA), lambda b, q: (b, layer, 0, 0)),
                  pl.BlockSpec((1, None, past, LANES), lambda b, q: (b, layer, 0, 0)),
                  _const_spec(rc), _const_spec(rs)] + [_layer_spec(a, layer) for a in weights],
        out_specs=pl.BlockSpec((tq, MLA_HEADS * MLA_V), lambda b, q: (b * (seq_len // tq) + q, 0)),
        out_shape=jax.ShapeDtypeStruct((n, MLA_HEADS * MLA_V), F32),
        scratch_shapes=[
            pltpu.VMEM((tk, MLA_HEADS * QH_W), BF16),
            pltpu.VMEM((tk, MLA_HEADS * MLA_V), BF16),
        ],
        compiler_params=pltpu.CompilerParams(dimension_semantics=("arbitrary", "arbitrary"),
                                             vmem_limit_bytes=VMEM_LIMIT),
        name="mla_lat",
    )(zm, cckv, ckpe, rc, rs, *weights)


SEQ_PER_STEP = 4
CHUNK_GROUP = 4
GDN_PRE_W = 5 * MIX_W


def _for_chunk_groups(n_chunks, fn):
    if n_chunks == CHUNK_GROUP:
        fn(0)
    else:
        def body(gi, carry):
            fn(gi * CHUNK_GROUP)
            return carry
        lax.fori_loop(0, n_chunks // CHUNK_GROUP, body, 0)


def _gdn_prepare(items, maskbd):
    n = range(len(items))
    qs, ks, vs, gs, betas, masks, revs = zip(*items)
    gcs = [_mm_sel_l(masks[i][2], gs[i]) for i in n]
    decays = []
    for i in n:
        inc, eye = masks[i][0], masks[i][3]
        gc_row = jnp.sum(eye * gcs[i], axis=0, keepdims=True)
        decays.append(jnp.where(inc, jnp.exp(jnp.where(inc, gcs[i] - gc_row, 0.0)), 0.0))
    kbs = [ks[i] * betas[i] for i in n]
    aqs = [_mm_nt(jnp.concatenate([kbs[i], qs[i]], axis=0), _bd(ks[i], maskbd)) for i in n]
    a_mats = [jnp.where(masks[i][1], aqs[i][:CHUNK] * decays[i], 0.0) for i in n]
    t_invs = _neumann_inverse(a_mats, [m[3] for m in masks], maskbd)
    egcs = [jnp.exp(gc) for gc in gcs]
    uws = [_mm(t_invs[i], jnp.concatenate([_bd(vs[i] * betas[i], maskbd), _bd(kbs[i] * egcs[i], maskbd)], axis=1))
           for i in n]
    out = []
    for i in n:
        g_last = gcs[i][0:1] if revs[i] else gcs[i][CHUNK - 1:CHUNK]
        pre = jnp.concatenate([uws[i], qs[i] * egcs[i], aqs[i][CHUNK:] * decays[i],
                               ks[i] * jnp.exp(g_last - gcs[i])], axis=1)
        out.append((pre, jnp.broadcast_to(jnp.exp(g_last), (8, MIX_W))))
    return out


def _gdn_step(items, maskbd):
    w = MIX_W
    n = range(len(items))
    pres, egls, states = zip(*items)
    wqs = [_mm(jnp.concatenate([pres[i][:, w:2 * w], pres[i][:, 2 * w:3 * w]], axis=0), states[i]) for i in n]
    v_news = [pres[i][:, :w] - wqs[i][:CHUNK] for i in n]
    outs = [wqs[i][CHUNK:] + _mm(pres[i][:, 3 * w:4 * w], _bd(v_news[i], maskbd)) for i in n]
    upds = [_mm_tn(pres[i][:, 4 * w:], v_news[i]) for i in n]
    mask_f = maskbd.astype(F32)
    return [(outs[i], states[i] * egls[i] + upds[i] * mask_f) for i in n]


def _gdn_kernel(*refs, seq_len, n_seq, cached):
    if cached:
        (zg_ref, zab_ref, s0_ref, conv_ref, alog_ref, dtb_ref, gn_ref, eexp_ref, ones_ref, maskbd_ref,
         o_ref, sout_ref, q_s, k_s, v_s, ge_s, pre_s, gl_s, oacc_s, st_s) = refs
    else:
        zg_ref, zab_ref, conv_ref, alog_ref, dtb_ref, gn_ref, eexp_ref, ones_ref, maskbd_ref = refs[:9]
        o_ref, sout_ref, q_s, k_s, v_s, ge_s, pre_s, gl_s, oacc_s, st_s = refs[-10:]
    t = seq_len * n_seq
    z = zg_ref[:, :GDN_CONV_CH]
    rowi = jnp.bitwise_and(_iota((t, 1), 0), seq_len - 1)
    zp = jnp.where(rowi == 0, 0.0, pltpu.roll(z, 1, 0))
    zn = jnp.where(rowi == seq_len - 1, 0.0, pltpu.roll(z, t - 1, 0))
    cw = conv_ref[...]
    qkv = _silu(zp * cw[0:1] + z * cw[1:2] + zn * cw[2:3])
    ones_bd = ones_ref[...]
    q = qkv[:, :MIX_W]
    k = qkv[:, MIX_W:2 * MIX_W]
    q_s[...] = q * lax.rsqrt(_mm_sel_r(q * q, ones_bd) + 1e-6) * (HEAD_DIM ** -0.5)
    k_s[...] = k * lax.rsqrt(_mm_sel_r(k * k, ones_bd) + 1e-6)
    v_s[...] = qkv[:, 2 * MIX_W:]
    ab = zab_ref[...]
    lane = _iota((t, LANES), 1)
    gb = jnp.where(lane < 2 * HEADS, -jnp.exp(alog_ref[...]) * _softplus(ab + dtb_ref[...]), _sigmoid(ab))
    ge_s[...] = _mm_sel_r(gb, eexp_ref[...])
    oacc_s[...] = jnp.zeros((t, MIX_W), F32)
    if cached:
        st_s[...] = s0_ref[...]
    else:
        st_s[...] = jnp.zeros((n_seq, 2, MIX_W, MIX_W), F32)
    maskbd = maskbd_ref[...]
    masks = (_chunk_masks(False), _chunk_masks(True))
    n_chunks = seq_len // CHUNK

    def prepare_group(c0):
        where, items = [], []
        for j in range(CHUNK_GROUP):
            c = c0 + j
            rows = pl.ds(pl.multiple_of(c * CHUNK, CHUNK), CHUNK)
            for d in range(2):
                where.append((d, c, rows))
                items.append((q_s[rows, :], k_s[rows, :], v_s[rows, :], ge_s[rows, d * MIX_W:(d + 1) * MIX_W],
                              ge_s[rows, (2 + d) * MIX_W:(3 + d) * MIX_W], masks[d], d == 1))
        for (d, c, rows), (pre, egl) in zip(where, _gdn_prepare(items, maskbd)):
            pre_s[d, rows, :] = pre
            gl_s[d, pl.ds(pl.multiple_of(c * 8, 8), 8), :] = egl

    _for_chunk_groups(n_chunks * n_seq, prepare_group)

    def body(i, carry):
        where, items = [], []
        for q in range(n_seq):
            for d in range(2):
                c = q * n_chunks + (i if d == 0 else n_chunks - 1 - i)
                rows = pl.ds(pl.multiple_of(c * CHUNK, CHUNK), CHUNK)
                where.append((q, d, rows))
                items.append((pre_s[d, rows, :], gl_s[d, pl.ds(pl.multiple_of(c * 8, 8), 8), :][0:1], st_s[q, d]))
        for (q, d, rows), (o, s_new) in zip(where, _gdn_step(items, maskbd)):
            oacc_s[rows, :] = oacc_s[rows, :] + o
            st_s[q, d] = s_new
        return carry

    lax.fori_loop(0, n_chunks, body, 0)
    o = oacc_s[...]
    ms = _mm_sel_r(o * o, ones_bd) * (1.0 / HEAD_DIM)
    gate = zg_ref[:, GDN_CONV_CH:]
    o_ref[...] = o * lax.rsqrt(ms + NORM_EPS) * gn_ref[...] * _silu(gate)
    for q in range(n_seq):
        for d in range(2):
            for h in range(HEADS):
                sl = slice(h * HEAD_DIM, (h + 1) * HEAD_DIM)
                sout_ref[q, d, h] = st_s[q, d, sl, sl]


def _state_out(nb, n_seq, layer, cached, prev, args, in_specs):
    if cached:
        return (pl.BlockSpec((n_seq, 2, HEADS, HEAD_DIM, HEAD_DIM), lambda b: (b, 0, 0, 0, 0)),
                jax.ShapeDtypeStruct((nb, 2, HEADS, HEAD_DIM, HEAD_DIM), F32), {})
    aliases = {len(args): 1}
    args.append(prev)
    in_specs.append(pl.BlockSpec(memory_space=pl.ANY))
    return (pl.BlockSpec((n_seq, None, 2, HEADS, HEAD_DIM, HEAD_DIM), lambda b: (b, layer, 0, 0, 0, 0)),
            jax.ShapeDtypeStruct((nb, DEPTH, 2, HEADS, HEAD_DIM, HEAD_DIM), F32), aliases)


def _gdn(zg, zab, lw, layer, consts, seq_len, s0_bd=None, prev=None):
    n = zg.shape[0]
    nb = n // seq_len
    cached = s0_bd is not None
    n_seq = 1 if cached else SEQ_PER_STEP
    rows = seq_len * n_seq
    args = [zg, zab]
    in_specs = [pl.BlockSpec((rows, ZG_W), lambda b: (b, 0)),
                pl.BlockSpec((rows, LANES), lambda b: (b, 0))]
    if cached:
        args.append(s0_bd)
        in_specs.append(pl.BlockSpec((1, None, 2, MIX_W, MIX_W), lambda b: (b, layer, 0, 0, 0)))
    layered = [lw["gdn_conv"], lw["gdn_alog"], lw["gdn_dtb"], lw["gdn_norm"]]
    const = [consts["eexp"], consts["ones_bd"], consts["maskbd"]]
    args += layered + const
    in_specs += [_layer_spec(a, layer) for a in layered] + [_const_spec(a) for a in const]
    s_spec, s_shape, aliases = _state_out(nb, n_seq, layer, cached, prev, args, in_specs)
    return pl.pallas_call(
        functools.partial(_gdn_kernel, seq_len=seq_len, n_seq=n_seq, cached=cached),
        grid=(nb // n_seq,),
        in_specs=in_specs,
        out_specs=[
            pl.BlockSpec((rows, MIX_W), lambda b: (b, 0)),
            s_spec,
        ],
        out_shape=[jax.ShapeDtypeStruct((n, MIX_W), F32), s_shape],
        input_output_aliases=aliases,
        scratch_shapes=[
            pltpu.VMEM((rows, MIX_W), F32),
            pltpu.VMEM((rows, MIX_W), F32),
            pltpu.VMEM((rows, MIX_W), F32),
            pltpu.VMEM((rows, 4 * MIX_W), F32),
            pltpu.VMEM((2, rows, GDN_PRE_W), F32),
            pltpu.VMEM((2, rows // CHUNK * 8, MIX_W), F32),
            pltpu.VMEM((rows, MIX_W), F32),
            pltpu.VMEM((n_seq, 2, MIX_W, MIX_W), F32),
        ],
        compiler_params=pltpu.CompilerParams(dimension_semantics=("arbitrary",),
                                             vmem_limit_bytes=VMEM_LIMIT),
        name="gdn_lat" if cached else "gdn_ctx",
    )(*args)


RWKV_PRE_W = 7 * MIX_W


def _rwkv_prepare(items, maskbd, eye_full):
    n = range(len(items))
    rs, kds, vs, kks, bs, lws, masks, revs = zip(*items)
    cums = [_mm_sel_l(masks[i][2], lws[i]) for i in n]
    einvs = [jnp.exp(-c) for c in cums]
    kts = [kks[i] * jnp.exp(cums[i] - lws[i]) for i in n]
    rts = [rs[i] * jnp.exp(cums[i]) for i in n]
    krs = [jnp.concatenate([kts[i], rts[i]], axis=0) for i in n]
    lb_alls = [_mm_nt(krs[i], _bd(bs[i] * einvs[i], maskbd)) for i in n]
    lk_alls = [_mm_nt(krs[i], _bd(kds[i] * einvs[i], maskbd)) for i in n]
    lbs = [jnp.where(masks[i][1], lb_alls[i][:CHUNK], 0.0) for i in n]
    t_invs = _neumann_inverse(lbs, [m[3] for m in masks], maskbd)
    lvs = [_mm(jnp.concatenate([jnp.where(masks[i][1], lk_alls[i][:CHUNK], 0.0),
                                jnp.where(masks[i][0], lk_alls[i][CHUNK:], 0.0)], axis=0), _bd(vs[i], maskbd))
           for i in n]
    tkps = [_mm(t_invs[i], jnp.concatenate([_bd(kts[i], maskbd), _bd(lvs[i][:CHUNK], maskbd)], axis=1)) for i in n]
    out = []
    for i in n:
        c_last = cums[i][0:1] if revs[i] else cums[i][CHUNK - 1:CHUNK]
        tail = jnp.exp(c_last - cums[i])
        rb = jnp.where(masks[i][0], lb_alls[i][CHUNK:], 0.0)
        pre = jnp.concatenate([tkps[i][:, :MIX_W], rts[i], tkps[i][:, MIX_W:], lvs[i][CHUNK:], rb,
                               kds[i] * tail, bs[i] * tail], axis=1)
        gcol = jnp.sum(eye_full * jnp.exp(c_last), axis=1, keepdims=True)
        out.append((pre, jnp.broadcast_to(gcol, (MIX_W, LANES))))
    return out


def _rwkv_step(items, maskbd):
    w = MIX_W
    n = range(len(items))
    pres, vs, gcols, states = zip(*items)
    prs = [_mm(jnp.concatenate([pres[i][:, :w], pres[i][:, w:2 * w]], axis=0), states[i]) for i in n]
    ps = [prs[i][:CHUNK] + pres[i][:, 2 * w:3 * w] for i in n]
    outs = [prs[i][CHUNK:] + pres[i][:, 3 * w:4 * w] - _mm(pres[i][:, 4 * w:5 * w], _bd(ps[i], maskbd)) for i in n]
    upds = [_mm_tn(jnp.concatenate([pres[i][:, 5 * w:6 * w], pres[i][:, 6 * w:]], axis=0),
                   jnp.concatenate([vs[i], -ps[i]], axis=0)) for i in n]
    mask_f = maskbd.astype(F32)
    return [(outs[i], states[i] * jnp.concatenate([gcols[i], gcols[i]], axis=1) + upds[i] * mask_f) for i in n]


def _rwkv_kernel(*refs, seq_len, n_seq, cached):
    if cached:
        (zr_ref, s0_ref, mup_ref, mun_ref, w0_ref, w2_ref, a0_ref, a2_ref, g2_ref, kk_ref, ka_ref, rk_ref,
         gnw_ref, gnb_ref, ones_ref, maskbd_ref, o_ref, sout_ref,
         r_s, v_s, kk_s, dir_s, bg_s, pre_s, gcol_s, oacc_s, st_s) = refs
    else:
        (zr_ref, mup_ref, mun_ref, w0_ref, w2_ref, a0_ref, a2_ref, g2_ref, kk_ref, ka_ref, rk_ref,
         gnw_ref, gnb_ref, ones_ref, maskbd_ref) = refs[:15]
        o_ref, sout_ref, r_s, v_s, kk_s, dir_s, bg_s, pre_s, gcol_s, oacc_s, st_s = refs[-11:]
    t = seq_len * n_seq
    z = zr_ref[...]
    rowi = jnp.bitwise_and(_iota((t, 1), 0), seq_len - 1)
    zp = jnp.where(rowi == 0, 0.0, pltpu.roll(z, 1, 0))
    zn = jnp.where(rowi == seq_len - 1, 0.0, pltpu.roll(z, t - 1, 0))
    z = z + mup_ref[...] * (zp - z) + mun_ref[...] * (zn - z)
    w = MIX_W
    r = z[:, :w]
    k = z[:, w:2 * w]
    v = z[:, 2 * w:3 * w]
    wd = jnp.tanh(z[:, 3 * w:3 * w + LANES])
    ad = z[:, 3 * w + LANES:3 * w + 2 * LANES]
    gd = _sigmoid(z[:, 3 * w + 2 * LANES:])
    ones_bd = ones_ref[...]
    kk = k * kk_ref[...]
    kk = kk * lax.rsqrt(_mm_sel_r(kk * kk, ones_bd) + 1e-6)
    r_s[...] = r
    v_s[...] = v
    kk_s[...] = kk
    bonus = jnp.zeros((t, w), F32)
    for d in range(2):
        w_log = -_softplus(-(w0_ref[d:d + 1] + _mm(wd, w2_ref[d]))) - 0.5
        a = _sigmoid(a0_ref[d:d + 1] + _mm(ad, a2_ref[d]))
        kd = k * (1.0 + (a - 1.0) * ka_ref[...])
        dir_s[:, (3 * d) * w:(3 * d + 1) * w] = -jnp.exp(w_log)
        dir_s[:, (3 * d + 1) * w:(3 * d + 2) * w] = kd
        dir_s[:, (3 * d + 2) * w:(3 * d + 3) * w] = kk * a
        bonus = bonus + _mm_sel_r(r * kd * rk_ref[...], ones_bd) * v
    bg_s[:, :w] = bonus
    bg_s[:, w:] = _mm(gd, g2_ref[...])
    oacc_s[...] = jnp.zeros((t, w), F32)
    if cached:
        st_s[...] = s0_ref[...]
    else:
        st_s[...] = jnp.zeros((n_seq, 2, w, w), F32)
    maskbd = maskbd_ref[...]
    masks = (_chunk_masks(False), _chunk_masks(True))
    eye_full = jnp.where(_iota((w, w), 0) == _iota((w, w), 1), 1.0, 0.0).astype(F32)
    n_chunks = seq_len // CHUNK

    def prepare_group(c0):
        where, items = [], []
        for j in range(CHUNK_GROUP):
            c = c0 + j
            rows = pl.ds(pl.multiple_of(c * CHUNK, CHUNK), CHUNK)
            for d in range(2):
                where.append((d, c, rows))
                items.append((r_s[rows, :], dir_s[rows, (3 * d + 1) * w:(3 * d + 2) * w], v_s[rows, :], kk_s[rows, :],
                              dir_s[rows, (3 * d + 2) * w:(3 * d + 3) * w], dir_s[rows, (3 * d) * w:(3 * d + 1) * w],
                              masks[d], d == 1))
        for (d, c, rows), (pre, gcol) in zip(where, _rwkv_prepare(items, maskbd, eye_full)):
            pre_s[d, rows, :] = pre
            gcol_s[d, pl.ds(pl.multiple_of(c * w, w), w), :] = gcol

    _for_chunk_groups(n_chunks * n_seq, prepare_group)

    def body(i, carry):
        where, items = [], []
        for q in range(n_seq):
            for d in range(2):
                c = q * n_chunks + (i if d == 0 else n_chunks - 1 - i)
                rows = pl.ds(pl.multiple_of(c * CHUNK, CHUNK), CHUNK)
                where.append((q, d, rows))
                items.append((pre_s[d, rows, :], v_s[rows, :], gcol_s[d, pl.ds(pl.multiple_of(c * w, w), w), :],
                              st_s[q, d]))
        for (q, d, rows), (o, z_new) in zip(where, _rwkv_step(items, maskbd)):
            oacc_s[rows, :] = oacc_s[rows, :] + o
            st_s[q, d] = z_new
        return carry

    lax.fori_loop(0, n_chunks, body, 0)
    o = oacc_s[...]
    inv_n = 1.0 / HEAD_DIM
    mu = _mm_sel_r(o, ones_bd) * inv_n
    oc = o - mu
    var = _mm_sel_r(oc * oc, ones_bd) * inv_n
    y = oc * lax.rsqrt(var + RWKV_GN_EPS) * gnw_ref[...] + gnb_ref[...]
    o_ref[...] = (y + bg_s[:, :w]) * bg_s[:, w:]
    for q in range(n_seq):
        for d in range(2):
            state = st_s[q, d] if cached else st_s[q, d].T
            for h in range(HEADS):
                sl = slice(h * HEAD_DIM, (h + 1) * HEAD_DIM)
                sout_ref[q, d, h] = state[sl, sl]


def _rwkv(zr, lw, layer, consts, seq_len, s0_bd=None, prev=None):
    n = zr.shape[0]
    nb = n // seq_len
    cached = s0_bd is not None
    n_seq = 1 if cached else SEQ_PER_STEP
    rows = seq_len * n_seq
    args = [zr]
    in_specs = [pl.BlockSpec((rows, ZR_W), lambda b: (b, 0))]
    if cached:
        args.append(s0_bd)
        in_specs.append(pl.BlockSpec((1, None, 2, MIX_W, MIX_W), lambda b: (b, layer, 0, 0, 0)))
    layered = [lw["rwkv_mu_prev"], lw["rwkv_mu_next"], lw["rwkv_w0"], lw["rwkv_w2"], lw["rwkv_a0"], lw["rwkv_a2"],
               lw["rwkv_g2"], lw["rwkv_k_k"], lw["rwkv_k_a"], lw["rwkv_r_k"], lw["rwkv_gn_w"], lw["rwkv_gn_b"]]
    const = [consts["ones_bd"], consts["maskbd"]]
    args += layered + const
    in_specs += [_layer_spec(a, layer) for a in layered] + [_const_spec(a) for a in const]
    s_spec, s_shape, aliases = _state_out(nb, n_seq, layer, cached, prev, args, in_specs)
    return pl.pallas_call(
        functools.partial(_rwkv_kernel, seq_len=seq_len, n_seq=n_seq, cached=cached),
        grid=(nb // n_seq,),
        in_specs=in_specs,
        out_specs=[
            pl.BlockSpec((rows, MIX_W), lambda b: (b, 0)),
            s_spec,
        ],
        out_shape=[jax.ShapeDtypeStruct((n, MIX_W), F32), s_shape],
        input_output_aliases=aliases,
        scratch_shapes=[
            pltpu.VMEM((rows, MIX_W), F32),
            pltpu.VMEM((rows, MIX_W), F32),
            pltpu.VMEM((rows, MIX_W), F32),
            pltpu.VMEM((rows, 6 * MIX_W), F32),
            pltpu.VMEM((rows, 2 * MIX_W), F32),
            pltpu.VMEM((2, rows, RWKV_PRE_W), F32),
            pltpu.VMEM((2, rows // CHUNK * MIX_W, LANES), F32),
            pltpu.VMEM((rows, MIX_W), F32),
            pltpu.VMEM((n_seq, 2, MIX_W, MIX_W), F32),
        ],
        compiler_params=pltpu.CompilerParams(dimension_semantics=("arbitrary",),
                                             vmem_limit_bytes=VMEM_LIMIT),
        name="rwkv_lat" if cached else "rwkv_ctx",
    )(*args)


def _route(logits_t, bias):
    tm = logits_t.shape[1]
    neg = -jnp.inf
    sc = _sigmoid(logits_t)
    sc3 = sc.reshape(N_GROUPS, GROUP_SIZE, tm)
    sel = (sc + bias).reshape(N_GROUPS, GROUP_SIZE, tm)
    si = _iota(sel.shape, 1).astype(F32)
    m1 = jnp.max(sel, axis=1, keepdims=True)
    f1 = jnp.min(jnp.where(sel == m1, si, float(GROUP_SIZE)), axis=1, keepdims=True)
    m2 = jnp.max(jnp.where(si == f1, neg, sel), axis=1, keepdims=True)
    grp = m1 + m2
    gi = _iota(grp.shape, 0).astype(F32)
    gsel = jnp.zeros(grp.shape, F32)
    for _ in range(TOPK_GROUPS):
        mx = jnp.max(grp, axis=0, keepdims=True)
        fi = jnp.min(jnp.where(grp == mx, gi, float(N_GROUPS)), axis=0, keepdims=True)
        hit = gi == fi
        gsel = jnp.where(hit, 1.0, gsel)
        grp = jnp.where(hit, neg, grp)
    cur = jnp.where(gsel > 0.0, sel, neg)
    ei = (_iota(cur.shape, 0) * GROUP_SIZE + _iota(cur.shape, 1)).astype(F32)
    chosen = jnp.zeros(cur.shape, F32)
    ids, wts = [], []
    for _ in range(TOP_K):
        mx = jnp.max(jnp.max(cur, axis=0, keepdims=True), axis=1, keepdims=True)
        fi = jnp.min(jnp.min(jnp.where(cur == mx, ei, float(N_EXPERTS)), axis=0, keepdims=True),
                     axis=1, keepdims=True)
        hit = ei == fi
        chosen = jnp.where(hit, 1.0, chosen)
        cur = jnp.where(hit, neg, cur)
        ids.append(fi.reshape(1, tm))
        wts.append(jnp.sum(jnp.sum(jnp.where(hit, sc3, 0.0), axis=0, keepdims=True), axis=1, keepdims=True)
                   .reshape(1, tm))
    w = jnp.concatenate(wts, axis=0)
    w = w / jnp.sum(w, axis=0, keepdims=True) * ROUTE_SCALE
    return chosen.reshape(N_EXPERTS, tm), jnp.concatenate(ids, axis=0), w


def _pack_halves(x):
    half = x.shape[1] // 2
    bits = lax.bitcast_convert_type(x.astype(BF16).astype(F32), jnp.int32)
    lo = lax.shift_right_logical(bits[:, :half], jnp.int32(16))
    return jnp.bitwise_or(lo, jnp.bitwise_and(bits[:, half:], jnp.int32(-65536)))


def _unpack_halves(word):
    lo = lax.bitcast_convert_type(lax.shift_left(word, jnp.int32(16)), F32)
    hi = lax.bitcast_convert_type(jnp.bitwise_and(word, jnp.int32(-65536)), F32)
    return lo, hi


def _post_kernel(x_ref, om_ref, og_ref, or_ref, mod_ref, wo_ref, n2_ref, rt_ref, rb_ref, sgu_ref, sdn_ref,
                 tri_ref, cin_ref, x1_ref, h2_ref, eid_ref, rank_ref, ew_ref, cnt_ref, carry_s):
    @pl.when(pl.program_id(0) == 0)
    def _():
        carry_s[...] = cin_ref[...]

    m = mod_ref[0]
    g1 = m[:, 2 * D_MODEL:3 * D_MODEL]
    sh2 = m[:, 3 * D_MODEL:4 * D_MODEL]
    sc2 = m[:, 4 * D_MODEL:5 * D_MODEL]
    w_mla = MLA_HEADS * MLA_V
    mix = (_mm(om_ref[...], wo_ref[0:w_mla, :]) + _mm(og_ref[...], wo_ref[w_mla:w_mla + MIX_W, :])
           + _mm(or_ref[...], wo_ref[w_mla + MIX_W:, :]))
    x1 = x_ref[...] + g1 * mix
    h2 = _rms(x1, n2_ref[...]) * (1.0 + sc2) + sh2
    h2_ref[...] = _pack_halves(h2)
    g2 = m[:, 5 * D_MODEL:]
    x1_ref[...] = x1 + g2 * _mm(_swiglu_act(_mm(h2, sgu_ref[...])), sdn_ref[...])
    r_hi, r_lo = _split2(rt_ref[...])
    h_hi, h_lo = _split2(h2)
    logits_t = _mm_nt(r_hi, h_hi) + _mm_nt(r_hi, h_lo) + _mm_nt(r_lo, h_hi)
    chosen, ids, w = _route(logits_t, rb_ref[...])
    tm = chosen.shape[1]
    rank_et = (carry_s[:, 0:1] + _mm(chosen, tri_ref[...])).reshape(N_GROUPS, GROUP_SIZE, tm)
    ei = (_iota(rank_et.shape, 0) * GROUP_SIZE + _iota(rank_et.shape, 1)).astype(F32)
    ranks = []
    for k in range(TOP_K):
        pick = jnp.where(ei == ids[k:k + 1].reshape(1, 1, tm), rank_et, 0.0)
        ranks.append(jnp.sum(jnp.sum(pick, axis=0, keepdims=True), axis=1, keepdims=True).reshape(1, tm))
    eid_ref[...] = ids.astype(jnp.int32)
    rank_ref[...] = jnp.concatenate(ranks, axis=0).astype(jnp.int32)
    ew_ref[...] = jnp.concatenate([w, jnp.zeros((LANES - TOP_K, tm), F32)], axis=0).T
    total = carry_s[...] + jnp.sum(chosen, axis=1, keepdims=True)
    carry_s[...] = total
    cnt_ref[...] = total


def _post(x2d, om, og, orw, mods, lw, layer, consts, tm, seq_len, mod_base, counts_in):
    n = x2d.shape[0]
    tiles_per_seq = seq_len // tm if mod_base else 1

    def mod_idx(i):
        return (layer, mod_base + i // tiles_per_seq if mod_base else 0, 0, 0)

    row = lambda w: pl.BlockSpec((tm, w), lambda i: (i, 0))
    col = lambda h: pl.BlockSpec((h, tm), lambda i: (0, i))
    full = lambda a: _layer_spec(a, layer)
    tail = [lw["w_out"], lw["norm2"], lw["router_t"], lw["router_b"], lw["shared_w_gu"], lw["shared_w_down"]]
    tri = consts["tri_tokens"]
    return pl.pallas_call(
        _post_kernel,
        grid=(n // tm,),
        in_specs=[row(D_MODEL), row(MLA_HEADS * MLA_V), row(MIX_W), row(MIX_W),
                  pl.BlockSpec((None, 1, 1, 6 * D_MODEL), mod_idx)] + [full(a) for a in tail]
        + [_const_spec(tri), _const_spec(counts_in)],
        out_specs=[row(D_MODEL), row(D_MODEL // 2), col(TOP_K), col(TOP_K), row(LANES),
                   pl.BlockSpec((N_EXPERTS, LANES), lambda i: (0, 0))],
        out_shape=[
            jax.ShapeDtypeStruct((n, D_MODEL), F32),
            jax.ShapeDtypeStruct((n, D_MODEL // 2), jnp.int32),
            jax.ShapeDtypeStruct((TOP_K, n), jnp.int32),
            jax.ShapeDtypeStruct((TOP_K, n), jnp.int32),
            jax.ShapeDtypeStruct((n, LANES), F32),
            jax.ShapeDtypeStruct((N_EXPERTS, LANES), F32),
        ],
        scratch_shapes=[pltpu.VMEM((N_EXPERTS, LANES), F32)],
        compiler_params=pltpu.CompilerParams(dimension_semantics=("arbitrary",),
                                             vmem_limit_bytes=VMEM_LIMIT),
        name="post",
    )(x2d, om, og, orw, mods, *tail, tri, counts_in)


MOE_ROWS = 512
SC_ROWS = 128
SC_SUBCORES = 32


def _swiglu_act(gu):
    return _silu(gu[:, :D_EXPERT]) * gu[:, D_EXPERT:]


def _dispatch_plan(eid, rank, counts, n, rows):
    n_blocks = n * TOP_K // rows + N_EXPERTS
    cnt = counts[:, 0].astype(jnp.int32)
    blocks = (cnt + rows - 1) // rows
    block_end = jnp.cumsum(blocks)
    offset = (block_end - blocks) * rows
    experts = jnp.arange(N_EXPERTS, dtype=jnp.int32)
    dest = jnp.sum(jnp.where(eid[..., None] == experts, offset, 0), axis=-1) + rank
    block_ids = jnp.arange(n_blocks, dtype=jnp.int32)
    block_expert = jnp.minimum(jnp.sum((block_end[None, :] <= block_ids[:, None]).astype(jnp.int32), axis=1),
                               N_EXPERTS - 1)
    return dest, block_expert, block_end[-1:].astype(jnp.int32), n_blocks


def _sc_mesh():
    return plsc.VectorSubcoreMesh(core_axis_name="core", subcore_axis_name="subcore")


def _sc_dispatch(groups, dest, n_rows):
    w = groups[0].shape[1]
    dtype = groups[0].dtype

    @functools.partial(pl.kernel, out_type=jax.ShapeDtypeStruct((n_rows, w), dtype), mesh=_sc_mesh(),
                       scratch_types=[pltpu.VMEM((SC_ROWS, w), dtype), pltpu.VMEM((TOP_K, SC_ROWS), jnp.int32)])
    def kern(*refs):
        x_refs, d_hbm, o_hbm, xv, dv = refs[:len(groups)], *refs[len(groups):]
        sid = lax.axis_index("core") * (SC_SUBCORES // 2) + lax.axis_index("subcore")
        start = 0
        for x_hbm, x in zip(x_refs, groups):
            def chunk(c, x_hbm=x_hbm, start=start):
                r0 = pl.multiple_of(c * SC_ROWS, SC_ROWS)
                pltpu.sync_copy(x_hbm.at[pl.ds(r0, SC_ROWS)], xv)
                pltpu.sync_copy(d_hbm.at[:, pl.ds(start + r0, SC_ROWS)], dv)
                for k in range(TOP_K):
                    pltpu.sync_copy(xv, o_hbm.at[dv.at[k]])

            pl.loop(sid, x.shape[0] // SC_ROWS, step=SC_SUBCORES)(chunk)
            start += x.shape[0]

    return kern(*groups, dest)


def _sc_gather(y, idx):
    w = y.shape[1]
    n_chunks = idx.shape[0]

    @functools.partial(pl.kernel, out_type=jax.ShapeDtypeStruct((n_chunks * SC_ROWS, w), y.dtype), mesh=_sc_mesh(),
                       scratch_types=[pltpu.VMEM((SC_ROWS, w), y.dtype), pltpu.VMEM((1, SC_ROWS), jnp.int32)])
    def kern(y_hbm, i_hbm, o_hbm, ov, iv):
        sid = lax.axis_index("core") * (SC_SUBCORES // 2) + lax.axis_index("subcore")

        @pl.loop(sid, n_chunks, step=SC_SUBCORES)
        def _(c):
            pltpu.sync_copy(i_hbm.at[pl.ds(c, 1)], iv)
            pltpu.sync_copy(y_hbm.at[iv.at[0]], ov)
            pltpu.sync_copy(ov, o_hbm.at[pl.ds(pl.multiple_of(c * SC_ROWS, SC_ROWS), SC_ROWS)])

    return kern(y, idx)


def _moe_rows_kernel(be_ref, nu_ref, x_ref, wgu_ref, wdn_ref, y_ref):
    @pl.when(pl.program_id(0) < nu_ref[0])
    def _():
        half = D_MODEL // 2
        lo, hi = _unpack_halves(x_ref[...])
        gu = _mm(lo, wgu_ref[0:half, :]) + _mm(hi, wgu_ref[half:, :])
        y_ref[...] = _pack_halves(_mm(_swiglu_act(gu), wdn_ref[...]))


def _moe_rows(xs, block_expert, n_used, lw, layer, n_blocks, rows):
    half = D_MODEL // 2
    last = lambda b, be, nu: jnp.minimum(b, nu[0] - 1)
    return pl.pallas_call(
        _moe_rows_kernel,
        grid_spec=pltpu.PrefetchScalarGridSpec(
            num_scalar_prefetch=2,
            grid=(n_blocks,),
            in_specs=[pl.BlockSpec((rows, half), lambda b, be, nu: (last(b, be, nu), 0)),
                      pl.BlockSpec((None, None, D_MODEL, 2 * D_EXPERT),
                                   lambda b, be, nu: (layer, be[last(b, be, nu)], 0, 0)),
                      pl.BlockSpec((None, None, D_EXPERT, D_MODEL),
                                   lambda b, be, nu: (layer, be[last(b, be, nu)], 0, 0))],
            out_specs=pl.BlockSpec((rows, half), lambda b, be, nu: (last(b, be, nu), 0)),
        ),
        out_shape=jax.ShapeDtypeStruct(xs.shape, jnp.int32),
        compiler_params=pltpu.CompilerParams(dimension_semantics=("arbitrary",),
                                             vmem_limit_bytes=VMEM_LIMIT),
        name="moe_rows",
    )(block_expert, n_used, xs, lw["moe_w_gu"], lw["moe_w_down"])


def _moe_combine_kernel(yg_ref, ew_ref, x1_ref, mod_ref, nf_ref, o_ref, *, final):
    ew = ew_ref[...]
    acc_lo = acc_hi = None
    for k in range(TOP_K):
        lo, hi = _unpack_halves(yg_ref[k])
        wk = ew[:, k:k + 1]
        acc_lo = wk * lo if acc_lo is None else acc_lo + wk * lo
        acc_hi = wk * hi if acc_hi is None else acc_hi + wk * hi
    g2 = mod_ref[0][:, 5 * D_MODEL:]
    x2 = x1_ref[...] + g2 * jnp.concatenate([acc_lo, acc_hi], axis=1)
    if final:
        x2 = _rms(x2, nf_ref[...])
    o_ref[...] = x2


def _moe_combine(yg, row0, ew, x1, mods, layer, norm_f, tm, seq_len, mod_base, final):
    n = x1.shape[0]
    half = D_MODEL // 2
    tiles_per_seq = seq_len // tm if mod_base else 1
    tile0 = row0 // tm

    def mod_idx(i):
        return (layer, mod_base + i // tiles_per_seq if mod_base else 0, 0, 0)

    row = lambda w: pl.BlockSpec((tm, w), lambda i: (i, 0))
    return pl.pallas_call(
        functools.partial(_moe_combine_kernel, final=final),
        grid=(n // tm,),
        in_specs=[pl.BlockSpec((TOP_K, tm, half), lambda i: (0, tile0 + i, 0)), row(LANES), row(D_MODEL),
                  pl.BlockSpec((None, 1, 1, 6 * D_MODEL), mod_idx), _const_spec(norm_f)],
        out_specs=row(D_MODEL),
        out_shape=jax.ShapeDtypeStruct((n, D_MODEL), F32),
        compiler_params=pltpu.CompilerParams(dimension_semantics=("arbitrary",),
                                             vmem_limit_bytes=VMEM_LIMIT),
        name="moe_combine_final" if final else "moe_combine",
    )(yg, ew, x1, mods, norm_f)


def _moe_experts(groups, eid, rank, counts, lw, layer):
    group_sizes = [g.shape[0] for g in groups]
    n = sum(group_sizes)
    dest, block_expert, n_used, n_blocks = _dispatch_plan(eid, rank, counts, n, MOE_ROWS)
    xs = _sc_dispatch(groups, dest, n_blocks * MOE_ROWS)
    y = _moe_rows(xs, block_expert, n_used, lw, layer, n_blocks, MOE_ROWS)
    outs, start = [], 0
    for size in group_sizes:
        idx = dest[:, start:start + size].reshape(size * TOP_K // SC_ROWS, SC_ROWS)
        outs.append(_sc_gather(y, idx).reshape(TOP_K, size, D_MODEL // 2))
        start += size
    return outs


def _constants():
    idx = np.arange(MIX_W)
    same_head = (idx[:, None] // HEAD_DIM) == (idx[None, :] // HEAD_DIM)
    maskbd = jnp.asarray(same_head, BF16)
    eexp = np.zeros((LANES, 4 * MIX_W), np.float32)
    for blk in range(4):
        kind, d = divmod(blk, 2)
        for h in range(HEADS):
            src = kind * 2 * HEADS + d * HEADS + h
            eexp[src, blk * MIX_W + h * HEAD_DIM: blk * MIX_W + (h + 1) * HEAD_DIM] = 1.0
    tri = np.triu(np.ones((POST_TM, POST_TM), np.float32), 1)
    return {"maskbd": maskbd, "ones_bd": maskbd, "eexp": jnp.asarray(eexp, BF16), "tri_tokens": jnp.asarray(tri, BF16)}


def _rope_tables(n):
    rows = n // GRID_W
    row = jnp.repeat(jnp.arange(rows, dtype=F32), GRID_W)
    col = jnp.tile(jnp.arange(GRID_W, dtype=F32), rows)
    axis_dim = MLA_ROPE // 2
    inv = jnp.power(ROPE_BASE, -jnp.arange(0, axis_dim, 2, dtype=F32) / axis_dim)
    ang_r = row[:, None] * inv
    ang_c = col[:, None] * inv
    cr, sr, cc, sc = jnp.cos(ang_r), jnp.sin(ang_r), jnp.cos(ang_c), jnp.sin(ang_c)
    zeros = jnp.zeros((n, LANES - MLA_ROPE), F32)
    cos_t = jnp.concatenate([cr, cr, cc, cc, zeros], axis=1)
    sin_t = jnp.concatenate([-sr, sr, -sc, sc, zeros], axis=1)
    return cos_t, sin_t


def _stacked_weights(p):
    w_uq = p["mla_w_uq"].reshape(DEPTH, Q_LORA, MLA_HEADS, MLA_NOPE + MLA_ROPE)
    zq = jnp.zeros((DEPTH, Q_LORA, MLA_HEADS, QH_W - MLA_NOPE - MLA_ROPE), F32)
    w_uq_a = jnp.concatenate([w_uq, zq], axis=-1).reshape(DEPTH, Q_LORA, MLA_HEADS * QH_W).astype(BF16)
    w_uq_sw = jnp.concatenate([jnp.zeros((DEPTH, Q_LORA, MLA_HEADS, MLA_NOPE), F32),
                               w_uq[..., MLA_NOPE + _ROPE_SWAP], zq], axis=-1)
    w_uq_sw = w_uq_sw.reshape(DEPTH, Q_LORA, MLA_HEADS * QH_W).astype(BF16)

    def per_direction(w):
        half = jnp.zeros((DEPTH, 64, MIX_W), F32)
        return jnp.stack([jnp.concatenate([w[:, 0], half], axis=1),
                          jnp.concatenate([half, w[:, 1]], axis=1)], axis=1).astype(BF16)

    row = lambda v: v.reshape(DEPTH, 1, -1)
    pad_row = lambda v: jnp.pad(row(v), ((0, 0), (0, 0), (0, LANES - 2 * HEADS)))
    return {
        "norm1": row(p["norm1"]),
        "w_in": p["w_in"],
        "q_norm": row(p["mla_q_norm"]),
        "w_uq": w_uq_a, "w_uq_sw": w_uq_sw,
        "kv_norm": row(p["mla_kv_norm"]),
        "w_ukv": p["mla_w_ukv"].astype(BF16),
        "gdn_conv": p["gdn_conv"],
        "gdn_alog": pad_row(p["gdn_a_log"]),
        "gdn_dtb": pad_row(p["gdn_dt_bias"]),
        "gdn_norm": jnp.tile(row(p["gdn_norm"]), (1, 1, HEADS)),
        "rwkv_mu_prev": row(p["rwkv_mu_prev"]),
        "rwkv_mu_next": row(p["rwkv_mu_next"]),
        "rwkv_w0": p["rwkv_w0"],
        "rwkv_w2": per_direction(p["rwkv_w2"]),
        "rwkv_a0": p["rwkv_a0"],
        "rwkv_a2": per_direction(p["rwkv_a2"]),
        "rwkv_g2": p["rwkv_g2"].astype(BF16),
        "rwkv_k_k": row(p["rwkv_k_k"]),
        "rwkv_k_a": row(p["rwkv_k_a"]),
        "rwkv_r_k": row(p["rwkv_r_k"]),
        "rwkv_gn_w": row(p["rwkv_gn_w"]),
        "rwkv_gn_b": row(p["rwkv_gn_b"]),
        "w_out": p["w_out"].astype(BF16),
        "norm2": row(p["norm2"]),
        "router_t": jnp.swapaxes(p["moe_router"], 1, 2),
        "router_b": p["moe_bias"].reshape(DEPTH, N_EXPERTS, 1),
        "moe_w_gu": p["moe_w_gu"],
        "moe_w_down": p["moe_w_down"],
        "shared_w_gu": p["shared_w_gu"].astype(BF16),
        "shared_w_down": p["shared_w_down"].astype(BF16),
    }


def _embed_block_diag(s):
    b = s.shape[0]
    eye = jnp.eye(HEADS, dtype=s.dtype)
    out = jnp.einsum("bdhkv,hg->bdhkgv", s, eye)
    return out.reshape(b, 2, MIX_W, MIX_W)


def _layer_front(x2d, mods, lw, l, consts, seq_len, mod_base, cache, tm, tq, counts_in, prev=None):
    zm, zg, zab, zr = _inproj(x2d, mods, lw, l, tm, seq_len, mod_base)
    if cache is None:
        o_mla, ckv, kpe = _mla_ctx(zm, lw, l, seq_len, prev[:2])
        o_gdn, s_gdn = _gdn(zg, zab, lw, l, consts, seq_len, prev=prev[2])
        o_rwkv, s_rwkv = _rwkv(zr, lw, l, consts, seq_len, prev=prev[3])
        new = (ckv, kpe, s_gdn, s_rwkv)
    else:
        cckv, ckpe, rc, rs, sg, sr = cache
        o_mla = _mla_lat(zm, lw, l, seq_len, tq, (cckv, ckpe, rc, rs))
        o_gdn, _ = _gdn(zg, zab, lw, l, consts, seq_len, sg)
        o_rwkv, _ = _rwkv(zr, lw, l, consts, seq_len, sr)
        new = None
    routed = _post(x2d, o_mla, o_gdn, o_rwkv, mods, lw, l, consts, tm, seq_len, mod_base, counts_in)
    return routed, new


def kernel(x_prompt, x_sample, cache_mla_ckv, cache_mla_kpe, state_gdn, state_rwkv, c, c_ctx, ada_w, ada_b, norm1, w_in, mla_q_norm, mla_w_uq, mla_kv_norm, mla_w_ukv, gdn_conv, gdn_a_log, gdn_dt_bias, gdn_norm, rwkv_mu_prev, rwkv_mu_next, rwkv_w0, rwkv_w2, rwkv_a0, rwkv_a2, rwkv_g2, rwkv_k_k, rwkv_k_a, rwkv_r_k, rwkv_gn_w, rwkv_gn_b, w_out, norm2, moe_router, moe_bias, moe_w_gu, moe_w_down, shared_w_gu, shared_w_down, norm_f):
    p = dict(norm1=norm1, w_in=w_in, mla_q_norm=mla_q_norm, mla_w_uq=mla_w_uq, mla_kv_norm=mla_kv_norm,
             mla_w_ukv=mla_w_ukv, gdn_conv=gdn_conv, gdn_a_log=gdn_a_log, gdn_dt_bias=gdn_dt_bias,
             gdn_norm=gdn_norm, rwkv_mu_prev=rwkv_mu_prev, rwkv_mu_next=rwkv_mu_next, rwkv_w0=rwkv_w0,
             rwkv_w2=rwkv_w2, rwkv_a0=rwkv_a0, rwkv_a2=rwkv_a2, rwkv_g2=rwkv_g2, rwkv_k_k=rwkv_k_k,
             rwkv_k_a=rwkv_k_a, rwkv_r_k=rwkv_r_k, rwkv_gn_w=rwkv_gn_w, rwkv_gn_b=rwkv_gn_b, w_out=w_out,
             norm2=norm2, moe_router=moe_router, moe_bias=moe_bias, moe_w_gu=moe_w_gu, moe_w_down=moe_w_down,
             shared_w_gu=shared_w_gu, shared_w_down=shared_w_down)
    weights = _stacked_weights(p)
    consts = _constants()
    nf = norm_f.reshape(1, D_MODEL)
    b_ctx, t_ctx, _ = x_prompt.shape
    b_lat, t_lat, _ = x_sample.shape

    cvec8 = jnp.concatenate([c_ctx[None, :], c, jnp.zeros((8 - 1 - b_lat, D_MODEL), F32)], axis=0)
    mods = _adaln(cvec8, ada_w, ada_b)
    mods = mods.reshape(DEPTH, 8, 1, 6 * D_MODEL)

    rc, rs = _rope_tables(t_lat)
    ckpe = jnp.pad(cache_mla_kpe, ((0, 0), (0, 0), (0, 0), (0, LANES - MLA_ROPE)))
    cache = (cache_mla_ckv, ckpe, rc, rs, _embed_block_diag_layers(state_gdn),
             _embed_block_diag_layers(jnp.swapaxes(state_rwkv, -1, -2)))
    xp = x_prompt.reshape(b_ctx * t_ctx, D_MODEL)
    xs = x_sample.reshape(b_lat * t_lat, D_MODEL)
    n_ctx = xp.shape[0]
    tm = POST_TM
    state_shape = (b_ctx, DEPTH, 2, HEADS, HEAD_DIM, HEAD_DIM)
    ctx_outs = (jnp.zeros((b_ctx, DEPTH, t_ctx, KV_LORA), F32), jnp.zeros((b_ctx, DEPTH, t_ctx, MLA_ROPE), F32),
                jnp.zeros(state_shape, F32), jnp.zeros(state_shape, F32))
    for l in range(DEPTH):
        final = l == DEPTH - 1
        no_pairs = jnp.zeros((N_EXPERTS, LANES), F32)
        (x1c, hc, eidc, rankc, ewc, cnt_c), ctx_outs = _layer_front(xp, mods, weights, l, consts, t_ctx, 0, None,
                                                                    tm, t_ctx, no_pairs, ctx_outs)
        (x1s, hs, eids, ranks, ews, cnt), _ = _layer_front(xs, mods, weights, l, consts, t_lat, 1, cache,
                                                           tm, 256, cnt_c)
        yg_c, yg_s = _moe_experts([hc, hs], jnp.concatenate([eidc, eids], axis=1),
                                  jnp.concatenate([rankc, ranks], axis=1), cnt, weights, l)
        xp = _moe_combine(yg_c, 0, ewc, x1c, mods, l, nf, tm, t_ctx, 0, final)
        xs = _moe_combine(yg_s, 0, ews, x1s, mods, l, nf, tm, t_lat, 1, final)

    y_prompt = xp.reshape(b_ctx, t_ctx, D_MODEL)
    y_sample = xs.reshape(b_lat, t_lat, D_MODEL)
    new_ckv, new_kpe, new_gdn, new_rwkv = ctx_outs
    return (y_prompt, y_sample, new_ckv, new_kpe, new_gdn, new_rwkv)


def _embed_block_diag_layers(s):
    b = s.shape[0]
    return _embed_block_diag(s.reshape(b * DEPTH, 2, HEADS, HEAD_DIM, HEAD_DIM)).reshape(
        b, DEPTH, 2, MIX_W, MIX_W)
```

```python
import functools

import numpy as np
import jax
import jax.numpy as jnp
from jax import lax
from jax.experimental import pallas as pl
from jax.experimental.pallas import tpu as pltpu
from jax.experimental.pallas import tpu_sc as plsc

F32 = jnp.float32
BF16 = jnp.bfloat16

D_MODEL = 1024
DEPTH = 2
PAST_LEN = 512
GRID_W = 64
NORM_EPS = 1e-6

MLA_HEADS = 4
MLA_NOPE = 128
MLA_ROPE = 64
MLA_V = 128
Q_LORA = 384
KV_LORA = 256
ROPE_BASE = 10000.0
MLA_SCALE = (MLA_NOPE + MLA_ROPE) ** -0.5

HEADS = 4
HEAD_DIM = 64
MIX_W = HEADS * HEAD_DIM
GDN_CONV_CH = 3 * MIX_W
CHUNK = 64
RWKV_GN_EPS = 64e-5

N_EXPERTS = 64
TOP_K = 8
N_GROUPS = 8
GROUP_SIZE = N_EXPERTS // N_GROUPS
TOPK_GROUPS = 4
D_EXPERT = 256
ROUTE_SCALE = 2.5

P_MLA = Q_LORA + KV_LORA + MLA_ROPE
P_GDN = GDN_CONV_CH + MIX_W + 4 * HEADS
P_RWKV = 3 * MIX_W + 128 + 128 + 128

LANES = 128
ZM_W = Q_LORA + KV_LORA + 2 * LANES
ZG_W = GDN_CONV_CH + MIX_W
ZR_W = P_RWKV
QH_W = 2 * LANES
VMEM_LIMIT = 56 * 1024 * 1024
POST_TM = 512

_ROPE_SWAP = np.concatenate([np.arange(16, 32), np.arange(0, 16), np.arange(48, 64), np.arange(32, 48)])


def _sigmoid(x):
    return 1.0 / (1.0 + jnp.exp(-x))


def _silu(x):
    return x * _sigmoid(x)


def _softplus(x):
    return jnp.maximum(x, 0.0) + jnp.log(1.0 + jnp.exp(-jnp.abs(x)))


def _rms(x, g, eps=NORM_EPS):
    return x * lax.rsqrt(jnp.mean(x * x, axis=-1, keepdims=True) + eps) * g


def _mm(a, b):
    return jnp.dot(a.astype(BF16), b.astype(BF16), preferred_element_type=F32)


def _mm_nt(a, b):
    return lax.dot_general(a.astype(BF16), b.astype(BF16), (((1,), (1,)), ((), ())),
                           preferred_element_type=F32)


def _mm_tn(a, b):
    return lax.dot_general(a.astype(BF16), b.astype(BF16), (((0,), (0,)), ((), ())),
                           preferred_element_type=F32)


def _split3(x):
    p1 = x.astype(BF16)
    r1 = x - p1.astype(F32)
    p2 = r1.astype(BF16)
    r2 = r1 - p2.astype(F32)
    return p1, p2, r2.astype(BF16)


def _mm_sel_l(sel, x):
    p1, p2, p3 = _split3(x)
    return _mm(sel, p1) + _mm(sel, p2) + _mm(sel, p3)


def _mm_sel_r(x, sel):
    p1, p2, p3 = _split3(x)
    return _mm(p1, sel) + _mm(p2, sel) + _mm(p3, sel)


def _iota(shape, dim):
    return lax.broadcasted_iota(jnp.int32, shape, dim)


def _layer_spec(a, layer, **kw):
    nd = a.ndim - 1
    return pl.BlockSpec((None,) + a.shape[1:], lambda *_: (layer,) + (0,) * nd, **kw)


def _const_spec(a, **kw):
    return pl.BlockSpec(a.shape, lambda *_: (0,) * a.ndim, **kw)


def _bd(x, maskbd):
    xb = x.astype(BF16)
    return jnp.concatenate([xb] * HEADS, axis=0) * maskbd


def _chunk_masks(rev):
    row = _iota((CHUNK, MIX_W), 0)
    col = jnp.bitwise_and(_iota((CHUNK, MIX_W), 1), HEAD_DIM - 1)
    r2 = _iota((CHUNK, CHUNK), 0)
    c2 = _iota((CHUNK, CHUNK), 1)
    if rev:
        inc, strict, tri = row <= col, row < col, r2 <= c2
    else:
        inc, strict, tri = row >= col, row > col, r2 >= c2
    eye = jnp.where(row == col, 1.0, 0.0).astype(F32)
    return inc, strict, jnp.where(tri, 1.0, 0.0).astype(BF16), eye


def _split2(x):
    hi = x.astype(BF16)
    return hi, (x - hi.astype(F32)).astype(BF16)


def _mm_bd3(x, p, maskbd):
    n = x.shape[0]
    xh, xl = _split2(x)
    ph, pl_ = _split2(p)
    r = jnp.dot(jnp.concatenate([xh, xl], axis=0), _bd(ph, maskbd), preferred_element_type=F32)
    return r[:n] + r[n:] + jnp.dot(xh, _bd(pl_, maskbd), preferred_element_type=F32)


def _neumann_inverse(a_list, eye_list, maskbd):
    bs = [-a for a in a_list]
    ms = [eye + b for eye, b in zip(eye_list, bs)]
    ps = [_mm_bd3(b, b, maskbd) for b in bs]
    for _ in range(4):
        boths = [_mm_bd3(jnp.concatenate([m, p], axis=0), p, maskbd) for m, p in zip(ms, ps)]
        ms = [m + both[:CHUNK] for m, both in zip(ms, boths)]
        ps = [both[CHUNK:] for both in boths]
    return [m + _mm_bd3(m, p, maskbd) for m, p in zip(ms, ps)]


def _adaln_kernel(c_ref, w_ref, b_ref, o_ref):
    cv = c_ref[...]
    o_ref[0] = _mm(_silu(cv), w_ref[0]) + b_ref[0]


def _adaln(cvec8, ada_w, ada_b):
    tn = 768
    n_out = 6 * D_MODEL
    return pl.pallas_call(
        _adaln_kernel,
        grid=(DEPTH, n_out // tn),
        in_specs=[
            pl.BlockSpec((8, D_MODEL), lambda l, j: (0, 0)),
            pl.BlockSpec((1, D_MODEL, tn), lambda l, j: (l, 0, j)),
            pl.BlockSpec((1, 1, tn), lambda l, j: (l, 0, j)),
        ],
        out_specs=pl.BlockSpec((1, 8, tn), lambda l, j: (l, 0, j)),
        out_shape=jax.ShapeDtypeStruct((DEPTH, 8, n_out), F32),
        compiler_params=pltpu.CompilerParams(dimension_semantics=("arbitrary", "arbitrary"),
                                             vmem_limit_bytes=VMEM_LIMIT),
        name="adaln",
    )(cvec8, ada_w, ada_b.reshape(DEPTH, 1, n_out))


_KPE0 = Q_LORA + KV_LORA
_W_IN_MOVES = (
    [(0, 0, P_MLA)]
    + [(P_MLA + LANES - MLA_ROPE + 16 * j, _KPE0 + 16 * int(_ROPE_SWAP[16 * j] // 16), 16) for j in range(4)]
    + [(ZM_W, P_MLA, ZG_W), (ZM_W + ZG_W, P_MLA + ZG_W, 4 * HEADS), (ZM_W + ZG_W + LANES, P_MLA + P_GDN, P_RWKV)]
)
W_IN_PAD = ZM_W + ZG_W + LANES + ZR_W


def _inproj_kernel(x_ref, mod_ref, n1_ref, w_ref, zm_ref, zg_ref, zab_ref, zr_ref, w_s):
    @pl.when(pl.program_id(0) == 0)
    def _():
        w_s[...] = jnp.zeros(w_s.shape, BF16)
        for dst, src, width in _W_IN_MOVES:
            w_s[:, dst:dst + width] = w_ref[:, src:src + width].astype(BF16)

    m = mod_ref[0]
    sh = m[:, 0:D_MODEL]
    sc = m[:, D_MODEL:2 * D_MODEL]
    h = _rms(x_ref[...], n1_ref[...]) * (1.0 + sc) + sh
    z = _mm(h, w_s[...])
    o1 = ZM_W
    o2 = o1 + ZG_W
    o3 = o2 + LANES
    zm_ref[...] = z[:, :o1]
    zg_ref[...] = z[:, o1:o2]
    zab_ref[...] = z[:, o2:o3]
    zr_ref[...] = z[:, o3:]


def _inproj(x2d, mods, lw, layer, tm, seq_len, mod_base):
    n = x2d.shape[0]
    tiles_per_seq = seq_len // tm if mod_base else 1

    def mod_idx(i):
        return (layer, mod_base + i // tiles_per_seq if mod_base else 0, 0, 0)

    return pl.pallas_call(
        _inproj_kernel,
        grid=(n // tm,),
        in_specs=[
            pl.BlockSpec((tm, D_MODEL), lambda i: (i, 0)),
            pl.BlockSpec((None, 1, 1, 6 * D_MODEL), mod_idx),
            _layer_spec(lw["norm1"], layer),
            _layer_spec(lw["w_in"], layer, pipeline_mode=pl.Buffered(1)),
        ],
        out_specs=[
            pl.BlockSpec((tm, ZM_W), lambda i: (i, 0)),
            pl.BlockSpec((tm, ZG_W), lambda i: (i, 0)),
            pl.BlockSpec((tm, LANES), lambda i: (i, 0)),
            pl.BlockSpec((tm, ZR_W), lambda i: (i, 0)),
        ],
        out_shape=[
            jax.ShapeDtypeStruct((n, ZM_W), F32),
            jax.ShapeDtypeStruct((n, ZG_W), F32),
            jax.ShapeDtypeStruct((n, LANES), F32),
            jax.ShapeDtypeStruct((n, ZR_W), F32),
        ],
        scratch_shapes=[pltpu.VMEM((D_MODEL, W_IN_PAD), BF16)],
        compiler_params=pltpu.CompilerParams(dimension_semantics=("arbitrary",),
                                             vmem_limit_bytes=VMEM_LIMIT),
        name="inproj",
    )(x2d, mods, lw["norm1"], lw["w_in"])


MLA_SEQ_PER_STEP = 4


def _mla_ctx_kernel(zm_ref, qn_ref, wuq_ref, kvn_ref, wukv_ref, prev_ckv, prev_kpe, o_ref, ckv_ref, kpe_ref,
                    *, seq_len, n_seq):
    del prev_ckv, prev_kpe
    o_kpe = Q_LORA + KV_LORA
    zm = zm_ref[...]
    ckv = _rms(zm[:, Q_LORA:o_kpe], kvn_ref[...])
    kpe = zm[:, o_kpe:o_kpe + LANES]
    kv = _mm(ckv, wukv_ref[...])
    q = _mm(_rms(zm[:, :Q_LORA], qn_ref[...]), wuq_ref[...])
    kpe_b = kpe.astype(BF16)
    seqs = [slice(i * seq_len, (i + 1) * seq_len) for i in range(n_seq)]
    for i, r in enumerate(seqs):
        ckv_ref[i] = ckv[r]
        kpe_ref[i] = kpe[r, :MLA_ROPE]
    pairs = [(r, h * QH_W) for r in seqs for h in range(MLA_HEADS)]
    scores = [_mm_nt(q[r, c0:c0 + QH_W], jnp.concatenate([kv[r, c0:c0 + LANES].astype(BF16), kpe_b[r]], axis=1))
              * MLA_SCALE for r, c0 in pairs]
    exps = [jnp.exp(s - jnp.max(s, axis=-1, keepdims=True)) for s in scores]
    outs = [_mm(e, kv[r, c0 + LANES:c0 + QH_W]) / jnp.sum(e, axis=-1, keepdims=True)
            for e, (r, c0) in zip(exps, pairs)]
    for o, (r, c0) in zip(outs, pairs):
        h = c0 // QH_W
        o_ref[r, h * MLA_V:(h + 1) * MLA_V] = o


def _mla_ctx(zm, lw, layer, seq_len, prev):
    n = zm.shape[0]
    nb = n // seq_len
    n_seq = MLA_SEQ_PER_STEP
    rows = n_seq * seq_len
    weights = [lw["q_norm"], lw["w_uq"], lw["kv_norm"], lw["w_ukv"]]
    return pl.pallas_call(
        functools.partial(_mla_ctx_kernel, seq_len=seq_len, n_seq=n_seq),
        grid=(nb // n_seq,),
        in_specs=[pl.BlockSpec((rows, ZM_W), lambda b: (b, 0))] + [_layer_spec(a, layer) for a in weights]
        + [pl.BlockSpec(memory_space=pl.ANY)] * 2,
        out_specs=[
            pl.BlockSpec((rows, MLA_HEADS * MLA_V), lambda b: (b, 0)),
            pl.BlockSpec((n_seq, None, seq_len, KV_LORA), lambda b: (b, layer, 0, 0)),
            pl.BlockSpec((n_seq, None, seq_len, MLA_ROPE), lambda b: (b, layer, 0, 0)),
        ],
        out_shape=[
            jax.ShapeDtypeStruct((n, MLA_HEADS * MLA_V), F32),
            jax.ShapeDtypeStruct((nb, DEPTH, seq_len, KV_LORA), F32),
            jax.ShapeDtypeStruct((nb, DEPTH, seq_len, MLA_ROPE), F32),
        ],
        input_output_aliases={5: 1, 6: 2},
        compiler_params=pltpu.CompilerParams(dimension_semantics=("arbitrary",),
                                             vmem_limit_bytes=VMEM_LIMIT),
        name="mla_ctx",
    )(zm, *weights, *prev)


def _mla_lat_kernel(zm_ref, cckv_ref, ckpe_ref, rc_ref, rs_ref, qn_ref, wuq_ref, wuqs_ref, kvn_ref, wukv_ref,
                    o_ref, k_s, v_s, *, seq_len, tq, past):
    qi = pl.program_id(1)
    o_kpe = Q_LORA + KV_LORA

    @pl.when(qi == 0)
    def _():
        zm = zm_ref[...]
        ckv = _rms(zm[:, Q_LORA:o_kpe], kvn_ref[...])
        kpe = zm[:, o_kpe:o_kpe + LANES] * rc_ref[...] + zm[:, o_kpe + LANES:o_kpe + 2 * LANES] * rs_ref[...]
        kvc = _mm(cckv_ref[0], wukv_ref[...])
        kpc = ckpe_ref[0].astype(BF16)
        kv = _mm(ckv, wukv_ref[...])
        kpe = kpe.astype(BF16)
        for h in range(MLA_HEADS):
            c0 = h * QH_W
            k_s[0:past, c0:c0 + LANES] = kvc[:, c0:c0 + LANES].astype(BF16)
            k_s[0:past, c0 + LANES:c0 + QH_W] = kpc
            v_s[0:past, h * MLA_V:(h + 1) * MLA_V] = kvc[:, c0 + LANES:c0 + QH_W].astype(BF16)
            k_s[past:past + seq_len, c0:c0 + LANES] = kv[:, c0:c0 + LANES].astype(BF16)
            k_s[past:past + seq_len, c0 + LANES:c0 + QH_W] = kpe
            v_s[past:past + seq_len, h * MLA_V:(h + 1) * MLA_V] = kv[:, c0 + LANES:c0 + QH_W].astype(BF16)

    r0 = pl.multiple_of(qi * tq, tq)
    zq = zm_ref[pl.ds(r0, tq), :]
    cq = _rms(zq[:, :Q_LORA], qn_ref[...])
    q = _mm(cq, wuq_ref[...])
    qs = _mm(cq, wuqs_ref[...])
    qc = jnp.concatenate([jnp.ones((tq, LANES), F32), rc_ref[pl.ds(r0, tq), :]], axis=1)
    qsn = jnp.concatenate([jnp.zeros((tq, LANES), F32), rs_ref[pl.ds(r0, tq), :]], axis=1)
    for h in range(MLA_HEADS):
        c0 = h * QH_W
        qh = q[:, c0:c0 + QH_W] * qc + qs[:, c0:c0 + QH_W] * qsn
        s = _mm_nt(qh, k_s[:, c0:c0 + QH_W]) * MLA_SCALE
        e = jnp.exp(s - jnp.max(s, axis=-1, keepdims=True))
        den = jnp.sum(e, axis=-1, keepdims=True)
        o_ref[:, h * MLA_V:(h + 1) * MLA_V] = _mm(e, v_s[:, h * MLA_V:(h + 1) * MLA_V]) / den


def _mla_lat(zm, lw, layer, seq_len, tq, cache):
    n = zm.shape[0]
    nb = n // seq_len
    past = PAST_LEN
    tk = past + seq_len
    cckv, ckpe, rc, rs = cache
    weights = [lw["q_norm"], lw["w_uq"], lw["w_uq_sw"], lw["kv_norm"], lw["w_ukv"]]
    return pl.pallas_call(
        functools.partial(_mla_lat_kernel, seq_len=seq_len, tq=tq, past=past),
        grid=(nb, seq_len // tq),
        in_specs=[pl.BlockSpec((seq_len, ZM_W), lambda b, q: (b, 0)),
                  pl.BlockSpec((1, None, past, KV_LORA), lambda b, q: (b, layer, 0, 0)),
                  pl.BlockSpec((1, None, past, LANES), lambda b, q: (b, layer, 0, 0)),
                  _const_spec(rc), _const_spec(rs)] + [_layer_spec(a, layer) for a in weights],
        out_specs=pl.BlockSpec((tq, MLA_HEADS * MLA_V), lambda b, q: (b * (seq_len // tq) + q, 0)),
        out_shape=jax.ShapeDtypeStruct((n, MLA_HEADS * MLA_V), F32),
        scratch_shapes=[
            pltpu.VMEM((tk, MLA_HEADS * QH_W), BF16),
            pltpu.VMEM((tk, MLA_HEADS * MLA_V), BF16),
        ],
        compiler_params=pltpu.CompilerParams(dimension_semantics=("arbitrary", "arbitrary"),
                                             vmem_limit_bytes=VMEM_LIMIT),
        name="mla_lat",
    )(zm, cckv, ckpe, rc, rs, *weights)


SEQ_PER_STEP = 4
CHUNK_GROUP = 4
GDN_PRE_W = 5 * MIX_W


def _for_chunk_groups(n_chunks, fn):
    if n_chunks == CHUNK_GROUP:
        fn(0)
    else:
        def body(gi, carry):
            fn(gi * CHUNK_GROUP)
            return carry
        lax.fori_loop(0, n_chunks // CHUNK_GROUP, body, 0)


def _gdn_prepare(items, maskbd):
    n = range(len(items))
    qs, ks, vs, gs, betas, masks, revs = zip(*items)
    gcs = [_mm_sel_l(masks[i][2], gs[i]) for i in n]
    decays = []
    for i in n:
        inc, eye = masks[i][0], masks[i][3]
        gc_row = jnp.sum(eye * gcs[i], axis=0, keepdims=True)
        decays.append(jnp.where(inc, jnp.exp(jnp.where(inc, gcs[i] - gc_row, 0.0)), 0.0))
    kbs = [ks[i] * betas[i] for i in n]
    aqs = [_mm_nt(jnp.concatenate([kbs[i], qs[i]], axis=0), _bd(ks[i], maskbd)) for i in n]
    a_mats = [jnp.where(masks[i][1], aqs[i][:CHUNK] * decays[i], 0.0) for i in n]
    t_invs = _neumann_inverse(a_mats, [m[3] for m in masks], maskbd)
    egcs = [jnp.exp(gc) for gc in gcs]
    uws = [_mm(t_invs[i], jnp.concatenate([_bd(vs[i] * betas[i], maskbd), _bd(kbs[i] * egcs[i], maskbd)], axis=1))
           for i in n]
    out = []
    for i in n:
        g_last = gcs[i][0:1] if revs[i] else gcs[i][CHUNK - 1:CHUNK]
        pre = jnp.concatenate([uws[i], qs[i] * egcs[i], aqs[i][CHUNK:] * decays[i],
                               ks[i] * jnp.exp(g_last - gcs[i])], axis=1)
        out.append((pre, jnp.broadcast_to(jnp.exp(g_last), (8, MIX_W))))
    return out


def _gdn_step(items, maskbd):
    w = MIX_W
    n = range(len(items))
    pres, egls, states = zip(*items)
    wqs = [_mm(jnp.concatenate([pres[i][:, w:2 * w], pres[i][:, 2 * w:3 * w]], axis=0), states[i]) for i in n]
    v_news = [pres[i][:, :w] - wqs[i][:CHUNK] for i in n]
    outs = [wqs[i][CHUNK:] + _mm(pres[i][:, 3 * w:4 * w], _bd(v_news[i], maskbd)) for i in n]
    upds = [_mm_tn(pres[i][:, 4 * w:], v_news[i]) for i in n]
    mask_f = maskbd.astype(F32)
    return [(outs[i], states[i] * egls[i] + upds[i] * mask_f) for i in n]


def _gdn_kernel(*refs, seq_len, n_seq, cached):
    if cached:
        (zg_ref, zab_ref, s0_ref, conv_ref, alog_ref, dtb_ref, gn_ref, eexp_ref, ones_ref, maskbd_ref,
         o_ref, sout_ref, q_s, k_s, v_s, ge_s, pre_s, gl_s, oacc_s, st_s) = refs
    else:
        zg_ref, zab_ref, conv_ref, alog_ref, dtb_ref, gn_ref, eexp_ref, ones_ref, maskbd_ref = refs[:9]
        o_ref, sout_ref, q_s, k_s, v_s, ge_s, pre_s, gl_s, oacc_s, st_s = refs[-10:]
    t = seq_len * n_seq
    z = zg_ref[:, :GDN_CONV_CH]
    rowi = jnp.bitwise_and(_iota((t, 1), 0), seq_len - 1)
    zp = jnp.where(rowi == 0, 0.0, pltpu.roll(z, 1, 0))
    zn = jnp.where(rowi == seq_len - 1, 0.0, pltpu.roll(z, t - 1, 0))
    cw = conv_ref[...]
    qkv = _silu(zp * cw[0:1] + z * cw[1:2] + zn * cw[2:3])
    ones_bd = ones_ref[...]
    q = qkv[:, :MIX_W]
    k = qkv[:, MIX_W:2 * MIX_W]
    q_s[...] = q * lax.rsqrt(_mm_sel_r(q * q, ones_bd) + 1e-6) * (HEAD_DIM ** -0.5)
    k_s[...] = k * lax.rsqrt(_mm_sel_r(k * k, ones_bd) + 1e-6)
    v_s[...] = qkv[:, 2 * MIX_W:]
    ab = zab_ref[...]
    lane = _iota((t, LANES), 1)
    gb = jnp.where(lane < 2 * HEADS, -jnp.exp(alog_ref[...]) * _softplus(ab + dtb_ref[...]), _sigmoid(ab))
    ge_s[...] = _mm_sel_r(gb, eexp_ref[...])
    oacc_s[...] = jnp.zeros((t, MIX_W), F32)
    if cached:
        st_s[...] = s0_ref[...]
    else:
        st_s[...] = jnp.zeros((n_seq, 2, MIX_W, MIX_W), F32)
    maskbd = maskbd_ref[...]
    masks = (_chunk_masks(False), _chunk_masks(True))
    n_chunks = seq_len // CHUNK

    def prepare_group(c0):
        where, items = [], []
        for j in range(CHUNK_GROUP):
            c = c0 + j
            rows = pl.ds(pl.multiple_of(c * CHUNK, CHUNK), CHUNK)
            for d in range(2):
                where.append((d, c, rows))
                items.append((q_s[rows, :], k_s[rows, :], v_s[rows, :], ge_s[rows, d * MIX_W:(d + 1) * MIX_W],
                              ge_s[rows, (2 + d) * MIX_W:(3 + d) * MIX_W], masks[d], d == 1))
        for (d, c, rows), (pre, egl) in zip(where, _gdn_prepare(items, maskbd)):
            pre_s[d, rows, :] = pre
            gl_s[d, pl.ds(pl.multiple_of(c * 8, 8), 8), :] = egl

    _for_chunk_groups(n_chunks * n_seq, prepare_group)

    def body(i, carry):
        where, items = [], []
        for q in range(n_seq):
            for d in range(2):
                c = q * n_chunks + (i if d == 0 else n_chunks - 1 - i)
                rows = pl.ds(pl.multiple_of(c * CHUNK, CHUNK), CHUNK)
                where.append((q, d, rows))
                items.append((pre_s[d, rows, :], gl_s[d, pl.ds(pl.multiple_of(c * 8, 8), 8), :][0:1], st_s[q, d]))
        for (q, d, rows), (o, s_new) in zip(where, _gdn_step(items, maskbd)):
            oacc_s[rows, :] = oacc_s[rows, :] + o
            st_s[q, d] = s_new
        return carry

    lax.fori_loop(0, n_chunks, body, 0)
    o = oacc_s[...]
    ms = _mm_sel_r(o * o, ones_bd) * (1.0 / HEAD_DIM)
    gate = zg_ref[:, GDN_CONV_CH:]
    o_ref[...] = o * lax.rsqrt(ms + NORM_EPS) * gn_ref[...] * _silu(gate)
    for q in range(n_seq):
        for d in range(2):
            for h in range(HEADS):
                sl = slice(h * HEAD_DIM, (h + 1) * HEAD_DIM)
                sout_ref[q, d, h] = st_s[q, d, sl, sl]


def _state_out(nb, n_seq, layer, cached, prev, args, in_specs):
    if cached:
        return (pl.BlockSpec((n_seq, 2, HEADS, HEAD_DIM, HEAD_DIM), lambda b: (b, 0, 0, 0, 0)),
                jax.ShapeDtypeStruct((nb, 2, HEADS, HEAD_DIM, HEAD_DIM), F32), {})
    aliases = {len(args): 1}
    args.append(prev)
    in_specs.append(pl.BlockSpec(memory_space=pl.ANY))
    return (pl.BlockSpec((n_seq, None, 2, HEADS, HEAD_DIM, HEAD_DIM), lambda b: (b, layer, 0, 0, 0, 0)),
            jax.ShapeDtypeStruct((nb, DEPTH, 2, HEADS, HEAD_DIM, HEAD_DIM), F32), aliases)


def _gdn(zg, zab, lw, layer, consts, seq_len, s0_bd=None, prev=None):
    n = zg.shape[0]
    nb = n // seq_len
    cached = s0_bd is not None
    n_seq = 1 if cached else SEQ_PER_STEP
    rows = seq_len * n_seq
    args = [zg, zab]
    in_specs = [pl.BlockSpec((rows, ZG_W), lambda b: (b, 0)),
                pl.BlockSpec((rows, LANES), lambda b: (b, 0))]
    if cached:
        args.append(s0_bd)
        in_specs.append(pl.BlockSpec((1, None, 2, MIX_W, MIX_W), lambda b: (b, layer, 0, 0, 0)))
    layered = [lw["gdn_conv"], lw["gdn_alog"], lw["gdn_dtb"], lw["gdn_norm"]]
    const = [consts["eexp"], consts["ones_bd"], consts["maskbd"]]
    args += layered + const
    in_specs += [_layer_spec(a, layer) for a in layered] + [_const_spec(a) for a in const]
    s_spec, s_shape, aliases = _state_out(nb, n_seq, layer, cached, prev, args, in_specs)
    return pl.pallas_call(
        functools.partial(_gdn_kernel, seq_len=seq_len, n_seq=n_seq, cached=cached),
        grid=(nb // n_seq,),
        in_specs=in_specs,
        out_specs=[
            pl.BlockSpec((rows, MIX_W), lambda b: (b, 0)),
            s_spec,
        ],
        out_shape=[jax.ShapeDtypeStruct((n, MIX_W), F32), s_shape],
        input_output_aliases=aliases,
        scratch_shapes=[
            pltpu.VMEM((rows, MIX_W), F32),
            pltpu.VMEM((rows, MIX_W), F32),
            pltpu.VMEM((rows, MIX_W), F32),
            pltpu.VMEM((rows, 4 * MIX_W), F32),
            pltpu.VMEM((2, rows, GDN_PRE_W), F32),
            pltpu.VMEM((2, rows // CHUNK * 8, MIX_W), F32),
            pltpu.VMEM((rows, MIX_W), F32),
            pltpu.VMEM((n_seq, 2, MIX_W, MIX_W), F32),
        ],
        compiler_params=pltpu.CompilerParams(dimension_semantics=("arbitrary",),
                                             vmem_limit_bytes=VMEM_LIMIT),
        name="gdn_lat" if cached else "gdn_ctx",
    )(*args)


RWKV_PRE_W = 7 * MIX_W


def _rwkv_prepare(items, maskbd, eye_full):
    n = range(len(items))
    rs, kds, vs, kks, bs, lws, masks, revs = zip(*items)
    cums = [_mm_sel_l(masks[i][2], lws[i]) for i in n]
    einvs = [jnp.exp(-c) for c in cums]
    kts = [kks[i] * jnp.exp(cums[i] - lws[i]) for i in n]
    rts = [rs[i] * jnp.exp(cums[i]) for i in n]
    krs = [jnp.concatenate([kts[i], rts[i]], axis=0) for i in n]
    lb_alls = [_mm_nt(krs[i], _bd(bs[i] * einvs[i], maskbd)) for i in n]
    lk_alls = [_mm_nt(krs[i], _bd(kds[i] * einvs[i], maskbd)) for i in n]
    lbs = [jnp.where(masks[i][1], lb_alls[i][:CHUNK], 0.0) for i in n]
    t_invs = _neumann_inverse(lbs, [m[3] for m in masks], maskbd)
    lvs = [_mm(jnp.concatenate([jnp.where(masks[i][1], lk_alls[i][:CHUNK], 0.0),
                                jnp.where(masks[i][0], lk_alls[i][CHUNK:], 0.0)], axis=0), _bd(vs[i], maskbd))
           for i in n]
    tkps = [_mm(t_invs[i], jnp.concatenate([_bd(kts[i], maskbd), _bd(lvs[i][:CHUNK], maskbd)], axis=1)) for i in n]
    out = []
    for i in n:
        c_last = cums[i][0:1] if revs[i] else cums[i][CHUNK - 1:CHUNK]
        tail = jnp.exp(c_last - cums[i])
        rb = jnp.where(masks[i][0], lb_alls[i][CHUNK:], 0.0)
        pre = jnp.concatenate([tkps[i][:, :MIX_W], rts[i], tkps[i][:, MIX_W:], lvs[i][CHUNK:], rb,
                               kds[i] * tail, bs[i] * tail], axis=1)
        gcol = jnp.sum(eye_full * jnp.exp(c_last), axis=1, keepdims=True)
        out.append((pre, jnp.broadcast_to(gcol, (MIX_W, LANES))))
    return out


def _rwkv_step(items, maskbd):
    w = MIX_W
    n = range(len(items))
    pres, vs, gcols, states = zip(*items)
    prs = [_mm(jnp.concatenate([pres[i][:, :w], pres[i][:, w:2 * w]], axis=0), states[i]) for i in n]
    ps = [prs[i][:CHUNK] + pres[i][:, 2 * w:3 * w] for i in n]
    outs = [prs[i][CHUNK:] + pres[i][:, 3 * w:4 * w] - _mm(pres[i][:, 4 * w:5 * w], _bd(ps[i], maskbd)) for i in n]
    upds = [_mm_tn(jnp.concatenate([pres[i][:, 5 * w:6 * w], pres[i][:, 6 * w:]], axis=0),
                   jnp.concatenate([vs[i], -ps[i]], axis=0)) for i in n]
    mask_f = maskbd.astype(F32)
    return [(outs[i], states[i] * jnp.concatenate([gcols[i], gcols[i]], axis=1) + upds[i] * mask_f) for i in n]


def _rwkv_kernel(*refs, seq_len, n_seq, cached):
    if cached:
        (zr_ref, s0_ref, mup_ref, mun_ref, w0_ref, w2_ref, a0_ref, a2_ref, g2_ref, kk_ref, ka_ref, rk_ref,
         gnw_ref, gnb_ref, ones_ref, maskbd_ref, o_ref, sout_ref,
         r_s, v_s, kk_s, dir_s, bg_s, pre_s, gcol_s, oacc_s, st_s) = refs
    else:
        (zr_ref, mup_ref, mun_ref, w0_ref, w2_ref, a0_ref, a2_ref, g2_ref, kk_ref, ka_ref, rk_ref,
         gnw_ref, gnb_ref, ones_ref, maskbd_ref) = refs[:15]
        o_ref, sout_ref, r_s, v_s, kk_s, dir_s, bg_s, pre_s, gcol_s, oacc_s, st_s = refs[-11:]
    t = seq_len * n_seq
    z = zr_ref[...]
    rowi = jnp.bitwise_and(_iota((t, 1), 0), seq_len - 1)
    zp = jnp.where(rowi == 0, 0.0, pltpu.roll(z, 1, 0))
    zn = jnp.where(rowi == seq_len - 1, 0.0, pltpu.roll(z, t - 1, 0))
    z = z + mup_ref[...] * (zp - z) + mun_ref[...] * (zn - z)
    w = MIX_W
    r = z[:, :w]
    k = z[:, w:2 * w]
    v = z[:, 2 * w:3 * w]
    wd = jnp.tanh(z[:, 3 * w:3 * w + LANES])
    ad = z[:, 3 * w + LANES:3 * w + 2 * LANES]
    gd = _sigmoid(z[:, 3 * w + 2 * LANES:])
    ones_bd = ones_ref[...]
    kk = k * kk_ref[...]
    kk = kk * lax.rsqrt(_mm_sel_r(kk * kk, ones_bd) + 1e-6)
    r_s[...] = r
    v_s[...] = v
    kk_s[...] = kk
    bonus = jnp.zeros((t, w), F32)
    for d in range(2):
        w_log = -_softplus(-(w0_ref[d:d + 1] + _mm(wd, w2_ref[d]))) - 0.5
        a = _sigmoid(a0_ref[d:d + 1] + _mm(ad, a2_ref[d]))
        kd = k * (1.0 + (a - 1.0) * ka_ref[...])
        dir_s[:, (3 * d) * w:(3 * d + 1) * w] = -jnp.exp(w_log)
        dir_s[:, (3 * d + 1) * w:(3 * d + 2) * w] = kd
        dir_s[:, (3 * d + 2) * w:(3 * d + 3) * w] = kk * a
        bonus = bonus + _mm_sel_r(r * kd * rk_ref[...], ones_bd) * v
    bg_s[:, :w] = bonus
    bg_s[:, w:] = _mm(gd, g2_ref[...])
    oacc_s[...] = jnp.zeros((t, w), F32)
    if cached:
        st_s[...] = s0_ref[...]
    else:
        st_s[...] = jnp.zeros((n_seq, 2, w, w), F32)
    maskbd = maskbd_ref[...]
    masks = (_chunk_masks(False), _chunk_masks(True))
    eye_full = jnp.where(_iota((w, w), 0) == _iota((w, w), 1), 1.0, 0.0).astype(F32)
    n_chunks = seq_len // CHUNK

    def prepare_group(c0):
        where, items = [], []
        for j in range(CHUNK_GROUP):
            c = c0 + j
            rows = pl.ds(pl.multiple_of(c * CHUNK, CHUNK), CHUNK)
            for d in range(2):
                where.append((d, c, rows))
                items.append((r_s[rows, :], dir_s[rows, (3 * d + 1) * w:(3 * d + 2) * w], v_s[rows, :], kk_s[rows, :],
                              dir_s[rows, (3 * d + 2) * w:(3 * d + 3) * w], dir_s[rows, (3 * d) * w:(3 * d + 1) * w],
                              masks[d], d == 1))
        for (d, c, rows), (pre, gcol) in zip(where, _rwkv_prepare(items, maskbd, eye_full)):
            pre_s[d, rows, :] = pre
            gcol_s[d, pl.ds(pl.multiple_of(c * w, w), w), :] = gcol

    _for_chunk_groups(n_chunks * n_seq, prepare_group)

    def body(i, carry):
        where, items = [], []
        for q in range(n_seq):
            for d in range(2):
                c = q * n_chunks + (i if d == 0 else n_chunks - 1 - i)
                rows = pl.ds(pl.multiple_of(c * CHUNK, CHUNK), CHUNK)
                where.append((q, d, rows))
                items.append((pre_s[d, rows, :], v_s[rows, :], gcol_s[d, pl.ds(pl.multiple_of(c * w, w), w), :],
                              st_s[q, d]))
        for (q, d, rows), (o, z_new) in zip(where, _rwkv_step(items, maskbd)):
            oacc_s[rows, :] = oacc_s[rows, :] + o
            st_s[q, d] = z_new
        return carry

    lax.fori_loop(0, n_chunks, body, 0)
    o = oacc_s[...]
    inv_n = 1.0 / HEAD_DIM
    mu = _mm_sel_r(o, ones_bd) * inv_n
    oc = o - mu
    var = _mm_sel_r(oc * oc, ones_bd) * inv_n
    y = oc * lax.rsqrt(var + RWKV_GN_EPS) * gnw_ref[...] + gnb_ref[...]
    o_ref[...] = (y + bg_s[:, :w]) * bg_s[:, w:]
    for q in range(n_seq):
        for d in range(2):
            state = st_s[q, d] if cached else st_s[q, d].T
            for h in range(HEADS):
                sl = slice(h * HEAD_DIM, (h + 1) * HEAD_DIM)
                sout_ref[q, d, h] = state[sl, sl]


def _rwkv(zr, lw, layer, consts, seq_len, s0_bd=None, prev=None):
    n = zr.shape[0]
    nb = n // seq_len
    cached = s0_bd is not None
    n_seq = 1 if cached else SEQ_PER_STEP
    rows = seq_len * n_seq
    args = [zr]
    in_specs = [pl.BlockSpec((rows, ZR_W), lambda b: (b, 0))]
    if cached:
        args.append(s0_bd)
        in_specs.append(pl.BlockSpec((1, None, 2, MIX_W, MIX_W), lambda b: (b, layer, 0, 0, 0)))
    layered = [lw["rwkv_mu_prev"], lw["rwkv_mu_next"], lw["rwkv_w0"], lw["rwkv_w2"], lw["rwkv_a0"], lw["rwkv_a2"],
               lw["rwkv_g2"], lw["rwkv_k_k"], lw["rwkv_k_a"], lw["rwkv_r_k"], lw["rwkv_gn_w"], lw["rwkv_gn_b"]]
    const = [consts["ones_bd"], consts["maskbd"]]
    args += layered + const
    in_specs += [_layer_spec(a, layer) for a in layered] + [_const_spec(a) for a in const]
    s_spec, s_shape, aliases = _state_out(nb, n_seq, layer, cached, prev, args, in_specs)
    return pl.pallas_call(
        functools.partial(_rwkv_kernel, seq_len=seq_len, n_seq=n_seq, cached=cached),
        grid=(nb // n_seq,),
        in_specs=in_specs,
        out_specs=[
            pl.BlockSpec((rows, MIX_W), lambda b: (b, 0)),
            s_spec,
        ],
        out_shape=[jax.ShapeDtypeStruct((n, MIX_W), F32), s_shape],
        input_output_aliases=aliases,
        scratch_shapes=[
            pltpu.VMEM((rows, MIX_W), F32),
            pltpu.VMEM((rows, MIX_W), F32),
            pltpu.VMEM((rows, MIX_W), F32),
            pltpu.VMEM((rows, 6 * MIX_W), F32),
            pltpu.VMEM((rows, 2 * MIX_W), F32),
            pltpu.VMEM((2, rows, RWKV_PRE_W), F32),
            pltpu.VMEM((2, rows // CHUNK * MIX_W, LANES), F32),
            pltpu.VMEM((rows, MIX_W), F32),
            pltpu.VMEM((n_seq, 2, MIX_W, MIX_W), F32),
        ],
        compiler_params=pltpu.CompilerParams(dimension_semantics=("arbitrary",),
                                             vmem_limit_bytes=VMEM_LIMIT),
        name="rwkv_lat" if cached else "rwkv_ctx",
    )(*args)


def _route(logits_t, bias):
    tm = logits_t.shape[1]
    neg = -jnp.inf
    sc = _sigmoid(logits_t)
    sc3 = sc.reshape(N_GROUPS, GROUP_SIZE, tm)
    sel = (sc + bias).reshape(N_GROUPS, GROUP_SIZE, tm)
    si = _iota(sel.shape, 1).astype(F32)
    m1 = jnp.max(sel, axis=1, keepdims=True)
    f1 = jnp.min(jnp.where(sel == m1, si, float(GROUP_SIZE)), axis=1, keepdims=True)
    m2 = jnp.max(jnp.where(si == f1, neg, sel), axis=1, keepdims=True)
    grp = m1 + m2
    gi = _iota(grp.shape, 0).astype(F32)
    gsel = jnp.zeros(grp.shape, F32)
    for _ in range(TOPK_GROUPS):
        mx = jnp.max(grp, axis=0, keepdims=True)
        fi = jnp.min(jnp.where(grp == mx, gi, float(N_GROUPS)), axis=0, keepdims=True)
        hit = gi == fi
        gsel = jnp.where(hit, 1.0, gsel)
        grp = jnp.where(hit, neg, grp)
    cur = jnp.where(gsel > 0.0, sel, neg)
    ei = (_iota(cur.shape, 0) * GROUP_SIZE + _iota(cur.shape, 1)).astype(F32)
    chosen = jnp.zeros(cur.shape, F32)
    ids, wts = [], []
    for _ in range(TOP_K):
        mx = jnp.max(jnp.max(cur, axis=0, keepdims=True), axis=1, keepdims=True)
        fi = jnp.min(jnp.min(jnp.where(cur == mx, ei, float(N_EXPERTS)), axis=0, keepdims=True),
                     axis=1, keepdims=True)
        hit = ei == fi
        chosen = jnp.where(hit, 1.0, chosen)
        cur = jnp.where(hit, neg, cur)
        ids.append(fi.reshape(1, tm))
        wts.append(jnp.sum(jnp.sum(jnp.where(hit, sc3, 0.0), axis=0, keepdims=True), axis=1, keepdims=True)
                   .reshape(1, tm))
    w = jnp.concatenate(wts, axis=0)
    w = w / jnp.sum(w, axis=0, keepdims=True) * ROUTE_SCALE
    return chosen.reshape(N_EXPERTS, tm), jnp.concatenate(ids, axis=0), w


def _pack_halves(x):
    half = x.shape[1] // 2
    bits = lax.bitcast_convert_type(x.astype(BF16).astype(F32), jnp.int32)
    lo = lax.shift_right_logical(bits[:, :half], jnp.int32(16))
    return jnp.bitwise_or(lo, jnp.bitwise_and(bits[:, half:], jnp.int32(-65536)))


def _unpack_halves(word):
    lo = lax.bitcast_convert_type(lax.shift_left(word, jnp.int32(16)), F32)
    hi = lax.bitcast_convert_type(jnp.bitwise_and(word, jnp.int32(-65536)), F32)
    return lo, hi


def _post_kernel(x_ref, om_ref, og_ref, or_ref, mod_ref, wo_ref, n2_ref, rt_ref, rb_ref, sgu_ref, sdn_ref,
                 tri_ref, cin_ref, x1_ref, h2_ref, eid_ref, rank_ref, ew_ref, cnt_ref, carry_s):
    @pl.when(pl.program_id(0) == 0)
    def _():
        carry_s[...] = cin_ref[...]

    m = mod_ref[0]
    g1 = m[:, 2 * D_MODEL:3 * D_MODEL]
    sh2 = m[:, 3 * D_MODEL:4 * D_MODEL]
    sc2 = m[:, 4 * D_MODEL:5 * D_MODEL]
    w_mla = MLA_HEADS * MLA_V
    mix = (_mm(om_ref[...], wo_ref[0:w_mla, :]) + _mm(og_ref[...], wo_ref[w_mla:w_mla + MIX_W, :])
           + _mm(or_ref[...], wo_ref[w_mla + MIX_W:, :]))
    x1 = x_ref[...] + g1 * mix
    h2 = _rms(x1, n2_ref[...]) * (1.0 + sc2) + sh2
    h2_ref[...] = _pack_halves(h2)
    g2 = m[:, 5 * D_MODEL:]
    x1_ref[...] = x1 + g2 * _mm(_swiglu_act(_mm(h2, sgu_ref[...])), sdn_ref[...])
    r_hi, r_lo = _split2(rt_ref[...])
    h_hi, h_lo = _split2(h2)
    logits_t = _mm_nt(r_hi, h_hi) + _mm_nt(r_hi, h_lo) + _mm_nt(r_lo, h_hi)
    chosen, ids, w = _route(logits_t, rb_ref[...])
    tm = chosen.shape[1]
    rank_et = (carry_s[:, 0:1] + _mm(chosen, tri_ref[...])).reshape(N_GROUPS, GROUP_SIZE, tm)
    ei = (_iota(rank_et.shape, 0) * GROUP_SIZE + _iota(rank_et.shape, 1)).astype(F32)
    ranks = []
    for k in range(TOP_K):
        pick = jnp.where(ei == ids[k:k + 1].reshape(1, 1, tm), rank_et, 0.0)
        ranks.append(jnp.sum(jnp.sum(pick, axis=0, keepdims=True), axis=1, keepdims=True).reshape(1, tm))
    eid_ref[...] = ids.astype(jnp.int32)
    rank_ref[...] = jnp.concatenate(ranks, axis=0).astype(jnp.int32)
    ew_ref[...] = jnp.concatenate([w, jnp.zeros((LANES - TOP_K, tm), F32)], axis=0).T
    total = carry_s[...] + jnp.sum(chosen, axis=1, keepdims=True)
    carry_s[...] = total
    cnt_ref[...] = total


def _post(x2d, om, og, orw, mods, lw, layer, consts, tm, seq_len, mod_base, counts_in):
    n = x2d.shape[0]
    tiles_per_seq = seq_len // tm if mod_base else 1

    def mod_idx(i):
        return (layer, mod_base + i // tiles_per_seq if mod_base else 0, 0, 0)

    row = lambda w: pl.BlockSpec((tm, w), lambda i: (i, 0))
    col = lambda h: pl.BlockSpec((h, tm), lambda i: (0, i))
    full = lambda a: _layer_spec(a, layer)
    tail = [lw["w_out"], lw["norm2"], lw["router_t"], lw["router_b"], lw["shared_w_gu"], lw["shared_w_down"]]
    tri = consts["tri_tokens"]
    return pl.pallas_call(
        _post_kernel,
        grid=(n // tm,),
        in_specs=[row(D_MODEL), row(MLA_HEADS * MLA_V), row(MIX_W), row(MIX_W),
                  pl.BlockSpec((None, 1, 1, 6 * D_MODEL), mod_idx)] + [full(a) for a in tail]
        + [_const_spec(tri), _const_spec(counts_in)],
        out_specs=[row(D_MODEL), row(D_MODEL // 2), col(TOP_K), col(TOP_K), row(LANES),
                   pl.BlockSpec((N_EXPERTS, LANES), lambda i: (0, 0))],
        out_shape=[
            jax.ShapeDtypeStruct((n, D_MODEL), F32),
            jax.ShapeDtypeStruct((n, D_MODEL // 2), jnp.int32),
            jax.ShapeDtypeStruct((TOP_K, n), jnp.int32),
            jax.ShapeDtypeStruct((TOP_K, n), jnp.int32),
            jax.ShapeDtypeStruct((n, LANES), F32),
            jax.ShapeDtypeStruct((N_EXPERTS, LANES), F32),
        ],
        scratch_shapes=[pltpu.VMEM((N_EXPERTS, LANES), F32)],
        compiler_params=pltpu.CompilerParams(dimension_semantics=("arbitrary",),
                                             vmem_limit_bytes=VMEM_LIMIT),
        name="post",
    )(x2d, om, og, orw, mods, *tail, tri, counts_in)


MOE_ROWS = 512
SC_ROWS = 128
SC_SUBCORES = 32


def _swiglu_act(gu):
    return _silu(gu[:, :D_EXPERT]) * gu[:, D_EXPERT:]


def _dispatch_plan(eid, rank, counts, n, rows):
    n_blocks = n * TOP_K // rows + N_EXPERTS
    cnt = counts[:, 0].astype(jnp.int32)
    blocks = (cnt + rows - 1) // rows
    block_end = jnp.cumsum(blocks)
    offset = (block_end - blocks) * rows
    experts = jnp.arange(N_EXPERTS, dtype=jnp.int32)
    dest = jnp.sum(jnp.where(eid[..., None] == experts, offset, 0), axis=-1) + rank
    block_ids = jnp.arange(n_blocks, dtype=jnp.int32)
    block_expert = jnp.minimum(jnp.sum((block_end[None, :] <= block_ids[:, None]).astype(jnp.int32), axis=1),
                               N_EXPERTS - 1)
    n_used = block_end[-1:]
    prev_expert = jnp.concatenate([jnp.full((1,), -1, jnp.int32), block_expert[:-1]])
    first = ((block_ids < n_used[0]) & (block_expert != prev_expert)).astype(jnp.int32)
    slot = jnp.bitwise_and(jnp.cumsum(first) - 1, 1)
    owner_or_none = jnp.where(blocks > 0, experts, N_EXPERTS)
    next_owner = jnp.concatenate([lax.cummin(owner_or_none[::-1])[::-1][1:], jnp.full((1,), N_EXPERTS, jnp.int32)])
    nxt = jnp.sum(jnp.where(block_expert[:, None] == experts, next_owner, 0), axis=1)
    return dest, (block_expert, first, slot.astype(jnp.int32), nxt.astype(jnp.int32), n_used.astype(jnp.int32)), n_blocks


def _sc_mesh():
    return plsc.VectorSubcoreMesh(core_axis_name="core", subcore_axis_name="subcore")


def _sc_dispatch(groups, dest, n_rows):
    w = groups[0].shape[1]
    dtype = groups[0].dtype

    @functools.partial(pl.kernel, out_type=jax.ShapeDtypeStruct((n_rows, w), dtype), mesh=_sc_mesh(),
                       scratch_types=[pltpu.VMEM((SC_ROWS, w), dtype), pltpu.VMEM((TOP_K, SC_ROWS), jnp.int32)])
    def kern(*refs):
        x_refs, d_hbm, o_hbm, xv, dv = refs[:len(groups)], *refs[len(groups):]
        sid = lax.axis_index("core") * (SC_SUBCORES // 2) + lax.axis_index("subcore")
        start = 0
        for x_hbm, x in zip(x_refs, groups):
            def chunk(c, x_hbm=x_hbm, start=start):
                r0 = pl.multiple_of(c * SC_ROWS, SC_ROWS)
                pltpu.sync_copy(x_hbm.at[pl.ds(r0, SC_ROWS)], xv)
                pltpu.sync_copy(d_hbm.at[:, pl.ds(start + r0, SC_ROWS)], dv)
                for k in range(TOP_K):
                    pltpu.sync_copy(xv, o_hbm.at[dv.at[k]])

            pl.loop(sid, x.shape[0] // SC_ROWS, step=SC_SUBCORES)(chunk)
            start += x.shape[0]

    return kern(*groups, dest)


def _sc_gather(y, idx):
    w = y.shape[1]
    n_chunks = idx.shape[0]

    @functools.partial(pl.kernel, out_type=jax.ShapeDtypeStruct((n_chunks * SC_ROWS, w), y.dtype), mesh=_sc_mesh(),
                       scratch_types=[pltpu.VMEM((SC_ROWS, w), y.dtype), pltpu.VMEM((1, SC_ROWS), jnp.int32)])
    def kern(y_hbm, i_hbm, o_hbm, ov, iv):
        sid = lax.axis_index("core") * (SC_SUBCORES // 2) + lax.axis_index("subcore")

        @pl.loop(sid, n_chunks, step=SC_SUBCORES)
        def _(c):
            pltpu.sync_copy(i_hbm.at[pl.ds(c, 1)], iv)
            pltpu.sync_copy(y_hbm.at[iv.at[0]], ov)
            pltpu.sync_copy(ov, o_hbm.at[pl.ds(pl.multiple_of(c * SC_ROWS, SC_ROWS), SC_ROWS)])

    return kern(y, idx)


def _moe_rows_kernel(be_ref, first_ref, slot_ref, nxt_ref, nu_ref, x_ref, wgu_hbm, wdn_hbm, y_ref,
                     wgu_f, wdn_f, wgu_b, wdn_b, sem, *, layer):
    b = pl.program_id(0)

    def weight_copies(expert, slot):
        return (pltpu.make_async_copy(wgu_hbm.at[layer, expert], wgu_f.at[slot], sem.at[slot, 0]),
                pltpu.make_async_copy(wdn_hbm.at[layer, expert], wdn_f.at[slot], sem.at[slot, 1]))

    @pl.when(b == 0)
    def _():
        for copy in weight_copies(be_ref[0], 0):
            copy.start()

    @pl.when(first_ref[b] == 1)
    def _():
        slot = slot_ref[b]
        for copy in weight_copies(be_ref[b], slot):
            copy.wait()

        @pl.when(nxt_ref[b] < N_EXPERTS)
        def _():
            for copy in weight_copies(nxt_ref[b], 1 - slot):
                copy.start()

        wgu_b[...] = wgu_f[slot].astype(BF16)
        wdn_b[...] = wdn_f[slot].astype(BF16)

    @pl.when(b < nu_ref[0])
    def _():
        half = D_MODEL // 2
        lo, hi = _unpack_halves(x_ref[...])
        gu = _mm(lo, wgu_b[0:half, :]) + _mm(hi, wgu_b[half:, :])
        y_ref[...] = _pack_halves(_mm(_swiglu_act(gu), wdn_b[...]))


def _moe_rows(xs, schedule, lw, layer, n_blocks, rows):
    half = D_MODEL // 2
    last = lambda b, be, first, slot, nxt, nu: jnp.minimum(b, nu[0] - 1)
    row_spec = pl.BlockSpec((rows, half), lambda *a: (last(*a), 0))
    return pl.pallas_call(
        functools.partial(_moe_rows_kernel, layer=layer),
        grid_spec=pltpu.PrefetchScalarGridSpec(
            num_scalar_prefetch=5,
            grid=(n_blocks,),
            in_specs=[row_spec, pl.BlockSpec(memory_space=pl.ANY), pl.BlockSpec(memory_space=pl.ANY)],
            out_specs=row_spec,
            scratch_shapes=[
                pltpu.VMEM((2, D_MODEL, 2 * D_EXPERT), F32),
                pltpu.VMEM((2, D_EXPERT, D_MODEL), F32),
                pltpu.VMEM((D_MODEL, 2 * D_EXPERT), BF16),
                pltpu.VMEM((D_EXPERT, D_MODEL), BF16),
                pltpu.SemaphoreType.DMA((2, 2)),
            ],
        ),
        out_shape=jax.ShapeDtypeStruct(xs.shape, jnp.int32),
        compiler_params=pltpu.CompilerParams(dimension_semantics=("arbitrary",),
                                             vmem_limit_bytes=VMEM_LIMIT),
        name="moe_rows",
    )(*schedule, xs, lw["moe_w_gu"], lw["moe_w_down"])


def _moe_combine_kernel(yg_ref, ew_ref, x1_ref, mod_ref, nf_ref, o_ref, *, final):
    ew = ew_ref[...]
    acc_lo = acc_hi = None
    for k in range(TOP_K):
        lo, hi = _unpack_halves(yg_ref[k])
        wk = ew[:, k:k + 1]
        acc_lo = wk * lo if acc_lo is None else acc_lo + wk * lo
        acc_hi = wk * hi if acc_hi is None else acc_hi + wk * hi
    g2 = mod_ref[0][:, 5 * D_MODEL:]
    x2 = x1_ref[...] + g2 * jnp.concatenate([acc_lo, acc_hi], axis=1)
    if final:
        x2 = _rms(x2, nf_ref[...])
    o_ref[...] = x2


def _moe_combine(yg, row0, ew, x1, mods, layer, norm_f, tm, seq_len, mod_base, final):
    n = x1.shape[0]
    half = D_MODEL // 2
    tiles_per_seq = seq_len // tm if mod_base else 1
    tile0 = row0 // tm

    def mod_idx(i):
        return (layer, mod_base + i // tiles_per_seq if mod_base else 0, 0, 0)

    row = lambda w: pl.BlockSpec((tm, w), lambda i: (i, 0))
    return pl.pallas_call(
        functools.partial(_moe_combine_kernel, final=final),
        grid=(n // tm,),
        in_specs=[pl.BlockSpec((TOP_K, tm, half), lambda i: (0, tile0 + i, 0)), row(LANES), row(D_MODEL),
                  pl.BlockSpec((None, 1, 1, 6 * D_MODEL), mod_idx), _const_spec(norm_f)],
        out_specs=row(D_MODEL),
        out_shape=jax.ShapeDtypeStruct((n, D_MODEL), F32),
        compiler_params=pltpu.CompilerParams(dimension_semantics=("arbitrary",),
                                             vmem_limit_bytes=VMEM_LIMIT),
        name="moe_combine_final" if final else "moe_combine",
    )(yg, ew, x1, mods, norm_f)


def _moe_experts(groups, eid, rank, counts, lw, layer):
    group_sizes = [g.shape[0] for g in groups]
    n = sum(group_sizes)
    dest, schedule, n_blocks = _dispatch_plan(eid, rank, counts, n, MOE_ROWS)
    xs = _sc_dispatch(groups, dest, n_blocks * MOE_ROWS)
    y = _moe_rows(xs, schedule, lw, layer, n_blocks, MOE_ROWS)
    outs, start = [], 0
    for size in group_sizes:
        idx = dest[:, start:start + size].reshape(size * TOP_K // SC_ROWS, SC_ROWS)
        outs.append(_sc_gather(y, idx).reshape(TOP_K, size, D_MODEL // 2))
        start += size
    return outs


def _constants():
    idx = np.arange(MIX_W)
    same_head = (idx[:, None] // HEAD_DIM) == (idx[None, :] // HEAD_DIM)
    maskbd = jnp.asarray(same_head, BF16)
    eexp = np.zeros((LANES, 4 * MIX_W), np.float32)
    for blk in range(4):
        kind, d = divmod(blk, 2)
        for h in range(HEADS):
            src = kind * 2 * HEADS + d * HEADS + h
            eexp[src, blk * MIX_W + h * HEAD_DIM: blk * MIX_W + (h + 1) * HEAD_DIM] = 1.0
    tri = np.triu(np.ones((POST_TM, POST_TM), np.float32), 1)
    return {"maskbd": maskbd, "ones_bd": maskbd, "eexp": jnp.asarray(eexp, BF16), "tri_tokens": jnp.asarray(tri, BF16)}


def _rope_tables(n):
    rows = n // GRID_W
    row = jnp.repeat(jnp.arange(rows, dtype=F32), GRID_W)
    col = jnp.tile(jnp.arange(GRID_W, dtype=F32), rows)
    axis_dim = MLA_ROPE // 2
    inv = jnp.power(ROPE_BASE, -jnp.arange(0, axis_dim, 2, dtype=F32) / axis_dim)
    ang_r = row[:, None] * inv
    ang_c = col[:, None] * inv
    cr, sr, cc, sc = jnp.cos(ang_r), jnp.sin(ang_r), jnp.cos(ang_c), jnp.sin(ang_c)
    zeros = jnp.zeros((n, LANES - MLA_ROPE), F32)
    cos_t = jnp.concatenate([cr, cr, cc, cc, zeros], axis=1)
    sin_t = jnp.concatenate([-sr, sr, -sc, sc, zeros], axis=1)
    return cos_t, sin_t


def _stacked_weights(p):
    w_uq = p["mla_w_uq"].reshape(DEPTH, Q_LORA, MLA_HEADS, MLA_NOPE + MLA_ROPE)
    zq = jnp.zeros((DEPTH, Q_LORA, MLA_HEADS, QH_W - MLA_NOPE - MLA_ROPE), F32)
    w_uq_a = jnp.concatenate([w_uq, zq], axis=-1).reshape(DEPTH, Q_LORA, MLA_HEADS * QH_W).astype(BF16)
    w_uq_sw = jnp.concatenate([jnp.zeros((DEPTH, Q_LORA, MLA_HEADS, MLA_NOPE), F32),
                               w_uq[..., MLA_NOPE + _ROPE_SWAP], zq], axis=-1)
    w_uq_sw = w_uq_sw.reshape(DEPTH, Q_LORA, MLA_HEADS * QH_W).astype(BF16)

    def per_direction(w):
        half = jnp.zeros((DEPTH, 64, MIX_W), F32)
        return jnp.stack([jnp.concatenate([w[:, 0], half], axis=1),
                          jnp.concatenate([half, w[:, 1]], axis=1)], axis=1).astype(BF16)

    row = lambda v: v.reshape(DEPTH, 1, -1)
    pad_row = lambda v: jnp.pad(row(v), ((0, 0), (0, 0), (0, LANES - 2 * HEADS)))
    return {
        "norm1": row(p["norm1"]),
        "w_in": p["w_in"],
        "q_norm": row(p["mla_q_norm"]),
        "w_uq": w_uq_a, "w_uq_sw": w_uq_sw,
        "kv_norm": row(p["mla_kv_norm"]),
        "w_ukv": p["mla_w_ukv"].astype(BF16),
        "gdn_conv": p["gdn_conv"],
        "gdn_alog": pad_row(p["gdn_a_log"]),
        "gdn_dtb": pad_row(p["gdn_dt_bias"]),
        "gdn_norm": jnp.tile(row(p["gdn_norm"]), (1, 1, HEADS)),
        "rwkv_mu_prev": row(p["rwkv_mu_prev"]),
        "rwkv_mu_next": row(p["rwkv_mu_next"]),
        "rwkv_w0": p["rwkv_w0"],
        "rwkv_w2": per_direction(p["rwkv_w2"]),
        "rwkv_a0": p["rwkv_a0"],
        "rwkv_a2": per_direction(p["rwkv_a2"]),
        "rwkv_g2": p["rwkv_g2"].astype(BF16),
        "rwkv_k_k": row(p["rwkv_k_k"]),
        "rwkv_k_a": row(p["rwkv_k_a"]),
        "rwkv_r_k": row(p["rwkv_r_k"]),
        "rwkv_gn_w": row(p["rwkv_gn_w"]),
        "rwkv_gn_b": row(p["rwkv_gn_b"]),
        "w_out": p["w_out"].astype(BF16),
        "norm2": row(p["norm2"]),
        "router_t": jnp.swapaxes(p["moe_router"], 1, 2),
        "router_b": p["moe_bias"].reshape(DEPTH, N_EXPERTS, 1),
        "moe_w_gu": p["moe_w_gu"],
        "moe_w_down": p["moe_w_down"],
        "shared_w_gu": p["shared_w_gu"].astype(BF16),
        "shared_w_down": p["shared_w_down"].astype(BF16),
    }


def _embed_block_diag(s):
    b = s.shape[0]
    eye = jnp.eye(HEADS, dtype=s.dtype)
    out = jnp.einsum("bdhkv,hg->bdhkgv", s, eye)
    return out.reshape(b, 2, MIX_W, MIX_W)


def _layer_front(x2d, mods, lw, l, consts, seq_len, mod_base, cache, tm, tq, counts_in, prev=None):
    zm, zg, zab, zr = _inproj(x2d, mods, lw, l, tm, seq_len, mod_base)
    if cache is None:
        o_mla, ckv, kpe = _mla_ctx(zm, lw, l, seq_len, prev[:2])
        o_gdn, s_gdn = _gdn(zg, zab, lw, l, consts, seq_len, prev=prev[2])
        o_rwkv, s_rwkv = _rwkv(zr, lw, l, consts, seq_len, prev=prev[3])
        new = (ckv, kpe, s_gdn, s_rwkv)
    else:
        cckv, ckpe, rc, rs, sg, sr = cache
        o_mla = _mla_lat(zm, lw, l, seq_len, tq, (cckv, ckpe, rc, rs))
        o_gdn, _ = _gdn(zg, zab, lw, l, consts, seq_len, sg)
        o_rwkv, _ = _rwkv(zr, lw, l, consts, seq_len, sr)
        new = None
    routed = _post(x2d, o_mla, o_gdn, o_rwkv, mods, lw, l, consts, tm, seq_len, mod_base, counts_in)
    return routed, new


def kernel(x_prompt, x_sample, cache_mla_ckv, cache_mla_kpe, state_gdn, state_rwkv, c, c_ctx, ada_w, ada_b, norm1, w_in, mla_q_norm, mla_w_uq, mla_kv_norm, mla_w_ukv, gdn_conv, gdn_a_log, gdn_dt_bias, gdn_norm, rwkv_mu_prev, rwkv_mu_next, rwkv_w0, rwkv_w2, rwkv_a0, rwkv_a2, rwkv_g2, rwkv_k_k, rwkv_k_a, rwkv_r_k, rwkv_gn_w, rwkv_gn_b, w_out, norm2, moe_router, moe_bias, moe_w_gu, moe_w_down, shared_w_gu, shared_w_down, norm_f):
    p = dict(norm1=norm1, w_in=w_in, mla_q_norm=mla_q_norm, mla_w_uq=mla_w_uq, mla_kv_norm=mla_kv_norm,
             mla_w_ukv=mla_w_ukv, gdn_conv=gdn_conv, gdn_a_log=gdn_a_log, gdn_dt_bias=gdn_dt_bias,
             gdn_norm=gdn_norm, rwkv_mu_prev=rwkv_mu_prev, rwkv_mu_next=rwkv_mu_next, rwkv_w0=rwkv_w0,
             rwkv_w2=rwkv_w2, rwkv_a0=rwkv_a0, rwkv_a2=rwkv_a2, rwkv_g2=rwkv_g2, rwkv_k_k=rwkv_k_k,
             rwkv_k_a=rwkv_k_a, rwkv_r_k=rwkv_r_k, rwkv_gn_w=rwkv_gn_w, rwkv_gn_b=rwkv_gn_b, w_out=w_out,
             norm2=norm2, moe_router=moe_router, moe_bias=moe_bias, moe_w_gu=moe_w_gu, moe_w_down=moe_w_down,
             shared_w_gu=shared_w_gu, shared_w_down=shared_w_down)
    weights = _stacked_weights(p)
    consts = _constants()
    nf = norm_f.reshape(1, D_MODEL)
    b_ctx, t_ctx, _ = x_prompt.shape
    b_lat, t_lat, _ = x_sample.shape

    cvec8 = jnp.concatenate([c_ctx[None, :], c, jnp.zeros((8 - 1 - b_lat, D_MODEL), F32)], axis=0)
    mods = _adaln(cvec8, ada_w, ada_b)
    mods = mods.reshape(DEPTH, 8, 1, 6 * D_MODEL)

    rc, rs = _rope_tables(t_lat)
    ckpe = jnp.pad(cache_mla_kpe, ((0, 0), (0, 0), (0, 0), (0, LANES - MLA_ROPE)))
    cache = (cache_mla_ckv, ckpe, rc, rs, _embed_block_diag_layers(state_gdn),
             _embed_block_diag_layers(jnp.swapaxes(state_rwkv, -1, -2)))
    xp = x_prompt.reshape(b_ctx * t_ctx, D_MODEL)
    xs = x_sample.reshape(b_lat * t_lat, D_MODEL)
    n_ctx = xp.shape[0]
    tm = POST_TM
    state_shape = (b_ctx, DEPTH, 2, HEADS, HEAD_DIM, HEAD_DIM)
    ctx_outs = (jnp.zeros((b_ctx, DEPTH, t_ctx, KV_LORA), F32), jnp.zeros((b_ctx, DEPTH, t_ctx, MLA_ROPE), F32),
                jnp.zeros(state_shape, F32), jnp.zeros(state_shape, F32))
    for l in range(DEPTH):
        final = l == DEPTH - 1
        no_pairs = jnp.zeros((N_EXPERTS, LANES), F32)
        (x1c, hc, eidc, rankc, ewc, cnt_c), ctx_outs = _layer_front(xp, mods, weights, l, consts, t_ctx, 0, None,
                                                                    tm, t_ctx, no_pairs, ctx_outs)
        (x1s, hs, eids, ranks, ews, cnt), _ = _layer_front(xs, mods, weights, l, consts, t_lat, 1, cache,
                                                           tm, 256, cnt_c)
        yg_c, yg_s = _moe_experts([hc, hs], jnp.concatenate([eidc, eids], axis=1),
                                  jnp.concatenate([rankc, ranks], axis=1), cnt, weights, l)
        xp = _moe_combine(yg_c, 0, ewc, x1c, mods, l, nf, tm, t_ctx, 0, final)
        xs = _moe_combine(yg_s, 0, ews, x1s, mods, l, nf, tm, t_lat, 1, final)

    y_prompt = xp.reshape(b_ctx, t_ctx, D_MODEL)
    y_sample = xs.reshape(b_lat, t_lat, D_MODEL)
    new_ckv, new_kpe, new_gdn, new_rwkv = ctx_outs
    return (y_prompt, y_sample, new_ckv, new_kpe, new_gdn, new_rwkv)


def _embed_block_diag_layers(s):
    b = s.shape[0]
    return _embed_block_diag(s.reshape(b * DEPTH, 2, HEADS, HEAD_DIM, HEAD_DIM)).reshape(
        b, DEPTH, 2, MIX_W, MIX_W)
```

```python
import functools

import numpy as np
import jax
import jax.numpy as jnp
from jax import lax
from jax.experimental import pallas as pl
from jax.experimental.pallas import tpu as pltpu
from jax.experimental.pallas import tpu_sc as plsc

F32 = jnp.float32
BF16 = jnp.bfloat16

D_MODEL = 1024
DEPTH = 2
PAST_LEN = 512
GRID_W = 64
NORM_EPS = 1e-6

MLA_HEADS = 4
MLA_NOPE = 128
MLA_ROPE = 64
MLA_V = 128
Q_LORA = 384
KV_LORA = 256
ROPE_BASE = 10000.0
MLA_SCALE = (MLA_NOPE + MLA_ROPE) ** -0.5

HEADS = 4
HEAD_DIM = 64
MIX_W = HEADS * HEAD_DIM
GDN_CONV_CH = 3 * MIX_W
CHUNK = 64
RWKV_GN_EPS = 64e-5

N_EXPERTS = 64
TOP_K = 8
N_GROUPS = 8
GROUP_SIZE = N_EXPERTS // N_GROUPS
TOPK_GROUPS = 4
D_EXPERT = 256
ROUTE_SCALE = 2.5

P_MLA = Q_LORA + KV_LORA + MLA_ROPE
P_GDN = GDN_CONV_CH + MIX_W + 4 * HEADS
P_RWKV = 3 * MIX_W + 128 + 128 + 128

LANES = 128
ZM_W = Q_LORA + KV_LORA + 2 * LANES
ZG_W = GDN_CONV_CH + MIX_W
ZR_W = P_RWKV
QH_W = 2 * LANES
VMEM_LIMIT = 56 * 1024 * 1024
POST_TM = 512

_ROPE_SWAP = np.concatenate([np.arange(16, 32), np.arange(0, 16), np.arange(48, 64), np.arange(32, 48)])


def _sigmoid(x):
    return 1.0 / (1.0 + jnp.exp(-x))


def _silu(x):
    return x * _sigmoid(x)


def _softplus(x):
    return jnp.maximum(x, 0.0) + jnp.log(1.0 + jnp.exp(-jnp.abs(x)))


def _rms(x, g, eps=NORM_EPS):
    return x * lax.rsqrt(jnp.mean(x * x, axis=-1, keepdims=True) + eps) * g


def _mm(a, b):
    return jnp.dot(a.astype(BF16), b.astype(BF16), preferred_element_type=F32)


def _mm_nt(a, b):
    return lax.dot_general(a.astype(BF16), b.astype(BF16), (((1,), (1,)), ((), ())),
                           preferred_element_type=F32)


def _mm_tn(a, b):
    return lax.dot_general(a.astype(BF16), b.astype(BF16), (((0,), (0,)), ((), ())),
                           preferred_element_type=F32)


def _split3(x):
    p1 = x.astype(BF16)
    r1 = x - p1.astype(F32)
    p2 = r1.astype(BF16)
    r2 = r1 - p2.astype(F32)
    return p1, p2, r2.astype(BF16)


def _mm_sel_l(sel, x):
    p1, p2, p3 = _split3(x)
    return _mm(sel, p1) + _mm(sel, p2) + _mm(sel, p3)


def _mm_sel_r(x, sel):
    p1, p2, p3 = _split3(x)
    return _mm(p1, sel) + _mm(p2, sel) + _mm(p3, sel)


def _iota(shape, dim):
    return lax.broadcasted_iota(jnp.int32, shape, dim)


def _layer_spec(a, layer, **kw):
    nd = a.ndim - 1
    return pl.BlockSpec((None,) + a.shape[1:], lambda *_: (layer,) + (0,) * nd, **kw)


def _const_spec(a, **kw):
    return pl.BlockSpec(a.shape, lambda *_: (0,) * a.ndim, **kw)


def _bd(x, maskbd):
    xb = x.astype(BF16)
    return jnp.concatenate([xb] * HEADS, axis=0) * maskbd


def _chunk_masks(rev):
    row = _iota((CHUNK, MIX_W), 0)
    col = jnp.bitwise_and(_iota((CHUNK, MIX_W), 1), HEAD_DIM - 1)
    r2 = _iota((CHUNK, CHUNK), 0)
    c2 = _iota((CHUNK, CHUNK), 1)
    if rev:
        inc, strict, tri = row <= col, row < col, r2 <= c2
    else:
        inc, strict, tri = row >= col, row > col, r2 >= c2
    eye = jnp.where(row == col, 1.0, 0.0).astype(F32)
    return inc, strict, jnp.where(tri, 1.0, 0.0).astype(BF16), eye


def _split2(x):
    hi = x.astype(BF16)
    return hi, (x - hi.astype(F32)).astype(BF16)


def _mm_bd3(x, p, maskbd):
    n = x.shape[0]
    xh, xl = _split2(x)
    ph, pl_ = _split2(p)
    r = jnp.dot(jnp.concatenate([xh, xl], axis=0), _bd(ph, maskbd), preferred_element_type=F32)
    return r[:n] + r[n:] + jnp.dot(xh, _bd(pl_, maskbd), preferred_element_type=F32)


def _neumann_inverse(a_list, eye_list, maskbd):
    bs = [-a for a in a_list]
    ms = [eye + b for eye, b in zip(eye_list, bs)]
    ps = [_mm_bd3(b, b, maskbd) for b in bs]
    for _ in range(4):
        boths = [_mm_bd3(jnp.concatenate([m, p], axis=0), p, maskbd) for m, p in zip(ms, ps)]
        ms = [m + both[:CHUNK] for m, both in zip(ms, boths)]
        ps = [both[CHUNK:] for both in boths]
    return [m + _mm_bd3(m, p, maskbd) for m, p in zip(ms, ps)]


def _adaln_kernel(c_ref, w_ref, b_ref, o_ref):
    cv = c_ref[...]
    o_ref[0] = _mm(_silu(cv), w_ref[0]) + b_ref[0]


def _adaln(cvec8, ada_w, ada_b):
    tn = 768
    n_out = 6 * D_MODEL
    return pl.pallas_call(
        _adaln_kernel,
        grid=(DEPTH, n_out // tn),
        in_specs=[
            pl.BlockSpec((8, D_MODEL), lambda l, j: (0, 0)),
            pl.BlockSpec((1, D_MODEL, tn), lambda l, j: (l, 0, j)),
            pl.BlockSpec((1, 1, tn), lambda l, j: (l, 0, j)),
        ],
        out_specs=pl.BlockSpec((1, 8, tn), lambda l, j: (l, 0, j)),
        out_shape=jax.ShapeDtypeStruct((DEPTH, 8, n_out), F32),
        compiler_params=pltpu.CompilerParams(dimension_semantics=("arbitrary", "arbitrary"),
                                             vmem_limit_bytes=VMEM_LIMIT),
        name="adaln",
    )(cvec8, ada_w, ada_b.reshape(DEPTH, 1, n_out))


_KPE0 = Q_LORA + KV_LORA
_W_IN_MOVES = (
    [(0, 0, P_MLA)]
    + [(P_MLA + LANES - MLA_ROPE + 16 * j, _KPE0 + 16 * int(_ROPE_SWAP[16 * j] // 16), 16) for j in range(4)]
    + [(ZM_W, P_MLA, ZG_W), (ZM_W + ZG_W, P_MLA + ZG_W, 4 * HEADS), (ZM_W + ZG_W + LANES, P_MLA + P_GDN, P_RWKV)]
)
W_IN_PAD = ZM_W + ZG_W + LANES + ZR_W


def _inproj_kernel(x_ref, mod_ref, n1_ref, w_ref, zm_ref, zg_ref, zab_ref, zr_ref, w_s):
    @pl.when(pl.program_id(0) == 0)
    def _():
        w_s[...] = jnp.zeros(w_s.shape, BF16)
        for dst, src, width in _W_IN_MOVES:
            w_s[:, dst:dst + width] = w_ref[:, src:src + width].astype(BF16)

    m = mod_ref[0]
    sh = m[:, 0:D_MODEL]
    sc = m[:, D_MODEL:2 * D_MODEL]
    h = _rms(x_ref[...], n1_ref[...]) * (1.0 + sc) + sh
    z = _mm(h, w_s[...])
    o1 = ZM_W
    o2 = o1 + ZG_W
    o3 = o2 + LANES
    zm_ref[...] = z[:, :o1]
    zg_ref[...] = z[:, o1:o2]
    zab_ref[...] = z[:, o2:o3]
    zr_ref[...] = z[:, o3:]


def _inproj(x2d, mods, lw, layer, tm, seq_len, mod_base):
    n = x2d.shape[0]
    tiles_per_seq = seq_len // tm if mod_base else 1

    def mod_idx(i):
        return (layer, mod_base + i // tiles_per_seq if mod_base else 0, 0, 0)

    return pl.pallas_call(
        _inproj_kernel,
        grid=(n // tm,),
        in_specs=[
            pl.BlockSpec((tm, D_MODEL), lambda i: (i, 0)),
            pl.BlockSpec((None, 1, 1, 6 * D_MODEL), mod_idx),
            _layer_spec(lw["norm1"], layer),
            _layer_spec(lw["w_in"], layer, pipeline_mode=pl.Buffered(1)),
        ],
        out_specs=[
            pl.BlockSpec((tm, ZM_W), lambda i: (i, 0)),
            pl.BlockSpec((tm, ZG_W), lambda i: (i, 0)),
            pl.BlockSpec((tm, LANES), lambda i: (i, 0)),
            pl.BlockSpec((tm, ZR_W), lambda i: (i, 0)),
        ],
        out_shape=[
            jax.ShapeDtypeStruct((n, ZM_W), F32),
            jax.ShapeDtypeStruct((n, ZG_W), F32),
            jax.ShapeDtypeStruct((n, LANES), F32),
            jax.ShapeDtypeStruct((n, ZR_W), F32),
        ],
        scratch_shapes=[pltpu.VMEM((D_MODEL, W_IN_PAD), BF16)],
        compiler_params=pltpu.CompilerParams(dimension_semantics=("arbitrary",),
                                             vmem_limit_bytes=VMEM_LIMIT),
        name="inproj",
    )(x2d, mods, lw["norm1"], lw["w_in"])


MLA_SEQ_PER_STEP = 4
Q_HEAD_W = MLA_NOPE + MLA_ROPE


def _arrange_w_uq(w_ref, wq_s, wqs_s=None):
    wq_s[...] = jnp.zeros(wq_s.shape, BF16)
    if wqs_s is not None:
        wqs_s[...] = jnp.zeros(wqs_s.shape, BF16)
    for h in range(MLA_HEADS):
        src, dst = h * Q_HEAD_W, h * QH_W
        wq_s[:, dst:dst + Q_HEAD_W] = w_ref[:, src:src + Q_HEAD_W].astype(BF16)
        if wqs_s is not None:
            for j in range(4):
                s0 = src + MLA_NOPE + 16 * int(_ROPE_SWAP[16 * j] // 16)
                wqs_s[:, dst + MLA_NOPE + 16 * j:dst + MLA_NOPE + 16 * (j + 1)] = w_ref[:, s0:s0 + 16].astype(BF16)


def _mla_ctx_kernel(zm_ref, qn_ref, wuq_ref, kvn_ref, wukv_ref, prev_ckv, prev_kpe, o_ref, ckv_ref, kpe_ref, wq_s,
                    *, seq_len, n_seq):
    del prev_ckv, prev_kpe

    @pl.when(pl.program_id(0) == 0)
    def _():
        _arrange_w_uq(wuq_ref, wq_s)

    o_kpe = Q_LORA + KV_LORA
    zm = zm_ref[...]
    ckv = _rms(zm[:, Q_LORA:o_kpe], kvn_ref[...])
    kpe = zm[:, o_kpe:o_kpe + LANES]
    kv = _mm(ckv, wukv_ref[...])
    q = _mm(_rms(zm[:, :Q_LORA], qn_ref[...]), wq_s[...])
    kpe_b = kpe.astype(BF16)
    seqs = [slice(i * seq_len, (i + 1) * seq_len) for i in range(n_seq)]
    for i, r in enumerate(seqs):
        ckv_ref[i] = ckv[r]
        kpe_ref[i] = kpe[r, :MLA_ROPE]
    pairs = [(r, h * QH_W) for r in seqs for h in range(MLA_HEADS)]
    scores = [_mm_nt(q[r, c0:c0 + QH_W], jnp.concatenate([kv[r, c0:c0 + LANES].astype(BF16), kpe_b[r]], axis=1))
              * MLA_SCALE for r, c0 in pairs]
    exps = [jnp.exp(s - jnp.max(s, axis=-1, keepdims=True)) for s in scores]
    outs = [_mm(e, kv[r, c0 + LANES:c0 + QH_W]) / jnp.sum(e, axis=-1, keepdims=True)
            for e, (r, c0) in zip(exps, pairs)]
    for o, (r, c0) in zip(outs, pairs):
        h = c0 // QH_W
        o_ref[r, h * MLA_V:(h + 1) * MLA_V] = o


def _mla_ctx(zm, lw, layer, seq_len, prev):
    n = zm.shape[0]
    nb = n // seq_len
    n_seq = MLA_SEQ_PER_STEP
    rows = n_seq * seq_len
    weights = [lw["q_norm"], lw["w_uq"], lw["kv_norm"], lw["w_ukv"]]
    return pl.pallas_call(
        functools.partial(_mla_ctx_kernel, seq_len=seq_len, n_seq=n_seq),
        grid=(nb // n_seq,),
        in_specs=[pl.BlockSpec((rows, ZM_W), lambda b: (b, 0))] + [_layer_spec(a, layer) for a in weights]
        + [pl.BlockSpec(memory_space=pl.ANY)] * 2,
        out_specs=[
            pl.BlockSpec((rows, MLA_HEADS * MLA_V), lambda b: (b, 0)),
            pl.BlockSpec((n_seq, None, seq_len, KV_LORA), lambda b: (b, layer, 0, 0)),
            pl.BlockSpec((n_seq, None, seq_len, MLA_ROPE), lambda b: (b, layer, 0, 0)),
        ],
        out_shape=[
            jax.ShapeDtypeStruct((n, MLA_HEADS * MLA_V), F32),
            jax.ShapeDtypeStruct((nb, DEPTH, seq_len, KV_LORA), F32),
            jax.ShapeDtypeStruct((nb, DEPTH, seq_len, MLA_ROPE), F32),
        ],
        input_output_aliases={5: 1, 6: 2},
        scratch_shapes=[pltpu.VMEM((Q_LORA, MLA_HEADS * QH_W), BF16)],
        compiler_params=pltpu.CompilerParams(dimension_semantics=("arbitrary",),
                                             vmem_limit_bytes=VMEM_LIMIT),
        name="mla_ctx",
    )(zm, *weights, *prev)


def _mla_lat_kernel(zm_ref, cckv_ref, ckpe_ref, rc_ref, rs_ref, qn_ref, wuq_ref, kvn_ref, wukv_ref,
                    o_ref, k_s, v_s, wq_s, wqs_s, *, seq_len, tq, past):
    qi = pl.program_id(1)
    o_kpe = Q_LORA + KV_LORA

    @pl.when((pl.program_id(0) == 0) & (qi == 0))
    def _():
        _arrange_w_uq(wuq_ref, wq_s, wqs_s)

    @pl.when(qi == 0)
    def _():
        zm = zm_ref[...]
        ckv = _rms(zm[:, Q_LORA:o_kpe], kvn_ref[...])
        kpe = zm[:, o_kpe:o_kpe + LANES] * rc_ref[...] + zm[:, o_kpe + LANES:o_kpe + 2 * LANES] * rs_ref[...]
        kvc = _mm(cckv_ref[0], wukv_ref[...])
        kpc = ckpe_ref[0].astype(BF16)
        kv = _mm(ckv, wukv_ref[...])
        kpe = kpe.astype(BF16)
        for h in range(MLA_HEADS):
            c0 = h * QH_W
            k_s[0:past, c0:c0 + LANES] = kvc[:, c0:c0 + LANES].astype(BF16)
            k_s[0:past, c0 + LANES:c0 + QH_W] = kpc
            v_s[0:past, h * MLA_V:(h + 1) * MLA_V] = kvc[:, c0 + LANES:c0 + QH_W].astype(BF16)
            k_s[past:past + seq_len, c0:c0 + LANES] = kv[:, c0:c0 + LANES].astype(BF16)
            k_s[past:past + seq_len, c0 + LANES:c0 + QH_W] = kpe
            v_s[past:past + seq_len, h * MLA_V:(h + 1) * MLA_V] = kv[:, c0 + LANES:c0 + QH_W].astype(BF16)

    r0 = pl.multiple_of(qi * tq, tq)
    zq = zm_ref[pl.ds(r0, tq), :]
    cq = _rms(zq[:, :Q_LORA], qn_ref[...])
    q = _mm(cq, wq_s[...])
    qs = _mm(cq, wqs_s[...])
    qc = jnp.concatenate([jnp.ones((tq, LANES), F32), rc_ref[pl.ds(r0, tq), :]], axis=1)
    qsn = jnp.concatenate([jnp.zeros((tq, LANES), F32), rs_ref[pl.ds(r0, tq), :]], axis=1)
    for h in range(MLA_HEADS):
        c0 = h * QH_W
        qh = q[:, c0:c0 + QH_W] * qc + qs[:, c0:c0 + QH_W] * qsn
        s = _mm_nt(qh, k_s[:, c0:c0 + QH_W]) * MLA_SCALE
        e = jnp.exp(s - jnp.max(s, axis=-1, keepdims=True))
        den = jnp.sum(e, axis=-1, keepdims=True)
        o_ref[:, h * MLA_V:(h + 1) * MLA_V] = _mm(e, v_s[:, h * MLA_V:(h + 1) * MLA_V]) / den


def _mla_lat(zm, lw, layer, seq_len, tq, cache):
    n = zm.shape[0]
    nb = n // seq_len
    past = PAST_LEN
    tk = past + seq_len
    cckv, ckpe, rc, rs = cache
    weights = [lw["q_norm"], lw["w_uq"], lw["kv_norm"], lw["w_ukv"]]
    return pl.pallas_call(
        functools.partial(_mla_lat_kernel, seq_len=seq_len, tq=tq, past=past),
        grid=(nb, seq_len // tq),
        in_specs=[pl.BlockSpec((seq_len, ZM_W), lambda b, q: (b, 0)),
                  pl.BlockSpec((1, None, past, KV_LORA), lambda b, q: (b, layer, 0, 0)),
                  pl.BlockSpec((1, None, past, LANES), lambda b, q: (b, layer, 0, 0)),
                  _const_spec(rc), _const_spec(rs)] + [_layer_spec(a, layer) for a in weights],
        out_specs=pl.BlockSpec((tq, MLA_HEADS * MLA_V), lambda b, q: (b * (seq_len // tq) + q, 0)),
        out_shape=jax.ShapeDtypeStruct((n, MLA_HEADS * MLA_V), F32),
        scratch_shapes=[
            pltpu.VMEM((tk, MLA_HEADS * QH_W), BF16),
            pltpu.VMEM((tk, MLA_HEADS * MLA_V), BF16),
            pltpu.VMEM((Q_LORA, MLA_HEADS * QH_W), BF16),
            pltpu.VMEM((Q_LORA, MLA_HEADS * QH_W), BF16),
        ],
        compiler_params=pltpu.CompilerParams(dimension_semantics=("arbitrary", "arbitrary"),
                                             vmem_limit_bytes=VMEM_LIMIT),
        name="mla_lat",
    )(zm, cckv, ckpe, rc, rs, *weights)


SEQ_PER_STEP = 4
CHUNK_GROUP = 4
GDN_PRE_W = 5 * MIX_W


def _for_chunk_groups(n_chunks, fn):
    if n_chunks == CHUNK_GROUP:
        fn(0)
    else:
        def body(gi, carry):
            fn(gi * CHUNK_GROUP)
            return carry
        lax.fori_loop(0, n_chunks // CHUNK_GROUP, body, 0)


def _gdn_prepare(items, maskbd):
    n = range(len(items))
    qs, ks, vs, gs, betas, masks, revs = zip(*items)
    gcs = [_mm_sel_l(masks[i][2], gs[i]) for i in n]
    decays = []
    for i in n:
        inc, eye = masks[i][0], masks[i][3]
        gc_row = jnp.sum(eye * gcs[i], axis=0, keepdims=True)
        decays.append(jnp.where(inc, jnp.exp(jnp.where(inc, gcs[i] - gc_row, 0.0)), 0.0))
    kbs = [ks[i] * betas[i] for i in n]
    aqs = [_mm_nt(jnp.concatenate([kbs[i], qs[i]], axis=0), _bd(ks[i], maskbd)) for i in n]
    a_mats = [jnp.where(masks[i][1], aqs[i][:CHUNK] * decays[i], 0.0) for i in n]
    t_invs = _neumann_inverse(a_mats, [m[3] for m in masks], maskbd)
    egcs = [jnp.exp(gc) for gc in gcs]
    uws = [_mm(t_invs[i], jnp.concatenate([_bd(vs[i] * betas[i], maskbd), _bd(kbs[i] * egcs[i], maskbd)], axis=1))
           for i in n]
    out = []
    for i in n:
        g_last = gcs[i][0:1] if revs[i] else gcs[i][CHUNK - 1:CHUNK]
        pre = jnp.concatenate([uws[i], qs[i] * egcs[i], aqs[i][CHUNK:] * decays[i],
                               ks[i] * jnp.exp(g_last - gcs[i])], axis=1)
        out.append((pre, jnp.broadcast_to(jnp.exp(g_last), (8, MIX_W))))
    return out


def _gdn_step(items, maskbd):
    w = MIX_W
    n = range(len(items))
    pres, egls, states = zip(*items)
    wqs = [_mm(jnp.concatenate([pres[i][:, w:2 * w], pres[i][:, 2 * w:3 * w]], axis=0), states[i]) for i in n]
    v_news = [pres[i][:, :w] - wqs[i][:CHUNK] for i in n]
    outs = [wqs[i][CHUNK:] + _mm(pres[i][:, 3 * w:4 * w], _bd(v_news[i], maskbd)) for i in n]
    upds = [_mm_tn(pres[i][:, 4 * w:], v_news[i]) for i in n]
    mask_f = maskbd.astype(F32)
    return [(outs[i], states[i] * egls[i] + upds[i] * mask_f) for i in n]


def _gdn_kernel(*refs, seq_len, n_seq, cached):
    if cached:
        (zg_ref, zab_ref, s0_ref, conv_ref, alog_ref, dtb_ref, gn_ref, eexp_ref, ones_ref, maskbd_ref,
         o_ref, sout_ref, q_s, k_s, v_s, ge_s, pre_s, gl_s, oacc_s, st_s) = refs
    else:
        zg_ref, zab_ref, conv_ref, alog_ref, dtb_ref, gn_ref, eexp_ref, ones_ref, maskbd_ref = refs[:9]
        o_ref, sout_ref, q_s, k_s, v_s, ge_s, pre_s, gl_s, oacc_s, st_s = refs[-10:]
    t = seq_len * n_seq
    z = zg_ref[:, :GDN_CONV_CH]
    rowi = jnp.bitwise_and(_iota((t, 1), 0), seq_len - 1)
    zp = jnp.where(rowi == 0, 0.0, pltpu.roll(z, 1, 0))
    zn = jnp.where(rowi == seq_len - 1, 0.0, pltpu.roll(z, t - 1, 0))
    cw = conv_ref[...]
    qkv = _silu(zp * cw[0:1] + z * cw[1:2] + zn * cw[2:3])
    ones_bd = ones_ref[...]
    q = qkv[:, :MIX_W]
    k = qkv[:, MIX_W:2 * MIX_W]
    q_s[...] = q * lax.rsqrt(_mm_sel_r(q * q, ones_bd) + 1e-6) * (HEAD_DIM ** -0.5)
    k_s[...] = k * lax.rsqrt(_mm_sel_r(k * k, ones_bd) + 1e-6)
    v_s[...] = qkv[:, 2 * MIX_W:]
    ab = zab_ref[...]
    lane = _iota((t, LANES), 1)
    gb = jnp.where(lane < 2 * HEADS, -jnp.exp(alog_ref[...]) * _softplus(ab + dtb_ref[...]), _sigmoid(ab))
    ge_s[...] = _mm_sel_r(gb, eexp_ref[...])
    oacc_s[...] = jnp.zeros((t, MIX_W), F32)
    if cached:
        st_s[...] = s0_ref[...]
    else:
        st_s[...] = jnp.zeros((n_seq, 2, MIX_W, MIX_W), F32)
    maskbd = maskbd_ref[...]
    masks = (_chunk_masks(False), _chunk_masks(True))
    n_chunks = seq_len // CHUNK

    def prepare_group(c0):
        where, items = [], []
        for j in range(CHUNK_GROUP):
            c = c0 + j
            rows = pl.ds(pl.multiple_of(c * CHUNK, CHUNK), CHUNK)
            for d in range(2):
                where.append((d, c, rows))
                items.append((q_s[rows, :], k_s[rows, :], v_s[rows, :], ge_s[rows, d * MIX_W:(d + 1) * MIX_W],
                              ge_s[rows, (2 + d) * MIX_W:(3 + d) * MIX_W], masks[d], d == 1))
        for (d, c, rows), (pre, egl) in zip(where, _gdn_prepare(items, maskbd)):
            pre_s[d, rows, :] = pre
            gl_s[d, pl.ds(pl.multiple_of(c * 8, 8), 8), :] = egl

    _for_chunk_groups(n_chunks * n_seq, prepare_group)

    def body(i, carry):
        where, items = [], []
        for q in range(n_seq):
            for d in range(2):
                c = q * n_chunks + (i if d == 0 else n_chunks - 1 - i)
                rows = pl.ds(pl.multiple_of(c * CHUNK, CHUNK), CHUNK)
                where.append((q, d, rows))
                items.append((pre_s[d, rows, :], gl_s[d, pl.ds(pl.multiple_of(c * 8, 8), 8), :][0:1], st_s[q, d]))
        for (q, d, rows), (o, s_new) in zip(where, _gdn_step(items, maskbd)):
            oacc_s[rows, :] = oacc_s[rows, :] + o
            st_s[q, d] = s_new
        return carry

    lax.fori_loop(0, n_chunks, body, 0)
    o = oacc_s[...]
    ms = _mm_sel_r(o * o, ones_bd) * (1.0 / HEAD_DIM)
    gate = zg_ref[:, GDN_CONV_CH:]
    o_ref[...] = o * lax.rsqrt(ms + NORM_EPS) * gn_ref[...] * _silu(gate)
    for q in range(n_seq):
        for d in range(2):
            for h in range(HEADS):
                sl = slice(h * HEAD_DIM, (h + 1) * HEAD_DIM)
                sout_ref[q, d, h] = st_s[q, d, sl, sl]


def _state_out(nb, n_seq, layer, cached, prev, args, in_specs):
    if cached:
        return (pl.BlockSpec((n_seq, 2, HEADS, HEAD_DIM, HEAD_DIM), lambda b: (b, 0, 0, 0, 0)),
                jax.ShapeDtypeStruct((nb, 2, HEADS, HEAD_DIM, HEAD_DIM), F32), {})
    aliases = {len(args): 1}
    args.append(prev)
    in_specs.append(pl.BlockSpec(memory_space=pl.ANY))
    return (pl.BlockSpec((n_seq, None, 2, HEADS, HEAD_DIM, HEAD_DIM), lambda b: (b, layer, 0, 0, 0, 0)),
            jax.ShapeDtypeStruct((nb, DEPTH, 2, HEADS, HEAD_DIM, HEAD_DIM), F32), aliases)


def _gdn(zg, zab, lw, layer, consts, seq_len, s0_bd=None, prev=None):
    n = zg.shape[0]
    nb = n // seq_len
    cached = s0_bd is not None
    n_seq = 1 if cached else SEQ_PER_STEP
    assert seq_len & (seq_len - 1) == 0 and seq_len % (CHUNK * CHUNK_GROUP) == 0 and nb % n_seq == 0
    rows = seq_len * n_seq
    args = [zg, zab]
    in_specs = [pl.BlockSpec((rows, ZG_W), lambda b: (b, 0)),
                pl.BlockSpec((rows, LANES), lambda b: (b, 0))]
    if cached:
        args.append(s0_bd)
        in_specs.append(pl.BlockSpec((1, None, 2, MIX_W, MIX_W), lambda b: (b, layer, 0, 0, 0)))
    layered = [lw["gdn_conv"], lw["gdn_alog"], lw["gdn_dtb"], lw["gdn_norm"]]
    const = [consts["eexp"], consts["ones_bd"], consts["maskbd"]]
    args += layered + const
    in_specs += [_layer_spec(a, layer) for a in layered] + [_const_spec(a) for a in const]
    s_spec, s_shape, aliases = _state_out(nb, n_seq, layer, cached, prev, args, in_specs)
    return pl.pallas_call(
        functools.partial(_gdn_kernel, seq_len=seq_len, n_seq=n_seq, cached=cached),
        grid=(nb // n_seq,),
        in_specs=in_specs,
        out_specs=[
            pl.BlockSpec((rows, MIX_W), lambda b: (b, 0)),
            s_spec,
        ],
        out_shape=[jax.ShapeDtypeStruct((n, MIX_W), F32), s_shape],
        input_output_aliases=aliases,
        scratch_shapes=[
            pltpu.VMEM((rows, MIX_W), F32),
            pltpu.VMEM((rows, MIX_W), F32),
            pltpu.VMEM((rows, MIX_W), F32),
            pltpu.VMEM((rows, 4 * MIX_W), F32),
            pltpu.VMEM((2, rows, GDN_PRE_W), F32),
            pltpu.VMEM((2, rows // CHUNK * 8, MIX_W), F32),
            pltpu.VMEM((rows, MIX_W), F32),
            pltpu.VMEM((n_seq, 2, MIX_W, MIX_W), F32),
        ],
        compiler_params=pltpu.CompilerParams(dimension_semantics=("arbitrary",),
                                             vmem_limit_bytes=VMEM_LIMIT),
        name="gdn_lat" if cached else "gdn_ctx",
    )(*args)


RWKV_PRE_W = 7 * MIX_W


def _rwkv_prepare(items, maskbd, eye_full):
    n = range(len(items))
    rs, kds, vs, kks, bs, lws, masks, revs = zip(*items)
    cums = [_mm_sel_l(masks[i][2], lws[i]) for i in n]
    einvs = [jnp.exp(-c) for c in cums]
    kts = [kks[i] * jnp.exp(cums[i] - lws[i]) for i in n]
    rts = [rs[i] * jnp.exp(cums[i]) for i in n]
    krs = [jnp.concatenate([kts[i], rts[i]], axis=0) for i in n]
    lb_alls = [_mm_nt(krs[i], _bd(bs[i] * einvs[i], maskbd)) for i in n]
    lk_alls = [_mm_nt(krs[i], _bd(kds[i] * einvs[i], maskbd)) for i in n]
    lbs = [jnp.where(masks[i][1], lb_alls[i][:CHUNK], 0.0) for i in n]
    t_invs = _neumann_inverse(lbs, [m[3] for m in masks], maskbd)
    lvs = [_mm(jnp.concatenate([jnp.where(masks[i][1], lk_alls[i][:CHUNK], 0.0),
                                jnp.where(masks[i][0], lk_alls[i][CHUNK:], 0.0)], axis=0), _bd(vs[i], maskbd))
           for i in n]
    tkps = [_mm(t_invs[i], jnp.concatenate([_bd(kts[i], maskbd), _bd(lvs[i][:CHUNK], maskbd)], axis=1)) for i in n]
    out = []
    for i in n:
        c_last = cums[i][0:1] if revs[i] else cums[i][CHUNK - 1:CHUNK]
        tail = jnp.exp(c_last - cums[i])
        rb = jnp.where(masks[i][0], lb_alls[i][CHUNK:], 0.0)
        pre = jnp.concatenate([tkps[i][:, :MIX_W], rts[i], tkps[i][:, MIX_W:], lvs[i][CHUNK:], rb,
                               kds[i] * tail, bs[i] * tail], axis=1)
        gcol = jnp.sum(eye_full * jnp.exp(c_last), axis=1, keepdims=True)
        out.append((pre, jnp.broadcast_to(gcol, (MIX_W, LANES))))
    return out


def _rwkv_step(items, maskbd):
    w = MIX_W
    n = range(len(items))
    pres, vs, gcols, states = zip(*items)
    prs = [_mm(jnp.concatenate([pres[i][:, :w], pres[i][:, w:2 * w]], axis=0), states[i]) for i in n]
    ps = [prs[i][:CHUNK] + pres[i][:, 2 * w:3 * w] for i in n]
    outs = [prs[i][CHUNK:] + pres[i][:, 3 * w:4 * w] - _mm(pres[i][:, 4 * w:5 * w], _bd(ps[i], maskbd)) for i in n]
    upds = [_mm_tn(jnp.concatenate([pres[i][:, 5 * w:6 * w], pres[i][:, 6 * w:]], axis=0),
                   jnp.concatenate([vs[i], -ps[i]], axis=0)) for i in n]
    mask_f = maskbd.astype(F32)
    return [(outs[i], states[i] * jnp.concatenate([gcols[i], gcols[i]], axis=1) + upds[i] * mask_f) for i in n]


def _rwkv_kernel(*refs, seq_len, n_seq, cached):
    if cached:
        (zr_ref, s0_ref, mup_ref, mun_ref, w0_ref, w2_ref, a0_ref, a2_ref, g2_ref, kk_ref, ka_ref, rk_ref,
         gnw_ref, gnb_ref, ones_ref, maskbd_ref, o_ref, sout_ref,
         r_s, v_s, kk_s, dir_s, bg_s, pre_s, gcol_s, oacc_s, st_s) = refs
    else:
        (zr_ref, mup_ref, mun_ref, w0_ref, w2_ref, a0_ref, a2_ref, g2_ref, kk_ref, ka_ref, rk_ref,
         gnw_ref, gnb_ref, ones_ref, maskbd_ref) = refs[:15]
        o_ref, sout_ref, r_s, v_s, kk_s, dir_s, bg_s, pre_s, gcol_s, oacc_s, st_s = refs[-11:]
    t = seq_len * n_seq
    z = zr_ref[...]
    rowi = jnp.bitwise_and(_iota((t, 1), 0), seq_len - 1)
    zp = jnp.where(rowi == 0, 0.0, pltpu.roll(z, 1, 0))
    zn = jnp.where(rowi == seq_len - 1, 0.0, pltpu.roll(z, t - 1, 0))
    z = z + mup_ref[...] * (zp - z) + mun_ref[...] * (zn - z)
    w = MIX_W
    r = z[:, :w]
    k = z[:, w:2 * w]
    v = z[:, 2 * w:3 * w]
    wd = jnp.tanh(z[:, 3 * w:3 * w + LANES])
    ad = z[:, 3 * w + LANES:3 * w + 2 * LANES]
    gd = _sigmoid(z[:, 3 * w + 2 * LANES:])
    ones_bd = ones_ref[...]
    kk = k * kk_ref[...]
    kk = kk * lax.rsqrt(_mm_sel_r(kk * kk, ones_bd) + 1e-6)
    r_s[...] = r
    v_s[...] = v
    kk_s[...] = kk
    bonus = jnp.zeros((t, w), F32)
    for d in range(2):
        w_log = -_softplus(-(w0_ref[d:d + 1] + _mm(wd, w2_ref[d]))) - 0.5
        a = _sigmoid(a0_ref[d:d + 1] + _mm(ad, a2_ref[d]))
        kd = k * (1.0 + (a - 1.0) * ka_ref[...])
        dir_s[:, (3 * d) * w:(3 * d + 1) * w] = -jnp.exp(w_log)
        dir_s[:, (3 * d + 1) * w:(3 * d + 2) * w] = kd
        dir_s[:, (3 * d + 2) * w:(3 * d + 3) * w] = kk * a
        bonus = bonus + _mm_sel_r(r * kd * rk_ref[...], ones_bd) * v
    bg_s[:, :w] = bonus
    bg_s[:, w:] = _mm(gd, g2_ref[...])
    oacc_s[...] = jnp.zeros((t, w), F32)
    if cached:
        st_s[...] = s0_ref[...]
    else:
        st_s[...] = jnp.zeros((n_seq, 2, w, w), F32)
    maskbd = maskbd_ref[...]
    masks = (_chunk_masks(False), _chunk_masks(True))
    eye_full = jnp.where(_iota((w, w), 0) == _iota((w, w), 1), 1.0, 0.0).astype(F32)
    n_chunks = seq_len // CHUNK

    def prepare_group(c0):
        where, items = [], []
        for j in range(CHUNK_GROUP):
            c = c0 + j
            rows = pl.ds(pl.multiple_of(c * CHUNK, CHUNK), CHUNK)
            for d in range(2):
                where.append((d, c, rows))
                items.append((r_s[rows, :], dir_s[rows, (3 * d + 1) * w:(3 * d + 2) * w], v_s[rows, :], kk_s[rows, :],
                              dir_s[rows, (3 * d + 2) * w:(3 * d + 3) * w], dir_s[rows, (3 * d) * w:(3 * d + 1) * w],
                              masks[d], d == 1))
        for (d, c, rows), (pre, gcol) in zip(where, _rwkv_prepare(items, maskbd, eye_full)):
            pre_s[d, rows, :] = pre
            gcol_s[d, pl.ds(pl.multiple_of(c * w, w), w), :] = gcol

    _for_chunk_groups(n_chunks * n_seq, prepare_group)

    def body(i, carry):
        where, items = [], []
        for q in range(n_seq):
            for d in range(2):
                c = q * n_chunks + (i if d == 0 else n_chunks - 1 - i)
                rows = pl.ds(pl.multiple_of(c * CHUNK, CHUNK), CHUNK)
                where.append((q, d, rows))
                items.append((pre_s[d, rows, :], v_s[rows, :], gcol_s[d, pl.ds(pl.multiple_of(c * w, w), w), :],
                              st_s[q, d]))
        for (q, d, rows), (o, z_new) in zip(where, _rwkv_step(items, maskbd)):
            oacc_s[rows, :] = oacc_s[rows, :] + o
            st_s[q, d] = z_new
        return carry

    lax.fori_loop(0, n_chunks, body, 0)
    o = oacc_s[...]
    inv_n = 1.0 / HEAD_DIM
    mu = _mm_sel_r(o, ones_bd) * inv_n
    oc = o - mu
    var = _mm_sel_r(oc * oc, ones_bd) * inv_n
    y = oc * lax.rsqrt(var + RWKV_GN_EPS) * gnw_ref[...] + gnb_ref[...]
    o_ref[...] = (y + bg_s[:, :w]) * bg_s[:, w:]
    for q in range(n_seq):
        for d in range(2):
            state = st_s[q, d] if cached else st_s[q, d].T
            for h in range(HEADS):
                sl = slice(h * HEAD_DIM, (h + 1) * HEAD_DIM)
                sout_ref[q, d, h] = state[sl, sl]


def _rwkv(zr, lw, layer, consts, seq_len, s0_bd=None, prev=None):
    n = zr.shape[0]
    nb = n // seq_len
    cached = s0_bd is not None
    n_seq = 1 if cached else SEQ_PER_STEP
    assert seq_len & (seq_len - 1) == 0 and seq_len % (CHUNK * CHUNK_GROUP) == 0 and nb % n_seq == 0
    rows = seq_len * n_seq
    args = [zr]
    in_specs = [pl.BlockSpec((rows, ZR_W), lambda b: (b, 0))]
    if cached:
        args.append(s0_bd)
        in_specs.append(pl.BlockSpec((1, None, 2, MIX_W, MIX_W), lambda b: (b, layer, 0, 0, 0)))
    layered = [lw["rwkv_mu_prev"], lw["rwkv_mu_next"], lw["rwkv_w0"], lw["rwkv_w2"], lw["rwkv_a0"], lw["rwkv_a2"],
               lw["rwkv_g2"], lw["rwkv_k_k"], lw["rwkv_k_a"], lw["rwkv_r_k"], lw["rwkv_gn_w"], lw["rwkv_gn_b"]]
    const = [consts["ones_bd"], consts["maskbd"]]
    args += layered + const
    in_specs += [_layer_spec(a, layer) for a in layered] + [_const_spec(a) for a in const]
    s_spec, s_shape, aliases = _state_out(nb, n_seq, layer, cached, prev, args, in_specs)
    return pl.pallas_call(
        functools.partial(_rwkv_kernel, seq_len=seq_len, n_seq=n_seq, cached=cached),
        grid=(nb // n_seq,),
        in_specs=in_specs,
        out_specs=[
            pl.BlockSpec((rows, MIX_W), lambda b: (b, 0)),
            s_spec,
        ],
        out_shape=[jax.ShapeDtypeStruct((n, MIX_W), F32), s_shape],
        input_output_aliases=aliases,
        scratch_shapes=[
            pltpu.VMEM((rows, MIX_W), F32),
            pltpu.VMEM((rows, MIX_W), F32),
            pltpu.VMEM((rows, MIX_W), F32),
            pltpu.VMEM((rows, 6 * MIX_W), F32),
            pltpu.VMEM((rows, 2 * MIX_W), F32),
            pltpu.VMEM((2, rows, RWKV_PRE_W), F32),
            pltpu.VMEM((2, rows // CHUNK * MIX_W, LANES), F32),
            pltpu.VMEM((rows, MIX_W), F32),
            pltpu.VMEM((n_seq, 2, MIX_W, MIX_W), F32),
        ],
        compiler_params=pltpu.CompilerParams(dimension_semantics=("arbitrary",),
                                             vmem_limit_bytes=VMEM_LIMIT),
        name="rwkv_lat" if cached else "rwkv_ctx",
    )(*args)


def _route(logits_t, bias):
    tm = logits_t.shape[1]
    neg = -jnp.inf
    sc = _sigmoid(logits_t)
    sc3 = sc.reshape(N_GROUPS, GROUP_SIZE, tm)
    sel = (sc + bias).reshape(N_GROUPS, GROUP_SIZE, tm)
    si = _iota(sel.shape, 1).astype(F32)
    m1 = jnp.max(sel, axis=1, keepdims=True)
    f1 = jnp.min(jnp.where(sel == m1, si, float(GROUP_SIZE)), axis=1, keepdims=True)
    m2 = jnp.max(jnp.where(si == f1, neg, sel), axis=1, keepdims=True)
    grp = m1 + m2
    gi = _iota(grp.shape, 0).astype(F32)
    gsel = jnp.zeros(grp.shape, F32)
    for _ in range(TOPK_GROUPS):
        mx = jnp.max(grp, axis=0, keepdims=True)
        fi = jnp.min(jnp.where(grp == mx, gi, float(N_GROUPS)), axis=0, keepdims=True)
        hit = gi == fi
        gsel = jnp.where(hit, 1.0, gsel)
        grp = jnp.where(hit, neg, grp)
    cur = jnp.where(gsel > 0.0, sel, neg)
    ei = (_iota(cur.shape, 0) * GROUP_SIZE + _iota(cur.shape, 1)).astype(F32)
    chosen = jnp.zeros(cur.shape, F32)
    ids, wts = [], []
    for _ in range(TOP_K):
        mx = jnp.max(jnp.max(cur, axis=0, keepdims=True), axis=1, keepdims=True)
        fi = jnp.min(jnp.min(jnp.where(cur == mx, ei, float(N_EXPERTS)), axis=0, keepdims=True),
                     axis=1, keepdims=True)
        hit = ei == fi
        chosen = jnp.where(hit, 1.0, chosen)
        cur = jnp.where(hit, neg, cur)
        ids.append(fi.reshape(1, tm))
        wts.append(jnp.sum(jnp.sum(jnp.where(hit, sc3, 0.0), axis=0, keepdims=True), axis=1, keepdims=True)
                   .reshape(1, tm))
    w = jnp.concatenate(wts, axis=0)
    w = w / jnp.sum(w, axis=0, keepdims=True) * ROUTE_SCALE
    return chosen.reshape(N_EXPERTS, tm), jnp.concatenate(ids, axis=0), w


def _pack_halves(x):
    half = x.shape[1] // 2
    bits = lax.bitcast_convert_type(x.astype(BF16).astype(F32), jnp.int32)
    lo = lax.shift_right_logical(bits[:, :half], jnp.int32(16))
    return jnp.bitwise_or(lo, jnp.bitwise_and(bits[:, half:], jnp.int32(-65536)))


def _unpack_halves(word):
    lo = lax.bitcast_convert_type(lax.shift_left(word, jnp.int32(16)), F32)
    hi = lax.bitcast_convert_type(jnp.bitwise_and(word, jnp.int32(-65536)), F32)
    return lo, hi


def _post_kernel(x_ref, om_ref, og_ref, or_ref, mod_ref, wo_ref, n2_ref, rt_ref, rb_ref, sgu_ref, sdn_ref,
                 tri_ref, cin_ref, x1_ref, h2_ref, eid_ref, rank_ref, ew_ref, cnt_ref, carry_s):
    @pl.when(pl.program_id(0) == 0)
    def _():
        carry_s[...] = cin_ref[...]

    m = mod_ref[0]
    g1 = m[:, 2 * D_MODEL:3 * D_MODEL]
    sh2 = m[:, 3 * D_MODEL:4 * D_MODEL]
    sc2 = m[:, 4 * D_MODEL:5 * D_MODEL]
    w_mla = MLA_HEADS * MLA_V
    mix = (_mm(om_ref[...], wo_ref[0:w_mla, :]) + _mm(og_ref[...], wo_ref[w_mla:w_mla + MIX_W, :])
           + _mm(or_ref[...], wo_ref[w_mla + MIX_W:, :]))
    x1 = x_ref[...] + g1 * mix
    h2 = _rms(x1, n2_ref[...]) * (1.0 + sc2) + sh2
    h2_ref[...] = _pack_halves(h2)
    g2 = m[:, 5 * D_MODEL:]
    x1_ref[...] = x1 + g2 * _mm(_swiglu_act(_mm(h2, sgu_ref[...])), sdn_ref[...])
    r_hi, r_lo = _split2(rt_ref[...])
    h_hi, h_lo = _split2(h2)
    logits_t = _mm_nt(r_hi, h_hi) + _mm_nt(r_hi, h_lo) + _mm_nt(r_lo, h_hi)
    chosen, ids, w = _route(logits_t, rb_ref[...])
    tm = chosen.shape[1]
    rank_et = (carry_s[:, 0:1] + _mm(chosen, tri_ref[...])).reshape(N_GROUPS, GROUP_SIZE, tm)
    ei = (_iota(rank_et.shape, 0) * GROUP_SIZE + _iota(rank_et.shape, 1)).astype(F32)
    ranks = []
    for k in range(TOP_K):
        pick = jnp.where(ei == ids[k:k + 1].reshape(1, 1, tm), rank_et, 0.0)
        ranks.append(jnp.sum(jnp.sum(pick, axis=0, keepdims=True), axis=1, keepdims=True).reshape(1, tm))
    eid_ref[...] = ids.astype(jnp.int32)
    rank_ref[...] = jnp.concatenate(ranks, axis=0).astype(jnp.int32)
    ew_ref[...] = jnp.concatenate([w, jnp.zeros((LANES - TOP_K, tm), F32)], axis=0).T
    total = carry_s[...] + jnp.sum(chosen, axis=1, keepdims=True)
    carry_s[...] = total
    cnt_ref[...] = total


def _post(x2d, om, og, orw, mods, lw, layer, consts, tm, seq_len, mod_base, counts_in):
    n = x2d.shape[0]
    tiles_per_seq = seq_len // tm if mod_base else 1

    def mod_idx(i):
        return (layer, mod_base + i // tiles_per_seq if mod_base else 0, 0, 0)

    row = lambda w: pl.BlockSpec((tm, w), lambda i: (i, 0))
    col = lambda h: pl.BlockSpec((h, tm), lambda i: (0, i))
    full = lambda a: _layer_spec(a, layer)
    tail = [lw["w_out"], lw["norm2"], lw["router_t"], lw["router_b"], lw["shared_w_gu"], lw["shared_w_down"]]
    tri = consts["tri_tokens"]
    return pl.pallas_call(
        _post_kernel,
        grid=(n // tm,),
        in_specs=[row(D_MODEL), row(MLA_HEADS * MLA_V), row(MIX_W), row(MIX_W),
                  pl.BlockSpec((None, 1, 1, 6 * D_MODEL), mod_idx)] + [full(a) for a in tail]
        + [_const_spec(tri), _const_spec(counts_in)],
        out_specs=[row(D_MODEL), row(D_MODEL // 2), col(TOP_K), col(TOP_K), row(LANES),
                   pl.BlockSpec((N_EXPERTS, LANES), lambda i: (0, 0))],
        out_shape=[
            jax.ShapeDtypeStruct((n, D_MODEL), F32),
            jax.ShapeDtypeStruct((n, D_MODEL // 2), jnp.int32),
            jax.ShapeDtypeStruct((TOP_K, n), jnp.int32),
            jax.ShapeDtypeStruct((TOP_K, n), jnp.int32),
            jax.ShapeDtypeStruct((n, LANES), F32),
            jax.ShapeDtypeStruct((N_EXPERTS, LANES), F32),
        ],
        scratch_shapes=[pltpu.VMEM((N_EXPERTS, LANES), F32)],
        compiler_params=pltpu.CompilerParams(dimension_semantics=("arbitrary",),
                                             vmem_limit_bytes=VMEM_LIMIT),
        name="post",
    )(x2d, om, og, orw, mods, *tail, tri, counts_in)


MOE_ROWS = 512
SC_ROWS = 128
SC_SUBCORES = 32


def _swiglu_act(gu):
    return _silu(gu[:, :D_EXPERT]) * gu[:, D_EXPERT:]


def _dispatch_plan(eid, rank, counts, n, rows):
    n_blocks = n * TOP_K // rows + N_EXPERTS
    cnt = counts[:, 0].astype(jnp.int32)
    blocks = (cnt + rows - 1) // rows
    block_end = jnp.cumsum(blocks)
    offset = (block_end - blocks) * rows
    experts = jnp.arange(N_EXPERTS, dtype=jnp.int32)
    dest = jnp.sum(jnp.where(eid[..., None] == experts, offset, 0), axis=-1) + rank
    block_ids = jnp.arange(n_blocks, dtype=jnp.int32)
    block_expert = jnp.minimum(jnp.sum((block_end[None, :] <= block_ids[:, None]).astype(jnp.int32), axis=1),
                               N_EXPERTS - 1)
    n_used = block_end[-1:]
    prev_expert = jnp.concatenate([jnp.full((1,), -1, jnp.int32), block_expert[:-1]])
    first = ((block_ids < n_used[0]) & (block_expert != prev_expert)).astype(jnp.int32)
    slot = jnp.bitwise_and(jnp.cumsum(first) - 1, 1)
    owner_or_none = jnp.where(blocks > 0, experts, N_EXPERTS)
    next_owner = jnp.concatenate([lax.cummin(owner_or_none[::-1])[::-1][1:], jnp.full((1,), N_EXPERTS, jnp.int32)])
    nxt = jnp.sum(jnp.where(block_expert[:, None] == experts, next_owner, 0), axis=1)
    return dest, (block_expert, first, slot.astype(jnp.int32), nxt.astype(jnp.int32), n_used.astype(jnp.int32)), n_blocks


def _sc_mesh():
    return plsc.VectorSubcoreMesh(core_axis_name="core", subcore_axis_name="subcore")


def _sc_dispatch(groups, dest, n_rows):
    w = groups[0].shape[1]
    dtype = groups[0].dtype

    @functools.partial(pl.kernel, out_type=jax.ShapeDtypeStruct((n_rows, w), dtype), mesh=_sc_mesh(),
                       scratch_types=[pltpu.VMEM((SC_ROWS, w), dtype), pltpu.VMEM((TOP_K, SC_ROWS), jnp.int32)])
    def kern(*refs):
        x_refs, d_hbm, o_hbm, xv, dv = refs[:len(groups)], *refs[len(groups):]
        sid = lax.axis_index("core") * (SC_SUBCORES // 2) + lax.axis_index("subcore")
        start = 0
        for x_hbm, x in zip(x_refs, groups):
            def chunk(c, x_hbm=x_hbm, start=start):
                r0 = pl.multiple_of(c * SC_ROWS, SC_ROWS)
                pltpu.sync_copy(x_hbm.at[pl.ds(r0, SC_ROWS)], xv)
                pltpu.sync_copy(d_hbm.at[:, pl.ds(start + r0, SC_ROWS)], dv)
                for k in range(TOP_K):
                    pltpu.sync_copy(xv, o_hbm.at[dv.at[k]])

            pl.loop(sid, x.shape[0] // SC_ROWS, step=SC_SUBCORES)(chunk)
            start += x.shape[0]

    return kern(*groups, dest)


def _sc_gather(y, idx):
    w = y.shape[1]
    n_chunks = idx.shape[0]

    @functools.partial(pl.kernel, out_type=jax.ShapeDtypeStruct((n_chunks * SC_ROWS, w), y.dtype), mesh=_sc_mesh(),
                       scratch_types=[pltpu.VMEM((SC_ROWS, w), y.dtype), pltpu.VMEM((1, SC_ROWS), jnp.int32)])
    def kern(y_hbm, i_hbm, o_hbm, ov, iv):
        sid = lax.axis_index("core") * (SC_SUBCORES // 2) + lax.axis_index("subcore")

        @pl.loop(sid, n_chunks, step=SC_SUBCORES)
        def _(c):
            pltpu.sync_copy(i_hbm.at[pl.ds(c, 1)], iv)
            pltpu.sync_copy(y_hbm.at[iv.at[0]], ov)
            pltpu.sync_copy(ov, o_hbm.at[pl.ds(pl.multiple_of(c * SC_ROWS, SC_ROWS), SC_ROWS)])

    return kern(y, idx)


def _moe_rows_kernel(be_ref, first_ref, slot_ref, nxt_ref, nu_ref, x_ref, wgu_hbm, wdn_hbm, y_ref,
                     wgu_f, wdn_f, wgu_b, wdn_b, sem, *, layer):
    b = pl.program_id(0)

    def weight_copies(expert, slot):
        return (pltpu.make_async_copy(wgu_hbm.at[layer, expert], wgu_f.at[slot], sem.at[slot, 0]),
                pltpu.make_async_copy(wdn_hbm.at[layer, expert], wdn_f.at[slot], sem.at[slot, 1]))

    @pl.when(b == 0)
    def _():
        for copy in weight_copies(be_ref[0], 0):
            copy.start()

    @pl.when(first_ref[b] == 1)
    def _():
        slot = slot_ref[b]
        for copy in weight_copies(be_ref[b], slot):
            copy.wait()

        @pl.when(nxt_ref[b] < N_EXPERTS)
        def _():
            for copy in weight_copies(nxt_ref[b], 1 - slot):
                copy.start()

        wgu_b[...] = wgu_f[slot].astype(BF16)
        wdn_b[...] = wdn_f[slot].astype(BF16)

    @pl.when(b < nu_ref[0])
    def _():
        half = D_MODEL // 2
        lo, hi = _unpack_halves(x_ref[...])
        gu = _mm(lo, wgu_b[0:half, :]) + _mm(hi, wgu_b[half:, :])
        y_ref[...] = _pack_halves(_mm(_swiglu_act(gu), wdn_b[...]))


def _moe_rows(xs, schedule, lw, layer, n_blocks, rows):
    half = D_MODEL // 2
    last = lambda b, be, first, slot, nxt, nu: jnp.minimum(b, nu[0] - 1)
    row_spec = pl.BlockSpec((rows, half), lambda *a: (last(*a), 0))
    return pl.pallas_call(
        functools.partial(_moe_rows_kernel, layer=layer),
        grid_spec=pltpu.PrefetchScalarGridSpec(
            num_scalar_prefetch=5,
            grid=(n_blocks,),
            in_specs=[row_spec, pl.BlockSpec(memory_space=pl.ANY), pl.BlockSpec(memory_space=pl.ANY)],
            out_specs=row_spec,
            scratch_shapes=[
                pltpu.VMEM((2, D_MODEL, 2 * D_EXPERT), F32),
                pltpu.VMEM((2, D_EXPERT, D_MODEL), F32),
                pltpu.VMEM((D_MODEL, 2 * D_EXPERT), BF16),
                pltpu.VMEM((D_EXPERT, D_MODEL), BF16),
                pltpu.SemaphoreType.DMA((2, 2)),
            ],
        ),
        out_shape=jax.ShapeDtypeStruct(xs.shape, jnp.int32),
        compiler_params=pltpu.CompilerParams(dimension_semantics=("arbitrary",),
                                             vmem_limit_bytes=VMEM_LIMIT),
        name="moe_rows",
    )(*schedule, xs, lw["moe_w_gu"], lw["moe_w_down"])


def _moe_combine_kernel(yg_ref, ew_ref, x1_ref, mod_ref, nf_ref, o_ref, *, final):
    ew = ew_ref[...]
    acc_lo = acc_hi = None
    for k in range(TOP_K):
        lo, hi = _unpack_halves(yg_ref[k])
        wk = ew[:, k:k + 1]
        acc_lo = wk * lo if acc_lo is None else acc_lo + wk * lo
        acc_hi = wk * hi if acc_hi is None else acc_hi + wk * hi
    g2 = mod_ref[0][:, 5 * D_MODEL:]
    x2 = x1_ref[...] + g2 * jnp.concatenate([acc_lo, acc_hi], axis=1)
    if final:
        x2 = _rms(x2, nf_ref[...])
    o_ref[...] = x2


def _moe_combine(yg, row0, ew, x1, mods, layer, norm_f, tm, seq_len, mod_base, final):
    n = x1.shape[0]
    half = D_MODEL // 2
    tiles_per_seq = seq_len // tm if mod_base else 1
    tile0 = row0 // tm

    def mod_idx(i):
        return (layer, mod_base + i // tiles_per_seq if mod_base else 0, 0, 0)

    row = lambda w: pl.BlockSpec((tm, w), lambda i: (i, 0))
    return pl.pallas_call(
        functools.partial(_moe_combine_kernel, final=final),
        grid=(n // tm,),
        in_specs=[pl.BlockSpec((TOP_K, tm, half), lambda i: (0, tile0 + i, 0)), row(LANES), row(D_MODEL),
                  pl.BlockSpec((None, 1, 1, 6 * D_MODEL), mod_idx), _const_spec(norm_f)],
        out_specs=row(D_MODEL),
        out_shape=jax.ShapeDtypeStruct((n, D_MODEL), F32),
        compiler_params=pltpu.CompilerParams(dimension_semantics=("arbitrary",),
                                             vmem_limit_bytes=VMEM_LIMIT),
        name="moe_combine_final" if final else "moe_combine",
    )(yg, ew, x1, mods, norm_f)


def _moe_experts(groups, eid, rank, counts, lw, layer):
    group_sizes = [g.shape[0] for g in groups]
    n = sum(group_sizes)
    dest, schedule, n_blocks = _dispatch_plan(eid, rank, counts, n, MOE_ROWS)
    xs = _sc_dispatch(groups, dest, n_blocks * MOE_ROWS)
    y = _moe_rows(xs, schedule, lw, layer, n_blocks, MOE_ROWS)
    outs, start = [], 0
    for size in group_sizes:
        idx = dest[:, start:start + size].reshape(size * TOP_K // SC_ROWS, SC_ROWS)
        outs.append(_sc_gather(y, idx).reshape(TOP_K, size, D_MODEL // 2))
        start += size
    return outs


def _constants():
    idx = np.arange(MIX_W)
    same_head = (idx[:, None] // HEAD_DIM) == (idx[None, :] // HEAD_DIM)
    maskbd = jnp.asarray(same_head, BF16)
    eexp = np.zeros((LANES, 4 * MIX_W), np.float32)
    for blk in range(4):
        kind, d = divmod(blk, 2)
        for h in range(HEADS):
            src = kind * 2 * HEADS + d * HEADS + h
            eexp[src, blk * MIX_W + h * HEAD_DIM: blk * MIX_W + (h + 1) * HEAD_DIM] = 1.0
    tri = np.triu(np.ones((POST_TM, POST_TM), np.float32), 1)
    return {"maskbd": maskbd, "ones_bd": maskbd, "eexp": jnp.asarray(eexp, BF16), "tri_tokens": jnp.asarray(tri, BF16)}


def _rope_tables(n):
    rows = n // GRID_W
    row = jnp.repeat(jnp.arange(rows, dtype=F32), GRID_W)
    col = jnp.tile(jnp.arange(GRID_W, dtype=F32), rows)
    axis_dim = MLA_ROPE // 2
    inv = jnp.power(ROPE_BASE, -jnp.arange(0, axis_dim, 2, dtype=F32) / axis_dim)
    ang_r = row[:, None] * inv
    ang_c = col[:, None] * inv
    cr, sr, cc, sc = jnp.cos(ang_r), jnp.sin(ang_r), jnp.cos(ang_c), jnp.sin(ang_c)
    zeros = jnp.zeros((n, LANES - MLA_ROPE), F32)
    cos_t = jnp.concatenate([cr, cr, cc, cc, zeros], axis=1)
    sin_t = jnp.concatenate([-sr, sr, -sc, sc, zeros], axis=1)
    return cos_t, sin_t


def _stacked_weights(p):
    def per_direction(w):
        half = jnp.zeros((DEPTH, 64, MIX_W), F32)
        return jnp.stack([jnp.concatenate([w[:, 0], half], axis=1),
                          jnp.concatenate([half, w[:, 1]], axis=1)], axis=1).astype(BF16)

    row = lambda v: v.reshape(DEPTH, 1, -1)
    pad_row = lambda v: jnp.pad(row(v), ((0, 0), (0, 0), (0, LANES - 2 * HEADS)))
    return {
        "norm1": row(p["norm1"]),
        "w_in": p["w_in"],
        "q_norm": row(p["mla_q_norm"]),
        "w_uq": p["mla_w_uq"],
        "kv_norm": row(p["mla_kv_norm"]),
        "w_ukv": p["mla_w_ukv"].astype(BF16),
        "gdn_conv": p["gdn_conv"],
        "gdn_alog": pad_row(p["gdn_a_log"]),
        "gdn_dtb": pad_row(p["gdn_dt_bias"]),
        "gdn_norm": jnp.tile(row(p["gdn_norm"]), (1, 1, HEADS)),
        "rwkv_mu_prev": row(p["rwkv_mu_prev"]),
        "rwkv_mu_next": row(p["rwkv_mu_next"]),
        "rwkv_w0": p["rwkv_w0"],
        "rwkv_w2": per_direction(p["rwkv_w2"]),
        "rwkv_a0": p["rwkv_a0"],
        "rwkv_a2": per_direction(p["rwkv_a2"]),
        "rwkv_g2": p["rwkv_g2"].astype(BF16),
        "rwkv_k_k": row(p["rwkv_k_k"]),
        "rwkv_k_a": row(p["rwkv_k_a"]),
        "rwkv_r_k": row(p["rwkv_r_k"]),
        "rwkv_gn_w": row(p["rwkv_gn_w"]),
        "rwkv_gn_b": row(p["rwkv_gn_b"]),
        "w_out": p["w_out"].astype(BF16),
        "norm2": row(p["norm2"]),
        "router_t": jnp.swapaxes(p["moe_router"], 1, 2),
        "router_b": p["moe_bias"].reshape(DEPTH, N_EXPERTS, 1),
        "moe_w_gu": p["moe_w_gu"],
        "moe_w_down": p["moe_w_down"],
        "shared_w_gu": p["shared_w_gu"].astype(BF16),
        "shared_w_down": p["shared_w_down"].astype(BF16),
    }


def _embed_block_diag(s):
    b = s.shape[0]
    eye = jnp.eye(HEADS, dtype=s.dtype)
    out = jnp.einsum("bdhkv,hg->bdhkgv", s, eye)
    return out.reshape(b, 2, MIX_W, MIX_W)


def _layer_front(x2d, mods, lw, l, consts, seq_len, mod_base, cache, tm, tq, counts_in, prev=None):
    zm, zg, zab, zr = _inproj(x2d, mods, lw, l, tm, seq_len, mod_base)
    if cache is None:
        o_mla, ckv, kpe = _mla_ctx(zm, lw, l, seq_len, prev[:2])
        o_gdn, s_gdn = _gdn(zg, zab, lw, l, consts, seq_len, prev=prev[2])
        o_rwkv, s_rwkv = _rwkv(zr, lw, l, consts, seq_len, prev=prev[3])
        new = (ckv, kpe, s_gdn, s_rwkv)
    else:
        cckv, ckpe, rc, rs, sg, sr = cache
        o_mla = _mla_lat(zm, lw, l, seq_len, tq, (cckv, ckpe, rc, rs))
        o_gdn, _ = _gdn(zg, zab, lw, l, consts, seq_len, sg)
        o_rwkv, _ = _rwkv(zr, lw, l, consts, seq_len, sr)
        new = None
    routed = _post(x2d, o_mla, o_gdn, o_rwkv, mods, lw, l, consts, tm, seq_len, mod_base, counts_in)
    return routed, new


def kernel(x_prompt, x_sample, cache_mla_ckv, cache_mla_kpe, state_gdn, state_rwkv, c, c_ctx, ada_w, ada_b, norm1, w_in, mla_q_norm, mla_w_uq, mla_kv_norm, mla_w_ukv, gdn_conv, gdn_a_log, gdn_dt_bias, gdn_norm, rwkv_mu_prev, rwkv_mu_next, rwkv_w0, rwkv_w2, rwkv_a0, rwkv_a2, rwkv_g2, rwkv_k_k, rwkv_k_a, rwkv_r_k, rwkv_gn_w, rwkv_gn_b, w_out, norm2, moe_router, moe_bias, moe_w_gu, moe_w_down, shared_w_gu, shared_w_down, norm_f):
    p = dict(norm1=norm1, w_in=w_in, mla_q_norm=mla_q_norm, mla_w_uq=mla_w_uq, mla_kv_norm=mla_kv_norm,
             mla_w_ukv=mla_w_ukv, gdn_conv=gdn_conv, gdn_a_log=gdn_a_log, gdn_dt_bias=gdn_dt_bias,
             gdn_norm=gdn_norm, rwkv_mu_prev=rwkv_mu_prev, rwkv_mu_next=rwkv_mu_next, rwkv_w0=rwkv_w0,
             rwkv_w2=rwkv_w2, rwkv_a0=rwkv_a0, rwkv_a2=rwkv_a2, rwkv_g2=rwkv_g2, rwkv_k_k=rwkv_k_k,
             rwkv_k_a=rwkv_k_a, rwkv_r_k=rwkv_r_k, rwkv_gn_w=rwkv_gn_w, rwkv_gn_b=rwkv_gn_b, w_out=w_out,
             norm2=norm2, moe_router=moe_router, moe_bias=moe_bias, moe_w_gu=moe_w_gu, moe_w_down=moe_w_down,
             shared_w_gu=shared_w_gu, shared_w_down=shared_w_down)
    weights = _stacked_weights(p)
    consts = _constants()
    nf = norm_f.reshape(1, D_MODEL)
    b_ctx, t_ctx, _ = x_prompt.shape
    b_lat, t_lat, _ = x_sample.shape

    cvec8 = jnp.concatenate([c_ctx[None, :], c, jnp.zeros((8 - 1 - b_lat, D_MODEL), F32)], axis=0)
    mods = _adaln(cvec8, ada_w, ada_b)
    mods = mods.reshape(DEPTH, 8, 1, 6 * D_MODEL)

    rc, rs = _rope_tables(t_lat)
    ckpe = jnp.pad(cache_mla_kpe, ((0, 0), (0, 0), (0, 0), (0, LANES - MLA_ROPE)))
    cache = (cache_mla_ckv, ckpe, rc, rs, _embed_block_diag_layers(state_gdn),
             _embed_block_diag_layers(jnp.swapaxes(state_rwkv, -1, -2)))
    xp = x_prompt.reshape(b_ctx * t_ctx, D_MODEL)
    xs = x_sample.reshape(b_lat * t_lat, D_MODEL)
    n_ctx = xp.shape[0]
    tm = POST_TM
    state_shape = (b_ctx, DEPTH, 2, HEADS, HEAD_DIM, HEAD_DIM)
    ctx_outs = (jnp.zeros((b_ctx, DEPTH, t_ctx, KV_LORA), F32), jnp.zeros((b_ctx, DEPTH, t_ctx, MLA_ROPE), F32),
                jnp.zeros(state_shape, F32), jnp.zeros(state_shape, F32))
    for l in range(DEPTH):
        final = l == DEPTH - 1
        no_pairs = jnp.zeros((N_EXPERTS, LANES), F32)
        (x1c, hc, eidc, rankc, ewc, cnt_c), ctx_outs = _layer_front(xp, mods, weights, l, consts, t_ctx, 0, None,
                                                                    tm, t_ctx, no_pairs, ctx_outs)
        (x1s, hs, eids, ranks, ews, cnt), _ = _layer_front(xs, mods, weights, l, consts, t_lat, 1, cache,
                                                           tm, 256, cnt_c)
        yg_c, yg_s = _moe_experts([hc, hs], jnp.concatenate([eidc, eids], axis=1),
                                  jnp.concatenate([rankc, ranks], axis=1), cnt, weights, l)
        xp = _moe_combine(yg_c, 0, ewc, x1c, mods, l, nf, tm, t_ctx, 0, final)
        xs = _moe_combine(yg_s, 0, ews, x1s, mods, l, nf, tm, t_lat, 1, final)

    y_prompt = xp.reshape(b_ctx, t_ctx, D_MODEL)
    y_sample = xs.reshape(b_lat, t_lat, D_MODEL)
    new_ckv, new_kpe, new_gdn, new_rwkv = ctx_outs
    return (y_prompt, y_sample, new_ckv, new_kpe, new_gdn, new_rwkv)


def _embed_block_diag_layers(s):
    b = s.shape[0]
    return _embed_block_diag(s.reshape(b * DEPTH, 2, HEADS, HEAD_DIM, HEAD_DIM)).reshape(
        b, DEPTH, 2, MIX_W, MIX_W)
```

```python
import functools

import numpy as np
import jax
import jax.numpy as jnp
from jax import lax
from jax.experimental import pallas as pl
from jax.experimental.pallas import tpu as pltpu
from jax.experimental.pallas import tpu_sc as plsc

F32 = jnp.float32
BF16 = jnp.bfloat16

D_MODEL = 1024
DEPTH = 2
PAST_LEN = 512
GRID_W = 64
NORM_EPS = 1e-6

MLA_HEADS = 4
MLA_NOPE = 128
MLA_ROPE = 64
MLA_V = 128
Q_LORA = 384
KV_LORA = 256
ROPE_BASE = 10000.0
MLA_SCALE = (MLA_NOPE + MLA_ROPE) ** -0.5

HEADS = 4
HEAD_DIM = 64
MIX_W = HEADS * HEAD_DIM
GDN_CONV_CH = 3 * MIX_W
CHUNK = 64
RWKV_GN_EPS = 64e-5

N_EXPERTS = 64
TOP_K = 8
N_GROUPS = 8
GROUP_SIZE = N_EXPERTS // N_GROUPS
TOPK_GROUPS = 4
D_EXPERT = 256
ROUTE_SCALE = 2.5

P_MLA = Q_LORA + KV_LORA + MLA_ROPE
P_GDN = GDN_CONV_CH + MIX_W + 4 * HEADS
P_RWKV = 3 * MIX_W + 128 + 128 + 128

LANES = 128
ZM_W = Q_LORA + KV_LORA + 2 * LANES
ZG_W = GDN_CONV_CH + MIX_W
ZR_W = P_RWKV
QH_W = 2 * LANES
VMEM_LIMIT = 56 * 1024 * 1024
POST_TM = 512

_ROPE_SWAP = np.concatenate([np.arange(16, 32), np.arange(0, 16), np.arange(48, 64), np.arange(32, 48)])


def _sigmoid(x):
    return 1.0 / (1.0 + jnp.exp(-x))


def _silu(x):
    return x * _sigmoid(x)


def _softplus(x):
    return jnp.maximum(x, 0.0) + jnp.log(1.0 + jnp.exp(-jnp.abs(x)))


def _rms(x, g, eps=NORM_EPS):
    return x * lax.rsqrt(jnp.mean(x * x, axis=-1, keepdims=True) + eps) * g


def _mm(a, b):
    return jnp.dot(a.astype(BF16), b.astype(BF16), preferred_element_type=F32)


def _mm_nt(a, b):
    return lax.dot_general(a.astype(BF16), b.astype(BF16), (((1,), (1,)), ((), ())),
                           preferred_element_type=F32)


def _mm_tn(a, b):
    return lax.dot_general(a.astype(BF16), b.astype(BF16), (((0,), (0,)), ((), ())),
                           preferred_element_type=F32)


def _split3(x):
    p1 = x.astype(BF16)
    r1 = x - p1.astype(F32)
    p2 = r1.astype(BF16)
    r2 = r1 - p2.astype(F32)
    return p1, p2, r2.astype(BF16)


def _mm_sel_l(sel, x):
    p1, p2, p3 = _split3(x)
    return _mm(sel, p1) + _mm(sel, p2) + _mm(sel, p3)


def _mm_sel_r(x, sel):
    p1, p2, p3 = _split3(x)
    return _mm(p1, sel) + _mm(p2, sel) + _mm(p3, sel)


def _iota(shape, dim):
    return lax.broadcasted_iota(jnp.int32, shape, dim)


def _layer_spec(a, layer, **kw):
    nd = a.ndim - 1
    return pl.BlockSpec((None,) + a.shape[1:], lambda *_: (layer,) + (0,) * nd, **kw)


def _const_spec(a, **kw):
    return pl.BlockSpec(a.shape, lambda *_: (0,) * a.ndim, **kw)


def _bd(x, maskbd):
    xb = x.astype(BF16)
    return jnp.concatenate([xb] * HEADS, axis=0) * maskbd


def _chunk_masks(rev):
    row = _iota((CHUNK, MIX_W), 0)
    col = jnp.bitwise_and(_iota((CHUNK, MIX_W), 1), HEAD_DIM - 1)
    r2 = _iota((CHUNK, CHUNK), 0)
    c2 = _iota((CHUNK, CHUNK), 1)
    if rev:
        inc, strict, tri = row <= col, row < col, r2 <= c2
    else:
        inc, strict, tri = row >= col, row > col, r2 >= c2
    eye = jnp.where(row == col, 1.0, 0.0).astype(F32)
    return inc, strict, jnp.where(tri, 1.0, 0.0).astype(BF16), eye


def _split2(x):
    hi = x.astype(BF16)
    return hi, (x - hi.astype(F32)).astype(BF16)


def _mm_bd3(x, p, maskbd):
    n = x.shape[0]
    xh, xl = _split2(x)
    ph, pl_ = _split2(p)
    r = jnp.dot(jnp.concatenate([xh, xl], axis=0), _bd(ph, maskbd), preferred_element_type=F32)
    return r[:n] + r[n:] + jnp.dot(xh, _bd(pl_, maskbd), preferred_element_type=F32)


def _neumann_inverse(a_list, eye_list, maskbd):
    bs = [-a for a in a_list]
    ms = [eye + b for eye, b in zip(eye_list, bs)]
    ps = [_mm_bd3(b, b, maskbd) for b in bs]
    for _ in range(4):
        boths = [_mm_bd3(jnp.concatenate([m, p], axis=0), p, maskbd) for m, p in zip(ms, ps)]
        ms = [m + both[:CHUNK] for m, both in zip(ms, boths)]
        ps = [both[CHUNK:] for both in boths]
    return [m + _mm_bd3(m, p, maskbd) for m, p in zip(ms, ps)]


def _adaln_kernel(c_ref, w_ref, b_ref, o_ref):
    cv = c_ref[...]
    o_ref[0] = _mm(_silu(cv), w_ref[0]) + b_ref[0]


def _adaln(cvec8, ada_w, ada_b):
    tn = 768
    n_out = 6 * D_MODEL
    return pl.pallas_call(
        _adaln_kernel,
        grid=(DEPTH, n_out // tn),
        in_specs=[
            pl.BlockSpec((8, D_MODEL), lambda l, j: (0, 0)),
            pl.BlockSpec((1, D_MODEL, tn), lambda l, j: (l, 0, j)),
            pl.BlockSpec((1, 1, tn), lambda l, j: (l, 0, j)),
        ],
        out_specs=pl.BlockSpec((1, 8, tn), lambda l, j: (l, 0, j)),
        out_shape=jax.ShapeDtypeStruct((DEPTH, 8, n_out), F32),
        compiler_params=pltpu.CompilerParams(dimension_semantics=("arbitrary", "arbitrary"),
                                             vmem_limit_bytes=VMEM_LIMIT),
        name="adaln",
    )(cvec8, ada_w, ada_b.reshape(DEPTH, 1, n_out))


_KPE0 = Q_LORA + KV_LORA
_W_IN_MOVES = (
    [(0, 0, P_MLA)]
    + [(P_MLA + LANES - MLA_ROPE + 16 * j, _KPE0 + 16 * int(_ROPE_SWAP[16 * j] // 16), 16) for j in range(4)]
    + [(ZM_W, P_MLA, ZG_W), (ZM_W + ZG_W, P_MLA + ZG_W, 4 * HEADS), (ZM_W + ZG_W + LANES, P_MLA + P_GDN, P_RWKV)]
)
W_IN_PAD = ZM_W + ZG_W + LANES + ZR_W


def _inproj_kernel(x_ref, mod_ref, n1_ref, w_ref, zm_ref, zg_ref, zab_ref, zr_ref, w_s):
    @pl.when(pl.program_id(0) == 0)
    def _():
        w_s[...] = jnp.zeros(w_s.shape, BF16)
        for dst, src, width in _W_IN_MOVES:
            w_s[:, dst:dst + width] = w_ref[:, src:src + width].astype(BF16)

    m = mod_ref[0]
    sh = m[:, 0:D_MODEL]
    sc = m[:, D_MODEL:2 * D_MODEL]
    h = _rms(x_ref[...], n1_ref[...]) * (1.0 + sc) + sh
    z = _mm(h, w_s[...])
    o1 = ZM_W
    o2 = o1 + ZG_W
    o3 = o2 + LANES
    zm_ref[...] = z[:, :o1]
    zg_ref[...] = z[:, o1:o2]
    zab_ref[...] = z[:, o2:o3]
    zr_ref[...] = z[:, o3:]


def _inproj(x2d, mods, lw, layer, tm, seq_len, mod_base):
    n = x2d.shape[0]
    tiles_per_seq = seq_len // tm if mod_base else 1

    def mod_idx(i):
        return (layer, mod_base + i // tiles_per_seq if mod_base else 0, 0, 0)

    return pl.pallas_call(
        _inproj_kernel,
        grid=(n // tm,),
        in_specs=[
            pl.BlockSpec((tm, D_MODEL), lambda i: (i, 0)),
            pl.BlockSpec((None, 1, 1, 6 * D_MODEL), mod_idx),
            _layer_spec(lw["norm1"], layer),
            _layer_spec(lw["w_in"], layer, pipeline_mode=pl.Buffered(1)),
        ],
        out_specs=[
            pl.BlockSpec((tm, ZM_W), lambda i: (i, 0)),
            pl.BlockSpec((tm, ZG_W), lambda i: (i, 0)),
            pl.BlockSpec((tm, LANES), lambda i: (i, 0)),
            pl.BlockSpec((tm, ZR_W), lambda i: (i, 0)),
        ],
        out_shape=[
            jax.ShapeDtypeStruct((n, ZM_W), F32),
            jax.ShapeDtypeStruct((n, ZG_W), F32),
            jax.ShapeDtypeStruct((n, LANES), F32),
            jax.ShapeDtypeStruct((n, ZR_W), F32),
        ],
        scratch_shapes=[pltpu.VMEM((D_MODEL, W_IN_PAD), BF16)],
        compiler_params=pltpu.CompilerParams(dimension_semantics=("arbitrary",),
                                             vmem_limit_bytes=VMEM_LIMIT),
        name="inproj",
    )(x2d, mods, lw["norm1"], lw["w_in"])


MLA_SEQ_PER_STEP = 4
Q_HEAD_W = MLA_NOPE + MLA_ROPE


def _arrange_w_uq(w_ref, wq_s, wqs_s=None):
    wq_s[...] = jnp.zeros(wq_s.shape, BF16)
    if wqs_s is not None:
        wqs_s[...] = jnp.zeros(wqs_s.shape, BF16)
    for h in range(MLA_HEADS):
        src, dst = h * Q_HEAD_W, h * QH_W
        wq_s[:, dst:dst + Q_HEAD_W] = w_ref[:, src:src + Q_HEAD_W].astype(BF16)
        if wqs_s is not None:
            for j in range(4):
                s0 = src + MLA_NOPE + 16 * int(_ROPE_SWAP[16 * j] // 16)
                wqs_s[:, dst + MLA_NOPE + 16 * j:dst + MLA_NOPE + 16 * (j + 1)] = w_ref[:, s0:s0 + 16].astype(BF16)


def _mla_ctx_kernel(zm_ref, qn_ref, wuq_ref, kvn_ref, wukv_ref, prev_ckv, prev_kpe, o_ref, ckv_ref, kpe_ref, wq_s,
                    *, seq_len, n_seq):
    del prev_ckv, prev_kpe

    @pl.when(pl.program_id(0) == 0)
    def _():
        _arrange_w_uq(wuq_ref, wq_s)

    o_kpe = Q_LORA + KV_LORA
    zm = zm_ref[...]
    ckv = _rms(zm[:, Q_LORA:o_kpe], kvn_ref[...])
    kpe = zm[:, o_kpe:o_kpe + LANES]
    kv = _mm(ckv, wukv_ref[...])
    q = _mm(_rms(zm[:, :Q_LORA], qn_ref[...]), wq_s[...])
    kpe_b = kpe.astype(BF16)
    seqs = [slice(i * seq_len, (i + 1) * seq_len) for i in range(n_seq)]
    for i, r in enumerate(seqs):
        ckv_ref[i] = ckv[r]
        kpe_ref[i] = kpe[r, :MLA_ROPE]
    pairs = [(r, h * QH_W) for r in seqs for h in range(MLA_HEADS)]
    scores = [_mm_nt(q[r, c0:c0 + QH_W], jnp.concatenate([kv[r, c0:c0 + LANES].astype(BF16), kpe_b[r]], axis=1))
              * MLA_SCALE for r, c0 in pairs]
    exps = [jnp.exp(s - jnp.max(s, axis=-1, keepdims=True)) for s in scores]
    outs = [_mm(e, kv[r, c0 + LANES:c0 + QH_W]) / jnp.sum(e, axis=-1, keepdims=True)
            for e, (r, c0) in zip(exps, pairs)]
    for o, (r, c0) in zip(outs, pairs):
        h = c0 // QH_W
        o_ref[r, h * MLA_V:(h + 1) * MLA_V] = o


def _mla_ctx(zm, lw, layer, seq_len, prev):
    n = zm.shape[0]
    nb = n // seq_len
    n_seq = MLA_SEQ_PER_STEP
    rows = n_seq * seq_len
    weights = [lw["q_norm"], lw["w_uq"], lw["kv_norm"], lw["w_ukv"]]
    return pl.pallas_call(
        functools.partial(_mla_ctx_kernel, seq_len=seq_len, n_seq=n_seq),
        grid=(nb // n_seq,),
        in_specs=[pl.BlockSpec((rows, ZM_W), lambda b: (b, 0))] + [_layer_spec(a, layer) for a in weights]
        + [pl.BlockSpec(memory_space=pl.ANY)] * 2,
        out_specs=[
            pl.BlockSpec((rows, MLA_HEADS * MLA_V), lambda b: (b, 0)),
            pl.BlockSpec((n_seq, None, seq_len, KV_LORA), lambda b: (b, layer, 0, 0)),
            pl.BlockSpec((n_seq, None, seq_len, MLA_ROPE), lambda b: (b, layer, 0, 0)),
        ],
        out_shape=[
            jax.ShapeDtypeStruct((n, MLA_HEADS * MLA_V), F32),
            jax.ShapeDtypeStruct((nb, DEPTH, seq_len, KV_LORA), F32),
            jax.ShapeDtypeStruct((nb, DEPTH, seq_len, MLA_ROPE), F32),
        ],
        input_output_aliases={5: 1, 6: 2},
        scratch_shapes=[pltpu.VMEM((Q_LORA, MLA_HEADS * QH_W), BF16)],
        compiler_params=pltpu.CompilerParams(dimension_semantics=("arbitrary",),
                                             vmem_limit_bytes=VMEM_LIMIT),
        name="mla_ctx",
    )(zm, *weights, *prev)


def _mla_lat_kernel(zm_ref, cckv_ref, ckpe_ref, rc_ref, rs_ref, qn_ref, wuq_ref, kvn_ref, wukv_ref,
                    o_ref, k_s, v_s, wq_s, wqs_s, *, seq_len, tq, past):
    qi = pl.program_id(1)
    o_kpe = Q_LORA + KV_LORA

    @pl.when((pl.program_id(0) == 0) & (qi == 0))
    def _():
        _arrange_w_uq(wuq_ref, wq_s, wqs_s)

    @pl.when(qi == 0)
    def _():
        zm = zm_ref[...]
        ckv = _rms(zm[:, Q_LORA:o_kpe], kvn_ref[...])
        kpe = zm[:, o_kpe:o_kpe + LANES] * rc_ref[...] + zm[:, o_kpe + LANES:o_kpe + 2 * LANES] * rs_ref[...]
        kvc = _mm(cckv_ref[0], wukv_ref[...])
        kpc = ckpe_ref[0].astype(BF16)
        kv = _mm(ckv, wukv_ref[...])
        kpe = kpe.astype(BF16)
        for h in range(MLA_HEADS):
            c0 = h * QH_W
            k_s[0:past, c0:c0 + LANES] = kvc[:, c0:c0 + LANES].astype(BF16)
            k_s[0:past, c0 + LANES:c0 + QH_W] = kpc
            v_s[0:past, h * MLA_V:(h + 1) * MLA_V] = kvc[:, c0 + LANES:c0 + QH_W].astype(BF16)
            k_s[past:past + seq_len, c0:c0 + LANES] = kv[:, c0:c0 + LANES].astype(BF16)
            k_s[past:past + seq_len, c0 + LANES:c0 + QH_W] = kpe
            v_s[past:past + seq_len, h * MLA_V:(h + 1) * MLA_V] = kv[:, c0 + LANES:c0 + QH_W].astype(BF16)

    r0 = pl.multiple_of(qi * tq, tq)
    zq = zm_ref[pl.ds(r0, tq), :]
    cq = _rms(zq[:, :Q_LORA], qn_ref[...])
    q = _mm(cq, wq_s[...])
    qs = _mm(cq, wqs_s[...])
    qc = jnp.concatenate([jnp.ones((tq, LANES), F32), rc_ref[pl.ds(r0, tq), :]], axis=1)
    qsn = jnp.concatenate([jnp.zeros((tq, LANES), F32), rs_ref[pl.ds(r0, tq), :]], axis=1)
    for h in range(MLA_HEADS):
        c0 = h * QH_W
        qh = q[:, c0:c0 + QH_W] * qc + qs[:, c0:c0 + QH_W] * qsn
        s = _mm_nt(qh, k_s[:, c0:c0 + QH_W]) * MLA_SCALE
        e = jnp.exp(s - jnp.max(s, axis=-1, keepdims=True))
        den = jnp.sum(e, axis=-1, keepdims=True)
        o_ref[:, h * MLA_V:(h + 1) * MLA_V] = _mm(e, v_s[:, h * MLA_V:(h + 1) * MLA_V]) / den


def _mla_lat(zm, lw, layer, seq_len, tq, cache):
    n = zm.shape[0]
    nb = n // seq_len
    past = PAST_LEN
    tk = past + seq_len
    cckv, ckpe, rc, rs = cache
    weights = [lw["q_norm"], lw["w_uq"], lw["kv_norm"], lw["w_ukv"]]
    return pl.pallas_call(
        functools.partial(_mla_lat_kernel, seq_len=seq_len, tq=tq, past=past),
        grid=(nb, seq_len // tq),
        in_specs=[pl.BlockSpec((seq_len, ZM_W), lambda b, q: (b, 0)),
                  pl.BlockSpec((1, None, past, KV_LORA), lambda b, q: (b, layer, 0, 0)),
                  pl.BlockSpec((1, None, past, LANES), lambda b, q: (b, layer, 0, 0)),
                  _const_spec(rc), _const_spec(rs)] + [_layer_spec(a, layer) for a in weights],
        out_specs=pl.BlockSpec((tq, MLA_HEADS * MLA_V), lambda b, q: (b * (seq_len // tq) + q, 0)),
        out_shape=jax.ShapeDtypeStruct((n, MLA_HEADS * MLA_V), F32),
        scratch_shapes=[
            pltpu.VMEM((tk, MLA_HEADS * QH_W), BF16),
            pltpu.VMEM((tk, MLA_HEADS * MLA_V), BF16),
            pltpu.VMEM((Q_LORA, MLA_HEADS * QH_W), BF16),
            pltpu.VMEM((Q_LORA, MLA_HEADS * QH_W), BF16),
        ],
        compiler_params=pltpu.CompilerParams(dimension_semantics=("arbitrary", "arbitrary"),
                                             vmem_limit_bytes=VMEM_LIMIT),
        name="mla_lat",
    )(zm, cckv, ckpe, rc, rs, *weights)


SEQ_PER_STEP = 4
CHUNK_GROUP = 4
GDN_PRE_W = 5 * MIX_W


def _for_chunk_groups(n_chunks, fn):
    if n_chunks == CHUNK_GROUP:
        fn(0)
    else:
        def body(gi, carry):
            fn(gi * CHUNK_GROUP)
            return carry
        lax.fori_loop(0, n_chunks // CHUNK_GROUP, body, 0)


def _gdn_prepare(items, maskbd):
    n = range(len(items))
    qs, ks, vs, gs, betas, masks, revs = zip(*items)
    gcs = [_mm_sel_l(masks[i][2], gs[i]) for i in n]
    decays = []
    for i in n:
        inc, eye = masks[i][0], masks[i][3]
        gc_row = jnp.sum(eye * gcs[i], axis=0, keepdims=True)
        decays.append(jnp.where(inc, jnp.exp(jnp.where(inc, gcs[i] - gc_row, 0.0)), 0.0))
    kbs = [ks[i] * betas[i] for i in n]
    aqs = [_mm_nt(jnp.concatenate([kbs[i], qs[i]], axis=0), _bd(ks[i], maskbd)) for i in n]
    a_mats = [jnp.where(masks[i][1], aqs[i][:CHUNK] * decays[i], 0.0) for i in n]
    t_invs = _neumann_inverse(a_mats, [m[3] for m in masks], maskbd)
    egcs = [jnp.exp(gc) for gc in gcs]
    uws = [_mm(t_invs[i], jnp.concatenate([_bd(vs[i] * betas[i], maskbd), _bd(kbs[i] * egcs[i], maskbd)], axis=1))
           for i in n]
    out = []
    for i in n:
        g_last = gcs[i][0:1] if revs[i] else gcs[i][CHUNK - 1:CHUNK]
        pre = jnp.concatenate([uws[i], qs[i] * egcs[i], aqs[i][CHUNK:] * decays[i],
                               ks[i] * jnp.exp(g_last - gcs[i])], axis=1)
        out.append((pre, jnp.broadcast_to(jnp.exp(g_last), (8, MIX_W))))
    return out


def _gdn_step(items, maskbd):
    w = MIX_W
    n = range(len(items))
    pres, egls, states = zip(*items)
    wqs = [_mm(jnp.concatenate([pres[i][:, w:2 * w], pres[i][:, 2 * w:3 * w]], axis=0), states[i]) for i in n]
    v_news = [pres[i][:, :w] - wqs[i][:CHUNK] for i in n]
    outs = [wqs[i][CHUNK:] + _mm(pres[i][:, 3 * w:4 * w], _bd(v_news[i], maskbd)) for i in n]
    upds = [_mm_tn(pres[i][:, 4 * w:], v_news[i]) for i in n]
    mask_f = maskbd.astype(F32)
    return [(outs[i], states[i] * egls[i] + upds[i] * mask_f) for i in n]


def _gdn_kernel(*refs, seq_len, n_seq, cached):
    if cached:
        (zg_ref, zab_ref, s0_ref, conv_ref, alog_ref, dtb_ref, gn_ref, eexp_ref, ones_ref, maskbd_ref,
         o_ref, sout_ref, q_s, k_s, v_s, ge_s, pre_s, gl_s, oacc_s, st_s) = refs
    else:
        zg_ref, zab_ref, conv_ref, alog_ref, dtb_ref, gn_ref, eexp_ref, ones_ref, maskbd_ref = refs[:9]
        o_ref, sout_ref, q_s, k_s, v_s, ge_s, pre_s, gl_s, oacc_s, st_s = refs[-10:]
    t = seq_len * n_seq
    z = zg_ref[:, :GDN_CONV_CH]
    rowi = jnp.bitwise_and(_iota((t, 1), 0), seq_len - 1)
    zp = jnp.where(rowi == 0, 0.0, pltpu.roll(z, 1, 0))
    zn = jnp.where(rowi == seq_len - 1, 0.0, pltpu.roll(z, t - 1, 0))
    cw = conv_ref[...]
    qkv = _silu(zp * cw[0:1] + z * cw[1:2] + zn * cw[2:3])
    ones_bd = ones_ref[...]
    q = qkv[:, :MIX_W]
    k = qkv[:, MIX_W:2 * MIX_W]
    q_s[...] = q * lax.rsqrt(_mm_sel_r(q * q, ones_bd) + 1e-6) * (HEAD_DIM ** -0.5)
    k_s[...] = k * lax.rsqrt(_mm_sel_r(k * k, ones_bd) + 1e-6)
    v_s[...] = qkv[:, 2 * MIX_W:]
    ab = zab_ref[...]
    lane = _iota((t, LANES), 1)
    gb = jnp.where(lane < 2 * HEADS, -jnp.exp(alog_ref[...]) * _softplus(ab + dtb_ref[...]), _sigmoid(ab))
    ge_s[...] = _mm_sel_r(gb, eexp_ref[...])
    oacc_s[...] = jnp.zeros((t, MIX_W), F32)
    if cached:
        st_s[...] = s0_ref[...]
    else:
        st_s[...] = jnp.zeros((n_seq, 2, MIX_W, MIX_W), F32)
    maskbd = maskbd_ref[...]
    masks = (_chunk_masks(False), _chunk_masks(True))
    n_chunks = seq_len // CHUNK

    def prepare_group(c0):
        where, items = [], []
        for j in range(CHUNK_GROUP):
            c = c0 + j
            rows = pl.ds(pl.multiple_of(c * CHUNK, CHUNK), CHUNK)
            for d in range(2):
                where.append((d, c, rows))
                items.append((q_s[rows, :], k_s[rows, :], v_s[rows, :], ge_s[rows, d * MIX_W:(d + 1) * MIX_W],
                              ge_s[rows, (2 + d) * MIX_W:(3 + d) * MIX_W], masks[d], d == 1))
        for (d, c, rows), (pre, egl) in zip(where, _gdn_prepare(items, maskbd)):
            pre_s[d, rows, :] = pre
            gl_s[d, pl.ds(pl.multiple_of(c * 8, 8), 8), :] = egl

    _for_chunk_groups(n_chunks * n_seq, prepare_group)

    def body(i, carry):
        where, items = [], []
        for q in range(n_seq):
            for d in range(2):
                c = q * n_chunks + (i if d == 0 else n_chunks - 1 - i)
                rows = pl.ds(pl.multiple_of(c * CHUNK, CHUNK), CHUNK)
                where.append((q, d, rows))
                items.append((pre_s[d, rows, :], gl_s[d, pl.ds(pl.multiple_of(c * 8, 8), 8), :][0:1], st_s[q, d]))
        for (q, d, rows), (o, s_new) in zip(where, _gdn_step(items, maskbd)):
            oacc_s[rows, :] = oacc_s[rows, :] + o
            st_s[q, d] = s_new
        return carry

    lax.fori_loop(0, n_chunks, body, 0)
    o = oacc_s[...]
    ms = _mm_sel_r(o * o, ones_bd) * (1.0 / HEAD_DIM)
    gate = zg_ref[:, GDN_CONV_CH:]
    o_ref[...] = o * lax.rsqrt(ms + NORM_EPS) * gn_ref[...] * _silu(gate)
    for q in range(n_seq):
        for d in range(2):
            for h in range(HEADS):
                sl = slice(h * HEAD_DIM, (h + 1) * HEAD_DIM)
                sout_ref[q, d, h] = st_s[q, d, sl, sl]


def _state_out(nb, n_seq, layer, cached, prev, args, in_specs):
    if cached:
        return (pl.BlockSpec((n_seq, 2, HEADS, HEAD_DIM, HEAD_DIM), lambda b: (b, 0, 0, 0, 0)),
                jax.ShapeDtypeStruct((nb, 2, HEADS, HEAD_DIM, HEAD_DIM), F32), {})
    aliases = {len(args): 1}
    args.append(prev)
    in_specs.append(pl.BlockSpec(memory_space=pl.ANY))
    return (pl.BlockSpec((n_seq, None, 2, HEADS, HEAD_DIM, HEAD_DIM), lambda b: (b, layer, 0, 0, 0, 0)),
            jax.ShapeDtypeStruct((nb, DEPTH, 2, HEADS, HEAD_DIM, HEAD_DIM), F32), aliases)


def _gdn(zg, zab, lw, layer, consts, seq_len, s0_bd=None, prev=None):
    n = zg.shape[0]
    nb = n // seq_len
    cached = s0_bd is not None
    n_seq = 1 if cached else SEQ_PER_STEP
    assert seq_len & (seq_len - 1) == 0 and seq_len % (CHUNK * CHUNK_GROUP) == 0 and nb % n_seq == 0
    rows = seq_len * n_seq
    args = [zg, zab]
    in_specs = [pl.BlockSpec((rows, ZG_W), lambda b: (b, 0)),
                pl.BlockSpec((rows, LANES), lambda b: (b, 0))]
    if cached:
        args.append(s0_bd)
        in_specs.append(pl.BlockSpec((1, None, 2, MIX_W, MIX_W), lambda b: (b, layer, 0, 0, 0)))
    layered = [lw["gdn_conv"], lw["gdn_alog"], lw["gdn_dtb"], lw["gdn_norm"]]
    const = [consts["eexp"], consts["ones_bd"], consts["maskbd"]]
    args += layered + const
    in_specs += [_layer_spec(a, layer) for a in layered] + [_const_spec(a) for a in const]
    s_spec, s_shape, aliases = _state_out(nb, n_seq, layer, cached, prev, args, in_specs)
    return pl.pallas_call(
        functools.partial(_gdn_kernel, seq_len=seq_len, n_seq=n_seq, cached=cached),
        grid=(nb // n_seq,),
        in_specs=in_specs,
        out_specs=[
            pl.BlockSpec((rows, MIX_W), lambda b: (b, 0)),
            s_spec,
        ],
        out_shape=[jax.ShapeDtypeStruct((n, MIX_W), F32), s_shape],
        input_output_aliases=aliases,
        scratch_shapes=[
            pltpu.VMEM((rows, MIX_W), F32),
            pltpu.VMEM((rows, MIX_W), F32),
            pltpu.VMEM((rows, MIX_W), F32),
            pltpu.VMEM((rows, 4 * MIX_W), F32),
            pltpu.VMEM((2, rows, GDN_PRE_W), F32),
            pltpu.VMEM((2, rows // CHUNK * 8, MIX_W), F32),
            pltpu.VMEM((rows, MIX_W), F32),
            pltpu.VMEM((n_seq, 2, MIX_W, MIX_W), F32),
        ],
        compiler_params=pltpu.CompilerParams(dimension_semantics=("arbitrary",),
                                             vmem_limit_bytes=VMEM_LIMIT),
        name="gdn_lat" if cached else "gdn_ctx",
    )(*args)


RWKV_PRE_W = 7 * MIX_W


def _rwkv_prepare(items, maskbd, eye_full):
    n = range(len(items))
    rs, kds, vs, kks, bs, lws, masks, revs = zip(*items)
    cums = [_mm_sel_l(masks[i][2], lws[i]) for i in n]
    einvs = [jnp.exp(-c) for c in cums]
    kts = [kks[i] * jnp.exp(cums[i] - lws[i]) for i in n]
    rts = [rs[i] * jnp.exp(cums[i]) for i in n]
    krs = [jnp.concatenate([kts[i], rts[i]], axis=0) for i in n]
    lb_alls = [_mm_nt(krs[i], _bd(bs[i] * einvs[i], maskbd)) for i in n]
    lk_alls = [_mm_nt(krs[i], _bd(kds[i] * einvs[i], maskbd)) for i in n]
    lbs = [jnp.where(masks[i][1], lb_alls[i][:CHUNK], 0.0) for i in n]
    t_invs = _neumann_inverse(lbs, [m[3] for m in masks], maskbd)
    lvs = [_mm(jnp.concatenate([jnp.where(masks[i][1], lk_alls[i][:CHUNK], 0.0),
                                jnp.where(masks[i][0], lk_alls[i][CHUNK:], 0.0)], axis=0), _bd(vs[i], maskbd))
           for i in n]
    tkps = [_mm(t_invs[i], jnp.concatenate([_bd(kts[i], maskbd), _bd(lvs[i][:CHUNK], maskbd)], axis=1)) for i in n]
    out = []
    for i in n:
        c_last = cums[i][0:1] if revs[i] else cums[i][CHUNK - 1:CHUNK]
        tail = jnp.exp(c_last - cums[i])
        rb = jnp.where(masks[i][0], lb_alls[i][CHUNK:], 0.0)
        pre = jnp.concatenate([tkps[i][:, :MIX_W], rts[i], tkps[i][:, MIX_W:], lvs[i][CHUNK:], rb,
                               kds[i] * tail, bs[i] * tail], axis=1)
        gcol = jnp.sum(eye_full * jnp.exp(c_last), axis=1, keepdims=True)
        out.append((pre, jnp.broadcast_to(gcol, (MIX_W, LANES))))
    return out


def _rwkv_step(items, maskbd):
    w = MIX_W
    n = range(len(items))
    pres, vs, gcols, states = zip(*items)
    prs = [_mm(jnp.concatenate([pres[i][:, :w], pres[i][:, w:2 * w]], axis=0), states[i]) for i in n]
    ps = [prs[i][:CHUNK] + pres[i][:, 2 * w:3 * w] for i in n]
    outs = [prs[i][CHUNK:] + pres[i][:, 3 * w:4 * w] - _mm(pres[i][:, 4 * w:5 * w], _bd(ps[i], maskbd)) for i in n]
    upds = [_mm_tn(jnp.concatenate([pres[i][:, 5 * w:6 * w], pres[i][:, 6 * w:]], axis=0),
                   jnp.concatenate([vs[i], -ps[i]], axis=0)) for i in n]
    mask_f = maskbd.astype(F32)
    return [(outs[i], states[i] * jnp.concatenate([gcols[i], gcols[i]], axis=1) + upds[i] * mask_f) for i in n]


def _rwkv_kernel(*refs, seq_len, n_seq, cached):
    if cached:
        (zr_ref, s0_ref, mup_ref, mun_ref, w0_ref, w2_ref, a0_ref, a2_ref, g2_ref, kk_ref, ka_ref, rk_ref,
         gnw_ref, gnb_ref, ones_ref, maskbd_ref, o_ref, sout_ref,
         r_s, v_s, kk_s, dir_s, bg_s, pre_s, gcol_s, oacc_s, st_s) = refs
    else:
        (zr_ref, mup_ref, mun_ref, w0_ref, w2_ref, a0_ref, a2_ref, g2_ref, kk_ref, ka_ref, rk_ref,
         gnw_ref, gnb_ref, ones_ref, maskbd_ref) = refs[:15]
        o_ref, sout_ref, r_s, v_s, kk_s, dir_s, bg_s, pre_s, gcol_s, oacc_s, st_s = refs[-11:]
    t = seq_len * n_seq
    z = zr_ref[...]
    rowi = jnp.bitwise_and(_iota((t, 1), 0), seq_len - 1)
    zp = jnp.where(rowi == 0, 0.0, pltpu.roll(z, 1, 0))
    zn = jnp.where(rowi == seq_len - 1, 0.0, pltpu.roll(z, t - 1, 0))
    z = z + mup_ref[...] * (zp - z) + mun_ref[...] * (zn - z)
    w = MIX_W
    r = z[:, :w]
    k = z[:, w:2 * w]
    v = z[:, 2 * w:3 * w]
    wd = jnp.tanh(z[:, 3 * w:3 * w + LANES])
    ad = z[:, 3 * w + LANES:3 * w + 2 * LANES]
    gd = _sigmoid(z[:, 3 * w + 2 * LANES:])
    ones_bd = ones_ref[...]
    kk = k * kk_ref[...]
    kk = kk * lax.rsqrt(_mm_sel_r(kk * kk, ones_bd) + 1e-6)
    r_s[...] = r
    v_s[...] = v
    kk_s[...] = kk
    bonus = jnp.zeros((t, w), F32)
    for d in range(2):
        w_log = -_softplus(-(w0_ref[d:d + 1] + _mm(wd, w2_ref[d]))) - 0.5
        a = _sigmoid(a0_ref[d:d + 1] + _mm(ad, a2_ref[d]))
        kd = k * (1.0 + (a - 1.0) * ka_ref[...])
        dir_s[:, (3 * d) * w:(3 * d + 1) * w] = -jnp.exp(w_log)
        dir_s[:, (3 * d + 1) * w:(3 * d + 2) * w] = kd
        dir_s[:, (3 * d + 2) * w:(3 * d + 3) * w] = kk * a
        bonus = bonus + _mm_sel_r(r * kd * rk_ref[...], ones_bd) * v
    bg_s[:, :w] = bonus
    bg_s[:, w:] = _mm(gd, g2_ref[...])
    oacc_s[...] = jnp.zeros((t, w), F32)
    if cached:
        st_s[...] = s0_ref[...]
    else:
        st_s[...] = jnp.zeros((n_seq, 2, w, w), F32)
    maskbd = maskbd_ref[...]
    masks = (_chunk_masks(False), _chunk_masks(True))
    eye_full = jnp.where(_iota((w, w), 0) == _iota((w, w), 1), 1.0, 0.0).astype(F32)
    n_chunks = seq_len // CHUNK

    def prepare_group(c0):
        where, items = [], []
        for j in range(CHUNK_GROUP):
            c = c0 + j
            rows = pl.ds(pl.multiple_of(c * CHUNK, CHUNK), CHUNK)
            for d in range(2):
                where.append((d, c, rows))
                items.append((r_s[rows, :], dir_s[rows, (3 * d + 1) * w:(3 * d + 2) * w], v_s[rows, :], kk_s[rows, :],
                              dir_s[rows, (3 * d + 2) * w:(3 * d + 3) * w], dir_s[rows, (3 * d) * w:(3 * d + 1) * w],
                              masks[d], d == 1))
        for (d, c, rows), (pre, gcol) in zip(where, _rwkv_prepare(items, maskbd, eye_full)):
            pre_s[d, rows, :] = pre
            gcol_s[d, pl.ds(pl.multiple_of(c * w, w), w), :] = gcol

    _for_chunk_groups(n_chunks * n_seq, prepare_group)

    def body(i, carry):
        where, items = [], []
        for q in range(n_seq):
            for d in range(2):
                c = q * n_chunks + (i if d == 0 else n_chunks - 1 - i)
                rows = pl.ds(pl.multiple_of(c * CHUNK, CHUNK), CHUNK)
                where.append((q, d, rows))
                items.append((pre_s[d, rows, :], v_s[rows, :], gcol_s[d, pl.ds(pl.multiple_of(c * w, w), w), :],
                              st_s[q, d]))
        for (q, d, rows), (o, z_new) in zip(where, _rwkv_step(items, maskbd)):
            oacc_s[rows, :] = oacc_s[rows, :] + o
            st_s[q, d] = z_new
        return carry

    lax.fori_loop(0, n_chunks, body, 0)
    o = oacc_s[...]
    inv_n = 1.0 / HEAD_DIM
    mu = _mm_sel_r(o, ones_bd) * inv_n
    oc = o - mu
    var = _mm_sel_r(oc * oc, ones_bd) * inv_n
    y = oc * lax.rsqrt(var + RWKV_GN_EPS) * gnw_ref[...] + gnb_ref[...]
    o_ref[...] = (y + bg_s[:, :w]) * bg_s[:, w:]
    for q in range(n_seq):
        for d in range(2):
            state = st_s[q, d] if cached else st_s[q, d].T
            for h in range(HEADS):
                sl = slice(h * HEAD_DIM, (h + 1) * HEAD_DIM)
                sout_ref[q, d, h] = state[sl, sl]


def _rwkv(zr, lw, layer, consts, seq_len, s0_bd=None, prev=None):
    n = zr.shape[0]
    nb = n // seq_len
    cached = s0_bd is not None
    n_seq = 1 if cached else SEQ_PER_STEP
    assert seq_len & (seq_len - 1) == 0 and seq_len % (CHUNK * CHUNK_GROUP) == 0 and nb % n_seq == 0
    rows = seq_len * n_seq
    args = [zr]
    in_specs = [pl.BlockSpec((rows, ZR_W), lambda b: (b, 0))]
    if cached:
        args.append(s0_bd)
        in_specs.append(pl.BlockSpec((1, None, 2, MIX_W, MIX_W), lambda b: (b, layer, 0, 0, 0)))
    layered = [lw["rwkv_mu_prev"], lw["rwkv_mu_next"], lw["rwkv_w0"], lw["rwkv_w2"], lw["rwkv_a0"], lw["rwkv_a2"],
               lw["rwkv_g2"], lw["rwkv_k_k"], lw["rwkv_k_a"], lw["rwkv_r_k"], lw["rwkv_gn_w"], lw["rwkv_gn_b"]]
    const = [consts["ones_bd"], consts["maskbd"]]
    args += layered + const
    in_specs += [_layer_spec(a, layer) for a in layered] + [_const_spec(a) for a in const]
    s_spec, s_shape, aliases = _state_out(nb, n_seq, layer, cached, prev, args, in_specs)
    return pl.pallas_call(
        functools.partial(_rwkv_kernel, seq_len=seq_len, n_seq=n_seq, cached=cached),
        grid=(nb // n_seq,),
        in_specs=in_specs,
        out_specs=[
            pl.BlockSpec((rows, MIX_W), lambda b: (b, 0)),
            s_spec,
        ],
        out_shape=[jax.ShapeDtypeStruct((n, MIX_W), F32), s_shape],
        input_output_aliases=aliases,
        scratch_shapes=[
            pltpu.VMEM((rows, MIX_W), F32),
            pltpu.VMEM((rows, MIX_W), F32),
            pltpu.VMEM((rows, MIX_W), F32),
            pltpu.VMEM((rows, 6 * MIX_W), F32),
            pltpu.VMEM((rows, 2 * MIX_W), F32),
            pltpu.VMEM((2, rows, RWKV_PRE_W), F32),
            pltpu.VMEM((2, rows // CHUNK * MIX_W, LANES), F32),
            pltpu.VMEM((rows, MIX_W), F32),
            pltpu.VMEM((n_seq, 2, MIX_W, MIX_W), F32),
        ],
        compiler_params=pltpu.CompilerParams(dimension_semantics=("arbitrary",),
                                             vmem_limit_bytes=VMEM_LIMIT),
        name="rwkv_lat" if cached else "rwkv_ctx",
    )(*args)


def _route(logits_t, bias):
    tm = logits_t.shape[1]
    neg = -jnp.inf
    sc = _sigmoid(logits_t)
    sc3 = sc.reshape(N_GROUPS, GROUP_SIZE, tm)
    sel = (sc + bias).reshape(N_GROUPS, GROUP_SIZE, tm)
    si = _iota(sel.shape, 1).astype(F32)
    m1 = jnp.max(sel, axis=1, keepdims=True)
    f1 = jnp.min(jnp.where(sel == m1, si, float(GROUP_SIZE)), axis=1, keepdims=True)
    m2 = jnp.max(jnp.where(si == f1, neg, sel), axis=1, keepdims=True)
    grp = m1 + m2
    gi = _iota(grp.shape, 0).astype(F32)
    gsel = jnp.zeros(grp.shape, F32)
    for _ in range(TOPK_GROUPS):
        mx = jnp.max(grp, axis=0, keepdims=True)
        fi = jnp.min(jnp.where(grp == mx, gi, float(N_GROUPS)), axis=0, keepdims=True)
        hit = gi == fi
        gsel = jnp.where(hit, 1.0, gsel)
        grp = jnp.where(hit, neg, grp)
    cur = jnp.where(gsel > 0.0, sel, neg)
    ei = (_iota(cur.shape, 0) * GROUP_SIZE + _iota(cur.shape, 1)).astype(F32)
    chosen = jnp.zeros(cur.shape, F32)
    ids, wts = [], []
    for _ in range(TOP_K):
        mx = jnp.max(jnp.max(cur, axis=0, keepdims=True), axis=1, keepdims=True)
        fi = jnp.min(jnp.min(jnp.where(cur == mx, ei, float(N_EXPERTS)), axis=0, keepdims=True),
                     axis=1, keepdims=True)
        hit = ei == fi
        chosen = jnp.where(hit, 1.0, chosen)
        cur = jnp.where(hit, neg, cur)
        ids.append(fi.reshape(1, tm))
        wts.append(jnp.sum(jnp.sum(jnp.where(hit, sc3, 0.0), axis=0, keepdims=True), axis=1, keepdims=True)
                   .reshape(1, tm))
    w = jnp.concatenate(wts, axis=0)
    w = w / jnp.sum(w, axis=0, keepdims=True) * ROUTE_SCALE
    return chosen.reshape(N_EXPERTS, tm), jnp.concatenate(ids, axis=0), w


def _pack_halves(x):
    half = x.shape[1] // 2
    bits = lax.bitcast_convert_type(x.astype(BF16).astype(F32), jnp.int32)
    lo = lax.shift_right_logical(bits[:, :half], jnp.int32(16))
    return jnp.bitwise_or(lo, jnp.bitwise_and(bits[:, half:], jnp.int32(-65536)))


def _unpack_halves(word):
    lo = lax.bitcast_convert_type(lax.shift_left(word, jnp.int32(16)), F32)
    hi = lax.bitcast_convert_type(jnp.bitwise_and(word, jnp.int32(-65536)), F32)
    return lo, hi


def _post_kernel(x_ref, om_ref, og_ref, or_ref, mod_ref, wo_ref, n2_ref, rt_ref, rb_ref, sgu_ref, sdn_ref,
                 tri_ref, cin_ref, x1_ref, h2_ref, eid_ref, rank_ref, ew_ref, cnt_ref, carry_s):
    @pl.when(pl.program_id(0) == 0)
    def _():
        carry_s[...] = cin_ref[...]

    m = mod_ref[0]
    g1 = m[:, 2 * D_MODEL:3 * D_MODEL]
    sh2 = m[:, 3 * D_MODEL:4 * D_MODEL]
    sc2 = m[:, 4 * D_MODEL:5 * D_MODEL]
    w_mla = MLA_HEADS * MLA_V
    mix = (_mm(om_ref[...], wo_ref[0:w_mla, :]) + _mm(og_ref[...], wo_ref[w_mla:w_mla + MIX_W, :])
           + _mm(or_ref[...], wo_ref[w_mla + MIX_W:, :]))
    x1 = x_ref[...] + g1 * mix
    h2 = _rms(x1, n2_ref[...]) * (1.0 + sc2) + sh2
    h2_ref[...] = _pack_halves(h2)
    g2 = m[:, 5 * D_MODEL:]
    x1_ref[...] = x1 + g2 * _mm(_swiglu_act(_mm(h2, sgu_ref[...])), sdn_ref[...])
    r_hi, r_lo = _split2(rt_ref[...])
    h_hi, h_lo = _split2(h2)
    logits_t = _mm_nt(r_hi, h_hi) + _mm_nt(r_hi, h_lo) + _mm_nt(r_lo, h_hi)
    chosen, ids, w = _route(logits_t, rb_ref[...])
    tm = chosen.shape[1]
    rank_et = (carry_s[:, 0:1] + _mm(chosen, tri_ref[...])).reshape(N_GROUPS, GROUP_SIZE, tm)
    ei = (_iota(rank_et.shape, 0) * GROUP_SIZE + _iota(rank_et.shape, 1)).astype(F32)
    ranks = []
    for k in range(TOP_K):
        pick = jnp.where(ei == ids[k:k + 1].reshape(1, 1, tm), rank_et, 0.0)
        ranks.append(jnp.sum(jnp.sum(pick, axis=0, keepdims=True), axis=1, keepdims=True).reshape(1, tm))
    eid_ref[...] = ids.astype(jnp.int32)
    rank_ref[...] = jnp.concatenate(ranks, axis=0).astype(jnp.int32)
    ew_ref[...] = jnp.concatenate([w, jnp.zeros((LANES - TOP_K, tm), F32)], axis=0).T
    total = carry_s[...] + jnp.sum(chosen, axis=1, keepdims=True)
    carry_s[...] = total
    cnt_ref[...] = total


def _post(x2d, om, og, orw, mods, lw, layer, consts, tm, seq_len, mod_base, counts_in):
    n = x2d.shape[0]
    tiles_per_seq = seq_len // tm if mod_base else 1

    def mod_idx(i):
        return (layer, mod_base + i // tiles_per_seq if mod_base else 0, 0, 0)

    row = lambda w: pl.BlockSpec((tm, w), lambda i: (i, 0))
    col = lambda h: pl.BlockSpec((h, tm), lambda i: (0, i))
    full = lambda a: _layer_spec(a, layer)
    tail = [lw["w_out"], lw["norm2"], lw["router_t"], lw["router_b"], lw["shared_w_gu"], lw["shared_w_down"]]
    tri = consts["tri_tokens"]
    return pl.pallas_call(
        _post_kernel,
        grid=(n // tm,),
        in_specs=[row(D_MODEL), row(MLA_HEADS * MLA_V), row(MIX_W), row(MIX_W),
                  pl.BlockSpec((None, 1, 1, 6 * D_MODEL), mod_idx)] + [full(a) for a in tail]
        + [_const_spec(tri), _const_spec(counts_in)],
        out_specs=[row(D_MODEL), row(D_MODEL // 2), col(TOP_K), col(TOP_K), row(LANES),
                   pl.BlockSpec((N_EXPERTS, LANES), lambda i: (0, 0))],
        out_shape=[
            jax.ShapeDtypeStruct((n, D_MODEL), F32),
            jax.ShapeDtypeStruct((n, D_MODEL // 2), jnp.int32),
            jax.ShapeDtypeStruct((TOP_K, n), jnp.int32),
            jax.ShapeDtypeStruct((TOP_K, n), jnp.int32),
            jax.ShapeDtypeStruct((n, LANES), F32),
            jax.ShapeDtypeStruct((N_EXPERTS, LANES), F32),
        ],
        scratch_shapes=[pltpu.VMEM((N_EXPERTS, LANES), F32)],
        compiler_params=pltpu.CompilerParams(dimension_semantics=("arbitrary",),
                                             vmem_limit_bytes=VMEM_LIMIT),
        name="post",
    )(x2d, om, og, orw, mods, *tail, tri, counts_in)


MOE_ROWS = 512
SC_ROWS = 128
SC_SUBCORES = 32


def _swiglu_act(gu):
    return _silu(gu[:, :D_EXPERT]) * gu[:, D_EXPERT:]


def _dispatch_plan(eid, rank, counts, n, rows):
    n_blocks = n * TOP_K // rows + N_EXPERTS
    cnt = counts[:, 0].astype(jnp.int32)
    blocks = (cnt + rows - 1) // rows
    block_end = jnp.cumsum(blocks)
    offset = (block_end - blocks) * rows
    experts = jnp.arange(N_EXPERTS, dtype=jnp.int32)
    dest = jnp.sum(jnp.where(eid[..., None] == experts, offset, 0), axis=-1) + rank
    block_ids = jnp.arange(n_blocks, dtype=jnp.int32)
    block_expert = jnp.minimum(jnp.sum((block_end[None, :] <= block_ids[:, None]).astype(jnp.int32), axis=1),
                               N_EXPERTS - 1)
    n_used = block_end[-1:]
    prev_expert = jnp.concatenate([jnp.full((1,), -1, jnp.int32), block_expert[:-1]])
    first = ((block_ids < n_used[0]) & (block_expert != prev_expert)).astype(jnp.int32)
    slot = jnp.bitwise_and(jnp.cumsum(first) - 1, 1)
    owner_or_none = jnp.where(blocks > 0, experts, N_EXPERTS)
    next_owner = jnp.concatenate([lax.cummin(owner_or_none[::-1])[::-1][1:], jnp.full((1,), N_EXPERTS, jnp.int32)])
    nxt = jnp.sum(jnp.where(block_expert[:, None] == experts, next_owner, 0), axis=1)
    schedule = (block_expert, first, slot.astype(jnp.int32), nxt.astype(jnp.int32))
    return dest, schedule, n_used[0].astype(jnp.int32), n_blocks


def _sc_mesh():
    return plsc.VectorSubcoreMesh(core_axis_name="core", subcore_axis_name="subcore")


def _sc_dispatch(groups, dest, n_rows):
    w = groups[0].shape[1]
    dtype = groups[0].dtype

    @functools.partial(pl.kernel, out_type=jax.ShapeDtypeStruct((n_rows, w), dtype), mesh=_sc_mesh(),
                       scratch_types=[pltpu.VMEM((SC_ROWS, w), dtype), pltpu.VMEM((TOP_K, SC_ROWS), jnp.int32)])
    def kern(*refs):
        x_refs, d_hbm, o_hbm, xv, dv = refs[:len(groups)], *refs[len(groups):]
        sid = lax.axis_index("core") * (SC_SUBCORES // 2) + lax.axis_index("subcore")
        start = 0
        for x_hbm, x in zip(x_refs, groups):
            def chunk(c, x_hbm=x_hbm, start=start):
                r0 = pl.multiple_of(c * SC_ROWS, SC_ROWS)
                pltpu.sync_copy(x_hbm.at[pl.ds(r0, SC_ROWS)], xv)
                pltpu.sync_copy(d_hbm.at[:, pl.ds(start + r0, SC_ROWS)], dv)
                for k in range(TOP_K):
                    pltpu.sync_copy(xv, o_hbm.at[dv.at[k]])

            pl.loop(sid, x.shape[0] // SC_ROWS, step=SC_SUBCORES)(chunk)
            start += x.shape[0]

    return kern(*groups, dest)


def _sc_gather(y, idx):
    w = y.shape[1]
    n_chunks = idx.shape[0]

    @functools.partial(pl.kernel, out_type=jax.ShapeDtypeStruct((n_chunks * SC_ROWS, w), y.dtype), mesh=_sc_mesh(),
                       scratch_types=[pltpu.VMEM((SC_ROWS, w), y.dtype), pltpu.VMEM((1, SC_ROWS), jnp.int32)])
    def kern(y_hbm, i_hbm, o_hbm, ov, iv):
        sid = lax.axis_index("core") * (SC_SUBCORES // 2) + lax.axis_index("subcore")

        @pl.loop(sid, n_chunks, step=SC_SUBCORES)
        def _(c):
            pltpu.sync_copy(i_hbm.at[pl.ds(c, 1)], iv)
            pltpu.sync_copy(y_hbm.at[iv.at[0]], ov)
            pltpu.sync_copy(ov, o_hbm.at[pl.ds(pl.multiple_of(c * SC_ROWS, SC_ROWS), SC_ROWS)])

    return kern(y, idx)


def _moe_rows_kernel(be_ref, first_ref, slot_ref, nxt_ref, x_ref, wgu_hbm, wdn_hbm, y_ref,
                     wgu_f, wdn_f, wgu_b, wdn_b, sem, *, layer):
    b = pl.program_id(0)

    def weight_copies(expert, slot):
        return (pltpu.make_async_copy(wgu_hbm.at[layer, expert], wgu_f.at[slot], sem.at[slot, 0]),
                pltpu.make_async_copy(wdn_hbm.at[layer, expert], wdn_f.at[slot], sem.at[slot, 1]))

    @pl.when(b == 0)
    def _():
        for copy in weight_copies(be_ref[0], 0):
            copy.start()

    @pl.when(first_ref[b] == 1)
    def _():
        slot = slot_ref[b]
        for copy in weight_copies(be_ref[b], slot):
            copy.wait()

        @pl.when(nxt_ref[b] < N_EXPERTS)
        def _():
            for copy in weight_copies(nxt_ref[b], 1 - slot):
                copy.start()

        wgu_b[...] = wgu_f[slot].astype(BF16)
        wdn_b[...] = wdn_f[slot].astype(BF16)

    half = D_MODEL // 2
    lo, hi = _unpack_halves(x_ref[...])
    gu = _mm(lo, wgu_b[0:half, :]) + _mm(hi, wgu_b[half:, :])
    y_ref[...] = _pack_halves(_mm(_swiglu_act(gu), wdn_b[...]))


def _moe_rows(xs, schedule, n_used, lw, layer, rows):
    half = D_MODEL // 2
    row_spec = pl.BlockSpec((rows, half), lambda b, *_: (b, 0))
    return pl.pallas_call(
        functools.partial(_moe_rows_kernel, layer=layer),
        grid_spec=pltpu.PrefetchScalarGridSpec(
            num_scalar_prefetch=4,
            grid=(n_used,),
            in_specs=[row_spec, pl.BlockSpec(memory_space=pl.ANY), pl.BlockSpec(memory_space=pl.ANY)],
            out_specs=row_spec,
            scratch_shapes=[
                pltpu.VMEM((2, D_MODEL, 2 * D_EXPERT), F32),
                pltpu.VMEM((2, D_EXPERT, D_MODEL), F32),
                pltpu.VMEM((D_MODEL, 2 * D_EXPERT), BF16),
                pltpu.VMEM((D_EXPERT, D_MODEL), BF16),
                pltpu.SemaphoreType.DMA((2, 2)),
            ],
        ),
        out_shape=jax.ShapeDtypeStruct(xs.shape, jnp.int32),
        compiler_params=pltpu.CompilerParams(dimension_semantics=("arbitrary",),
                                             vmem_limit_bytes=VMEM_LIMIT),
        name="moe_rows",
    )(*schedule, xs, lw["moe_w_gu"], lw["moe_w_down"])


def _moe_combine_kernel(yg_ref, ew_ref, x1_ref, mod_ref, nf_ref, o_ref, *, final):
    ew = ew_ref[...]
    acc_lo = acc_hi = None
    for k in range(TOP_K):
        lo, hi = _unpack_halves(yg_ref[k])
        wk = ew[:, k:k + 1]
        acc_lo = wk * lo if acc_lo is None else acc_lo + wk * lo
        acc_hi = wk * hi if acc_hi is None else acc_hi + wk * hi
    g2 = mod_ref[0][:, 5 * D_MODEL:]
    x2 = x1_ref[...] + g2 * jnp.concatenate([acc_lo, acc_hi], axis=1)
    if final:
        x2 = _rms(x2, nf_ref[...])
    o_ref[...] = x2


def _moe_combine(yg, row0, ew, x1, mods, layer, norm_f, tm, seq_len, mod_base, final):
    n = x1.shape[0]
    half = D_MODEL // 2
    tiles_per_seq = seq_len // tm if mod_base else 1
    tile0 = row0 // tm

    def mod_idx(i):
        return (layer, mod_base + i // tiles_per_seq if mod_base else 0, 0, 0)

    row = lambda w: pl.BlockSpec((tm, w), lambda i: (i, 0))
    return pl.pallas_call(
        functools.partial(_moe_combine_kernel, final=final),
        grid=(n // tm,),
        in_specs=[pl.BlockSpec((TOP_K, tm, half), lambda i: (0, tile0 + i, 0)), row(LANES), row(D_MODEL),
                  pl.BlockSpec((None, 1, 1, 6 * D_MODEL), mod_idx), _const_spec(norm_f)],
        out_specs=row(D_MODEL),
        out_shape=jax.ShapeDtypeStruct((n, D_MODEL), F32),
        compiler_params=pltpu.CompilerParams(dimension_semantics=("arbitrary",),
                                             vmem_limit_bytes=VMEM_LIMIT),
        name="moe_combine_final" if final else "moe_combine",
    )(yg, ew, x1, mods, norm_f)


def _moe_experts(groups, eid, rank, counts, lw, layer):
    group_sizes = [g.shape[0] for g in groups]
    n = sum(group_sizes)
    dest, schedule, n_used, n_blocks = _dispatch_plan(eid, rank, counts, n, MOE_ROWS)
    xs = _sc_dispatch(groups, dest, n_blocks * MOE_ROWS)
    y = _moe_rows(xs, schedule, n_used, lw, layer, MOE_ROWS)
    outs, start = [], 0
    for size in group_sizes:
        idx = dest[:, start:start + size].reshape(size * TOP_K // SC_ROWS, SC_ROWS)
        outs.append(_sc_gather(y, idx).reshape(TOP_K, size, D_MODEL // 2))
        start += size
    return outs


def _constants():
    idx = np.arange(MIX_W)
    same_head = (idx[:, None] // HEAD_DIM) == (idx[None, :] // HEAD_DIM)
    maskbd = jnp.asarray(same_head, BF16)
    eexp = np.zeros((LANES, 4 * MIX_W), np.float32)
    for blk in range(4):
        kind, d = divmod(blk, 2)
        for h in range(HEADS):
            src = kind * 2 * HEADS + d * HEADS + h
            eexp[src, blk * MIX_W + h * HEAD_DIM: blk * MIX_W + (h + 1) * HEAD_DIM] = 1.0
    tri = np.triu(np.ones((POST_TM, POST_TM), np.float32), 1)
    return {"maskbd": maskbd, "ones_bd": maskbd, "eexp": jnp.asarray(eexp, BF16), "tri_tokens": jnp.asarray(tri, BF16)}


def _rope_tables(n):
    rows = n // GRID_W
    row = jnp.repeat(jnp.arange(rows, dtype=F32), GRID_W)
    col = jnp.tile(jnp.arange(GRID_W, dtype=F32), rows)
    axis_dim = MLA_ROPE // 2
    inv = jnp.power(ROPE_BASE, -jnp.arange(0, axis_dim, 2, dtype=F32) / axis_dim)
    ang_r = row[:, None] * inv
    ang_c = col[:, None] * inv
    cr, sr, cc, sc = jnp.cos(ang_r), jnp.sin(ang_r), jnp.cos(ang_c), jnp.sin(ang_c)
    zeros = jnp.zeros((n, LANES - MLA_ROPE), F32)
    cos_t = jnp.concatenate([cr, cr, cc, cc, zeros], axis=1)
    sin_t = jnp.concatenate([-sr, sr, -sc, sc, zeros], axis=1)
    return cos_t, sin_t


def _stacked_weights(p):
    def per_direction(w):
        half = jnp.zeros((DEPTH, 64, MIX_W), F32)
        return jnp.stack([jnp.concatenate([w[:, 0], half], axis=1),
                          jnp.concatenate([half, w[:, 1]], axis=1)], axis=1).astype(BF16)

    row = lambda v: v.reshape(DEPTH, 1, -1)
    pad_row = lambda v: jnp.pad(row(v), ((0, 0), (0, 0), (0, LANES - 2 * HEADS)))
    return {
        "norm1": row(p["norm1"]),
        "w_in": p["w_in"],
        "q_norm": row(p["mla_q_norm"]),
        "w_uq": p["mla_w_uq"],
        "kv_norm": row(p["mla_kv_norm"]),
        "w_ukv": p["mla_w_ukv"].astype(BF16),
        "gdn_conv": p["gdn_conv"],
        "gdn_alog": pad_row(p["gdn_a_log"]),
        "gdn_dtb": pad_row(p["gdn_dt_bias"]),
        "gdn_norm": jnp.tile(row(p["gdn_norm"]), (1, 1, HEADS)),
        "rwkv_mu_prev": row(p["rwkv_mu_prev"]),
        "rwkv_mu_next": row(p["rwkv_mu_next"]),
        "rwkv_w0": p["rwkv_w0"],
        "rwkv_w2": per_direction(p["rwkv_w2"]),
        "rwkv_a0": p["rwkv_a0"],
        "rwkv_a2": per_direction(p["rwkv_a2"]),
        "rwkv_g2": p["rwkv_g2"].astype(BF16),
        "rwkv_k_k": row(p["rwkv_k_k"]),
        "rwkv_k_a": row(p["rwkv_k_a"]),
        "rwkv_r_k": row(p["rwkv_r_k"]),
        "rwkv_gn_w": row(p["rwkv_gn_w"]),
        "rwkv_gn_b": row(p["rwkv_gn_b"]),
        "w_out": p["w_out"].astype(BF16),
        "norm2": row(p["norm2"]),
        "router_t": jnp.swapaxes(p["moe_router"], 1, 2),
        "router_b": p["moe_bias"].reshape(DEPTH, N_EXPERTS, 1),
        "moe_w_gu": p["moe_w_gu"],
        "moe_w_down": p["moe_w_down"],
        "shared_w_gu": p["shared_w_gu"].astype(BF16),
        "shared_w_down": p["shared_w_down"].astype(BF16),
    }


def _embed_block_diag(s):
    b = s.shape[0]
    eye = jnp.eye(HEADS, dtype=s.dtype)
    out = jnp.einsum("bdhkv,hg->bdhkgv", s, eye)
    return out.reshape(b, 2, MIX_W, MIX_W)


def _layer_front(x2d, mods, lw, l, consts, seq_len, mod_base, cache, tm, tq, counts_in, prev=None):
    zm, zg, zab, zr = _inproj(x2d, mods, lw, l, tm, seq_len, mod_base)
    if cache is None:
        o_mla, ckv, kpe = _mla_ctx(zm, lw, l, seq_len, prev[:2])
        o_gdn, s_gdn = _gdn(zg, zab, lw, l, consts, seq_len, prev=prev[2])
        o_rwkv, s_rwkv = _rwkv(zr, lw, l, consts, seq_len, prev=prev[3])
        new = (ckv, kpe, s_gdn, s_rwkv)
    else:
        cckv, ckpe, rc, rs, sg, sr = cache
        o_mla = _mla_lat(zm, lw, l, seq_len, tq, (cckv, ckpe, rc, rs))
        o_gdn, _ = _gdn(zg, zab, lw, l, consts, seq_len, sg)
        o_rwkv, _ = _rwkv(zr, lw, l, consts, seq_len, sr)
        new = None
    routed = _post(x2d, o_mla, o_gdn, o_rwkv, mods, lw, l, consts, tm, seq_len, mod_base, counts_in)
    return routed, new


def kernel(x_prompt, x_sample, cache_mla_ckv, cache_mla_kpe, state_gdn, state_rwkv, c, c_ctx, ada_w, ada_b, norm1, w_in, mla_q_norm, mla_w_uq, mla_kv_norm, mla_w_ukv, gdn_conv, gdn_a_log, gdn_dt_bias, gdn_norm, rwkv_mu_prev, rwkv_mu_next, rwkv_w0, rwkv_w2, rwkv_a0, rwkv_a2, rwkv_g2, rwkv_k_k, rwkv_k_a, rwkv_r_k, rwkv_gn_w, rwkv_gn_b, w_out, norm2, moe_router, moe_bias, moe_w_gu, moe_w_down, shared_w_gu, shared_w_down, norm_f):
    p = dict(norm1=norm1, w_in=w_in, mla_q_norm=mla_q_norm, mla_w_uq=mla_w_uq, mla_kv_norm=mla_kv_norm,
             mla_w_ukv=mla_w_ukv, gdn_conv=gdn_conv, gdn_a_log=gdn_a_log, gdn_dt_bias=gdn_dt_bias,
             gdn_norm=gdn_norm, rwkv_mu_prev=rwkv_mu_prev, rwkv_mu_next=rwkv_mu_next, rwkv_w0=rwkv_w0,
             rwkv_w2=rwkv_w2, rwkv_a0=rwkv_a0, rwkv_a2=rwkv_a2, rwkv_g2=rwkv_g2, rwkv_k_k=rwkv_k_k,
             rwkv_k_a=rwkv_k_a, rwkv_r_k=rwkv_r_k, rwkv_gn_w=rwkv_gn_w, rwkv_gn_b=rwkv_gn_b, w_out=w_out,
             norm2=norm2, moe_router=moe_router, moe_bias=moe_bias, moe_w_gu=moe_w_gu, moe_w_down=moe_w_down,
             shared_w_gu=shared_w_gu, shared_w_down=shared_w_down)
    weights = _stacked_weights(p)
    consts = _constants()
    nf = norm_f.reshape(1, D_MODEL)
    b_ctx, t_ctx, _ = x_prompt.shape
    b_lat, t_lat, _ = x_sample.shape

    cvec8 = jnp.concatenate([c_ctx[None, :], c, jnp.zeros((8 - 1 - b_lat, D_MODEL), F32)], axis=0)
    mods = _adaln(cvec8, ada_w, ada_b)
    mods = mods.reshape(DEPTH, 8, 1, 6 * D_MODEL)

    rc, rs = _rope_tables(t_lat)
    ckpe = jnp.pad(cache_mla_kpe, ((0, 0), (0, 0), (0, 0), (0, LANES - MLA_ROPE)))
    cache = (cache_mla_ckv, ckpe, rc, rs, _embed_block_diag_layers(state_gdn),
             _embed_block_diag_layers(jnp.swapaxes(state_rwkv, -1, -2)))
    xp = x_prompt.reshape(b_ctx * t_ctx, D_MODEL)
    xs = x_sample.reshape(b_lat * t_lat, D_MODEL)
    n_ctx = xp.shape[0]
    tm = POST_TM
    state_shape = (b_ctx, DEPTH, 2, HEADS, HEAD_DIM, HEAD_DIM)
    ctx_outs = (jnp.zeros((b_ctx, DEPTH, t_ctx, KV_LORA), F32), jnp.zeros((b_ctx, DEPTH, t_ctx, MLA_ROPE), F32),
                jnp.zeros(state_shape, F32), jnp.zeros(state_shape, F32))
    for l in range(DEPTH):
        final = l == DEPTH - 1
        no_pairs = jnp.zeros((N_EXPERTS, LANES), F32)
        (x1c, hc, eidc, rankc, ewc, cnt_c), ctx_outs = _layer_front(xp, mods, weights, l, consts, t_ctx, 0, None,
                                                                    tm, t_ctx, no_pairs, ctx_outs)
        (x1s, hs, eids, ranks, ews, cnt), _ = _layer_front(xs, mods, weights, l, consts, t_lat, 1, cache,
                                                           tm, 256, cnt_c)
        yg_c, yg_s = _moe_experts([hc, hs], jnp.concatenate([eidc, eids], axis=1),
                                  jnp.concatenate([rankc, ranks], axis=1), cnt, weights, l)
        xp = _moe_combine(yg_c, 0, ewc, x1c, mods, l, nf, tm, t_ctx, 0, final)
        xs = _moe_combine(yg_s, 0, ews, x1s, mods, l, nf, tm, t_lat, 1, final)

    y_prompt = xp.reshape(b_ctx, t_ctx, D_MODEL)
    y_sample = xs.reshape(b_lat, t_lat, D_MODEL)
    new_ckv, new_kpe, new_gdn, new_rwkv = ctx_outs
    return (y_prompt, y_sample, new_ckv, new_kpe, new_gdn, new_rwkv)


def _embed_block_diag_layers(s):
    b = s.shape[0]
    return _embed_block_diag(s.reshape(b * DEPTH, 2, HEADS, HEAD_DIM, HEAD_DIM)).reshape(
        b, DEPTH, 2, MIX_W, MIX_W)
```

```python
import functools

import numpy as np
import jax
import jax.numpy as jnp
from jax import lax
from jax.experimental import pallas as pl
from jax.experimental.pallas import tpu as pltpu
from jax.experimental.pallas import tpu_sc as plsc

F32 = jnp.float32
BF16 = jnp.bfloat16

D_MODEL = 1024
DEPTH = 2
PAST_LEN = 512
GRID_W = 64
NORM_EPS = 1e-6

MLA_HEADS = 4
MLA_NOPE = 128
MLA_ROPE = 64
MLA_V = 128
Q_LORA = 384
KV_LORA = 256
ROPE_BASE = 10000.0
MLA_SCALE = (MLA_NOPE + MLA_ROPE) ** -0.5

HEADS = 4
HEAD_DIM = 64
MIX_W = HEADS * HEAD_DIM
GDN_CONV_CH = 3 * MIX_W
CHUNK = 64
RWKV_GN_EPS = 64e-5

N_EXPERTS = 64
TOP_K = 8
N_GROUPS = 8
GROUP_SIZE = N_EXPERTS // N_GROUPS
TOPK_GROUPS = 4
D_EXPERT = 256
ROUTE_SCALE = 2.5

P_MLA = Q_LORA + KV_LORA + MLA_ROPE
P_GDN = GDN_CONV_CH + MIX_W + 4 * HEADS
P_RWKV = 3 * MIX_W + 128 + 128 + 128

LANES = 128
ZM_W = Q_LORA + KV_LORA + 2 * LANES
ZG_W = GDN_CONV_CH + MIX_W
ZR_W = P_RWKV
QH_W = 2 * LANES
VMEM_LIMIT = 56 * 1024 * 1024
POST_TM = 512

_ROPE_SWAP = np.concatenate([np.arange(16, 32), np.arange(0, 16), np.arange(48, 64), np.arange(32, 48)])


def _sigmoid(x):
    return 1.0 / (1.0 + jnp.exp(-x))


def _silu(x):
    return x * _sigmoid(x)


def _softplus(x):
    return jnp.maximum(x, 0.0) + jnp.log(1.0 + jnp.exp(-jnp.abs(x)))


def _rms(x, g, eps=NORM_EPS):
    return x * lax.rsqrt(jnp.mean(x * x, axis=-1, keepdims=True) + eps) * g


def _mm(a, b):
    return jnp.dot(a.astype(BF16), b.astype(BF16), preferred_element_type=F32)


def _mm_nt(a, b):
    return lax.dot_general(a.astype(BF16), b.astype(BF16), (((1,), (1,)), ((), ())),
                           preferred_element_type=F32)


def _mm_tn(a, b):
    return lax.dot_general(a.astype(BF16), b.astype(BF16), (((0,), (0,)), ((), ())),
                           preferred_element_type=F32)


def _split3(x):
    p1 = x.astype(BF16)
    r1 = x - p1.astype(F32)
    p2 = r1.astype(BF16)
    r2 = r1 - p2.astype(F32)
    return p1, p2, r2.astype(BF16)


def _mm_sel_l(sel, x):
    p1, p2, p3 = _split3(x)
    return _mm(sel, p1) + _mm(sel, p2) + _mm(sel, p3)


def _mm_sel_r(x, sel):
    p1, p2, p3 = _split3(x)
    return _mm(p1, sel) + _mm(p2, sel) + _mm(p3, sel)


def _iota(shape, dim):
    return lax.broadcasted_iota(jnp.int32, shape, dim)


def _layer_spec(a, layer, **kw):
    nd = a.ndim - 1
    return pl.BlockSpec((None,) + a.shape[1:], lambda *_: (layer,) + (0,) * nd, **kw)


def _const_spec(a, **kw):
    return pl.BlockSpec(a.shape, lambda *_: (0,) * a.ndim, **kw)


def _bd(x, maskbd):
    xb = x.astype(BF16)
    return jnp.concatenate([xb] * HEADS, axis=0) * maskbd


def _chunk_masks(rev):
    row = _iota((CHUNK, MIX_W), 0)
    col = jnp.bitwise_and(_iota((CHUNK, MIX_W), 1), HEAD_DIM - 1)
    r2 = _iota((CHUNK, CHUNK), 0)
    c2 = _iota((CHUNK, CHUNK), 1)
    if rev:
        inc, strict, tri = row <= col, row < col, r2 <= c2
    else:
        inc, strict, tri = row >= col, row > col, r2 >= c2
    eye = jnp.where(row == col, 1.0, 0.0).astype(F32)
    return inc, strict, jnp.where(tri, 1.0, 0.0).astype(BF16), eye


def _split2(x):
    hi = x.astype(BF16)
    return hi, (x - hi.astype(F32)).astype(BF16)


def _mm_bd3(x, p, maskbd):
    n = x.shape[0]
    xh, xl = _split2(x)
    ph, pl_ = _split2(p)
    r = jnp.dot(jnp.concatenate([xh, xl], axis=0), _bd(ph, maskbd), preferred_element_type=F32)
    return r[:n] + r[n:] + jnp.dot(xh, _bd(pl_, maskbd), preferred_element_type=F32)


def _neumann_inverse(a_list, eye_list, maskbd):
    bs = [-a for a in a_list]
    ms = [eye + b for eye, b in zip(eye_list, bs)]
    ps = [_mm_bd3(b, b, maskbd) for b in bs]
    for _ in range(4):
        boths = [_mm_bd3(jnp.concatenate([m, p], axis=0), p, maskbd) for m, p in zip(ms, ps)]
        ms = [m + both[:CHUNK] for m, both in zip(ms, boths)]
        ps = [both[CHUNK:] for both in boths]
    return [m + _mm_bd3(m, p, maskbd) for m, p in zip(ms, ps)]


def _adaln_kernel(c_ref, w_ref, b_ref, o_ref):
    cv = c_ref[...]
    o_ref[0] = _mm(_silu(cv), w_ref[0]) + b_ref[0]


def _adaln(cvec8, ada_w, ada_b):
    tn = 768
    n_out = 6 * D_MODEL
    return pl.pallas_call(
        _adaln_kernel,
        grid=(DEPTH, n_out // tn),
        in_specs=[
            pl.BlockSpec((8, D_MODEL), lambda l, j: (0, 0)),
            pl.BlockSpec((1, D_MODEL, tn), lambda l, j: (l, 0, j)),
            pl.BlockSpec((1, 1, tn), lambda l, j: (l, 0, j)),
        ],
        out_specs=pl.BlockSpec((1, 8, tn), lambda l, j: (l, 0, j)),
        out_shape=jax.ShapeDtypeStruct((DEPTH, 8, n_out), F32),
        compiler_params=pltpu.CompilerParams(dimension_semantics=("arbitrary", "arbitrary"),
                                             vmem_limit_bytes=VMEM_LIMIT),
        name="adaln",
    )(cvec8, ada_w, ada_b.reshape(DEPTH, 1, n_out))


_KPE0 = Q_LORA + KV_LORA
_W_IN_MOVES = (
    [(0, 0, P_MLA)]
    + [(P_MLA + LANES - MLA_ROPE + 16 * j, _KPE0 + 16 * int(_ROPE_SWAP[16 * j] // 16), 16) for j in range(4)]
    + [(ZM_W, P_MLA, ZG_W), (ZM_W + ZG_W, P_MLA + ZG_W, 4 * HEADS), (ZM_W + ZG_W + LANES, P_MLA + P_GDN, P_RWKV)]
)
W_IN_PAD = ZM_W + ZG_W + LANES + ZR_W


def _inproj_kernel(x_ref, mod_ref, n1_ref, w_ref, zm_ref, zg_ref, zab_ref, zr_ref, w_s):
    @pl.when(pl.program_id(0) == 0)
    def _():
        w_s[...] = jnp.zeros(w_s.shape, BF16)
        for dst, src, width in _W_IN_MOVES:
            w_s[:, dst:dst + width] = w_ref[:, src:src + width].astype(BF16)

    m = mod_ref[0]
    sh = m[:, 0:D_MODEL]
    sc = m[:, D_MODEL:2 * D_MODEL]
    h = _rms(x_ref[...], n1_ref[...]) * (1.0 + sc) + sh
    z = _mm(h, w_s[...])
    o1 = ZM_W
    o2 = o1 + ZG_W
    o3 = o2 + LANES
    zm_ref[...] = z[:, :o1]
    zg_ref[...] = z[:, o1:o2]
    zab_ref[...] = z[:, o2:o3]
    zr_ref[...] = z[:, o3:]


def _inproj(x2d, mods, lw, layer, tm, seq_len, mod_base):
    n = x2d.shape[0]
    tiles_per_seq = seq_len // tm if mod_base else 1

    def mod_idx(i):
        return (layer, mod_base + i // tiles_per_seq if mod_base else 0, 0, 0)

    return pl.pallas_call(
        _inproj_kernel,
        grid=(n // tm,),
        in_specs=[
            pl.BlockSpec((tm, D_MODEL), lambda i: (i, 0)),
            pl.BlockSpec((None, 1, 1, 6 * D_MODEL), mod_idx),
            _layer_spec(lw["norm1"], layer),
            _layer_spec(lw["w_in"], layer, pipeline_mode=pl.Buffered(1)),
        ],
        out_specs=[
            pl.BlockSpec((tm, ZM_W), lambda i: (i, 0)),
            pl.BlockSpec((tm, ZG_W), lambda i: (i, 0)),
            pl.BlockSpec((tm, LANES), lambda i: (i, 0)),
            pl.BlockSpec((tm, ZR_W), lambda i: (i, 0)),
        ],
        out_shape=[
            jax.ShapeDtypeStruct((n, ZM_W), F32),
            jax.ShapeDtypeStruct((n, ZG_W), F32),
            jax.ShapeDtypeStruct((n, LANES), F32),
            jax.ShapeDtypeStruct((n, ZR_W), F32),
        ],
        scratch_shapes=[pltpu.VMEM((D_MODEL, W_IN_PAD), BF16)],
        compiler_params=pltpu.CompilerParams(dimension_semantics=("arbitrary",),
                                             vmem_limit_bytes=VMEM_LIMIT),
        name="inproj",
    )(x2d, mods, lw["norm1"], lw["w_in"])


MLA_SEQ_PER_STEP = 4
Q_HEAD_W = MLA_NOPE + MLA_ROPE


def _arrange_w_uq(w_ref, wq_s, wqs_s=None):
    wq_s[...] = jnp.zeros(wq_s.shape, BF16)
    if wqs_s is not None:
        wqs_s[...] = jnp.zeros(wqs_s.shape, BF16)
    for h in range(MLA_HEADS):
        src, dst = h * Q_HEAD_W, h * QH_W
        wq_s[:, dst:dst + Q_HEAD_W] = w_ref[:, src:src + Q_HEAD_W].astype(BF16)
        if wqs_s is not None:
            for j in range(4):
                s0 = src + MLA_NOPE + 16 * int(_ROPE_SWAP[16 * j] // 16)
                wqs_s[:, dst + MLA_NOPE + 16 * j:dst + MLA_NOPE + 16 * (j + 1)] = w_ref[:, s0:s0 + 16].astype(BF16)


def _mla_ctx_kernel(zm_ref, qn_ref, wuq_ref, kvn_ref, wukv_ref, prev_ckv, prev_kpe, o_ref, ckv_ref, kpe_ref, wq_s,
                    *, seq_len, n_seq):
    del prev_ckv, prev_kpe

    @pl.when(pl.program_id(0) == 0)
    def _():
        _arrange_w_uq(wuq_ref, wq_s)

    o_kpe = Q_LORA + KV_LORA
    zm = zm_ref[...]
    ckv = _rms(zm[:, Q_LORA:o_kpe], kvn_ref[...])
    kpe = zm[:, o_kpe:o_kpe + LANES]
    kv = _mm(ckv, wukv_ref[...])
    q = _mm(_rms(zm[:, :Q_LORA], qn_ref[...]), wq_s[...])
    kpe_b = kpe.astype(BF16)
    seqs = [slice(i * seq_len, (i + 1) * seq_len) for i in range(n_seq)]
    for i, r in enumerate(seqs):
        ckv_ref[i] = ckv[r]
        kpe_ref[i] = kpe[r, :MLA_ROPE]
    pairs = [(r, h * QH_W) for r in seqs for h in range(MLA_HEADS)]
    scores = [_mm_nt(q[r, c0:c0 + QH_W], jnp.concatenate([kv[r, c0:c0 + LANES].astype(BF16), kpe_b[r]], axis=1))
              * MLA_SCALE for r, c0 in pairs]
    exps = [jnp.exp(s - jnp.max(s, axis=-1, keepdims=True)) for s in scores]
    outs = [_mm(e, kv[r, c0 + LANES:c0 + QH_W]) / jnp.sum(e, axis=-1, keepdims=True)
            for e, (r, c0) in zip(exps, pairs)]
    for o, (r, c0) in zip(outs, pairs):
        h = c0 // QH_W
        o_ref[r, h * MLA_V:(h + 1) * MLA_V] = o


def _mla_ctx(zm, lw, layer, seq_len, prev):
    n = zm.shape[0]
    nb = n // seq_len
    n_seq = MLA_SEQ_PER_STEP
    rows = n_seq * seq_len
    weights = [lw["q_norm"], lw["w_uq"], lw["kv_norm"], lw["w_ukv"]]
    return pl.pallas_call(
        functools.partial(_mla_ctx_kernel, seq_len=seq_len, n_seq=n_seq),
        grid=(nb // n_seq,),
        in_specs=[pl.BlockSpec((rows, ZM_W), lambda b: (b, 0))] + [_layer_spec(a, layer) for a in weights]
        + [pl.BlockSpec(memory_space=pl.ANY)] * 2,
        out_specs=[
            pl.BlockSpec((rows, MLA_HEADS * MLA_V), lambda b: (b, 0)),
            pl.BlockSpec((n_seq, None, seq_len, KV_LORA), lambda b: (b, layer, 0, 0)),
            pl.BlockSpec((n_seq, None, seq_len, MLA_ROPE), lambda b: (b, layer, 0, 0)),
        ],
        out_shape=[
            jax.ShapeDtypeStruct((n, MLA_HEADS * MLA_V), F32),
            jax.ShapeDtypeStruct((nb, DEPTH, seq_len, KV_LORA), F32),
            jax.ShapeDtypeStruct((nb, DEPTH, seq_len, MLA_ROPE), F32),
        ],
        input_output_aliases={5: 1, 6: 2},
        scratch_shapes=[pltpu.VMEM((Q_LORA, MLA_HEADS * QH_W), BF16)],
        compiler_params=pltpu.CompilerParams(dimension_semantics=("arbitrary",),
                                             vmem_limit_bytes=VMEM_LIMIT),
        name="mla_ctx",
    )(zm, *weights, *prev)


def _mla_lat_kernel(zm_ref, cckv_ref, ckpe_ref, rc_ref, rs_ref, qn_ref, wuq_ref, kvn_ref, wukv_ref,
                    o_ref, k_s, v_s, wq_s, wqs_s, *, seq_len, tq, past):
    qi = pl.program_id(1)
    o_kpe = Q_LORA + KV_LORA

    @pl.when((pl.program_id(0) == 0) & (qi == 0))
    def _():
        _arrange_w_uq(wuq_ref, wq_s, wqs_s)

    @pl.when(qi == 0)
    def _():
        zm = zm_ref[...]
        ckv = _rms(zm[:, Q_LORA:o_kpe], kvn_ref[...])
        kpe = zm[:, o_kpe:o_kpe + LANES] * rc_ref[...] + zm[:, o_kpe + LANES:o_kpe + 2 * LANES] * rs_ref[...]
        kvc = _mm(cckv_ref[0], wukv_ref[...])
        kpc = ckpe_ref[0].astype(BF16)
        kv = _mm(ckv, wukv_ref[...])
        kpe = kpe.astype(BF16)
        for h in range(MLA_HEADS):
            c0 = h * QH_W
            k_s[0:past, c0:c0 + LANES] = kvc[:, c0:c0 + LANES].astype(BF16)
            k_s[0:past, c0 + LANES:c0 + QH_W] = kpc
            v_s[0:past, h * MLA_V:(h + 1) * MLA_V] = kvc[:, c0 + LANES:c0 + QH_W].astype(BF16)
            k_s[past:past + seq_len, c0:c0 + LANES] = kv[:, c0:c0 + LANES].astype(BF16)
            k_s[past:past + seq_len, c0 + LANES:c0 + QH_W] = kpe
            v_s[past:past + seq_len, h * MLA_V:(h + 1) * MLA_V] = kv[:, c0 + LANES:c0 + QH_W].astype(BF16)

    r0 = pl.multiple_of(qi * tq, tq)
    zq = zm_ref[pl.ds(r0, tq), :]
    cq = _rms(zq[:, :Q_LORA], qn_ref[...])
    q = _mm(cq, wq_s[...])
    qs = _mm(cq, wqs_s[...])
    qc = jnp.concatenate([jnp.ones((tq, LANES), F32), rc_ref[pl.ds(r0, tq), :]], axis=1)
    qsn = jnp.concatenate([jnp.zeros((tq, LANES), F32), rs_ref[pl.ds(r0, tq), :]], axis=1)
    for h in range(MLA_HEADS):
        c0 = h * QH_W
        qh = q[:, c0:c0 + QH_W] * qc + qs[:, c0:c0 + QH_W] * qsn
        s = _mm_nt(qh, k_s[:, c0:c0 + QH_W]) * MLA_SCALE
        e = jnp.exp(s - jnp.max(s, axis=-1, keepdims=True))
        den = jnp.sum(e, axis=-1, keepdims=True)
        o_ref[:, h * MLA_V:(h + 1) * MLA_V] = _mm(e, v_s[:, h * MLA_V:(h + 1) * MLA_V]) / den


def _mla_lat(zm, lw, layer, seq_len, tq, cache):
    n = zm.shape[0]
    nb = n // seq_len
    past = PAST_LEN
    tk = past + seq_len
    cckv, ckpe, rc, rs = cache
    weights = [lw["q_norm"], lw["w_uq"], lw["kv_norm"], lw["w_ukv"]]
    return pl.pallas_call(
        functools.partial(_mla_lat_kernel, seq_len=seq_len, tq=tq, past=past),
        grid=(nb, seq_len // tq),
        in_specs=[pl.BlockSpec((seq_len, ZM_W), lambda b, q: (b, 0)),
                  pl.BlockSpec((1, None, past, KV_LORA), lambda b, q: (b, layer, 0, 0)),
                  pl.BlockSpec((1, None, past, LANES), lambda b, q: (b, layer, 0, 0)),
                  _const_spec(rc), _const_spec(rs)] + [_layer_spec(a, layer) for a in weights],
        out_specs=pl.BlockSpec((tq, MLA_HEADS * MLA_V), lambda b, q: (b * (seq_len // tq) + q, 0)),
        out_shape=jax.ShapeDtypeStruct((n, MLA_HEADS * MLA_V), F32),
        scratch_shapes=[
            pltpu.VMEM((tk, MLA_HEADS * QH_W), BF16),
            pltpu.VMEM((tk, MLA_HEADS * MLA_V), BF16),
            pltpu.VMEM((Q_LORA, MLA_HEADS * QH_W), BF16),
            pltpu.VMEM((Q_LORA, MLA_HEADS * QH_W), BF16),
        ],
        compiler_params=pltpu.CompilerParams(dimension_semantics=("arbitrary", "arbitrary"),
                                             vmem_limit_bytes=VMEM_LIMIT),
        name="mla_lat",
    )(zm, cckv, ckpe, rc, rs, *weights)


SEQ_PER_STEP = 4
CHUNK_GROUP = 4
GDN_PRE_W = 5 * MIX_W


def _for_chunk_groups(n_chunks, fn):
    if n_chunks == CHUNK_GROUP:
        fn(0)
    else:
        def body(gi, carry):
            fn(gi * CHUNK_GROUP)
            return carry
        lax.fori_loop(0, n_chunks // CHUNK_GROUP, body, 0)


def _gdn_prepare(items, maskbd):
    n = range(len(items))
    qs, ks, vs, gs, betas, masks, revs = zip(*items)
    gcs = [_mm_sel_l(masks[i][2], gs[i]) for i in n]
    decays = []
    for i in n:
        inc, eye = masks[i][0], masks[i][3]
        gc_row = jnp.sum(eye * gcs[i], axis=0, keepdims=True)
        decays.append(jnp.where(inc, jnp.exp(jnp.where(inc, gcs[i] - gc_row, 0.0)), 0.0))
    kbs = [ks[i] * betas[i] for i in n]
    aqs = [_mm_nt(jnp.concatenate([kbs[i], qs[i]], axis=0), _bd(ks[i], maskbd)) for i in n]
    a_mats = [jnp.where(masks[i][1], aqs[i][:CHUNK] * decays[i], 0.0) for i in n]
    t_invs = _neumann_inverse(a_mats, [m[3] for m in masks], maskbd)
    egcs = [jnp.exp(gc) for gc in gcs]
    uws = [_mm(t_invs[i], jnp.concatenate([_bd(vs[i] * betas[i], maskbd), _bd(kbs[i] * egcs[i], maskbd)], axis=1))
           for i in n]
    out = []
    for i in n:
        g_last = gcs[i][0:1] if revs[i] else gcs[i][CHUNK - 1:CHUNK]
        pre = jnp.concatenate([uws[i], qs[i] * egcs[i], aqs[i][CHUNK:] * decays[i],
                               ks[i] * jnp.exp(g_last - gcs[i])], axis=1)
        out.append((pre, jnp.broadcast_to(jnp.exp(g_last), (8, MIX_W))))
    return out


def _gdn_step(items, maskbd):
    w = MIX_W
    n = range(len(items))
    pres, egls, states = zip(*items)
    wqs = [_mm(jnp.concatenate([pres[i][:, w:2 * w], pres[i][:, 2 * w:3 * w]], axis=0), states[i]) for i in n]
    v_news = [pres[i][:, :w] - wqs[i][:CHUNK] for i in n]
    outs = [wqs[i][CHUNK:] + _mm(pres[i][:, 3 * w:4 * w], _bd(v_news[i], maskbd)) for i in n]
    upds = [_mm_tn(pres[i][:, 4 * w:], v_news[i]) for i in n]
    mask_f = maskbd.astype(F32)
    return [(outs[i], states[i] * egls[i] + upds[i] * mask_f) for i in n]


def _gdn_kernel(*refs, seq_len, n_seq, cached):
    if cached:
        (zg_ref, zab_ref, s0_ref, conv_ref, alog_ref, dtb_ref, gn_ref, eexp_ref, ones_ref, maskbd_ref,
         o_ref, sout_ref, q_s, k_s, v_s, ge_s, pre_s, gl_s, oacc_s, st_s) = refs
    else:
        zg_ref, zab_ref, conv_ref, alog_ref, dtb_ref, gn_ref, eexp_ref, ones_ref, maskbd_ref = refs[:9]
        o_ref, sout_ref, q_s, k_s, v_s, ge_s, pre_s, gl_s, oacc_s, st_s = refs[-10:]
    t = seq_len * n_seq
    z = zg_ref[:, :GDN_CONV_CH]
    rowi = jnp.bitwise_and(_iota((t, 1), 0), seq_len - 1)
    zp = jnp.where(rowi == 0, 0.0, pltpu.roll(z, 1, 0))
    zn = jnp.where(rowi == seq_len - 1, 0.0, pltpu.roll(z, t - 1, 0))
    cw = conv_ref[...]
    qkv = _silu(zp * cw[0:1] + z * cw[1:2] + zn * cw[2:3])
    ones_bd = ones_ref[...]
    q = qkv[:, :MIX_W]
    k = qkv[:, MIX_W:2 * MIX_W]
    q_s[...] = q * lax.rsqrt(_mm_sel_r(q * q, ones_bd) + 1e-6) * (HEAD_DIM ** -0.5)
    k_s[...] = k * lax.rsqrt(_mm_sel_r(k * k, ones_bd) + 1e-6)
    v_s[...] = qkv[:, 2 * MIX_W:]
    ab = zab_ref[...]
    lane = _iota((t, LANES), 1)
    gb = jnp.where(lane < 2 * HEADS, -jnp.exp(alog_ref[...]) * _softplus(ab + dtb_ref[...]), _sigmoid(ab))
    ge_s[...] = _mm_sel_r(gb, eexp_ref[...])
    oacc_s[...] = jnp.zeros((t, MIX_W), F32)
    if cached:
        st_s[...] = s0_ref[...]
    else:
        st_s[...] = jnp.zeros((n_seq, 2, MIX_W, MIX_W), F32)
    maskbd = maskbd_ref[...]
    masks = (_chunk_masks(False), _chunk_masks(True))
    n_chunks = seq_len // CHUNK

    def prepare_group(c0):
        where, items = [], []
        for j in range(CHUNK_GROUP):
            c = c0 + j
            rows = pl.ds(pl.multiple_of(c * CHUNK, CHUNK), CHUNK)
            for d in range(2):
                where.append((d, c, rows))
                items.append((q_s[rows, :], k_s[rows, :], v_s[rows, :], ge_s[rows, d * MIX_W:(d + 1) * MIX_W],
                              ge_s[rows, (2 + d) * MIX_W:(3 + d) * MIX_W], masks[d], d == 1))
        for (d, c, rows), (pre, egl) in zip(where, _gdn_prepare(items, maskbd)):
            pre_s[d, rows, :] = pre
            gl_s[d, pl.ds(pl.multiple_of(c * 8, 8), 8), :] = egl

    _for_chunk_groups(n_chunks * n_seq, prepare_group)

    def body(i, carry):
        where, items = [], []
        for q in range(n_seq):
            for d in range(2):
                c = q * n_chunks + (i if d == 0 else n_chunks - 1 - i)
                rows = pl.ds(pl.multiple_of(c * CHUNK, CHUNK), CHUNK)
                where.append((q, d, rows))
                items.append((pre_s[d, rows, :], gl_s[d, pl.ds(pl.multiple_of(c * 8, 8), 8), :][0:1], st_s[q, d]))
        for (q, d, rows), (o, s_new) in zip(where, _gdn_step(items, maskbd)):
            oacc_s[rows, :] = oacc_s[rows, :] + o
            st_s[q, d] = s_new
        return carry

    lax.fori_loop(0, n_chunks, body, 0)
    o = oacc_s[...]
    ms = _mm_sel_r(o * o, ones_bd) * (1.0 / HEAD_DIM)
    gate = zg_ref[:, GDN_CONV_CH:]
    o_ref[...] = o * lax.rsqrt(ms + NORM_EPS) * gn_ref[...] * _silu(gate)
    for q in range(n_seq):
        for d in range(2):
            for h in range(HEADS):
                sl = slice(h * HEAD_DIM, (h + 1) * HEAD_DIM)
                sout_ref[q, d, h] = st_s[q, d, sl, sl]


def _state_out(nb, n_seq, layer, cached, prev, args, in_specs):
    if cached:
        return (pl.BlockSpec((n_seq, 2, HEADS, HEAD_DIM, HEAD_DIM), lambda b: (b, 0, 0, 0, 0)),
                jax.ShapeDtypeStruct((nb, 2, HEADS, HEAD_DIM, HEAD_DIM), F32), {})
    aliases = {len(args): 1}
    args.append(prev)
    in_specs.append(pl.BlockSpec(memory_space=pl.ANY))
    return (pl.BlockSpec((n_seq, None, 2, HEADS, HEAD_DIM, HEAD_DIM), lambda b: (b, layer, 0, 0, 0, 0)),
            jax.ShapeDtypeStruct((nb, DEPTH, 2, HEADS, HEAD_DIM, HEAD_DIM), F32), aliases)


def _gdn(zg, zab, lw, layer, consts, seq_len, s0_bd=None, prev=None):
    n = zg.shape[0]
    nb = n // seq_len
    cached = s0_bd is not None
    n_seq = 1 if cached else SEQ_PER_STEP
    assert seq_len & (seq_len - 1) == 0 and seq_len % (CHUNK * CHUNK_GROUP) == 0 and nb % n_seq == 0
    rows = seq_len * n_seq
    args = [zg, zab]
    in_specs = [pl.BlockSpec((rows, ZG_W), lambda b: (b, 0)),
                pl.BlockSpec((rows, LANES), lambda b: (b, 0))]
    if cached:
        args.append(s0_bd)
        in_specs.append(pl.BlockSpec((1, None, 2, MIX_W, MIX_W), lambda b: (b, layer, 0, 0, 0)))
    layered = [lw["gdn_conv"], lw["gdn_alog"], lw["gdn_dtb"], lw["gdn_norm"]]
    const = [consts["eexp"], consts["ones_bd"], consts["maskbd"]]
    args += layered + const
    in_specs += [_layer_spec(a, layer) for a in layered] + [_const_spec(a) for a in const]
    s_spec, s_shape, aliases = _state_out(nb, n_seq, layer, cached, prev, args, in_specs)
    return pl.pallas_call(
        functools.partial(_gdn_kernel, seq_len=seq_len, n_seq=n_seq, cached=cached),
        grid=(nb // n_seq,),
        in_specs=in_specs,
        out_specs=[
            pl.BlockSpec((rows, MIX_W), lambda b: (b, 0)),
            s_spec,
        ],
        out_shape=[jax.ShapeDtypeStruct((n, MIX_W), F32), s_shape],
        input_output_aliases=aliases,
        scratch_shapes=[
            pltpu.VMEM((rows, MIX_W), F32),
            pltpu.VMEM((rows, MIX_W), F32),
            pltpu.VMEM((rows, MIX_W), F32),
            pltpu.VMEM((rows, 4 * MIX_W), F32),
            pltpu.VMEM((2, rows, GDN_PRE_W), F32),
            pltpu.VMEM((2, rows // CHUNK * 8, MIX_W), F32),
            pltpu.VMEM((rows, MIX_W), F32),
            pltpu.VMEM((n_seq, 2, MIX_W, MIX_W), F32),
        ],
        compiler_params=pltpu.CompilerParams(dimension_semantics=("arbitrary",),
                                             vmem_limit_bytes=VMEM_LIMIT),
        name="gdn_lat" if cached else "gdn_ctx",
    )(*args)


RWKV_PRE_W = 7 * MIX_W


def _rwkv_prepare(items, maskbd, eye_full):
    n = range(len(items))
    rs, kds, vs, kks, bs, lws, masks, revs = zip(*items)
    cums = [_mm_sel_l(masks[i][2], lws[i]) for i in n]
    einvs = [jnp.exp(-c) for c in cums]
    kts = [kks[i] * jnp.exp(cums[i] - lws[i]) for i in n]
    rts = [rs[i] * jnp.exp(cums[i]) for i in n]
    krs = [jnp.concatenate([kts[i], rts[i]], axis=0) for i in n]
    lb_alls = [_mm_nt(krs[i], _bd(bs[i] * einvs[i], maskbd)) for i in n]
    lk_alls = [_mm_nt(krs[i], _bd(kds[i] * einvs[i], maskbd)) for i in n]
    lbs = [jnp.where(masks[i][1], lb_alls[i][:CHUNK], 0.0) for i in n]
    t_invs = _neumann_inverse(lbs, [m[3] for m in masks], maskbd)
    lvs = [_mm(jnp.concatenate([jnp.where(masks[i][1], lk_alls[i][:CHUNK], 0.0),
                                jnp.where(masks[i][0], lk_alls[i][CHUNK:], 0.0)], axis=0), _bd(vs[i], maskbd))
           for i in n]
    tkps = [_mm(t_invs[i], jnp.concatenate([_bd(kts[i], maskbd), _bd(lvs[i][:CHUNK], maskbd)], axis=1)) for i in n]
    out = []
    for i in n:
        c_last = cums[i][0:1] if revs[i] else cums[i][CHUNK - 1:CHUNK]
        tail = jnp.exp(c_last - cums[i])
        rb = jnp.where(masks[i][0], lb_alls[i][CHUNK:], 0.0)
        pre = jnp.concatenate([tkps[i][:, :MIX_W], rts[i], tkps[i][:, MIX_W:], lvs[i][CHUNK:], rb,
                               kds[i] * tail, bs[i] * tail], axis=1)
        gcol = jnp.sum(eye_full * jnp.exp(c_last), axis=1, keepdims=True)
        out.append((pre, jnp.broadcast_to(gcol, (MIX_W, LANES))))
    return out


def _rwkv_step(items, maskbd):
    w = MIX_W
    n = range(len(items))
    pres, vs, gcols, states = zip(*items)
    prs = [_mm(jnp.concatenate([pres[i][:, :w], pres[i][:, w:2 * w]], axis=0), states[i]) for i in n]
    ps = [prs[i][:CHUNK] + pres[i][:, 2 * w:3 * w] for i in n]
    outs = [prs[i][CHUNK:] + pres[i][:, 3 * w:4 * w] - _mm(pres[i][:, 4 * w:5 * w], _bd(ps[i], maskbd)) for i in n]
    upds = [_mm_tn(jnp.concatenate([pres[i][:, 5 * w:6 * w], pres[i][:, 6 * w:]], axis=0),
                   jnp.concatenate([vs[i], -ps[i]], axis=0)) for i in n]
    mask_f = maskbd.astype(F32)
    return [(outs[i], states[i] * jnp.concatenate([gcols[i], gcols[i]], axis=1) + upds[i] * mask_f) for i in n]


def _rwkv_kernel(*refs, seq_len, n_seq, cached):
    if cached:
        (zr_ref, s0_ref, mup_ref, mun_ref, w0_ref, w2_ref, a0_ref, a2_ref, g2_ref, kk_ref, ka_ref, rk_ref,
         gnw_ref, gnb_ref, ones_ref, maskbd_ref, o_ref, sout_ref,
         r_s, v_s, kk_s, dir_s, bg_s, pre_s, gcol_s, oacc_s, st_s) = refs
    else:
        (zr_ref, mup_ref, mun_ref, w0_ref, w2_ref, a0_ref, a2_ref, g2_ref, kk_ref, ka_ref, rk_ref,
         gnw_ref, gnb_ref, ones_ref, maskbd_ref) = refs[:15]
        o_ref, sout_ref, r_s, v_s, kk_s, dir_s, bg_s, pre_s, gcol_s, oacc_s, st_s = refs[-11:]
    t = seq_len * n_seq
    z = zr_ref[...]
    rowi = jnp.bitwise_and(_iota((t, 1), 0), seq_len - 1)
    zp = jnp.where(rowi == 0, 0.0, pltpu.roll(z, 1, 0))
    zn = jnp.where(rowi == seq_len - 1, 0.0, pltpu.roll(z, t - 1, 0))
    z = z + mup_ref[...] * (zp - z) + mun_ref[...] * (zn - z)
    w = MIX_W
    r = z[:, :w]
    k = z[:, w:2 * w]
    v = z[:, 2 * w:3 * w]
    wd = jnp.tanh(z[:, 3 * w:3 * w + LANES])
    ad = z[:, 3 * w + LANES:3 * w + 2 * LANES]
    gd = _sigmoid(z[:, 3 * w + 2 * LANES:])
    ones_bd = ones_ref[...]
    kk = k * kk_ref[...]
    kk = kk * lax.rsqrt(_mm_sel_r(kk * kk, ones_bd) + 1e-6)
    r_s[...] = r
    v_s[...] = v
    kk_s[...] = kk
    bonus = jnp.zeros((t, w), F32)
    for d in range(2):
        w_log = -_softplus(-(w0_ref[d:d + 1] + _mm(wd, w2_ref[d]))) - 0.5
        a = _sigmoid(a0_ref[d:d + 1] + _mm(ad, a2_ref[d]))
        kd = k * (1.0 + (a - 1.0) * ka_ref[...])
        dir_s[:, (3 * d) * w:(3 * d + 1) * w] = -jnp.exp(w_log)
        dir_s[:, (3 * d + 1) * w:(3 * d + 2) * w] = kd
        dir_s[:, (3 * d + 2) * w:(3 * d + 3) * w] = kk * a
        bonus = bonus + _mm_sel_r(r * kd * rk_ref[...], ones_bd) * v
    bg_s[:, :w] = bonus
    bg_s[:, w:] = _mm(gd, g2_ref[...])
    oacc_s[...] = jnp.zeros((t, w), F32)
    if cached:
        st_s[...] = s0_ref[...]
    else:
        st_s[...] = jnp.zeros((n_seq, 2, w, w), F32)
    maskbd = maskbd_ref[...]
    masks = (_chunk_masks(False), _chunk_masks(True))
    eye_full = jnp.where(_iota((w, w), 0) == _iota((w, w), 1), 1.0, 0.0).astype(F32)
    n_chunks = seq_len // CHUNK

    def prepare_group(c0):
        where, items = [], []
        for j in range(CHUNK_GROUP):
            c = c0 + j
            rows = pl.ds(pl.multiple_of(c * CHUNK, CHUNK), CHUNK)
            for d in range(2):
                where.append((d, c, rows))
                items.append((r_s[rows, :], dir_s[rows, (3 * d + 1) * w:(3 * d + 2) * w], v_s[rows, :], kk_s[rows, :],
                              dir_s[rows, (3 * d + 2) * w:(3 * d + 3) * w], dir_s[rows, (3 * d) * w:(3 * d + 1) * w],
                              masks[d], d == 1))
        for (d, c, rows), (pre, gcol) in zip(where, _rwkv_prepare(items, maskbd, eye_full)):
            pre_s[d, rows, :] = pre
            gcol_s[d, pl.ds(pl.multiple_of(c * w, w), w), :] = gcol

    _for_chunk_groups(n_chunks * n_seq, prepare_group)

    def body(i, carry):
        where, items = [], []
        for q in range(n_seq):
            for d in range(2):
                c = q * n_chunks + (i if d == 0 else n_chunks - 1 - i)
                rows = pl.ds(pl.multiple_of(c * CHUNK, CHUNK), CHUNK)
                where.append((q, d, rows))
                items.append((pre_s[d, rows, :], v_s[rows, :], gcol_s[d, pl.ds(pl.multiple_of(c * w, w), w), :],
                              st_s[q, d]))
        for (q, d, rows), (o, z_new) in zip(where, _rwkv_step(items, maskbd)):
            oacc_s[rows, :] = oacc_s[rows, :] + o
            st_s[q, d] = z_new
        return carry

    lax.fori_loop(0, n_chunks, body, 0)
    o = oacc_s[...]
    inv_n = 1.0 / HEAD_DIM
    mu = _mm_sel_r(o, ones_bd) * inv_n
    oc = o - mu
    var = _mm_sel_r(oc * oc, ones_bd) * inv_n
    y = oc * lax.rsqrt(var + RWKV_GN_EPS) * gnw_ref[...] + gnb_ref[...]
    o_ref[...] = (y + bg_s[:, :w]) * bg_s[:, w:]
    for q in range(n_seq):
        for d in range(2):
            state = st_s[q, d] if cached else st_s[q, d].T
            for h in range(HEADS):
                sl = slice(h * HEAD_DIM, (h + 1) * HEAD_DIM)
                sout_ref[q, d, h] = state[sl, sl]


def _rwkv(zr, lw, layer, consts, seq_len, s0_bd=None, prev=None):
    n = zr.shape[0]
    nb = n // seq_len
    cached = s0_bd is not None
    n_seq = 1 if cached else SEQ_PER_STEP
    assert seq_len & (seq_len - 1) == 0 and seq_len % (CHUNK * CHUNK_GROUP) == 0 and nb % n_seq == 0
    rows = seq_len * n_seq
    args = [zr]
    in_specs = [pl.BlockSpec((rows, ZR_W), lambda b: (b, 0))]
    if cached:
        args.append(s0_bd)
        in_specs.append(pl.BlockSpec((1, None, 2, MIX_W, MIX_W), lambda b: (b, layer, 0, 0, 0)))
    layered = [lw["rwkv_mu_prev"], lw["rwkv_mu_next"], lw["rwkv_w0"], lw["rwkv_w2"], lw["rwkv_a0"], lw["rwkv_a2"],
               lw["rwkv_g2"], lw["rwkv_k_k"], lw["rwkv_k_a"], lw["rwkv_r_k"], lw["rwkv_gn_w"], lw["rwkv_gn_b"]]
    const = [consts["ones_bd"], consts["maskbd"]]
    args += layered + const
    in_specs += [_layer_spec(a, layer) for a in layered] + [_const_spec(a) for a in const]
    s_spec, s_shape, aliases = _state_out(nb, n_seq, layer, cached, prev, args, in_specs)
    return pl.pallas_call(
        functools.partial(_rwkv_kernel, seq_len=seq_len, n_seq=n_seq, cached=cached),
        grid=(nb // n_seq,),
        in_specs=in_specs,
        out_specs=[
            pl.BlockSpec((rows, MIX_W), lambda b: (b, 0)),
            s_spec,
        ],
        out_shape=[jax.ShapeDtypeStruct((n, MIX_W), F32), s_shape],
        input_output_aliases=aliases,
        scratch_shapes=[
            pltpu.VMEM((rows, MIX_W), F32),
            pltpu.VMEM((rows, MIX_W), F32),
            pltpu.VMEM((rows, MIX_W), F32),
            pltpu.VMEM((rows, 6 * MIX_W), F32),
            pltpu.VMEM((rows, 2 * MIX_W), F32),
            pltpu.VMEM((2, rows, RWKV_PRE_W), F32),
            pltpu.VMEM((2, rows // CHUNK * MIX_W, LANES), F32),
            pltpu.VMEM((rows, MIX_W), F32),
            pltpu.VMEM((n_seq, 2, MIX_W, MIX_W), F32),
        ],
        compiler_params=pltpu.CompilerParams(dimension_semantics=("arbitrary",),
                                             vmem_limit_bytes=VMEM_LIMIT),
        name="rwkv_lat" if cached else "rwkv_ctx",
    )(*args)


def _route(logits_t, bias):
    tm = logits_t.shape[1]
    neg = -jnp.inf
    sc = _sigmoid(logits_t)
    sc3 = sc.reshape(N_GROUPS, GROUP_SIZE, tm)
    sel = (sc + bias).reshape(N_GROUPS, GROUP_SIZE, tm)
    si = _iota(sel.shape, 1).astype(F32)
    m1 = jnp.max(sel, axis=1, keepdims=True)
    f1 = jnp.min(jnp.where(sel == m1, si, float(GROUP_SIZE)), axis=1, keepdims=True)
    m2 = jnp.max(jnp.where(si == f1, neg, sel), axis=1, keepdims=True)
    grp = m1 + m2
    gi = _iota(grp.shape, 0).astype(F32)
    gsel = jnp.zeros(grp.shape, F32)
    for _ in range(TOPK_GROUPS):
        mx = jnp.max(grp, axis=0, keepdims=True)
        fi = jnp.min(jnp.where(grp == mx, gi, float(N_GROUPS)), axis=0, keepdims=True)
        hit = gi == fi
        gsel = jnp.where(hit, 1.0, gsel)
        grp = jnp.where(hit, neg, grp)
    cur = jnp.where(gsel > 0.0, sel, neg)
    ei = (_iota(cur.shape, 0) * GROUP_SIZE + _iota(cur.shape, 1)).astype(F32)
    chosen = jnp.zeros(cur.shape, F32)
    ids, wts = [], []
    for _ in range(TOP_K):
        mx = jnp.max(jnp.max(cur, axis=0, keepdims=True), axis=1, keepdims=True)
        fi = jnp.min(jnp.min(jnp.where(cur == mx, ei, float(N_EXPERTS)), axis=0, keepdims=True),
                     axis=1, keepdims=True)
        hit = ei == fi
        chosen = jnp.where(hit, 1.0, chosen)
        cur = jnp.where(hit, neg, cur)
        ids.append(fi.reshape(1, tm))
        wts.append(jnp.sum(jnp.sum(jnp.where(hit, sc3, 0.0), axis=0, keepdims=True), axis=1, keepdims=True)
                   .reshape(1, tm))
    w = jnp.concatenate(wts, axis=0)
    w = w / jnp.sum(w, axis=0, keepdims=True) * ROUTE_SCALE
    return chosen.reshape(N_EXPERTS, tm), jnp.concatenate(ids, axis=0), w


def _pack_halves(x):
    half = x.shape[1] // 2
    bits = lax.bitcast_convert_type(x.astype(BF16).astype(F32), jnp.int32)
    lo = lax.shift_right_logical(bits[:, :half], jnp.int32(16))
    return jnp.bitwise_or(lo, jnp.bitwise_and(bits[:, half:], jnp.int32(-65536)))


def _unpack_halves(word):
    lo = lax.bitcast_convert_type(lax.shift_left(word, jnp.int32(16)), F32)
    hi = lax.bitcast_convert_type(jnp.bitwise_and(word, jnp.int32(-65536)), F32)
    return lo, hi


def _post_kernel(x_ref, om_ref, og_ref, or_ref, mod_ref, wo_ref, n2_ref, rt_ref, rb_ref, sgu_ref, sdn_ref,
                 tri_ref, cin_ref, x1_ref, h2_ref, eid_ref, rank_ref, ew_ref, cnt_ref, carry_s):
    @pl.when(pl.program_id(0) == 0)
    def _():
        carry_s[...] = cin_ref[...]

    m = mod_ref[0]
    g1 = m[:, 2 * D_MODEL:3 * D_MODEL]
    sh2 = m[:, 3 * D_MODEL:4 * D_MODEL]
    sc2 = m[:, 4 * D_MODEL:5 * D_MODEL]
    w_mla = MLA_HEADS * MLA_V
    mix = (_mm(om_ref[...], wo_ref[0:w_mla, :]) + _mm(og_ref[...], wo_ref[w_mla:w_mla + MIX_W, :])
           + _mm(or_ref[...], wo_ref[w_mla + MIX_W:, :]))
    x1 = x_ref[...] + g1 * mix
    h2 = _rms(x1, n2_ref[...]) * (1.0 + sc2) + sh2
    h2_ref[...] = _pack_halves(h2)
    g2 = m[:, 5 * D_MODEL:]
    x1_ref[...] = x1 + g2 * _mm(_swiglu_act(_mm(h2, sgu_ref[...])), sdn_ref[...])
    r_hi, r_lo = _split2(rt_ref[...])
    h_hi, h_lo = _split2(h2)
    logits_t = _mm_nt(r_hi, h_hi) + _mm_nt(r_hi, h_lo) + _mm_nt(r_lo, h_hi)
    chosen, ids, w = _route(logits_t, rb_ref[...])
    tm = chosen.shape[1]
    rank_et = (carry_s[:, 0:1] + _mm(chosen, tri_ref[...])).reshape(N_GROUPS, GROUP_SIZE, tm)
    ei = (_iota(rank_et.shape, 0) * GROUP_SIZE + _iota(rank_et.shape, 1)).astype(F32)
    ranks = []
    for k in range(TOP_K):
        pick = jnp.where(ei == ids[k:k + 1].reshape(1, 1, tm), rank_et, 0.0)
        ranks.append(jnp.sum(jnp.sum(pick, axis=0, keepdims=True), axis=1, keepdims=True).reshape(1, tm))
    eid_ref[...] = ids.astype(jnp.int32)
    rank_ref[...] = jnp.concatenate(ranks, axis=0).astype(jnp.int32)
    ew_ref[...] = jnp.concatenate([w, jnp.zeros((LANES - TOP_K, tm), F32)], axis=0).T
    total = carry_s[...] + jnp.sum(chosen, axis=1, keepdims=True)
    carry_s[...] = total
    cnt_ref[...] = total


def _post(x2d, om, og, orw, mods, lw, layer, consts, tm, seq_len, mod_base, counts_in):
    n = x2d.shape[0]
    tiles_per_seq = seq_len // tm if mod_base else 1

    def mod_idx(i):
        return (layer, mod_base + i // tiles_per_seq if mod_base else 0, 0, 0)

    row = lambda w: pl.BlockSpec((tm, w), lambda i: (i, 0))
    col = lambda h: pl.BlockSpec((h, tm), lambda i: (0, i))
    full = lambda a: _layer_spec(a, layer)
    tail = [lw["w_out"], lw["norm2"], lw["router_t"], lw["router_b"], lw["shared_w_gu"], lw["shared_w_down"]]
    tri = consts["tri_tokens"]
    return pl.pallas_call(
        _post_kernel,
        grid=(n // tm,),
        in_specs=[row(D_MODEL), row(MLA_HEADS * MLA_V), row(MIX_W), row(MIX_W),
                  pl.BlockSpec((None, 1, 1, 6 * D_MODEL), mod_idx)] + [full(a) for a in tail]
        + [_const_spec(tri), _const_spec(counts_in)],
        out_specs=[row(D_MODEL), row(D_MODEL // 2), col(TOP_K), col(TOP_K), row(LANES),
                   pl.BlockSpec((N_EXPERTS, LANES), lambda i: (0, 0))],
        out_shape=[
            jax.ShapeDtypeStruct((n, D_MODEL), F32),
            jax.ShapeDtypeStruct((n, D_MODEL // 2), jnp.int32),
            jax.ShapeDtypeStruct((TOP_K, n), jnp.int32),
            jax.ShapeDtypeStruct((TOP_K, n), jnp.int32),
            jax.ShapeDtypeStruct((n, LANES), F32),
            jax.ShapeDtypeStruct((N_EXPERTS, LANES), F32),
        ],
        scratch_shapes=[pltpu.VMEM((N_EXPERTS, LANES), F32)],
        compiler_params=pltpu.CompilerParams(dimension_semantics=("arbitrary",),
                                             vmem_limit_bytes=VMEM_LIMIT),
        name="post",
    )(x2d, om, og, orw, mods, *tail, tri, counts_in)


MOE_ROWS = 512
SC_ROWS = 128
SC_GATHER_ROWS = 64
SC_SUBCORES = 32


def _swiglu_act(gu):
    return _silu(gu[:, :D_EXPERT]) * gu[:, D_EXPERT:]


def _dispatch_plan(eid, rank, counts, n, rows):
    n_blocks = n * TOP_K // rows + N_EXPERTS
    cnt = counts[:, 0].astype(jnp.int32)
    blocks = (cnt + rows - 1) // rows
    block_end = jnp.cumsum(blocks)
    offset = (block_end - blocks) * rows
    experts = jnp.arange(N_EXPERTS, dtype=jnp.int32)
    dest = jnp.sum(jnp.where(eid[..., None] == experts, offset, 0), axis=-1) + rank
    block_ids = jnp.arange(n_blocks, dtype=jnp.int32)
    block_expert = jnp.minimum(jnp.sum((block_end[None, :] <= block_ids[:, None]).astype(jnp.int32), axis=1),
                               N_EXPERTS - 1)
    n_used = block_end[-1:]
    prev_expert = jnp.concatenate([jnp.full((1,), -1, jnp.int32), block_expert[:-1]])
    first = ((block_ids < n_used[0]) & (block_expert != prev_expert)).astype(jnp.int32)
    slot = jnp.bitwise_and(jnp.cumsum(first) - 1, 1)
    owner_or_none = jnp.where(blocks > 0, experts, N_EXPERTS)
    next_owner = jnp.concatenate([lax.cummin(owner_or_none[::-1])[::-1][1:], jnp.full((1,), N_EXPERTS, jnp.int32)])
    nxt = jnp.sum(jnp.where(block_expert[:, None] == experts, next_owner, 0), axis=1)
    schedule = (block_expert, first, slot.astype(jnp.int32), nxt.astype(jnp.int32))
    return dest, schedule, n_used[0].astype(jnp.int32), n_blocks


def _sc_mesh():
    return plsc.VectorSubcoreMesh(core_axis_name="core", subcore_axis_name="subcore")


def _sc_dispatch(groups, dest, n_rows):
    w = groups[0].shape[1]
    dtype = groups[0].dtype

    @functools.partial(pl.kernel, out_type=jax.ShapeDtypeStruct((n_rows, w), dtype), mesh=_sc_mesh(),
                       scratch_types=[pltpu.VMEM((SC_ROWS, w), dtype), pltpu.VMEM((TOP_K, SC_ROWS), jnp.int32)])
    def kern(*refs):
        x_refs, d_hbm, o_hbm, xv, dv = refs[:len(groups)], *refs[len(groups):]
        sid = lax.axis_index("core") * (SC_SUBCORES // 2) + lax.axis_index("subcore")
        start = 0
        for x_hbm, x in zip(x_refs, groups):
            def chunk(c, x_hbm=x_hbm, start=start):
                r0 = pl.multiple_of(c * SC_ROWS, SC_ROWS)
                pltpu.sync_copy(x_hbm.at[pl.ds(r0, SC_ROWS)], xv)
                pltpu.sync_copy(d_hbm.at[:, pl.ds(start + r0, SC_ROWS)], dv)
                for k in range(TOP_K):
                    pltpu.sync_copy(xv, o_hbm.at[dv.at[k]])

            pl.loop(sid, x.shape[0] // SC_ROWS, step=SC_SUBCORES)(chunk)
            start += x.shape[0]

    return kern(*groups, dest)


def _sc_gather(y, idx):
    w = y.shape[1]
    n_chunks = idx.shape[0]
    per = n_chunks // SC_SUBCORES
    assert per * SC_SUBCORES == n_chunks

    @functools.partial(pl.kernel, out_type=jax.ShapeDtypeStruct((n_chunks * SC_GATHER_ROWS, w), y.dtype),
                       mesh=_sc_mesh(),
                       scratch_types=[pltpu.VMEM((2, SC_GATHER_ROWS, w), y.dtype),
                                      pltpu.VMEM((2, 1, SC_GATHER_ROWS), jnp.int32),
                                      pltpu.SemaphoreType.DMA((2,)), pltpu.SemaphoreType.DMA((2,))])
    def kern(y_hbm, i_hbm, o_hbm, ov, iv, gather_sem, store_sem):
        base = (lax.axis_index("core") * (SC_SUBCORES // 2) + lax.axis_index("subcore")) * per

        def gather(j, slot):
            pltpu.sync_copy(i_hbm.at[pl.ds(base + j, 1)], iv.at[slot])
            return pltpu.async_copy(y_hbm.at[iv.at[slot, 0]], ov.at[slot], gather_sem.at[slot])

        def store(j, slot):
            r0 = pl.multiple_of((base + j) * SC_GATHER_ROWS, SC_GATHER_ROWS)
            return pltpu.async_copy(ov.at[slot], o_hbm.at[pl.ds(r0, SC_GATHER_ROWS)], store_sem.at[slot])

        gather(0, 0).wait()

        @pl.loop(0, per)
        def _(j):
            slot = lax.rem(j, 2)
            storing = store(j, slot)

            @pl.when(j + 1 < per)
            def _():
                gather(j + 1, 1 - slot).wait()

            storing.wait()

    return kern(y, idx)


def _moe_rows_kernel(be_ref, first_ref, slot_ref, nxt_ref, x_ref, wgu_hbm, wdn_hbm, y_ref,
                     wgu_f, wdn_f, wgu_b, wdn_b, sem, *, layer):
    b = pl.program_id(0)

    def weight_copies(expert, slot):
        return (pltpu.make_async_copy(wgu_hbm.at[layer, expert], wgu_f.at[slot], sem.at[slot, 0]),
                pltpu.make_async_copy(wdn_hbm.at[layer, expert], wdn_f.at[slot], sem.at[slot, 1]))

    @pl.when(b == 0)
    def _():
        for copy in weight_copies(be_ref[0], 0):
            copy.start()

    @pl.when(first_ref[b] == 1)
    def _():
        slot = slot_ref[b]
        for copy in weight_copies(be_ref[b], slot):
            copy.wait()

        @pl.when(nxt_ref[b] < N_EXPERTS)
        def _():
            for copy in weight_copies(nxt_ref[b], 1 - slot):
                copy.start()

        wgu_b[...] = wgu_f[slot].astype(BF16)
        wdn_b[...] = wdn_f[slot].astype(BF16)

    half = D_MODEL // 2
    lo, hi = _unpack_halves(x_ref[...])
    gu = _mm(lo, wgu_b[0:half, :]) + _mm(hi, wgu_b[half:, :])
    y_ref[...] = _pack_halves(_mm(_swiglu_act(gu), wdn_b[...]))


def _moe_rows(xs, schedule, n_used, lw, layer, rows):
    half = D_MODEL // 2
    row_spec = pl.BlockSpec((rows, half), lambda b, *_: (b, 0))
    return pl.pallas_call(
        functools.partial(_moe_rows_kernel, layer=layer),
        grid_spec=pltpu.PrefetchScalarGridSpec(
            num_scalar_prefetch=4,
            grid=(n_used,),
            in_specs=[row_spec, pl.BlockSpec(memory_space=pl.ANY), pl.BlockSpec(memory_space=pl.ANY)],
            out_specs=row_spec,
            scratch_shapes=[
                pltpu.VMEM((2, D_MODEL, 2 * D_EXPERT), F32),
                pltpu.VMEM((2, D_EXPERT, D_MODEL), F32),
                pltpu.VMEM((D_MODEL, 2 * D_EXPERT), BF16),
                pltpu.VMEM((D_EXPERT, D_MODEL), BF16),
                pltpu.SemaphoreType.DMA((2, 2)),
            ],
        ),
        out_shape=jax.ShapeDtypeStruct(xs.shape, jnp.int32),
        compiler_params=pltpu.CompilerParams(dimension_semantics=("arbitrary",),
                                             vmem_limit_bytes=VMEM_LIMIT),
        name="moe_rows",
    )(*schedule, xs, lw["moe_w_gu"], lw["moe_w_down"])


def _moe_combine_kernel(yg_ref, ew_ref, x1_ref, mod_ref, nf_ref, o_ref, *, final):
    ew = ew_ref[...]
    acc_lo = acc_hi = None
    for k in range(TOP_K):
        lo, hi = _unpack_halves(yg_ref[k])
        wk = ew[:, k:k + 1]
        acc_lo = wk * lo if acc_lo is None else acc_lo + wk * lo
        acc_hi = wk * hi if acc_hi is None else acc_hi + wk * hi
    g2 = mod_ref[0][:, 5 * D_MODEL:]
    x2 = x1_ref[...] + g2 * jnp.concatenate([acc_lo, acc_hi], axis=1)
    if final:
        x2 = _rms(x2, nf_ref[...])
    o_ref[...] = x2


def _moe_combine(yg, row0, ew, x1, mods, layer, norm_f, tm, seq_len, mod_base, final):
    n = x1.shape[0]
    half = D_MODEL // 2
    tiles_per_seq = seq_len // tm if mod_base else 1
    tile0 = row0 // tm

    def mod_idx(i):
        return (layer, mod_base + i // tiles_per_seq if mod_base else 0, 0, 0)

    row = lambda w: pl.BlockSpec((tm, w), lambda i: (i, 0))
    return pl.pallas_call(
        functools.partial(_moe_combine_kernel, final=final),
        grid=(n // tm,),
        in_specs=[pl.BlockSpec((TOP_K, tm, half), lambda i: (0, tile0 + i, 0)), row(LANES), row(D_MODEL),
                  pl.BlockSpec((None, 1, 1, 6 * D_MODEL), mod_idx), _const_spec(norm_f)],
        out_specs=row(D_MODEL),
        out_shape=jax.ShapeDtypeStruct((n, D_MODEL), F32),
        compiler_params=pltpu.CompilerParams(dimension_semantics=("arbitrary",),
                                             vmem_limit_bytes=VMEM_LIMIT),
        name="moe_combine_final" if final else "moe_combine",
    )(yg, ew, x1, mods, norm_f)


def _moe_experts(groups, eid, rank, counts, lw, layer):
    group_sizes = [g.shape[0] for g in groups]
    n = sum(group_sizes)
    dest, schedule, n_used, n_blocks = _dispatch_plan(eid, rank, counts, n, MOE_ROWS)
    xs = _sc_dispatch(groups, dest, n_blocks * MOE_ROWS)
    y = _moe_rows(xs, schedule, n_used, lw, layer, MOE_ROWS)
    outs, start = [], 0
    for size in group_sizes:
        idx = dest[:, start:start + size].reshape(size * TOP_K // SC_GATHER_ROWS, SC_GATHER_ROWS)
        outs.append(_sc_gather(y, idx).reshape(TOP_K, size, D_MODEL // 2))
        start += size
    return outs


def _constants():
    idx = np.arange(MIX_W)
    same_head = (idx[:, None] // HEAD_DIM) == (idx[None, :] // HEAD_DIM)
    maskbd = jnp.asarray(same_head, BF16)
    eexp = np.zeros((LANES, 4 * MIX_W), np.float32)
    for blk in range(4):
        kind, d = divmod(blk, 2)
        for h in range(HEADS):
            src = kind * 2 * HEADS + d * HEADS + h
            eexp[src, blk * MIX_W + h * HEAD_DIM: blk * MIX_W + (h + 1) * HEAD_DIM] = 1.0
    tri = np.triu(np.ones((POST_TM, POST_TM), np.float32), 1)
    return {"maskbd": maskbd, "ones_bd": maskbd, "eexp": jnp.asarray(eexp, BF16), "tri_tokens": jnp.asarray(tri, BF16)}


def _rope_tables(n):
    rows = n // GRID_W
    row = jnp.repeat(jnp.arange(rows, dtype=F32), GRID_W)
    col = jnp.tile(jnp.arange(GRID_W, dtype=F32), rows)
    axis_dim = MLA_ROPE // 2
    inv = jnp.power(ROPE_BASE, -jnp.arange(0, axis_dim, 2, dtype=F32) / axis_dim)
    ang_r = row[:, None] * inv
    ang_c = col[:, None] * inv
    cr, sr, cc, sc = jnp.cos(ang_r), jnp.sin(ang_r), jnp.cos(ang_c), jnp.sin(ang_c)
    zeros = jnp.zeros((n, LANES - MLA_ROPE), F32)
    cos_t = jnp.concatenate([cr, cr, cc, cc, zeros], axis=1)
    sin_t = jnp.concatenate([-sr, sr, -sc, sc, zeros], axis=1)
    return cos_t, sin_t


def _stacked_weights(p):
    def per_direction(w):
        half = jnp.zeros((DEPTH, 64, MIX_W), F32)
        return jnp.stack([jnp.concatenate([w[:, 0], half], axis=1),
                          jnp.concatenate([half, w[:, 1]], axis=1)], axis=1).astype(BF16)

    row = lambda v: v.reshape(DEPTH, 1, -1)
    pad_row = lambda v: jnp.pad(row(v), ((0, 0), (0, 0), (0, LANES - 2 * HEADS)))
    return {
        "norm1": row(p["norm1"]),
        "w_in": p["w_in"],
        "q_norm": row(p["mla_q_norm"]),
        "w_uq": p["mla_w_uq"],
        "kv_norm": row(p["mla_kv_norm"]),
        "w_ukv": p["mla_w_ukv"].astype(BF16),
        "gdn_conv": p["gdn_conv"],
        "gdn_alog": pad_row(p["gdn_a_log"]),
        "gdn_dtb": pad_row(p["gdn_dt_bias"]),
        "gdn_norm": jnp.tile(row(p["gdn_norm"]), (1, 1, HEADS)),
        "rwkv_mu_prev": row(p["rwkv_mu_prev"]),
        "rwkv_mu_next": row(p["rwkv_mu_next"]),
        "rwkv_w0": p["rwkv_w0"],
        "rwkv_w2": per_direction(p["rwkv_w2"]),
        "rwkv_a0": p["rwkv_a0"],
        "rwkv_a2": per_direction(p["rwkv_a2"]),
        "rwkv_g2": p["rwkv_g2"].astype(BF16),
        "rwkv_k_k": row(p["rwkv_k_k"]),
        "rwkv_k_a": row(p["rwkv_k_a"]),
        "rwkv_r_k": row(p["rwkv_r_k"]),
        "rwkv_gn_w": row(p["rwkv_gn_w"]),
        "rwkv_gn_b": row(p["rwkv_gn_b"]),
        "w_out": p["w_out"].astype(BF16),
        "norm2": row(p["norm2"]),
        "router_t": jnp.swapaxes(p["moe_router"], 1, 2),
        "router_b": p["moe_bias"].reshape(DEPTH, N_EXPERTS, 1),
        "moe_w_gu": p["moe_w_gu"],
        "moe_w_down": p["moe_w_down"],
        "shared_w_gu": p["shared_w_gu"].astype(BF16),
        "shared_w_down": p["shared_w_down"].astype(BF16),
    }


def _embed_block_diag(s):
    b = s.shape[0]
    eye = jnp.eye(HEADS, dtype=s.dtype)
    out = jnp.einsum("bdhkv,hg->bdhkgv", s, eye)
    return out.reshape(b, 2, MIX_W, MIX_W)


def _layer_front(x2d, mods, lw, l, consts, seq_len, mod_base, cache, tm, tq, counts_in, prev=None):
    zm, zg, zab, zr = _inproj(x2d, mods, lw, l, tm, seq_len, mod_base)
    if cache is None:
        o_mla, ckv, kpe = _mla_ctx(zm, lw, l, seq_len, prev[:2])
        o_gdn, s_gdn = _gdn(zg, zab, lw, l, consts, seq_len, prev=prev[2])
        o_rwkv, s_rwkv = _rwkv(zr, lw, l, consts, seq_len, prev=prev[3])
        new = (ckv, kpe, s_gdn, s_rwkv)
    else:
        cckv, ckpe, rc, rs, sg, sr = cache
        o_mla = _mla_lat(zm, lw, l, seq_len, tq, (cckv, ckpe, rc, rs))
        o_gdn, _ = _gdn(zg, zab, lw, l, consts, seq_len, sg)
        o_rwkv, _ = _rwkv(zr, lw, l, consts, seq_len, sr)
        new = None
    routed = _post(x2d, o_mla, o_gdn, o_rwkv, mods, lw, l, consts, tm, seq_len, mod_base, counts_in)
    return routed, new


def kernel(x_prompt, x_sample, cache_mla_ckv, cache_mla_kpe, state_gdn, state_rwkv, c, c_ctx, ada_w, ada_b, norm1, w_in, mla_q_norm, mla_w_uq, mla_kv_norm, mla_w_ukv, gdn_conv, gdn_a_log, gdn_dt_bias, gdn_norm, rwkv_mu_prev, rwkv_mu_next, rwkv_w0, rwkv_w2, rwkv_a0, rwkv_a2, rwkv_g2, rwkv_k_k, rwkv_k_a, rwkv_r_k, rwkv_gn_w, rwkv_gn_b, w_out, norm2, moe_router, moe_bias, moe_w_gu, moe_w_down, shared_w_gu, shared_w_down, norm_f):
    p = dict(norm1=norm1, w_in=w_in, mla_q_norm=mla_q_norm, mla_w_uq=mla_w_uq, mla_kv_norm=mla_kv_norm,
             mla_w_ukv=mla_w_ukv, gdn_conv=gdn_conv, gdn_a_log=gdn_a_log, gdn_dt_bias=gdn_dt_bias,
             gdn_norm=gdn_norm, rwkv_mu_prev=rwkv_mu_prev, rwkv_mu_next=rwkv_mu_next, rwkv_w0=rwkv_w0,
             rwkv_w2=rwkv_w2, rwkv_a0=rwkv_a0, rwkv_a2=rwkv_a2, rwkv_g2=rwkv_g2, rwkv_k_k=rwkv_k_k,
             rwkv_k_a=rwkv_k_a, rwkv_r_k=rwkv_r_k, rwkv_gn_w=rwkv_gn_w, rwkv_gn_b=rwkv_gn_b, w_out=w_out,
             norm2=norm2, moe_router=moe_router, moe_bias=moe_bias, moe_w_gu=moe_w_gu, moe_w_down=moe_w_down,
             shared_w_gu=shared_w_gu, shared_w_down=shared_w_down)
    weights = _stacked_weights(p)
    consts = _constants()
    nf = norm_f.reshape(1, D_MODEL)
    b_ctx, t_ctx, _ = x_prompt.shape
    b_lat, t_lat, _ = x_sample.shape

    cvec8 = jnp.concatenate([c_ctx[None, :], c, jnp.zeros((8 - 1 - b_lat, D_MODEL), F32)], axis=0)
    mods = _adaln(cvec8, ada_w, ada_b)
    mods = mods.reshape(DEPTH, 8, 1, 6 * D_MODEL)

    rc, rs = _rope_tables(t_lat)
    ckpe = jnp.pad(cache_mla_kpe, ((0, 0), (0, 0), (0, 0), (0, LANES - MLA_ROPE)))
    cache = (cache_mla_ckv, ckpe, rc, rs, _embed_block_diag_layers(state_gdn),
             _embed_block_diag_layers(jnp.swapaxes(state_rwkv, -1, -2)))
    xp = x_prompt.reshape(b_ctx * t_ctx, D_MODEL)
    xs = x_sample.reshape(b_lat * t_lat, D_MODEL)
    n_ctx = xp.shape[0]
    tm = POST_TM
    state_shape = (b_ctx, DEPTH, 2, HEADS, HEAD_DIM, HEAD_DIM)
    ctx_outs = (jnp.zeros((b_ctx, DEPTH, t_ctx, KV_LORA), F32), jnp.zeros((b_ctx, DEPTH, t_ctx, MLA_ROPE), F32),
                jnp.zeros(state_shape, F32), jnp.zeros(state_shape, F32))
    for l in range(DEPTH):
        final = l == DEPTH - 1
        no_pairs = jnp.zeros((N_EXPERTS, LANES), F32)
        (x1c, hc, eidc, rankc, ewc, cnt_c), ctx_outs = _layer_front(xp, mods, weights, l, consts, t_ctx, 0, None,
                                                                    tm, t_ctx, no_pairs, ctx_outs)
        (x1s, hs, eids, ranks, ews, cnt), _ = _layer_front(xs, mods, weights, l, consts, t_lat, 1, cache,
                                                           tm, 256, cnt_c)
        yg_c, yg_s = _moe_experts([hc, hs], jnp.concatenate([eidc, eids], axis=1),
                                  jnp.concatenate([rankc, ranks], axis=1), cnt, weights, l)
        xp = _moe_combine(yg_c, 0, ewc, x1c, mods, l, nf, tm, t_ctx, 0, final)
        xs = _moe_combine(yg_s, 0, ews, x1s, mods, l, nf, tm, t_lat, 1, final)

    y_prompt = xp.reshape(b_ctx, t_ctx, D_MODEL)
    y_sample = xs.reshape(b_lat, t_lat, D_MODEL)
    new_ckv, new_kpe, new_gdn, new_rwkv = ctx_outs
    return (y_prompt, y_sample, new_ckv, new_kpe, new_gdn, new_rwkv)


def _embed_block_diag_layers(s):
    b = s.shape[0]
    return _embed_block_diag(s.reshape(b * DEPTH, 2, HEADS, HEAD_DIM, HEAD_DIM)).reshape(
        b, DEPTH, 2, MIX_W, MIX_W)
```

```python
import functools

import numpy as np
import jax
import jax.numpy as jnp
from jax import lax
from jax.experimental import pallas as pl
from jax.experimental.pallas import tpu as pltpu
from jax.experimental.pallas import tpu_sc as plsc

F32 = jnp.float32
BF16 = jnp.bfloat16

D_MODEL = 1024
DEPTH = 2
PAST_LEN = 512
GRID_W = 64
NORM_EPS = 1e-6

MLA_HEADS = 4
MLA_NOPE = 128
MLA_ROPE = 64
MLA_V = 128
Q_LORA = 384
KV_LORA = 256
ROPE_BASE = 10000.0
MLA_SCALE = (MLA_NOPE + MLA_ROPE) ** -0.5

HEADS = 4
HEAD_DIM = 64
MIX_W = HEADS * HEAD_DIM
GDN_CONV_CH = 3 * MIX_W
CHUNK = 64
RWKV_GN_EPS = 64e-5

N_EXPERTS = 64
TOP_K = 8
N_GROUPS = 8
GROUP_SIZE = N_EXPERTS // N_GROUPS
TOPK_GROUPS = 4
D_EXPERT = 256
ROUTE_SCALE = 2.5

P_MLA = Q_LORA + KV_LORA + MLA_ROPE
P_GDN = GDN_CONV_CH + MIX_W + 4 * HEADS
P_RWKV = 3 * MIX_W + 128 + 128 + 128

LANES = 128
ZM_W = Q_LORA + KV_LORA + 2 * LANES
ZG_W = GDN_CONV_CH + MIX_W
ZR_W = P_RWKV
QH_W = 2 * LANES
VMEM_LIMIT = 56 * 1024 * 1024
POST_TM = 512

_ROPE_SWAP = np.concatenate([np.arange(16, 32), np.arange(0, 16), np.arange(48, 64), np.arange(32, 48)])


def _sigmoid(x):
    return 1.0 / (1.0 + jnp.exp(-x))


def _silu(x):
    return x * _sigmoid(x)


def _softplus(x):
    return jnp.maximum(x, 0.0) + jnp.log(1.0 + jnp.exp(-jnp.abs(x)))


def _rms(x, g, eps=NORM_EPS):
    return x * lax.rsqrt(jnp.mean(x * x, axis=-1, keepdims=True) + eps) * g


def _mm(a, b):
    return jnp.dot(a.astype(BF16), b.astype(BF16), preferred_element_type=F32)


def _mm_nt(a, b):
    return lax.dot_general(a.astype(BF16), b.astype(BF16), (((1,), (1,)), ((), ())),
                           preferred_element_type=F32)


def _mm_tn(a, b):
    return lax.dot_general(a.astype(BF16), b.astype(BF16), (((0,), (0,)), ((), ())),
                           preferred_element_type=F32)


def _split3(x):
    p1 = x.astype(BF16)
    r1 = x - p1.astype(F32)
    p2 = r1.astype(BF16)
    r2 = r1 - p2.astype(F32)
    return p1, p2, r2.astype(BF16)


def _mm_sel_l(sel, x):
    p1, p2, p3 = _split3(x)
    return _mm(sel, p1) + _mm(sel, p2) + _mm(sel, p3)


def _mm_sel_r(x, sel):
    p1, p2, p3 = _split3(x)
    return _mm(p1, sel) + _mm(p2, sel) + _mm(p3, sel)


def _iota(shape, dim):
    return lax.broadcasted_iota(jnp.int32, shape, dim)


def _layer_spec(a, layer, **kw):
    nd = a.ndim - 1
    return pl.BlockSpec((None,) + a.shape[1:], lambda *_: (layer,) + (0,) * nd, **kw)


def _const_spec(a, **kw):
    return pl.BlockSpec(a.shape, lambda *_: (0,) * a.ndim, **kw)


def _bd(x, maskbd):
    xb = x.astype(BF16)
    return jnp.concatenate([xb] * HEADS, axis=0) * maskbd


def _chunk_masks(rev):
    row = _iota((CHUNK, MIX_W), 0)
    col = jnp.bitwise_and(_iota((CHUNK, MIX_W), 1), HEAD_DIM - 1)
    r2 = _iota((CHUNK, CHUNK), 0)
    c2 = _iota((CHUNK, CHUNK), 1)
    if rev:
        inc, strict, tri = row <= col, row < col, r2 <= c2
    else:
        inc, strict, tri = row >= col, row > col, r2 >= c2
    eye = jnp.where(row == col, 1.0, 0.0).astype(F32)
    return inc, strict, jnp.where(tri, 1.0, 0.0).astype(BF16), eye


def _split2(x):
    hi = x.astype(BF16)
    return hi, (x - hi.astype(F32)).astype(BF16)


def _mm_bd3(x, p, maskbd):
    n = x.shape[0]
    xh, xl = _split2(x)
    ph, pl_ = _split2(p)
    r = jnp.dot(jnp.concatenate([xh, xl], axis=0), _bd(ph, maskbd), preferred_element_type=F32)
    return r[:n] + r[n:] + jnp.dot(xh, _bd(pl_, maskbd), preferred_element_type=F32)


def _neumann_inverse(a_list, eye_list, maskbd):
    bs = [-a for a in a_list]
    ms = [eye + b for eye, b in zip(eye_list, bs)]
    ps = [_mm_bd3(b, b, maskbd) for b in bs]
    for _ in range(4):
        boths = [_mm_bd3(jnp.concatenate([m, p], axis=0), p, maskbd) for m, p in zip(ms, ps)]
        ms = [m + both[:CHUNK] for m, both in zip(ms, boths)]
        ps = [both[CHUNK:] for both in boths]
    return [m + _mm_bd3(m, p, maskbd) for m, p in zip(ms, ps)]


def _adaln_kernel(c_ref, w_ref, b_ref, o_ref):
    cv = c_ref[...]
    o_ref[0] = _mm(_silu(cv), w_ref[0]) + b_ref[0]


def _adaln(cvec8, ada_w, ada_b):
    tn = 768
    n_out = 6 * D_MODEL
    return pl.pallas_call(
        _adaln_kernel,
        grid=(DEPTH, n_out // tn),
        in_specs=[
            pl.BlockSpec((8, D_MODEL), lambda l, j: (0, 0)),
            pl.BlockSpec((1, D_MODEL, tn), lambda l, j: (l, 0, j)),
            pl.BlockSpec((1, 1, tn), lambda l, j: (l, 0, j)),
        ],
        out_specs=pl.BlockSpec((1, 8, tn), lambda l, j: (l, 0, j)),
        out_shape=jax.ShapeDtypeStruct((DEPTH, 8, n_out), F32),
        compiler_params=pltpu.CompilerParams(dimension_semantics=("arbitrary", "arbitrary"),
                                             vmem_limit_bytes=VMEM_LIMIT),
        name="adaln",
    )(cvec8, ada_w, ada_b.reshape(DEPTH, 1, n_out))


_KPE0 = Q_LORA + KV_LORA
_W_IN_MOVES = (
    [(0, 0, P_MLA)]
    + [(P_MLA + LANES - MLA_ROPE + 16 * j, _KPE0 + 16 * int(_ROPE_SWAP[16 * j] // 16), 16) for j in range(4)]
    + [(ZM_W, P_MLA, ZG_W), (ZM_W + ZG_W, P_MLA + ZG_W, 4 * HEADS), (ZM_W + ZG_W + LANES, P_MLA + P_GDN, P_RWKV)]
)
W_IN_PAD = ZM_W + ZG_W + LANES + ZR_W


def _inproj_kernel(x_ref, mod_ref, n1_ref, w_ref, zm_ref, zg_ref, zab_ref, zr_ref, w_s):
    @pl.when(pl.program_id(0) == 0)
    def _():
        w_s[...] = jnp.zeros(w_s.shape, BF16)
        for dst, src, width in _W_IN_MOVES:
            w_s[:, dst:dst + width] = w_ref[:, src:src + width].astype(BF16)

    m = mod_ref[0]
    sh = m[:, 0:D_MODEL]
    sc = m[:, D_MODEL:2 * D_MODEL]
    h = _rms(x_ref[...], n1_ref[...]) * (1.0 + sc) + sh
    z = _mm(h, w_s[...])
    o1 = ZM_W
    o2 = o1 + ZG_W
    o3 = o2 + LANES
    zm_ref[...] = z[:, :o1]
    zg_ref[...] = z[:, o1:o2]
    zab_ref[...] = z[:, o2:o3]
    zr_ref[...] = z[:, o3:]


def _inproj(x2d, mods, lw, layer, tm, seq_len, mod_base):
    n = x2d.shape[0]
    tiles_per_seq = seq_len // tm if mod_base else 1

    def mod_idx(i):
        return (layer, mod_base + i // tiles_per_seq if mod_base else 0, 0, 0)

    return pl.pallas_call(
        _inproj_kernel,
        grid=(n // tm,),
        in_specs=[
            pl.BlockSpec((tm, D_MODEL), lambda i: (i, 0)),
            pl.BlockSpec((None, 1, 1, 6 * D_MODEL), mod_idx),
            _layer_spec(lw["norm1"], layer),
            _layer_spec(lw["w_in"], layer, pipeline_mode=pl.Buffered(1)),
        ],
        out_specs=[
            pl.BlockSpec((tm, ZM_W), lambda i: (i, 0)),
            pl.BlockSpec((tm, ZG_W), lambda i: (i, 0)),
            pl.BlockSpec((tm, LANES), lambda i: (i, 0)),
            pl.BlockSpec((tm, ZR_W), lambda i: (i, 0)),
        ],
        out_shape=[
            jax.ShapeDtypeStruct((n, ZM_W), F32),
            jax.ShapeDtypeStruct((n, ZG_W), F32),
            jax.ShapeDtypeStruct((n, LANES), F32),
            jax.ShapeDtypeStruct((n, ZR_W), F32),
        ],
        scratch_shapes=[pltpu.VMEM((D_MODEL, W_IN_PAD), BF16)],
        compiler_params=pltpu.CompilerParams(dimension_semantics=("arbitrary",),
                                             vmem_limit_bytes=VMEM_LIMIT),
        name="inproj",
    )(x2d, mods, lw["norm1"], lw["w_in"])


MLA_SEQ_PER_STEP = 4
Q_HEAD_W = MLA_NOPE + MLA_ROPE


def _arrange_w_uq(w_ref, wq_s, wqs_s=None):
    wq_s[...] = jnp.zeros(wq_s.shape, BF16)
    if wqs_s is not None:
        wqs_s[...] = jnp.zeros(wqs_s.shape, BF16)
    for h in range(MLA_HEADS):
        src, dst = h * Q_HEAD_W, h * QH_W
        wq_s[:, dst:dst + Q_HEAD_W] = w_ref[:, src:src + Q_HEAD_W].astype(BF16)
        if wqs_s is not None:
            for j in range(4):
                s0 = src + MLA_NOPE + 16 * int(_ROPE_SWAP[16 * j] // 16)
                wqs_s[:, dst + MLA_NOPE + 16 * j:dst + MLA_NOPE + 16 * (j + 1)] = w_ref[:, s0:s0 + 16].astype(BF16)


def _mla_ctx_kernel(zm_ref, qn_ref, wuq_ref, kvn_ref, wukv_ref, prev_ckv, prev_kpe, o_ref, ckv_ref, kpe_ref, wq_s,
                    *, seq_len, n_seq):
    del prev_ckv, prev_kpe

    @pl.when(pl.program_id(0) == 0)
    def _():
        _arrange_w_uq(wuq_ref, wq_s)

    o_kpe = Q_LORA + KV_LORA
    zm = zm_ref[...]
    ckv = _rms(zm[:, Q_LORA:o_kpe], kvn_ref[...])
    kpe = zm[:, o_kpe:o_kpe + LANES]
    kv = _mm(ckv, wukv_ref[...])
    q = _mm(_rms(zm[:, :Q_LORA], qn_ref[...]), wq_s[...])
    kpe_b = kpe.astype(BF16)
    seqs = [slice(i * seq_len, (i + 1) * seq_len) for i in range(n_seq)]
    for i, r in enumerate(seqs):
        ckv_ref[i] = ckv[r]
        kpe_ref[i] = kpe[r, :MLA_ROPE]
    pairs = [(r, h * QH_W) for r in seqs for h in range(MLA_HEADS)]
    scores = [_mm_nt(q[r, c0:c0 + QH_W], jnp.concatenate([kv[r, c0:c0 + LANES].astype(BF16), kpe_b[r]], axis=1))
              * MLA_SCALE for r, c0 in pairs]
    exps = [jnp.exp(s - jnp.max(s, axis=-1, keepdims=True)) for s in scores]
    outs = [_mm(e, kv[r, c0 + LANES:c0 + QH_W]) / jnp.sum(e, axis=-1, keepdims=True)
            for e, (r, c0) in zip(exps, pairs)]
    for o, (r, c0) in zip(outs, pairs):
        h = c0 // QH_W
        o_ref[r, h * MLA_V:(h + 1) * MLA_V] = o


def _mla_ctx(zm, lw, layer, seq_len, prev):
    n = zm.shape[0]
    nb = n // seq_len
    n_seq = MLA_SEQ_PER_STEP
    rows = n_seq * seq_len
    weights = [lw["q_norm"], lw["w_uq"], lw["kv_norm"], lw["w_ukv"]]
    return pl.pallas_call(
        functools.partial(_mla_ctx_kernel, seq_len=seq_len, n_seq=n_seq),
        grid=(nb // n_seq,),
        in_specs=[pl.BlockSpec((rows, ZM_W), lambda b: (b, 0))] + [_layer_spec(a, layer) for a in weights]
        + [pl.BlockSpec(memory_space=pl.ANY)] * 2,
        out_specs=[
            pl.BlockSpec((rows, MLA_HEADS * MLA_V), lambda b: (b, 0)),
            pl.BlockSpec((n_seq, None, seq_len, KV_LORA), lambda b: (b, layer, 0, 0)),
            pl.BlockSpec((n_seq, None, seq_len, MLA_ROPE), lambda b: (b, layer, 0, 0)),
        ],
        out_shape=[
            jax.ShapeDtypeStruct((n, MLA_HEADS * MLA_V), F32),
            jax.ShapeDtypeStruct((nb, DEPTH, seq_len, KV_LORA), F32),
            jax.ShapeDtypeStruct((nb, DEPTH, seq_len, MLA_ROPE), F32),
        ],
        input_output_aliases={5: 1, 6: 2},
        scratch_shapes=[pltpu.VMEM((Q_LORA, MLA_HEADS * QH_W), BF16)],
        compiler_params=pltpu.CompilerParams(dimension_semantics=("arbitrary",),
                                             vmem_limit_bytes=VMEM_LIMIT),
        name="mla_ctx",
    )(zm, *weights, *prev)


def _mla_lat_kernel(zm_ref, cckv_ref, ckpe_ref, rc_ref, rs_ref, qn_ref, wuq_ref, kvn_ref, wukv_ref,
                    o_ref, k_s, v_s, wq_s, wqs_s, *, seq_len, tq, past):
    qi = pl.program_id(1)
    o_kpe = Q_LORA + KV_LORA

    @pl.when((pl.program_id(0) == 0) & (qi == 0))
    def _():
        _arrange_w_uq(wuq_ref, wq_s, wqs_s)

    @pl.when(qi == 0)
    def _():
        zm = zm_ref[...]
        ckv = _rms(zm[:, Q_LORA:o_kpe], kvn_ref[...])
        kpe = zm[:, o_kpe:o_kpe + LANES] * rc_ref[...] + zm[:, o_kpe + LANES:o_kpe + 2 * LANES] * rs_ref[...]
        kvc = _mm(cckv_ref[0], wukv_ref[...])
        kpc = ckpe_ref[0].astype(BF16)
        kv = _mm(ckv, wukv_ref[...])
        kpe = kpe.astype(BF16)
        for h in range(MLA_HEADS):
            c0 = h * QH_W
            k_s[0:past, c0:c0 + LANES] = kvc[:, c0:c0 + LANES].astype(BF16)
            k_s[0:past, c0 + LANES:c0 + QH_W] = kpc
            v_s[0:past, h * MLA_V:(h + 1) * MLA_V] = kvc[:, c0 + LANES:c0 + QH_W].astype(BF16)
            k_s[past:past + seq_len, c0:c0 + LANES] = kv[:, c0:c0 + LANES].astype(BF16)
            k_s[past:past + seq_len, c0 + LANES:c0 + QH_W] = kpe
            v_s[past:past + seq_len, h * MLA_V:(h + 1) * MLA_V] = kv[:, c0 + LANES:c0 + QH_W].astype(BF16)

    r0 = pl.multiple_of(qi * tq, tq)
    zq = zm_ref[pl.ds(r0, tq), :]
    cq = _rms(zq[:, :Q_LORA], qn_ref[...])
    q = _mm(cq, wq_s[...])
    qs = _mm(cq, wqs_s[...])
    qc = jnp.concatenate([jnp.ones((tq, LANES), F32), rc_ref[pl.ds(r0, tq), :]], axis=1)
    qsn = jnp.concatenate([jnp.zeros((tq, LANES), F32), rs_ref[pl.ds(r0, tq), :]], axis=1)
    for h in range(MLA_HEADS):
        c0 = h * QH_W
        qh = q[:, c0:c0 + QH_W] * qc + qs[:, c0:c0 + QH_W] * qsn
        s = _mm_nt(qh, k_s[:, c0:c0 + QH_W]) * MLA_SCALE
        e = jnp.exp(s - jnp.max(s, axis=-1, keepdims=True))
        den = jnp.sum(e, axis=-1, keepdims=True)
        o_ref[:, h * MLA_V:(h + 1) * MLA_V] = _mm(e, v_s[:, h * MLA_V:(h + 1) * MLA_V]) / den


def _mla_lat(zm, lw, layer, seq_len, tq, cache):
    n = zm.shape[0]
    nb = n // seq_len
    past = PAST_LEN
    tk = past + seq_len
    cckv, ckpe, rc, rs = cache
    weights = [lw["q_norm"], lw["w_uq"], lw["kv_norm"], lw["w_ukv"]]
    return pl.pallas_call(
        functools.partial(_mla_lat_kernel, seq_len=seq_len, tq=tq, past=past),
        grid=(nb, seq_len // tq),
        in_specs=[pl.BlockSpec((seq_len, ZM_W), lambda b, q: (b, 0)),
                  pl.BlockSpec((1, None, past, KV_LORA), lambda b, q: (b, layer, 0, 0)),
                  pl.BlockSpec((1, None, past, LANES), lambda b, q: (b, layer, 0, 0)),
                  _const_spec(rc), _const_spec(rs)] + [_layer_spec(a, layer) for a in weights],
        out_specs=pl.BlockSpec((tq, MLA_HEADS * MLA_V), lambda b, q: (b * (seq_len // tq) + q, 0)),
        out_shape=jax.ShapeDtypeStruct((n, MLA_HEADS * MLA_V), F32),
        scratch_shapes=[
            pltpu.VMEM((tk, MLA_HEADS * QH_W), BF16),
            pltpu.VMEM((tk, MLA_HEADS * MLA_V), BF16),
            pltpu.VMEM((Q_LORA, MLA_HEADS * QH_W), BF16),
            pltpu.VMEM((Q_LORA, MLA_HEADS * QH_W), BF16),
        ],
        compiler_params=pltpu.CompilerParams(dimension_semantics=("arbitrary", "arbitrary"),
                                             vmem_limit_bytes=VMEM_LIMIT),
        name="mla_lat",
    )(zm, cckv, ckpe, rc, rs, *weights)


SEQ_PER_STEP = 4
CHUNK_GROUP = 4
GDN_PRE_W = 5 * MIX_W


def _for_chunk_groups(n_chunks, fn):
    if n_chunks == CHUNK_GROUP:
        fn(0)
    else:
        def body(gi, carry):
            fn(gi * CHUNK_GROUP)
            return carry
        lax.fori_loop(0, n_chunks // CHUNK_GROUP, body, 0)


def _gdn_prepare(items, maskbd):
    n = range(len(items))
    qs, ks, vs, gs, betas, masks, revs = zip(*items)
    gcs = [_mm_sel_l(masks[i][2], gs[i]) for i in n]
    decays = []
    for i in n:
        inc, eye = masks[i][0], masks[i][3]
        gc_row = jnp.sum(eye * gcs[i], axis=0, keepdims=True)
        decays.append(jnp.where(inc, jnp.exp(jnp.where(inc, gcs[i] - gc_row, 0.0)), 0.0))
    kbs = [ks[i] * betas[i] for i in n]
    aqs = [_mm_nt(jnp.concatenate([kbs[i], qs[i]], axis=0), _bd(ks[i], maskbd)) for i in n]
    a_mats = [jnp.where(masks[i][1], aqs[i][:CHUNK] * decays[i], 0.0) for i in n]
    t_invs = _neumann_inverse(a_mats, [m[3] for m in masks], maskbd)
    egcs = [jnp.exp(gc) for gc in gcs]
    uws = [_mm(t_invs[i], jnp.concatenate([_bd(vs[i] * betas[i], maskbd), _bd(kbs[i] * egcs[i], maskbd)], axis=1))
           for i in n]
    out = []
    for i in n:
        g_last = gcs[i][0:1] if revs[i] else gcs[i][CHUNK - 1:CHUNK]
        pre = jnp.concatenate([uws[i], qs[i] * egcs[i], aqs[i][CHUNK:] * decays[i],
                               ks[i] * jnp.exp(g_last - gcs[i])], axis=1)
        out.append((pre, jnp.broadcast_to(jnp.exp(g_last), (8, MIX_W))))
    return out


def _gdn_step(items, maskbd):
    w = MIX_W
    n = range(len(items))
    pres, egls, states = zip(*items)
    wqs = [_mm(jnp.concatenate([pres[i][:, w:2 * w], pres[i][:, 2 * w:3 * w]], axis=0), states[i]) for i in n]
    v_news = [pres[i][:, :w] - wqs[i][:CHUNK] for i in n]
    outs = [wqs[i][CHUNK:] + _mm(pres[i][:, 3 * w:4 * w], _bd(v_news[i], maskbd)) for i in n]
    upds = [_mm_tn(pres[i][:, 4 * w:], v_news[i]) for i in n]
    mask_f = maskbd.astype(F32)
    return [(outs[i], states[i] * egls[i] + upds[i] * mask_f) for i in n]


def _gdn_kernel(*refs, seq_len, n_seq, cached):
    if cached:
        (zg_ref, zab_ref, s0_ref, conv_ref, alog_ref, dtb_ref, gn_ref, eexp_ref, ones_ref, maskbd_ref,
         o_ref, sout_ref, q_s, k_s, v_s, ge_s, pre_s, gl_s, oacc_s, st_s) = refs
    else:
        zg_ref, zab_ref, conv_ref, alog_ref, dtb_ref, gn_ref, eexp_ref, ones_ref, maskbd_ref = refs[:9]
        o_ref, sout_ref, q_s, k_s, v_s, ge_s, pre_s, gl_s, oacc_s, st_s = refs[-10:]
    t = seq_len * n_seq
    z = zg_ref[:, :GDN_CONV_CH]
    rowi = jnp.bitwise_and(_iota((t, 1), 0), seq_len - 1)
    zp = jnp.where(rowi == 0, 0.0, pltpu.roll(z, 1, 0))
    zn = jnp.where(rowi == seq_len - 1, 0.0, pltpu.roll(z, t - 1, 0))
    cw = conv_ref[...]
    qkv = _silu(zp * cw[0:1] + z * cw[1:2] + zn * cw[2:3])
    ones_bd = ones_ref[...]
    q = qkv[:, :MIX_W]
    k = qkv[:, MIX_W:2 * MIX_W]
    q_s[...] = q * lax.rsqrt(_mm_sel_r(q * q, ones_bd) + 1e-6) * (HEAD_DIM ** -0.5)
    k_s[...] = k * lax.rsqrt(_mm_sel_r(k * k, ones_bd) + 1e-6)
    v_s[...] = qkv[:, 2 * MIX_W:]
    ab = zab_ref[...]
    lane = _iota((t, LANES), 1)
    gb = jnp.where(lane < 2 * HEADS, -jnp.exp(alog_ref[...]) * _softplus(ab + dtb_ref[...]), _sigmoid(ab))
    ge_s[...] = _mm_sel_r(gb, eexp_ref[...])
    oacc_s[...] = jnp.zeros((t, MIX_W), F32)
    if cached:
        st_s[...] = s0_ref[...]
    else:
        st_s[...] = jnp.zeros((n_seq, 2, MIX_W, MIX_W), F32)
    maskbd = maskbd_ref[...]
    masks = (_chunk_masks(False), _chunk_masks(True))
    n_chunks = seq_len // CHUNK

    def prepare_group(c0):
        where, items = [], []
        for j in range(CHUNK_GROUP):
            c = c0 + j
            rows = pl.ds(pl.multiple_of(c * CHUNK, CHUNK), CHUNK)
            for d in range(2):
                where.append((d, c, rows))
                items.append((q_s[rows, :], k_s[rows, :], v_s[rows, :], ge_s[rows, d * MIX_W:(d + 1) * MIX_W],
                              ge_s[rows, (2 + d) * MIX_W:(3 + d) * MIX_W], masks[d], d == 1))
        for (d, c, rows), (pre, egl) in zip(where, _gdn_prepare(items, maskbd)):
            pre_s[d, rows, :] = pre
            gl_s[d, pl.ds(pl.multiple_of(c * 8, 8), 8), :] = egl

    _for_chunk_groups(n_chunks * n_seq, prepare_group)

    def body(i, carry):
        where, items = [], []
        for q in range(n_seq):
            for d in range(2):
                c = q * n_chunks + (i if d == 0 else n_chunks - 1 - i)
                rows = pl.ds(pl.multiple_of(c * CHUNK, CHUNK), CHUNK)
                where.append((q, d, rows))
                items.append((pre_s[d, rows, :], gl_s[d, pl.ds(pl.multiple_of(c * 8, 8), 8), :][0:1], st_s[q, d]))
        for (q, d, rows), (o, s_new) in zip(where, _gdn_step(items, maskbd)):
            oacc_s[rows, :] = oacc_s[rows, :] + o
            st_s[q, d] = s_new
        return carry

    lax.fori_loop(0, n_chunks, body, 0)
    o = oacc_s[...]
    ms = _mm_sel_r(o * o, ones_bd) * (1.0 / HEAD_DIM)
    gate = zg_ref[:, GDN_CONV_CH:]
    o_ref[...] = o * lax.rsqrt(ms + NORM_EPS) * gn_ref[...] * _silu(gate)
    for q in range(n_seq):
        for d in range(2):
            for h in range(HEADS):
                sl = slice(h * HEAD_DIM, (h + 1) * HEAD_DIM)
                sout_ref[q, d, h] = st_s[q, d, sl, sl]


def _state_out(nb, n_seq, layer, cached, prev, args, in_specs):
    if cached:
        return (pl.BlockSpec((n_seq, 2, HEADS, HEAD_DIM, HEAD_DIM), lambda b: (b, 0, 0, 0, 0)),
                jax.ShapeDtypeStruct((nb, 2, HEADS, HEAD_DIM, HEAD_DIM), F32), {})
    aliases = {len(args): 1}
    args.append(prev)
    in_specs.append(pl.BlockSpec(memory_space=pl.ANY))
    return (pl.BlockSpec((n_seq, None, 2, HEADS, HEAD_DIM, HEAD_DIM), lambda b: (b, layer, 0, 0, 0, 0)),
            jax.ShapeDtypeStruct((nb, DEPTH, 2, HEADS, HEAD_DIM, HEAD_DIM), F32), aliases)


def _gdn(zg, zab, lw, layer, consts, seq_len, s0_bd=None, prev=None):
    n = zg.shape[0]
    nb = n // seq_len
    cached = s0_bd is not None
    n_seq = 1 if cached else SEQ_PER_STEP
    assert seq_len & (seq_len - 1) == 0 and seq_len % (CHUNK * CHUNK_GROUP) == 0 and nb % n_seq == 0
    rows = seq_len * n_seq
    args = [zg, zab]
    in_specs = [pl.BlockSpec((rows, ZG_W), lambda b: (b, 0)),
                pl.BlockSpec((rows, LANES), lambda b: (b, 0))]
    if cached:
        args.append(s0_bd)
        in_specs.append(pl.BlockSpec((1, None, 2, MIX_W, MIX_W), lambda b: (b, layer, 0, 0, 0)))
    layered = [lw["gdn_conv"], lw["gdn_alog"], lw["gdn_dtb"], lw["gdn_norm"]]
    const = [consts["eexp"], consts["ones_bd"], consts["maskbd"]]
    args += layered + const
    in_specs += [_layer_spec(a, layer) for a in layered] + [_const_spec(a) for a in const]
    s_spec, s_shape, aliases = _state_out(nb, n_seq, layer, cached, prev, args, in_specs)
    return pl.pallas_call(
        functools.partial(_gdn_kernel, seq_len=seq_len, n_seq=n_seq, cached=cached),
        grid=(nb // n_seq,),
        in_specs=in_specs,
        out_specs=[
            pl.BlockSpec((rows, MIX_W), lambda b: (b, 0)),
            s_spec,
        ],
        out_shape=[jax.ShapeDtypeStruct((n, MIX_W), F32), s_shape],
        input_output_aliases=aliases,
        scratch_shapes=[
            pltpu.VMEM((rows, MIX_W), F32),
            pltpu.VMEM((rows, MIX_W), F32),
            pltpu.VMEM((rows, MIX_W), F32),
            pltpu.VMEM((rows, 4 * MIX_W), F32),
            pltpu.VMEM((2, rows, GDN_PRE_W), F32),
            pltpu.VMEM((2, rows // CHUNK * 8, MIX_W), F32),
            pltpu.VMEM((rows, MIX_W), F32),
            pltpu.VMEM((n_seq, 2, MIX_W, MIX_W), F32),
        ],
        compiler_params=pltpu.CompilerParams(dimension_semantics=("arbitrary",),
                                             vmem_limit_bytes=VMEM_LIMIT),
        name="gdn_lat" if cached else "gdn_ctx",
    )(*args)


RWKV_PRE_W = 7 * MIX_W


def _rwkv_prepare(items, maskbd, eye_full):
    n = range(len(items))
    rs, kds, vs, kks, bs, lws, masks, revs = zip(*items)
    cums = [_mm_sel_l(masks[i][2], lws[i]) for i in n]
    einvs = [jnp.exp(-c) for c in cums]
    kts = [kks[i] * jnp.exp(cums[i] - lws[i]) for i in n]
    rts = [rs[i] * jnp.exp(cums[i]) for i in n]
    krs = [jnp.concatenate([kts[i], rts[i]], axis=0) for i in n]
    lb_alls = [_mm_nt(krs[i], _bd(bs[i] * einvs[i], maskbd)) for i in n]
    lk_alls = [_mm_nt(krs[i], _bd(kds[i] * einvs[i], maskbd)) for i in n]
    lbs = [jnp.where(masks[i][1], lb_alls[i][:CHUNK], 0.0) for i in n]
    t_invs = _neumann_inverse(lbs, [m[3] for m in masks], maskbd)
    lvs = [_mm(jnp.concatenate([jnp.where(masks[i][1], lk_alls[i][:CHUNK], 0.0),
                                jnp.where(masks[i][0], lk_alls[i][CHUNK:], 0.0)], axis=0), _bd(vs[i], maskbd))
           for i in n]
    tkps = [_mm(t_invs[i], jnp.concatenate([_bd(kts[i], maskbd), _bd(lvs[i][:CHUNK], maskbd)], axis=1)) for i in n]
    out = []
    for i in n:
        c_last = cums[i][0:1] if revs[i] else cums[i][CHUNK - 1:CHUNK]
        tail = jnp.exp(c_last - cums[i])
        rb = jnp.where(masks[i][0], lb_alls[i][CHUNK:], 0.0)
        pre = jnp.concatenate([tkps[i][:, :MIX_W], rts[i], tkps[i][:, MIX_W:], lvs[i][CHUNK:], rb,
                               kds[i] * tail, bs[i] * tail], axis=1)
        gcol = jnp.sum(eye_full * jnp.exp(c_last), axis=1, keepdims=True)
        out.append((pre, jnp.broadcast_to(gcol, (MIX_W, LANES))))
    return out


def _rwkv_step(items, maskbd):
    w = MIX_W
    n = range(len(items))
    pres, vs, gcols, states = zip(*items)
    prs = [_mm(jnp.concatenate([pres[i][:, :w], pres[i][:, w:2 * w]], axis=0), states[i]) for i in n]
    ps = [prs[i][:CHUNK] + pres[i][:, 2 * w:3 * w] for i in n]
    outs = [prs[i][CHUNK:] + pres[i][:, 3 * w:4 * w] - _mm(pres[i][:, 4 * w:5 * w], _bd(ps[i], maskbd)) for i in n]
    upds = [_mm_tn(jnp.concatenate([pres[i][:, 5 * w:6 * w], pres[i][:, 6 * w:]], axis=0),
                   jnp.concatenate([vs[i], -ps[i]], axis=0)) for i in n]
    mask_f = maskbd.astype(F32)
    return [(outs[i], states[i] * jnp.concatenate([gcols[i], gcols[i]], axis=1) + upds[i] * mask_f) for i in n]


def _rwkv_kernel(*refs, seq_len, n_seq, cached):
    if cached:
        (zr_ref, s0_ref, mup_ref, mun_ref, w0_ref, w2_ref, a0_ref, a2_ref, g2_ref, kk_ref, ka_ref, rk_ref,
         gnw_ref, gnb_ref, ones_ref, maskbd_ref, o_ref, sout_ref,
         r_s, v_s, kk_s, dir_s, bg_s, pre_s, gcol_s, oacc_s, st_s) = refs
    else:
        (zr_ref, mup_ref, mun_ref, w0_ref, w2_ref, a0_ref, a2_ref, g2_ref, kk_ref, ka_ref, rk_ref,
         gnw_ref, gnb_ref, ones_ref, maskbd_ref) = refs[:15]
        o_ref, sout_ref, r_s, v_s, kk_s, dir_s, bg_s, pre_s, gcol_s, oacc_s, st_s = refs[-11:]
    t = seq_len * n_seq
    z = zr_ref[...]
    rowi = jnp.bitwise_and(_iota((t, 1), 0), seq_len - 1)
    zp = jnp.where(rowi == 0, 0.0, pltpu.roll(z, 1, 0))
    zn = jnp.where(rowi == seq_len - 1, 0.0, pltpu.roll(z, t - 1, 0))
    z = z + mup_ref[...] * (zp - z) + mun_ref[...] * (zn - z)
    w = MIX_W
    r = z[:, :w]
    k = z[:, w:2 * w]
    v = z[:, 2 * w:3 * w]
    wd = jnp.tanh(z[:, 3 * w:3 * w + LANES])
    ad = z[:, 3 * w + LANES:3 * w + 2 * LANES]
    gd = _sigmoid(z[:, 3 * w + 2 * LANES:])
    ones_bd = ones_ref[...]
    kk = k * kk_ref[...]
    kk = kk * lax.rsqrt(_mm_sel_r(kk * kk, ones_bd) + 1e-6)
    r_s[...] = r
    v_s[...] = v
    kk_s[...] = kk
    bonus = jnp.zeros((t, w), F32)
    for d in range(2):
        w_log = -_softplus(-(w0_ref[d:d + 1] + _mm(wd, w2_ref[d]))) - 0.5
        a = _sigmoid(a0_ref[d:d + 1] + _mm(ad, a2_ref[d]))
        kd = k * (1.0 + (a - 1.0) * ka_ref[...])
        dir_s[:, (3 * d) * w:(3 * d + 1) * w] = -jnp.exp(w_log)
        dir_s[:, (3 * d + 1) * w:(3 * d + 2) * w] = kd
        dir_s[:, (3 * d + 2) * w:(3 * d + 3) * w] = kk * a
        bonus = bonus + _mm_sel_r(r * kd * rk_ref[...], ones_bd) * v
    bg_s[:, :w] = bonus
    bg_s[:, w:] = _mm(gd, g2_ref[...])
    oacc_s[...] = jnp.zeros((t, w), F32)
    if cached:
        st_s[...] = s0_ref[...]
    else:
        st_s[...] = jnp.zeros((n_seq, 2, w, w), F32)
    maskbd = maskbd_ref[...]
    masks = (_chunk_masks(False), _chunk_masks(True))
    eye_full = jnp.where(_iota((w, w), 0) == _iota((w, w), 1), 1.0, 0.0).astype(F32)
    n_chunks = seq_len // CHUNK

    def prepare_group(c0):
        where, items = [], []
        for j in range(CHUNK_GROUP):
            c = c0 + j
            rows = pl.ds(pl.multiple_of(c * CHUNK, CHUNK), CHUNK)
            for d in range(2):
                where.append((d, c, rows))
                items.append((r_s[rows, :], dir_s[rows, (3 * d + 1) * w:(3 * d + 2) * w], v_s[rows, :], kk_s[rows, :],
                              dir_s[rows, (3 * d + 2) * w:(3 * d + 3) * w], dir_s[rows, (3 * d) * w:(3 * d + 1) * w],
                              masks[d], d == 1))
        for (d, c, rows), (pre, gcol) in zip(where, _rwkv_prepare(items, maskbd, eye_full)):
            pre_s[d, rows, :] = pre
            gcol_s[d, pl.ds(pl.multiple_of(c * w, w), w), :] = gcol

    _for_chunk_groups(n_chunks * n_seq, prepare_group)

    def body(i, carry):
        where, items = [], []
        for q in range(n_seq):
            for d in range(2):
                c = q * n_chunks + (i if d == 0 else n_chunks - 1 - i)
                rows = pl.ds(pl.multiple_of(c * CHUNK, CHUNK), CHUNK)
                where.append((q, d, rows))
                items.append((pre_s[d, rows, :], v_s[rows, :], gcol_s[d, pl.ds(pl.multiple_of(c * w, w), w), :],
                              st_s[q, d]))
        for (q, d, rows), (o, z_new) in zip(where, _rwkv_step(items, maskbd)):
            oacc_s[rows, :] = oacc_s[rows, :] + o
            st_s[q, d] = z_new
        return carry

    lax.fori_loop(0, n_chunks, body, 0)
    o = oacc_s[...]
    inv_n = 1.0 / HEAD_DIM
    mu = _mm_sel_r(o, ones_bd) * inv_n
    oc = o - mu
    var = _mm_sel_r(oc * oc, ones_bd) * inv_n
    y = oc * lax.rsqrt(var + RWKV_GN_EPS) * gnw_ref[...] + gnb_ref[...]
    o_ref[...] = (y + bg_s[:, :w]) * bg_s[:, w:]
    for q in range(n_seq):
        for d in range(2):
            state = st_s[q, d] if cached else st_s[q, d].T
            for h in range(HEADS):
                sl = slice(h * HEAD_DIM, (h + 1) * HEAD_DIM)
                sout_ref[q, d, h] = state[sl, sl]


def _rwkv(zr, lw, layer, consts, seq_len, s0_bd=None, prev=None):
    n = zr.shape[0]
    nb = n // seq_len
    cached = s0_bd is not None
    n_seq = 1 if cached else SEQ_PER_STEP
    assert seq_len & (seq_len - 1) == 0 and seq_len % (CHUNK * CHUNK_GROUP) == 0 and nb % n_seq == 0
    rows = seq_len * n_seq
    args = [zr]
    in_specs = [pl.BlockSpec((rows, ZR_W), lambda b: (b, 0))]
    if cached:
        args.append(s0_bd)
        in_specs.append(pl.BlockSpec((1, None, 2, MIX_W, MIX_W), lambda b: (b, layer, 0, 0, 0)))
    layered = [lw["rwkv_mu_prev"], lw["rwkv_mu_next"], lw["rwkv_w0"], lw["rwkv_w2"], lw["rwkv_a0"], lw["rwkv_a2"],
               lw["rwkv_g2"], lw["rwkv_k_k"], lw["rwkv_k_a"], lw["rwkv_r_k"], lw["rwkv_gn_w"], lw["rwkv_gn_b"]]
    const = [consts["ones_bd"], consts["maskbd"]]
    args += layered + const
    in_specs += [_layer_spec(a, layer) for a in layered] + [_const_spec(a) for a in const]
    s_spec, s_shape, aliases = _state_out(nb, n_seq, layer, cached, prev, args, in_specs)
    return pl.pallas_call(
        functools.partial(_rwkv_kernel, seq_len=seq_len, n_seq=n_seq, cached=cached),
        grid=(nb // n_seq,),
        in_specs=in_specs,
        out_specs=[
            pl.BlockSpec((rows, MIX_W), lambda b: (b, 0)),
            s_spec,
        ],
        out_shape=[jax.ShapeDtypeStruct((n, MIX_W), F32), s_shape],
        input_output_aliases=aliases,
        scratch_shapes=[
            pltpu.VMEM((rows, MIX_W), F32),
            pltpu.VMEM((rows, MIX_W), F32),
            pltpu.VMEM((rows, MIX_W), F32),
            pltpu.VMEM((rows, 6 * MIX_W), F32),
            pltpu.VMEM((rows, 2 * MIX_W), F32),
            pltpu.VMEM((2, rows, RWKV_PRE_W), F32),
            pltpu.VMEM((2, rows // CHUNK * MIX_W, LANES), F32),
            pltpu.VMEM((rows, MIX_W), F32),
            pltpu.VMEM((n_seq, 2, MIX_W, MIX_W), F32),
        ],
        compiler_params=pltpu.CompilerParams(dimension_semantics=("arbitrary",),
                                             vmem_limit_bytes=VMEM_LIMIT),
        name="rwkv_lat" if cached else "rwkv_ctx",
    )(*args)


def _route(logits_t, bias):
    tm = logits_t.shape[1]
    neg = -jnp.inf
    sc = _sigmoid(logits_t)
    sc3 = sc.reshape(N_GROUPS, GROUP_SIZE, tm)
    sel = (sc + bias).reshape(N_GROUPS, GROUP_SIZE, tm)
    si = _iota(sel.shape, 1).astype(F32)
    m1 = jnp.max(sel, axis=1, keepdims=True)
    f1 = jnp.min(jnp.where(sel == m1, si, float(GROUP_SIZE)), axis=1, keepdims=True)
    m2 = jnp.max(jnp.where(si == f1, neg, sel), axis=1, keepdims=True)
    grp = m1 + m2
    gi = _iota(grp.shape, 0).astype(F32)
    gsel = jnp.zeros(grp.shape, F32)
    for _ in range(TOPK_GROUPS):
        mx = jnp.max(grp, axis=0, keepdims=True)
        fi = jnp.min(jnp.where(grp == mx, gi, float(N_GROUPS)), axis=0, keepdims=True)
        hit = gi == fi
        gsel = jnp.where(hit, 1.0, gsel)
        grp = jnp.where(hit, neg, grp)
    cur = jnp.where(gsel > 0.0, sel, neg)
    ei = (_iota(cur.shape, 0) * GROUP_SIZE + _iota(cur.shape, 1)).astype(F32)
    chosen = jnp.zeros(cur.shape, F32)
    ids, wts = [], []
    for _ in range(TOP_K):
        mx = jnp.max(jnp.max(cur, axis=0, keepdims=True), axis=1, keepdims=True)
        fi = jnp.min(jnp.min(jnp.where(cur == mx, ei, float(N_EXPERTS)), axis=0, keepdims=True),
                     axis=1, keepdims=True)
        hit = ei == fi
        chosen = jnp.where(hit, 1.0, chosen)
        cur = jnp.where(hit, neg, cur)
        ids.append(fi.reshape(1, tm))
        wts.append(jnp.sum(jnp.sum(jnp.where(hit, sc3, 0.0), axis=0, keepdims=True), axis=1, keepdims=True)
                   .reshape(1, tm))
    w = jnp.concatenate(wts, axis=0)
    w = w / jnp.sum(w, axis=0, keepdims=True) * ROUTE_SCALE
    return chosen.reshape(N_EXPERTS, tm), jnp.concatenate(ids, axis=0), w


def _pack_halves(x):
    half = x.shape[1] // 2
    bits = lax.bitcast_convert_type(x.astype(BF16).astype(F32), jnp.int32)
    lo = lax.shift_right_logical(bits[:, :half], jnp.int32(16))
    return jnp.bitwise_or(lo, jnp.bitwise_and(bits[:, half:], jnp.int32(-65536)))


def _unpack_halves(word):
    lo = lax.bitcast_convert_type(lax.shift_left(word, jnp.int32(16)), F32)
    hi = lax.bitcast_convert_type(jnp.bitwise_and(word, jnp.int32(-65536)), F32)
    return lo, hi


def _post_kernel(x_ref, om_ref, og_ref, or_ref, mod_ref, wo_ref, n2_ref, rt_ref, rb_ref, sgu_ref, sdn_ref,
                 tri_ref, cin_ref, x1_ref, h2_ref, eid_ref, rank_ref, ew_ref, cnt_ref, carry_s):
    @pl.when(pl.program_id(0) == 0)
    def _():
        carry_s[...] = cin_ref[...]

    m = mod_ref[0]
    g1 = m[:, 2 * D_MODEL:3 * D_MODEL]
    sh2 = m[:, 3 * D_MODEL:4 * D_MODEL]
    sc2 = m[:, 4 * D_MODEL:5 * D_MODEL]
    w_mla = MLA_HEADS * MLA_V
    mix = (_mm(om_ref[...], wo_ref[0:w_mla, :]) + _mm(og_ref[...], wo_ref[w_mla:w_mla + MIX_W, :])
           + _mm(or_ref[...], wo_ref[w_mla + MIX_W:, :]))
    x1 = x_ref[...] + g1 * mix
    h2 = _rms(x1, n2_ref[...]) * (1.0 + sc2) + sh2
    h2_ref[...] = _pack_halves(h2)
    g2 = m[:, 5 * D_MODEL:]
    x1_ref[...] = x1 + g2 * _mm(_swiglu_act(_mm(h2, sgu_ref[...])), sdn_ref[...])
    r_hi, r_lo = _split2(rt_ref[...])
    h_hi, h_lo = _split2(h2)
    logits_t = _mm_nt(r_hi, h_hi) + _mm_nt(r_hi, h_lo) + _mm_nt(r_lo, h_hi)
    chosen, ids, w = _route(logits_t, rb_ref[...])
    tm = chosen.shape[1]
    rank_et = (carry_s[:, 0:1] + _mm(chosen, tri_ref[...])).reshape(N_GROUPS, GROUP_SIZE, tm)
    ei = (_iota(rank_et.shape, 0) * GROUP_SIZE + _iota(rank_et.shape, 1)).astype(F32)
    ranks = []
    for k in range(TOP_K):
        pick = jnp.where(ei == ids[k:k + 1].reshape(1, 1, tm), rank_et, 0.0)
        ranks.append(jnp.sum(jnp.sum(pick, axis=0, keepdims=True), axis=1, keepdims=True).reshape(1, tm))
    eid_ref[...] = ids.astype(jnp.int32)
    rank_ref[...] = jnp.concatenate(ranks, axis=0).astype(jnp.int32)
    ew_ref[...] = jnp.concatenate([w, jnp.zeros((LANES - TOP_K, tm), F32)], axis=0).T
    total = carry_s[...] + jnp.sum(chosen, axis=1, keepdims=True)
    carry_s[...] = total
    cnt_ref[...] = total


def _post(x2d, om, og, orw, mods, lw, layer, consts, tm, seq_len, mod_base, counts_in):
    n = x2d.shape[0]
    tiles_per_seq = seq_len // tm if mod_base else 1

    def mod_idx(i):
        return (layer, mod_base + i // tiles_per_seq if mod_base else 0, 0, 0)

    row = lambda w: pl.BlockSpec((tm, w), lambda i: (i, 0))
    col = lambda h: pl.BlockSpec((h, tm), lambda i: (0, i))
    full = lambda a: _layer_spec(a, layer)
    tail = [lw["w_out"], lw["norm2"], lw["router_t"], lw["router_b"], lw["shared_w_gu"], lw["shared_w_down"]]
    tri = consts["tri_tokens"]
    return pl.pallas_call(
        _post_kernel,
        grid=(n // tm,),
        in_specs=[row(D_MODEL), row(MLA_HEADS * MLA_V), row(MIX_W), row(MIX_W),
                  pl.BlockSpec((None, 1, 1, 6 * D_MODEL), mod_idx)] + [full(a) for a in tail]
        + [_const_spec(tri), _const_spec(counts_in)],
        out_specs=[row(D_MODEL), row(D_MODEL // 2), col(TOP_K), col(TOP_K), row(LANES),
                   pl.BlockSpec((N_EXPERTS, LANES), lambda i: (0, 0))],
        out_shape=[
            jax.ShapeDtypeStruct((n, D_MODEL), F32),
            jax.ShapeDtypeStruct((n, D_MODEL // 2), jnp.int32),
            jax.ShapeDtypeStruct((TOP_K, n), jnp.int32),
            jax.ShapeDtypeStruct((TOP_K, n), jnp.int32),
            jax.ShapeDtypeStruct((n, LANES), F32),
            jax.ShapeDtypeStruct((N_EXPERTS, LANES), F32),
        ],
        scratch_shapes=[pltpu.VMEM((N_EXPERTS, LANES), F32)],
        compiler_params=pltpu.CompilerParams(dimension_semantics=("arbitrary",),
                                             vmem_limit_bytes=VMEM_LIMIT),
        name="post",
    )(x2d, om, og, orw, mods, *tail, tri, counts_in)


MOE_ROWS = 256
SC_ROWS = 128
SC_SUBCORES = 32


def _swiglu_act(gu):
    return _silu(gu[:, :D_EXPERT]) * gu[:, D_EXPERT:]


def _group_plan(eid, rank, counts, rows):
    cnt = counts[:, 0].astype(jnp.int32)
    blocks = (cnt + rows - 1) // rows
    first_block = jnp.cumsum(blocks) - blocks
    experts = jnp.arange(N_EXPERTS, dtype=jnp.int32)
    dest = jnp.sum(jnp.where(eid[..., None] == experts, first_block * rows, 0), axis=-1) + rank
    return dest, blocks, first_block


def _block_schedule(plans, n_steps_max):
    (blocks_a, first_a), (blocks_b, first_b) = plans
    experts = jnp.arange(N_EXPERTS, dtype=jnp.int32)
    total = blocks_a + blocks_b
    end = jnp.cumsum(total)
    steps = jnp.arange(n_steps_max, dtype=jnp.int32)
    expert = jnp.minimum(jnp.sum((end[None, :] <= steps[:, None]).astype(jnp.int32), axis=1), N_EXPERTS - 1)
    pick = lambda table: jnp.sum(jnp.where(expert[:, None] == experts, table, 0), axis=1)
    within = steps - pick(end - total)
    in_b = (within >= pick(blocks_a)).astype(jnp.int32)
    n_used = end[-1]
    used = steps < n_used
    block_a = jnp.where(used & (in_b == 0), pick(first_a) + within, -1)
    block_b = jnp.where(used & (in_b == 1), pick(first_b) + within - pick(blocks_a), -1)
    hold = lambda blk: jnp.maximum(lax.cummax(blk), 0)
    prev_expert = jnp.concatenate([jnp.full((1,), -1, jnp.int32), expert[:-1]])
    first = (used & (expert != prev_expert)).astype(jnp.int32)
    slot = jnp.bitwise_and(jnp.cumsum(first) - 1, 1).astype(jnp.int32)
    owner_or_none = jnp.where(total > 0, experts, N_EXPERTS)
    next_owner = jnp.concatenate([lax.cummin(owner_or_none[::-1])[::-1][1:], jnp.full((1,), N_EXPERTS, jnp.int32)])
    schedule = (expert, first, slot, pick(next_owner).astype(jnp.int32), in_b, hold(block_a), hold(block_b))
    return schedule, n_used.astype(jnp.int32)


def _sc_mesh():
    return plsc.VectorSubcoreMesh(core_axis_name="core", subcore_axis_name="subcore")


def _sc_dispatch(x, dest, n_rows):
    n, w = x.shape
    n_chunks = n // SC_ROWS

    @functools.partial(pl.kernel, out_type=jax.ShapeDtypeStruct((n_rows, w), x.dtype), mesh=_sc_mesh(),
                       scratch_types=[pltpu.VMEM((SC_ROWS, w), x.dtype), pltpu.VMEM((TOP_K, SC_ROWS), jnp.int32)])
    def kern(x_hbm, d_hbm, o_hbm, xv, dv):
        sid = lax.axis_index("core") * (SC_SUBCORES // 2) + lax.axis_index("subcore")

        @pl.loop(sid, n_chunks, step=SC_SUBCORES)
        def _(c):
            r0 = pl.multiple_of(c * SC_ROWS, SC_ROWS)
            pltpu.sync_copy(x_hbm.at[pl.ds(r0, SC_ROWS)], xv)
            pltpu.sync_copy(d_hbm.at[:, pl.ds(r0, SC_ROWS)], dv)
            for k in range(TOP_K):
                pltpu.sync_copy(xv, o_hbm.at[dv.at[k]])

    return kern(x, dest)


def _sc_gather(y, idx):
    w = y.shape[1]
    n_chunks = idx.shape[0]

    @functools.partial(pl.kernel, out_type=jax.ShapeDtypeStruct((n_chunks * SC_ROWS, w), y.dtype), mesh=_sc_mesh(),
                       scratch_types=[pltpu.VMEM((SC_ROWS, w), y.dtype), pltpu.VMEM((1, SC_ROWS), jnp.int32)])
    def kern(y_hbm, i_hbm, o_hbm, ov, iv):
        sid = lax.axis_index("core") * (SC_SUBCORES // 2) + lax.axis_index("subcore")

        @pl.loop(sid, n_chunks, step=SC_SUBCORES)
        def _(c):
            pltpu.sync_copy(i_hbm.at[pl.ds(c, 1)], iv)
            pltpu.sync_copy(y_hbm.at[iv.at[0]], ov)
            pltpu.sync_copy(ov, o_hbm.at[pl.ds(pl.multiple_of(c * SC_ROWS, SC_ROWS), SC_ROWS)])

    return kern(y, idx)


def _moe_rows_kernel(be_ref, first_ref, slot_ref, nxt_ref, inb_ref, blka_ref, blkb_ref,
                     xa_ref, xb_ref, wgu_hbm, wdn_hbm, ya_ref, yb_ref, wgu_f, wdn_f, wgu_b, wdn_b, sem, *, layer):
    del blka_ref, blkb_ref
    b = pl.program_id(0)

    def weight_copies(expert, slot):
        return (pltpu.make_async_copy(wgu_hbm.at[layer, expert], wgu_f.at[slot], sem.at[slot, 0]),
                pltpu.make_async_copy(wdn_hbm.at[layer, expert], wdn_f.at[slot], sem.at[slot, 1]))

    @pl.when(b == 0)
    def _():
        for copy in weight_copies(be_ref[0], 0):
            copy.start()

    @pl.when(first_ref[b] == 1)
    def _():
        slot = slot_ref[b]
        for copy in weight_copies(be_ref[b], slot):
            copy.wait()

        @pl.when(nxt_ref[b] < N_EXPERTS)
        def _():
            for copy in weight_copies(nxt_ref[b], 1 - slot):
                copy.start()

        wgu_b[...] = wgu_f[slot].astype(BF16)
        wdn_b[...] = wdn_f[slot].astype(BF16)

    half = D_MODEL // 2
    second = inb_ref[b] == 1
    lo, hi = _unpack_halves(jnp.where(second, xb_ref[...], xa_ref[...]))
    gu = _mm(lo, wgu_b[0:half, :]) + _mm(hi, wgu_b[half:, :])
    y = _pack_halves(_mm(_swiglu_act(gu), wdn_b[...]))

    @pl.when(second)
    def _():
        yb_ref[...] = y

    @pl.when(jnp.logical_not(second))
    def _():
        ya_ref[...] = y


def _moe_rows(xs_a, xs_b, schedule, n_used, lw, layer, rows):
    half = D_MODEL // 2
    spec_a = pl.BlockSpec((rows, half), lambda b, be, first, slot, nxt, inb, blka, blkb: (blka[b], 0))
    spec_b = pl.BlockSpec((rows, half), lambda b, be, first, slot, nxt, inb, blka, blkb: (blkb[b], 0))
    return pl.pallas_call(
        functools.partial(_moe_rows_kernel, layer=layer),
        grid_spec=pltpu.PrefetchScalarGridSpec(
            num_scalar_prefetch=7,
            grid=(n_used,),
            in_specs=[spec_a, spec_b, pl.BlockSpec(memory_space=pl.ANY), pl.BlockSpec(memory_space=pl.ANY)],
            out_specs=[spec_a, spec_b],
            scratch_shapes=[
                pltpu.VMEM((2, D_MODEL, 2 * D_EXPERT), F32),
                pltpu.VMEM((2, D_EXPERT, D_MODEL), F32),
                pltpu.VMEM((D_MODEL, 2 * D_EXPERT), BF16),
                pltpu.VMEM((D_EXPERT, D_MODEL), BF16),
                pltpu.SemaphoreType.DMA((2, 2)),
            ],
        ),
        out_shape=[jax.ShapeDtypeStruct(xs_a.shape, jnp.int32), jax.ShapeDtypeStruct(xs_b.shape, jnp.int32)],
        compiler_params=pltpu.CompilerParams(dimension_semantics=("arbitrary",),
                                             vmem_limit_bytes=VMEM_LIMIT),
        name="moe_rows",
    )(*schedule, xs_a, xs_b, lw["moe_w_gu"], lw["moe_w_down"])


def _moe_combine_kernel(yg_ref, ew_ref, x1_ref, mod_ref, nf_ref, o_ref, *, final):
    ew = ew_ref[...]
    acc_lo = acc_hi = None
    for k in range(TOP_K):
        lo, hi = _unpack_halves(yg_ref[k])
        wk = ew[:, k:k + 1]
        acc_lo = wk * lo if acc_lo is None else acc_lo + wk * lo
        acc_hi = wk * hi if acc_hi is None else acc_hi + wk * hi
    g2 = mod_ref[0][:, 5 * D_MODEL:]
    x2 = x1_ref[...] + g2 * jnp.concatenate([acc_lo, acc_hi], axis=1)
    if final:
        x2 = _rms(x2, nf_ref[...])
    o_ref[...] = x2


def _moe_combine(yg, row0, ew, x1, mods, layer, norm_f, tm, seq_len, mod_base, final):
    n = x1.shape[0]
    half = D_MODEL // 2
    tiles_per_seq = seq_len // tm if mod_base else 1
    tile0 = row0 // tm

    def mod_idx(i):
        return (layer, mod_base + i // tiles_per_seq if mod_base else 0, 0, 0)

    row = lambda w: pl.BlockSpec((tm, w), lambda i: (i, 0))
    return pl.pallas_call(
        functools.partial(_moe_combine_kernel, final=final),
        grid=(n // tm,),
        in_specs=[pl.BlockSpec((TOP_K, tm, half), lambda i: (0, tile0 + i, 0)), row(LANES), row(D_MODEL),
                  pl.BlockSpec((None, 1, 1, 6 * D_MODEL), mod_idx), _const_spec(norm_f)],
        out_specs=row(D_MODEL),
        out_shape=jax.ShapeDtypeStruct((n, D_MODEL), F32),
        compiler_params=pltpu.CompilerParams(dimension_semantics=("arbitrary",),
                                             vmem_limit_bytes=VMEM_LIMIT),
        name="moe_combine_final" if final else "moe_combine",
    )(yg, ew, x1, mods, norm_f)


def _moe_dispatch(h2p, eid, rank, counts):
    n = h2p.shape[0]
    dest, blocks, first_block = _group_plan(eid, rank, counts, MOE_ROWS)
    n_blocks = n * TOP_K // MOE_ROWS + N_EXPERTS
    return _sc_dispatch(h2p, dest, n_blocks * MOE_ROWS), dest, (blocks, first_block), n_blocks


def _moe_gather(y, dest):
    n = dest.shape[1]
    return _sc_gather(y, dest.reshape(n * TOP_K // SC_ROWS, SC_ROWS)).reshape(TOP_K, n, D_MODEL // 2)


def _constants():
    idx = np.arange(MIX_W)
    same_head = (idx[:, None] // HEAD_DIM) == (idx[None, :] // HEAD_DIM)
    maskbd = jnp.asarray(same_head, BF16)
    eexp = np.zeros((LANES, 4 * MIX_W), np.float32)
    for blk in range(4):
        kind, d = divmod(blk, 2)
        for h in range(HEADS):
            src = kind * 2 * HEADS + d * HEADS + h
            eexp[src, blk * MIX_W + h * HEAD_DIM: blk * MIX_W + (h + 1) * HEAD_DIM] = 1.0
    tri = np.triu(np.ones((POST_TM, POST_TM), np.float32), 1)
    return {"maskbd": maskbd, "ones_bd": maskbd, "eexp": jnp.asarray(eexp, BF16), "tri_tokens": jnp.asarray(tri, BF16)}


def _rope_tables(n):
    rows = n // GRID_W
    row = jnp.repeat(jnp.arange(rows, dtype=F32), GRID_W)
    col = jnp.tile(jnp.arange(GRID_W, dtype=F32), rows)
    axis_dim = MLA_ROPE // 2
    inv = jnp.power(ROPE_BASE, -jnp.arange(0, axis_dim, 2, dtype=F32) / axis_dim)
    ang_r = row[:, None] * inv
    ang_c = col[:, None] * inv
    cr, sr, cc, sc = jnp.cos(ang_r), jnp.sin(ang_r), jnp.cos(ang_c), jnp.sin(ang_c)
    zeros = jnp.zeros((n, LANES - MLA_ROPE), F32)
    cos_t = jnp.concatenate([cr, cr, cc, cc, zeros], axis=1)
    sin_t = jnp.concatenate([-sr, sr, -sc, sc, zeros], axis=1)
    return cos_t, sin_t


def _stacked_weights(p):
    def per_direction(w):
        half = jnp.zeros((DEPTH, 64, MIX_W), F32)
        return jnp.stack([jnp.concatenate([w[:, 0], half], axis=1),
                          jnp.concatenate([half, w[:, 1]], axis=1)], axis=1).astype(BF16)

    row = lambda v: v.reshape(DEPTH, 1, -1)
    pad_row = lambda v: jnp.pad(row(v), ((0, 0), (0, 0), (0, LANES - 2 * HEADS)))
    return {
        "norm1": row(p["norm1"]),
        "w_in": p["w_in"],
        "q_norm": row(p["mla_q_norm"]),
        "w_uq": p["mla_w_uq"],
        "kv_norm": row(p["mla_kv_norm"]),
        "w_ukv": p["mla_w_ukv"].astype(BF16),
        "gdn_conv": p["gdn_conv"],
        "gdn_alog": pad_row(p["gdn_a_log"]),
        "gdn_dtb": pad_row(p["gdn_dt_bias"]),
        "gdn_norm": jnp.tile(row(p["gdn_norm"]), (1, 1, HEADS)),
        "rwkv_mu_prev": row(p["rwkv_mu_prev"]),
        "rwkv_mu_next": row(p["rwkv_mu_next"]),
        "rwkv_w0": p["rwkv_w0"],
        "rwkv_w2": per_direction(p["rwkv_w2"]),
        "rwkv_a0": p["rwkv_a0"],
        "rwkv_a2": per_direction(p["rwkv_a2"]),
        "rwkv_g2": p["rwkv_g2"].astype(BF16),
        "rwkv_k_k": row(p["rwkv_k_k"]),
        "rwkv_k_a": row(p["rwkv_k_a"]),
        "rwkv_r_k": row(p["rwkv_r_k"]),
        "rwkv_gn_w": row(p["rwkv_gn_w"]),
        "rwkv_gn_b": row(p["rwkv_gn_b"]),
        "w_out": p["w_out"].astype(BF16),
        "norm2": row(p["norm2"]),
        "router_t": jnp.swapaxes(p["moe_router"], 1, 2),
        "router_b": p["moe_bias"].reshape(DEPTH, N_EXPERTS, 1),
        "moe_w_gu": p["moe_w_gu"],
        "moe_w_down": p["moe_w_down"],
        "shared_w_gu": p["shared_w_gu"].astype(BF16),
        "shared_w_down": p["shared_w_down"].astype(BF16),
    }


def _embed_block_diag(s):
    b = s.shape[0]
    eye = jnp.eye(HEADS, dtype=s.dtype)
    out = jnp.einsum("bdhkv,hg->bdhkgv", s, eye)
    return out.reshape(b, 2, MIX_W, MIX_W)


def _layer_front(x2d, mods, lw, l, consts, seq_len, mod_base, cache, tm, tq, counts_in, prev=None):
    zm, zg, zab, zr = _inproj(x2d, mods, lw, l, tm, seq_len, mod_base)
    if cache is None:
        o_mla, ckv, kpe = _mla_ctx(zm, lw, l, seq_len, prev[:2])
        o_gdn, s_gdn = _gdn(zg, zab, lw, l, consts, seq_len, prev=prev[2])
        o_rwkv, s_rwkv = _rwkv(zr, lw, l, consts, seq_len, prev=prev[3])
        new = (ckv, kpe, s_gdn, s_rwkv)
    else:
        cckv, ckpe, rc, rs, sg, sr = cache
        o_mla = _mla_lat(zm, lw, l, seq_len, tq, (cckv, ckpe, rc, rs))
        o_gdn, _ = _gdn(zg, zab, lw, l, consts, seq_len, sg)
        o_rwkv, _ = _rwkv(zr, lw, l, consts, seq_len, sr)
        new = None
    routed = _post(x2d, o_mla, o_gdn, o_rwkv, mods, lw, l, consts, tm, seq_len, mod_base, counts_in)
    return routed, new


def kernel(x_prompt, x_sample, cache_mla_ckv, cache_mla_kpe, state_gdn, state_rwkv, c, c_ctx, ada_w, ada_b, norm1, w_in, mla_q_norm, mla_w_uq, mla_kv_norm, mla_w_ukv, gdn_conv, gdn_a_log, gdn_dt_bias, gdn_norm, rwkv_mu_prev, rwkv_mu_next, rwkv_w0, rwkv_w2, rwkv_a0, rwkv_a2, rwkv_g2, rwkv_k_k, rwkv_k_a, rwkv_r_k, rwkv_gn_w, rwkv_gn_b, w_out, norm2, moe_router, moe_bias, moe_w_gu, moe_w_down, shared_w_gu, shared_w_down, norm_f):
    p = dict(norm1=norm1, w_in=w_in, mla_q_norm=mla_q_norm, mla_w_uq=mla_w_uq, mla_kv_norm=mla_kv_norm,
             mla_w_ukv=mla_w_ukv, gdn_conv=gdn_conv, gdn_a_log=gdn_a_log, gdn_dt_bias=gdn_dt_bias,
             gdn_norm=gdn_norm, rwkv_mu_prev=rwkv_mu_prev, rwkv_mu_next=rwkv_mu_next, rwkv_w0=rwkv_w0,
             rwkv_w2=rwkv_w2, rwkv_a0=rwkv_a0, rwkv_a2=rwkv_a2, rwkv_g2=rwkv_g2, rwkv_k_k=rwkv_k_k,
             rwkv_k_a=rwkv_k_a, rwkv_r_k=rwkv_r_k, rwkv_gn_w=rwkv_gn_w, rwkv_gn_b=rwkv_gn_b, w_out=w_out,
             norm2=norm2, moe_router=moe_router, moe_bias=moe_bias, moe_w_gu=moe_w_gu, moe_w_down=moe_w_down,
             shared_w_gu=shared_w_gu, shared_w_down=shared_w_down)
    weights = _stacked_weights(p)
    consts = _constants()
    nf = norm_f.reshape(1, D_MODEL)
    b_ctx, t_ctx, _ = x_prompt.shape
    b_lat, t_lat, _ = x_sample.shape

    cvec8 = jnp.concatenate([c_ctx[None, :], c, jnp.zeros((8 - 1 - b_lat, D_MODEL), F32)], axis=0)
    mods = _adaln(cvec8, ada_w, ada_b)
    mods = mods.reshape(DEPTH, 8, 1, 6 * D_MODEL)

    rc, rs = _rope_tables(t_lat)
    ckpe = jnp.pad(cache_mla_kpe, ((0, 0), (0, 0), (0, 0), (0, LANES - MLA_ROPE)))
    cache = (cache_mla_ckv, ckpe, rc, rs, _embed_block_diag_layers(state_gdn),
             _embed_block_diag_layers(jnp.swapaxes(state_rwkv, -1, -2)))
    xp = x_prompt.reshape(b_ctx * t_ctx, D_MODEL)
    xs = x_sample.reshape(b_lat * t_lat, D_MODEL)
    n_ctx = xp.shape[0]
    tm = POST_TM
    state_shape = (b_ctx, DEPTH, 2, HEADS, HEAD_DIM, HEAD_DIM)
    ctx_outs = (jnp.zeros((b_ctx, DEPTH, t_ctx, KV_LORA), F32), jnp.zeros((b_ctx, DEPTH, t_ctx, MLA_ROPE), F32),
                jnp.zeros(state_shape, F32), jnp.zeros(state_shape, F32))
    for l in range(DEPTH):
        final = l == DEPTH - 1
        no_pairs = jnp.zeros((N_EXPERTS, LANES), F32)
        (x1c, hc, eidc, rankc, ewc, cnt_c), ctx_outs = _layer_front(xp, mods, weights, l, consts, t_ctx, 0, None,
                                                                    tm, t_ctx, no_pairs, ctx_outs)
        xs_c, dest_c, plan_c, nb_c = _moe_dispatch(hc, eidc, rankc, cnt_c)
        (x1s, hs, eids, ranks, ews, cnt_s), _ = _layer_front(xs, mods, weights, l, consts, t_lat, 1, cache,
                                                             tm, 256, no_pairs)
        xs_s, dest_s, plan_s, nb_s = _moe_dispatch(hs, eids, ranks, cnt_s)
        schedule, n_used = _block_schedule((plan_c, plan_s), nb_c + nb_s)
        y_c, y_s = _moe_rows(xs_c, xs_s, schedule, n_used, weights, l, MOE_ROWS)
        xp = _moe_combine(_moe_gather(y_c, dest_c), 0, ewc, x1c, mods, l, nf, tm, t_ctx, 0, final)
        xs = _moe_combine(_moe_gather(y_s, dest_s), 0, ews, x1s, mods, l, nf, tm, t_lat, 1, final)

    y_prompt = xp.reshape(b_ctx, t_ctx, D_MODEL)
    y_sample = xs.reshape(b_lat, t_lat, D_MODEL)
    new_ckv, new_kpe, new_gdn, new_rwkv = ctx_outs
    return (y_prompt, y_sample, new_ckv, new_kpe, new_gdn, new_rwkv)


def _embed_block_diag_layers(s):
    b = s.shape[0]
    return _embed_block_diag(s.reshape(b * DEPTH, 2, HEADS, HEAD_DIM, HEAD_DIM)).reshape(
        b, DEPTH, 2, MIX_W, MIX_W)
```

```python
import functools

import numpy as np
import jax
import jax.numpy as jnp
from jax import lax
from jax.experimental import pallas as pl
from jax.experimental.pallas import tpu as pltpu
from jax.experimental.pallas import tpu_sc as plsc

F32 = jnp.float32
BF16 = jnp.bfloat16

D_MODEL = 1024
DEPTH = 2
PAST_LEN = 512
GRID_W = 64
NORM_EPS = 1e-6

MLA_HEADS = 4
MLA_NOPE = 128
MLA_ROPE = 64
MLA_V = 128
Q_LORA = 384
KV_LORA = 256
ROPE_BASE = 10000.0
MLA_SCALE = (MLA_NOPE + MLA_ROPE) ** -0.5

HEADS = 4
HEAD_DIM = 64
MIX_W = HEADS * HEAD_DIM
GDN_CONV_CH = 3 * MIX_W
CHUNK = 64
RWKV_GN_EPS = 64e-5

N_EXPERTS = 64
TOP_K = 8
N_GROUPS = 8
GROUP_SIZE = N_EXPERTS // N_GROUPS
TOPK_GROUPS = 4
D_EXPERT = 256
ROUTE_SCALE = 2.5

P_MLA = Q_LORA + KV_LORA + MLA_ROPE
P_GDN = GDN_CONV_CH + MIX_W + 4 * HEADS
P_RWKV = 3 * MIX_W + 128 + 128 + 128

LANES = 128
ZM_W = Q_LORA + KV_LORA + 2 * LANES
ZG_W = GDN_CONV_CH + MIX_W
ZR_W = P_RWKV
QH_W = 2 * LANES
VMEM_LIMIT = 56 * 1024 * 1024
POST_TM = 512

_ROPE_SWAP = np.concatenate([np.arange(16, 32), np.arange(0, 16), np.arange(48, 64), np.arange(32, 48)])


def _sigmoid(x):
    return 1.0 / (1.0 + jnp.exp(-x))


def _silu(x):
    return x * _sigmoid(x)


def _softplus(x):
    return jnp.maximum(x, 0.0) + jnp.log(1.0 + jnp.exp(-jnp.abs(x)))


def _rms(x, g, eps=NORM_EPS):
    return x * lax.rsqrt(jnp.mean(x * x, axis=-1, keepdims=True) + eps) * g


def _mm(a, b):
    return jnp.dot(a.astype(BF16), b.astype(BF16), preferred_element_type=F32)


def _mm_nt(a, b):
    return lax.dot_general(a.astype(BF16), b.astype(BF16), (((1,), (1,)), ((), ())),
                           preferred_element_type=F32)


def _mm_tn(a, b):
    return lax.dot_general(a.astype(BF16), b.astype(BF16), (((0,), (0,)), ((), ())),
                           preferred_element_type=F32)


def _split3(x):
    p1 = x.astype(BF16)
    r1 = x - p1.astype(F32)
    p2 = r1.astype(BF16)
    r2 = r1 - p2.astype(F32)
    return p1, p2, r2.astype(BF16)


def _mm_sel_l(sel, x):
    p1, p2, p3 = _split3(x)
    return _mm(sel, p1) + _mm(sel, p2) + _mm(sel, p3)


def _mm_sel_r(x, sel):
    p1, p2, p3 = _split3(x)
    return _mm(p1, sel) + _mm(p2, sel) + _mm(p3, sel)


def _iota(shape, dim):
    return lax.broadcasted_iota(jnp.int32, shape, dim)


def _layer_spec(a, layer, **kw):
    nd = a.ndim - 1
    return pl.BlockSpec((None,) + a.shape[1:], lambda *_: (layer,) + (0,) * nd, **kw)


def _const_spec(a, **kw):
    return pl.BlockSpec(a.shape, lambda *_: (0,) * a.ndim, **kw)


def _bd(x, maskbd):
    xb = x.astype(BF16)
    return jnp.concatenate([xb] * HEADS, axis=0) * maskbd


def _chunk_masks(rev):
    row = _iota((CHUNK, MIX_W), 0)
    col = jnp.bitwise_and(_iota((CHUNK, MIX_W), 1), HEAD_DIM - 1)
    r2 = _iota((CHUNK, CHUNK), 0)
    c2 = _iota((CHUNK, CHUNK), 1)
    if rev:
        inc, strict, tri = row <= col, row < col, r2 <= c2
    else:
        inc, strict, tri = row >= col, row > col, r2 >= c2
    eye = jnp.where(row == col, 1.0, 0.0).astype(F32)
    return inc, strict, jnp.where(tri, 1.0, 0.0).astype(BF16), eye


def _split2(x):
    hi = x.astype(BF16)
    return hi, (x - hi.astype(F32)).astype(BF16)


def _mm_bd3(x, p, maskbd):
    n = x.shape[0]
    xh, xl = _split2(x)
    ph, pl_ = _split2(p)
    r = jnp.dot(jnp.concatenate([xh, xl], axis=0), _bd(ph, maskbd), preferred_element_type=F32)
    return r[:n] + r[n:] + jnp.dot(xh, _bd(pl_, maskbd), preferred_element_type=F32)


def _neumann_inverse(a_list, eye_list, maskbd):
    bs = [-a for a in a_list]
    ms = [eye + b for eye, b in zip(eye_list, bs)]
    ps = [_mm_bd3(b, b, maskbd) for b in bs]
    for _ in range(4):
        boths = [_mm_bd3(jnp.concatenate([m, p], axis=0), p, maskbd) for m, p in zip(ms, ps)]
        ms = [m + both[:CHUNK] for m, both in zip(ms, boths)]
        ps = [both[CHUNK:] for both in boths]
    return [m + _mm_bd3(m, p, maskbd) for m, p in zip(ms, ps)]


def _adaln_kernel(c_ref, w_ref, b_ref, o_ref):
    cv = c_ref[...]
    o_ref[0] = _mm(_silu(cv), w_ref[0]) + b_ref[0]


def _adaln(cvec8, ada_w, ada_b):
    tn = 768
    n_out = 6 * D_MODEL
    return pl.pallas_call(
        _adaln_kernel,
        grid=(DEPTH, n_out // tn),
        in_specs=[
            pl.BlockSpec((8, D_MODEL), lambda l, j: (0, 0)),
            pl.BlockSpec((1, D_MODEL, tn), lambda l, j: (l, 0, j)),
            pl.BlockSpec((1, 1, tn), lambda l, j: (l, 0, j)),
        ],
        out_specs=pl.BlockSpec((1, 8, tn), lambda l, j: (l, 0, j)),
        out_shape=jax.ShapeDtypeStruct((DEPTH, 8, n_out), F32),
        compiler_params=pltpu.CompilerParams(dimension_semantics=("arbitrary", "arbitrary"),
                                             vmem_limit_bytes=VMEM_LIMIT),
        name="adaln",
    )(cvec8, ada_w, ada_b.reshape(DEPTH, 1, n_out))


_KPE0 = Q_LORA + KV_LORA
_W_IN_MOVES = (
    [(0, 0, P_MLA)]
    + [(P_MLA + LANES - MLA_ROPE + 16 * j, _KPE0 + 16 * int(_ROPE_SWAP[16 * j] // 16), 16) for j in range(4)]
    + [(ZM_W, P_MLA, ZG_W), (ZM_W + ZG_W, P_MLA + ZG_W, 4 * HEADS), (ZM_W + ZG_W + LANES, P_MLA + P_GDN, P_RWKV)]
)
W_IN_PAD = ZM_W + ZG_W + LANES + ZR_W


def _inproj_kernel(x_ref, mod_ref, n1_ref, w_ref, zm_ref, zg_ref, zab_ref, zr_ref, w_s):
    @pl.when(pl.program_id(0) == 0)
    def _():
        w_s[...] = jnp.zeros(w_s.shape, BF16)
        for dst, src, width in _W_IN_MOVES:
            w_s[:, dst:dst + width] = w_ref[:, src:src + width].astype(BF16)

    m = mod_ref[0]
    sh = m[:, 0:D_MODEL]
    sc = m[:, D_MODEL:2 * D_MODEL]
    h = _rms(x_ref[...], n1_ref[...]) * (1.0 + sc) + sh
    z = _mm(h, w_s[...])
    o1 = ZM_W
    o2 = o1 + ZG_W
    o3 = o2 + LANES
    zm_ref[...] = z[:, :o1]
    zg_ref[...] = z[:, o1:o2]
    zab_ref[...] = z[:, o2:o3]
    zr_ref[...] = z[:, o3:]


def _inproj(x2d, mods, lw, layer, tm, seq_len, mod_base):
    n = x2d.shape[0]
    tiles_per_seq = seq_len // tm if mod_base else 1

    def mod_idx(i):
        return (layer, mod_base + i // tiles_per_seq if mod_base else 0, 0, 0)

    return pl.pallas_call(
        _inproj_kernel,
        grid=(n // tm,),
        in_specs=[
            pl.BlockSpec((tm, D_MODEL), lambda i: (i, 0)),
            pl.BlockSpec((None, 1, 1, 6 * D_MODEL), mod_idx),
            _layer_spec(lw["norm1"], layer),
            _layer_spec(lw["w_in"], layer, pipeline_mode=pl.Buffered(1)),
        ],
        out_specs=[
            pl.BlockSpec((tm, ZM_W), lambda i: (i, 0)),
            pl.BlockSpec((tm, ZG_W), lambda i: (i, 0)),
            pl.BlockSpec((tm, LANES), lambda i: (i, 0)),
            pl.BlockSpec((tm, ZR_W), lambda i: (i, 0)),
        ],
        out_shape=[
            jax.ShapeDtypeStruct((n, ZM_W), F32),
            jax.ShapeDtypeStruct((n, ZG_W), F32),
            jax.ShapeDtypeStruct((n, LANES), F32),
            jax.ShapeDtypeStruct((n, ZR_W), F32),
        ],
        scratch_shapes=[pltpu.VMEM((D_MODEL, W_IN_PAD), BF16)],
        compiler_params=pltpu.CompilerParams(dimension_semantics=("arbitrary",),
                                             vmem_limit_bytes=VMEM_LIMIT),
        name="inproj",
    )(x2d, mods, lw["norm1"], lw["w_in"])


MLA_SEQ_PER_STEP = 4
Q_HEAD_W = MLA_NOPE + MLA_ROPE


def _arrange_w_uq(w_ref, wq_s, wqs_s=None):
    wq_s[...] = jnp.zeros(wq_s.shape, BF16)
    if wqs_s is not None:
        wqs_s[...] = jnp.zeros(wqs_s.shape, BF16)
    for h in range(MLA_HEADS):
        src, dst = h * Q_HEAD_W, h * QH_W
        wq_s[:, dst:dst + Q_HEAD_W] = w_ref[:, src:src + Q_HEAD_W].astype(BF16)
        if wqs_s is not None:
            for j in range(4):
                s0 = src + MLA_NOPE + 16 * int(_ROPE_SWAP[16 * j] // 16)
                wqs_s[:, dst + MLA_NOPE + 16 * j:dst + MLA_NOPE + 16 * (j + 1)] = w_ref[:, s0:s0 + 16].astype(BF16)


def _mla_ctx_kernel(zm_ref, qn_ref, wuq_ref, kvn_ref, wukv_ref, prev_ckv, prev_kpe, o_ref, ckv_ref, kpe_ref, wq_s,
                    *, seq_len, n_seq):
    del prev_ckv, prev_kpe

    @pl.when(pl.program_id(0) == 0)
    def _():
        _arrange_w_uq(wuq_ref, wq_s)

    o_kpe = Q_LORA + KV_LORA
    zm = zm_ref[...]
    ckv = _rms(zm[:, Q_LORA:o_kpe], kvn_ref[...])
    kpe = zm[:, o_kpe:o_kpe + LANES]
    kv = _mm(ckv, wukv_ref[...])
    q = _mm(_rms(zm[:, :Q_LORA], qn_ref[...]), wq_s[...])
    kpe_b = kpe.astype(BF16)
    seqs = [slice(i * seq_len, (i + 1) * seq_len) for i in range(n_seq)]
    for i, r in enumerate(seqs):
        ckv_ref[i] = ckv[r]
        kpe_ref[i] = kpe[r, :MLA_ROPE]
    pairs = [(r, h * QH_W) for r in seqs for h in range(MLA_HEADS)]
    scores = [_mm_nt(q[r, c0:c0 + QH_W], jnp.concatenate([kv[r, c0:c0 + LANES].astype(BF16), kpe_b[r]], axis=1))
              * MLA_SCALE for r, c0 in pairs]
    exps = [jnp.exp(s - jnp.max(s, axis=-1, keepdims=True)) for s in scores]
    outs = [_mm(e, kv[r, c0 + LANES:c0 + QH_W]) / jnp.sum(e, axis=-1, keepdims=True)
            for e, (r, c0) in zip(exps, pairs)]
    for o, (r, c0) in zip(outs, pairs):
        h = c0 // QH_W
        o_ref[r, h * MLA_V:(h + 1) * MLA_V] = o


def _mla_ctx(zm, lw, layer, seq_len, prev):
    n = zm.shape[0]
    nb = n // seq_len
    n_seq = MLA_SEQ_PER_STEP
    rows = n_seq * seq_len
    weights = [lw["q_norm"], lw["w_uq"], lw["kv_norm"], lw["w_ukv"]]
    return pl.pallas_call(
        functools.partial(_mla_ctx_kernel, seq_len=seq_len, n_seq=n_seq),
        grid=(nb // n_seq,),
        in_specs=[pl.BlockSpec((rows, ZM_W), lambda b: (b, 0))] + [_layer_spec(a, layer) for a in weights]
        + [pl.BlockSpec(memory_space=pl.ANY)] * 2,
        out_specs=[
            pl.BlockSpec((rows, MLA_HEADS * MLA_V), lambda b: (b, 0)),
            pl.BlockSpec((n_seq, None, seq_len, KV_LORA), lambda b: (b, layer, 0, 0)),
            pl.BlockSpec((n_seq, None, seq_len, MLA_ROPE), lambda b: (b, layer, 0, 0)),
        ],
        out_shape=[
            jax.ShapeDtypeStruct((n, MLA_HEADS * MLA_V), F32),
            jax.ShapeDtypeStruct((nb, DEPTH, seq_len, KV_LORA), F32),
            jax.ShapeDtypeStruct((nb, DEPTH, seq_len, MLA_ROPE), F32),
        ],
        input_output_aliases={5: 1, 6: 2},
        scratch_shapes=[pltpu.VMEM((Q_LORA, MLA_HEADS * QH_W), BF16)],
        compiler_params=pltpu.CompilerParams(dimension_semantics=("arbitrary",),
                                             vmem_limit_bytes=VMEM_LIMIT),
        name="mla_ctx",
    )(zm, *weights, *prev)


def _mla_lat_kernel(zm_ref, cckv_ref, ckpe_ref, rc_ref, rs_ref, qn_ref, wuq_ref, kvn_ref, wukv_ref,
                    o_ref, k_s, v_s, wq_s, wqs_s, *, seq_len, tq, past):
    qi = pl.program_id(1)
    o_kpe = Q_LORA + KV_LORA

    @pl.when((pl.program_id(0) == 0) & (qi == 0))
    def _():
        _arrange_w_uq(wuq_ref, wq_s, wqs_s)

    @pl.when(qi == 0)
    def _():
        zm = zm_ref[...]
        ckv = _rms(zm[:, Q_LORA:o_kpe], kvn_ref[...])
        kpe = zm[:, o_kpe:o_kpe + LANES] * rc_ref[...] + zm[:, o_kpe + LANES:o_kpe + 2 * LANES] * rs_ref[...]
        kvc = _mm(cckv_ref[0], wukv_ref[...])
        kpc = ckpe_ref[0].astype(BF16)
        kv = _mm(ckv, wukv_ref[...])
        kpe = kpe.astype(BF16)
        for h in range(MLA_HEADS):
            c0 = h * QH_W
            k_s[0:past, c0:c0 + LANES] = kvc[:, c0:c0 + LANES].astype(BF16)
            k_s[0:past, c0 + LANES:c0 + QH_W] = kpc
            v_s[0:past, h * MLA_V:(h + 1) * MLA_V] = kvc[:, c0 + LANES:c0 + QH_W].astype(BF16)
            k_s[past:past + seq_len, c0:c0 + LANES] = kv[:, c0:c0 + LANES].astype(BF16)
            k_s[past:past + seq_len, c0 + LANES:c0 + QH_W] = kpe
            v_s[past:past + seq_len, h * MLA_V:(h + 1) * MLA_V] = kv[:, c0 + LANES:c0 + QH_W].astype(BF16)

    r0 = pl.multiple_of(qi * tq, tq)
    zq = zm_ref[pl.ds(r0, tq), :]
    cq = _rms(zq[:, :Q_LORA], qn_ref[...])
    q = _mm(cq, wq_s[...])
    qs = _mm(cq, wqs_s[...])
    qc = jnp.concatenate([jnp.ones((tq, LANES), F32), rc_ref[pl.ds(r0, tq), :]], axis=1)
    qsn = jnp.concatenate([jnp.zeros((tq, LANES), F32), rs_ref[pl.ds(r0, tq), :]], axis=1)
    for h in range(MLA_HEADS):
        c0 = h * QH_W
        qh = q[:, c0:c0 + QH_W] * qc + qs[:, c0:c0 + QH_W] * qsn
        s = _mm_nt(qh, k_s[:, c0:c0 + QH_W]) * MLA_SCALE
        e = jnp.exp(s - jnp.max(s, axis=-1, keepdims=True))
        den = jnp.sum(e, axis=-1, keepdims=True)
        o_ref[:, h * MLA_V:(h + 1) * MLA_V] = _mm(e, v_s[:, h * MLA_V:(h + 1) * MLA_V]) / den


def _mla_lat(zm, lw, layer, seq_len, tq, cache):
    n = zm.shape[0]
    nb = n // seq_len
    past = PAST_LEN
    tk = past + seq_len
    cckv, ckpe, rc, rs = cache
    weights = [lw["q_norm"], lw["w_uq"], lw["kv_norm"], lw["w_ukv"]]
    return pl.pallas_call(
        functools.partial(_mla_lat_kernel, seq_len=seq_len, tq=tq, past=past),
        grid=(nb, seq_len // tq),
        in_specs=[pl.BlockSpec((seq_len, ZM_W), lambda b, q: (b, 0)),
                  pl.BlockSpec((1, None, past, KV_LORA), lambda b, q: (b, layer, 0, 0)),
                  pl.BlockSpec((1, None, past, LANES), lambda b, q: (b, layer, 0, 0)),
                  _const_spec(rc), _const_spec(rs)] + [_layer_spec(a, layer) for a in weights],
        out_specs=pl.BlockSpec((tq, MLA_HEADS * MLA_V), lambda b, q: (b * (seq_len // tq) + q, 0)),
        out_shape=jax.ShapeDtypeStruct((n, MLA_HEADS * MLA_V), F32),
        scratch_shapes=[
            pltpu.VMEM((tk, MLA_HEADS * QH_W), BF16),
            pltpu.VMEM((tk, MLA_HEADS * MLA_V), BF16),
            pltpu.VMEM((Q_LORA, MLA_HEADS * QH_W), BF16),
            pltpu.VMEM((Q_LORA, MLA_HEADS * QH_W), BF16),
        ],
        compiler_params=pltpu.CompilerParams(dimension_semantics=("arbitrary", "arbitrary"),
                                             vmem_limit_bytes=VMEM_LIMIT),
        name="mla_lat",
    )(zm, cckv, ckpe, rc, rs, *weights)


SEQ_PER_STEP = 4
CHUNK_GROUP = 4
GDN_PRE_W = 5 * MIX_W


def _for_chunk_groups(n_chunks, fn):
    if n_chunks == CHUNK_GROUP:
        fn(0)
    else:
        def body(gi, carry):
            fn(gi * CHUNK_GROUP)
            return carry
        lax.fori_loop(0, n_chunks // CHUNK_GROUP, body, 0)


def _gdn_prepare(items, maskbd):
    n = range(len(items))
    qs, ks, vs, gs, betas, masks, revs = zip(*items)
    gcs = [_mm_sel_l(masks[i][2], gs[i]) for i in n]
    decays = []
    for i in n:
        inc, eye = masks[i][0], masks[i][3]
        gc_row = jnp.sum(eye * gcs[i], axis=0, keepdims=True)
        decays.append(jnp.where(inc, jnp.exp(jnp.where(inc, gcs[i] - gc_row, 0.0)), 0.0))
    kbs = [ks[i] * betas[i] for i in n]
    aqs = [_mm_nt(jnp.concatenate([kbs[i], qs[i]], axis=0), _bd(ks[i], maskbd)) for i in n]
    a_mats = [jnp.where(masks[i][1], aqs[i][:CHUNK] * decays[i], 0.0) for i in n]
    t_invs = _neumann_inverse(a_mats, [m[3] for m in masks], maskbd)
    egcs = [jnp.exp(gc) for gc in gcs]
    uws = [_mm(t_invs[i], jnp.concatenate([_bd(vs[i] * betas[i], maskbd), _bd(kbs[i] * egcs[i], maskbd)], axis=1))
           for i in n]
    out = []
    for i in n:
        g_last = gcs[i][0:1] if revs[i] else gcs[i][CHUNK - 1:CHUNK]
        pre = jnp.concatenate([uws[i], qs[i] * egcs[i], aqs[i][CHUNK:] * decays[i],
                               ks[i] * jnp.exp(g_last - gcs[i])], axis=1)
        out.append((pre, jnp.broadcast_to(jnp.exp(g_last), (8, MIX_W))))
    return out


def _gdn_step(items, maskbd):
    w = MIX_W
    n = range(len(items))
    pres, egls, states = zip(*items)
    wqs = [_mm(jnp.concatenate([pres[i][:, w:2 * w], pres[i][:, 2 * w:3 * w]], axis=0), states[i]) for i in n]
    v_news = [pres[i][:, :w] - wqs[i][:CHUNK] for i in n]
    outs = [wqs[i][CHUNK:] + _mm(pres[i][:, 3 * w:4 * w], _bd(v_news[i], maskbd)) for i in n]
    upds = [_mm_tn(pres[i][:, 4 * w:], v_news[i]) for i in n]
    mask_f = maskbd.astype(F32)
    return [(outs[i], states[i] * egls[i] + upds[i] * mask_f) for i in n]


def _gdn_kernel(*refs, seq_len, n_seq, cached):
    if cached:
        (zg_ref, zab_ref, s0_ref, conv_ref, alog_ref, dtb_ref, gn_ref, eexp_ref, ones_ref, maskbd_ref,
         o_ref, sout_ref, q_s, k_s, v_s, ge_s, pre_s, gl_s, oacc_s, st_s) = refs
    else:
        zg_ref, zab_ref, conv_ref, alog_ref, dtb_ref, gn_ref, eexp_ref, ones_ref, maskbd_ref = refs[:9]
        o_ref, sout_ref, q_s, k_s, v_s, ge_s, pre_s, gl_s, oacc_s, st_s = refs[-10:]
    t = seq_len * n_seq
    z = zg_ref[:, :GDN_CONV_CH]
    rowi = jnp.bitwise_and(_iota((t, 1), 0), seq_len - 1)
    zp = jnp.where(rowi == 0, 0.0, pltpu.roll(z, 1, 0))
    zn = jnp.where(rowi == seq_len - 1, 0.0, pltpu.roll(z, t - 1, 0))
    cw = conv_ref[...]
    qkv = _silu(zp * cw[0:1] + z * cw[1:2] + zn * cw[2:3])
    ones_bd = ones_ref[...]
    q = qkv[:, :MIX_W]
    k = qkv[:, MIX_W:2 * MIX_W]
    q_s[...] = q * lax.rsqrt(_mm_sel_r(q * q, ones_bd) + 1e-6) * (HEAD_DIM ** -0.5)
    k_s[...] = k * lax.rsqrt(_mm_sel_r(k * k, ones_bd) + 1e-6)
    v_s[...] = qkv[:, 2 * MIX_W:]
    ab = zab_ref[...]
    lane = _iota((t, LANES), 1)
    gb = jnp.where(lane < 2 * HEADS, -jnp.exp(alog_ref[...]) * _softplus(ab + dtb_ref[...]), _sigmoid(ab))
    ge_s[...] = _mm_sel_r(gb, eexp_ref[...])
    oacc_s[...] = jnp.zeros((t, MIX_W), F32)
    if cached:
        st_s[...] = s0_ref[...]
    else:
        st_s[...] = jnp.zeros((n_seq, 2, MIX_W, MIX_W), F32)
    maskbd = maskbd_ref[...]
    masks = (_chunk_masks(False), _chunk_masks(True))
    n_chunks = seq_len // CHUNK

    def prepare_group(c0):
        where, items = [], []
        for j in range(CHUNK_GROUP):
            c = c0 + j
            rows = pl.ds(pl.multiple_of(c * CHUNK, CHUNK), CHUNK)
            for d in range(2):
                where.append((d, c, rows))
                items.append((q_s[rows, :], k_s[rows, :], v_s[rows, :], ge_s[rows, d * MIX_W:(d + 1) * MIX_W],
                              ge_s[rows, (2 + d) * MIX_W:(3 + d) * MIX_W], masks[d], d == 1))
        for (d, c, rows), (pre, egl) in zip(where, _gdn_prepare(items, maskbd)):
            pre_s[d, rows, :] = pre
            gl_s[d, pl.ds(pl.multiple_of(c * 8, 8), 8), :] = egl

    _for_chunk_groups(n_chunks * n_seq, prepare_group)

    def body(i, carry):
        where, items = [], []
        for q in range(n_seq):
            for d in range(2):
                c = q * n_chunks + (i if d == 0 else n_chunks - 1 - i)
                rows = pl.ds(pl.multiple_of(c * CHUNK, CHUNK), CHUNK)
                where.append((q, d, rows))
                items.append((pre_s[d, rows, :], gl_s[d, pl.ds(pl.multiple_of(c * 8, 8), 8), :][0:1], st_s[q, d]))
        for (q, d, rows), (o, s_new) in zip(where, _gdn_step(items, maskbd)):
            oacc_s[rows, :] = oacc_s[rows, :] + o
            st_s[q, d] = s_new
        return carry

    lax.fori_loop(0, n_chunks, body, 0)
    o = oacc_s[...]
    ms = _mm_sel_r(o * o, ones_bd) * (1.0 / HEAD_DIM)
    gate = zg_ref[:, GDN_CONV_CH:]
    o_ref[...] = o * lax.rsqrt(ms + NORM_EPS) * gn_ref[...] * _silu(gate)
    for q in range(n_seq):
        for d in range(2):
            for h in range(HEADS):
                sl = slice(h * HEAD_DIM, (h + 1) * HEAD_DIM)
                sout_ref[q, d, h] = st_s[q, d, sl, sl]


def _state_out(nb, n_seq, layer, cached, prev, args, in_specs):
    if cached:
        return (pl.BlockSpec((n_seq, 2, HEADS, HEAD_DIM, HEAD_DIM), lambda b: (b, 0, 0, 0, 0)),
                jax.ShapeDtypeStruct((nb, 2, HEADS, HEAD_DIM, HEAD_DIM), F32), {})
    aliases = {len(args): 1}
    args.append(prev)
    in_specs.append(pl.BlockSpec(memory_space=pl.ANY))
    return (pl.BlockSpec((n_seq, None, 2, HEADS, HEAD_DIM, HEAD_DIM), lambda b: (b, layer, 0, 0, 0, 0)),
            jax.ShapeDtypeStruct((nb, DEPTH, 2, HEADS, HEAD_DIM, HEAD_DIM), F32), aliases)


def _gdn(zg, zab, lw, layer, consts, seq_len, s0_bd=None, prev=None):
    n = zg.shape[0]
    nb = n // seq_len
    cached = s0_bd is not None
    n_seq = 1 if cached else SEQ_PER_STEP
    assert seq_len & (seq_len - 1) == 0 and seq_len % (CHUNK * CHUNK_GROUP) == 0 and nb % n_seq == 0
    rows = seq_len * n_seq
    args = [zg, zab]
    in_specs = [pl.BlockSpec((rows, ZG_W), lambda b: (b, 0)),
                pl.BlockSpec((rows, LANES), lambda b: (b, 0))]
    if cached:
        args.append(s0_bd)
        in_specs.append(pl.BlockSpec((1, None, 2, MIX_W, MIX_W), lambda b: (b, layer, 0, 0, 0)))
    layered = [lw["gdn_conv"], lw["gdn_alog"], lw["gdn_dtb"], lw["gdn_norm"]]
    const = [consts["eexp"], consts["ones_bd"], consts["maskbd"]]
    args += layered + const
    in_specs += [_layer_spec(a, layer) for a in layered] + [_const_spec(a) for a in const]
    s_spec, s_shape, aliases = _state_out(nb, n_seq, layer, cached, prev, args, in_specs)
    return pl.pallas_call(
        functools.partial(_gdn_kernel, seq_len=seq_len, n_seq=n_seq, cached=cached),
        grid=(nb // n_seq,),
        in_specs=in_specs,
        out_specs=[
            pl.BlockSpec((rows, MIX_W), lambda b: (b, 0)),
            s_spec,
        ],
        out_shape=[jax.ShapeDtypeStruct((n, MIX_W), F32), s_shape],
        input_output_aliases=aliases,
        scratch_shapes=[
            pltpu.VMEM((rows, MIX_W), F32),
            pltpu.VMEM((rows, MIX_W), F32),
            pltpu.VMEM((rows, MIX_W), F32),
            pltpu.VMEM((rows, 4 * MIX_W), F32),
            pltpu.VMEM((2, rows, GDN_PRE_W), F32),
            pltpu.VMEM((2, rows // CHUNK * 8, MIX_W), F32),
            pltpu.VMEM((rows, MIX_W), F32),
            pltpu.VMEM((n_seq, 2, MIX_W, MIX_W), F32),
        ],
        compiler_params=pltpu.CompilerParams(dimension_semantics=("arbitrary",),
                                             vmem_limit_bytes=VMEM_LIMIT),
        name="gdn_lat" if cached else "gdn_ctx",
    )(*args)


RWKV_PRE_W = 7 * MIX_W


def _rwkv_prepare(items, maskbd, eye_full):
    n = range(len(items))
    rs, kds, vs, kks, bs, lws, masks, revs = zip(*items)
    cums = [_mm_sel_l(masks[i][2], lws[i]) for i in n]
    einvs = [jnp.exp(-c) for c in cums]
    kts = [kks[i] * jnp.exp(cums[i] - lws[i]) for i in n]
    rts = [rs[i] * jnp.exp(cums[i]) for i in n]
    krs = [jnp.concatenate([kts[i], rts[i]], axis=0) for i in n]
    lb_alls = [_mm_nt(krs[i], _bd(bs[i] * einvs[i], maskbd)) for i in n]
    lk_alls = [_mm_nt(krs[i], _bd(kds[i] * einvs[i], maskbd)) for i in n]
    lbs = [jnp.where(masks[i][1], lb_alls[i][:CHUNK], 0.0) for i in n]
    t_invs = _neumann_inverse(lbs, [m[3] for m in masks], maskbd)
    lvs = [_mm(jnp.concatenate([jnp.where(masks[i][1], lk_alls[i][:CHUNK], 0.0),
                                jnp.where(masks[i][0], lk_alls[i][CHUNK:], 0.0)], axis=0), _bd(vs[i], maskbd))
           for i in n]
    tkps = [_mm(t_invs[i], jnp.concatenate([_bd(kts[i], maskbd), _bd(lvs[i][:CHUNK], maskbd)], axis=1)) for i in n]
    out = []
    for i in n:
        c_last = cums[i][0:1] if revs[i] else cums[i][CHUNK - 1:CHUNK]
        tail = jnp.exp(c_last - cums[i])
        rb = jnp.where(masks[i][0], lb_alls[i][CHUNK:], 0.0)
        pre = jnp.concatenate([tkps[i][:, :MIX_W], rts[i], tkps[i][:, MIX_W:], lvs[i][CHUNK:], rb,
                               kds[i] * tail, bs[i] * tail], axis=1)
        gcol = jnp.sum(eye_full * jnp.exp(c_last), axis=1, keepdims=True)
        out.append((pre, jnp.broadcast_to(gcol, (MIX_W, LANES))))
    return out


def _rwkv_step(items, maskbd):
    w = MIX_W
    n = range(len(items))
    pres, vs, gcols, states = zip(*items)
    prs = [_mm(jnp.concatenate([pres[i][:, :w], pres[i][:, w:2 * w]], axis=0), states[i]) for i in n]
    ps = [prs[i][:CHUNK] + pres[i][:, 2 * w:3 * w] for i in n]
    outs = [prs[i][CHUNK:] + pres[i][:, 3 * w:4 * w] - _mm(pres[i][:, 4 * w:5 * w], _bd(ps[i], maskbd)) for i in n]
    upds = [_mm_tn(jnp.concatenate([pres[i][:, 5 * w:6 * w], pres[i][:, 6 * w:]], axis=0),
                   jnp.concatenate([vs[i], -ps[i]], axis=0)) for i in n]
    mask_f = maskbd.astype(F32)
    return [(outs[i], states[i] * jnp.concatenate([gcols[i], gcols[i]], axis=1) + upds[i] * mask_f) for i in n]


def _rwkv_kernel(*refs, seq_len, n_seq, cached):
    if cached:
        (zr_ref, s0_ref, mup_ref, mun_ref, w0_ref, w2_ref, a0_ref, a2_ref, g2_ref, kk_ref, ka_ref, rk_ref,
         gnw_ref, gnb_ref, ones_ref, maskbd_ref, o_ref, sout_ref,
         r_s, v_s, kk_s, dir_s, bg_s, pre_s, gcol_s, oacc_s, st_s) = refs
    else:
        (zr_ref, mup_ref, mun_ref, w0_ref, w2_ref, a0_ref, a2_ref, g2_ref, kk_ref, ka_ref, rk_ref,
         gnw_ref, gnb_ref, ones_ref, maskbd_ref) = refs[:15]
        o_ref, sout_ref, r_s, v_s, kk_s, dir_s, bg_s, pre_s, gcol_s, oacc_s, st_s = refs[-11:]
    t = seq_len * n_seq
    z = zr_ref[...]
    rowi = jnp.bitwise_and(_iota((t, 1), 0), seq_len - 1)
    zp = jnp.where(rowi == 0, 0.0, pltpu.roll(z, 1, 0))
    zn = jnp.where(rowi == seq_len - 1, 0.0, pltpu.roll(z, t - 1, 0))
    z = z + mup_ref[...] * (zp - z) + mun_ref[...] * (zn - z)
    w = MIX_W
    r = z[:, :w]
    k = z[:, w:2 * w]
    v = z[:, 2 * w:3 * w]
    wd = jnp.tanh(z[:, 3 * w:3 * w + LANES])
    ad = z[:, 3 * w + LANES:3 * w + 2 * LANES]
    gd = _sigmoid(z[:, 3 * w + 2 * LANES:])
    ones_bd = ones_ref[...]
    kk = k * kk_ref[...]
    kk = kk * lax.rsqrt(_mm_sel_r(kk * kk, ones_bd) + 1e-6)
    r_s[...] = r
    v_s[...] = v
    kk_s[...] = kk
    bonus = jnp.zeros((t, w), F32)
    for d in range(2):
        w_log = -_softplus(-(w0_ref[d:d + 1] + _mm(wd, w2_ref[d]))) - 0.5
        a = _sigmoid(a0_ref[d:d + 1] + _mm(ad, a2_ref[d]))
        kd = k * (1.0 + (a - 1.0) * ka_ref[...])
        dir_s[:, (3 * d) * w:(3 * d + 1) * w] = -jnp.exp(w_log)
        dir_s[:, (3 * d + 1) * w:(3 * d + 2) * w] = kd
        dir_s[:, (3 * d + 2) * w:(3 * d + 3) * w] = kk * a
        bonus = bonus + _mm_sel_r(r * kd * rk_ref[...], ones_bd) * v
    bg_s[:, :w] = bonus
    bg_s[:, w:] = _mm(gd, g2_ref[...])
    oacc_s[...] = jnp.zeros((t, w), F32)
    if cached:
        st_s[...] = s0_ref[...]
    else:
        st_s[...] = jnp.zeros((n_seq, 2, w, w), F32)
    maskbd = maskbd_ref[...]
    masks = (_chunk_masks(False), _chunk_masks(True))
    eye_full = jnp.where(_iota((w, w), 0) == _iota((w, w), 1), 1.0, 0.0).astype(F32)
    n_chunks = seq_len // CHUNK

    def prepare_group(c0):
        where, items = [], []
        for j in range(CHUNK_GROUP):
            c = c0 + j
            rows = pl.ds(pl.multiple_of(c * CHUNK, CHUNK), CHUNK)
            for d in range(2):
                where.append((d, c, rows))
                items.append((r_s[rows, :], dir_s[rows, (3 * d + 1) * w:(3 * d + 2) * w], v_s[rows, :], kk_s[rows, :],
                              dir_s[rows, (3 * d + 2) * w:(3 * d + 3) * w], dir_s[rows, (3 * d) * w:(3 * d + 1) * w],
                              masks[d], d == 1))
        for (d, c, rows), (pre, gcol) in zip(where, _rwkv_prepare(items, maskbd, eye_full)):
            pre_s[d, rows, :] = pre
            gcol_s[d, pl.ds(pl.multiple_of(c * w, w), w), :] = gcol

    _for_chunk_groups(n_chunks * n_seq, prepare_group)

    def body(i, carry):
        where, items = [], []
        for q in range(n_seq):
            for d in range(2):
                c = q * n_chunks + (i if d == 0 else n_chunks - 1 - i)
                rows = pl.ds(pl.multiple_of(c * CHUNK, CHUNK), CHUNK)
                where.append((q, d, rows))
                items.append((pre_s[d, rows, :], v_s[rows, :], gcol_s[d, pl.ds(pl.multiple_of(c * w, w), w), :],
                              st_s[q, d]))
        for (q, d, rows), (o, z_new) in zip(where, _rwkv_step(items, maskbd)):
            oacc_s[rows, :] = oacc_s[rows, :] + o
            st_s[q, d] = z_new
        return carry

    lax.fori_loop(0, n_chunks, body, 0)
    o = oacc_s[...]
    inv_n = 1.0 / HEAD_DIM
    mu = _mm_sel_r(o, ones_bd) * inv_n
    oc = o - mu
    var = _mm_sel_r(oc * oc, ones_bd) * inv_n
    y = oc * lax.rsqrt(var + RWKV_GN_EPS) * gnw_ref[...] + gnb_ref[...]
    o_ref[...] = (y + bg_s[:, :w]) * bg_s[:, w:]
    for q in range(n_seq):
        for d in range(2):
            state = st_s[q, d] if cached else st_s[q, d].T
            for h in range(HEADS):
                sl = slice(h * HEAD_DIM, (h + 1) * HEAD_DIM)
                sout_ref[q, d, h] = state[sl, sl]


def _rwkv(zr, lw, layer, consts, seq_len, s0_bd=None, prev=None):
    n = zr.shape[0]
    nb = n // seq_len
    cached = s0_bd is not None
    n_seq = 1 if cached else SEQ_PER_STEP
    assert seq_len & (seq_len - 1) == 0 and seq_len % (CHUNK * CHUNK_GROUP) == 0 and nb % n_seq == 0
    rows = seq_len * n_seq
    args = [zr]
    in_specs = [pl.BlockSpec((rows, ZR_W), lambda b: (b, 0))]
    if cached:
        args.append(s0_bd)
        in_specs.append(pl.BlockSpec((1, None, 2, MIX_W, MIX_W), lambda b: (b, layer, 0, 0, 0)))
    layered = [lw["rwkv_mu_prev"], lw["rwkv_mu_next"], lw["rwkv_w0"], lw["rwkv_w2"], lw["rwkv_a0"], lw["rwkv_a2"],
               lw["rwkv_g2"], lw["rwkv_k_k"], lw["rwkv_k_a"], lw["rwkv_r_k"], lw["rwkv_gn_w"], lw["rwkv_gn_b"]]
    const = [consts["ones_bd"], consts["maskbd"]]
    args += layered + const
    in_specs += [_layer_spec(a, layer) for a in layered] + [_const_spec(a) for a in const]
    s_spec, s_shape, aliases = _state_out(nb, n_seq, layer, cached, prev, args, in_specs)
    return pl.pallas_call(
        functools.partial(_rwkv_kernel, seq_len=seq_len, n_seq=n_seq, cached=cached),
        grid=(nb // n_seq,),
        in_specs=in_specs,
        out_specs=[
            pl.BlockSpec((rows, MIX_W), lambda b: (b, 0)),
            s_spec,
        ],
        out_shape=[jax.ShapeDtypeStruct((n, MIX_W), F32), s_shape],
        input_output_aliases=aliases,
        scratch_shapes=[
            pltpu.VMEM((rows, MIX_W), F32),
            pltpu.VMEM((rows, MIX_W), F32),
            pltpu.VMEM((rows, MIX_W), F32),
            pltpu.VMEM((rows, 6 * MIX_W), F32),
            pltpu.VMEM((rows, 2 * MIX_W), F32),
            pltpu.VMEM((2, rows, RWKV_PRE_W), F32),
            pltpu.VMEM((2, rows // CHUNK * MIX_W, LANES), F32),
            pltpu.VMEM((rows, MIX_W), F32),
            pltpu.VMEM((n_seq, 2, MIX_W, MIX_W), F32),
        ],
        compiler_params=pltpu.CompilerParams(dimension_semantics=("arbitrary",),
                                             vmem_limit_bytes=VMEM_LIMIT),
        name="rwkv_lat" if cached else "rwkv_ctx",
    )(*args)


def _route(logits_t, bias):
    tm = logits_t.shape[1]
    neg = -jnp.inf
    sc = _sigmoid(logits_t)
    sc3 = sc.reshape(N_GROUPS, GROUP_SIZE, tm)
    sel = (sc + bias).reshape(N_GROUPS, GROUP_SIZE, tm)
    si = _iota(sel.shape, 1).astype(F32)
    m1 = jnp.max(sel, axis=1, keepdims=True)
    f1 = jnp.min(jnp.where(sel == m1, si, float(GROUP_SIZE)), axis=1, keepdims=True)
    m2 = jnp.max(jnp.where(si == f1, neg, sel), axis=1, keepdims=True)
    grp = m1 + m2
    gi = _iota(grp.shape, 0).astype(F32)
    gsel = jnp.zeros(grp.shape, F32)
    for _ in range(TOPK_GROUPS):
        mx = jnp.max(grp, axis=0, keepdims=True)
        fi = jnp.min(jnp.where(grp == mx, gi, float(N_GROUPS)), axis=0, keepdims=True)
        hit = gi == fi
        gsel = jnp.where(hit, 1.0, gsel)
        grp = jnp.where(hit, neg, grp)
    cur = jnp.where(gsel > 0.0, sel, neg)
    ei = (_iota(cur.shape, 0) * GROUP_SIZE + _iota(cur.shape, 1)).astype(F32)
    chosen = jnp.zeros(cur.shape, F32)
    ids, wts = [], []
    for _ in range(TOP_K):
        mx = jnp.max(jnp.max(cur, axis=0, keepdims=True), axis=1, keepdims=True)
        fi = jnp.min(jnp.min(jnp.where(cur == mx, ei, float(N_EXPERTS)), axis=0, keepdims=True),
                     axis=1, keepdims=True)
        hit = ei == fi
        chosen = jnp.where(hit, 1.0, chosen)
        cur = jnp.where(hit, neg, cur)
        ids.append(fi.reshape(1, tm))
        wts.append(jnp.sum(jnp.sum(jnp.where(hit, sc3, 0.0), axis=0, keepdims=True), axis=1, keepdims=True)
                   .reshape(1, tm))
    w = jnp.concatenate(wts, axis=0)
    w = w / jnp.sum(w, axis=0, keepdims=True) * ROUTE_SCALE
    return chosen.reshape(N_EXPERTS, tm), jnp.concatenate(ids, axis=0), w


def _pack_halves(x):
    half = x.shape[1] // 2
    bits = lax.bitcast_convert_type(x.astype(BF16).astype(F32), jnp.int32)
    lo = lax.shift_right_logical(bits[:, :half], jnp.int32(16))
    return jnp.bitwise_or(lo, jnp.bitwise_and(bits[:, half:], jnp.int32(-65536)))


def _unpack_halves(word):
    lo = lax.bitcast_convert_type(lax.shift_left(word, jnp.int32(16)), F32)
    hi = lax.bitcast_convert_type(jnp.bitwise_and(word, jnp.int32(-65536)), F32)
    return lo, hi


def _post_kernel(x_ref, om_ref, og_ref, or_ref, mod_ref, wo_ref, n2_ref, rt_ref, rb_ref, sgu_ref, sdn_ref,
                 tri_ref, cin_ref, x1_ref, h2_ref, eid_ref, rank_ref, ew_ref, cnt_ref, carry_s):
    @pl.when(pl.program_id(0) == 0)
    def _():
        carry_s[...] = cin_ref[...]

    m = mod_ref[0]
    g1 = m[:, 2 * D_MODEL:3 * D_MODEL]
    sh2 = m[:, 3 * D_MODEL:4 * D_MODEL]
    sc2 = m[:, 4 * D_MODEL:5 * D_MODEL]
    w_mla = MLA_HEADS * MLA_V
    mix = (_mm(om_ref[...], wo_ref[0:w_mla, :]) + _mm(og_ref[...], wo_ref[w_mla:w_mla + MIX_W, :])
           + _mm(or_ref[...], wo_ref[w_mla + MIX_W:, :]))
    x1 = x_ref[...] + g1 * mix
    h2 = _rms(x1, n2_ref[...]) * (1.0 + sc2) + sh2
    h2_ref[...] = _pack_halves(h2)
    g2 = m[:, 5 * D_MODEL:]
    x1_ref[...] = x1 + g2 * _mm(_swiglu_act(_mm(h2, sgu_ref[...])), sdn_ref[...])
    r_hi, r_lo = _split2(rt_ref[...])
    h_hi, h_lo = _split2(h2)
    logits_t = _mm_nt(r_hi, h_hi) + _mm_nt(r_hi, h_lo) + _mm_nt(r_lo, h_hi)
    chosen, ids, w = _route(logits_t, rb_ref[...])
    tm = chosen.shape[1]
    rank_et = (carry_s[:, 0:1] + _mm(chosen, tri_ref[...])).reshape(N_GROUPS, GROUP_SIZE, tm)
    ei = (_iota(rank_et.shape, 0) * GROUP_SIZE + _iota(rank_et.shape, 1)).astype(F32)
    ranks = []
    for k in range(TOP_K):
        pick = jnp.where(ei == ids[k:k + 1].reshape(1, 1, tm), rank_et, 0.0)
        ranks.append(jnp.sum(jnp.sum(pick, axis=0, keepdims=True), axis=1, keepdims=True).reshape(1, tm))
    eid_ref[...] = ids.astype(jnp.int32)
    rank_ref[...] = jnp.concatenate(ranks, axis=0).astype(jnp.int32)
    ew_ref[...] = jnp.concatenate([w, jnp.zeros((LANES - TOP_K, tm), F32)], axis=0).T
    total = carry_s[...] + jnp.sum(chosen, axis=1, keepdims=True)
    carry_s[...] = total
    cnt_ref[...] = total


def _post(x2d, om, og, orw, mods, lw, layer, consts, tm, seq_len, mod_base, counts_in):
    n = x2d.shape[0]
    tiles_per_seq = seq_len // tm if mod_base else 1

    def mod_idx(i):
        return (layer, mod_base + i // tiles_per_seq if mod_base else 0, 0, 0)

    row = lambda w: pl.BlockSpec((tm, w), lambda i: (i, 0))
    col = lambda h: pl.BlockSpec((h, tm), lambda i: (0, i))
    full = lambda a: _layer_spec(a, layer)
    tail = [lw["w_out"], lw["norm2"], lw["router_t"], lw["router_b"], lw["shared_w_gu"], lw["shared_w_down"]]
    tri = consts["tri_tokens"]
    return pl.pallas_call(
        _post_kernel,
        grid=(n // tm,),
        in_specs=[row(D_MODEL), row(MLA_HEADS * MLA_V), row(MIX_W), row(MIX_W),
                  pl.BlockSpec((None, 1, 1, 6 * D_MODEL), mod_idx)] + [full(a) for a in tail]
        + [_const_spec(tri), _const_spec(counts_in)],
        out_specs=[row(D_MODEL), row(D_MODEL // 2), col(TOP_K), col(TOP_K), row(LANES),
                   pl.BlockSpec((N_EXPERTS, LANES), lambda i: (0, 0))],
        out_shape=[
            jax.ShapeDtypeStruct((n, D_MODEL), F32),
            jax.ShapeDtypeStruct((n, D_MODEL // 2), jnp.int32),
            jax.ShapeDtypeStruct((TOP_K, n), jnp.int32),
            jax.ShapeDtypeStruct((TOP_K, n), jnp.int32),
            jax.ShapeDtypeStruct((n, LANES), F32),
            jax.ShapeDtypeStruct((N_EXPERTS, LANES), F32),
        ],
        scratch_shapes=[pltpu.VMEM((N_EXPERTS, LANES), F32)],
        compiler_params=pltpu.CompilerParams(dimension_semantics=("arbitrary",),
                                             vmem_limit_bytes=VMEM_LIMIT),
        name="post",
    )(x2d, om, og, orw, mods, *tail, tri, counts_in)


FINAL_CTX_PIECES = 4
MOE_ROWS = 512
SC_ROWS = 128
SC_SUBCORES = 32


def _swiglu_act(gu):
    return _silu(gu[:, :D_EXPERT]) * gu[:, D_EXPERT:]


def _dispatch_plan(eid, rank, counts, n, rows):
    n_blocks = n * TOP_K // rows + N_EXPERTS
    cnt = counts[:, 0].astype(jnp.int32)
    blocks = (cnt + rows - 1) // rows
    block_end = jnp.cumsum(blocks)
    offset = (block_end - blocks) * rows
    experts = jnp.arange(N_EXPERTS, dtype=jnp.int32)
    dest = jnp.sum(jnp.where(eid[..., None] == experts, offset, 0), axis=-1) + rank
    block_ids = jnp.arange(n_blocks, dtype=jnp.int32)
    block_expert = jnp.minimum(jnp.sum((block_end[None, :] <= block_ids[:, None]).astype(jnp.int32), axis=1),
                               N_EXPERTS - 1)
    n_used = block_end[-1:]
    prev_expert = jnp.concatenate([jnp.full((1,), -1, jnp.int32), block_expert[:-1]])
    first = ((block_ids < n_used[0]) & (block_expert != prev_expert)).astype(jnp.int32)
    slot = jnp.bitwise_and(jnp.cumsum(first) - 1, 1)
    owner_or_none = jnp.where(blocks > 0, experts, N_EXPERTS)
    next_owner = jnp.concatenate([lax.cummin(owner_or_none[::-1])[::-1][1:], jnp.full((1,), N_EXPERTS, jnp.int32)])
    nxt = jnp.sum(jnp.where(block_expert[:, None] == experts, next_owner, 0), axis=1)
    schedule = (block_expert, first, slot.astype(jnp.int32), nxt.astype(jnp.int32))
    return dest, schedule, n_used[0].astype(jnp.int32), n_blocks


def _sc_mesh():
    return plsc.VectorSubcoreMesh(core_axis_name="core", subcore_axis_name="subcore")


def _sc_dispatch(groups, dest, n_rows):
    w = groups[0].shape[1]
    dtype = groups[0].dtype

    @functools.partial(pl.kernel, out_type=jax.ShapeDtypeStruct((n_rows, w), dtype), mesh=_sc_mesh(),
                       scratch_types=[pltpu.VMEM((SC_ROWS, w), dtype), pltpu.VMEM((TOP_K, SC_ROWS), jnp.int32)])
    def kern(*refs):
        x_refs, d_hbm, o_hbm, xv, dv = refs[:len(groups)], *refs[len(groups):]
        sid = lax.axis_index("core") * (SC_SUBCORES // 2) + lax.axis_index("subcore")
        start = 0
        for x_hbm, x in zip(x_refs, groups):
            def chunk(c, x_hbm=x_hbm, start=start):
                r0 = pl.multiple_of(c * SC_ROWS, SC_ROWS)
                pltpu.sync_copy(x_hbm.at[pl.ds(r0, SC_ROWS)], xv)
                pltpu.sync_copy(d_hbm.at[:, pl.ds(start + r0, SC_ROWS)], dv)
                for k in range(TOP_K):
                    pltpu.sync_copy(xv, o_hbm.at[dv.at[k]])

            pl.loop(sid, x.shape[0] // SC_ROWS, step=SC_SUBCORES)(chunk)
            start += x.shape[0]

    return kern(*groups, dest)


def _sc_gather(y, idx):
    w = y.shape[1]
    n_chunks = idx.shape[0]

    @functools.partial(pl.kernel, out_type=jax.ShapeDtypeStruct((n_chunks * SC_ROWS, w), y.dtype), mesh=_sc_mesh(),
                       scratch_types=[pltpu.VMEM((SC_ROWS, w), y.dtype), pltpu.VMEM((1, SC_ROWS), jnp.int32)])
    def kern(y_hbm, i_hbm, o_hbm, ov, iv):
        sid = lax.axis_index("core") * (SC_SUBCORES // 2) + lax.axis_index("subcore")

        @pl.loop(sid, n_chunks, step=SC_SUBCORES)
        def _(c):
            pltpu.sync_copy(i_hbm.at[pl.ds(c, 1)], iv)
            pltpu.sync_copy(y_hbm.at[iv.at[0]], ov)
            pltpu.sync_copy(ov, o_hbm.at[pl.ds(pl.multiple_of(c * SC_ROWS, SC_ROWS), SC_ROWS)])

    return kern(y, idx)


def _moe_rows_kernel(be_ref, first_ref, slot_ref, nxt_ref, x_ref, wgu_hbm, wdn_hbm, y_ref,
                     wgu_f, wdn_f, wgu_b, wdn_b, sem, *, layer):
    b = pl.program_id(0)

    def weight_copies(expert, slot):
        return (pltpu.make_async_copy(wgu_hbm.at[layer, expert], wgu_f.at[slot], sem.at[slot, 0]),
                pltpu.make_async_copy(wdn_hbm.at[layer, expert], wdn_f.at[slot], sem.at[slot, 1]))

    @pl.when(b == 0)
    def _():
        for copy in weight_copies(be_ref[0], 0):
            copy.start()

    @pl.when(first_ref[b] == 1)
    def _():
        slot = slot_ref[b]
        for copy in weight_copies(be_ref[b], slot):
            copy.wait()

        @pl.when(nxt_ref[b] < N_EXPERTS)
        def _():
            for copy in weight_copies(nxt_ref[b], 1 - slot):
                copy.start()

        wgu_b[...] = wgu_f[slot].astype(BF16)
        wdn_b[...] = wdn_f[slot].astype(BF16)

    half = D_MODEL // 2
    lo, hi = _unpack_halves(x_ref[...])
    gu = _mm(lo, wgu_b[0:half, :]) + _mm(hi, wgu_b[half:, :])
    y_ref[...] = _pack_halves(_mm(_swiglu_act(gu), wdn_b[...]))


def _moe_rows(xs, schedule, n_used, lw, layer, rows):
    half = D_MODEL // 2
    row_spec = pl.BlockSpec((rows, half), lambda b, *_: (b, 0))
    return pl.pallas_call(
        functools.partial(_moe_rows_kernel, layer=layer),
        grid_spec=pltpu.PrefetchScalarGridSpec(
            num_scalar_prefetch=4,
            grid=(n_used,),
            in_specs=[row_spec, pl.BlockSpec(memory_space=pl.ANY), pl.BlockSpec(memory_space=pl.ANY)],
            out_specs=row_spec,
            scratch_shapes=[
                pltpu.VMEM((2, D_MODEL, 2 * D_EXPERT), F32),
                pltpu.VMEM((2, D_EXPERT, D_MODEL), F32),
                pltpu.VMEM((D_MODEL, 2 * D_EXPERT), BF16),
                pltpu.VMEM((D_EXPERT, D_MODEL), BF16),
                pltpu.SemaphoreType.DMA((2, 2)),
            ],
        ),
        out_shape=jax.ShapeDtypeStruct(xs.shape, jnp.int32),
        compiler_params=pltpu.CompilerParams(dimension_semantics=("arbitrary",),
                                             vmem_limit_bytes=VMEM_LIMIT),
        name="moe_rows",
    )(*schedule, xs, lw["moe_w_gu"], lw["moe_w_down"])


def _moe_combine_kernel(yg_ref, ew_ref, x1_ref, mod_ref, nf_ref, o_ref, *, final):
    ew = ew_ref[...]
    acc_lo = acc_hi = None
    for k in range(TOP_K):
        lo, hi = _unpack_halves(yg_ref[k])
        wk = ew[:, k:k + 1]
        acc_lo = wk * lo if acc_lo is None else acc_lo + wk * lo
        acc_hi = wk * hi if acc_hi is None else acc_hi + wk * hi
    g2 = mod_ref[0][:, 5 * D_MODEL:]
    x2 = x1_ref[...] + g2 * jnp.concatenate([acc_lo, acc_hi], axis=1)
    if final:
        x2 = _rms(x2, nf_ref[...])
    o_ref[...] = x2


def _moe_combine(yg, tile0, ew, x1, mods, layer, norm_f, tm, seq_len, mod_base, final):
    half = D_MODEL // 2
    tiles_per_seq = seq_len // tm if mod_base else 1
    assert yg.shape[1] % tm == 0 and yg.shape[1] > 0

    def mod_idx(i):
        return (layer, mod_base + (tile0 + i) // tiles_per_seq if mod_base else 0, 0, 0)

    row = lambda w: pl.BlockSpec((tm, w), lambda i: (tile0 + i, 0))
    return pl.pallas_call(
        functools.partial(_moe_combine_kernel, final=final),
        grid=(yg.shape[1] // tm,),
        in_specs=[pl.BlockSpec((TOP_K, tm, half), lambda i: (0, i, 0)), row(LANES), row(D_MODEL),
                  pl.BlockSpec((None, 1, 1, 6 * D_MODEL), mod_idx), _const_spec(norm_f)],
        out_specs=row(D_MODEL),
        out_shape=jax.ShapeDtypeStruct(x1.shape, F32),
        input_output_aliases={2: 0},
        compiler_params=pltpu.CompilerParams(dimension_semantics=("arbitrary",),
                                             vmem_limit_bytes=VMEM_LIMIT),
        name="moe_combine_final" if final else "moe_combine",
    )(yg, ew, x1, mods, norm_f)


def _moe_experts(groups, pieces, eid, rank, counts, lw, layer):
    n = sum(g.shape[0] for g in groups)
    dest, schedule, n_used, n_blocks = _dispatch_plan(eid, rank, counts, n, MOE_ROWS)
    xs = _sc_dispatch(groups, dest, n_blocks * MOE_ROWS)
    y = _moe_rows(xs, schedule, n_used, lw, layer, MOE_ROWS)
    outs, start = [], 0
    for g, n_pieces in zip(groups, pieces):
        size = g.shape[0] // n_pieces
        outs.append([])
        for _ in range(n_pieces):
            idx = dest[:, start:start + size].reshape(size * TOP_K // SC_ROWS, SC_ROWS)
            outs[-1].append(_sc_gather(y, idx).reshape(TOP_K, size, D_MODEL // 2))
            start += size
    return outs


def _constants():
    idx = np.arange(MIX_W)
    same_head = (idx[:, None] // HEAD_DIM) == (idx[None, :] // HEAD_DIM)
    maskbd = jnp.asarray(same_head, BF16)
    eexp = np.zeros((LANES, 4 * MIX_W), np.float32)
    for blk in range(4):
        kind, d = divmod(blk, 2)
        for h in range(HEADS):
            src = kind * 2 * HEADS + d * HEADS + h
            eexp[src, blk * MIX_W + h * HEAD_DIM: blk * MIX_W + (h + 1) * HEAD_DIM] = 1.0
    tri = np.triu(np.ones((POST_TM, POST_TM), np.float32), 1)
    return {"maskbd": maskbd, "ones_bd": maskbd, "eexp": jnp.asarray(eexp, BF16), "tri_tokens": jnp.asarray(tri, BF16)}


def _rope_tables(n):
    rows = n // GRID_W
    row = jnp.repeat(jnp.arange(rows, dtype=F32), GRID_W)
    col = jnp.tile(jnp.arange(GRID_W, dtype=F32), rows)
    axis_dim = MLA_ROPE // 2
    inv = jnp.power(ROPE_BASE, -jnp.arange(0, axis_dim, 2, dtype=F32) / axis_dim)
    ang_r = row[:, None] * inv
    ang_c = col[:, None] * inv
    cr, sr, cc, sc = jnp.cos(ang_r), jnp.sin(ang_r), jnp.cos(ang_c), jnp.sin(ang_c)
    zeros = jnp.zeros((n, LANES - MLA_ROPE), F32)
    cos_t = jnp.concatenate([cr, cr, cc, cc, zeros], axis=1)
    sin_t = jnp.concatenate([-sr, sr, -sc, sc, zeros], axis=1)
    return cos_t, sin_t


def _stacked_weights(p):
    def per_direction(w):
        half = jnp.zeros((DEPTH, 64, MIX_W), F32)
        return jnp.stack([jnp.concatenate([w[:, 0], half], axis=1),
                          jnp.concatenate([half, w[:, 1]], axis=1)], axis=1).astype(BF16)

    row = lambda v: v.reshape(DEPTH, 1, -1)
    pad_row = lambda v: jnp.pad(row(v), ((0, 0), (0, 0), (0, LANES - 2 * HEADS)))
    return {
        "norm1": row(p["norm1"]),
        "w_in": p["w_in"],
        "q_norm": row(p["mla_q_norm"]),
        "w_uq": p["mla_w_uq"],
        "kv_norm": row(p["mla_kv_norm"]),
        "w_ukv": p["mla_w_ukv"].astype(BF16),
        "gdn_conv": p["gdn_conv"],
        "gdn_alog": pad_row(p["gdn_a_log"]),
        "gdn_dtb": pad_row(p["gdn_dt_bias"]),
        "gdn_norm": jnp.tile(row(p["gdn_norm"]), (1, 1, HEADS)),
        "rwkv_mu_prev": row(p["rwkv_mu_prev"]),
        "rwkv_mu_next": row(p["rwkv_mu_next"]),
        "rwkv_w0": p["rwkv_w0"],
        "rwkv_w2": per_direction(p["rwkv_w2"]),
        "rwkv_a0": p["rwkv_a0"],
        "rwkv_a2": per_direction(p["rwkv_a2"]),
        "rwkv_g2": p["rwkv_g2"].astype(BF16),
        "rwkv_k_k": row(p["rwkv_k_k"]),
        "rwkv_k_a": row(p["rwkv_k_a"]),
        "rwkv_r_k": row(p["rwkv_r_k"]),
        "rwkv_gn_w": row(p["rwkv_gn_w"]),
        "rwkv_gn_b": row(p["rwkv_gn_b"]),
        "w_out": p["w_out"].astype(BF16),
        "norm2": row(p["norm2"]),
        "router_t": jnp.swapaxes(p["moe_router"], 1, 2),
        "router_b": p["moe_bias"].reshape(DEPTH, N_EXPERTS, 1),
        "moe_w_gu": p["moe_w_gu"],
        "moe_w_down": p["moe_w_down"],
        "shared_w_gu": p["shared_w_gu"].astype(BF16),
        "shared_w_down": p["shared_w_down"].astype(BF16),
    }


def _embed_block_diag(s):
    b = s.shape[0]
    eye = jnp.eye(HEADS, dtype=s.dtype)
    out = jnp.einsum("bdhkv,hg->bdhkgv", s, eye)
    return out.reshape(b, 2, MIX_W, MIX_W)


def _layer_front(x2d, mods, lw, l, consts, seq_len, mod_base, cache, tm, tq, counts_in, prev=None):
    zm, zg, zab, zr = _inproj(x2d, mods, lw, l, tm, seq_len, mod_base)
    if cache is None:
        o_mla, ckv, kpe = _mla_ctx(zm, lw, l, seq_len, prev[:2])
        o_gdn, s_gdn = _gdn(zg, zab, lw, l, consts, seq_len, prev=prev[2])
        o_rwkv, s_rwkv = _rwkv(zr, lw, l, consts, seq_len, prev=prev[3])
        new = (ckv, kpe, s_gdn, s_rwkv)
    else:
        cckv, ckpe, rc, rs, sg, sr = cache
        o_mla = _mla_lat(zm, lw, l, seq_len, tq, (cckv, ckpe, rc, rs))
        o_gdn, _ = _gdn(zg, zab, lw, l, consts, seq_len, sg)
        o_rwkv, _ = _rwkv(zr, lw, l, consts, seq_len, sr)
        new = None
    routed = _post(x2d, o_mla, o_gdn, o_rwkv, mods, lw, l, consts, tm, seq_len, mod_base, counts_in)
    return routed, new


def kernel(x_prompt, x_sample, cache_mla_ckv, cache_mla_kpe, state_gdn, state_rwkv, c, c_ctx, ada_w, ada_b, norm1, w_in, mla_q_norm, mla_w_uq, mla_kv_norm, mla_w_ukv, gdn_conv, gdn_a_log, gdn_dt_bias, gdn_norm, rwkv_mu_prev, rwkv_mu_next, rwkv_w0, rwkv_w2, rwkv_a0, rwkv_a2, rwkv_g2, rwkv_k_k, rwkv_k_a, rwkv_r_k, rwkv_gn_w, rwkv_gn_b, w_out, norm2, moe_router, moe_bias, moe_w_gu, moe_w_down, shared_w_gu, shared_w_down, norm_f):
    p = dict(norm1=norm1, w_in=w_in, mla_q_norm=mla_q_norm, mla_w_uq=mla_w_uq, mla_kv_norm=mla_kv_norm,
             mla_w_ukv=mla_w_ukv, gdn_conv=gdn_conv, gdn_a_log=gdn_a_log, gdn_dt_bias=gdn_dt_bias,
             gdn_norm=gdn_norm, rwkv_mu_prev=rwkv_mu_prev, rwkv_mu_next=rwkv_mu_next, rwkv_w0=rwkv_w0,
             rwkv_w2=rwkv_w2, rwkv_a0=rwkv_a0, rwkv_a2=rwkv_a2, rwkv_g2=rwkv_g2, rwkv_k_k=rwkv_k_k,
             rwkv_k_a=rwkv_k_a, rwkv_r_k=rwkv_r_k, rwkv_gn_w=rwkv_gn_w, rwkv_gn_b=rwkv_gn_b, w_out=w_out,
             norm2=norm2, moe_router=moe_router, moe_bias=moe_bias, moe_w_gu=moe_w_gu, moe_w_down=moe_w_down,
             shared_w_gu=shared_w_gu, shared_w_down=shared_w_down)
    weights = _stacked_weights(p)
    consts = _constants()
    nf = norm_f.reshape(1, D_MODEL)
    b_ctx, t_ctx, _ = x_prompt.shape
    b_lat, t_lat, _ = x_sample.shape

    cvec8 = jnp.concatenate([c_ctx[None, :], c, jnp.zeros((8 - 1 - b_lat, D_MODEL), F32)], axis=0)
    mods = _adaln(cvec8, ada_w, ada_b)
    mods = mods.reshape(DEPTH, 8, 1, 6 * D_MODEL)

    rc, rs = _rope_tables(t_lat)
    ckpe = jnp.pad(cache_mla_kpe, ((0, 0), (0, 0), (0, 0), (0, LANES - MLA_ROPE)))
    cache = (cache_mla_ckv, ckpe, rc, rs, _embed_block_diag_layers(state_gdn),
             _embed_block_diag_layers(jnp.swapaxes(state_rwkv, -1, -2)))
    xp = x_prompt.reshape(b_ctx * t_ctx, D_MODEL)
    xs = x_sample.reshape(b_lat * t_lat, D_MODEL)
    n_ctx = xp.shape[0]
    tm = POST_TM
    state_shape = (b_ctx, DEPTH, 2, HEADS, HEAD_DIM, HEAD_DIM)
    ctx_outs = (jnp.zeros((b_ctx, DEPTH, t_ctx, KV_LORA), F32), jnp.zeros((b_ctx, DEPTH, t_ctx, MLA_ROPE), F32),
                jnp.zeros(state_shape, F32), jnp.zeros(state_shape, F32))
    for l in range(DEPTH):
        final = l == DEPTH - 1
        no_pairs = jnp.zeros((N_EXPERTS, LANES), F32)
        (x1c, hc, eidc, rankc, ewc, cnt_c), ctx_outs = _layer_front(xp, mods, weights, l, consts, t_ctx, 0, None,
                                                                    tm, t_ctx, no_pairs, ctx_outs)
        (x1s, hs, eids, ranks, ews, cnt), _ = _layer_front(xs, mods, weights, l, consts, t_lat, 1, cache,
                                                           tm, 256, cnt_c)
        pieces = (FINAL_CTX_PIECES if final else 1, 1)
        ygs_c, ygs_s = _moe_experts([hc, hs], pieces, jnp.concatenate([eidc, eids], axis=1),
                                    jnp.concatenate([rankc, ranks], axis=1), cnt, weights, l)
        xp, xs = x1c, x1s
        for i, yg in enumerate(ygs_c):
            xp = _moe_combine(yg, i * (yg.shape[1] // tm), ewc, xp, mods, l, nf, tm, t_ctx, 0, final)
        for i, yg in enumerate(ygs_s):
            xs = _moe_combine(yg, i * (yg.shape[1] // tm), ews, xs, mods, l, nf, tm, t_lat, 1, final)

    y_prompt = xp.reshape(b_ctx, t_ctx, D_MODEL)
    y_sample = xs.reshape(b_lat, t_lat, D_MODEL)
    new_ckv, new_kpe, new_gdn, new_rwkv = ctx_outs
    return (y_prompt, y_sample, new_ckv, new_kpe, new_gdn, new_rwkv)


def _embed_block_diag_layers(s):
    b = s.shape[0]
    return _embed_block_diag(s.reshape(b * DEPTH, 2, HEADS, HEAD_DIM, HEAD_DIM)).reshape(
        b, DEPTH, 2, MIX_W, MIX_W)
```

```python
import functools

import numpy as np
import jax
import jax.numpy as jnp
from jax import lax
from jax.experimental import pallas as pl
from jax.experimental.pallas import tpu as pltpu
from jax.experimental.pallas import tpu_sc as plsc

F32 = jnp.float32
BF16 = jnp.bfloat16

D_MODEL = 1024
DEPTH = 2
PAST_LEN = 512
GRID_W = 64
NORM_EPS = 1e-6

MLA_HEADS = 4
MLA_NOPE = 128
MLA_ROPE = 64
MLA_V = 128
Q_LORA = 384
KV_LORA = 256
ROPE_BASE = 10000.0
MLA_SCALE = (MLA_NOPE + MLA_ROPE) ** -0.5

HEADS = 4
HEAD_DIM = 64
MIX_W = HEADS * HEAD_DIM
GDN_CONV_CH = 3 * MIX_W
CHUNK = 64
RWKV_GN_EPS = 64e-5

N_EXPERTS = 64
TOP_K = 8
N_GROUPS = 8
GROUP_SIZE = N_EXPERTS // N_GROUPS
TOPK_GROUPS = 4
D_EXPERT = 256
ROUTE_SCALE = 2.5

P_MLA = Q_LORA + KV_LORA + MLA_ROPE
P_GDN = GDN_CONV_CH + MIX_W + 4 * HEADS
P_RWKV = 3 * MIX_W + 128 + 128 + 128

LANES = 128
ZM_W = Q_LORA + KV_LORA + 2 * LANES
ZG_W = GDN_CONV_CH + MIX_W
ZR_W = P_RWKV
QH_W = 2 * LANES
VMEM_LIMIT = 56 * 1024 * 1024
POST_TM = 512

_ROPE_SWAP = np.concatenate([np.arange(16, 32), np.arange(0, 16), np.arange(48, 64), np.arange(32, 48)])


def _sigmoid(x):
    return 1.0 / (1.0 + jnp.exp(-x))


def _silu(x):
    return x * _sigmoid(x)


def _softplus(x):
    return jnp.maximum(x, 0.0) + jnp.log(1.0 + jnp.exp(-jnp.abs(x)))


def _rms(x, g, eps=NORM_EPS):
    return x * lax.rsqrt(jnp.mean(x * x, axis=-1, keepdims=True) + eps) * g


def _mm(a, b):
    return jnp.dot(a.astype(BF16), b.astype(BF16), preferred_element_type=F32)


def _mm_nt(a, b):
    return lax.dot_general(a.astype(BF16), b.astype(BF16), (((1,), (1,)), ((), ())),
                           preferred_element_type=F32)


def _mm_tn(a, b):
    return lax.dot_general(a.astype(BF16), b.astype(BF16), (((0,), (0,)), ((), ())),
                           preferred_element_type=F32)


def _split3(x):
    p1 = x.astype(BF16)
    r1 = x - p1.astype(F32)
    p2 = r1.astype(BF16)
    r2 = r1 - p2.astype(F32)
    return p1, p2, r2.astype(BF16)


def _mm_sel_l(sel, x):
    p1, p2, p3 = _split3(x)
    return _mm(sel, p1) + _mm(sel, p2) + _mm(sel, p3)


def _mm_sel_r(x, sel):
    p1, p2, p3 = _split3(x)
    return _mm(p1, sel) + _mm(p2, sel) + _mm(p3, sel)


def _iota(shape, dim):
    return lax.broadcasted_iota(jnp.int32, shape, dim)


def _layer_spec(a, layer, **kw):
    nd = a.ndim - 1
    return pl.BlockSpec((None,) + a.shape[1:], lambda *_: (layer,) + (0,) * nd, **kw)


def _const_spec(a, **kw):
    return pl.BlockSpec(a.shape, lambda *_: (0,) * a.ndim, **kw)


def _bd(x, maskbd):
    xb = x.astype(BF16)
    return jnp.concatenate([xb] * HEADS, axis=0) * maskbd


def _chunk_masks(rev):
    row = _iota((CHUNK, MIX_W), 0)
    col = jnp.bitwise_and(_iota((CHUNK, MIX_W), 1), HEAD_DIM - 1)
    r2 = _iota((CHUNK, CHUNK), 0)
    c2 = _iota((CHUNK, CHUNK), 1)
    if rev:
        inc, strict, tri = row <= col, row < col, r2 <= c2
    else:
        inc, strict, tri = row >= col, row > col, r2 >= c2
    eye = jnp.where(row == col, 1.0, 0.0).astype(F32)
    return inc, strict, jnp.where(tri, 1.0, 0.0).astype(BF16), eye


def _split2(x):
    hi = x.astype(BF16)
    return hi, (x - hi.astype(F32)).astype(BF16)


def _mm_bd3(x, p, maskbd):
    n = x.shape[0]
    xh, xl = _split2(x)
    ph, pl_ = _split2(p)
    r = jnp.dot(jnp.concatenate([xh, xl], axis=0), _bd(ph, maskbd), preferred_element_type=F32)
    return r[:n] + r[n:] + jnp.dot(xh, _bd(pl_, maskbd), preferred_element_type=F32)


def _neumann_inverse(a_list, eye_list, maskbd):
    bs = [-a for a in a_list]
    ms = [eye + b for eye, b in zip(eye_list, bs)]
    ps = [_mm_bd3(b, b, maskbd) for b in bs]
    for _ in range(4):
        boths = [_mm_bd3(jnp.concatenate([m, p], axis=0), p, maskbd) for m, p in zip(ms, ps)]
        ms = [m + both[:CHUNK] for m, both in zip(ms, boths)]
        ps = [both[CHUNK:] for both in boths]
    return [m + _mm_bd3(m, p, maskbd) for m, p in zip(ms, ps)]


def _adaln_kernel(c_ref, w_ref, b_ref, o_ref):
    cv = c_ref[...]
    o_ref[0] = _mm(_silu(cv), w_ref[0]) + b_ref[0]


def _adaln(cvec8, ada_w, ada_b):
    tn = 768
    n_out = 6 * D_MODEL
    return pl.pallas_call(
        _adaln_kernel,
        grid=(DEPTH, n_out // tn),
        in_specs=[
            pl.BlockSpec((8, D_MODEL), lambda l, j: (0, 0)),
            pl.BlockSpec((1, D_MODEL, tn), lambda l, j: (l, 0, j)),
            pl.BlockSpec((1, 1, tn), lambda l, j: (l, 0, j)),
        ],
        out_specs=pl.BlockSpec((1, 8, tn), lambda l, j: (l, 0, j)),
        out_shape=jax.ShapeDtypeStruct((DEPTH, 8, n_out), F32),
        compiler_params=pltpu.CompilerParams(dimension_semantics=("arbitrary", "arbitrary"),
                                             vmem_limit_bytes=VMEM_LIMIT),
        name="adaln",
    )(cvec8, ada_w, ada_b.reshape(DEPTH, 1, n_out))


_KPE0 = Q_LORA + KV_LORA
_W_IN_MOVES = (
    [(0, 0, P_MLA)]
    + [(P_MLA + LANES - MLA_ROPE + 16 * j, _KPE0 + 16 * int(_ROPE_SWAP[16 * j] // 16), 16) for j in range(4)]
    + [(ZM_W, P_MLA, ZG_W), (ZM_W + ZG_W, P_MLA + ZG_W, 4 * HEADS), (ZM_W + ZG_W + LANES, P_MLA + P_GDN, P_RWKV)]
)
W_IN_PAD = ZM_W + ZG_W + LANES + ZR_W


def _inproj_kernel(x_ref, mod_ref, n1_ref, w_ref, zm_ref, zg_ref, zab_ref, zr_ref, w_s):
    @pl.when(pl.program_id(0) == 0)
    def _():
        w_s[...] = jnp.zeros(w_s.shape, BF16)
        for dst, src, width in _W_IN_MOVES:
            w_s[:, dst:dst + width] = w_ref[:, src:src + width].astype(BF16)

    m = mod_ref[0]
    sh = m[:, 0:D_MODEL]
    sc = m[:, D_MODEL:2 * D_MODEL]
    h = _rms(x_ref[...], n1_ref[...]) * (1.0 + sc) + sh
    z = _mm(h, w_s[...])
    o1 = ZM_W
    o2 = o1 + ZG_W
    o3 = o2 + LANES
    zm_ref[...] = z[:, :o1]
    zg_ref[...] = z[:, o1:o2]
    zab_ref[...] = z[:, o2:o3]
    zr_ref[...] = z[:, o3:]


def _inproj(x2d, mods, lw, layer, tm, seq_len, mod_base):
    n = x2d.shape[0]
    tiles_per_seq = seq_len // tm if mod_base else 1

    def mod_idx(i):
        return (layer, mod_base + i // tiles_per_seq if mod_base else 0, 0, 0)

    return pl.pallas_call(
        _inproj_kernel,
        grid=(n // tm,),
        in_specs=[
            pl.BlockSpec((tm, D_MODEL), lambda i: (i, 0)),
            pl.BlockSpec((None, 1, 1, 6 * D_MODEL), mod_idx),
            _layer_spec(lw["norm1"], layer),
            _layer_spec(lw["w_in"], layer, pipeline_mode=pl.Buffered(1)),
        ],
        out_specs=[
            pl.BlockSpec((tm, ZM_W), lambda i: (i, 0)),
            pl.BlockSpec((tm, ZG_W), lambda i: (i, 0)),
            pl.BlockSpec((tm, LANES), lambda i: (i, 0)),
            pl.BlockSpec((tm, ZR_W), lambda i: (i, 0)),
        ],
        out_shape=[
            jax.ShapeDtypeStruct((n, ZM_W), F32),
            jax.ShapeDtypeStruct((n, ZG_W), F32),
            jax.ShapeDtypeStruct((n, LANES), F32),
            jax.ShapeDtypeStruct((n, ZR_W), F32),
        ],
        scratch_shapes=[pltpu.VMEM((D_MODEL, W_IN_PAD), BF16)],
        compiler_params=pltpu.CompilerParams(dimension_semantics=("arbitrary",),
                                             vmem_limit_bytes=VMEM_LIMIT),
        name="inproj",
    )(x2d, mods, lw["norm1"], lw["w_in"])


MLA_SEQ_PER_STEP = 4


def _mla_ctx_kernel(zm_ref, qn_ref, wuq_ref, kvn_ref, wukv_ref, prev_ckv, prev_kpe, o_ref, ckv_ref, kpe_ref,
                    *, seq_len, n_seq):
    del prev_ckv, prev_kpe
    o_kpe = Q_LORA + KV_LORA
    zm = zm_ref[...]
    ckv = _rms(zm[:, Q_LORA:o_kpe], kvn_ref[...])
    kpe = zm[:, o_kpe:o_kpe + LANES]
    kv = _mm(ckv, wukv_ref[...])
    q = _mm(_rms(zm[:, :Q_LORA], qn_ref[...]), wuq_ref[...])
    kpe_b = kpe.astype(BF16)
    seqs = [slice(i * seq_len, (i + 1) * seq_len) for i in range(n_seq)]
    for i, r in enumerate(seqs):
        ckv_ref[i] = ckv[r]
        kpe_ref[i] = kpe[r, :MLA_ROPE]
    pairs = [(r, h * QH_W) for r in seqs for h in range(MLA_HEADS)]
    scores = [_mm_nt(q[r, c0:c0 + QH_W], jnp.concatenate([kv[r, c0:c0 + LANES].astype(BF16), kpe_b[r]], axis=1))
              * MLA_SCALE for r, c0 in pairs]
    exps = [jnp.exp(s - jnp.max(s, axis=-1, keepdims=True)) for s in scores]
    outs = [_mm(e, kv[r, c0 + LANES:c0 + QH_W]) / jnp.sum(e, axis=-1, keepdims=True)
            for e, (r, c0) in zip(exps, pairs)]
    for o, (r, c0) in zip(outs, pairs):
        h = c0 // QH_W
        o_ref[r, h * MLA_V:(h + 1) * MLA_V] = o


def _mla_ctx(zm, lw, layer, seq_len, prev):
    n = zm.shape[0]
    nb = n // seq_len
    n_seq = MLA_SEQ_PER_STEP
    rows = n_seq * seq_len
    weights = [lw["q_norm"], lw["w_uq"], lw["kv_norm"], lw["w_ukv"]]
    return pl.pallas_call(
        functools.partial(_mla_ctx_kernel, seq_len=seq_len, n_seq=n_seq),
        grid=(nb // n_seq,),
        in_specs=[pl.BlockSpec((rows, ZM_W), lambda b: (b, 0))] + [_layer_spec(a, layer) for a in weights]
        + [pl.BlockSpec(memory_space=pl.ANY)] * 2,
        out_specs=[
            pl.BlockSpec((rows, MLA_HEADS * MLA_V), lambda b: (b, 0)),
            pl.BlockSpec((n_seq, None, seq_len, KV_LORA), lambda b: (b, layer, 0, 0)),
            pl.BlockSpec((n_seq, None, seq_len, MLA_ROPE), lambda b: (b, layer, 0, 0)),
        ],
        out_shape=[
            jax.ShapeDtypeStruct((n, MLA_HEADS * MLA_V), F32),
            jax.ShapeDtypeStruct((nb, DEPTH, seq_len, KV_LORA), F32),
            jax.ShapeDtypeStruct((nb, DEPTH, seq_len, MLA_ROPE), F32),
        ],
        input_output_aliases={5: 1, 6: 2},
        compiler_params=pltpu.CompilerParams(dimension_semantics=("arbitrary",),
                                             vmem_limit_bytes=VMEM_LIMIT),
        name="mla_ctx",
    )(zm, *weights, *prev)


def _mla_lat_kernel(zm_ref, cckv_ref, ckpe_ref, rc_ref, rs_ref, qn_ref, wuq_ref, wuqs_ref, kvn_ref, wukv_ref,
                    o_ref, k_s, v_s, *, seq_len, tq, past):
    qi = pl.program_id(1)
    o_kpe = Q_LORA + KV_LORA

    @pl.when(qi == 0)
    def _():
        zm = zm_ref[...]
        ckv = _rms(zm[:, Q_LORA:o_kpe], kvn_ref[...])
        kpe = zm[:, o_kpe:o_kpe + LANES] * rc_ref[...] + zm[:, o_kpe + LANES:o_kpe + 2 * LANES] * rs_ref[...]
        kvc = _mm(cckv_ref[0], wukv_ref[...])
        kpc = ckpe_ref[0].astype(BF16)
        kv = _mm(ckv, wukv_ref[...])
        kpe = kpe.astype(BF16)
        for h in range(MLA_HEADS):
            c0 = h * QH_W
            k_s[0:past, c0:c0 + LANES] = kvc[:, c0:c0 + LANES].astype(BF16)
            k_s[0:past, c0 + LANES:c0 + QH_W] = kpc
            v_s[0:past, h * MLA_V:(h + 1) * MLA_V] = kvc[:, c0 + LANES:c0 + QH_W].astype(BF16)
            k_s[past:past + seq_len, c0:c0 + LANES] = kv[:, c0:c0 + LANES].astype(BF16)
            k_s[past:past + seq_len, c0 + LANES:c0 + QH_W] = kpe
            v_s[past:past + seq_len, h * MLA_V:(h + 1) * MLA_V] = kv[:, c0 + LANES:c0 + QH_W].astype(BF16)

    r0 = pl.multiple_of(qi * tq, tq)
    zq = zm_ref[pl.ds(r0, tq), :]
    cq = _rms(zq[:, :Q_LORA], qn_ref[...])
    q = _mm(cq, wuq_ref[...])
    qs = _mm(cq, wuqs_ref[...])
    qc = jnp.concatenate([jnp.ones((tq, LANES), F32), rc_ref[pl.ds(r0, tq), :]], axis=1)
    qsn = jnp.concatenate([jnp.zeros((tq, LANES), F32), rs_ref[pl.ds(r0, tq), :]], axis=1)
    for h in range(MLA_HEADS):
        c0 = h * QH_W
        qh = q[:, c0:c0 + QH_W] * qc + qs[:, c0:c0 + QH_W] * qsn
        s = _mm_nt(qh, k_s[:, c0:c0 + QH_W]) * MLA_SCALE
        e = jnp.exp(s - jnp.max(s, axis=-1, keepdims=True))
        den = jnp.sum(e, axis=-1, keepdims=True)
        o_ref[:, h * MLA_V:(h + 1) * MLA_V] = _mm(e, v_s[:, h * MLA_V:(h + 1) * MLA_V]) / den


def _mla_lat(zm, lw, layer, seq_len, tq, cache):
    n = zm.shape[0]
    nb = n // seq_len
    past = PAST_LEN
    tk = past + seq_len
    cckv, ckpe, rc, rs = cache
    weights = [lw["q_norm"], lw["w_uq"], lw["w_uq_sw"], lw["kv_norm"], lw["w_ukv"]]
    return pl.pallas_call(
        functools.partial(_mla_lat_kernel, seq_len=seq_len, tq=tq, past=past),
        grid=(nb, seq_len // tq),
        in_specs=[pl.BlockSpec((seq_len, ZM_W), lambda b, q: (b, 0)),
                  pl.BlockSpec((1, None, past, KV_LORA), lambda b, q: (b, layer, 0, 0)),
                  pl.BlockSpec((1, None, past, LANES), lambda b, q: (b, layer, 0, 0)),
                  _const_spec(rc), _const_spec(rs)] + [_layer_spec(a, layer) for a in weights],
        out_specs=pl.BlockSpec((tq, MLA_HEADS * MLA_V), lambda b, q: (b * (seq_len // tq) + q, 0)),
        out_shape=jax.ShapeDtypeStruct((n, MLA_HEADS * MLA_V), F32),
        scratch_shapes=[
            pltpu.VMEM((tk, MLA_HEADS * QH_W), BF16),
            pltpu.VMEM((tk, MLA_HEADS * MLA_V), BF16),
        ],
        compiler_params=pltpu.CompilerParams(dimension_semantics=("arbitrary", "arbitrary"),
                                             vmem_limit_bytes=VMEM_LIMIT),
        name="mla_lat",
    )(zm, cckv, ckpe, rc, rs, *weights)


SEQ_PER_STEP = 4
CHUNK_GROUP = 4
GDN_PRE_W = 5 * MIX_W


def _for_chunk_groups(n_chunks, fn):
    if n_chunks == CHUNK_GROUP:
        fn(0)
    else:
        def body(gi, carry):
            fn(gi * CHUNK_GROUP)
            return carry
        lax.fori_loop(0, n_chunks // CHUNK_GROUP, body, 0)


def _gdn_prepare(items, maskbd):
    n = range(len(items))
    qs, ks, vs, gs, betas, masks, revs = zip(*items)
    gcs = [_mm_sel_l(masks[i][2], gs[i]) for i in n]
    decays = []
    for i in n:
        inc, eye = masks[i][0], masks[i][3]
        gc_row = jnp.sum(eye * gcs[i], axis=0, keepdims=True)
        decays.append(jnp.where(inc, jnp.exp(jnp.where(inc, gcs[i] - gc_row, 0.0)), 0.0))
    kbs = [ks[i] * betas[i] for i in n]
    aqs = [_mm_nt(jnp.concatenate([kbs[i], qs[i]], axis=0), _bd(ks[i], maskbd)) for i in n]
    a_mats = [jnp.where(masks[i][1], aqs[i][:CHUNK] * decays[i], 0.0) for i in n]
    t_invs = _neumann_inverse(a_mats, [m[3] for m in masks], maskbd)
    egcs = [jnp.exp(gc) for gc in gcs]
    uws = [_mm(t_invs[i], jnp.concatenate([_bd(vs[i] * betas[i], maskbd), _bd(kbs[i] * egcs[i], maskbd)], axis=1))
           for i in n]
    out = []
    for i in n:
        g_last = gcs[i][0:1] if revs[i] else gcs[i][CHUNK - 1:CHUNK]
        pre = jnp.concatenate([uws[i], qs[i] * egcs[i], aqs[i][CHUNK:] * decays[i],
                               ks[i] * jnp.exp(g_last - gcs[i])], axis=1)
        out.append((pre, jnp.broadcast_to(jnp.exp(g_last), (8, MIX_W))))
    return out


def _gdn_step(items, maskbd):
    w = MIX_W
    n = range(len(items))
    pres, egls, states = zip(*items)
    wqs = [_mm(jnp.concatenate([pres[i][:, w:2 * w], pres[i][:, 2 * w:3 * w]], axis=0), states[i]) for i in n]
    v_news = [pres[i][:, :w] - wqs[i][:CHUNK] for i in n]
    outs = [wqs[i][CHUNK:] + _mm(pres[i][:, 3 * w:4 * w], _bd(v_news[i], maskbd)) for i in n]
    upds = [_mm_tn(pres[i][:, 4 * w:], v_news[i]) for i in n]
    mask_f = maskbd.astype(F32)
    return [(outs[i], states[i] * egls[i] + upds[i] * mask_f) for i in n]


def _gdn_kernel(*refs, seq_len, n_seq, cached):
    if cached:
        (zg_ref, zab_ref, s0_ref, conv_ref, alog_ref, dtb_ref, gn_ref, eexp_ref, ones_ref, maskbd_ref,
         o_ref, sout_ref, q_s, k_s, v_s, ge_s, pre_s, gl_s, oacc_s, st_s) = refs
    else:
        zg_ref, zab_ref, conv_ref, alog_ref, dtb_ref, gn_ref, eexp_ref, ones_ref, maskbd_ref = refs[:9]
        o_ref, sout_ref, q_s, k_s, v_s, ge_s, pre_s, gl_s, oacc_s, st_s = refs[-10:]
    t = seq_len * n_seq
    z = zg_ref[:, :GDN_CONV_CH]
    rowi = jnp.bitwise_and(_iota((t, 1), 0), seq_len - 1)
    zp = jnp.where(rowi == 0, 0.0, pltpu.roll(z, 1, 0))
    zn = jnp.where(rowi == seq_len - 1, 0.0, pltpu.roll(z, t - 1, 0))
    cw = conv_ref[...]
    qkv = _silu(zp * cw[0:1] + z * cw[1:2] + zn * cw[2:3])
    ones_bd = ones_ref[...]
    q = qkv[:, :MIX_W]
    k = qkv[:, MIX_W:2 * MIX_W]
    q_s[...] = q * lax.rsqrt(_mm_sel_r(q * q, ones_bd) + 1e-6) * (HEAD_DIM ** -0.5)
    k_s[...] = k * lax.rsqrt(_mm_sel_r(k * k, ones_bd) + 1e-6)
    v_s[...] = qkv[:, 2 * MIX_W:]
    ab = zab_ref[...]
    lane = _iota((t, LANES), 1)
    gb = jnp.where(lane < 2 * HEADS, -jnp.exp(alog_ref[...]) * _softplus(ab + dtb_ref[...]), _sigmoid(ab))
    ge_s[...] = _mm_sel_r(gb, eexp_ref[...])
    oacc_s[...] = jnp.zeros((t, MIX_W), F32)
    if cached:
        st_s[...] = s0_ref[...]
    else:
        st_s[...] = jnp.zeros((n_seq, 2, MIX_W, MIX_W), F32)
    maskbd = maskbd_ref[...]
    masks = (_chunk_masks(False), _chunk_masks(True))
    n_chunks = seq_len // CHUNK

    def prepare_group(c0):
        where, items = [], []
        for j in range(CHUNK_GROUP):
            c = c0 + j
            rows = pl.ds(pl.multiple_of(c * CHUNK, CHUNK), CHUNK)
            for d in range(2):
                where.append((d, c, rows))
                items.append((q_s[rows, :], k_s[rows, :], v_s[rows, :], ge_s[rows, d * MIX_W:(d + 1) * MIX_W],
                              ge_s[rows, (2 + d) * MIX_W:(3 + d) * MIX_W], masks[d], d == 1))
        for (d, c, rows), (pre, egl) in zip(where, _gdn_prepare(items, maskbd)):
            pre_s[d, rows, :] = pre
            gl_s[d, pl.ds(pl.multiple_of(c * 8, 8), 8), :] = egl

    _for_chunk_groups(n_chunks * n_seq, prepare_group)

    def body(i, carry):
        where, items = [], []
        for q in range(n_seq):
            for d in range(2):
                c = q * n_chunks + (i if d == 0 else n_chunks - 1 - i)
                rows = pl.ds(pl.multiple_of(c * CHUNK, CHUNK), CHUNK)
                where.append((q, d, rows))
                items.append((pre_s[d, rows, :], gl_s[d, pl.ds(pl.multiple_of(c * 8, 8), 8), :][0:1], st_s[q, d]))
        for (q, d, rows), (o, s_new) in zip(where, _gdn_step(items, maskbd)):
            oacc_s[rows, :] = oacc_s[rows, :] + o
            st_s[q, d] = s_new
        return carry

    lax.fori_loop(0, n_chunks, body, 0)
    o = oacc_s[...]
    ms = _mm_sel_r(o * o, ones_bd) * (1.0 / HEAD_DIM)
    gate = zg_ref[:, GDN_CONV_CH:]
    o_ref[...] = o * lax.rsqrt(ms + NORM_EPS) * gn_ref[...] * _silu(gate)
    for q in range(n_seq):
        for d in range(2):
            for h in range(HEADS):
                sl = slice(h * HEAD_DIM, (h + 1) * HEAD_DIM)
                sout_ref[q, d, h] = st_s[q, d, sl, sl]


def _state_out(nb, n_seq, layer, cached, prev, args, in_specs):
    if cached:
        return (pl.BlockSpec((n_seq, 2, HEADS, HEAD_DIM, HEAD_DIM), lambda b: (b, 0, 0, 0, 0)),
                jax.ShapeDtypeStruct((nb, 2, HEADS, HEAD_DIM, HEAD_DIM), F32), {})
    aliases = {len(args): 1}
    args.append(prev)
    in_specs.append(pl.BlockSpec(memory_space=pl.ANY))
    return (pl.BlockSpec((n_seq, None, 2, HEADS, HEAD_DIM, HEAD_DIM), lambda b: (b, layer, 0, 0, 0, 0)),
            jax.ShapeDtypeStruct((nb, DEPTH, 2, HEADS, HEAD_DIM, HEAD_DIM), F32), aliases)


def _gdn(zg, zab, lw, layer, consts, seq_len, s0_bd=None, prev=None):
    n = zg.shape[0]
    nb = n // seq_len
    cached = s0_bd is not None
    n_seq = 1 if cached else SEQ_PER_STEP
    rows = seq_len * n_seq
    args = [zg, zab]
    in_specs = [pl.BlockSpec((rows, ZG_W), lambda b: (b, 0)),
                pl.BlockSpec((rows, LANES), lambda b: (b, 0))]
    if cached:
        args.append(s0_bd)
        in_specs.append(pl.BlockSpec((1, None, 2, MIX_W, MIX_W), lambda b: (b, layer, 0, 0, 0)))
    layered = [lw["gdn_conv"], lw["gdn_alog"], lw["gdn_dtb"], lw["gdn_norm"]]
    const = [consts["eexp"], consts["ones_bd"], consts["maskbd"]]
    args += layered + const
    in_specs += [_layer_spec(a, layer) for a in layered] + [_const_spec(a) for a in const]
    s_spec, s_shape, aliases = _state_out(nb, n_seq, layer, cached, prev, args, in_specs)
    return pl.pallas_call(
        functools.partial(_gdn_kernel, seq_len=seq_len, n_seq=n_seq, cached=cached),
        grid=(nb // n_seq,),
        in_specs=in_specs,
        out_specs=[
            pl.BlockSpec((rows, MIX_W), lambda b: (b, 0)),
            s_spec,
        ],
        out_shape=[jax.ShapeDtypeStruct((n, MIX_W), F32), s_shape],
        input_output_aliases=aliases,
        scratch_shapes=[
            pltpu.VMEM((rows, MIX_W), F32),
            pltpu.VMEM((rows, MIX_W), F32),
            pltpu.VMEM((rows, MIX_W), F32),
            pltpu.VMEM((rows, 4 * MIX_W), F32),
            pltpu.VMEM((2, rows, GDN_PRE_W), F32),
            pltpu.VMEM((2, rows // CHUNK * 8, MIX_W), F32),
            pltpu.VMEM((rows, MIX_W), F32),
            pltpu.VMEM((n_seq, 2, MIX_W, MIX_W), F32),
        ],
        compiler_params=pltpu.CompilerParams(dimension_semantics=("arbitrary",),
                                             vmem_limit_bytes=VMEM_LIMIT),
        name="gdn_lat" if cached else "gdn_ctx",
    )(*args)


RWKV_PRE_W = 7 * MIX_W


def _rwkv_prepare(items, maskbd, eye_full):
    n = range(len(items))
    rs, kds, vs, kks, bs, lws, masks, revs = zip(*items)
    cums = [_mm_sel_l(masks[i][2], lws[i]) for i in n]
    einvs = [jnp.exp(-c) for c in cums]
    kts = [kks[i] * jnp.exp(cums[i] - lws[i]) for i in n]
    rts = [rs[i] * jnp.exp(cums[i]) for i in n]
    krs = [jnp.concatenate([kts[i], rts[i]], axis=0) for i in n]
    lb_alls = [_mm_nt(krs[i], _bd(bs[i] * einvs[i], maskbd)) for i in n]
    lk_alls = [_mm_nt(krs[i], _bd(kds[i] * einvs[i], maskbd)) for i in n]
    lbs = [jnp.where(masks[i][1], lb_alls[i][:CHUNK], 0.0) for i in n]
    t_invs = _neumann_inverse(lbs, [m[3] for m in masks], maskbd)
    lvs = [_mm(jnp.concatenate([jnp.where(masks[i][1], lk_alls[i][:CHUNK], 0.0),
                                jnp.where(masks[i][0], lk_alls[i][CHUNK:], 0.0)], axis=0), _bd(vs[i], maskbd))
           for i in n]
    tkps = [_mm(t_invs[i], jnp.concatenate([_bd(kts[i], maskbd), _bd(lvs[i][:CHUNK], maskbd)], axis=1)) for i in n]
    out = []
    for i in n:
        c_last = cums[i][0:1] if revs[i] else cums[i][CHUNK - 1:CHUNK]
        tail = jnp.exp(c_last - cums[i])
        rb = jnp.where(masks[i][0], lb_alls[i][CHUNK:], 0.0)
        pre = jnp.concatenate([tkps[i][:, :MIX_W], rts[i], tkps[i][:, MIX_W:], lvs[i][CHUNK:], rb,
                               kds[i] * tail, bs[i] * tail], axis=1)
        gcol = jnp.sum(eye_full * jnp.exp(c_last), axis=1, keepdims=True)
        out.append((pre, jnp.broadcast_to(gcol, (MIX_W, LANES))))
    return out


def _rwkv_step(items, maskbd):
    w = MIX_W
    n = range(len(items))
    pres, vs, gcols, states = zip(*items)
    prs = [_mm(jnp.concatenate([pres[i][:, :w], pres[i][:, w:2 * w]], axis=0), states[i]) for i in n]
    ps = [prs[i][:CHUNK] + pres[i][:, 2 * w:3 * w] for i in n]
    outs = [prs[i][CHUNK:] + pres[i][:, 3 * w:4 * w] - _mm(pres[i][:, 4 * w:5 * w], _bd(ps[i], maskbd)) for i in n]
    upds = [_mm_tn(jnp.concatenate([pres[i][:, 5 * w:6 * w], pres[i][:, 6 * w:]], axis=0),
                   jnp.concatenate([vs[i], -ps[i]], axis=0)) for i in n]
    mask_f = maskbd.astype(F32)
    return [(outs[i], states[i] * jnp.concatenate([gcols[i], gcols[i]], axis=1) + upds[i] * mask_f) for i in n]


def _rwkv_kernel(*refs, seq_len, n_seq, cached):
    if cached:
        (zr_ref, s0_ref, mup_ref, mun_ref, w0_ref, w2_ref, a0_ref, a2_ref, g2_ref, kk_ref, ka_ref, rk_ref,
         gnw_ref, gnb_ref, ones_ref, maskbd_ref, o_ref, sout_ref,
         r_s, v_s, kk_s, dir_s, bg_s, pre_s, gcol_s, oacc_s, st_s) = refs
    else:
        (zr_ref, mup_ref, mun_ref, w0_ref, w2_ref, a0_ref, a2_ref, g2_ref, kk_ref, ka_ref, rk_ref,
         gnw_ref, gnb_ref, ones_ref, maskbd_ref) = refs[:15]
        o_ref, sout_ref, r_s, v_s, kk_s, dir_s, bg_s, pre_s, gcol_s, oacc_s, st_s = refs[-11:]
    t = seq_len * n_seq
    z = zr_ref[...]
    rowi = jnp.bitwise_and(_iota((t, 1), 0), seq_len - 1)
    zp = jnp.where(rowi == 0, 0.0, pltpu.roll(z, 1, 0))
    zn = jnp.where(rowi == seq_len - 1, 0.0, pltpu.roll(z, t - 1, 0))
    z = z + mup_ref[...] * (zp - z) + mun_ref[...] * (zn - z)
    w = MIX_W
    r = z[:, :w]
    k = z[:, w:2 * w]
    v = z[:, 2 * w:3 * w]
    wd = jnp.tanh(z[:, 3 * w:3 * w + LANES])
    ad = z[:, 3 * w + LANES:3 * w + 2 * LANES]
    gd = _sigmoid(z[:, 3 * w + 2 * LANES:])
    ones_bd = ones_ref[...]
    kk = k * kk_ref[...]
    kk = kk * lax.rsqrt(_mm_sel_r(kk * kk, ones_bd) + 1e-6)
    r_s[...] = r
    v_s[...] = v
    kk_s[...] = kk
    bonus = jnp.zeros((t, w), F32)
    for d in range(2):
        w_log = -_softplus(-(w0_ref[d:d + 1] + _mm(wd, w2_ref[d]))) - 0.5
        a = _sigmoid(a0_ref[d:d + 1] + _mm(ad, a2_ref[d]))
        kd = k * (1.0 + (a - 1.0) * ka_ref[...])
        dir_s[:, (3 * d) * w:(3 * d + 1) * w] = -jnp.exp(w_log)
        dir_s[:, (3 * d + 1) * w:(3 * d + 2) * w] = kd
        dir_s[:, (3 * d + 2) * w:(3 * d + 3) * w] = kk * a
        bonus = bonus + _mm_sel_r(r * kd * rk_ref[...], ones_bd) * v
    bg_s[:, :w] = bonus
    bg_s[:, w:] = _mm(gd, g2_ref[...])
    oacc_s[...] = jnp.zeros((t, w), F32)
    if cached:
        st_s[...] = s0_ref[...]
    else:
        st_s[...] = jnp.zeros((n_seq, 2, w, w), F32)
    maskbd = maskbd_ref[...]
    masks = (_chunk_masks(False), _chunk_masks(True))
    eye_full = jnp.where(_iota((w, w), 0) == _iota((w, w), 1), 1.0, 0.0).astype(F32)
    n_chunks = seq_len // CHUNK

    def prepare_group(c0):
        where, items = [], []
        for j in range(CHUNK_GROUP):
            c = c0 + j
            rows = pl.ds(pl.multiple_of(c * CHUNK, CHUNK), CHUNK)
            for d in range(2):
                where.append((d, c, rows))
                items.append((r_s[rows, :], dir_s[rows, (3 * d + 1) * w:(3 * d + 2) * w], v_s[rows, :], kk_s[rows, :],
                              dir_s[rows, (3 * d + 2) * w:(3 * d + 3) * w], dir_s[rows, (3 * d) * w:(3 * d + 1) * w],
                              masks[d], d == 1))
        for (d, c, rows), (pre, gcol) in zip(where, _rwkv_prepare(items, maskbd, eye_full)):
            pre_s[d, rows, :] = pre
            gcol_s[d, pl.ds(pl.multiple_of(c * w, w), w), :] = gcol

    _for_chunk_groups(n_chunks * n_seq, prepare_group)

    def body(i, carry):
        where, items = [], []
        for q in range(n_seq):
            for d in range(2):
                c = q * n_chunks + (i if d == 0 else n_chunks - 1 - i)
                rows = pl.ds(pl.multiple_of(c * CHUNK, CHUNK), CHUNK)
                where.append((q, d, rows))
                items.append((pre_s[d, rows, :], v_s[rows, :], gcol_s[d, pl.ds(pl.multiple_of(c * w, w), w), :],
                              st_s[q, d]))
        for (q, d, rows), (o, z_new) in zip(where, _rwkv_step(items, maskbd)):
            oacc_s[rows, :] = oacc_s[rows, :] + o
            st_s[q, d] = z_new
        return carry

    lax.fori_loop(0, n_chunks, body, 0)
    o = oacc_s[...]
    inv_n = 1.0 / HEAD_DIM
    mu = _mm_sel_r(o, ones_bd) * inv_n
    oc = o - mu
    var = _mm_sel_r(oc * oc, ones_bd) * inv_n
    y = oc * lax.rsqrt(var + RWKV_GN_EPS) * gnw_ref[...] + gnb_ref[...]
    o_ref[...] = (y + bg_s[:, :w]) * bg_s[:, w:]
    for q in range(n_seq):
        for d in range(2):
            state = st_s[q, d] if cached else st_s[q, d].T
            for h in range(HEADS):
                sl = slice(h * HEAD_DIM, (h + 1) * HEAD_DIM)
                sout_ref[q, d, h] = state[sl, sl]


def _rwkv(zr, lw, layer, consts, seq_len, s0_bd=None, prev=None):
    n = zr.shape[0]
    nb = n // seq_len
    cached = s0_bd is not None
    n_seq = 1 if cached else SEQ_PER_STEP
    rows = seq_len * n_seq
    args = [zr]
    in_specs = [pl.BlockSpec((rows, ZR_W), lambda b: (b, 0))]
    if cached:
        args.append(s0_bd)
        in_specs.append(pl.BlockSpec((1, None, 2, MIX_W, MIX_W), lambda b: (b, layer, 0, 0, 0)))
    layered = [lw["rwkv_mu_prev"], lw["rwkv_mu_next"], lw["rwkv_w0"], lw["rwkv_w2"], lw["rwkv_a0"], lw["rwkv_a2"],
               lw["rwkv_g2"], lw["rwkv_k_k"], lw["rwkv_k_a"], lw["rwkv_r_k"], lw["rwkv_gn_w"], lw["rwkv_gn_b"]]
    const = [consts["ones_bd"], consts["maskbd"]]
    args += layered + const
    in_specs += [_layer_spec(a, layer) for a in layered] + [_const_spec(a) for a in const]
    s_spec, s_shape, aliases = _state_out(nb, n_seq, layer, cached, prev, args, in_specs)
    return pl.pallas_call(
        functools.partial(_rwkv_kernel, seq_len=seq_len, n_seq=n_seq, cached=cached),
        grid=(nb // n_seq,),
        in_specs=in_specs,
        out_specs=[
            pl.BlockSpec((rows, MIX_W), lambda b: (b, 0)),
            s_spec,
        ],
        out_shape=[jax.ShapeDtypeStruct((n, MIX_W), F32), s_shape],
        input_output_aliases=aliases,
        scratch_shapes=[
            pltpu.VMEM((rows, MIX_W), F32),
            pltpu.VMEM((rows, MIX_W), F32),
            pltpu.VMEM((rows, MIX_W), F32),
            pltpu.VMEM((rows, 6 * MIX_W), F32),
            pltpu.VMEM((rows, 2 * MIX_W), F32),
            pltpu.VMEM((2, rows, RWKV_PRE_W), F32),
            pltpu.VMEM((2, rows // CHUNK * MIX_W, LANES), F32),
            pltpu.VMEM((rows, MIX_W), F32),
            pltpu.VMEM((n_seq, 2, MIX_W, MIX_W), F32),
        ],
        compiler_params=pltpu.CompilerParams(dimension_semantics=("arbitrary",),
                                             vmem_limit_bytes=VMEM_LIMIT),
        name="rwkv_lat" if cached else "rwkv_ctx",
    )(*args)


def _route(logits_t, bias):
    tm = logits_t.shape[1]
    neg = -jnp.inf
    sc = _sigmoid(logits_t)
    sc3 = sc.reshape(N_GROUPS, GROUP_SIZE, tm)
    sel = (sc + bias).reshape(N_GROUPS, GROUP_SIZE, tm)
    si = _iota(sel.shape, 1).astype(F32)
    m1 = jnp.max(sel, axis=1, keepdims=True)
    f1 = jnp.min(jnp.where(sel == m1, si, float(GROUP_SIZE)), axis=1, keepdims=True)
    m2 = jnp.max(jnp.where(si == f1, neg, sel), axis=1, keepdims=True)
    grp = m1 + m2
    gi = _iota(grp.shape, 0).astype(F32)
    gsel = jnp.zeros(grp.shape, F32)
    for _ in range(TOPK_GROUPS):
        mx = jnp.max(grp, axis=0, keepdims=True)
        fi = jnp.min(jnp.where(grp == mx, gi, float(N_GROUPS)), axis=0, keepdims=True)
        hit = gi == fi
        gsel = jnp.where(hit, 1.0, gsel)
        grp = jnp.where(hit, neg, grp)
    cur = jnp.where(gsel > 0.0, sel, neg)
    ei = (_iota(cur.shape, 0) * GROUP_SIZE + _iota(cur.shape, 1)).astype(F32)
    chosen = jnp.zeros(cur.shape, F32)
    ids, wts = [], []
    for _ in range(TOP_K):
        mx = jnp.max(jnp.max(cur, axis=0, keepdims=True), axis=1, keepdims=True)
        fi = jnp.min(jnp.min(jnp.where(cur == mx, ei, float(N_EXPERTS)), axis=0, keepdims=True),
                     axis=1, keepdims=True)
        hit = ei == fi
        chosen = jnp.where(hit, 1.0, chosen)
        cur = jnp.where(hit, neg, cur)
        ids.append(fi.reshape(1, tm))
        wts.append(jnp.sum(jnp.sum(jnp.where(hit, sc3, 0.0), axis=0, keepdims=True), axis=1, keepdims=True)
                   .reshape(1, tm))
    w = jnp.concatenate(wts, axis=0)
    w = w / jnp.sum(w, axis=0, keepdims=True) * ROUTE_SCALE
    return chosen.reshape(N_EXPERTS, tm), jnp.concatenate(ids, axis=0), w


def _pack_halves(x):
    half = x.shape[1] // 2
    bits = lax.bitcast_convert_type(x.astype(BF16).astype(F32), jnp.int32)
    lo = lax.shift_right_logical(bits[:, :half], jnp.int32(16))
    return jnp.bitwise_or(lo, jnp.bitwise_and(bits[:, half:], jnp.int32(-65536)))


def _unpack_halves(word):
    lo = lax.bitcast_convert_type(lax.shift_left(word, jnp.int32(16)), F32)
    hi = lax.bitcast_convert_type(jnp.bitwise_and(word, jnp.int32(-65536)), F32)
    return lo, hi


def _post_kernel(x_ref, om_ref, og_ref, or_ref, mod_ref, wo_ref, n2_ref, rt_ref, rb_ref, sgu_ref, sdn_ref,
                 tri_ref, cin_ref, x1_ref, h2_ref, eid_ref, rank_ref, ew_ref, cnt_ref, carry_s):
    @pl.when(pl.program_id(0) == 0)
    def _():
        carry_s[...] = cin_ref[...]

    m = mod_ref[0]
    g1 = m[:, 2 * D_MODEL:3 * D_MODEL]
    sh2 = m[:, 3 * D_MODEL:4 * D_MODEL]
    sc2 = m[:, 4 * D_MODEL:5 * D_MODEL]
    w_mla = MLA_HEADS * MLA_V
    mix = (_mm(om_ref[...], wo_ref[0:w_mla, :]) + _mm(og_ref[...], wo_ref[w_mla:w_mla + MIX_W, :])
           + _mm(or_ref[...], wo_ref[w_mla + MIX_W:, :]))
    x1 = x_ref[...] + g1 * mix
    h2 = _rms(x1, n2_ref[...]) * (1.0 + sc2) + sh2
    h2_ref[...] = _pack_halves(h2)
    g2 = m[:, 5 * D_MODEL:]
    x1_ref[...] = x1 + g2 * _mm(_swiglu_act(_mm(h2, sgu_ref[...])), sdn_ref[...])
    r_hi, r_lo = _split2(rt_ref[...])
    h_hi, h_lo = _split2(h2)
    logits_t = _mm_nt(r_hi, h_hi) + _mm_nt(r_hi, h_lo) + _mm_nt(r_lo, h_hi)
    chosen, ids, w = _route(logits_t, rb_ref[...])
    tm = chosen.shape[1]
    rank_et = (carry_s[:, 0:1] + _mm(chosen, tri_ref[...])).reshape(N_GROUPS, GROUP_SIZE, tm)
    ei = (_iota(rank_et.shape, 0) * GROUP_SIZE + _iota(rank_et.shape, 1)).astype(F32)
    ranks = []
    for k in range(TOP_K):
        pick = jnp.where(ei == ids[k:k + 1].reshape(1, 1, tm), rank_et, 0.0)
        ranks.append(jnp.sum(jnp.sum(pick, axis=0, keepdims=True), axis=1, keepdims=True).reshape(1, tm))
    eid_ref[...] = ids.astype(jnp.int32)
    rank_ref[...] = jnp.concatenate(ranks, axis=0).astype(jnp.int32)
    ew_ref[...] = jnp.concatenate([w, jnp.zeros((LANES - TOP_K, tm), F32)], axis=0).T
    total = carry_s[...] + jnp.sum(chosen, axis=1, keepdims=True)
    carry_s[...] = total
    cnt_ref[...] = total


def _post(x2d, om, og, orw, mods, lw, layer, consts, tm, seq_len, mod_base, counts_in):
    n = x2d.shape[0]
    tiles_per_seq = seq_len // tm if mod_base else 1

    def mod_idx(i):
        return (layer, mod_base + i // tiles_per_seq if mod_base else 0, 0, 0)

    row = lambda w: pl.BlockSpec((tm, w), lambda i: (i, 0))
    col = lambda h: pl.BlockSpec((h, tm), lambda i: (0, i))
    full = lambda a: _layer_spec(a, layer)
    tail = [lw["w_out"], lw["norm2"], lw["router_t"], lw["router_b"], lw["shared_w_gu"], lw["shared_w_down"]]
    tri = consts["tri_tokens"]
    return pl.pallas_call(
        _post_kernel,
        grid=(n // tm,),
        in_specs=[row(D_MODEL), row(MLA_HEADS * MLA_V), row(MIX_W), row(MIX_W),
                  pl.BlockSpec((None, 1, 1, 6 * D_MODEL), mod_idx)] + [full(a) for a in tail]
        + [_const_spec(tri), _const_spec(counts_in)],
        out_specs=[row(D_MODEL), row(D_MODEL // 2), col(TOP_K), col(TOP_K), row(LANES),
                   pl.BlockSpec((N_EXPERTS, LANES), lambda i: (0, 0))],
        out_shape=[
            jax.ShapeDtypeStruct((n, D_MODEL), F32),
            jax.ShapeDtypeStruct((n, D_MODEL // 2), jnp.int32),
            jax.ShapeDtypeStruct((TOP_K, n), jnp.int32),
            jax.ShapeDtypeStruct((TOP_K, n), jnp.int32),
            jax.ShapeDtypeStruct((n, LANES), F32),
            jax.ShapeDtypeStruct((N_EXPERTS, LANES), F32),
        ],
        scratch_shapes=[pltpu.VMEM((N_EXPERTS, LANES), F32)],
        compiler_params=pltpu.CompilerParams(dimension_semantics=("arbitrary",),
                                             vmem_limit_bytes=VMEM_LIMIT),
        name="post",
    )(x2d, om, og, orw, mods, *tail, tri, counts_in)


MOE_ROWS = 1024
SC_ROWS = 128
SC_SUBCORES = 32


def _swiglu_act(gu):
    return _silu(gu[:, :D_EXPERT]) * gu[:, D_EXPERT:]


def _dispatch_plan(eid, rank, counts, n, rows):
    n_blocks = n * TOP_K // rows + N_EXPERTS
    cnt = counts[:, 0].astype(jnp.int32)
    blocks = (cnt + rows - 1) // rows
    block_end = jnp.cumsum(blocks)
    offset = (block_end - blocks) * rows
    experts = jnp.arange(N_EXPERTS, dtype=jnp.int32)
    dest = jnp.sum(jnp.where(eid[..., None] == experts, offset, 0), axis=-1) + rank
    block_ids = jnp.arange(n_blocks, dtype=jnp.int32)
    block_expert = jnp.minimum(jnp.sum((block_end[None, :] <= block_ids[:, None]).astype(jnp.int32), axis=1),
                               N_EXPERTS - 1)
    n_used = block_end[-1:]
    prev_expert = jnp.concatenate([jnp.full((1,), -1, jnp.int32), block_expert[:-1]])
    first = ((block_ids < n_used[0]) & (block_expert != prev_expert)).astype(jnp.int32)
    slot = jnp.bitwise_and(jnp.cumsum(first) - 1, 1)
    owner_or_none = jnp.where(blocks > 0, experts, N_EXPERTS)
    next_owner = jnp.concatenate([lax.cummin(owner_or_none[::-1])[::-1][1:], jnp.full((1,), N_EXPERTS, jnp.int32)])
    nxt = jnp.sum(jnp.where(block_expert[:, None] == experts, next_owner, 0), axis=1)
    return dest, (block_expert, first, slot.astype(jnp.int32), nxt.astype(jnp.int32), n_used.astype(jnp.int32)), n_blocks


def _sc_mesh():
    return plsc.VectorSubcoreMesh(core_axis_name="core", subcore_axis_name="subcore")


def _sc_dispatch(groups, dest, n_rows):
    w = groups[0].shape[1]
    dtype = groups[0].dtype

    @functools.partial(pl.kernel, out_type=jax.ShapeDtypeStruct((n_rows, w), dtype), mesh=_sc_mesh(),
                       scratch_types=[pltpu.VMEM((SC_ROWS, w), dtype), pltpu.VMEM((TOP_K, SC_ROWS), jnp.int32)])
    def kern(*refs):
        x_refs, d_hbm, o_hbm, xv, dv = refs[:len(groups)], *refs[len(groups):]
        sid = lax.axis_index("core") * (SC_SUBCORES // 2) + lax.axis_index("subcore")
        start = 0
        for x_hbm, x in zip(x_refs, groups):
            def chunk(c, x_hbm=x_hbm, start=start):
                r0 = pl.multiple_of(c * SC_ROWS, SC_ROWS)
                pltpu.sync_copy(x_hbm.at[pl.ds(r0, SC_ROWS)], xv)
                pltpu.sync_copy(d_hbm.at[:, pl.ds(start + r0, SC_ROWS)], dv)
                for k in range(TOP_K):
                    pltpu.sync_copy(xv, o_hbm.at[dv.at[k]])

            pl.loop(sid, x.shape[0] // SC_ROWS, step=SC_SUBCORES)(chunk)
            start += x.shape[0]

    return kern(*groups, dest)


def _sc_gather(y, idx):
    w = y.shape[1]
    n_chunks = idx.shape[0]

    @functools.partial(pl.kernel, out_type=jax.ShapeDtypeStruct((n_chunks * SC_ROWS, w), y.dtype), mesh=_sc_mesh(),
                       scratch_types=[pltpu.VMEM((SC_ROWS, w), y.dtype), pltpu.VMEM((1, SC_ROWS), jnp.int32)])
    def kern(y_hbm, i_hbm, o_hbm, ov, iv):
        sid = lax.axis_index("core") * (SC_SUBCORES // 2) + lax.axis_index("subcore")

        @pl.loop(sid, n_chunks, step=SC_SUBCORES)
        def _(c):
            pltpu.sync_copy(i_hbm.at[pl.ds(c, 1)], iv)
            pltpu.sync_copy(y_hbm.at[iv.at[0]], ov)
            pltpu.sync_copy(ov, o_hbm.at[pl.ds(pl.multiple_of(c * SC_ROWS, SC_ROWS), SC_ROWS)])

    return kern(y, idx)


def _moe_rows_kernel(be_ref, first_ref, slot_ref, nxt_ref, nu_ref, x_ref, wgu_hbm, wdn_hbm, y_ref,
                     wgu_f, wdn_f, wgu_b, wdn_b, sem, *, layer):
    b = pl.program_id(0)

    def weight_copies(expert, slot):
        return (pltpu.make_async_copy(wgu_hbm.at[layer, expert], wgu_f.at[slot], sem.at[slot, 0]),
                pltpu.make_async_copy(wdn_hbm.at[layer, expert], wdn_f.at[slot], sem.at[slot, 1]))

    @pl.when(b == 0)
    def _():
        for copy in weight_copies(be_ref[0], 0):
            copy.start()

    @pl.when(first_ref[b] == 1)
    def _():
        slot = slot_ref[b]
        for copy in weight_copies(be_ref[b], slot):
            copy.wait()

        @pl.when(nxt_ref[b] < N_EXPERTS)
        def _():
            for copy in weight_copies(nxt_ref[b], 1 - slot):
                copy.start()

        wgu_b[...] = wgu_f[slot].astype(BF16)
        wdn_b[...] = wdn_f[slot].astype(BF16)

    @pl.when(b < nu_ref[0])
    def _():
        half = D_MODEL // 2
        lo, hi = _unpack_halves(x_ref[...])
        gu = _mm(lo, wgu_b[0:half, :]) + _mm(hi, wgu_b[half:, :])
        y_ref[...] = _pack_halves(_mm(_swiglu_act(gu), wdn_b[...]))


def _moe_rows(xs, schedule, lw, layer, n_blocks, rows):
    half = D_MODEL // 2
    last = lambda b, be, first, slot, nxt, nu: jnp.minimum(b, nu[0] - 1)
    row_spec = pl.BlockSpec((rows, half), lambda *a: (last(*a), 0))
    return pl.pallas_call(
        functools.partial(_moe_rows_kernel, layer=layer),
        grid_spec=pltpu.PrefetchScalarGridSpec(
            num_scalar_prefetch=5,
            grid=(n_blocks,),
            in_specs=[row_spec, pl.BlockSpec(memory_space=pl.ANY), pl.BlockSpec(memory_space=pl.ANY)],
            out_specs=row_spec,
            scratch_shapes=[
                pltpu.VMEM((2, D_MODEL, 2 * D_EXPERT), F32),
                pltpu.VMEM((2, D_EXPERT, D_MODEL), F32),
                pltpu.VMEM((D_MODEL, 2 * D_EXPERT), BF16),
                pltpu.VMEM((D_EXPERT, D_MODEL), BF16),
                pltpu.SemaphoreType.DMA((2, 2)),
            ],
        ),
        out_shape=jax.ShapeDtypeStruct(xs.shape, jnp.int32),
        compiler_params=pltpu.CompilerParams(dimension_semantics=("arbitrary",),
                                             vmem_limit_bytes=VMEM_LIMIT),
        name="moe_rows",
    )(*schedule, xs, lw["moe_w_gu"], lw["moe_w_down"])


def _moe_combine_kernel(yg_ref, ew_ref, x1_ref, mod_ref, nf_ref, o_ref, *, final):
    ew = ew_ref[...]
    acc_lo = acc_hi = None
    for k in range(TOP_K):
        lo, hi = _unpack_halves(yg_ref[k])
        wk = ew[:, k:k + 1]
        acc_lo = wk * lo if acc_lo is None else acc_lo + wk * lo
        acc_hi = wk * hi if acc_hi is None else acc_hi + wk * hi
    g2 = mod_ref[0][:, 5 * D_MODEL:]
    x2 = x1_ref[...] + g2 * jnp.concatenate([acc_lo, acc_hi], axis=1)
    if final:
        x2 = _rms(x2, nf_ref[...])
    o_ref[...] = x2


def _moe_combine(yg, row0, ew, x1, mods, layer, norm_f, tm, seq_len, mod_base, final):
    n = x1.shape[0]
    half = D_MODEL // 2
    tiles_per_seq = seq_len // tm if mod_base else 1
    tile0 = row0 // tm

    def mod_idx(i):
        return (layer, mod_base + i // tiles_per_seq if mod_base else 0, 0, 0)

    row = lambda w: pl.BlockSpec((tm, w), lambda i: (i, 0))
    return pl.pallas_call(
        functools.partial(_moe_combine_kernel, final=final),
        grid=(n // tm,),
        in_specs=[pl.BlockSpec((TOP_K, tm, half), lambda i: (0, tile0 + i, 0)), row(LANES), row(D_MODEL),
                  pl.BlockSpec((None, 1, 1, 6 * D_MODEL), mod_idx), _const_spec(norm_f)],
        out_specs=row(D_MODEL),
        out_shape=jax.ShapeDtypeStruct((n, D_MODEL), F32),
        compiler_params=pltpu.CompilerParams(dimension_semantics=("arbitrary",),
                                             vmem_limit_bytes=VMEM_LIMIT),
        name="moe_combine_final" if final else "moe_combine",
    )(yg, ew, x1, mods, norm_f)


def _moe_experts(groups, eid, rank, counts, lw, layer):
    group_sizes = [g.shape[0] for g in groups]
    n = sum(group_sizes)
    dest, schedule, n_blocks = _dispatch_plan(eid, rank, counts, n, MOE_ROWS)
    xs = _sc_dispatch(groups, dest, n_blocks * MOE_ROWS)
    y = _moe_rows(xs, schedule, lw, layer, n_blocks, MOE_ROWS)
    outs, start = [], 0
    for size in group_sizes:
        idx = dest[:, start:start + size].reshape(size * TOP_K // SC_ROWS, SC_ROWS)
        outs.append(_sc_gather(y, idx).reshape(TOP_K, size, D_MODEL // 2))
        start += size
    return outs


def _constants():
    idx = np.arange(MIX_W)
    same_head = (idx[:, None] // HEAD_DIM) == (idx[None, :] // HEAD_DIM)
    maskbd = jnp.asarray(same_head, BF16)
    eexp = np.zeros((LANES, 4 * MIX_W), np.float32)
    for blk in range(4):
        kind, d = divmod(blk, 2)
        for h in range(HEADS):
            src = kind * 2 * HEADS + d * HEADS + h
            eexp[src, blk * MIX_W + h * HEAD_DIM: blk * MIX_W + (h + 1) * HEAD_DIM] = 1.0
    tri = np.triu(np.ones((POST_TM, POST_TM), np.float32), 1)
    return {"maskbd": maskbd, "ones_bd": maskbd, "eexp": jnp.asarray(eexp, BF16), "tri_tokens": jnp.asarray(tri, BF16)}


def _rope_tables(n):
    rows = n // GRID_W
    row = jnp.repeat(jnp.arange(rows, dtype=F32), GRID_W)
    col = jnp.tile(jnp.arange(GRID_W, dtype=F32), rows)
    axis_dim = MLA_ROPE // 2
    inv = jnp.power(ROPE_BASE, -jnp.arange(0, axis_dim, 2, dtype=F32) / axis_dim)
    ang_r = row[:, None] * inv
    ang_c = col[:, None] * inv
    cr, sr, cc, sc = jnp.cos(ang_r), jnp.sin(ang_r), jnp.cos(ang_c), jnp.sin(ang_c)
    zeros = jnp.zeros((n, LANES - MLA_ROPE), F32)
    cos_t = jnp.concatenate([cr, cr, cc, cc, zeros], axis=1)
    sin_t = jnp.concatenate([-sr, sr, -sc, sc, zeros], axis=1)
    return cos_t, sin_t


def _stacked_weights(p):
    w_uq = p["mla_w_uq"].reshape(DEPTH, Q_LORA, MLA_HEADS, MLA_NOPE + MLA_ROPE)
    zq = jnp.zeros((DEPTH, Q_LORA, MLA_HEADS, QH_W - MLA_NOPE - MLA_ROPE), F32)
    w_uq_a = jnp.concatenate([w_uq, zq], axis=-1).reshape(DEPTH, Q_LORA, MLA_HEADS * QH_W).astype(BF16)
    w_uq_sw = jnp.concatenate([jnp.zeros((DEPTH, Q_LORA, MLA_HEADS, MLA_NOPE), F32),
                               w_uq[..., MLA_NOPE + _ROPE_SWAP], zq], axis=-1)
    w_uq_sw = w_uq_sw.reshape(DEPTH, Q_LORA, MLA_HEADS * QH_W).astype(BF16)

    def per_direction(w):
        half = jnp.zeros((DEPTH, 64, MIX_W), F32)
        return jnp.stack([jnp.concatenate([w[:, 0], half], axis=1),
                          jnp.concatenate([half, w[:, 1]], axis=1)], axis=1).astype(BF16)

    row = lambda v: v.reshape(DEPTH, 1, -1)
    pad_row = lambda v: jnp.pad(row(v), ((0, 0), (0, 0), (0, LANES - 2 * HEADS)))
    return {
        "norm1": row(p["norm1"]),
        "w_in": p["w_in"],
        "q_norm": row(p["mla_q_norm"]),
        "w_uq": w_uq_a, "w_uq_sw": w_uq_sw,
        "kv_norm": row(p["mla_kv_norm"]),
        "w_ukv": p["mla_w_ukv"].astype(BF16),
        "gdn_conv": p["gdn_conv"],
        "gdn_alog": pad_row(p["gdn_a_log"]),
        "gdn_dtb": pad_row(p["gdn_dt_bias"]),
        "gdn_norm": jnp.tile(row(p["gdn_norm"]), (1, 1, HEADS)),
        "rwkv_mu_prev": row(p["rwkv_mu_prev"]),
        "rwkv_mu_next": row(p["rwkv_mu_next"]),
        "rwkv_w0": p["rwkv_w0"],
        "rwkv_w2": per_direction(p["rwkv_w2"]),
        "rwkv_a0": p["rwkv_a0"],
        "rwkv_a2": per_direction(p["rwkv_a2"]),
        "rwkv_g2": p["rwkv_g2"].astype(BF16),
        "rwkv_k_k": row(p["rwkv_k_k"]),
        "rwkv_k_a": row(p["rwkv_k_a"]),
        "rwkv_r_k": row(p["rwkv_r_k"]),
        "rwkv_gn_w": row(p["rwkv_gn_w"]),
        "rwkv_gn_b": row(p["rwkv_gn_b"]),
        "w_out": p["w_out"].astype(BF16),
        "norm2": row(p["norm2"]),
        "router_t": jnp.swapaxes(p["moe_router"], 1, 2),
        "router_b": p["moe_bias"].reshape(DEPTH, N_EXPERTS, 1),
        "moe_w_gu": p["moe_w_gu"],
        "moe_w_down": p["moe_w_down"],
        "shared_w_gu": p["shared_w_gu"].astype(BF16),
        "shared_w_down": p["shared_w_down"].astype(BF16),
    }


def _embed_block_diag(s):
    b = s.shape[0]
    eye = jnp.eye(HEADS, dtype=s.dtype)
    out = jnp.einsum("bdhkv,hg->bdhkgv", s, eye)
    return out.reshape(b, 2, MIX_W, MIX_W)


def _layer_front(x2d, mods, lw, l, consts, seq_len, mod_base, cache, tm, tq, counts_in, prev=None):
    zm, zg, zab, zr = _inproj(x2d, mods, lw, l, tm, seq_len, mod_base)
    if cache is None:
        o_mla, ckv, kpe = _mla_ctx(zm, lw, l, seq_len, prev[:2])
        o_gdn, s_gdn = _gdn(zg, zab, lw, l, consts, seq_len, prev=prev[2])
        o_rwkv, s_rwkv = _rwkv(zr, lw, l, consts, seq_len, prev=prev[3])
        new = (ckv, kpe, s_gdn, s_rwkv)
    else:
        cckv, ckpe, rc, rs, sg, sr = cache
        o_mla = _mla_lat(zm, lw, l, seq_len, tq, (cckv, ckpe, rc, rs))
        o_gdn, _ = _gdn(zg, zab, lw, l, consts, seq_len, sg)
        o_rwkv, _ = _rwkv(zr, lw, l, consts, seq_len, sr)
        new = None
    routed = _post(x2d, o_mla, o_gdn, o_rwkv, mods, lw, l, consts, tm, seq_len, mod_base, counts_in)
    return routed, new


def kernel(x_prompt, x_sample, cache_mla_ckv, cache_mla_kpe, state_gdn, state_rwkv, c, c_ctx, ada_w, ada_b, norm1, w_in, mla_q_norm, mla_w_uq, mla_kv_norm, mla_w_ukv, gdn_conv, gdn_a_log, gdn_dt_bias, gdn_norm, rwkv_mu_prev, rwkv_mu_next, rwkv_w0, rwkv_w2, rwkv_a0, rwkv_a2, rwkv_g2, rwkv_k_k, rwkv_k_a, rwkv_r_k, rwkv_gn_w, rwkv_gn_b, w_out, norm2, moe_router, moe_bias, moe_w_gu, moe_w_down, shared_w_gu, shared_w_down, norm_f):
    p = dict(norm1=norm1, w_in=w_in, mla_q_norm=mla_q_norm, mla_w_uq=mla_w_uq, mla_kv_norm=mla_kv_norm,
             mla_w_ukv=mla_w_ukv, gdn_conv=gdn_conv, gdn_a_log=gdn_a_log, gdn_dt_bias=gdn_dt_bias,
             gdn_norm=gdn_norm, rwkv_mu_prev=rwkv_mu_prev, rwkv_mu_next=rwkv_mu_next, rwkv_w0=rwkv_w0,
             rwkv_w2=rwkv_w2, rwkv_a0=rwkv_a0, rwkv_a2=rwkv_a2, rwkv_g2=rwkv_g2, rwkv_k_k=rwkv_k_k,
             rwkv_k_a=rwkv_k_a, rwkv_r_k=rwkv_r_k, rwkv_gn_w=rwkv_gn_w, rwkv_gn_b=rwkv_gn_b, w_out=w_out,
             norm2=norm2, moe_router=moe_router, moe_bias=moe_bias, moe_w_gu=moe_w_gu, moe_w_down=moe_w_down,
             shared_w_gu=shared_w_gu, shared_w_down=shared_w_down)
    weights = _stacked_weights(p)
    consts = _constants()
    nf = norm_f.reshape(1, D_MODEL)
    b_ctx, t_ctx, _ = x_prompt.shape
    b_lat, t_lat, _ = x_sample.shape

    cvec8 = jnp.concatenate([c_ctx[None, :], c, jnp.zeros((8 - 1 - b_lat, D_MODEL), F32)], axis=0)
    mods = _adaln(cvec8, ada_w, ada_b)
    mods = mods.reshape(DEPTH, 8, 1, 6 * D_MODEL)

    rc, rs = _rope_tables(t_lat)
    ckpe = jnp.pad(cache_mla_kpe, ((0, 0), (0, 0), (0, 0), (0, LANES - MLA_ROPE)))
    cache = (cache_mla_ckv, ckpe, rc, rs, _embed_block_diag_layers(state_gdn),
             _embed_block_diag_layers(jnp.swapaxes(state_rwkv, -1, -2)))
    xp = x_prompt.reshape(b_ctx * t_ctx, D_MODEL)
    xs = x_sample.reshape(b_lat * t_lat, D_MODEL)
    n_ctx = xp.shape[0]
    tm = POST_TM
    state_shape = (b_ctx, DEPTH, 2, HEADS, HEAD_DIM, HEAD_DIM)
    ctx_outs = (jnp.zeros((b_ctx, DEPTH, t_ctx, KV_LORA), F32), jnp.zeros((b_ctx, DEPTH, t_ctx, MLA_ROPE), F32),
                jnp.zeros(state_shape, F32), jnp.zeros(state_shape, F32))
    for l in range(DEPTH):
        final = l == DEPTH - 1
        no_pairs = jnp.zeros((N_EXPERTS, LANES), F32)
        (x1c, hc, eidc, rankc, ewc, cnt_c), ctx_outs = _layer_front(xp, mods, weights, l, consts, t_ctx, 0, None,
                                                                    tm, t_ctx, no_pairs, ctx_outs)
        (x1s, hs, eids, ranks, ews, cnt), _ = _layer_front(xs, mods, weights, l, consts, t_lat, 1, cache,
                                                           tm, 256, cnt_c)
        yg_c, yg_s = _moe_experts([hc, hs], jnp.concatenate([eidc, eids], axis=1),
                                  jnp.concatenate([rankc, ranks], axis=1), cnt, weights, l)
        xp = _moe_combine(yg_c, 0, ewc, x1c, mods, l, nf, tm, t_ctx, 0, final)
        xs = _moe_combine(yg_s, 0, ews, x1s, mods, l, nf, tm, t_lat, 1, final)

    y_prompt = xp.reshape(b_ctx, t_ctx, D_MODEL)
    y_sample = xs.reshape(b_lat, t_lat, D_MODEL)
    new_ckv, new_kpe, new_gdn, new_rwkv = ctx_outs
    return (y_prompt, y_sample, new_ckv, new_kpe, new_gdn, new_rwkv)


def _embed_block_diag_layers(s):
    b = s.shape[0]
    return _embed_block_diag(s.reshape(b * DEPTH, 2, HEADS, HEAD_DIM, HEAD_DIM)).reshape(
        b, DEPTH, 2, MIX_W, MIX_W)
```

```python
import functools

import numpy as np
import jax
import jax.numpy as jnp
from jax import lax
from jax.experimental import pallas as pl
from jax.experimental.pallas import tpu as pltpu
from jax.experimental.pallas import tpu_sc as plsc

F32 = jnp.float32
BF16 = jnp.bfloat16

D_MODEL = 1024
DEPTH = 2
PAST_LEN = 512
GRID_W = 64
NORM_EPS = 1e-6

MLA_HEADS = 4
MLA_NOPE = 128
MLA_ROPE = 64
MLA_V = 128
Q_LORA = 384
KV_LORA = 256
ROPE_BASE = 10000.0
MLA_SCALE = (MLA_NOPE + MLA_ROPE) ** -0.5

HEADS = 4
HEAD_DIM = 64
MIX_W = HEADS * HEAD_DIM
GDN_CONV_CH = 3 * MIX_W
CHUNK = 64
RWKV_GN_EPS = 64e-5

N_EXPERTS = 64
TOP_K = 8
N_GROUPS = 8
GROUP_SIZE = N_EXPERTS // N_GROUPS
TOPK_GROUPS = 4
D_EXPERT = 256
ROUTE_SCALE = 2.5

P_MLA = Q_LORA + KV_LORA + MLA_ROPE
P_GDN = GDN_CONV_CH + MIX_W + 4 * HEADS
P_RWKV = 3 * MIX_W + 128 + 128 + 128

LANES = 128
ZM_W = Q_LORA + KV_LORA + 2 * LANES
ZG_W = GDN_CONV_CH + MIX_W
ZR_W = P_RWKV
QH_W = 2 * LANES
VMEM_LIMIT = 56 * 1024 * 1024
POST_TM = 512

_ROPE_SWAP = np.concatenate([np.arange(16, 32), np.arange(0, 16), np.arange(48, 64), np.arange(32, 48)])


def _sigmoid(x):
    return 1.0 / (1.0 + jnp.exp(-x))


def _silu(x):
    return x * _sigmoid(x)


def _softplus(x):
    return jnp.maximum(x, 0.0) + jnp.log(1.0 + jnp.exp(-jnp.abs(x)))


def _rms(x, g, eps=NORM_EPS):
    return x * lax.rsqrt(jnp.mean(x * x, axis=-1, keepdims=True) + eps) * g


def _mm(a, b):
    return jnp.dot(a.astype(BF16), b.astype(BF16), preferred_element_type=F32)


def _mm_nt(a, b):
    return lax.dot_general(a.astype(BF16), b.astype(BF16), (((1,), (1,)), ((), ())),
                           preferred_element_type=F32)


def _mm_tn(a, b):
    return lax.dot_general(a.astype(BF16), b.astype(BF16), (((0,), (0,)), ((), ())),
                           preferred_element_type=F32)


def _split3(x):
    p1 = x.astype(BF16)
    r1 = x - p1.astype(F32)
    p2 = r1.astype(BF16)
    r2 = r1 - p2.astype(F32)
    return p1, p2, r2.astype(BF16)


def _mm_sel_l(sel, x):
    p1, p2, p3 = _split3(x)
    return _mm(sel, p1) + _mm(sel, p2) + _mm(sel, p3)


def _mm_sel_r(x, sel):
    p1, p2, p3 = _split3(x)
    return _mm(p1, sel) + _mm(p2, sel) + _mm(p3, sel)


def _iota(shape, dim):
    return lax.broadcasted_iota(jnp.int32, shape, dim)


def _layer_spec(a, layer, **kw):
    nd = a.ndim - 1
    return pl.BlockSpec((None,) + a.shape[1:], lambda *_: (layer,) + (0,) * nd, **kw)


def _const_spec(a, **kw):
    return pl.BlockSpec(a.shape, lambda *_: (0,) * a.ndim, **kw)


def _bd(x, maskbd):
    xb = x.astype(BF16)
    return jnp.concatenate([xb] * HEADS, axis=0) * maskbd


def _chunk_masks(rev):
    row = _iota((CHUNK, MIX_W), 0)
    col = jnp.bitwise_and(_iota((CHUNK, MIX_W), 1), HEAD_DIM - 1)
    r2 = _iota((CHUNK, CHUNK), 0)
    c2 = _iota((CHUNK, CHUNK), 1)
    if rev:
        inc, strict, tri = row <= col, row < col, r2 <= c2
    else:
        inc, strict, tri = row >= col, row > col, r2 >= c2
    eye = jnp.where(row == col, 1.0, 0.0).astype(F32)
    return inc, strict, jnp.where(tri, 1.0, 0.0).astype(BF16), eye


def _split2(x):
    hi = x.astype(BF16)
    return hi, (x - hi.astype(F32)).astype(BF16)


def _mm_bd3(x, p, maskbd):
    n = x.shape[0]
    xh, xl = _split2(x)
    ph, pl_ = _split2(p)
    r = jnp.dot(jnp.concatenate([xh, xl], axis=0), _bd(ph, maskbd), preferred_element_type=F32)
    return r[:n] + r[n:] + jnp.dot(xh, _bd(pl_, maskbd), preferred_element_type=F32)


def _neumann_inverse(a_list, eye_list, maskbd):
    bs = [-a for a in a_list]
    ms = [eye + b for eye, b in zip(eye_list, bs)]
    ps = [_mm_bd3(b, b, maskbd) for b in bs]
    for _ in range(4):
        boths = [_mm_bd3(jnp.concatenate([m, p], axis=0), p, maskbd) for m, p in zip(ms, ps)]
        ms = [m + both[:CHUNK] for m, both in zip(ms, boths)]
        ps = [both[CHUNK:] for both in boths]
    return [m + _mm_bd3(m, p, maskbd) for m, p in zip(ms, ps)]


def _adaln_kernel(c_ref, w_ref, b_ref, o_ref):
    cv = c_ref[...]
    o_ref[0] = _mm(_silu(cv), w_ref[0]) + b_ref[0]


def _adaln(cvec8, ada_w, ada_b):
    tn = 768
    n_out = 6 * D_MODEL
    return pl.pallas_call(
        _adaln_kernel,
        grid=(DEPTH, n_out // tn),
        in_specs=[
            pl.BlockSpec((8, D_MODEL), lambda l, j: (0, 0)),
            pl.BlockSpec((1, D_MODEL, tn), lambda l, j: (l, 0, j)),
            pl.BlockSpec((1, 1, tn), lambda l, j: (l, 0, j)),
        ],
        out_specs=pl.BlockSpec((1, 8, tn), lambda l, j: (l, 0, j)),
        out_shape=jax.ShapeDtypeStruct((DEPTH, 8, n_out), F32),
        compiler_params=pltpu.CompilerParams(dimension_semantics=("arbitrary", "arbitrary"),
                                             vmem_limit_bytes=VMEM_LIMIT),
        name="adaln",
    )(cvec8, ada_w, ada_b.reshape(DEPTH, 1, n_out))


_KPE0 = Q_LORA + KV_LORA
_W_IN_MOVES = (
    [(0, 0, P_MLA)]
    + [(P_MLA + LANES - MLA_ROPE + 16 * j, _KPE0 + 16 * int(_ROPE_SWAP[16 * j] // 16), 16) for j in range(4)]
    + [(ZM_W, P_MLA, ZG_W), (ZM_W + ZG_W, P_MLA + ZG_W, 4 * HEADS), (ZM_W + ZG_W + LANES, P_MLA + P_GDN, P_RWKV)]
)
W_IN_PAD = ZM_W + ZG_W + LANES + ZR_W


def _inproj_kernel(x_ref, mod_ref, n1_ref, w_ref, zm_ref, zg_ref, zab_ref, zr_ref, w_s):
    @pl.when(pl.program_id(0) == 0)
    def _():
        w_s[...] = jnp.zeros(w_s.shape, BF16)
        for dst, src, width in _W_IN_MOVES:
            w_s[:, dst:dst + width] = w_ref[:, src:src + width].astype(BF16)

    m = mod_ref[0]
    sh = m[:, 0:D_MODEL]
    sc = m[:, D_MODEL:2 * D_MODEL]
    h = _rms(x_ref[...], n1_ref[...]) * (1.0 + sc) + sh
    z = _mm(h, w_s[...])
    o1 = ZM_W
    o2 = o1 + ZG_W
    o3 = o2 + LANES
    zm_ref[...] = z[:, :o1]
    zg_ref[...] = z[:, o1:o2]
    zab_ref[...] = z[:, o2:o3]
    zr_ref[...] = z[:, o3:]


def _inproj(x2d, mods, lw, layer, tm, seq_len, mod_base):
    n = x2d.shape[0]
    tiles_per_seq = seq_len // tm if mod_base else 1

    def mod_idx(i):
        return (layer, mod_base + i // tiles_per_seq if mod_base else 0, 0, 0)

    return pl.pallas_call(
        _inproj_kernel,
        grid=(n // tm,),
        in_specs=[
            pl.BlockSpec((tm, D_MODEL), lambda i: (i, 0)),
            pl.BlockSpec((None, 1, 1, 6 * D_MODEL), mod_idx),
            _layer_spec(lw["norm1"], layer),
            _layer_spec(lw["w_in"], layer, pipeline_mode=pl.Buffered(1)),
        ],
        out_specs=[
            pl.BlockSpec((tm, ZM_W), lambda i: (i, 0)),
            pl.BlockSpec((tm, ZG_W), lambda i: (i, 0)),
            pl.BlockSpec((tm, LANES), lambda i: (i, 0)),
            pl.BlockSpec((tm, ZR_W), lambda i: (i, 0)),
        ],
        out_shape=[
            jax.ShapeDtypeStruct((n, ZM_W), F32),
            jax.ShapeDtypeStruct((n, ZG_W), F32),
            jax.ShapeDtypeStruct((n, LANES), F32),
            jax.ShapeDtypeStruct((n, ZR_W), F32),
        ],
        scratch_shapes=[pltpu.VMEM((D_MODEL, W_IN_PAD), BF16)],
        compiler_params=pltpu.CompilerParams(dimension_semantics=("arbitrary",),
                                             vmem_limit_bytes=VMEM_LIMIT),
        name="inproj",
    )(x2d, mods, lw["norm1"], lw["w_in"])


MLA_SEQ_PER_STEP = 4


def _mla_ctx_kernel(zm_ref, qn_ref, wuq_ref, kvn_ref, wukv_ref, prev_ckv, prev_kpe, o_ref, ckv_ref, kpe_ref,
                    *, seq_len, n_seq):
    del prev_ckv, prev_kpe
    o_kpe = Q_LORA + KV_LORA
    zm = zm_ref[...]
    ckv = _rms(zm[:, Q_LORA:o_kpe], kvn_ref[...])
    kpe = zm[:, o_kpe:o_kpe + LANES]
    kv = _mm(ckv, wukv_ref[...])
    q = _mm(_rms(zm[:, :Q_LORA], qn_ref[...]), wuq_ref[...])
    kpe_b = kpe.astype(BF16)
    seqs = [slice(i * seq_len, (i + 1) * seq_len) for i in range(n_seq)]
    for i, r in enumerate(seqs):
        ckv_ref[i] = ckv[r]
        kpe_ref[i] = kpe[r, :MLA_ROPE]
    pairs = [(r, h * QH_W) for r in seqs for h in range(MLA_HEADS)]
    scores = [_mm_nt(q[r, c0:c0 + QH_W], jnp.concatenate([kv[r, c0:c0 + LANES].astype(BF16), kpe_b[r]], axis=1))
              * MLA_SCALE for r, c0 in pairs]
    exps = [jnp.exp(s - jnp.max(s, axis=-1, keepdims=True)) for s in scores]
    outs = [_mm(e, kv[r, c0 + LANES:c0 + QH_W]) / jnp.sum(e, axis=-1, keepdims=True)
            for e, (r, c0) in zip(exps, pairs)]
    for o, (r, c0) in zip(outs, pairs):
        h = c0 // QH_W
        o_ref[r, h * MLA_V:(h + 1) * MLA_V] = o


def _mla_ctx(zm, lw, layer, seq_len, prev):
    n = zm.shape[0]
    nb = n // seq_len
    n_seq = MLA_SEQ_PER_STEP
    rows = n_seq * seq_len
    weights = [lw["q_norm"], lw["w_uq"], lw["kv_norm"], lw["w_ukv"]]
    return pl.pallas_call(
        functools.partial(_mla_ctx_kernel, seq_len=seq_len, n_seq=n_seq),
        grid=(nb // n_seq,),
        in_specs=[pl.BlockSpec((rows, ZM_W), lambda b: (b, 0))] + [_layer_spec(a, layer) for a in weights]
        + [pl.BlockSpec(memory_space=pl.ANY)] * 2,
        out_specs=[
            pl.BlockSpec((rows, MLA_HEADS * MLA_V), lambda b: (b, 0)),
            pl.BlockSpec((n_seq, None, seq_len, KV_LORA), lambda b: (b, layer, 0, 0)),
            pl.BlockSpec((n_seq, None, seq_len, MLA_ROPE), lambda b: (b, layer, 0, 0)),
        ],
        out_shape=[
            jax.ShapeDtypeStruct((n, MLA_HEADS * MLA_V), F32),
            jax.ShapeDtypeStruct((nb, DEPTH, seq_len, KV_LORA), F32),
            jax.ShapeDtypeStruct((nb, DEPTH, seq_len, MLA_ROPE), F32),
        ],
        input_output_aliases={5: 1, 6: 2},
        compiler_params=pltpu.CompilerParams(dimension_semantics=("arbitrary",),
                                             vmem_limit_bytes=VMEM_LIMIT),
        name="mla_ctx",
    )(zm, *weights, *prev)


def _mla_lat_kernel(zm_ref, cckv_ref, ckpe_ref, rc_ref, rs_ref, qn_ref, wuq_ref, wuqs_ref, kvn_ref, wukv_ref,
                    o_ref, k_s, v_s, *, seq_len, tq, past):
    qi = pl.program_id(1)
    o_kpe = Q_LORA + KV_LORA

    @pl.when(qi == 0)
    def _():
        zm = zm_ref[...]
        ckv = _rms(zm[:, Q_LORA:o_kpe], kvn_ref[...])
        kpe = zm[:, o_kpe:o_kpe + LANES] * rc_ref[...] + zm[:, o_kpe + LANES:o_kpe + 2 * LANES] * rs_ref[...]
        kvc = _mm(cckv_ref[0], wukv_ref[...])
        kpc = ckpe_ref[0].astype(BF16)
        kv = _mm(ckv, wukv_ref[...])
        kpe = kpe.astype(BF16)
        for h in range(MLA_HEADS):
            c0 = h * QH_W
            k_s[0:past, c0:c0 + LANES] = kvc[:, c0:c0 + LANES].astype(BF16)
            k_s[0:past, c0 + LANES:c0 + QH_W] = kpc
            v_s[0:past, h * MLA_V:(h + 1) * MLA_V] = kvc[:, c0 + LANES:c0 + QH_W].astype(BF16)
            k_s[past:past + seq_len, c0:c0 + LANES] = kv[:, c0:c0 + LANES].astype(BF16)
            k_s[past:past + seq_len, c0 + LANES:c0 + QH_W] = kpe
            v_s[past:past + seq_len, h * MLA_V:(h + 1) * MLA_V] = kv[:, c0 + LANES:c0 + QH_W].astype(BF16)

    r0 = pl.multiple_of(qi * tq, tq)
    zq = zm_ref[pl.ds(r0, tq), :]
    cq = _rms(zq[:, :Q_LORA], qn_ref[...])
    q = _mm(cq, wuq_ref[...])
    qs = _mm(cq, wuqs_ref[...])
    qc = jnp.concatenate([jnp.ones((tq, LANES), F32), rc_ref[pl.ds(r0, tq), :]], axis=1)
    qsn = jnp.concatenate([jnp.zeros((tq, LANES), F32), rs_ref[pl.ds(r0, tq), :]], axis=1)
    for h in range(MLA_HEADS):
        c0 = h * QH_W
        qh = q[:, c0:c0 + QH_W] * qc + qs[:, c0:c0 + QH_W] * qsn
        s = _mm_nt(qh, k_s[:, c0:c0 + QH_W]) * MLA_SCALE
        e = jnp.exp(s - jnp.max(s, axis=-1, keepdims=True))
        den = jnp.sum(e, axis=-1, keepdims=True)
        o_ref[:, h * MLA_V:(h + 1) * MLA_V] = _mm(e, v_s[:, h * MLA_V:(h + 1) * MLA_V]) / den


def _mla_lat(zm, lw, layer, seq_len, tq, cache):
    n = zm.shape[0]
    nb = n // seq_len
    past = PAST_LEN
    tk = past + seq_len
    cckv, ckpe, rc, rs = cache
    weights = [lw["q_norm"], lw["w_uq"], lw["w_uq_sw"], lw["kv_norm"], lw["w_ukv"]]
    return pl.pallas_call(
        functools.partial(_mla_lat_kernel, seq_len=seq_len, tq=tq, past=past),
        grid=(nb, seq_len // tq),
        in_specs=[pl.BlockSpec((seq_len, ZM_W), lambda b, q: (b, 0)),
                  pl.BlockSpec((1, None, past, KV_LORA), lambda b, q: (b, layer, 0, 0)),
                  pl.BlockSpec((1, None, past, LANES), lambda b, q: (b, layer, 0, 0)),
                  _const_spec(rc), _const_spec(rs)] + [_layer_spec(a, layer) for a in weights],
        out_specs=pl.BlockSpec((tq, MLA_HEADS * MLA_V), lambda b, q: (b * (seq_len // tq) + q, 0)),
        out_shape=jax.ShapeDtypeStruct((n, MLA_HEADS * MLA_V), F32),
        scratch_shapes=[
            pltpu.VMEM((tk, MLA_HEADS * QH_W), BF16),
            pltpu.VMEM((tk, MLA_HEADS * MLA_V), BF16),
        ],
        compiler_params=pltpu.CompilerParams(dimension_semantics=("arbitrary", "arbitrary"),
                                             vmem_limit_bytes=VMEM_LIMIT),
        name="mla_lat",
    )(zm, cckv, ckpe, rc, rs, *weights)


SEQ_PER_STEP = 4
CHUNK_GROUP = 4
GDN_PRE_W = 5 * MIX_W


def _for_chunk_groups(n_chunks, fn):
    if n_chunks == CHUNK_GROUP:
        fn(0)
    else:
        def body(gi, carry):
            fn(gi * CHUNK_GROUP)
            return carry
        lax.fori_loop(0, n_chunks // CHUNK_GROUP, body, 0)


def _gdn_prepare(items, maskbd):
    n = range(len(items))
    qs, ks, vs, gs, betas, masks, revs = zip(*items)
    gcs = [_mm_sel_l(masks[i][2], gs[i]) for i in n]
    decays = []
    for i in n:
        inc, eye = masks[i][0], masks[i][3]
        gc_row = jnp.sum(eye * gcs[i], axis=0, keepdims=True)
        decays.append(jnp.where(inc, jnp.exp(jnp.where(inc, gcs[i] - gc_row, 0.0)), 0.0))
    kbs = [ks[i] * betas[i] for i in n]
    aqs = [_mm_nt(jnp.concatenate([kbs[i], qs[i]], axis=0), _bd(ks[i], maskbd)) for i in n]
    a_mats = [jnp.where(masks[i][1], aqs[i][:CHUNK] * decays[i], 0.0) for i in n]
    t_invs = _neumann_inverse(a_mats, [m[3] for m in masks], maskbd)
    egcs = [jnp.exp(gc) for gc in gcs]
    uws = [_mm(t_invs[i], jnp.concatenate([_bd(vs[i] * betas[i], maskbd), _bd(kbs[i] * egcs[i], maskbd)], axis=1))
           for i in n]
    out = []
    for i in n:
        g_last = gcs[i][0:1] if revs[i] else gcs[i][CHUNK - 1:CHUNK]
        pre = jnp.concatenate([uws[i], qs[i] * egcs[i], aqs[i][CHUNK:] * decays[i],
                               ks[i] * jnp.exp(g_last - gcs[i])], axis=1)
        out.append((pre, jnp.broadcast_to(jnp.exp(g_last), (8, MIX_W))))
    return out


def _gdn_step(items, maskbd):
    w = MIX_W
    n = range(len(items))
    pres, egls, states = zip(*items)
    wqs = [_mm(jnp.concatenate([pres[i][:, w:2 * w], pres[i][:, 2 * w:3 * w]], axis=0), states[i]) for i in n]
    v_news = [pres[i][:, :w] - wqs[i][:CHUNK] for i in n]
    outs = [wqs[i][CHUNK:] + _mm(pres[i][:, 3 * w:4 * w], _bd(v_news[i], maskbd)) for i in n]
    upds = [_mm_tn(pres[i][:, 4 * w:], v_news[i]) for i in n]
    mask_f = maskbd.astype(F32)
    return [(outs[i], states[i] * egls[i] + upds[i] * mask_f) for i in n]


def _gdn_kernel(*refs, seq_len, n_seq, cached):
    if cached:
        (zg_ref, zab_ref, s0_ref, conv_ref, alog_ref, dtb_ref, gn_ref, eexp_ref, ones_ref, maskbd_ref,
         o_ref, sout_ref, q_s, k_s, v_s, ge_s, pre_s, gl_s, oacc_s, st_s) = refs
    else:
        zg_ref, zab_ref, conv_ref, alog_ref, dtb_ref, gn_ref, eexp_ref, ones_ref, maskbd_ref = refs[:9]
        o_ref, sout_ref, q_s, k_s, v_s, ge_s, pre_s, gl_s, oacc_s, st_s = refs[-10:]
    t = seq_len * n_seq
    z = zg_ref[:, :GDN_CONV_CH]
    rowi = jnp.bitwise_and(_iota((t, 1), 0), seq_len - 1)
    zp = jnp.where(rowi == 0, 0.0, pltpu.roll(z, 1, 0))
    zn = jnp.where(rowi == seq_len - 1, 0.0, pltpu.roll(z, t - 1, 0))
    cw = conv_ref[...]
    qkv = _silu(zp * cw[0:1] + z * cw[1:2] + zn * cw[2:3])
    ones_bd = ones_ref[...]
    q = qkv[:, :MIX_W]
    k = qkv[:, MIX_W:2 * MIX_W]
    q_s[...] = q * lax.rsqrt(_mm_sel_r(q * q, ones_bd) + 1e-6) * (HEAD_DIM ** -0.5)
    k_s[...] = k * lax.rsqrt(_mm_sel_r(k * k, ones_bd) + 1e-6)
    v_s[...] = qkv[:, 2 * MIX_W:]
    ab = zab_ref[...]
    lane = _iota((t, LANES), 1)
    gb = jnp.where(lane < 2 * HEADS, -jnp.exp(alog_ref[...]) * _softplus(ab + dtb_ref[...]), _sigmoid(ab))
    ge_s[...] = _mm_sel_r(gb, eexp_ref[...])
    oacc_s[...] = jnp.zeros((t, MIX_W), F32)
    if cached:
        st_s[...] = s0_ref[...]
    else:
        st_s[...] = jnp.zeros((n_seq, 2, MIX_W, MIX_W), F32)
    maskbd = maskbd_ref[...]
    masks = (_chunk_masks(False), _chunk_masks(True))
    n_chunks = seq_len // CHUNK

    def prepare_group(c0):
        where, items = [], []
        for j in range(CHUNK_GROUP):
            c = c0 + j
            rows = pl.ds(pl.multiple_of(c * CHUNK, CHUNK), CHUNK)
            for d in range(2):
                where.append((d, c, rows))
                items.append((q_s[rows, :], k_s[rows, :], v_s[rows, :], ge_s[rows, d * MIX_W:(d + 1) * MIX_W],
                              ge_s[rows, (2 + d) * MIX_W:(3 + d) * MIX_W], masks[d], d == 1))
        for (d, c, rows), (pre, egl) in zip(where, _gdn_prepare(items, maskbd)):
            pre_s[d, rows, :] = pre
            gl_s[d, pl.ds(pl.multiple_of(c * 8, 8), 8), :] = egl

    _for_chunk_groups(n_chunks * n_seq, prepare_group)

    def body(i, carry):
        where, items = [], []
        for q in range(n_seq):
            for d in range(2):
                c = q * n_chunks + (i if d == 0 else n_chunks - 1 - i)
                rows = pl.ds(pl.multiple_of(c * CHUNK, CHUNK), CHUNK)
                where.append((q, d, rows))
                items.append((pre_s[d, rows, :], gl_s[d, pl.ds(pl.multiple_of(c * 8, 8), 8), :][0:1], st_s[q, d]))
        for (q, d, rows), (o, s_new) in zip(where, _gdn_step(items, maskbd)):
            oacc_s[rows, :] = oacc_s[rows, :] + o
            st_s[q, d] = s_new
        return carry

    lax.fori_loop(0, n_chunks, body, 0)
    o = oacc_s[...]
    ms = _mm_sel_r(o * o, ones_bd) * (1.0 / HEAD_DIM)
    gate = zg_ref[:, GDN_CONV_CH:]
    o_ref[...] = o * lax.rsqrt(ms + NORM_EPS) * gn_ref[...] * _silu(gate)
    for q in range(n_seq):
        for d in range(2):
            for h in range(HEADS):
                sl = slice(h * HEAD_DIM, (h + 1) * HEAD_DIM)
                sout_ref[q, d, h] = st_s[q, d, sl, sl]


def _state_out(nb, n_seq, layer, cached, prev, args, in_specs):
    if cached:
        return (pl.BlockSpec((n_seq, 2, HEADS, HEAD_DIM, HEAD_DIM), lambda b: (b, 0, 0, 0, 0)),
                jax.ShapeDtypeStruct((nb, 2, HEADS, HEAD_DIM, HEAD_DIM), F32), {})
    aliases = {len(args): 1}
    args.append(prev)
    in_specs.append(pl.BlockSpec(memory_space=pl.ANY))
    return (pl.BlockSpec((n_seq, None, 2, HEADS, HEAD_DIM, HEAD_DIM), lambda b: (b, layer, 0, 0, 0, 0)),
            jax.ShapeDtypeStruct((nb, DEPTH, 2, HEADS, HEAD_DIM, HEAD_DIM), F32), aliases)


def _gdn(zg, zab, lw, layer, consts, seq_len, s0_bd=None, prev=None):
    n = zg.shape[0]
    nb = n // seq_len
    cached = s0_bd is not None
    n_seq = 1 if cached else SEQ_PER_STEP
    rows = seq_len * n_seq
    args = [zg, zab]
    in_specs = [pl.BlockSpec((rows, ZG_W), lambda b: (b, 0)),
                pl.BlockSpec((rows, LANES), lambda b: (b, 0))]
    if cached:
        args.append(s0_bd)
        in_specs.append(pl.BlockSpec((1, None, 2, MIX_W, MIX_W), lambda b: (b, layer, 0, 0, 0)))
    layered = [lw["gdn_conv"], lw["gdn_alog"], lw["gdn_dtb"], lw["gdn_norm"]]
    const = [consts["eexp"], consts["ones_bd"], consts["maskbd"]]
    args += layered + const
    in_specs += [_layer_spec(a, layer) for a in layered] + [_const_spec(a) for a in const]
    s_spec, s_shape, aliases = _state_out(nb, n_seq, layer, cached, prev, args, in_specs)
    return pl.pallas_call(
        functools.partial(_gdn_kernel, seq_len=seq_len, n_seq=n_seq, cached=cached),
        grid=(nb // n_seq,),
        in_specs=in_specs,
        out_specs=[
            pl.BlockSpec((rows, MIX_W), lambda b: (b, 0)),
            s_spec,
        ],
        out_shape=[jax.ShapeDtypeStruct((n, MIX_W), F32), s_shape],
        input_output_aliases=aliases,
        scratch_shapes=[
            pltpu.VMEM((rows, MIX_W), F32),
            pltpu.VMEM((rows, MIX_W), F32),
            pltpu.VMEM((rows, MIX_W), F32),
            pltpu.VMEM((rows, 4 * MIX_W), F32),
            pltpu.VMEM((2, rows, GDN_PRE_W), F32),
            pltpu.VMEM((2, rows // CHUNK * 8, MIX_W), F32),
            pltpu.VMEM((rows, MIX_W), F32),
            pltpu.VMEM((n_seq, 2, MIX_W, MIX_W), F32),
        ],
        compiler_params=pltpu.CompilerParams(dimension_semantics=("arbitrary",),
                                             vmem_limit_bytes=VMEM_LIMIT),
        name="gdn_lat" if cached else "gdn_ctx",
    )(*args)


RWKV_PRE_W = 7 * MIX_W


def _rwkv_prepare(items, maskbd, eye_full):
    n = range(len(items))
    rs, kds, vs, kks, bs, lws, masks, revs = zip(*items)
    cums = [_mm_sel_l(masks[i][2], lws[i]) for i in n]
    einvs = [jnp.exp(-c) for c in cums]
    kts = [kks[i] * jnp.exp(cums[i] - lws[i]) for i in n]
    rts = [rs[i] * jnp.exp(cums[i]) for i in n]
    krs = [jnp.concatenate([kts[i], rts[i]], axis=0) for i in n]
    lb_alls = [_mm_nt(krs[i], _bd(bs[i] * einvs[i], maskbd)) for i in n]
    lk_alls = [_mm_nt(krs[i], _bd(kds[i] * einvs[i], maskbd)) for i in n]
    lbs = [jnp.where(masks[i][1], lb_alls[i][:CHUNK], 0.0) for i in n]
    t_invs = _neumann_inverse(lbs, [m[3] for m in masks], maskbd)
    lvs = [_mm(jnp.concatenate([jnp.where(masks[i][1], lk_alls[i][:CHUNK], 0.0),
                                jnp.where(masks[i][0], lk_alls[i][CHUNK:], 0.0)], axis=0), _bd(vs[i], maskbd))
           for i in n]
    tkps = [_mm(t_invs[i], jnp.concatenate([_bd(kts[i], maskbd), _bd(lvs[i][:CHUNK], maskbd)], axis=1)) for i in n]
    out = []
    for i in n:
        c_last = cums[i][0:1] if revs[i] else cums[i][CHUNK - 1:CHUNK]
        tail = jnp.exp(c_last - cums[i])
        rb = jnp.where(masks[i][0], lb_alls[i][CHUNK:], 0.0)
        pre = jnp.concatenate([tkps[i][:, :MIX_W], rts[i], tkps[i][:, MIX_W:], lvs[i][CHUNK:], rb,
                               kds[i] * tail, bs[i] * tail], axis=1)
        gcol = jnp.sum(eye_full * jnp.exp(c_last), axis=1, keepdims=True)
        out.append((pre, jnp.broadcast_to(gcol, (MIX_W, LANES))))
    return out


def _rwkv_step(items, maskbd):
    w = MIX_W
    n = range(len(items))
    pres, vs, gcols, states = zip(*items)
    prs = [_mm(jnp.concatenate([pres[i][:, :w], pres[i][:, w:2 * w]], axis=0), states[i]) for i in n]
    ps = [prs[i][:CHUNK] + pres[i][:, 2 * w:3 * w] for i in n]
    outs = [prs[i][CHUNK:] + pres[i][:, 3 * w:4 * w] - _mm(pres[i][:, 4 * w:5 * w], _bd(ps[i], maskbd)) for i in n]
    upds = [_mm_tn(jnp.concatenate([pres[i][:, 5 * w:6 * w], pres[i][:, 6 * w:]], axis=0),
                   jnp.concatenate([vs[i], -ps[i]], axis=0)) for i in n]
    mask_f = maskbd.astype(F32)
    return [(outs[i], states[i] * jnp.concatenate([gcols[i], gcols[i]], axis=1) + upds[i] * mask_f) for i in n]


def _rwkv_kernel(*refs, seq_len, n_seq, cached):
    if cached:
        (zr_ref, s0_ref, mup_ref, mun_ref, w0_ref, w2_ref, a0_ref, a2_ref, g2_ref, kk_ref, ka_ref, rk_ref,
         gnw_ref, gnb_ref, ones_ref, maskbd_ref, o_ref, sout_ref,
         r_s, v_s, kk_s, dir_s, bg_s, pre_s, gcol_s, oacc_s, st_s) = refs
    else:
        (zr_ref, mup_ref, mun_ref, w0_ref, w2_ref, a0_ref, a2_ref, g2_ref, kk_ref, ka_ref, rk_ref,
         gnw_ref, gnb_ref, ones_ref, maskbd_ref) = refs[:15]
        o_ref, sout_ref, r_s, v_s, kk_s, dir_s, bg_s, pre_s, gcol_s, oacc_s, st_s = refs[-11:]
    t = seq_len * n_seq
    z = zr_ref[...]
    rowi = jnp.bitwise_and(_iota((t, 1), 0), seq_len - 1)
    zp = jnp.where(rowi == 0, 0.0, pltpu.roll(z, 1, 0))
    zn = jnp.where(rowi == seq_len - 1, 0.0, pltpu.roll(z, t - 1, 0))
    z = z + mup_ref[...] * (zp - z) + mun_ref[...] * (zn - z)
    w = MIX_W
    r = z[:, :w]
    k = z[:, w:2 * w]
    v = z[:, 2 * w:3 * w]
    wd = jnp.tanh(z[:, 3 * w:3 * w + LANES])
    ad = z[:, 3 * w + LANES:3 * w + 2 * LANES]
    gd = _sigmoid(z[:, 3 * w + 2 * LANES:])
    ones_bd = ones_ref[...]
    kk = k * kk_ref[...]
    kk = kk * lax.rsqrt(_mm_sel_r(kk * kk, ones_bd) + 1e-6)
    r_s[...] = r
    v_s[...] = v
    kk_s[...] = kk
    bonus = jnp.zeros((t, w), F32)
    for d in range(2):
        w_log = -_softplus(-(w0_ref[d:d + 1] + _mm(wd, w2_ref[d]))) - 0.5
        a = _sigmoid(a0_ref[d:d + 1] + _mm(ad, a2_ref[d]))
        kd = k * (1.0 + (a - 1.0) * ka_ref[...])
        dir_s[:, (3 * d) * w:(3 * d + 1) * w] = -jnp.exp(w_log)
        dir_s[:, (3 * d + 1) * w:(3 * d + 2) * w] = kd
        dir_s[:, (3 * d + 2) * w:(3 * d + 3) * w] = kk * a
        bonus = bonus + _mm_sel_r(r * kd * rk_ref[...], ones_bd) * v
    bg_s[:, :w] = bonus
    bg_s[:, w:] = _mm(gd, g2_ref[...])
    oacc_s[...] = jnp.zeros((t, w), F32)
    if cached:
        st_s[...] = s0_ref[...]
    else:
        st_s[...] = jnp.zeros((n_seq, 2, w, w), F32)
    maskbd = maskbd_ref[...]
    masks = (_chunk_masks(False), _chunk_masks(True))
    eye_full = jnp.where(_iota((w, w), 0) == _iota((w, w), 1), 1.0, 0.0).astype(F32)
    n_chunks = seq_len // CHUNK

    def prepare_group(c0):
        where, items = [], []
        for j in range(CHUNK_GROUP):
            c = c0 + j
            rows = pl.ds(pl.multiple_of(c * CHUNK, CHUNK), CHUNK)
            for d in range(2):
                where.append((d, c, rows))
                items.append((r_s[rows, :], dir_s[rows, (3 * d + 1) * w:(3 * d + 2) * w], v_s[rows, :], kk_s[rows, :],
                              dir_s[rows, (3 * d + 2) * w:(3 * d + 3) * w], dir_s[rows, (3 * d) * w:(3 * d + 1) * w],
                              masks[d], d == 1))
        for (d, c, rows), (pre, gcol) in zip(where, _rwkv_prepare(items, maskbd, eye_full)):
            pre_s[d, rows, :] = pre
            gcol_s[d, pl.ds(pl.multiple_of(c * w, w), w), :] = gcol

    _for_chunk_groups(n_chunks * n_seq, prepare_group)

    def body(i, carry):
        where, items = [], []
        for q in range(n_seq):
            for d in range(2):
                c = q * n_chunks + (i if d == 0 else n_chunks - 1 - i)
                rows = pl.ds(pl.multiple_of(c * CHUNK, CHUNK), CHUNK)
                where.append((q, d, rows))
                items.append((pre_s[d, rows, :], v_s[rows, :], gcol_s[d, pl.ds(pl.multiple_of(c * w, w), w), :],
                              st_s[q, d]))
        for (q, d, rows), (o, z_new) in zip(where, _rwkv_step(items, maskbd)):
            oacc_s[rows, :] = oacc_s[rows, :] + o
            st_s[q, d] = z_new
        return carry

    lax.fori_loop(0, n_chunks, body, 0)
    o = oacc_s[...]
    inv_n = 1.0 / HEAD_DIM
    mu = _mm_sel_r(o, ones_bd) * inv_n
    oc = o - mu
    var = _mm_sel_r(oc * oc, ones_bd) * inv_n
    y = oc * lax.rsqrt(var + RWKV_GN_EPS) * gnw_ref[...] + gnb_ref[...]
    o_ref[...] = (y + bg_s[:, :w]) * bg_s[:, w:]
    for q in range(n_seq):
        for d in range(2):
            state = st_s[q, d] if cached else st_s[q, d].T
            for h in range(HEADS):
                sl = slice(h * HEAD_DIM, (h + 1) * HEAD_DIM)
                sout_ref[q, d, h] = state[sl, sl]


def _rwkv(zr, lw, layer, consts, seq_len, s0_bd=None, prev=None):
    n = zr.shape[0]
    nb = n // seq_len
    cached = s0_bd is not None
    n_seq = 1 if cached else SEQ_PER_STEP
    rows = seq_len * n_seq
    args = [zr]
    in_specs = [pl.BlockSpec((rows, ZR_W), lambda b: (b, 0))]
    if cached:
        args.append(s0_bd)
        in_specs.append(pl.BlockSpec((1, None, 2, MIX_W, MIX_W), lambda b: (b, layer, 0, 0, 0)))
    layered = [lw["rwkv_mu_prev"], lw["rwkv_mu_next"], lw["rwkv_w0"], lw["rwkv_w2"], lw["rwkv_a0"], lw["rwkv_a2"],
               lw["rwkv_g2"], lw["rwkv_k_k"], lw["rwkv_k_a"], lw["rwkv_r_k"], lw["rwkv_gn_w"], lw["rwkv_gn_b"]]
    const = [consts["ones_bd"], consts["maskbd"]]
    args += layered + const
    in_specs += [_layer_spec(a, layer) for a in layered] + [_const_spec(a) for a in const]
    s_spec, s_shape, aliases = _state_out(nb, n_seq, layer, cached, prev, args, in_specs)
    return pl.pallas_call(
        functools.partial(_rwkv_kernel, seq_len=seq_len, n_seq=n_seq, cached=cached),
        grid=(nb // n_seq,),
        in_specs=in_specs,
        out_specs=[
            pl.BlockSpec((rows, MIX_W), lambda b: (b, 0)),
            s_spec,
        ],
        out_shape=[jax.ShapeDtypeStruct((n, MIX_W), F32), s_shape],
        input_output_aliases=aliases,
        scratch_shapes=[
            pltpu.VMEM((rows, MIX_W), F32),
            pltpu.VMEM((rows, MIX_W), F32),
            pltpu.VMEM((rows, MIX_W), F32),
            pltpu.VMEM((rows, 6 * MIX_W), F32),
            pltpu.VMEM((rows, 2 * MIX_W), F32),
            pltpu.VMEM((2, rows, RWKV_PRE_W), F32),
            pltpu.VMEM((2, rows // CHUNK * MIX_W, LANES), F32),
            pltpu.VMEM((rows, MIX_W), F32),
            pltpu.VMEM((n_seq, 2, MIX_W, MIX_W), F32),
        ],
        compiler_params=pltpu.CompilerParams(dimension_semantics=("arbitrary",),
                                             vmem_limit_bytes=VMEM_LIMIT),
        name="rwkv_lat" if cached else "rwkv_ctx",
    )(*args)


def _route(logits_t, bias):
    tm = logits_t.shape[1]
    neg = -jnp.inf
    sc = _sigmoid(logits_t)
    sc3 = sc.reshape(N_GROUPS, GROUP_SIZE, tm)
    sel = (sc + bias).reshape(N_GROUPS, GROUP_SIZE, tm)
    si = _iota(sel.shape, 1).astype(F32)
    m1 = jnp.max(sel, axis=1, keepdims=True)
    f1 = jnp.min(jnp.where(sel == m1, si, float(GROUP_SIZE)), axis=1, keepdims=True)
    m2 = jnp.max(jnp.where(si == f1, neg, sel), axis=1, keepdims=True)
    grp = m1 + m2
    gi = _iota(grp.shape, 0).astype(F32)
    gsel = jnp.zeros(grp.shape, F32)
    for _ in range(TOPK_GROUPS):
        mx = jnp.max(grp, axis=0, keepdims=True)
        fi = jnp.min(jnp.where(grp == mx, gi, float(N_GROUPS)), axis=0, keepdims=True)
        hit = gi == fi
        gsel = jnp.where(hit, 1.0, gsel)
        grp = jnp.where(hit, neg, grp)
    cur = jnp.where(gsel > 0.0, sel, neg)
    ei = (_iota(cur.shape, 0) * GROUP_SIZE + _iota(cur.shape, 1)).astype(F32)
    chosen = jnp.zeros(cur.shape, F32)
    ids, wts = [], []
    for _ in range(TOP_K):
        mx = jnp.max(jnp.max(cur, axis=0, keepdims=True), axis=1, keepdims=True)
        fi = jnp.min(jnp.min(jnp.where(cur == mx, ei, float(N_EXPERTS)), axis=0, keepdims=True),
                     axis=1, keepdims=True)
        hit = ei == fi
        chosen = jnp.where(hit, 1.0, chosen)
        cur = jnp.where(hit, neg, cur)
        ids.append(fi.reshape(1, tm))
        wts.append(jnp.sum(jnp.sum(jnp.where(hit, sc3, 0.0), axis=0, keepdims=True), axis=1, keepdims=True)
                   .reshape(1, tm))
    w = jnp.concatenate(wts, axis=0)
    w = w / jnp.sum(w, axis=0, keepdims=True) * ROUTE_SCALE
    return chosen.reshape(N_EXPERTS, tm), jnp.concatenate(ids, axis=0), w


def _pack_halves(x):
    half = x.shape[1] // 2
    bits = lax.bitcast_convert_type(x.astype(BF16).astype(F32), jnp.int32)
    lo = lax.shift_right_logical(bits[:, :half], jnp.int32(16))
    return jnp.bitwise_or(lo, jnp.bitwise_and(bits[:, half:], jnp.int32(-65536)))


def _unpack_halves(word):
    lo = lax.bitcast_convert_type(lax.shift_left(word, jnp.int32(16)), F32)
    hi = lax.bitcast_convert_type(jnp.bitwise_and(word, jnp.int32(-65536)), F32)
    return lo, hi


def _post_kernel(x_ref, om_ref, og_ref, or_ref, mod_ref, wo_ref, n2_ref, rt_ref, rb_ref, sgu_ref, sdn_ref,
                 tri_ref, cin_ref, x1_ref, h2_ref, eid_ref, rank_ref, ew_ref, cnt_ref, carry_s):
    @pl.when(pl.program_id(0) == 0)
    def _():
        carry_s[...] = cin_ref[...]

    m = mod_ref[0]
    g1 = m[:, 2 * D_MODEL:3 * D_MODEL]
    sh2 = m[:, 3 * D_MODEL:4 * D_MODEL]
    sc2 = m[:, 4 * D_MODEL:5 * D_MODEL]
    w_mla = MLA_HEADS * MLA_V
    mix = (_mm(om_ref[...], wo_ref[0:w_mla, :]) + _mm(og_ref[...], wo_ref[w_mla:w_mla + MIX_W, :])
           + _mm(or_ref[...], wo_ref[w_mla + MIX_W:, :]))
    x1 = x_ref[...] + g1 * mix
    h2 = _rms(x1, n2_ref[...]) * (1.0 + sc2) + sh2
    h2_ref[...] = _pack_halves(h2)
    g2 = m[:, 5 * D_MODEL:]
    x1_ref[...] = x1 + g2 * _mm(_swiglu_act(_mm(h2, sgu_ref[...])), sdn_ref[...])
    r_hi, r_lo = _split2(rt_ref[...])
    h_hi, h_lo = _split2(h2)
    logits_t = _mm_nt(r_hi, h_hi) + _mm_nt(r_hi, h_lo) + _mm_nt(r_lo, h_hi)
    chosen, ids, w = _route(logits_t, rb_ref[...])
    tm = chosen.shape[1]
    rank_et = (carry_s[:, 0:1] + _mm(chosen, tri_ref[...])).reshape(N_GROUPS, GROUP_SIZE, tm)
    ei = (_iota(rank_et.shape, 0) * GROUP_SIZE + _iota(rank_et.shape, 1)).astype(F32)
    ranks = []
    for k in range(TOP_K):
        pick = jnp.where(ei == ids[k:k + 1].reshape(1, 1, tm), rank_et, 0.0)
        ranks.append(jnp.sum(jnp.sum(pick, axis=0, keepdims=True), axis=1, keepdims=True).reshape(1, tm))
    eid_ref[...] = ids.astype(jnp.int32)
    rank_ref[...] = jnp.concatenate(ranks, axis=0).astype(jnp.int32)
    ew_ref[...] = jnp.concatenate([w, jnp.zeros((LANES - TOP_K, tm), F32)], axis=0).T
    total = carry_s[...] + jnp.sum(chosen, axis=1, keepdims=True)
    carry_s[...] = total
    cnt_ref[...] = total


def _post(x2d, om, og, orw, mods, lw, layer, consts, tm, seq_len, mod_base, counts_in):
    n = x2d.shape[0]
    tiles_per_seq = seq_len // tm if mod_base else 1

    def mod_idx(i):
        return (layer, mod_base + i // tiles_per_seq if mod_base else 0, 0, 0)

    row = lambda w: pl.BlockSpec((tm, w), lambda i: (i, 0))
    col = lambda h: pl.BlockSpec((h, tm), lambda i: (0, i))
    full = lambda a: _layer_spec(a, layer)
    tail = [lw["w_out"], lw["norm2"], lw["router_t"], lw["router_b"], lw["shared_w_gu"], lw["shared_w_down"]]
    tri = consts["tri_tokens"]
    return pl.pallas_call(
        _post_kernel,
        grid=(n // tm,),
        in_specs=[row(D_MODEL), row(MLA_HEADS * MLA_V), row(MIX_W), row(MIX_W),
                  pl.BlockSpec((None, 1, 1, 6 * D_MODEL), mod_idx)] + [full(a) for a in tail]
        + [_const_spec(tri), _const_spec(counts_in)],
        out_specs=[row(D_MODEL), row(D_MODEL // 2), col(TOP_K), col(TOP_K), row(LANES),
                   pl.BlockSpec((N_EXPERTS, LANES), lambda i: (0, 0))],
        out_shape=[
            jax.ShapeDtypeStruct((n, D_MODEL), F32),
            jax.ShapeDtypeStruct((n, D_MODEL // 2), jnp.int32),
            jax.ShapeDtypeStruct((TOP_K, n), jnp.int32),
            jax.ShapeDtypeStruct((TOP_K, n), jnp.int32),
            jax.ShapeDtypeStruct((n, LANES), F32),
            jax.ShapeDtypeStruct((N_EXPERTS, LANES), F32),
        ],
        scratch_shapes=[pltpu.VMEM((N_EXPERTS, LANES), F32)],
        compiler_params=pltpu.CompilerParams(dimension_semantics=("arbitrary",),
                                             vmem_limit_bytes=VMEM_LIMIT),
        name="post",
    )(x2d, om, og, orw, mods, *tail, tri, counts_in)


MOE_ROWS = 1024
MOE_PART_ROWS = 512
SC_ROWS = 128
SC_SUBCORES = 32


def _swiglu_act(gu):
    return _silu(gu[:, :D_EXPERT]) * gu[:, D_EXPERT:]


def _dispatch_plan(eid, rank, counts, n, rows):
    n_blocks = n * TOP_K // rows + N_EXPERTS
    cnt = counts[:, 0].astype(jnp.int32)
    blocks = (cnt + rows - 1) // rows
    block_end = jnp.cumsum(blocks)
    offset = (block_end - blocks) * rows
    experts = jnp.arange(N_EXPERTS, dtype=jnp.int32)
    dest = jnp.sum(jnp.where(eid[..., None] == experts, offset, 0), axis=-1) + rank
    block_ids = jnp.arange(n_blocks, dtype=jnp.int32)
    block_expert = jnp.minimum(jnp.sum((block_end[None, :] <= block_ids[:, None]).astype(jnp.int32), axis=1),
                               N_EXPERTS - 1)
    n_used = block_end[-1:]
    prev_expert = jnp.concatenate([jnp.full((1,), -1, jnp.int32), block_expert[:-1]])
    first = ((block_ids < n_used[0]) & (block_expert != prev_expert)).astype(jnp.int32)
    slot = jnp.bitwise_and(jnp.cumsum(first) - 1, 1)
    owner_or_none = jnp.where(blocks > 0, experts, N_EXPERTS)
    next_owner = jnp.concatenate([lax.cummin(owner_or_none[::-1])[::-1][1:], jnp.full((1,), N_EXPERTS, jnp.int32)])
    nxt = jnp.sum(jnp.where(block_expert[:, None] == experts, next_owner, 0), axis=1)
    rows_end = jnp.sum(jnp.where(block_expert[:, None] == experts, offset + cnt, 0), axis=1)
    valid = jnp.where(block_ids < n_used[0], jnp.clip(rows_end - block_ids * rows, 0, rows), 0)
    schedule = (block_expert, first, slot.astype(jnp.int32), nxt.astype(jnp.int32), n_used.astype(jnp.int32),
                valid.astype(jnp.int32))
    return dest, schedule, n_blocks


def _sc_mesh():
    return plsc.VectorSubcoreMesh(core_axis_name="core", subcore_axis_name="subcore")


def _sc_dispatch(groups, dest, n_rows):
    w = groups[0].shape[1]
    dtype = groups[0].dtype

    @functools.partial(pl.kernel, out_type=jax.ShapeDtypeStruct((n_rows, w), dtype), mesh=_sc_mesh(),
                       scratch_types=[pltpu.VMEM((SC_ROWS, w), dtype), pltpu.VMEM((TOP_K, SC_ROWS), jnp.int32)])
    def kern(*refs):
        x_refs, d_hbm, o_hbm, xv, dv = refs[:len(groups)], *refs[len(groups):]
        sid = lax.axis_index("core") * (SC_SUBCORES // 2) + lax.axis_index("subcore")
        start = 0
        for x_hbm, x in zip(x_refs, groups):
            def chunk(c, x_hbm=x_hbm, start=start):
                r0 = pl.multiple_of(c * SC_ROWS, SC_ROWS)
                pltpu.sync_copy(x_hbm.at[pl.ds(r0, SC_ROWS)], xv)
                pltpu.sync_copy(d_hbm.at[:, pl.ds(start + r0, SC_ROWS)], dv)
                for k in range(TOP_K):
                    pltpu.sync_copy(xv, o_hbm.at[dv.at[k]])

            pl.loop(sid, x.shape[0] // SC_ROWS, step=SC_SUBCORES)(chunk)
            start += x.shape[0]

    return kern(*groups, dest)


def _sc_gather(y, idx):
    w = y.shape[1]
    n_chunks = idx.shape[0]

    @functools.partial(pl.kernel, out_type=jax.ShapeDtypeStruct((n_chunks * SC_ROWS, w), y.dtype), mesh=_sc_mesh(),
                       scratch_types=[pltpu.VMEM((SC_ROWS, w), y.dtype), pltpu.VMEM((1, SC_ROWS), jnp.int32)])
    def kern(y_hbm, i_hbm, o_hbm, ov, iv):
        sid = lax.axis_index("core") * (SC_SUBCORES // 2) + lax.axis_index("subcore")

        @pl.loop(sid, n_chunks, step=SC_SUBCORES)
        def _(c):
            pltpu.sync_copy(i_hbm.at[pl.ds(c, 1)], iv)
            pltpu.sync_copy(y_hbm.at[iv.at[0]], ov)
            pltpu.sync_copy(ov, o_hbm.at[pl.ds(pl.multiple_of(c * SC_ROWS, SC_ROWS), SC_ROWS)])

    return kern(y, idx)


def _moe_rows_kernel(be_ref, first_ref, slot_ref, nxt_ref, nu_ref, valid_ref, x_ref, wgu_hbm, wdn_hbm, y_ref,
                     wgu_f, wdn_f, wgu_b, wdn_b, sem, *, layer):
    b = pl.program_id(0)

    def weight_copies(expert, slot):
        return (pltpu.make_async_copy(wgu_hbm.at[layer, expert], wgu_f.at[slot], sem.at[slot, 0]),
                pltpu.make_async_copy(wdn_hbm.at[layer, expert], wdn_f.at[slot], sem.at[slot, 1]))

    @pl.when(b == 0)
    def _():
        for copy in weight_copies(be_ref[0], 0):
            copy.start()

    @pl.when(first_ref[b] == 1)
    def _():
        slot = slot_ref[b]
        for copy in weight_copies(be_ref[b], slot):
            copy.wait()

        @pl.when(nxt_ref[b] < N_EXPERTS)
        def _():
            for copy in weight_copies(nxt_ref[b], 1 - slot):
                copy.start()

        wgu_b[...] = wgu_f[slot].astype(BF16)
        wdn_b[...] = wdn_f[slot].astype(BF16)

    half = D_MODEL // 2
    for r0 in range(0, x_ref.shape[0], MOE_PART_ROWS):
        @pl.when(valid_ref[b] > r0)
        def _(r0=r0):
            part = slice(r0, r0 + MOE_PART_ROWS)
            lo, hi = _unpack_halves(x_ref[part, :])
            gu = _mm(lo, wgu_b[0:half, :]) + _mm(hi, wgu_b[half:, :])
            y_ref[part, :] = _pack_halves(_mm(_swiglu_act(gu), wdn_b[...]))


def _moe_rows(xs, schedule, lw, layer, n_blocks, rows):
    half = D_MODEL // 2
    assert rows % MOE_PART_ROWS == 0
    last = lambda b, be, first, slot, nxt, nu, valid: jnp.minimum(b, nu[0] - 1)
    row_spec = pl.BlockSpec((rows, half), lambda *a: (last(*a), 0))
    return pl.pallas_call(
        functools.partial(_moe_rows_kernel, layer=layer),
        grid_spec=pltpu.PrefetchScalarGridSpec(
            num_scalar_prefetch=6,
            grid=(n_blocks,),
            in_specs=[row_spec, pl.BlockSpec(memory_space=pl.ANY), pl.BlockSpec(memory_space=pl.ANY)],
            out_specs=row_spec,
            scratch_shapes=[
                pltpu.VMEM((2, D_MODEL, 2 * D_EXPERT), F32),
                pltpu.VMEM((2, D_EXPERT, D_MODEL), F32),
                pltpu.VMEM((D_MODEL, 2 * D_EXPERT), BF16),
                pltpu.VMEM((D_EXPERT, D_MODEL), BF16),
                pltpu.SemaphoreType.DMA((2, 2)),
            ],
        ),
        out_shape=jax.ShapeDtypeStruct(xs.shape, jnp.int32),
        compiler_params=pltpu.CompilerParams(dimension_semantics=("arbitrary",),
                                             vmem_limit_bytes=VMEM_LIMIT),
        name="moe_rows",
    )(*schedule, xs, lw["moe_w_gu"], lw["moe_w_down"])


def _moe_combine_kernel(yg_ref, ew_ref, x1_ref, mod_ref, nf_ref, o_ref, *, final):
    ew = ew_ref[...]
    acc_lo = acc_hi = None
    for k in range(TOP_K):
        lo, hi = _unpack_halves(yg_ref[k])
        wk = ew[:, k:k + 1]
        acc_lo = wk * lo if acc_lo is None else acc_lo + wk * lo
        acc_hi = wk * hi if acc_hi is None else acc_hi + wk * hi
    g2 = mod_ref[0][:, 5 * D_MODEL:]
    x2 = x1_ref[...] + g2 * jnp.concatenate([acc_lo, acc_hi], axis=1)
    if final:
        x2 = _rms(x2, nf_ref[...])
    o_ref[...] = x2


def _moe_combine(yg, row0, ew, x1, mods, layer, norm_f, tm, seq_len, mod_base, final):
    n = x1.shape[0]
    half = D_MODEL // 2
    tiles_per_seq = seq_len // tm if mod_base else 1
    tile0 = row0 // tm

    def mod_idx(i):
        return (layer, mod_base + i // tiles_per_seq if mod_base else 0, 0, 0)

    row = lambda w: pl.BlockSpec((tm, w), lambda i: (i, 0))
    return pl.pallas_call(
        functools.partial(_moe_combine_kernel, final=final),
        grid=(n // tm,),
        in_specs=[pl.BlockSpec((TOP_K, tm, half), lambda i: (0, tile0 + i, 0)), row(LANES), row(D_MODEL),
                  pl.BlockSpec((None, 1, 1, 6 * D_MODEL), mod_idx), _const_spec(norm_f)],
        out_specs=row(D_MODEL),
        out_shape=jax.ShapeDtypeStruct((n, D_MODEL), F32),
        compiler_params=pltpu.CompilerParams(dimension_semantics=("arbitrary",),
                                             vmem_limit_bytes=VMEM_LIMIT),
        name="moe_combine_final" if final else "moe_combine",
    )(yg, ew, x1, mods, norm_f)


def _moe_experts(groups, eid, rank, counts, lw, layer):
    group_sizes = [g.shape[0] for g in groups]
    n = sum(group_sizes)
    dest, schedule, n_blocks = _dispatch_plan(eid, rank, counts, n, MOE_ROWS)
    xs = _sc_dispatch(groups, dest, n_blocks * MOE_ROWS)
    y = _moe_rows(xs, schedule, lw, layer, n_blocks, MOE_ROWS)
    outs, start = [], 0
    for size in group_sizes:
        idx = dest[:, start:start + size].reshape(size * TOP_K // SC_ROWS, SC_ROWS)
        outs.append(_sc_gather(y, idx).reshape(TOP_K, size, D_MODEL // 2))
        start += size
    return outs


def _constants():
    idx = np.arange(MIX_W)
    same_head = (idx[:, None] // HEAD_DIM) == (idx[None, :] // HEAD_DIM)
    maskbd = jnp.asarray(same_head, BF16)
    eexp = np.zeros((LANES, 4 * MIX_W), np.float32)
    for blk in range(4):
        kind, d = divmod(blk, 2)
        for h in range(HEADS):
            src = kind * 2 * HEADS + d * HEADS + h
            eexp[src, blk * MIX_W + h * HEAD_DIM: blk * MIX_W + (h + 1) * HEAD_DIM] = 1.0
    tri = np.triu(np.ones((POST_TM, POST_TM), np.float32), 1)
    return {"maskbd": maskbd, "ones_bd": maskbd, "eexp": jnp.asarray(eexp, BF16), "tri_tokens": jnp.asarray(tri, BF16)}


def _rope_tables(n):
    rows = n // GRID_W
    row = jnp.repeat(jnp.arange(rows, dtype=F32), GRID_W)
    col = jnp.tile(jnp.arange(GRID_W, dtype=F32), rows)
    axis_dim = MLA_ROPE // 2
    inv = jnp.power(ROPE_BASE, -jnp.arange(0, axis_dim, 2, dtype=F32) / axis_dim)
    ang_r = row[:, None] * inv
    ang_c = col[:, None] * inv
    cr, sr, cc, sc = jnp.cos(ang_r), jnp.sin(ang_r), jnp.cos(ang_c), jnp.sin(ang_c)
    zeros = jnp.zeros((n, LANES - MLA_ROPE), F32)
    cos_t = jnp.concatenate([cr, cr, cc, cc, zeros], axis=1)
    sin_t = jnp.concatenate([-sr, sr, -sc, sc, zeros], axis=1)
    return cos_t, sin_t


def _stacked_weights(p):
    w_uq = p["mla_w_uq"].reshape(DEPTH, Q_LORA, MLA_HEADS, MLA_NOPE + MLA_ROPE)
    zq = jnp.zeros((DEPTH, Q_LORA, MLA_HEADS, QH_W - MLA_NOPE - MLA_ROPE), F32)
    w_uq_a = jnp.concatenate([w_uq, zq], axis=-1).reshape(DEPTH, Q_LORA, MLA_HEADS * QH_W).astype(BF16)
    w_uq_sw = jnp.concatenate([jnp.zeros((DEPTH, Q_LORA, MLA_HEADS, MLA_NOPE), F32),
                               w_uq[..., MLA_NOPE + _ROPE_SWAP], zq], axis=-1)
    w_uq_sw = w_uq_sw.reshape(DEPTH, Q_LORA, MLA_HEADS * QH_W).astype(BF16)

    def per_direction(w):
        half = jnp.zeros((DEPTH, 64, MIX_W), F32)
        return jnp.stack([jnp.concatenate([w[:, 0], half], axis=1),
                          jnp.concatenate([half, w[:, 1]], axis=1)], axis=1).astype(BF16)

    row = lambda v: v.reshape(DEPTH, 1, -1)
    pad_row = lambda v: jnp.pad(row(v), ((0, 0), (0, 0), (0, LANES - 2 * HEADS)))
    return {
        "norm1": row(p["norm1"]),
        "w_in": p["w_in"],
        "q_norm": row(p["mla_q_norm"]),
        "w_uq": w_uq_a, "w_uq_sw": w_uq_sw,
        "kv_norm": row(p["mla_kv_norm"]),
        "w_ukv": p["mla_w_ukv"].astype(BF16),
        "gdn_conv": p["gdn_conv"],
        "gdn_alog": pad_row(p["gdn_a_log"]),
        "gdn_dtb": pad_row(p["gdn_dt_bias"]),
        "gdn_norm": jnp.tile(row(p["gdn_norm"]), (1, 1, HEADS)),
        "rwkv_mu_prev": row(p["rwkv_mu_prev"]),
        "rwkv_mu_next": row(p["rwkv_mu_next"]),
        "rwkv_w0": p["rwkv_w0"],
        "rwkv_w2": per_direction(p["rwkv_w2"]),
        "rwkv_a0": p["rwkv_a0"],
        "rwkv_a2": per_direction(p["rwkv_a2"]),
        "rwkv_g2": p["rwkv_g2"].astype(BF16),
        "rwkv_k_k": row(p["rwkv_k_k"]),
        "rwkv_k_a": row(p["rwkv_k_a"]),
        "rwkv_r_k": row(p["rwkv_r_k"]),
        "rwkv_gn_w": row(p["rwkv_gn_w"]),
        "rwkv_gn_b": row(p["rwkv_gn_b"]),
        "w_out": p["w_out"].astype(BF16),
        "norm2": row(p["norm2"]),
        "router_t": jnp.swapaxes(p["moe_router"], 1, 2),
        "router_b": p["moe_bias"].reshape(DEPTH, N_EXPERTS, 1),
        "moe_w_gu": p["moe_w_gu"],
        "moe_w_down": p["moe_w_down"],
        "shared_w_gu": p["shared_w_gu"].astype(BF16),
        "shared_w_down": p["shared_w_down"].astype(BF16),
    }


def _embed_block_diag(s):
    b = s.shape[0]
    eye = jnp.eye(HEADS, dtype=s.dtype)
    out = jnp.einsum("bdhkv,hg->bdhkgv", s, eye)
    return out.reshape(b, 2, MIX_W, MIX_W)


def _layer_front(x2d, mods, lw, l, consts, seq_len, mod_base, cache, tm, tq, counts_in, prev=None):
    zm, zg, zab, zr = _inproj(x2d, mods, lw, l, tm, seq_len, mod_base)
    if cache is None:
        o_mla, ckv, kpe = _mla_ctx(zm, lw, l, seq_len, prev[:2])
        o_gdn, s_gdn = _gdn(zg, zab, lw, l, consts, seq_len, prev=prev[2])
        o_rwkv, s_rwkv = _rwkv(zr, lw, l, consts, seq_len, prev=prev[3])
        new = (ckv, kpe, s_gdn, s_rwkv)
    else:
        cckv, ckpe, rc, rs, sg, sr = cache
        o_mla = _mla_lat(zm, lw, l, seq_len, tq, (cckv, ckpe, rc, rs))
        o_gdn, _ = _gdn(zg, zab, lw, l, consts, seq_len, sg)
        o_rwkv, _ = _rwkv(zr, lw, l, consts, seq_len, sr)
        new = None
    routed = _post(x2d, o_mla, o_gdn, o_rwkv, mods, lw, l, consts, tm, seq_len, mod_base, counts_in)
    return routed, new


def kernel(x_prompt, x_sample, cache_mla_ckv, cache_mla_kpe, state_gdn, state_rwkv, c, c_ctx, ada_w, ada_b, norm1, w_in, mla_q_norm, mla_w_uq, mla_kv_norm, mla_w_ukv, gdn_conv, gdn_a_log, gdn_dt_bias, gdn_norm, rwkv_mu_prev, rwkv_mu_next, rwkv_w0, rwkv_w2, rwkv_a0, rwkv_a2, rwkv_g2, rwkv_k_k, rwkv_k_a, rwkv_r_k, rwkv_gn_w, rwkv_gn_b, w_out, norm2, moe_router, moe_bias, moe_w_gu, moe_w_down, shared_w_gu, shared_w_down, norm_f):
    p = dict(norm1=norm1, w_in=w_in, mla_q_norm=mla_q_norm, mla_w_uq=mla_w_uq, mla_kv_norm=mla_kv_norm,
             mla_w_ukv=mla_w_ukv, gdn_conv=gdn_conv, gdn_a_log=gdn_a_log, gdn_dt_bias=gdn_dt_bias,
             gdn_norm=gdn_norm, rwkv_mu_prev=rwkv_mu_prev, rwkv_mu_next=rwkv_mu_next, rwkv_w0=rwkv_w0,
             rwkv_w2=rwkv_w2, rwkv_a0=rwkv_a0, rwkv_a2=rwkv_a2, rwkv_g2=rwkv_g2, rwkv_k_k=rwkv_k_k,
             rwkv_k_a=rwkv_k_a, rwkv_r_k=rwkv_r_k, rwkv_gn_w=rwkv_gn_w, rwkv_gn_b=rwkv_gn_b, w_out=w_out,
             norm2=norm2, moe_router=moe_router, moe_bias=moe_bias, moe_w_gu=moe_w_gu, moe_w_down=moe_w_down,
             shared_w_gu=shared_w_gu, shared_w_down=shared_w_down)
    weights = _stacked_weights(p)
    consts = _constants()
    nf = norm_f.reshape(1, D_MODEL)
    b_ctx, t_ctx, _ = x_prompt.shape
    b_lat, t_lat, _ = x_sample.shape

    cvec8 = jnp.concatenate([c_ctx[None, :], c, jnp.zeros((8 - 1 - b_lat, D_MODEL), F32)], axis=0)
    mods = _adaln(cvec8, ada_w, ada_b)
    mods = mods.reshape(DEPTH, 8, 1, 6 * D_MODEL)

    rc, rs = _rope_tables(t_lat)
    ckpe = jnp.pad(cache_mla_kpe, ((0, 0), (0, 0), (0, 0), (0, LANES - MLA_ROPE)))
    cache = (cache_mla_ckv, ckpe, rc, rs, _embed_block_diag_layers(state_gdn),
             _embed_block_diag_layers(jnp.swapaxes(state_rwkv, -1, -2)))
    xp = x_prompt.reshape(b_ctx * t_ctx, D_MODEL)
    xs = x_sample.reshape(b_lat * t_lat, D_MODEL)
    n_ctx = xp.shape[0]
    tm = POST_TM
    state_shape = (b_ctx, DEPTH, 2, HEADS, HEAD_DIM, HEAD_DIM)
    ctx_outs = (jnp.zeros((b_ctx, DEPTH, t_ctx, KV_LORA), F32), jnp.zeros((b_ctx, DEPTH, t_ctx, MLA_ROPE), F32),
                jnp.zeros(state_shape, F32), jnp.zeros(state_shape, F32))
    for l in range(DEPTH):
        final = l == DEPTH - 1
        no_pairs = jnp.zeros((N_EXPERTS, LANES), F32)
        (x1c, hc, eidc, rankc, ewc, cnt_c), ctx_outs = _layer_front(xp, mods, weights, l, consts, t_ctx, 0, None,
                                                                    tm, t_ctx, no_pairs, ctx_outs)
        (x1s, hs, eids, ranks, ews, cnt), _ = _layer_front(xs, mods, weights, l, consts, t_lat, 1, cache,
                                                           tm, 256, cnt_c)
        yg_c, yg_s = _moe_experts([hc, hs], jnp.concatenate([eidc, eids], axis=1),
                                  jnp.concatenate([rankc, ranks], axis=1), cnt, weights, l)
        xp = _moe_combine(yg_c, 0, ewc, x1c, mods, l, nf, tm, t_ctx, 0, final)
        xs = _moe_combine(yg_s, 0, ews, x1s, mods, l, nf, tm, t_lat, 1, final)

    y_prompt = xp.reshape(b_ctx, t_ctx, D_MODEL)
    y_sample = xs.reshape(b_lat, t_lat, D_MODEL)
    new_ckv, new_kpe, new_gdn, new_rwkv = ctx_outs
    return (y_prompt, y_sample, new_ckv, new_kpe, new_gdn, new_rwkv)


def _embed_block_diag_layers(s):
    b = s.shape[0]
    return _embed_block_diag(s.reshape(b * DEPTH, 2, HEADS, HEAD_DIM, HEAD_DIM)).reshape(
        b, DEPTH, 2, MIX_W, MIX_W)
```

```python
import functools

import numpy as np
import jax
import jax.numpy as jnp
from jax import lax
from jax.experimental import pallas as pl
from jax.experimental.pallas import tpu as pltpu
from jax.experimental.pallas import tpu_sc as plsc

F32 = jnp.float32
BF16 = jnp.bfloat16

D_MODEL = 1024
DEPTH = 2
PAST_LEN = 512
GRID_W = 64
NORM_EPS = 1e-6

MLA_HEADS = 4
MLA_NOPE = 128
MLA_ROPE = 64
MLA_V = 128
Q_LORA = 384
KV_LORA = 256
ROPE_BASE = 10000.0
MLA_SCALE = (MLA_NOPE + MLA_ROPE) ** -0.5

HEADS = 4
HEAD_DIM = 64
MIX_W = HEADS * HEAD_DIM
GDN_CONV_CH = 3 * MIX_W
CHUNK = 64
RWKV_GN_EPS = 64e-5

N_EXPERTS = 64
TOP_K = 8
N_GROUPS = 8
GROUP_SIZE = N_EXPERTS // N_GROUPS
TOPK_GROUPS = 4
D_EXPERT = 256
ROUTE_SCALE = 2.5

P_MLA = Q_LORA + KV_LORA + MLA_ROPE
P_GDN = GDN_CONV_CH + MIX_W + 4 * HEADS
P_RWKV = 3 * MIX_W + 128 + 128 + 128

LANES = 128
ZM_W = Q_LORA + KV_LORA + 2 * LANES
ZG_W = GDN_CONV_CH + MIX_W
ZR_W = P_RWKV
QH_W = 2 * LANES
VMEM_LIMIT = 56 * 1024 * 1024
POST_TM = 512

_ROPE_SWAP = np.concatenate([np.arange(16, 32), np.arange(0, 16), np.arange(48, 64), np.arange(32, 48)])


def _sigmoid(x):
    return 1.0 / (1.0 + jnp.exp(-x))


def _silu(x):
    return x * _sigmoid(x)


def _softplus(x):
    return jnp.maximum(x, 0.0) + jnp.log(1.0 + jnp.exp(-jnp.abs(x)))


def _rms(x, g, eps=NORM_EPS):
    return x * lax.rsqrt(jnp.mean(x * x, axis=-1, keepdims=True) + eps) * g


def _mm(a, b):
    return jnp.dot(a.astype(BF16), b.astype(BF16), preferred_element_type=F32)


def _mm_nt(a, b):
    return lax.dot_general(a.astype(BF16), b.astype(BF16), (((1,), (1,)), ((), ())),
                           preferred_element_type=F32)


def _mm_tn(a, b):
    return lax.dot_general(a.astype(BF16), b.astype(BF16), (((0,), (0,)), ((), ())),
                           preferred_element_type=F32)


def _split3(x):
    p1 = x.astype(BF16)
    r1 = x - p1.astype(F32)
    p2 = r1.astype(BF16)
    r2 = r1 - p2.astype(F32)
    return p1, p2, r2.astype(BF16)


def _mm_sel_l(sel, x):
    p1, p2, p3 = _split3(x)
    return _mm(sel, p1) + _mm(sel, p2) + _mm(sel, p3)


def _mm_sel_r(x, sel):
    p1, p2, p3 = _split3(x)
    return _mm(p1, sel) + _mm(p2, sel) + _mm(p3, sel)


def _iota(shape, dim):
    return lax.broadcasted_iota(jnp.int32, shape, dim)


def _layer_spec(a, layer, **kw):
    nd = a.ndim - 1
    return pl.BlockSpec((None,) + a.shape[1:], lambda *_: (layer,) + (0,) * nd, **kw)


def _const_spec(a, **kw):
    return pl.BlockSpec(a.shape, lambda *_: (0,) * a.ndim, **kw)


def _bd(x, maskbd):
    xb = x.astype(BF16)
    return jnp.concatenate([xb] * HEADS, axis=0) * maskbd


def _chunk_masks(rev):
    row = _iota((CHUNK, MIX_W), 0)
    col = jnp.bitwise_and(_iota((CHUNK, MIX_W), 1), HEAD_DIM - 1)
    r2 = _iota((CHUNK, CHUNK), 0)
    c2 = _iota((CHUNK, CHUNK), 1)
    if rev:
        inc, strict, tri = row <= col, row < col, r2 <= c2
    else:
        inc, strict, tri = row >= col, row > col, r2 >= c2
    eye = jnp.where(row == col, 1.0, 0.0).astype(F32)
    return inc, strict, jnp.where(tri, 1.0, 0.0).astype(BF16), eye


def _split2(x):
    hi = x.astype(BF16)
    return hi, (x - hi.astype(F32)).astype(BF16)


def _mm_bd3(x, p, maskbd):
    n = x.shape[0]
    xh, xl = _split2(x)
    ph, pl_ = _split2(p)
    r = jnp.dot(jnp.concatenate([xh, xl], axis=0), _bd(ph, maskbd), preferred_element_type=F32)
    return r[:n] + r[n:] + jnp.dot(xh, _bd(pl_, maskbd), preferred_element_type=F32)


def _neumann_inverse(a_list, eye_list, maskbd):
    bs = [-a for a in a_list]
    ms = [eye + b for eye, b in zip(eye_list, bs)]
    ps = [_mm_bd3(b, b, maskbd) for b in bs]
    for _ in range(4):
        boths = [_mm_bd3(jnp.concatenate([m, p], axis=0), p, maskbd) for m, p in zip(ms, ps)]
        ms = [m + both[:CHUNK] for m, both in zip(ms, boths)]
        ps = [both[CHUNK:] for both in boths]
    return [m + _mm_bd3(m, p, maskbd) for m, p in zip(ms, ps)]


def _adaln_kernel(c_ref, w_ref, b_ref, o_ref):
    cv = c_ref[...]
    o_ref[0] = _mm(_silu(cv), w_ref[0]) + b_ref[0]


def _adaln(cvec8, ada_w, ada_b):
    tn = 768
    n_out = 6 * D_MODEL
    return pl.pallas_call(
        _adaln_kernel,
        grid=(DEPTH, n_out // tn),
        in_specs=[
            pl.BlockSpec((8, D_MODEL), lambda l, j: (0, 0)),
            pl.BlockSpec((1, D_MODEL, tn), lambda l, j: (l, 0, j)),
            pl.BlockSpec((1, 1, tn), lambda l, j: (l, 0, j)),
        ],
        out_specs=pl.BlockSpec((1, 8, tn), lambda l, j: (l, 0, j)),
        out_shape=jax.ShapeDtypeStruct((DEPTH, 8, n_out), F32),
        compiler_params=pltpu.CompilerParams(dimension_semantics=("arbitrary", "arbitrary"),
                                             vmem_limit_bytes=VMEM_LIMIT),
        name="adaln",
    )(cvec8, ada_w, ada_b.reshape(DEPTH, 1, n_out))


_KPE0 = Q_LORA + KV_LORA
_W_IN_MOVES = (
    [(0, 0, P_MLA)]
    + [(P_MLA + LANES - MLA_ROPE + 16 * j, _KPE0 + 16 * int(_ROPE_SWAP[16 * j] // 16), 16) for j in range(4)]
    + [(ZM_W, P_MLA, ZG_W), (ZM_W + ZG_W, P_MLA + ZG_W, 4 * HEADS), (ZM_W + ZG_W + LANES, P_MLA + P_GDN, P_RWKV)]
)
W_IN_PAD = ZM_W + ZG_W + LANES + ZR_W


def _inproj_kernel(x_ref, mod_ref, n1_ref, w_ref, zm_ref, zg_ref, zab_ref, zr_ref, w_s):
    @pl.when(pl.program_id(0) == 0)
    def _():
        w_s[...] = jnp.zeros(w_s.shape, BF16)
        for dst, src, width in _W_IN_MOVES:
            w_s[:, dst:dst + width] = w_ref[:, src:src + width].astype(BF16)

    m = mod_ref[0]
    sh = m[:, 0:D_MODEL]
    sc = m[:, D_MODEL:2 * D_MODEL]
    h = _rms(x_ref[...], n1_ref[...]) * (1.0 + sc) + sh
    z = _mm(h, w_s[...])
    o1 = ZM_W
    o2 = o1 + ZG_W
    o3 = o2 + LANES
    zm_ref[...] = z[:, :o1]
    zg_ref[...] = z[:, o1:o2]
    zab_ref[...] = z[:, o2:o3]
    zr_ref[...] = z[:, o3:]


def _inproj(x2d, mods, lw, layer, tm, seq_len, mod_base):
    n = x2d.shape[0]
    tiles_per_seq = seq_len // tm if mod_base else 1

    def mod_idx(i):
        return (layer, mod_base + i // tiles_per_seq if mod_base else 0, 0, 0)

    return pl.pallas_call(
        _inproj_kernel,
        grid=(n // tm,),
        in_specs=[
            pl.BlockSpec((tm, D_MODEL), lambda i: (i, 0)),
            pl.BlockSpec((None, 1, 1, 6 * D_MODEL), mod_idx),
            _layer_spec(lw["norm1"], layer),
            _layer_spec(lw["w_in"], layer, pipeline_mode=pl.Buffered(1)),
        ],
        out_specs=[
            pl.BlockSpec((tm, ZM_W), lambda i: (i, 0)),
            pl.BlockSpec((tm, ZG_W), lambda i: (i, 0)),
            pl.BlockSpec((tm, LANES), lambda i: (i, 0)),
            pl.BlockSpec((tm, ZR_W), lambda i: (i, 0)),
        ],
        out_shape=[
            jax.ShapeDtypeStruct((n, ZM_W), F32),
            jax.ShapeDtypeStruct((n, ZG_W), F32),
            jax.ShapeDtypeStruct((n, LANES), F32),
            jax.ShapeDtypeStruct((n, ZR_W), F32),
        ],
        scratch_shapes=[pltpu.VMEM((D_MODEL, W_IN_PAD), BF16)],
        compiler_params=pltpu.CompilerParams(dimension_semantics=("arbitrary",),
                                             vmem_limit_bytes=VMEM_LIMIT),
        name="inproj",
    )(x2d, mods, lw["norm1"], lw["w_in"])


MLA_SEQ_PER_STEP = 4


def _mla_ctx_kernel(zm_ref, qn_ref, wuq_ref, kvn_ref, wukv_ref, prev_ckv, prev_kpe, o_ref, ckv_ref, kpe_ref,
                    *, seq_len, n_seq):
    del prev_ckv, prev_kpe
    o_kpe = Q_LORA + KV_LORA
    zm = zm_ref[...]
    ckv = _rms(zm[:, Q_LORA:o_kpe], kvn_ref[...])
    kpe = zm[:, o_kpe:o_kpe + LANES]
    kv = _mm(ckv, wukv_ref[...])
    q = _mm(_rms(zm[:, :Q_LORA], qn_ref[...]), wuq_ref[...])
    kpe_b = kpe.astype(BF16)
    seqs = [slice(i * seq_len, (i + 1) * seq_len) for i in range(n_seq)]
    for i, r in enumerate(seqs):
        ckv_ref[i] = ckv[r]
        kpe_ref[i] = kpe[r, :MLA_ROPE]
    pairs = [(r, h * QH_W) for r in seqs for h in range(MLA_HEADS)]
    scores = [_mm_nt(q[r, c0:c0 + QH_W], jnp.concatenate([kv[r, c0:c0 + LANES].astype(BF16), kpe_b[r]], axis=1))
              * MLA_SCALE for r, c0 in pairs]
    exps = [jnp.exp(s - jnp.max(s, axis=-1, keepdims=True)) for s in scores]
    outs = [_mm(e, kv[r, c0 + LANES:c0 + QH_W]) / jnp.sum(e, axis=-1, keepdims=True)
            for e, (r, c0) in zip(exps, pairs)]
    for o, (r, c0) in zip(outs, pairs):
        h = c0 // QH_W
        o_ref[r, h * MLA_V:(h + 1) * MLA_V] = o


def _mla_ctx(zm, lw, layer, seq_len, prev):
    n = zm.shape[0]
    nb = n // seq_len
    n_seq = MLA_SEQ_PER_STEP
    rows = n_seq * seq_len
    weights = [lw["q_norm"], lw["w_uq"], lw["kv_norm"], lw["w_ukv"]]
    return pl.pallas_call(
        functools.partial(_mla_ctx_kernel, seq_len=seq_len, n_seq=n_seq),
        grid=(nb // n_seq,),
        in_specs=[pl.BlockSpec((rows, ZM_W), lambda b: (b, 0))] + [_layer_spec(a, layer) for a in weights]
        + [pl.BlockSpec(memory_space=pl.ANY)] * 2,
        out_specs=[
            pl.BlockSpec((rows, MLA_HEADS * MLA_V), lambda b: (b, 0)),
            pl.BlockSpec((n_seq, None, seq_len, KV_LORA), lambda b: (b, layer, 0, 0)),
            pl.BlockSpec((n_seq, None, seq_len, MLA_ROPE), lambda b: (b, layer, 0, 0)),
        ],
        out_shape=[
            jax.ShapeDtypeStruct((n, MLA_HEADS * MLA_V), F32),
            jax.ShapeDtypeStruct((nb, DEPTH, seq_len, KV_LORA), F32),
            jax.ShapeDtypeStruct((nb, DEPTH, seq_len, MLA_ROPE), F32),
        ],
        input_output_aliases={5: 1, 6: 2},
        compiler_params=pltpu.CompilerParams(dimension_semantics=("arbitrary",),
                                             vmem_limit_bytes=VMEM_LIMIT),
        name="mla_ctx",
    )(zm, *weights, *prev)


def _mla_lat_kernel(zm_ref, cckv_ref, ckpe_ref, rc_ref, rs_ref, qn_ref, wuq_ref, wuqs_ref, kvn_ref, wukv_ref,
                    o_ref, k_s, v_s, *, seq_len, tq, past):
    qi = pl.program_id(1)
    o_kpe = Q_LORA + KV_LORA

    @pl.when(qi == 0)
    def _():
        zm = zm_ref[...]
        ckv = _rms(zm[:, Q_LORA:o_kpe], kvn_ref[...])
        kpe = zm[:, o_kpe:o_kpe + LANES] * rc_ref[...] + zm[:, o_kpe + LANES:o_kpe + 2 * LANES] * rs_ref[...]
        kvc = _mm(cckv_ref[0], wukv_ref[...])
        kpc = ckpe_ref[0].astype(BF16)
        kv = _mm(ckv, wukv_ref[...])
        kpe = kpe.astype(BF16)
        for h in range(MLA_HEADS):
            c0 = h * QH_W
            k_s[0:past, c0:c0 + LANES] = kvc[:, c0:c0 + LANES].astype(BF16)
            k_s[0:past, c0 + LANES:c0 + QH_W] = kpc
            v_s[0:past, h * MLA_V:(h + 1) * MLA_V] = kvc[:, c0 + LANES:c0 + QH_W].astype(BF16)
            k_s[past:past + seq_len, c0:c0 + LANES] = kv[:, c0:c0 + LANES].astype(BF16)
            k_s[past:past + seq_len, c0 + LANES:c0 + QH_W] = kpe
            v_s[past:past + seq_len, h * MLA_V:(h + 1) * MLA_V] = kv[:, c0 + LANES:c0 + QH_W].astype(BF16)

    r0 = pl.multiple_of(qi * tq, tq)
    zq = zm_ref[pl.ds(r0, tq), :]
    cq = _rms(zq[:, :Q_LORA], qn_ref[...])
    q = _mm(cq, wuq_ref[...])
    qs = _mm(cq, wuqs_ref[...])
    qc = jnp.concatenate([jnp.ones((tq, LANES), F32), rc_ref[pl.ds(r0, tq), :]], axis=1)
    qsn = jnp.concatenate([jnp.zeros((tq, LANES), F32), rs_ref[pl.ds(r0, tq), :]], axis=1)
    for h in range(MLA_HEADS):
        c0 = h * QH_W
        qh = q[:, c0:c0 + QH_W] * qc + qs[:, c0:c0 + QH_W] * qsn
        s = _mm_nt(qh, k_s[:, c0:c0 + QH_W]) * MLA_SCALE
        e = jnp.exp(s - jnp.max(s, axis=-1, keepdims=True))
        den = jnp.sum(e, axis=-1, keepdims=True)
        o_ref[:, h * MLA_V:(h + 1) * MLA_V] = _mm(e, v_s[:, h * MLA_V:(h + 1) * MLA_V]) / den


def _mla_lat(zm, lw, layer, seq_len, tq, cache):
    n = zm.shape[0]
    nb = n // seq_len
    past = PAST_LEN
    tk = past + seq_len
    cckv, ckpe, rc, rs = cache
    weights = [lw["q_norm"], lw["w_uq"], lw["w_uq_sw"], lw["kv_norm"], lw["w_ukv"]]
    return pl.pallas_call(
        functools.partial(_mla_lat_kernel, seq_len=seq_len, tq=tq, past=past),
        grid=(nb, seq_len // tq),
        in_specs=[pl.BlockSpec((seq_len, ZM_W), lambda b, q: (b, 0)),
                  pl.BlockSpec((1, None, past, KV_LORA), lambda b, q: (b, layer, 0, 0)),
                  pl.BlockSpec((1, None, past, LANES), lambda b, q: (b, layer, 0, 0)),
                  _const_spec(rc), _const_spec(rs)] + [_layer_spec(a, layer) for a in weights],
        out_specs=pl.BlockSpec((tq, MLA_HEADS * MLA_V), lambda b, q: (b * (seq_len // tq) + q, 0)),
        out_shape=jax.ShapeDtypeStruct((n, MLA_HEADS * MLA_V), F32),
        scratch_shapes=[
            pltpu.VMEM((tk, MLA_HEADS * QH_W), BF16),
            pltpu.VMEM((tk, MLA_HEADS * MLA_V), BF16),
        ],
        compiler_params=pltpu.CompilerParams(dimension_semantics=("arbitrary", "arbitrary"),
                                             vmem_limit_bytes=VMEM_LIMIT),
        name="mla_lat",
    )(zm, cckv, ckpe, rc, rs, *weights)


SEQ_PER_STEP = 4
CHUNK_GROUP = 4
GDN_PRE_W = 5 * MIX_W


def _for_chunk_groups(n_chunks, fn):
    if n_chunks == CHUNK_GROUP:
        fn(0)
    else:
        def body(gi, carry):
            fn(gi * CHUNK_GROUP)
            return carry
        lax.fori_loop(0, n_chunks // CHUNK_GROUP, body, 0)


def _gdn_prepare(items, maskbd):
    n = range(len(items))
    qs, ks, vs, gs, betas, masks, revs = zip(*items)
    gcs = [_mm_sel_l(masks[i][2], gs[i]) for i in n]
    decays = []
    for i in n:
        inc, eye = masks[i][0], masks[i][3]
        gc_row = jnp.sum(eye * gcs[i], axis=0, keepdims=True)
        decays.append(jnp.where(inc, jnp.exp(jnp.where(inc, gcs[i] - gc_row, 0.0)), 0.0))
    kbs = [ks[i] * betas[i] for i in n]
    aqs = [_mm_nt(jnp.concatenate([kbs[i], qs[i]], axis=0), _bd(ks[i], maskbd)) for i in n]
    a_mats = [jnp.where(masks[i][1], aqs[i][:CHUNK] * decays[i], 0.0) for i in n]
    t_invs = _neumann_inverse(a_mats, [m[3] for m in masks], maskbd)
    egcs = [jnp.exp(gc) for gc in gcs]
    uws = [_mm(t_invs[i], jnp.concatenate([_bd(vs[i] * betas[i], maskbd), _bd(kbs[i] * egcs[i], maskbd)], axis=1))
           for i in n]
    out = []
    for i in n:
        g_last = gcs[i][0:1] if revs[i] else gcs[i][CHUNK - 1:CHUNK]
        pre = jnp.concatenate([uws[i], qs[i] * egcs[i], aqs[i][CHUNK:] * decays[i],
                               ks[i] * jnp.exp(g_last - gcs[i])], axis=1)
        out.append((pre, jnp.broadcast_to(jnp.exp(g_last), (8, MIX_W))))
    return out


def _gdn_step(items, maskbd):
    w = MIX_W
    n = range(len(items))
    pres, egls, states = zip(*items)
    wqs = [_mm(jnp.concatenate([pres[i][:, w:2 * w], pres[i][:, 2 * w:3 * w]], axis=0), states[i]) for i in n]
    v_news = [pres[i][:, :w] - wqs[i][:CHUNK] for i in n]
    outs = [wqs[i][CHUNK:] + _mm(pres[i][:, 3 * w:4 * w], _bd(v_news[i], maskbd)) for i in n]
    upds = [_mm_tn(pres[i][:, 4 * w:], v_news[i]) for i in n]
    mask_f = maskbd.astype(F32)
    return [(outs[i], states[i] * egls[i] + upds[i] * mask_f) for i in n]


def _gdn_kernel(*refs, seq_len, n_seq, cached):
    if cached:
        (zg_ref, zab_ref, s0_ref, conv_ref, alog_ref, dtb_ref, gn_ref, eexp_ref, ones_ref, maskbd_ref,
         o_ref, sout_ref, q_s, k_s, v_s, ge_s, pre_s, gl_s, oacc_s, st_s) = refs
    else:
        zg_ref, zab_ref, conv_ref, alog_ref, dtb_ref, gn_ref, eexp_ref, ones_ref, maskbd_ref = refs[:9]
        o_ref, sout_ref, q_s, k_s, v_s, ge_s, pre_s, gl_s, oacc_s, st_s = refs[-10:]
    t = seq_len * n_seq
    z = zg_ref[:, :GDN_CONV_CH]
    rowi = jnp.bitwise_and(_iota((t, 1), 0), seq_len - 1)
    zp = jnp.where(rowi == 0, 0.0, pltpu.roll(z, 1, 0))
    zn = jnp.where(rowi == seq_len - 1, 0.0, pltpu.roll(z, t - 1, 0))
    cw = conv_ref[...]
    qkv = _silu(zp * cw[0:1] + z * cw[1:2] + zn * cw[2:3])
    ones_bd = ones_ref[...]
    q = qkv[:, :MIX_W]
    k = qkv[:, MIX_W:2 * MIX_W]
    q_s[...] = q * lax.rsqrt(_mm_sel_r(q * q, ones_bd) + 1e-6) * (HEAD_DIM ** -0.5)
    k_s[...] = k * lax.rsqrt(_mm_sel_r(k * k, ones_bd) + 1e-6)
    v_s[...] = qkv[:, 2 * MIX_W:]
    ab = zab_ref[...]
    lane = _iota((t, LANES), 1)
    gb = jnp.where(lane < 2 * HEADS, -jnp.exp(alog_ref[...]) * _softplus(ab + dtb_ref[...]), _sigmoid(ab))
    ge_s[...] = _mm_sel_r(gb, eexp_ref[...])
    oacc_s[...] = jnp.zeros((t, MIX_W), F32)
    if cached:
        st_s[...] = s0_ref[...]
    else:
        st_s[...] = jnp.zeros((n_seq, 2, MIX_W, MIX_W), F32)
    maskbd = maskbd_ref[...]
    masks = (_chunk_masks(False), _chunk_masks(True))
    n_chunks = seq_len // CHUNK

    def prepare_group(c0):
        where, items = [], []
        for j in range(CHUNK_GROUP):
            c = c0 + j
            rows = pl.ds(pl.multiple_of(c * CHUNK, CHUNK), CHUNK)
            for d in range(2):
                where.append((d, c, rows))
                items.append((q_s[rows, :], k_s[rows, :], v_s[rows, :], ge_s[rows, d * MIX_W:(d + 1) * MIX_W],
                              ge_s[rows, (2 + d) * MIX_W:(3 + d) * MIX_W], masks[d], d == 1))
        for (d, c, rows), (pre, egl) in zip(where, _gdn_prepare(items, maskbd)):
            pre_s[d, rows, :] = pre
            gl_s[d, pl.ds(pl.multiple_of(c * 8, 8), 8), :] = egl

    _for_chunk_groups(n_chunks * n_seq, prepare_group)

    def body(i, carry):
        where, items = [], []
        for q in range(n_seq):
            for d in range(2):
                c = q * n_chunks + (i if d == 0 else n_chunks - 1 - i)
                rows = pl.ds(pl.multiple_of(c * CHUNK, CHUNK), CHUNK)
                where.append((q, d, rows))
                items.append((pre_s[d, rows, :], gl_s[d, pl.ds(pl.multiple_of(c * 8, 8), 8), :][0:1], st_s[q, d]))
        for (q, d, rows), (o, s_new) in zip(where, _gdn_step(items, maskbd)):
            oacc_s[rows, :] = oacc_s[rows, :] + o
            st_s[q, d] = s_new
        return carry

    lax.fori_loop(0, n_chunks, body, 0)
    o = oacc_s[...]
    ms = _mm_sel_r(o * o, ones_bd) * (1.0 / HEAD_DIM)
    gate = zg_ref[:, GDN_CONV_CH:]
    o_ref[...] = o * lax.rsqrt(ms + NORM_EPS) * gn_ref[...] * _silu(gate)
    for q in range(n_seq):
        for d in range(2):
            for h in range(HEADS):
                sl = slice(h * HEAD_DIM, (h + 1) * HEAD_DIM)
                sout_ref[q, d, h] = st_s[q, d, sl, sl]


def _state_out(nb, n_seq, layer, cached, prev, args, in_specs):
    if cached:
        return (pl.BlockSpec((n_seq, 2, HEADS, HEAD_DIM, HEAD_DIM), lambda b: (b, 0, 0, 0, 0)),
                jax.ShapeDtypeStruct((nb, 2, HEADS, HEAD_DIM, HEAD_DIM), F32), {})
    aliases = {len(args): 1}
    args.append(prev)
    in_specs.append(pl.BlockSpec(memory_space=pl.ANY))
    return (pl.BlockSpec((n_seq, None, 2, HEADS, HEAD_DIM, HEAD_DIM), lambda b: (b, layer, 0, 0, 0, 0)),
            jax.ShapeDtypeStruct((nb, DEPTH, 2, HEADS, HEAD_DIM, HEAD_DIM), F32), aliases)


def _gdn(zg, zab, lw, layer, consts, seq_len, s0_bd=None, prev=None):
    n = zg.shape[0]
    nb = n // seq_len
    cached = s0_bd is not None
    n_seq = 1 if cached else SEQ_PER_STEP
    rows = seq_len * n_seq
    args = [zg, zab]
    in_specs = [pl.BlockSpec((rows, ZG_W), lambda b: (b, 0)),
                pl.BlockSpec((rows, LANES), lambda b: (b, 0))]
    if cached:
        args.append(s0_bd)
        in_specs.append(pl.BlockSpec((1, None, 2, MIX_W, MIX_W), lambda b: (b, layer, 0, 0, 0)))
    layered = [lw["gdn_conv"], lw["gdn_alog"], lw["gdn_dtb"], lw["gdn_norm"]]
    const = [consts["eexp"], consts["ones_bd"], consts["maskbd"]]
    args += layered + const
    in_specs += [_layer_spec(a, layer) for a in layered] + [_const_spec(a) for a in const]
    s_spec, s_shape, aliases = _state_out(nb, n_seq, layer, cached, prev, args, in_specs)
    return pl.pallas_call(
        functools.partial(_gdn_kernel, seq_len=seq_len, n_seq=n_seq, cached=cached),
        grid=(nb // n_seq,),
        in_specs=in_specs,
        out_specs=[
            pl.BlockSpec((rows, MIX_W), lambda b: (b, 0)),
            s_spec,
        ],
        out_shape=[jax.ShapeDtypeStruct((n, MIX_W), F32), s_shape],
        input_output_aliases=aliases,
        scratch_shapes=[
            pltpu.VMEM((rows, MIX_W), F32),
            pltpu.VMEM((rows, MIX_W), F32),
            pltpu.VMEM((rows, MIX_W), F32),
            pltpu.VMEM((rows, 4 * MIX_W), F32),
            pltpu.VMEM((2, rows, GDN_PRE_W), F32),
            pltpu.VMEM((2, rows // CHUNK * 8, MIX_W), F32),
            pltpu.VMEM((rows, MIX_W), F32),
            pltpu.VMEM((n_seq, 2, MIX_W, MIX_W), F32),
        ],
        compiler_params=pltpu.CompilerParams(dimension_semantics=("arbitrary",),
                                             vmem_limit_bytes=VMEM_LIMIT),
        name="gdn_lat" if cached else "gdn_ctx",
    )(*args)


RWKV_PRE_W = 7 * MIX_W


def _rwkv_prepare(items, maskbd, eye_full):
    n = range(len(items))
    rs, kds, vs, kks, bs, lws, masks, revs = zip(*items)
    cums = [_mm_sel_l(masks[i][2], lws[i]) for i in n]
    einvs = [jnp.exp(-c) for c in cums]
    kts = [kks[i] * jnp.exp(cums[i] - lws[i]) for i in n]
    rts = [rs[i] * jnp.exp(cums[i]) for i in n]
    krs = [jnp.concatenate([kts[i], rts[i]], axis=0) for i in n]
    lb_alls = [_mm_nt(krs[i], _bd(bs[i] * einvs[i], maskbd)) for i in n]
    lk_alls = [_mm_nt(krs[i], _bd(kds[i] * einvs[i], maskbd)) for i in n]
    lbs = [jnp.where(masks[i][1], lb_alls[i][:CHUNK], 0.0) for i in n]
    t_invs = _neumann_inverse(lbs, [m[3] for m in masks], maskbd)
    lvs = [_mm(jnp.concatenate([jnp.where(masks[i][1], lk_alls[i][:CHUNK], 0.0),
                                jnp.where(masks[i][0], lk_alls[i][CHUNK:], 0.0)], axis=0), _bd(vs[i], maskbd))
           for i in n]
    tkps = [_mm(t_invs[i], jnp.concatenate([_bd(kts[i], maskbd), _bd(lvs[i][:CHUNK], maskbd)], axis=1)) for i in n]
    out = []
    for i in n:
        c_last = cums[i][0:1] if revs[i] else cums[i][CHUNK - 1:CHUNK]
        tail = jnp.exp(c_last - cums[i])
        rb = jnp.where(masks[i][0], lb_alls[i][CHUNK:], 0.0)
        pre = jnp.concatenate([tkps[i][:, :MIX_W], rts[i], tkps[i][:, MIX_W:], lvs[i][CHUNK:], rb,
                               kds[i] * tail, bs[i] * tail], axis=1)
        gcol = jnp.sum(eye_full * jnp.exp(c_last), axis=1, keepdims=True)
        out.append((pre, jnp.broadcast_to(gcol, (MIX_W, LANES))))
    return out


def _rwkv_step(items, maskbd):
    w = MIX_W
    n = range(len(items))
    pres, vs, gcols, states = zip(*items)
    prs = [_mm(jnp.concatenate([pres[i][:, :w], pres[i][:, w:2 * w]], axis=0), states[i]) for i in n]
    ps = [prs[i][:CHUNK] + pres[i][:, 2 * w:3 * w] for i in n]
    outs = [prs[i][CHUNK:] + pres[i][:, 3 * w:4 * w] - _mm(pres[i][:, 4 * w:5 * w], _bd(ps[i], maskbd)) for i in n]
    upds = [_mm_tn(jnp.concatenate([pres[i][:, 5 * w:6 * w], pres[i][:, 6 * w:]], axis=0),
                   jnp.concatenate([vs[i], -ps[i]], axis=0)) for i in n]
    mask_f = maskbd.astype(F32)
    return [(outs[i], states[i] * jnp.concatenate([gcols[i], gcols[i]], axis=1) + upds[i] * mask_f) for i in n]


def _rwkv_kernel(*refs, seq_len, n_seq, cached):
    if cached:
        (zr_ref, s0_ref, mup_ref, mun_ref, w0_ref, w2_ref, a0_ref, a2_ref, g2_ref, kk_ref, ka_ref, rk_ref,
         gnw_ref, gnb_ref, ones_ref, maskbd_ref, o_ref, sout_ref,
         r_s, v_s, kk_s, dir_s, bg_s, pre_s, gcol_s, oacc_s, st_s) = refs
    else:
        (zr_ref, mup_ref, mun_ref, w0_ref, w2_ref, a0_ref, a2_ref, g2_ref, kk_ref, ka_ref, rk_ref,
         gnw_ref, gnb_ref, ones_ref, maskbd_ref) = refs[:15]
        o_ref, sout_ref, r_s, v_s, kk_s, dir_s, bg_s, pre_s, gcol_s, oacc_s, st_s = refs[-11:]
    t = seq_len * n_seq
    z = zr_ref[...]
    rowi = jnp.bitwise_and(_iota((t, 1), 0), seq_len - 1)
    zp = jnp.where(rowi == 0, 0.0, pltpu.roll(z, 1, 0))
    zn = jnp.where(rowi == seq_len - 1, 0.0, pltpu.roll(z, t - 1, 0))
    z = z + mup_ref[...] * (zp - z) + mun_ref[...] * (zn - z)
    w = MIX_W
    r = z[:, :w]
    k = z[:, w:2 * w]
    v = z[:, 2 * w:3 * w]
    wd = jnp.tanh(z[:, 3 * w:3 * w + LANES])
    ad = z[:, 3 * w + LANES:3 * w + 2 * LANES]
    gd = _sigmoid(z[:, 3 * w + 2 * LANES:])
    ones_bd = ones_ref[...]
    kk = k * kk_ref[...]
    kk = kk * lax.rsqrt(_mm_sel_r(kk * kk, ones_bd) + 1e-6)
    r_s[...] = r
    v_s[...] = v
    kk_s[...] = kk
    bonus = jnp.zeros((t, w), F32)
    for d in range(2):
        w_log = -_softplus(-(w0_ref[d:d + 1] + _mm(wd, w2_ref[d]))) - 0.5
        a = _sigmoid(a0_ref[d:d + 1] + _mm(ad, a2_ref[d]))
        kd = k * (1.0 + (a - 1.0) * ka_ref[...])
        dir_s[:, (3 * d) * w:(3 * d + 1) * w] = -jnp.exp(w_log)
        dir_s[:, (3 * d + 1) * w:(3 * d + 2) * w] = kd
        dir_s[:, (3 * d + 2) * w:(3 * d + 3) * w] = kk * a
        bonus = bonus + _mm_sel_r(r * kd * rk_ref[...], ones_bd) * v
    bg_s[:, :w] = bonus
    bg_s[:, w:] = _mm(gd, g2_ref[...])
    oacc_s[...] = jnp.zeros((t, w), F32)
    if cached:
        st_s[...] = s0_ref[...]
    else:
        st_s[...] = jnp.zeros((n_seq, 2, w, w), F32)
    maskbd = maskbd_ref[...]
    masks = (_chunk_masks(False), _chunk_masks(True))
    eye_full = jnp.where(_iota((w, w), 0) == _iota((w, w), 1), 1.0, 0.0).astype(F32)
    n_chunks = seq_len // CHUNK

    def prepare_group(c0):
        where, items = [], []
        for j in range(CHUNK_GROUP):
            c = c0 + j
            rows = pl.ds(pl.multiple_of(c * CHUNK, CHUNK), CHUNK)
            for d in range(2):
                where.append((d, c, rows))
                items.append((r_s[rows, :], dir_s[rows, (3 * d + 1) * w:(3 * d + 2) * w], v_s[rows, :], kk_s[rows, :],
                              dir_s[rows, (3 * d + 2) * w:(3 * d + 3) * w], dir_s[rows, (3 * d) * w:(3 * d + 1) * w],
                              masks[d], d == 1))
        for (d, c, rows), (pre, gcol) in zip(where, _rwkv_prepare(items, maskbd, eye_full)):
            pre_s[d, rows, :] = pre
            gcol_s[d, pl.ds(pl.multiple_of(c * w, w), w), :] = gcol

    _for_chunk_groups(n_chunks * n_seq, prepare_group)

    def body(i, carry):
        where, items = [], []
        for q in range(n_seq):
            for d in range(2):
                c = q * n_chunks + (i if d == 0 else n_chunks - 1 - i)
                rows = pl.ds(pl.multiple_of(c * CHUNK, CHUNK), CHUNK)
                where.append((q, d, rows))
                items.append((pre_s[d, rows, :], v_s[rows, :], gcol_s[d, pl.ds(pl.multiple_of(c * w, w), w), :],
                              st_s[q, d]))
        for (q, d, rows), (o, z_new) in zip(where, _rwkv_step(items, maskbd)):
            oacc_s[rows, :] = oacc_s[rows, :] + o
            st_s[q, d] = z_new
        return carry

    lax.fori_loop(0, n_chunks, body, 0)
    o = oacc_s[...]
    inv_n = 1.0 / HEAD_DIM
    mu = _mm_sel_r(o, ones_bd) * inv_n
    oc = o - mu
    var = _mm_sel_r(oc * oc, ones_bd) * inv_n
    y = oc * lax.rsqrt(var + RWKV_GN_EPS) * gnw_ref[...] + gnb_ref[...]
    o_ref[...] = (y + bg_s[:, :w]) * bg_s[:, w:]
    for q in range(n_seq):
        for d in range(2):
            state = st_s[q, d] if cached else st_s[q, d].T
            for h in range(HEADS):
                sl = slice(h * HEAD_DIM, (h + 1) * HEAD_DIM)
                sout_ref[q, d, h] = state[sl, sl]


def _rwkv(zr, lw, layer, consts, seq_len, s0_bd=None, prev=None):
    n = zr.shape[0]
    nb = n // seq_len
    cached = s0_bd is not None
    n_seq = 1 if cached else SEQ_PER_STEP
    rows = seq_len * n_seq
    args = [zr]
    in_specs = [pl.BlockSpec((rows, ZR_W), lambda b: (b, 0))]
    if cached:
        args.append(s0_bd)
        in_specs.append(pl.BlockSpec((1, None, 2, MIX_W, MIX_W), lambda b: (b, layer, 0, 0, 0)))
    layered = [lw["rwkv_mu_prev"], lw["rwkv_mu_next"], lw["rwkv_w0"], lw["rwkv_w2"], lw["rwkv_a0"], lw["rwkv_a2"],
               lw["rwkv_g2"], lw["rwkv_k_k"], lw["rwkv_k_a"], lw["rwkv_r_k"], lw["rwkv_gn_w"], lw["rwkv_gn_b"]]
    const = [consts["ones_bd"], consts["maskbd"]]
    args += layered + const
    in_specs += [_layer_spec(a, layer) for a in layered] + [_const_spec(a) for a in const]
    s_spec, s_shape, aliases = _state_out(nb, n_seq, layer, cached, prev, args, in_specs)
    return pl.pallas_call(
        functools.partial(_rwkv_kernel, seq_len=seq_len, n_seq=n_seq, cached=cached),
        grid=(nb // n_seq,),
        in_specs=in_specs,
        out_specs=[
            pl.BlockSpec((rows, MIX_W), lambda b: (b, 0)),
            s_spec,
        ],
        out_shape=[jax.ShapeDtypeStruct((n, MIX_W), F32), s_shape],
        input_output_aliases=aliases,
        scratch_shapes=[
            pltpu.VMEM((rows, MIX_W), F32),
            pltpu.VMEM((rows, MIX_W), F32),
            pltpu.VMEM((rows, MIX_W), F32),
            pltpu.VMEM((rows, 6 * MIX_W), F32),
            pltpu.VMEM((rows, 2 * MIX_W), F32),
            pltpu.VMEM((2, rows, RWKV_PRE_W), F32),
            pltpu.VMEM((2, rows // CHUNK * MIX_W, LANES), F32),
            pltpu.VMEM((rows, MIX_W), F32),
            pltpu.VMEM((n_seq, 2, MIX_W, MIX_W), F32),
        ],
        compiler_params=pltpu.CompilerParams(dimension_semantics=("arbitrary",),
                                             vmem_limit_bytes=VMEM_LIMIT),
        name="rwkv_lat" if cached else "rwkv_ctx",
    )(*args)


def _route(logits_t, bias):
    tm = logits_t.shape[1]
    neg = -jnp.inf
    sc = _sigmoid(logits_t)
    sc3 = sc.reshape(N_GROUPS, GROUP_SIZE, tm)
    sel = (sc + bias).reshape(N_GROUPS, GROUP_SIZE, tm)
    si = _iota(sel.shape, 1).astype(F32)
    m1 = jnp.max(sel, axis=1, keepdims=True)
    f1 = jnp.min(jnp.where(sel == m1, si, float(GROUP_SIZE)), axis=1, keepdims=True)
    m2 = jnp.max(jnp.where(si == f1, neg, sel), axis=1, keepdims=True)
    grp = m1 + m2
    gi = _iota(grp.shape, 0).astype(F32)
    gsel = jnp.zeros(grp.shape, F32)
    for _ in range(TOPK_GROUPS):
        mx = jnp.max(grp, axis=0, keepdims=True)
        fi = jnp.min(jnp.where(grp == mx, gi, float(N_GROUPS)), axis=0, keepdims=True)
        hit = gi == fi
        gsel = jnp.where(hit, 1.0, gsel)
        grp = jnp.where(hit, neg, grp)
    cur = jnp.where(gsel > 0.0, sel, neg)
    ei = (_iota(cur.shape, 0) * GROUP_SIZE + _iota(cur.shape, 1)).astype(F32)
    chosen = jnp.zeros(cur.shape, F32)
    ids, wts = [], []
    for _ in range(TOP_K):
        mx = jnp.max(jnp.max(cur, axis=0, keepdims=True), axis=1, keepdims=True)
        fi = jnp.min(jnp.min(jnp.where(cur == mx, ei, float(N_EXPERTS)), axis=0, keepdims=True),
                     axis=1, keepdims=True)
        hit = ei == fi
        chosen = jnp.where(hit, 1.0, chosen)
        cur = jnp.where(hit, neg, cur)
        ids.append(fi.reshape(1, tm))
        wts.append(jnp.sum(jnp.sum(jnp.where(hit, sc3, 0.0), axis=0, keepdims=True), axis=1, keepdims=True)
                   .reshape(1, tm))
    w = jnp.concatenate(wts, axis=0)
    w = w / jnp.sum(w, axis=0, keepdims=True) * ROUTE_SCALE
    return chosen.reshape(N_EXPERTS, tm), jnp.concatenate(ids, axis=0), w


def _pack_halves(x):
    half = x.shape[1] // 2
    bits = lax.bitcast_convert_type(x.astype(BF16).astype(F32), jnp.int32)
    lo = lax.shift_right_logical(bits[:, :half], jnp.int32(16))
    return jnp.bitwise_or(lo, jnp.bitwise_and(bits[:, half:], jnp.int32(-65536)))


def _unpack_halves(word):
    lo = lax.bitcast_convert_type(lax.shift_left(word, jnp.int32(16)), F32)
    hi = lax.bitcast_convert_type(jnp.bitwise_and(word, jnp.int32(-65536)), F32)
    return lo, hi


def _post_kernel(x_ref, om_ref, og_ref, or_ref, mod_ref, wo_ref, n2_ref, rt_ref, rb_ref, sgu_ref, sdn_ref,
                 tri_ref, cin_ref, x1_ref, h2_ref, eid_ref, rank_ref, ew_ref, cnt_ref, carry_s):
    @pl.when(pl.program_id(0) == 0)
    def _():
        carry_s[...] = cin_ref[...]

    m = mod_ref[0]
    g1 = m[:, 2 * D_MODEL:3 * D_MODEL]
    sh2 = m[:, 3 * D_MODEL:4 * D_MODEL]
    sc2 = m[:, 4 * D_MODEL:5 * D_MODEL]
    w_mla = MLA_HEADS * MLA_V
    mix = (_mm(om_ref[...], wo_ref[0:w_mla, :]) + _mm(og_ref[...], wo_ref[w_mla:w_mla + MIX_W, :])
           + _mm(or_ref[...], wo_ref[w_mla + MIX_W:, :]))
    x1 = x_ref[...] + g1 * mix
    h2 = _rms(x1, n2_ref[...]) * (1.0 + sc2) + sh2
    h2_ref[...] = _pack_halves(h2)
    g2 = m[:, 5 * D_MODEL:]
    x1_ref[...] = x1 + g2 * _mm(_swiglu_act(_mm(h2, sgu_ref[...])), sdn_ref[...])
    r_hi, r_lo = _split2(rt_ref[...])
    h_hi, h_lo = _split2(h2)
    logits_t = _mm_nt(r_hi, h_hi) + _mm_nt(r_hi, h_lo) + _mm_nt(r_lo, h_hi)
    chosen, ids, w = _route(logits_t, rb_ref[...])
    tm = chosen.shape[1]
    rank_et = (carry_s[:, 0:1] + _mm(chosen, tri_ref[...])).reshape(N_GROUPS, GROUP_SIZE, tm)
    ei = (_iota(rank_et.shape, 0) * GROUP_SIZE + _iota(rank_et.shape, 1)).astype(F32)
    ranks = []
    for k in range(TOP_K):
        pick = jnp.where(ei == ids[k:k + 1].reshape(1, 1, tm), rank_et, 0.0)
        ranks.append(jnp.sum(jnp.sum(pick, axis=0, keepdims=True), axis=1, keepdims=True).reshape(1, tm))
    eid_ref[...] = ids.astype(jnp.int32)
    rank_ref[...] = jnp.concatenate(ranks, axis=0).astype(jnp.int32)
    ew_ref[...] = jnp.concatenate([w, jnp.zeros((LANES - TOP_K, tm), F32)], axis=0).T
    total = carry_s[...] + jnp.sum(chosen, axis=1, keepdims=True)
    carry_s[...] = total
    cnt_ref[...] = total


def _post(x2d, om, og, orw, mods, lw, layer, consts, tm, seq_len, mod_base, counts_in):
    n = x2d.shape[0]
    tiles_per_seq = seq_len // tm if mod_base else 1

    def mod_idx(i):
        return (layer, mod_base + i // tiles_per_seq if mod_base else 0, 0, 0)

    row = lambda w: pl.BlockSpec((tm, w), lambda i: (i, 0))
    col = lambda h: pl.BlockSpec((h, tm), lambda i: (0, i))
    full = lambda a: _layer_spec(a, layer)
    tail = [lw["w_out"], lw["norm2"], lw["router_t"], lw["router_b"], lw["shared_w_gu"], lw["shared_w_down"]]
    tri = consts["tri_tokens"]
    return pl.pallas_call(
        _post_kernel,
        grid=(n // tm,),
        in_specs=[row(D_MODEL), row(MLA_HEADS * MLA_V), row(MIX_W), row(MIX_W),
                  pl.BlockSpec((None, 1, 1, 6 * D_MODEL), mod_idx)] + [full(a) for a in tail]
        + [_const_spec(tri), _const_spec(counts_in)],
        out_specs=[row(D_MODEL), row(D_MODEL // 2), col(TOP_K), col(TOP_K), row(LANES),
                   pl.BlockSpec((N_EXPERTS, LANES), lambda i: (0, 0))],
        out_shape=[
            jax.ShapeDtypeStruct((n, D_MODEL), F32),
            jax.ShapeDtypeStruct((n, D_MODEL // 2), jnp.int32),
            jax.ShapeDtypeStruct((TOP_K, n), jnp.int32),
            jax.ShapeDtypeStruct((TOP_K, n), jnp.int32),
            jax.ShapeDtypeStruct((n, LANES), F32),
            jax.ShapeDtypeStruct((N_EXPERTS, LANES), F32),
        ],
        scratch_shapes=[pltpu.VMEM((N_EXPERTS, LANES), F32)],
        compiler_params=pltpu.CompilerParams(dimension_semantics=("arbitrary",),
                                             vmem_limit_bytes=VMEM_LIMIT),
        name="post",
    )(x2d, om, og, orw, mods, *tail, tri, counts_in)


MOE_ROWS = 768
SC_ROWS = 128
SC_SUBCORES = 32


def _swiglu_act(gu):
    return _silu(gu[:, :D_EXPERT]) * gu[:, D_EXPERT:]


def _dispatch_plan(eid, rank, counts, n, rows):
    n_blocks = n * TOP_K // rows + N_EXPERTS
    cnt = counts[:, 0].astype(jnp.int32)
    blocks = (cnt + rows - 1) // rows
    block_end = jnp.cumsum(blocks)
    offset = (block_end - blocks) * rows
    experts = jnp.arange(N_EXPERTS, dtype=jnp.int32)
    dest = jnp.sum(jnp.where(eid[..., None] == experts, offset, 0), axis=-1) + rank
    block_ids = jnp.arange(n_blocks, dtype=jnp.int32)
    block_expert = jnp.minimum(jnp.sum((block_end[None, :] <= block_ids[:, None]).astype(jnp.int32), axis=1),
                               N_EXPERTS - 1)
    n_used = block_end[-1:]
    prev_expert = jnp.concatenate([jnp.full((1,), -1, jnp.int32), block_expert[:-1]])
    first = ((block_ids < n_used[0]) & (block_expert != prev_expert)).astype(jnp.int32)
    slot = jnp.bitwise_and(jnp.cumsum(first) - 1, 1)
    owner_or_none = jnp.where(blocks > 0, experts, N_EXPERTS)
    next_owner = jnp.concatenate([lax.cummin(owner_or_none[::-1])[::-1][1:], jnp.full((1,), N_EXPERTS, jnp.int32)])
    nxt = jnp.sum(jnp.where(block_expert[:, None] == experts, next_owner, 0), axis=1)
    return dest, (block_expert, first, slot.astype(jnp.int32), nxt.astype(jnp.int32), n_used.astype(jnp.int32)), n_blocks


def _sc_mesh():
    return plsc.VectorSubcoreMesh(core_axis_name="core", subcore_axis_name="subcore")


def _sc_dispatch(groups, dest, n_rows):
    w = groups[0].shape[1]
    dtype = groups[0].dtype

    @functools.partial(pl.kernel, out_type=jax.ShapeDtypeStruct((n_rows, w), dtype), mesh=_sc_mesh(),
                       scratch_types=[pltpu.VMEM((SC_ROWS, w), dtype), pltpu.VMEM((TOP_K, SC_ROWS), jnp.int32)])
    def kern(*refs):
        x_refs, d_hbm, o_hbm, xv, dv = refs[:len(groups)], *refs[len(groups):]
        sid = lax.axis_index("core") * (SC_SUBCORES // 2) + lax.axis_index("subcore")
        start = 0
        for x_hbm, x in zip(x_refs, groups):
            def chunk(c, x_hbm=x_hbm, start=start):
                r0 = pl.multiple_of(c * SC_ROWS, SC_ROWS)
                pltpu.sync_copy(x_hbm.at[pl.ds(r0, SC_ROWS)], xv)
                pltpu.sync_copy(d_hbm.at[:, pl.ds(start + r0, SC_ROWS)], dv)
                for k in range(TOP_K):
                    pltpu.sync_copy(xv, o_hbm.at[dv.at[k]])

            pl.loop(sid, x.shape[0] // SC_ROWS, step=SC_SUBCORES)(chunk)
            start += x.shape[0]

    return kern(*groups, dest)


def _sc_gather(y, idx):
    w = y.shape[1]
    n_chunks = idx.shape[0]

    @functools.partial(pl.kernel, out_type=jax.ShapeDtypeStruct((n_chunks * SC_ROWS, w), y.dtype), mesh=_sc_mesh(),
                       scratch_types=[pltpu.VMEM((SC_ROWS, w), y.dtype), pltpu.VMEM((1, SC_ROWS), jnp.int32)])
    def kern(y_hbm, i_hbm, o_hbm, ov, iv):
        sid = lax.axis_index("core") * (SC_SUBCORES // 2) + lax.axis_index("subcore")

        @pl.loop(sid, n_chunks, step=SC_SUBCORES)
        def _(c):
            pltpu.sync_copy(i_hbm.at[pl.ds(c, 1)], iv)
            pltpu.sync_copy(y_hbm.at[iv.at[0]], ov)
            pltpu.sync_copy(ov, o_hbm.at[pl.ds(pl.multiple_of(c * SC_ROWS, SC_ROWS), SC_ROWS)])

    return kern(y, idx)


def _moe_rows_kernel(be_ref, first_ref, slot_ref, nxt_ref, nu_ref, x_ref, wgu_hbm, wdn_hbm, y_ref,
                     wgu_f, wdn_f, wgu_b, wdn_b, sem, *, layer):
    b = pl.program_id(0)

    def weight_copies(expert, slot):
        return (pltpu.make_async_copy(wgu_hbm.at[layer, expert], wgu_f.at[slot], sem.at[slot, 0]),
                pltpu.make_async_copy(wdn_hbm.at[layer, expert], wdn_f.at[slot], sem.at[slot, 1]))

    @pl.when(b == 0)
    def _():
        for copy in weight_copies(be_ref[0], 0):
            copy.start()

    @pl.when(first_ref[b] == 1)
    def _():
        slot = slot_ref[b]
        for copy in weight_copies(be_ref[b], slot):
            copy.wait()

        @pl.when(nxt_ref[b] < N_EXPERTS)
        def _():
            for copy in weight_copies(nxt_ref[b], 1 - slot):
                copy.start()

        wgu_b[...] = wgu_f[slot].astype(BF16)
        wdn_b[...] = wdn_f[slot].astype(BF16)

    @pl.when(b < nu_ref[0])
    def _():
        half = D_MODEL // 2
        lo, hi = _unpack_halves(x_ref[...])
        gu = _mm(lo, wgu_b[0:half, :]) + _mm(hi, wgu_b[half:, :])
        y_ref[...] = _pack_halves(_mm(_swiglu_act(gu), wdn_b[...]))


def _moe_rows(xs, schedule, lw, layer, n_blocks, rows):
    half = D_MODEL // 2
    last = lambda b, be, first, slot, nxt, nu: jnp.minimum(b, nu[0] - 1)
    row_spec = pl.BlockSpec((rows, half), lambda *a: (last(*a), 0))
    return pl.pallas_call(
        functools.partial(_moe_rows_kernel, layer=layer),
        grid_spec=pltpu.PrefetchScalarGridSpec(
            num_scalar_prefetch=5,
            grid=(n_blocks,),
            in_specs=[row_spec, pl.BlockSpec(memory_space=pl.ANY), pl.BlockSpec(memory_space=pl.ANY)],
            out_specs=row_spec,
            scratch_shapes=[
                pltpu.VMEM((2, D_MODEL, 2 * D_EXPERT), F32),
                pltpu.VMEM((2, D_EXPERT, D_MODEL), F32),
                pltpu.VMEM((D_MODEL, 2 * D_EXPERT), BF16),
                pltpu.VMEM((D_EXPERT, D_MODEL), BF16),
                pltpu.SemaphoreType.DMA((2, 2)),
            ],
        ),
        out_shape=jax.ShapeDtypeStruct(xs.shape, jnp.int32),
        compiler_params=pltpu.CompilerParams(dimension_semantics=("arbitrary",),
                                             vmem_limit_bytes=VMEM_LIMIT),
        name="moe_rows",
    )(*schedule, xs, lw["moe_w_gu"], lw["moe_w_down"])


def _moe_combine_kernel(yg_ref, ew_ref, x1_ref, mod_ref, nf_ref, o_ref, *, final):
    ew = ew_ref[...]
    acc_lo = acc_hi = None
    for k in range(TOP_K):
        lo, hi = _unpack_halves(yg_ref[k])
        wk = ew[:, k:k + 1]
        acc_lo = wk * lo if acc_lo is None else acc_lo + wk * lo
        acc_hi = wk * hi if acc_hi is None else acc_hi + wk * hi
    g2 = mod_ref[0][:, 5 * D_MODEL:]
    x2 = x1_ref[...] + g2 * jnp.concatenate([acc_lo, acc_hi], axis=1)
    if final:
        x2 = _rms(x2, nf_ref[...])
    o_ref[...] = x2


def _moe_combine(yg, row0, ew, x1, mods, layer, norm_f, tm, seq_len, mod_base, final):
    n = x1.shape[0]
    half = D_MODEL // 2
    tiles_per_seq = seq_len // tm if mod_base else 1
    tile0 = row0 // tm

    def mod_idx(i):
        return (layer, mod_base + i // tiles_per_seq if mod_base else 0, 0, 0)

    row = lambda w: pl.BlockSpec((tm, w), lambda i: (i, 0))
    return pl.pallas_call(
        functools.partial(_moe_combine_kernel, final=final),
        grid=(n // tm,),
        in_specs=[pl.BlockSpec((TOP_K, tm, half), lambda i: (0, tile0 + i, 0)), row(LANES), row(D_MODEL),
                  pl.BlockSpec((None, 1, 1, 6 * D_MODEL), mod_idx), _const_spec(norm_f)],
        out_specs=row(D_MODEL),
        out_shape=jax.ShapeDtypeStruct((n, D_MODEL), F32),
        compiler_params=pltpu.CompilerParams(dimension_semantics=("arbitrary",),
                                             vmem_limit_bytes=VMEM_LIMIT),
        name="moe_combine_final" if final else "moe_combine",
    )(yg, ew, x1, mods, norm_f)


def _moe_experts(groups, eid, rank, counts, lw, layer):
    group_sizes = [g.shape[0] for g in groups]
    n = sum(group_sizes)
    dest, schedule, n_blocks = _dispatch_plan(eid, rank, counts, n, MOE_ROWS)
    xs = _sc_dispatch(groups, dest, n_blocks * MOE_ROWS)
    y = _moe_rows(xs, schedule, lw, layer, n_blocks, MOE_ROWS)
    outs, start = [], 0
    for size in group_sizes:
        idx = dest[:, start:start + size].reshape(size * TOP_K // SC_ROWS, SC_ROWS)
        outs.append(_sc_gather(y, idx).reshape(TOP_K, size, D_MODEL // 2))
        start += size
    return outs


def _constants():
    idx = np.arange(MIX_W)
    same_head = (idx[:, None] // HEAD_DIM) == (idx[None, :] // HEAD_DIM)
    maskbd = jnp.asarray(same_head, BF16)
    eexp = np.zeros((LANES, 4 * MIX_W), np.float32)
    for blk in range(4):
        kind, d = divmod(blk, 2)
        for h in range(HEADS):
            src = kind * 2 * HEADS + d * HEADS + h
            eexp[src, blk * MIX_W + h * HEAD_DIM: blk * MIX_W + (h + 1) * HEAD_DIM] = 1.0
    tri = np.triu(np.ones((POST_TM, POST_TM), np.float32), 1)
    return {"maskbd": maskbd, "ones_bd": maskbd, "eexp": jnp.asarray(eexp, BF16), "tri_tokens": jnp.asarray(tri, BF16)}


def _rope_tables(n):
    rows = n // GRID_W
    row = jnp.repeat(jnp.arange(rows, dtype=F32), GRID_W)
    col = jnp.tile(jnp.arange(GRID_W, dtype=F32), rows)
    axis_dim = MLA_ROPE // 2
    inv = jnp.power(ROPE_BASE, -jnp.arange(0, axis_dim, 2, dtype=F32) / axis_dim)
    ang_r = row[:, None] * inv
    ang_c = col[:, None] * inv
    cr, sr, cc, sc = jnp.cos(ang_r), jnp.sin(ang_r), jnp.cos(ang_c), jnp.sin(ang_c)
    zeros = jnp.zeros((n, LANES - MLA_ROPE), F32)
    cos_t = jnp.concatenate([cr, cr, cc, cc, zeros], axis=1)
    sin_t = jnp.concatenate([-sr, sr, -sc, sc, zeros], axis=1)
    return cos_t, sin_t


def _stacked_weights(p):
    w_uq = p["mla_w_uq"].reshape(DEPTH, Q_LORA, MLA_HEADS, MLA_NOPE + MLA_ROPE)
    zq = jnp.zeros((DEPTH, Q_LORA, MLA_HEADS, QH_W - MLA_NOPE - MLA_ROPE), F32)
    w_uq_a = jnp.concatenate([w_uq, zq], axis=-1).reshape(DEPTH, Q_LORA, MLA_HEADS * QH_W).astype(BF16)
    w_uq_sw = jnp.concatenate([jnp.zeros((DEPTH, Q_LORA, MLA_HEADS, MLA_NOPE), F32),
                               w_uq[..., MLA_NOPE + _ROPE_SWAP], zq], axis=-1)
    w_uq_sw = w_uq_sw.reshape(DEPTH, Q_LORA, MLA_HEADS * QH_W).astype(BF16)

    def per_direction(w):
        half = jnp.zeros((DEPTH, 64, MIX_W), F32)
        return jnp.stack([jnp.concatenate([w[:, 0], half], axis=1),
                          jnp.concatenate([half, w[:, 1]], axis=1)], axis=1).astype(BF16)

    row = lambda v: v.reshape(DEPTH, 1, -1)
    pad_row = lambda v: jnp.pad(row(v), ((0, 0), (0, 0), (0, LANES - 2 * HEADS)))
    return {
        "norm1": row(p["norm1"]),
        "w_in": p["w_in"],
        "q_norm": row(p["mla_q_norm"]),
        "w_uq": w_uq_a, "w_uq_sw": w_uq_sw,
        "kv_norm": row(p["mla_kv_norm"]),
        "w_ukv": p["mla_w_ukv"].astype(BF16),
        "gdn_conv": p["gdn_conv"],
        "gdn_alog": pad_row(p["gdn_a_log"]),
        "gdn_dtb": pad_row(p["gdn_dt_bias"]),
        "gdn_norm": jnp.tile(row(p["gdn_norm"]), (1, 1, HEADS)),
        "rwkv_mu_prev": row(p["rwkv_mu_prev"]),
        "rwkv_mu_next": row(p["rwkv_mu_next"]),
        "rwkv_w0": p["rwkv_w0"],
        "rwkv_w2": per_direction(p["rwkv_w2"]),
        "rwkv_a0": p["rwkv_a0"],
        "rwkv_a2": per_direction(p["rwkv_a2"]),
        "rwkv_g2": p["rwkv_g2"].astype(BF16),
        "rwkv_k_k": row(p["rwkv_k_k"]),
        "rwkv_k_a": row(p["rwkv_k_a"]),
        "rwkv_r_k": row(p["rwkv_r_k"]),
        "rwkv_gn_w": row(p["rwkv_gn_w"]),
        "rwkv_gn_b": row(p["rwkv_gn_b"]),
        "w_out": p["w_out"].astype(BF16),
        "norm2": row(p["norm2"]),
        "router_t": jnp.swapaxes(p["moe_router"], 1, 2),
        "router_b": p["moe_bias"].reshape(DEPTH, N_EXPERTS, 1),
        "moe_w_gu": p["moe_w_gu"],
        "moe_w_down": p["moe_w_down"],
        "shared_w_gu": p["shared_w_gu"].astype(BF16),
        "shared_w_down": p["shared_w_down"].astype(BF16),
    }


def _embed_block_diag(s):
    b = s.shape[0]
    eye = jnp.eye(HEADS, dtype=s.dtype)
    out = jnp.einsum("bdhkv,hg->bdhkgv", s, eye)
    return out.reshape(b, 2, MIX_W, MIX_W)


def _layer_front(x2d, mods, lw, l, consts, seq_len, mod_base, cache, tm, tq, counts_in, prev=None):
    zm, zg, zab, zr = _inproj(x2d, mods, lw, l, tm, seq_len, mod_base)
    if cache is None:
        o_mla, ckv, kpe = _mla_ctx(zm, lw, l, seq_len, prev[:2])
        o_gdn, s_gdn = _gdn(zg, zab, lw, l, consts, seq_len, prev=prev[2])
        o_rwkv, s_rwkv = _rwkv(zr, lw, l, consts, seq_len, prev=prev[3])
        new = (ckv, kpe, s_gdn, s_rwkv)
    else:
        cckv, ckpe, rc, rs, sg, sr = cache
        o_mla = _mla_lat(zm, lw, l, seq_len, tq, (cckv, ckpe, rc, rs))
        o_gdn, _ = _gdn(zg, zab, lw, l, consts, seq_len, sg)
        o_rwkv, _ = _rwkv(zr, lw, l, consts, seq_len, sr)
        new = None
    routed = _post(x2d, o_mla, o_gdn, o_rwkv, mods, lw, l, consts, tm, seq_len, mod_base, counts_in)
    return routed, new


def kernel(x_prompt, x_sample, cache_mla_ckv, cache_mla_kpe, state_gdn, state_rwkv, c, c_ctx, ada_w, ada_b, norm1, w_in, mla_q_norm, mla_w_uq, mla_kv_norm, mla_w_ukv, gdn_conv, gdn_a_log, gdn_dt_bias, gdn_norm, rwkv_mu_prev, rwkv_mu_next, rwkv_w0, rwkv_w2, rwkv_a0, rwkv_a2, rwkv_g2, rwkv_k_k, rwkv_k_a, rwkv_r_k, rwkv_gn_w, rwkv_gn_b, w_out, norm2, moe_router, moe_bias, moe_w_gu, moe_w_down, shared_w_gu, shared_w_down, norm_f):
    p = dict(norm1=norm1, w_in=w_in, mla_q_norm=mla_q_norm, mla_w_uq=mla_w_uq, mla_kv_norm=mla_kv_norm,
             mla_w_ukv=mla_w_ukv, gdn_conv=gdn_conv, gdn_a_log=gdn_a_log, gdn_dt_bias=gdn_dt_bias,
             gdn_norm=gdn_norm, rwkv_mu_prev=rwkv_mu_prev, rwkv_mu_next=rwkv_mu_next, rwkv_w0=rwkv_w0,
             rwkv_w2=rwkv_w2, rwkv_a0=rwkv_a0, rwkv_a2=rwkv_a2, rwkv_g2=rwkv_g2, rwkv_k_k=rwkv_k_k,
             rwkv_k_a=rwkv_k_a, rwkv_r_k=rwkv_r_k, rwkv_gn_w=rwkv_gn_w, rwkv_gn_b=rwkv_gn_b, w_out=w_out,
             norm2=norm2, moe_router=moe_router, moe_bias=moe_bias, moe_w_gu=moe_w_gu, moe_w_down=moe_w_down,
             shared_w_gu=shared_w_gu, shared_w_down=shared_w_down)
    weights = _stacked_weights(p)
    consts = _constants()
    nf = norm_f.reshape(1, D_MODEL)
    b_ctx, t_ctx, _ = x_prompt.shape
    b_lat, t_lat, _ = x_sample.shape

    cvec8 = jnp.concatenate([c_ctx[None, :], c, jnp.zeros((8 - 1 - b_lat, D_MODEL), F32)], axis=0)
    mods = _adaln(cvec8, ada_w, ada_b)
    mods = mods.reshape(DEPTH, 8, 1, 6 * D_MODEL)

    rc, rs = _rope_tables(t_lat)
    ckpe = jnp.pad(cache_mla_kpe, ((0, 0), (0, 0), (0, 0), (0, LANES - MLA_ROPE)))
    cache = (cache_mla_ckv, ckpe, rc, rs, _embed_block_diag_layers(state_gdn),
             _embed_block_diag_layers(jnp.swapaxes(state_rwkv, -1, -2)))
    xp = x_prompt.reshape(b_ctx * t_ctx, D_MODEL)
    xs = x_sample.reshape(b_lat * t_lat, D_MODEL)
    n_ctx = xp.shape[0]
    tm = POST_TM
    state_shape = (b_ctx, DEPTH, 2, HEADS, HEAD_DIM, HEAD_DIM)
    ctx_outs = (jnp.zeros((b_ctx, DEPTH, t_ctx, KV_LORA), F32), jnp.zeros((b_ctx, DEPTH, t_ctx, MLA_ROPE), F32),
                jnp.zeros(state_shape, F32), jnp.zeros(state_shape, F32))
    for l in range(DEPTH):
        final = l == DEPTH - 1
        no_pairs = jnp.zeros((N_EXPERTS, LANES), F32)
        (x1c, hc, eidc, rankc, ewc, cnt_c), ctx_outs = _layer_front(xp, mods, weights, l, consts, t_ctx, 0, None,
                                                                    tm, t_ctx, no_pairs, ctx_outs)
        (x1s, hs, eids, ranks, ews, cnt), _ = _layer_front(xs, mods, weights, l, consts, t_lat, 1, cache,
                                                           tm, 256, cnt_c)
        yg_c, yg_s = _moe_experts([hc, hs], jnp.concatenate([eidc, eids], axis=1),
                                  jnp.concatenate([rankc, ranks], axis=1), cnt, weights, l)
        xp = _moe_combine(yg_c, 0, ewc, x1c, mods, l, nf, tm, t_ctx, 0, final)
        xs = _moe_combine(yg_s, 0, ews, x1s, mods, l, nf, tm, t_lat, 1, final)

    y_prompt = xp.reshape(b_ctx, t_ctx, D_MODEL)
    y_sample = xs.reshape(b_lat, t_lat, D_MODEL)
    new_ckv, new_kpe, new_gdn, new_rwkv = ctx_outs
    return (y_prompt, y_sample, new_ckv, new_kpe, new_gdn, new_rwkv)


def _embed_block_diag_layers(s):
    b = s.shape[0]
    return _embed_block_diag(s.reshape(b * DEPTH, 2, HEADS, HEAD_DIM, HEAD_DIM)).reshape(
        b, DEPTH, 2, MIX_W, MIX_W)
```

```python
import functools

import numpy as np
import jax
import jax.numpy as jnp
from jax import lax
from jax.experimental import pallas as pl
from jax.experimental.pallas import tpu as pltpu
from jax.experimental.pallas import tpu_sc as plsc

F32 = jnp.float32
BF16 = jnp.bfloat16

D_MODEL = 1024
DEPTH = 2
PAST_LEN = 512
GRID_W = 64
NORM_EPS = 1e-6

MLA_HEADS = 4
MLA_NOPE = 128
MLA_ROPE = 64
MLA_V = 128
Q_LORA = 384
KV_LORA = 256
ROPE_BASE = 10000.0
MLA_SCALE = (MLA_NOPE + MLA_ROPE) ** -0.5

HEADS = 4
HEAD_DIM = 64
MIX_W = HEADS * HEAD_DIM
GDN_CONV_CH = 3 * MIX_W
CHUNK = 64
RWKV_GN_EPS = 64e-5

N_EXPERTS = 64
TOP_K = 8
N_GROUPS = 8
GROUP_SIZE = N_EXPERTS // N_GROUPS
TOPK_GROUPS = 4
D_EXPERT = 256
ROUTE_SCALE = 2.5

P_MLA = Q_LORA + KV_LORA + MLA_ROPE
P_GDN = GDN_CONV_CH + MIX_W + 4 * HEADS
P_RWKV = 3 * MIX_W + 128 + 128 + 128

LANES = 128
ZM_W = Q_LORA + KV_LORA + 2 * LANES
ZG_W = GDN_CONV_CH + MIX_W
ZR_W = P_RWKV
QH_W = 2 * LANES
VMEM_LIMIT = 56 * 1024 * 1024
POST_TM = 512

_ROPE_SWAP = np.concatenate([np.arange(16, 32), np.arange(0, 16), np.arange(48, 64), np.arange(32, 48)])


def _sigmoid(x):
    return 1.0 / (1.0 + jnp.exp(-x))


def _silu(x):
    return x * _sigmoid(x)


def _softplus(x):
    return jnp.maximum(x, 0.0) + jnp.log(1.0 + jnp.exp(-jnp.abs(x)))


def _rms(x, g, eps=NORM_EPS):
    return x * lax.rsqrt(jnp.mean(x * x, axis=-1, keepdims=True) + eps) * g


def _mm(a, b):
    return jnp.dot(a.astype(BF16), b.astype(BF16), preferred_element_type=F32)


def _mm_nt(a, b):
    return lax.dot_general(a.astype(BF16), b.astype(BF16), (((1,), (1,)), ((), ())),
                           preferred_element_type=F32)


def _mm_tn(a, b):
    return lax.dot_general(a.astype(BF16), b.astype(BF16), (((0,), (0,)), ((), ())),
                           preferred_element_type=F32)


def _split3(x):
    p1 = x.astype(BF16)
    r1 = x - p1.astype(F32)
    p2 = r1.astype(BF16)
    r2 = r1 - p2.astype(F32)
    return p1, p2, r2.astype(BF16)


def _mm_sel_l(sel, x):
    p1, p2, p3 = _split3(x)
    return _mm(sel, p1) + _mm(sel, p2) + _mm(sel, p3)


def _mm_sel_r(x, sel):
    p1, p2, p3 = _split3(x)
    return _mm(p1, sel) + _mm(p2, sel) + _mm(p3, sel)


def _iota(shape, dim):
    return lax.broadcasted_iota(jnp.int32, shape, dim)


def _layer_spec(a, layer, **kw):
    nd = a.ndim - 1
    return pl.BlockSpec((None,) + a.shape[1:], lambda *_: (layer,) + (0,) * nd, **kw)


def _const_spec(a, **kw):
    return pl.BlockSpec(a.shape, lambda *_: (0,) * a.ndim, **kw)


def _bd(x, maskbd):
    xb = x.astype(BF16)
    return jnp.concatenate([xb] * HEADS, axis=0) * maskbd


def _chunk_masks(rev):
    row = _iota((CHUNK, MIX_W), 0)
    col = jnp.bitwise_and(_iota((CHUNK, MIX_W), 1), HEAD_DIM - 1)
    r2 = _iota((CHUNK, CHUNK), 0)
    c2 = _iota((CHUNK, CHUNK), 1)
    if rev:
        inc, strict, tri = row <= col, row < col, r2 <= c2
    else:
        inc, strict, tri = row >= col, row > col, r2 >= c2
    eye = jnp.where(row == col, 1.0, 0.0).astype(F32)
    return inc, strict, jnp.where(tri, 1.0, 0.0).astype(BF16), eye


def _split2(x):
    hi = x.astype(BF16)
    return hi, (x - hi.astype(F32)).astype(BF16)


def _mm_bd3(x, p, maskbd):
    n = x.shape[0]
    xh, xl = _split2(x)
    ph, pl_ = _split2(p)
    r = jnp.dot(jnp.concatenate([xh, xl], axis=0), _bd(ph, maskbd), preferred_element_type=F32)
    return r[:n] + r[n:] + jnp.dot(xh, _bd(pl_, maskbd), preferred_element_type=F32)


def _neumann_inverse(a_list, eye_list, maskbd):
    bs = [-a for a in a_list]
    ms = [eye + b for eye, b in zip(eye_list, bs)]
    ps = [_mm_bd3(b, b, maskbd) for b in bs]
    for _ in range(4):
        boths = [_mm_bd3(jnp.concatenate([m, p], axis=0), p, maskbd) for m, p in zip(ms, ps)]
        ms = [m + both[:CHUNK] for m, both in zip(ms, boths)]
        ps = [both[CHUNK:] for both in boths]
    return [m + _mm_bd3(m, p, maskbd) for m, p in zip(ms, ps)]


def _adaln_kernel(c_ref, w_ref, b_ref, o_ref):
    cv = c_ref[...]
    o_ref[0] = _mm(_silu(cv), w_ref[0]) + b_ref[0]


def _adaln(cvec8, ada_w, ada_b):
    tn = 768
    n_out = 6 * D_MODEL
    return pl.pallas_call(
        _adaln_kernel,
        grid=(DEPTH, n_out // tn),
        in_specs=[
            pl.BlockSpec((8, D_MODEL), lambda l, j: (0, 0)),
            pl.BlockSpec((1, D_MODEL, tn), lambda l, j: (l, 0, j)),
            pl.BlockSpec((1, 1, tn), lambda l, j: (l, 0, j)),
        ],
        out_specs=pl.BlockSpec((1, 8, tn), lambda l, j: (l, 0, j)),
        out_shape=jax.ShapeDtypeStruct((DEPTH, 8, n_out), F32),
        compiler_params=pltpu.CompilerParams(dimension_semantics=("arbitrary", "arbitrary"),
                                             vmem_limit_bytes=VMEM_LIMIT),
        name="adaln",
    )(cvec8, ada_w, ada_b.reshape(DEPTH, 1, n_out))


_KPE0 = Q_LORA + KV_LORA
_W_IN_MOVES = (
    [(0, 0, P_MLA)]
    + [(P_MLA + LANES - MLA_ROPE + 16 * j, _KPE0 + 16 * int(_ROPE_SWAP[16 * j] // 16), 16) for j in range(4)]
    + [(ZM_W, P_MLA, ZG_W), (ZM_W + ZG_W, P_MLA + ZG_W, 4 * HEADS), (ZM_W + ZG_W + LANES, P_MLA + P_GDN, P_RWKV)]
)
W_IN_PAD = ZM_W + ZG_W + LANES + ZR_W


def _inproj_kernel(x_ref, mod_ref, n1_ref, w_ref, zm_ref, zg_ref, zab_ref, zr_ref, w_s):
    @pl.when(pl.program_id(0) == 0)
    def _():
        w_s[...] = jnp.zeros(w_s.shape, BF16)
        for dst, src, width in _W_IN_MOVES:
            w_s[:, dst:dst + width] = w_ref[:, src:src + width].astype(BF16)

    m = mod_ref[0]
    sh = m[:, 0:D_MODEL]
    sc = m[:, D_MODEL:2 * D_MODEL]
    h = _rms(x_ref[...], n1_ref[...]) * (1.0 + sc) + sh
    z = _mm(h, w_s[...])
    o1 = ZM_W
    o2 = o1 + ZG_W
    o3 = o2 + LANES
    zm_ref[...] = z[:, :o1]
    zg_ref[...] = z[:, o1:o2]
    zab_ref[...] = z[:, o2:o3]
    zr_ref[...] = z[:, o3:]


def _inproj(x2d, mods, lw, layer, tm, seq_len, mod_base):
    n = x2d.shape[0]
    tiles_per_seq = seq_len // tm if mod_base else 1

    def mod_idx(i):
        return (layer, mod_base + i // tiles_per_seq if mod_base else 0, 0, 0)

    return pl.pallas_call(
        _inproj_kernel,
        grid=(n // tm,),
        in_specs=[
            pl.BlockSpec((tm, D_MODEL), lambda i: (i, 0)),
            pl.BlockSpec((None, 1, 1, 6 * D_MODEL), mod_idx),
            _layer_spec(lw["norm1"], layer),
            _layer_spec(lw["w_in"], layer, pipeline_mode=pl.Buffered(1)),
        ],
        out_specs=[
            pl.BlockSpec((tm, ZM_W), lambda i: (i, 0)),
            pl.BlockSpec((tm, ZG_W), lambda i: (i, 0)),
            pl.BlockSpec((tm, LANES), lambda i: (i, 0)),
            pl.BlockSpec((tm, ZR_W), lambda i: (i, 0)),
        ],
        out_shape=[
            jax.ShapeDtypeStruct((n, ZM_W), F32),
            jax.ShapeDtypeStruct((n, ZG_W), F32),
            jax.ShapeDtypeStruct((n, LANES), F32),
            jax.ShapeDtypeStruct((n, ZR_W), F32),
        ],
        scratch_shapes=[pltpu.VMEM((D_MODEL, W_IN_PAD), BF16)],
        compiler_params=pltpu.CompilerParams(dimension_semantics=("arbitrary",),
                                             vmem_limit_bytes=VMEM_LIMIT),
        name="inproj",
    )(x2d, mods, lw["norm1"], lw["w_in"])


MLA_SEQ_PER_STEP = 4


def _mla_ctx_kernel(zm_ref, qn_ref, wuq_ref, kvn_ref, wukv_ref, prev_ckv, prev_kpe, o_ref, ckv_ref, kpe_ref,
                    *, seq_len, n_seq):
    del prev_ckv, prev_kpe
    o_kpe = Q_LORA + KV_LORA
    zm = zm_ref[...]
    ckv = _rms(zm[:, Q_LORA:o_kpe], kvn_ref[...])
    kpe = zm[:, o_kpe:o_kpe + LANES]
    kv = _mm(ckv, wukv_ref[...])
    q = _mm(_rms(zm[:, :Q_LORA], qn_ref[...]), wuq_ref[...])
    kpe_b = kpe.astype(BF16)
    seqs = [slice(i * seq_len, (i + 1) * seq_len) for i in range(n_seq)]
    for i, r in enumerate(seqs):
        ckv_ref[i] = ckv[r]
        kpe_ref[i] = kpe[r, :MLA_ROPE]
    pairs = [(r, h * QH_W) for r in seqs for h in range(MLA_HEADS)]
    scores = [_mm_nt(q[r, c0:c0 + QH_W], jnp.concatenate([kv[r, c0:c0 + LANES].astype(BF16), kpe_b[r]], axis=1))
              * MLA_SCALE for r, c0 in pairs]
    exps = [jnp.exp(s - jnp.max(s, axis=-1, keepdims=True)) for s in scores]
    outs = [_mm(e, kv[r, c0 + LANES:c0 + QH_W]) / jnp.sum(e, axis=-1, keepdims=True)
            for e, (r, c0) in zip(exps, pairs)]
    for o, (r, c0) in zip(outs, pairs):
        h = c0 // QH_W
        o_ref[r, h * MLA_V:(h + 1) * MLA_V] = o


def _mla_ctx(zm, lw, layer, seq_len, prev):
    n = zm.shape[0]
    nb = n // seq_len
    n_seq = MLA_SEQ_PER_STEP
    rows = n_seq * seq_len
    weights = [lw["q_norm"], lw["w_uq"], lw["kv_norm"], lw["w_ukv"]]
    return pl.pallas_call(
        functools.partial(_mla_ctx_kernel, seq_len=seq_len, n_seq=n_seq),
        grid=(nb // n_seq,),
        in_specs=[pl.BlockSpec((rows, ZM_W), lambda b: (b, 0))] + [_layer_spec(a, layer) for a in weights]
        + [pl.BlockSpec(memory_space=pl.ANY)] * 2,
        out_specs=[
            pl.BlockSpec((rows, MLA_HEADS * MLA_V), lambda b: (b, 0)),
            pl.BlockSpec((n_seq, None, seq_len, KV_LORA), lambda b: (b, layer, 0, 0)),
            pl.BlockSpec((n_seq, None, seq_len, MLA_ROPE), lambda b: (b, layer, 0, 0)),
        ],
        out_shape=[
            jax.ShapeDtypeStruct((n, MLA_HEADS * MLA_V), F32),
            jax.ShapeDtypeStruct((nb, DEPTH, seq_len, KV_LORA), F32),
            jax.ShapeDtypeStruct((nb, DEPTH, seq_len, MLA_ROPE), F32),
        ],
        input_output_aliases={5: 1, 6: 2},
        compiler_params=pltpu.CompilerParams(dimension_semantics=("arbitrary",),
                                             vmem_limit_bytes=VMEM_LIMIT),
        name="mla_ctx",
    )(zm, *weights, *prev)


def _mla_lat_kernel(zm_ref, cckv_ref, ckpe_ref, rc_ref, rs_ref, qn_ref, wuq_ref, wuqs_ref, kvn_ref, wukv_ref,
                    o_ref, k_s, v_s, *, seq_len, tq, past):
    qi = pl.program_id(1)
    o_kpe = Q_LORA + KV_LORA

    @pl.when(qi == 0)
    def _():
        zm = zm_ref[...]
        ckv = _rms(zm[:, Q_LORA:o_kpe], kvn_ref[...])
        kpe = zm[:, o_kpe:o_kpe + LANES] * rc_ref[...] + zm[:, o_kpe + LANES:o_kpe + 2 * LANES] * rs_ref[...]
        kvc = _mm(cckv_ref[0], wukv_ref[...])
        kpc = ckpe_ref[0].astype(BF16)
        kv = _mm(ckv, wukv_ref[...])
        kpe = kpe.astype(BF16)
        for h in range(MLA_HEADS):
            c0 = h * QH_W
            k_s[0:past, c0:c0 + LANES] = kvc[:, c0:c0 + LANES].astype(BF16)
            k_s[0:past, c0 + LANES:c0 + QH_W] = kpc
            v_s[0:past, h * MLA_V:(h + 1) * MLA_V] = kvc[:, c0 + LANES:c0 + QH_W].astype(BF16)
            k_s[past:past + seq_len, c0:c0 + LANES] = kv[:, c0:c0 + LANES].astype(BF16)
            k_s[past:past + seq_len, c0 + LANES:c0 + QH_W] = kpe
            v_s[past:past + seq_len, h * MLA_V:(h + 1) * MLA_V] = kv[:, c0 + LANES:c0 + QH_W].astype(BF16)

    r0 = pl.multiple_of(qi * tq, tq)
    zq = zm_ref[pl.ds(r0, tq), :]
    cq = _rms(zq[:, :Q_LORA], qn_ref[...])
    q = _mm(cq, wuq_ref[...])
    qs = _mm(cq, wuqs_ref[...])
    qc = jnp.concatenate([jnp.ones((tq, LANES), F32), rc_ref[pl.ds(r0, tq), :]], axis=1)
    qsn = jnp.concatenate([jnp.zeros((tq, LANES), F32), rs_ref[pl.ds(r0, tq), :]], axis=1)
    for h in range(MLA_HEADS):
        c0 = h * QH_W
        qh = q[:, c0:c0 + QH_W] * qc + qs[:, c0:c0 + QH_W] * qsn
        s = _mm_nt(qh, k_s[:, c0:c0 + QH_W]) * MLA_SCALE
        e = jnp.exp(s - jnp.max(s, axis=-1, keepdims=True))
        den = jnp.sum(e, axis=-1, keepdims=True)
        o_ref[:, h * MLA_V:(h + 1) * MLA_V] = _mm(e, v_s[:, h * MLA_V:(h + 1) * MLA_V]) / den


def _mla_lat(zm, lw, layer, seq_len, tq, cache):
    n = zm.shape[0]
    nb = n // seq_len
    past = PAST_LEN
    tk = past + seq_len
    cckv, ckpe, rc, rs = cache
    weights = [lw["q_norm"], lw["w_uq"], lw["w_uq_sw"], lw["kv_norm"], lw["w_ukv"]]
    return pl.pallas_call(
        functools.partial(_mla_lat_kernel, seq_len=seq_len, tq=tq, past=past),
        grid=(nb, seq_len // tq),
        in_specs=[pl.BlockSpec((seq_len, ZM_W), lambda b, q: (b, 0)),
                  pl.BlockSpec((1, None, past, KV_LORA), lambda b, q: (b, layer, 0, 0)),
                  pl.BlockSpec((1, None, past, LANES), lambda b, q: (b, layer, 0, 0)),
                  _const_spec(rc), _const_spec(rs)] + [_layer_spec(a, layer) for a in weights],
        out_specs=pl.BlockSpec((tq, MLA_HEADS * MLA_V), lambda b, q: (b * (seq_len // tq) + q, 0)),
        out_shape=jax.ShapeDtypeStruct((n, MLA_HEADS * MLA_V), F32),
        scratch_shapes=[
            pltpu.VMEM((tk, MLA_HEADS * QH_W), BF16),
            pltpu.VMEM((tk, MLA_HEADS * MLA_V), BF16),
        ],
        compiler_params=pltpu.CompilerParams(dimension_semantics=("arbitrary", "arbitrary"),
                                             vmem_limit_bytes=VMEM_LIMIT),
        name="mla_lat",
    )(zm, cckv, ckpe, rc, rs, *weights)


SEQ_PER_STEP = 4
CHUNK_GROUP = 4
GDN_PRE_W = 5 * MIX_W


def _for_chunk_groups(n_chunks, fn):
    if n_chunks == CHUNK_GROUP:
        fn(0)
    else:
        def body(gi, carry):
            fn(gi * CHUNK_GROUP)
            return carry
        lax.fori_loop(0, n_chunks // CHUNK_GROUP, body, 0)


def _gdn_prepare(items, maskbd):
    n = range(len(items))
    qs, ks, vs, gs, betas, masks, revs = zip(*items)
    gcs = [_mm_sel_l(masks[i][2], gs[i]) for i in n]
    decays = []
    for i in n:
        inc, eye = masks[i][0], masks[i][3]
        gc_row = jnp.sum(eye * gcs[i], axis=0, keepdims=True)
        decays.append(jnp.where(inc, jnp.exp(jnp.where(inc, gcs[i] - gc_row, 0.0)), 0.0))
    kbs = [ks[i] * betas[i] for i in n]
    aqs = [_mm_nt(jnp.concatenate([kbs[i], qs[i]], axis=0), _bd(ks[i], maskbd)) for i in n]
    a_mats = [jnp.where(masks[i][1], aqs[i][:CHUNK] * decays[i], 0.0) for i in n]
    t_invs = _neumann_inverse(a_mats, [m[3] for m in masks], maskbd)
    egcs = [jnp.exp(gc) for gc in gcs]
    uws = [_mm(t_invs[i], jnp.concatenate([_bd(vs[i] * betas[i], maskbd), _bd(kbs[i] * egcs[i], maskbd)], axis=1))
           for i in n]
    out = []
    for i in n:
        g_last = gcs[i][0:1] if revs[i] else gcs[i][CHUNK - 1:CHUNK]
        pre = jnp.concatenate([uws[i], qs[i] * egcs[i], aqs[i][CHUNK:] * decays[i],
                               ks[i] * jnp.exp(g_last - gcs[i])], axis=1)
        out.append((pre, jnp.broadcast_to(jnp.exp(g_last), (8, MIX_W))))
    return out


def _gdn_step(items, maskbd):
    w = MIX_W
    n = range(len(items))
    pres, egls, states = zip(*items)
    wqs = [_mm(jnp.concatenate([pres[i][:, w:2 * w], pres[i][:, 2 * w:3 * w]], axis=0), states[i]) for i in n]
    v_news = [pres[i][:, :w] - wqs[i][:CHUNK] for i in n]
    outs = [wqs[i][CHUNK:] + _mm(pres[i][:, 3 * w:4 * w], _bd(v_news[i], maskbd)) for i in n]
    upds = [_mm_tn(pres[i][:, 4 * w:], v_news[i]) for i in n]
    mask_f = maskbd.astype(F32)
    return [(outs[i], states[i] * egls[i] + upds[i] * mask_f) for i in n]


def _gdn_kernel(*refs, seq_len, n_seq, cached):
    if cached:
        (zg_ref, zab_ref, s0_ref, conv_ref, alog_ref, dtb_ref, gn_ref, eexp_ref, ones_ref, maskbd_ref,
         o_ref, sout_ref, q_s, k_s, v_s, ge_s, pre_s, gl_s, oacc_s, st_s) = refs
    else:
        zg_ref, zab_ref, conv_ref, alog_ref, dtb_ref, gn_ref, eexp_ref, ones_ref, maskbd_ref = refs[:9]
        o_ref, sout_ref, q_s, k_s, v_s, ge_s, pre_s, gl_s, oacc_s, st_s = refs[-10:]
    t = seq_len * n_seq
    z = zg_ref[:, :GDN_CONV_CH]
    rowi = jnp.bitwise_and(_iota((t, 1), 0), seq_len - 1)
    zp = jnp.where(rowi == 0, 0.0, pltpu.roll(z, 1, 0))
    zn = jnp.where(rowi == seq_len - 1, 0.0, pltpu.roll(z, t - 1, 0))
    cw = conv_ref[...]
    qkv = _silu(zp * cw[0:1] + z * cw[1:2] + zn * cw[2:3])
    ones_bd = ones_ref[...]
    q = qkv[:, :MIX_W]
    k = qkv[:, MIX_W:2 * MIX_W]
    q_s[...] = q * lax.rsqrt(_mm_sel_r(q * q, ones_bd) + 1e-6) * (HEAD_DIM ** -0.5)
    k_s[...] = k * lax.rsqrt(_mm_sel_r(k * k, ones_bd) + 1e-6)
    v_s[...] = qkv[:, 2 * MIX_W:]
    ab = zab_ref[...]
    lane = _iota((t, LANES), 1)
    gb = jnp.where(lane < 2 * HEADS, -jnp.exp(alog_ref[...]) * _softplus(ab + dtb_ref[...]), _sigmoid(ab))
    ge_s[...] = _mm_sel_r(gb, eexp_ref[...])
    oacc_s[...] = jnp.zeros((t, MIX_W), F32)
    if cached:
        st_s[...] = s0_ref[...]
    else:
        st_s[...] = jnp.zeros((n_seq, 2, MIX_W, MIX_W), F32)
    maskbd = maskbd_ref[...]
    masks = (_chunk_masks(False), _chunk_masks(True))
    n_chunks = seq_len // CHUNK

    def prepare_group(c0):
        where, items = [], []
        for j in range(CHUNK_GROUP):
            c = c0 + j
            rows = pl.ds(pl.multiple_of(c * CHUNK, CHUNK), CHUNK)
            for d in range(2):
                where.append((d, c, rows))
                items.append((q_s[rows, :], k_s[rows, :], v_s[rows, :], ge_s[rows, d * MIX_W:(d + 1) * MIX_W],
                              ge_s[rows, (2 + d) * MIX_W:(3 + d) * MIX_W], masks[d], d == 1))
        for (d, c, rows), (pre, egl) in zip(where, _gdn_prepare(items, maskbd)):
            pre_s[d, rows, :] = pre
            gl_s[d, pl.ds(pl.multiple_of(c * 8, 8), 8), :] = egl

    _for_chunk_groups(n_chunks * n_seq, prepare_group)

    def body(i, carry):
        where, items = [], []
        for q in range(n_seq):
            for d in range(2):
                c = q * n_chunks + (i if d == 0 else n_chunks - 1 - i)
                rows = pl.ds(pl.multiple_of(c * CHUNK, CHUNK), CHUNK)
                where.append((q, d, rows))
                items.append((pre_s[d, rows, :], gl_s[d, pl.ds(pl.multiple_of(c * 8, 8), 8), :][0:1], st_s[q, d]))
        for (q, d, rows), (o, s_new) in zip(where, _gdn_step(items, maskbd)):
            oacc_s[rows, :] = oacc_s[rows, :] + o
            st_s[q, d] = s_new
        return carry

    lax.fori_loop(0, n_chunks, body, 0)
    o = oacc_s[...]
    ms = _mm_sel_r(o * o, ones_bd) * (1.0 / HEAD_DIM)
    gate = zg_ref[:, GDN_CONV_CH:]
    o_ref[...] = o * lax.rsqrt(ms + NORM_EPS) * gn_ref[...] * _silu(gate)
    for q in range(n_seq):
        for d in range(2):
            for h in range(HEADS):
                sl = slice(h * HEAD_DIM, (h + 1) * HEAD_DIM)
                sout_ref[q, d, h] = st_s[q, d, sl, sl]


def _state_out(nb, n_seq, layer, cached, prev, args, in_specs):
    if cached:
        return (pl.BlockSpec((n_seq, 2, HEADS, HEAD_DIM, HEAD_DIM), lambda b: (b, 0, 0, 0, 0)),
                jax.ShapeDtypeStruct((nb, 2, HEADS, HEAD_DIM, HEAD_DIM), F32), {})
    aliases = {len(args): 1}
    args.append(prev)
    in_specs.append(pl.BlockSpec(memory_space=pl.ANY))
    return (pl.BlockSpec((n_seq, None, 2, HEADS, HEAD_DIM, HEAD_DIM), lambda b: (b, layer, 0, 0, 0, 0)),
            jax.ShapeDtypeStruct((nb, DEPTH, 2, HEADS, HEAD_DIM, HEAD_DIM), F32), aliases)


def _gdn(zg, zab, lw, layer, consts, seq_len, s0_bd=None, prev=None):
    n = zg.shape[0]
    nb = n // seq_len
    cached = s0_bd is not None
    n_seq = 1 if cached else SEQ_PER_STEP
    rows = seq_len * n_seq
    args = [zg, zab]
    in_specs = [pl.BlockSpec((rows, ZG_W), lambda b: (b, 0)),
                pl.BlockSpec((rows, LANES), lambda b: (b, 0))]
    if cached:
        args.append(s0_bd)
        in_specs.append(pl.BlockSpec((1, None, 2, MIX_W, MIX_W), lambda b: (b, layer, 0, 0, 0)))
    layered = [lw["gdn_conv"], lw["gdn_alog"], lw["gdn_dtb"], lw["gdn_norm"]]
    const = [consts["eexp"], consts["ones_bd"], consts["maskbd"]]
    args += layered + const
    in_specs += [_layer_spec(a, layer) for a in layered] + [_const_spec(a) for a in const]
    s_spec, s_shape, aliases = _state_out(nb, n_seq, layer, cached, prev, args, in_specs)
    return pl.pallas_call(
        functools.partial(_gdn_kernel, seq_len=seq_len, n_seq=n_seq, cached=cached),
        grid=(nb // n_seq,),
        in_specs=in_specs,
        out_specs=[
            pl.BlockSpec((rows, MIX_W), lambda b: (b, 0)),
            s_spec,
        ],
        out_shape=[jax.ShapeDtypeStruct((n, MIX_W), F32), s_shape],
        input_output_aliases=aliases,
        scratch_shapes=[
            pltpu.VMEM((rows, MIX_W), F32),
            pltpu.VMEM((rows, MIX_W), F32),
            pltpu.VMEM((rows, MIX_W), F32),
            pltpu.VMEM((rows, 4 * MIX_W), F32),
            pltpu.VMEM((2, rows, GDN_PRE_W), F32),
            pltpu.VMEM((2, rows // CHUNK * 8, MIX_W), F32),
            pltpu.VMEM((rows, MIX_W), F32),
            pltpu.VMEM((n_seq, 2, MIX_W, MIX_W), F32),
        ],
        compiler_params=pltpu.CompilerParams(dimension_semantics=("arbitrary",),
                                             vmem_limit_bytes=VMEM_LIMIT),
        name="gdn_lat" if cached else "gdn_ctx",
    )(*args)


RWKV_PRE_W = 7 * MIX_W


def _rwkv_prepare(items, maskbd, eye_full):
    n = range(len(items))
    rs, kds, vs, kks, bs, lws, masks, revs = zip(*items)
    cums = [_mm_sel_l(masks[i][2], lws[i]) for i in n]
    einvs = [jnp.exp(-c) for c in cums]
    kts = [kks[i] * jnp.exp(cums[i] - lws[i]) for i in n]
    rts = [rs[i] * jnp.exp(cums[i]) for i in n]
    krs = [jnp.concatenate([kts[i], rts[i]], axis=0) for i in n]
    lb_alls = [_mm_nt(krs[i], _bd(bs[i] * einvs[i], maskbd)) for i in n]
    lk_alls = [_mm_nt(krs[i], _bd(kds[i] * einvs[i], maskbd)) for i in n]
    lbs = [jnp.where(masks[i][1], lb_alls[i][:CHUNK], 0.0) for i in n]
    t_invs = _neumann_inverse(lbs, [m[3] for m in masks], maskbd)
    lvs = [_mm(jnp.concatenate([jnp.where(masks[i][1], lk_alls[i][:CHUNK], 0.0),
                                jnp.where(masks[i][0], lk_alls[i][CHUNK:], 0.0)], axis=0), _bd(vs[i], maskbd))
           for i in n]
    tkps = [_mm(t_invs[i], jnp.concatenate([_bd(kts[i], maskbd), _bd(lvs[i][:CHUNK], maskbd)], axis=1)) for i in n]
    out = []
    for i in n:
        c_last = cums[i][0:1] if revs[i] else cums[i][CHUNK - 1:CHUNK]
        tail = jnp.exp(c_last - cums[i])
        rb = jnp.where(masks[i][0], lb_alls[i][CHUNK:], 0.0)
        pre = jnp.concatenate([tkps[i][:, :MIX_W], rts[i], tkps[i][:, MIX_W:], lvs[i][CHUNK:], rb,
                               kds[i] * tail, bs[i] * tail], axis=1)
        gcol = jnp.sum(eye_full * jnp.exp(c_last), axis=1, keepdims=True)
        out.append((pre, jnp.broadcast_to(gcol, (MIX_W, LANES))))
    return out


def _rwkv_step(items, maskbd):
    w = MIX_W
    n = range(len(items))
    pres, vs, gcols, states = zip(*items)
    prs = [_mm(jnp.concatenate([pres[i][:, :w], pres[i][:, w:2 * w]], axis=0), states[i]) for i in n]
    ps = [prs[i][:CHUNK] + pres[i][:, 2 * w:3 * w] for i in n]
    outs = [prs[i][CHUNK:] + pres[i][:, 3 * w:4 * w] - _mm(pres[i][:, 4 * w:5 * w], _bd(ps[i], maskbd)) for i in n]
    upds = [_mm_tn(jnp.concatenate([pres[i][:, 5 * w:6 * w], pres[i][:, 6 * w:]], axis=0),
                   jnp.concatenate([vs[i], -ps[i]], axis=0)) for i in n]
    mask_f = maskbd.astype(F32)
    return [(outs[i], states[i] * jnp.concatenate([gcols[i], gcols[i]], axis=1) + upds[i] * mask_f) for i in n]


def _rwkv_kernel(*refs, seq_len, n_seq, cached):
    if cached:
        (zr_ref, s0_ref, mup_ref, mun_ref, w0_ref, w2_ref, a0_ref, a2_ref, g2_ref, kk_ref, ka_ref, rk_ref,
         gnw_ref, gnb_ref, ones_ref, maskbd_ref, o_ref, sout_ref,
         r_s, v_s, kk_s, dir_s, bg_s, pre_s, gcol_s, oacc_s, st_s) = refs
    else:
        (zr_ref, mup_ref, mun_ref, w0_ref, w2_ref, a0_ref, a2_ref, g2_ref, kk_ref, ka_ref, rk_ref,
         gnw_ref, gnb_ref, ones_ref, maskbd_ref) = refs[:15]
        o_ref, sout_ref, r_s, v_s, kk_s, dir_s, bg_s, pre_s, gcol_s, oacc_s, st_s = refs[-11:]
    t = seq_len * n_seq
    z = zr_ref[...]
    rowi = jnp.bitwise_and(_iota((t, 1), 0), seq_len - 1)
    zp = jnp.where(rowi == 0, 0.0, pltpu.roll(z, 1, 0))
    zn = jnp.where(rowi == seq_len - 1, 0.0, pltpu.roll(z, t - 1, 0))
    z = z + mup_ref[...] * (zp - z) + mun_ref[...] * (zn - z)
    w = MIX_W
    r = z[:, :w]
    k = z[:, w:2 * w]
    v = z[:, 2 * w:3 * w]
    wd = jnp.tanh(z[:, 3 * w:3 * w + LANES])
    ad = z[:, 3 * w + LANES:3 * w + 2 * LANES]
    gd = _sigmoid(z[:, 3 * w + 2 * LANES:])
    ones_bd = ones_ref[...]
    kk = k * kk_ref[...]
    kk = kk * lax.rsqrt(_mm_sel_r(kk * kk, ones_bd) + 1e-6)
    r_s[...] = r
    v_s[...] = v
    kk_s[...] = kk
    bonus = jnp.zeros((t, w), F32)
    for d in range(2):
        w_log = -_softplus(-(w0_ref[d:d + 1] + _mm(wd, w2_ref[d]))) - 0.5
        a = _sigmoid(a0_ref[d:d + 1] + _mm(ad, a2_ref[d]))
        kd = k * (1.0 + (a - 1.0) * ka_ref[...])
        dir_s[:, (3 * d) * w:(3 * d + 1) * w] = -jnp.exp(w_log)
        dir_s[:, (3 * d + 1) * w:(3 * d + 2) * w] = kd
        dir_s[:, (3 * d + 2) * w:(3 * d + 3) * w] = kk * a
        bonus = bonus + _mm_sel_r(r * kd * rk_ref[...], ones_bd) * v
    bg_s[:, :w] = bonus
    bg_s[:, w:] = _mm(gd, g2_ref[...])
    oacc_s[...] = jnp.zeros((t, w), F32)
    if cached:
        st_s[...] = s0_ref[...]
    else:
        st_s[...] = jnp.zeros((n_seq, 2, w, w), F32)
    maskbd = maskbd_ref[...]
    masks = (_chunk_masks(False), _chunk_masks(True))
    eye_full = jnp.where(_iota((w, w), 0) == _iota((w, w), 1), 1.0, 0.0).astype(F32)
    n_chunks = seq_len // CHUNK

    def prepare_group(c0):
        where, items = [], []
        for j in range(CHUNK_GROUP):
            c = c0 + j
            rows = pl.ds(pl.multiple_of(c * CHUNK, CHUNK), CHUNK)
            for d in range(2):
                where.append((d, c, rows))
                items.append((r_s[rows, :], dir_s[rows, (3 * d + 1) * w:(3 * d + 2) * w], v_s[rows, :], kk_s[rows, :],
                              dir_s[rows, (3 * d + 2) * w:(3 * d + 3) * w], dir_s[rows, (3 * d) * w:(3 * d + 1) * w],
                              masks[d], d == 1))
        for (d, c, rows), (pre, gcol) in zip(where, _rwkv_prepare(items, maskbd, eye_full)):
            pre_s[d, rows, :] = pre
            gcol_s[d, pl.ds(pl.multiple_of(c * w, w), w), :] = gcol

    _for_chunk_groups(n_chunks * n_seq, prepare_group)

    def body(i, carry):
        where, items = [], []
        for q in range(n_seq):
            for d in range(2):
                c = q * n_chunks + (i if d == 0 else n_chunks - 1 - i)
                rows = pl.ds(pl.multiple_of(c * CHUNK, CHUNK), CHUNK)
                where.append((q, d, rows))
                items.append((pre_s[d, rows, :], v_s[rows, :], gcol_s[d, pl.ds(pl.multiple_of(c * w, w), w), :],
                              st_s[q, d]))
        for (q, d, rows), (o, z_new) in zip(where, _rwkv_step(items, maskbd)):
            oacc_s[rows, :] = oacc_s[rows, :] + o
            st_s[q, d] = z_new
        return carry

    lax.fori_loop(0, n_chunks, body, 0)
    o = oacc_s[...]
    inv_n = 1.0 / HEAD_DIM
    mu = _mm_sel_r(o, ones_bd) * inv_n
    oc = o - mu
    var = _mm_sel_r(oc * oc, ones_bd) * inv_n
    y = oc * lax.rsqrt(var + RWKV_GN_EPS) * gnw_ref[...] + gnb_ref[...]
    o_ref[...] = (y + bg_s[:, :w]) * bg_s[:, w:]
    for q in range(n_seq):
        for d in range(2):
            state = st_s[q, d] if cached else st_s[q, d].T
            for h in range(HEADS):
                sl = slice(h * HEAD_DIM, (h + 1) * HEAD_DIM)
                sout_ref[q, d, h] = state[sl, sl]


def _rwkv(zr, lw, layer, consts, seq_len, s0_bd=None, prev=None):
    n = zr.shape[0]
    nb = n // seq_len
    cached = s0_bd is not None
    n_seq = 1 if cached else SEQ_PER_STEP
    rows = seq_len * n_seq
    args = [zr]
    in_specs = [pl.BlockSpec((rows, ZR_W), lambda b: (b, 0))]
    if cached:
        args.append(s0_bd)
        in_specs.append(pl.BlockSpec((1, None, 2, MIX_W, MIX_W), lambda b: (b, layer, 0, 0, 0)))
    layered = [lw["rwkv_mu_prev"], lw["rwkv_mu_next"], lw["rwkv_w0"], lw["rwkv_w2"], lw["rwkv_a0"], lw["rwkv_a2"],
               lw["rwkv_g2"], lw["rwkv_k_k"], lw["rwkv_k_a"], lw["rwkv_r_k"], lw["rwkv_gn_w"], lw["rwkv_gn_b"]]
    const = [consts["ones_bd"], consts["maskbd"]]
    args += layered + const
    in_specs += [_layer_spec(a, layer) for a in layered] + [_const_spec(a) for a in const]
    s_spec, s_shape, aliases = _state_out(nb, n_seq, layer, cached, prev, args, in_specs)
    return pl.pallas_call(
        functools.partial(_rwkv_kernel, seq_len=seq_len, n_seq=n_seq, cached=cached),
        grid=(nb // n_seq,),
        in_specs=in_specs,
        out_specs=[
            pl.BlockSpec((rows, MIX_W), lambda b: (b, 0)),
            s_spec,
        ],
        out_shape=[jax.ShapeDtypeStruct((n, MIX_W), F32), s_shape],
        input_output_aliases=aliases,
        scratch_shapes=[
            pltpu.VMEM((rows, MIX_W), F32),
            pltpu.VMEM((rows, MIX_W), F32),
            pltpu.VMEM((rows, MIX_W), F32),
            pltpu.VMEM((rows, 6 * MIX_W), F32),
            pltpu.VMEM((rows, 2 * MIX_W), F32),
            pltpu.VMEM((2, rows, RWKV_PRE_W), F32),
            pltpu.VMEM((2, rows // CHUNK * MIX_W, LANES), F32),
            pltpu.VMEM((rows, MIX_W), F32),
            pltpu.VMEM((n_seq, 2, MIX_W, MIX_W), F32),
        ],
        compiler_params=pltpu.CompilerParams(dimension_semantics=("arbitrary",),
                                             vmem_limit_bytes=VMEM_LIMIT),
        name="rwkv_lat" if cached else "rwkv_ctx",
    )(*args)


def _route(logits_t, bias):
    tm = logits_t.shape[1]
    neg = -jnp.inf
    sc = _sigmoid(logits_t)
    sc3 = sc.reshape(N_GROUPS, GROUP_SIZE, tm)
    sel = (sc + bias).reshape(N_GROUPS, GROUP_SIZE, tm)
    si = _iota(sel.shape, 1).astype(F32)
    m1 = jnp.max(sel, axis=1, keepdims=True)
    f1 = jnp.min(jnp.where(sel == m1, si, float(GROUP_SIZE)), axis=1, keepdims=True)
    m2 = jnp.max(jnp.where(si == f1, neg, sel), axis=1, keepdims=True)
    grp = m1 + m2
    gi = _iota(grp.shape, 0).astype(F32)
    gsel = jnp.zeros(grp.shape, F32)
    for _ in range(TOPK_GROUPS):
        mx = jnp.max(grp, axis=0, keepdims=True)
        fi = jnp.min(jnp.where(grp == mx, gi, float(N_GROUPS)), axis=0, keepdims=True)
        hit = gi == fi
        gsel = jnp.where(hit, 1.0, gsel)
        grp = jnp.where(hit, neg, grp)
    cur = jnp.where(gsel > 0.0, sel, neg)
    ei = (_iota(cur.shape, 0) * GROUP_SIZE + _iota(cur.shape, 1)).astype(F32)
    chosen = jnp.zeros(cur.shape, F32)
    ids, wts = [], []
    for _ in range(TOP_K):
        mx = jnp.max(jnp.max(cur, axis=0, keepdims=True), axis=1, keepdims=True)
        fi = jnp.min(jnp.min(jnp.where(cur == mx, ei, float(N_EXPERTS)), axis=0, keepdims=True),
                     axis=1, keepdims=True)
        hit = ei == fi
        chosen = jnp.where(hit, 1.0, chosen)
        cur = jnp.where(hit, neg, cur)
        ids.append(fi.reshape(1, tm))
        wts.append(jnp.sum(jnp.sum(jnp.where(hit, sc3, 0.0), axis=0, keepdims=True), axis=1, keepdims=True)
                   .reshape(1, tm))
    w = jnp.concatenate(wts, axis=0)
    w = w / jnp.sum(w, axis=0, keepdims=True) * ROUTE_SCALE
    return chosen.reshape(N_EXPERTS, tm), jnp.concatenate(ids, axis=0), w


def _pack_halves(x):
    half = x.shape[1] // 2
    bits = lax.bitcast_convert_type(x.astype(BF16).astype(F32), jnp.int32)
    lo = lax.shift_right_logical(bits[:, :half], jnp.int32(16))
    return jnp.bitwise_or(lo, jnp.bitwise_and(bits[:, half:], jnp.int32(-65536)))


def _unpack_halves(word):
    lo = lax.bitcast_convert_type(lax.shift_left(word, jnp.int32(16)), F32)
    hi = lax.bitcast_convert_type(jnp.bitwise_and(word, jnp.int32(-65536)), F32)
    return lo, hi


def _post_kernel(x_ref, om_ref, og_ref, or_ref, mod_ref, wo_ref, n2_ref, rt_ref, rb_ref, sgu_ref, sdn_ref,
                 tri_ref, cin_ref, x1_ref, h2_ref, eid_ref, rank_ref, ew_ref, cnt_ref, carry_s):
    @pl.when(pl.program_id(0) == 0)
    def _():
        carry_s[...] = cin_ref[...]

    m = mod_ref[0]
    g1 = m[:, 2 * D_MODEL:3 * D_MODEL]
    sh2 = m[:, 3 * D_MODEL:4 * D_MODEL]
    sc2 = m[:, 4 * D_MODEL:5 * D_MODEL]
    w_mla = MLA_HEADS * MLA_V
    mix = (_mm(om_ref[...], wo_ref[0:w_mla, :]) + _mm(og_ref[...], wo_ref[w_mla:w_mla + MIX_W, :])
           + _mm(or_ref[...], wo_ref[w_mla + MIX_W:, :]))
    x1 = x_ref[...] + g1 * mix
    h2 = _rms(x1, n2_ref[...]) * (1.0 + sc2) + sh2
    h2_ref[...] = _pack_halves(h2)
    g2 = m[:, 5 * D_MODEL:]
    x1_ref[...] = x1 + g2 * _mm(_swiglu_act(_mm(h2, sgu_ref[...])), sdn_ref[...])
    r_hi, r_lo = _split2(rt_ref[...])
    h_hi, h_lo = _split2(h2)
    logits_t = _mm_nt(r_hi, h_hi) + _mm_nt(r_hi, h_lo) + _mm_nt(r_lo, h_hi)
    chosen, ids, w = _route(logits_t, rb_ref[...])
    tm = chosen.shape[1]
    rank_et = (carry_s[:, 0:1] + _mm(chosen, tri_ref[...])).reshape(N_GROUPS, GROUP_SIZE, tm)
    ei = (_iota(rank_et.shape, 0) * GROUP_SIZE + _iota(rank_et.shape, 1)).astype(F32)
    ranks = []
    for k in range(TOP_K):
        pick = jnp.where(ei == ids[k:k + 1].reshape(1, 1, tm), rank_et, 0.0)
        ranks.append(jnp.sum(jnp.sum(pick, axis=0, keepdims=True), axis=1, keepdims=True).reshape(1, tm))
    eid_ref[...] = ids.astype(jnp.int32)
    rank_ref[...] = jnp.concatenate(ranks, axis=0).astype(jnp.int32)
    ew_ref[...] = jnp.concatenate([w, jnp.zeros((LANES - TOP_K, tm), F32)], axis=0).T
    total = carry_s[...] + jnp.sum(chosen, axis=1, keepdims=True)
    carry_s[...] = total
    cnt_ref[...] = total


def _post(x2d, om, og, orw, mods, lw, layer, consts, tm, seq_len, mod_base, counts_in):
    n = x2d.shape[0]
    tiles_per_seq = seq_len // tm if mod_base else 1

    def mod_idx(i):
        return (layer, mod_base + i // tiles_per_seq if mod_base else 0, 0, 0)

    row = lambda w: pl.BlockSpec((tm, w), lambda i: (i, 0))
    col = lambda h: pl.BlockSpec((h, tm), lambda i: (0, i))
    full = lambda a: _layer_spec(a, layer)
    tail = [lw["w_out"], lw["norm2"], lw["router_t"], lw["router_b"], lw["shared_w_gu"], lw["shared_w_down"]]
    tri = consts["tri_tokens"]
    return pl.pallas_call(
        _post_kernel,
        grid=(n // tm,),
        in_specs=[row(D_MODEL), row(MLA_HEADS * MLA_V), row(MIX_W), row(MIX_W),
                  pl.BlockSpec((None, 1, 1, 6 * D_MODEL), mod_idx)] + [full(a) for a in tail]
        + [_const_spec(tri), _const_spec(counts_in)],
        out_specs=[row(D_MODEL), row(D_MODEL // 2), col(TOP_K), col(TOP_K), row(LANES),
                   pl.BlockSpec((N_EXPERTS, LANES), lambda i: (0, 0))],
        out_shape=[
            jax.ShapeDtypeStruct((n, D_MODEL), F32),
            jax.ShapeDtypeStruct((n, D_MODEL // 2), jnp.int32),
            jax.ShapeDtypeStruct((TOP_K, n), jnp.int32),
            jax.ShapeDtypeStruct((TOP_K, n), jnp.int32),
            jax.ShapeDtypeStruct((n, LANES), F32),
            jax.ShapeDtypeStruct((N_EXPERTS, LANES), F32),
        ],
        scratch_shapes=[pltpu.VMEM((N_EXPERTS, LANES), F32)],
        compiler_params=pltpu.CompilerParams(dimension_semantics=("arbitrary",),
                                             vmem_limit_bytes=VMEM_LIMIT),
        name="post",
    )(x2d, om, og, orw, mods, *tail, tri, counts_in)


MOE_ROWS = 768
WEIGHT_DMA_THREAD = 1
SC_ROWS = 128
SC_SUBCORES = 32


def _swiglu_act(gu):
    return _silu(gu[:, :D_EXPERT]) * gu[:, D_EXPERT:]


def _dispatch_plan(eid, rank, counts, n, rows):
    n_blocks = n * TOP_K // rows + N_EXPERTS
    cnt = counts[:, 0].astype(jnp.int32)
    blocks = (cnt + rows - 1) // rows
    block_end = jnp.cumsum(blocks)
    offset = (block_end - blocks) * rows
    experts = jnp.arange(N_EXPERTS, dtype=jnp.int32)
    dest = jnp.sum(jnp.where(eid[..., None] == experts, offset, 0), axis=-1) + rank
    block_ids = jnp.arange(n_blocks, dtype=jnp.int32)
    block_expert = jnp.minimum(jnp.sum((block_end[None, :] <= block_ids[:, None]).astype(jnp.int32), axis=1),
                               N_EXPERTS - 1)
    n_used = block_end[-1:]
    prev_expert = jnp.concatenate([jnp.full((1,), -1, jnp.int32), block_expert[:-1]])
    first = ((block_ids < n_used[0]) & (block_expert != prev_expert)).astype(jnp.int32)
    slot = jnp.bitwise_and(jnp.cumsum(first) - 1, 1)
    owner_or_none = jnp.where(blocks > 0, experts, N_EXPERTS)
    next_owner = jnp.concatenate([lax.cummin(owner_or_none[::-1])[::-1][1:], jnp.full((1,), N_EXPERTS, jnp.int32)])
    nxt = jnp.sum(jnp.where(block_expert[:, None] == experts, next_owner, 0), axis=1)
    return dest, (block_expert, first, slot.astype(jnp.int32), nxt.astype(jnp.int32), n_used.astype(jnp.int32)), n_blocks


def _sc_mesh():
    return plsc.VectorSubcoreMesh(core_axis_name="core", subcore_axis_name="subcore")


def _sc_dispatch(groups, dest, n_rows):
    w = groups[0].shape[1]
    dtype = groups[0].dtype

    @functools.partial(pl.kernel, out_type=jax.ShapeDtypeStruct((n_rows, w), dtype), mesh=_sc_mesh(),
                       scratch_types=[pltpu.VMEM((SC_ROWS, w), dtype), pltpu.VMEM((TOP_K, SC_ROWS), jnp.int32)])
    def kern(*refs):
        x_refs, d_hbm, o_hbm, xv, dv = refs[:len(groups)], *refs[len(groups):]
        sid = lax.axis_index("core") * (SC_SUBCORES // 2) + lax.axis_index("subcore")
        start = 0
        for x_hbm, x in zip(x_refs, groups):
            def chunk(c, x_hbm=x_hbm, start=start):
                r0 = pl.multiple_of(c * SC_ROWS, SC_ROWS)
                pltpu.sync_copy(x_hbm.at[pl.ds(r0, SC_ROWS)], xv)
                pltpu.sync_copy(d_hbm.at[:, pl.ds(start + r0, SC_ROWS)], dv)
                for k in range(TOP_K):
                    pltpu.sync_copy(xv, o_hbm.at[dv.at[k]])

            pl.loop(sid, x.shape[0] // SC_ROWS, step=SC_SUBCORES)(chunk)
            start += x.shape[0]

    return kern(*groups, dest)


def _sc_gather(y, idx):
    w = y.shape[1]
    n_chunks = idx.shape[0]

    @functools.partial(pl.kernel, out_type=jax.ShapeDtypeStruct((n_chunks * SC_ROWS, w), y.dtype), mesh=_sc_mesh(),
                       scratch_types=[pltpu.VMEM((SC_ROWS, w), y.dtype), pltpu.VMEM((1, SC_ROWS), jnp.int32)])
    def kern(y_hbm, i_hbm, o_hbm, ov, iv):
        sid = lax.axis_index("core") * (SC_SUBCORES // 2) + lax.axis_index("subcore")

        @pl.loop(sid, n_chunks, step=SC_SUBCORES)
        def _(c):
            pltpu.sync_copy(i_hbm.at[pl.ds(c, 1)], iv)
            pltpu.sync_copy(y_hbm.at[iv.at[0]], ov)
            pltpu.sync_copy(ov, o_hbm.at[pl.ds(pl.multiple_of(c * SC_ROWS, SC_ROWS), SC_ROWS)])

    return kern(y, idx)


def _moe_rows_kernel(be_ref, first_ref, slot_ref, nxt_ref, nu_ref, x_ref, wgu_hbm, wdn_hbm, y_ref,
                     wgu_f, wdn_f, wgu_b, wdn_b, sem, *, layer):
    b = pl.program_id(0)

    def weight_copies(expert, slot):
        return (pltpu.make_async_copy(wgu_hbm.at[layer, expert], wgu_f.at[slot], sem.at[slot, 0]),
                pltpu.make_async_copy(wdn_hbm.at[layer, expert], wdn_f.at[slot], sem.at[slot, 1]))

    @pl.when(b == 0)
    def _():
        for copy in weight_copies(be_ref[0], 0):
            copy.start(priority=WEIGHT_DMA_THREAD)

    @pl.when(first_ref[b] == 1)
    def _():
        slot = slot_ref[b]
        for copy in weight_copies(be_ref[b], slot):
            copy.wait()

        @pl.when(nxt_ref[b] < N_EXPERTS)
        def _():
            for copy in weight_copies(nxt_ref[b], 1 - slot):
                copy.start(priority=WEIGHT_DMA_THREAD)

        wgu_b[...] = wgu_f[slot].astype(BF16)
        wdn_b[...] = wdn_f[slot].astype(BF16)

    @pl.when(b < nu_ref[0])
    def _():
        half = D_MODEL // 2
        lo, hi = _unpack_halves(x_ref[...])
        gu = _mm(lo, wgu_b[0:half, :]) + _mm(hi, wgu_b[half:, :])
        y_ref[...] = _pack_halves(_mm(_swiglu_act(gu), wdn_b[...]))


def _moe_rows(xs, schedule, lw, layer, n_blocks, rows):
    half = D_MODEL // 2
    last = lambda b, be, first, slot, nxt, nu: jnp.minimum(b, nu[0] - 1)
    row_spec = pl.BlockSpec((rows, half), lambda *a: (last(*a), 0))
    return pl.pallas_call(
        functools.partial(_moe_rows_kernel, layer=layer),
        grid_spec=pltpu.PrefetchScalarGridSpec(
            num_scalar_prefetch=5,
            grid=(n_blocks,),
            in_specs=[row_spec, pl.BlockSpec(memory_space=pl.ANY), pl.BlockSpec(memory_space=pl.ANY)],
            out_specs=row_spec,
            scratch_shapes=[
                pltpu.VMEM((2, D_MODEL, 2 * D_EXPERT), F32),
                pltpu.VMEM((2, D_EXPERT, D_MODEL), F32),
                pltpu.VMEM((D_MODEL, 2 * D_EXPERT), BF16),
                pltpu.VMEM((D_EXPERT, D_MODEL), BF16),
                pltpu.SemaphoreType.DMA((2, 2)),
            ],
        ),
        out_shape=jax.ShapeDtypeStruct(xs.shape, jnp.int32),
        compiler_params=pltpu.CompilerParams(dimension_semantics=("arbitrary",),
                                             vmem_limit_bytes=VMEM_LIMIT),
        name="moe_rows",
    )(*schedule, xs, lw["moe_w_gu"], lw["moe_w_down"])


def _moe_combine_kernel(yg_ref, ew_ref, x1_ref, mod_ref, nf_ref, o_ref, *, final):
    ew = ew_ref[...]
    acc_lo = acc_hi = None
    for k in range(TOP_K):
        lo, hi = _unpack_halves(yg_ref[k])
        wk = ew[:, k:k + 1]
        acc_lo = wk * lo if acc_lo is None else acc_lo + wk * lo
        acc_hi = wk * hi if acc_hi is None else acc_hi + wk * hi
    g2 = mod_ref[0][:, 5 * D_MODEL:]
    x2 = x1_ref[...] + g2 * jnp.concatenate([acc_lo, acc_hi], axis=1)
    if final:
        x2 = _rms(x2, nf_ref[...])
    o_ref[...] = x2


def _moe_combine(yg, row0, ew, x1, mods, layer, norm_f, tm, seq_len, mod_base, final):
    n = x1.shape[0]
    half = D_MODEL // 2
    tiles_per_seq = seq_len // tm if mod_base else 1
    tile0 = row0 // tm

    def mod_idx(i):
        return (layer, mod_base + i // tiles_per_seq if mod_base else 0, 0, 0)

    row = lambda w: pl.BlockSpec((tm, w), lambda i: (i, 0))
    return pl.pallas_call(
        functools.partial(_moe_combine_kernel, final=final),
        grid=(n // tm,),
        in_specs=[pl.BlockSpec((TOP_K, tm, half), lambda i: (0, tile0 + i, 0)), row(LANES), row(D_MODEL),
                  pl.BlockSpec((None, 1, 1, 6 * D_MODEL), mod_idx), _const_spec(norm_f)],
        out_specs=row(D_MODEL),
        out_shape=jax.ShapeDtypeStruct((n, D_MODEL), F32),
        compiler_params=pltpu.CompilerParams(dimension_semantics=("arbitrary",),
                                             vmem_limit_bytes=VMEM_LIMIT),
        name="moe_combine_final" if final else "moe_combine",
    )(yg, ew, x1, mods, norm_f)


def _moe_experts(groups, eid, rank, counts, lw, layer):
    group_sizes = [g.shape[0] for g in groups]
    n = sum(group_sizes)
    dest, schedule, n_blocks = _dispatch_plan(eid, rank, counts, n, MOE_ROWS)
    xs = _sc_dispatch(groups, dest, n_blocks * MOE_ROWS)
    y = _moe_rows(xs, schedule, lw, layer, n_blocks, MOE_ROWS)
    outs, start = [], 0
    for size in group_sizes:
        idx = dest[:, start:start + size].reshape(size * TOP_K // SC_ROWS, SC_ROWS)
        outs.append(_sc_gather(y, idx).reshape(TOP_K, size, D_MODEL // 2))
        start += size
    return outs


def _constants():
    idx = np.arange(MIX_W)
    same_head = (idx[:, None] // HEAD_DIM) == (idx[None, :] // HEAD_DIM)
    maskbd = jnp.asarray(same_head, BF16)
    eexp = np.zeros((LANES, 4 * MIX_W), np.float32)
    for blk in range(4):
        kind, d = divmod(blk, 2)
        for h in range(HEADS):
            src = kind * 2 * HEADS + d * HEADS + h
            eexp[src, blk * MIX_W + h * HEAD_DIM: blk * MIX_W + (h + 1) * HEAD_DIM] = 1.0
    tri = np.triu(np.ones((POST_TM, POST_TM), np.float32), 1)
    return {"maskbd": maskbd, "ones_bd": maskbd, "eexp": jnp.asarray(eexp, BF16), "tri_tokens": jnp.asarray(tri, BF16)}


def _rope_tables(n):
    rows = n // GRID_W
    row = jnp.repeat(jnp.arange(rows, dtype=F32), GRID_W)
    col = jnp.tile(jnp.arange(GRID_W, dtype=F32), rows)
    axis_dim = MLA_ROPE // 2
    inv = jnp.power(ROPE_BASE, -jnp.arange(0, axis_dim, 2, dtype=F32) / axis_dim)
    ang_r = row[:, None] * inv
    ang_c = col[:, None] * inv
    cr, sr, cc, sc = jnp.cos(ang_r), jnp.sin(ang_r), jnp.cos(ang_c), jnp.sin(ang_c)
    zeros = jnp.zeros((n, LANES - MLA_ROPE), F32)
    cos_t = jnp.concatenate([cr, cr, cc, cc, zeros], axis=1)
    sin_t = jnp.concatenate([-sr, sr, -sc, sc, zeros], axis=1)
    return cos_t, sin_t


def _stacked_weights(p):
    w_uq = p["mla_w_uq"].reshape(DEPTH, Q_LORA, MLA_HEADS, MLA_NOPE + MLA_ROPE)
    zq = jnp.zeros((DEPTH, Q_LORA, MLA_HEADS, QH_W - MLA_NOPE - MLA_ROPE), F32)
    w_uq_a = jnp.concatenate([w_uq, zq], axis=-1).reshape(DEPTH, Q_LORA, MLA_HEADS * QH_W).astype(BF16)
    w_uq_sw = jnp.concatenate([jnp.zeros((DEPTH, Q_LORA, MLA_HEADS, MLA_NOPE), F32),
                               w_uq[..., MLA_NOPE + _ROPE_SWAP], zq], axis=-1)
    w_uq_sw = w_uq_sw.reshape(DEPTH, Q_LORA, MLA_HEADS * QH_W).astype(BF16)

    def per_direction(w):
        half = jnp.zeros((DEPTH, 64, MIX_W), F32)
        return jnp.stack([jnp.concatenate([w[:, 0], half], axis=1),
                          jnp.concatenate([half, w[:, 1]], axis=1)], axis=1).astype(BF16)

    row = lambda v: v.reshape(DEPTH, 1, -1)
    pad_row = lambda v: jnp.pad(row(v), ((0, 0), (0, 0), (0, LANES - 2 * HEADS)))
    return {
        "norm1": row(p["norm1"]),
        "w_in": p["w_in"],
        "q_norm": row(p["mla_q_norm"]),
        "w_uq": w_uq_a, "w_uq_sw": w_uq_sw,
        "kv_norm": row(p["mla_kv_norm"]),
        "w_ukv": p["mla_w_ukv"].astype(BF16),
        "gdn_conv": p["gdn_conv"],
        "gdn_alog": pad_row(p["gdn_a_log"]),
        "gdn_dtb": pad_row(p["gdn_dt_bias"]),
        "gdn_norm": jnp.tile(row(p["gdn_norm"]), (1, 1, HEADS)),
        "rwkv_mu_prev": row(p["rwkv_mu_prev"]),
        "rwkv_mu_next": row(p["rwkv_mu_next"]),
        "rwkv_w0": p["rwkv_w0"],
        "rwkv_w2": per_direction(p["rwkv_w2"]),
        "rwkv_a0": p["rwkv_a0"],
        "rwkv_a2": per_direction(p["rwkv_a2"]),
        "rwkv_g2": p["rwkv_g2"].astype(BF16),
        "rwkv_k_k": row(p["rwkv_k_k"]),
        "rwkv_k_a": row(p["rwkv_k_a"]),
        "rwkv_r_k": row(p["rwkv_r_k"]),
        "rwkv_gn_w": row(p["rwkv_gn_w"]),
        "rwkv_gn_b": row(p["rwkv_gn_b"]),
        "w_out": p["w_out"].astype(BF16),
        "norm2": row(p["norm2"]),
        "router_t": jnp.swapaxes(p["moe_router"], 1, 2),
        "router_b": p["moe_bias"].reshape(DEPTH, N_EXPERTS, 1),
        "moe_w_gu": p["moe_w_gu"],
        "moe_w_down": p["moe_w_down"],
        "shared_w_gu": p["shared_w_gu"].astype(BF16),
        "shared_w_down": p["shared_w_down"].astype(BF16),
    }


def _embed_block_diag(s):
    b = s.shape[0]
    eye = jnp.eye(HEADS, dtype=s.dtype)
    out = jnp.einsum("bdhkv,hg->bdhkgv", s, eye)
    return out.reshape(b, 2, MIX_W, MIX_W)


def _layer_front(x2d, mods, lw, l, consts, seq_len, mod_base, cache, tm, tq, counts_in, prev=None):
    zm, zg, zab, zr = _inproj(x2d, mods, lw, l, tm, seq_len, mod_base)
    if cache is None:
        o_mla, ckv, kpe = _mla_ctx(zm, lw, l, seq_len, prev[:2])
        o_gdn, s_gdn = _gdn(zg, zab, lw, l, consts, seq_len, prev=prev[2])
        o_rwkv, s_rwkv = _rwkv(zr, lw, l, consts, seq_len, prev=prev[3])
        new = (ckv, kpe, s_gdn, s_rwkv)
    else:
        cckv, ckpe, rc, rs, sg, sr = cache
        o_mla = _mla_lat(zm, lw, l, seq_len, tq, (cckv, ckpe, rc, rs))
        o_gdn, _ = _gdn(zg, zab, lw, l, consts, seq_len, sg)
        o_rwkv, _ = _rwkv(zr, lw, l, consts, seq_len, sr)
        new = None
    routed = _post(x2d, o_mla, o_gdn, o_rwkv, mods, lw, l, consts, tm, seq_len, mod_base, counts_in)
    return routed, new


def kernel(x_prompt, x_sample, cache_mla_ckv, cache_mla_kpe, state_gdn, state_rwkv, c, c_ctx, ada_w, ada_b, norm1, w_in, mla_q_norm, mla_w_uq, mla_kv_norm, mla_w_ukv, gdn_conv, gdn_a_log, gdn_dt_bias, gdn_norm, rwkv_mu_prev, rwkv_mu_next, rwkv_w0, rwkv_w2, rwkv_a0, rwkv_a2, rwkv_g2, rwkv_k_k, rwkv_k_a, rwkv_r_k, rwkv_gn_w, rwkv_gn_b, w_out, norm2, moe_router, moe_bias, moe_w_gu, moe_w_down, shared_w_gu, shared_w_down, norm_f):
    p = dict(norm1=norm1, w_in=w_in, mla_q_norm=mla_q_norm, mla_w_uq=mla_w_uq, mla_kv_norm=mla_kv_norm,
             mla_w_ukv=mla_w_ukv, gdn_conv=gdn_conv, gdn_a_log=gdn_a_log, gdn_dt_bias=gdn_dt_bias,
             gdn_norm=gdn_norm, rwkv_mu_prev=rwkv_mu_prev, rwkv_mu_next=rwkv_mu_next, rwkv_w0=rwkv_w0,
             rwkv_w2=rwkv_w2, rwkv_a0=rwkv_a0, rwkv_a2=rwkv_a2, rwkv_g2=rwkv_g2, rwkv_k_k=rwkv_k_k,
             rwkv_k_a=rwkv_k_a, rwkv_r_k=rwkv_r_k, rwkv_gn_w=rwkv_gn_w, rwkv_gn_b=rwkv_gn_b, w_out=w_out,
             norm2=norm2, moe_router=moe_router, moe_bias=moe_bias, moe_w_gu=moe_w_gu, moe_w_down=moe_w_down,
             shared_w_gu=shared_w_gu, shared_w_down=shared_w_down)
    weights = _stacked_weights(p)
    consts = _constants()
    nf = norm_f.reshape(1, D_MODEL)
    b_ctx, t_ctx, _ = x_prompt.shape
    b_lat, t_lat, _ = x_sample.shape

    cvec8 = jnp.concatenate([c_ctx[None, :], c, jnp.zeros((8 - 1 - b_lat, D_MODEL), F32)], axis=0)
    mods = _adaln(cvec8, ada_w, ada_b)
    mods = mods.reshape(DEPTH, 8, 1, 6 * D_MODEL)

    rc, rs = _rope_tables(t_lat)
    ckpe = jnp.pad(cache_mla_kpe, ((0, 0), (0, 0), (0, 0), (0, LANES - MLA_ROPE)))
    cache = (cache_mla_ckv, ckpe, rc, rs, _embed_block_diag_layers(state_gdn),
             _embed_block_diag_layers(jnp.swapaxes(state_rwkv, -1, -2)))
    xp = x_prompt.reshape(b_ctx * t_ctx, D_MODEL)
    xs = x_sample.reshape(b_lat * t_lat, D_MODEL)
    n_ctx = xp.shape[0]
    tm = POST_TM
    state_shape = (b_ctx, DEPTH, 2, HEADS, HEAD_DIM, HEAD_DIM)
    ctx_outs = (jnp.zeros((b_ctx, DEPTH, t_ctx, KV_LORA), F32), jnp.zeros((b_ctx, DEPTH, t_ctx, MLA_ROPE), F32),
                jnp.zeros(state_shape, F32), jnp.zeros(state_shape, F32))
    for l in range(DEPTH):
        final = l == DEPTH - 1
        no_pairs = jnp.zeros((N_EXPERTS, LANES), F32)
        (x1c, hc, eidc, rankc, ewc, cnt_c), ctx_outs = _layer_front(xp, mods, weights, l, consts, t_ctx, 0, None,
                                                                    tm, t_ctx, no_pairs, ctx_outs)
        (x1s, hs, eids, ranks, ews, cnt), _ = _layer_front(xs, mods, weights, l, consts, t_lat, 1, cache,
                                                           tm, 256, cnt_c)
        yg_c, yg_s = _moe_experts([hc, hs], jnp.concatenate([eidc, eids], axis=1),
                                  jnp.concatenate([rankc, ranks], axis=1), cnt, weights, l)
        xp = _moe_combine(yg_c, 0, ewc, x1c, mods, l, nf, tm, t_ctx, 0, final)
        xs = _moe_combine(yg_s, 0, ews, x1s, mods, l, nf, tm, t_lat, 1, final)

    y_prompt = xp.reshape(b_ctx, t_ctx, D_MODEL)
    y_sample = xs.reshape(b_lat, t_lat, D_MODEL)
    new_ckv, new_kpe, new_gdn, new_rwkv = ctx_outs
    return (y_prompt, y_sample, new_ckv, new_kpe, new_gdn, new_rwkv)


def _embed_block_diag_layers(s):
    b = s.shape[0]
    return _embed_block_diag(s.reshape(b * DEPTH, 2, HEADS, HEAD_DIM, HEAD_DIM)).reshape(
        b, DEPTH, 2, MIX_W, MIX_W)
```
